```python
import jax
import jax.numpy as jnp
from jax import lax
import numpy as np

D_MODEL = 2048
BATCH = 4
SEQ = 2048
DEPTH = 1

GRID_W = 64
CTX_LEN = 256

RET_HEADS = 8
RET_DK = 128
RET_DV = 128
RET_CHUNK = 128
RET_W = RET_HEADS * RET_DV

MLA_HEADS = 8
MLA_Q_LORA = 512
MLA_KV_LORA = 256
MLA_NOPE = 128
MLA_ROPE = 64
MLA_DV = 128
MLA_W = MLA_HEADS * MLA_DV
Q_BLOCK = 128

MIX_W = RET_W + MLA_W
IN_SIZES = (RET_HEADS * RET_DK, RET_HEADS * RET_DK, RET_W, RET_W, MLA_Q_LORA, MLA_KV_LORA, MLA_ROPE)
IN_W = sum(IN_SIZES)

N_GROUPS = 4
EXPERTS_PER_GROUP = 8
TOP_K = 2
D_EXPERT = 512

ROPE_BASE = 10000.0
EPS = 1e-6
DEEPNORM_ALPHA = (2.0 * DEPTH) ** 0.25
DEEPNORM_BETA = (8.0 * DEPTH) ** -0.25
F32 = jnp.float32

kernel_name = 'hybrid_retention_mla_hmoe_diffusion_block'


def plain_norm(x):
    xf = x.astype(F32)
    mu = jnp.mean(xf, -1, keepdims=True)
    var = jnp.mean(jnp.square(xf - mu), -1, keepdims=True)
    return (xf - mu) * lax.rsqrt(var + EPS)


def layer_norm(x, w, b):
    return (plain_norm(x) * w.astype(F32) + b.astype(F32)).astype(x.dtype)


def rms_norm(x, w):
    xf = x.astype(F32)
    y = xf * lax.rsqrt(jnp.mean(jnp.square(xf), -1, keepdims=True) + EPS)
    return (y * w.astype(F32)).astype(x.dtype)


def modulate(x, shift, scale):
    y = plain_norm(x) * (1.0 + scale[:, None].astype(F32)) + shift[:, None].astype(F32)
    return y.astype(x.dtype)


def to_heads(t, n_heads):
    b, l, w = t.shape
    return t.reshape(b, l, n_heads, w // n_heads).transpose(0, 2, 1, 3)


def from_heads(t):
    b, h, l, d = t.shape
    return t.transpose(0, 2, 1, 3).reshape(b, l, h * d)


def split_projection(p):
    offs = np.cumsum(np.array(IN_SIZES))[:-1].tolist()
    return jnp.split(p, offs, axis=-1)


def axial_rope(x, rows, cols):
    half = x.shape[-1] // 2
    inv_freq = ROPE_BASE ** (-jnp.arange(0, half, 2, dtype=F32) / half)

    def rotate(xa, pos):
        ang = pos.astype(F32)[:, None] * inv_freq[None, :]
        cos, sin = jnp.cos(ang), jnp.sin(ang)
        x1, x2 = jnp.split(xa.astype(F32), 2, axis=-1)
        return jnp.concatenate([x1 * cos - x2 * sin, x1 * sin + x2 * cos], -1)

    return jnp.concatenate([rotate(x[..., :half], rows), rotate(x[..., half:], cols)], -1).astype(x.dtype)


def retention_chunkwise(q, k, v, log_gamma, state0):
    b, h, l, dk = q.shape
    dv = v.shape[-1]
    n = l // RET_CHUNK
    qc = q.reshape(b, h, n, RET_CHUNK, dk).astype(F32)
    kc = k.reshape(b, h, n, RET_CHUNK, dk).astype(F32)
    vc = v.reshape(b, h, n, RET_CHUNK, dv).astype(F32)
    idx = jnp.arange(RET_CHUNK, dtype=F32)
    lg = log_gamma[:, None]
    diff = idx[:, None] - idx[None, :]
    decay_in = jnp.where(diff >= 0, jnp.exp(log_gamma[:, None, None] * jnp.maximum(diff, 0.0)), 0.0)
    zeta = jnp.exp(lg * (RET_CHUNK - 1.0 - idx))
    xi = jnp.exp(lg * (idx + 1.0))
    chunk_decay = jnp.exp(log_gamma * RET_CHUNK)[None, :, None, None]
    scores = jnp.einsum('bhncd,bhnsd->bhncs', qc, kc) * decay_in[None, :, None]
    o_inner = jnp.einsum('bhncs,bhnse->bhnce', scores, vc)
    chunk_kv = jnp.einsum('bhnsd,bhnse->nbhde', kc * zeta[None, :, None, :, None], vc)

    def step(s, u):
        return chunk_decay * s + u, s

    _, s_prev = lax.scan(step, state0.astype(F32), chunk_kv)
    o_cross = jnp.einsum('bhncd,nbhde->bhnce', qc, s_prev) * xi[None, :, None, :, None]
    return (o_inner + o_cross).reshape(b, h, l, dv)


def retention_final_state(k, v, log_gamma):
    l = k.shape[2]
    pw = jnp.exp(log_gamma[:, None] * (l - 1.0 - jnp.arange(l, dtype=F32)))
    return jnp.einsum('bhld,bhle->bhde', k.astype(F32) * pw[None, :, :, None], v.astype(F32))


def context_retention_states(k, v, log_gamma):
    return (retention_final_state(k, v, log_gamma[0]),
            retention_final_state(jnp.flip(k, 2), jnp.flip(v, 2), log_gamma[1]))


def bidirectional_retention(q, k, v, log_gamma, s_fwd, s_bwd):
    o_fwd = retention_chunkwise(q, k, v, log_gamma[0], s_fwd)
    o_bwd = retention_chunkwise(jnp.flip(q, 2), jnp.flip(k, 2), jnp.flip(v, 2), log_gamma[1], s_bwd)
    return o_fwd + jnp.flip(o_bwd, 2)


def retention_output(o, g, gn_w):
    y = from_heads(plain_norm(o)) * gn_w.astype(F32)
    return (jax.nn.silu(g.astype(F32)) * y).astype(g.dtype)


def mla_queries(c_q, q_norm_w, w_uq):
    return to_heads(rms_norm(c_q, q_norm_w) @ w_uq, MLA_HEADS)


def mla_keys_values(c_kv, k_pe, kv_norm_w, w_ukv):
    kv = to_heads(rms_norm(c_kv, kv_norm_w) @ w_ukv, MLA_HEADS)
    k_nope, v = kv[..., :MLA_NOPE], kv[..., MLA_NOPE:]
    b, h, l, _ = k_nope.shape
    k_pe_h = jnp.broadcast_to(k_pe[:, None], (b, h, l, MLA_ROPE)).astype(k_nope.dtype)
    return jnp.concatenate([k_nope, k_pe_h], -1), v


def block_attention(q, k, v):
    b, h, lq, dq = q.shape
    nb = lq // Q_BLOCK
    qb = jnp.moveaxis(q.reshape(b, h, nb, Q_BLOCK, dq), 2, 0)
    scale = dq ** -0.5

    def one_block(qblk):
        s = jnp.einsum('bhqd,bhkd->bhqk', qblk, k).astype(F32) * scale
        p = jax.nn.softmax(s, axis=-1)
        return jnp.einsum('bhqk,bhkd->bhqd', p.astype(v.dtype), v)

    o = lax.map(one_block, qb)
    return jnp.moveaxis(o, 0, 2).reshape(b, h, lq, v.shape[-1])


def hier_moe(x, w_group, b_group, w_route, b_route, w_gate, w_up, w_down):
    t = x.reshape(-1, x.shape[-1])
    g_logits = (t @ w_group).astype(F32) + b_group.astype(F32)
    g_onehot = jax.nn.one_hot(jnp.argmax(g_logits, -1), N_GROUPS, dtype=F32)
    g_prob = jnp.sum(jax.nn.softmax(g_logits, -1) * g_onehot, -1, keepdims=True)
    e_logits = jnp.einsum('td,gde->tge', t, w_route).astype(F32) + b_route.astype(F32)
    e_logits = jnp.einsum('tge,tg->te', e_logits, g_onehot)
    top_v, top_i = lax.top_k(e_logits, TOP_K)
    top_w = jax.nn.softmax(top_v, -1) * g_prob
    e_w = jnp.sum(jax.nn.one_hot(top_i, EXPERTS_PER_GROUP, dtype=F32) * top_w[..., None], 1)
    gate = g_onehot[:, :, None] * e_w[:, None, :]
    y = jnp.zeros(t.shape, F32)
    for gi in range(N_GROUPS):
        hid = jax.nn.silu(jnp.einsum('td,edf->tef', t, w_gate[gi]).astype(F32)) * jnp.einsum('td,edf->tef', t, w_up[gi]).astype(F32)
        y = y + jnp.einsum('tef,efd->td', hid * gate[:, gi, :, None], w_down[gi].astype(F32))
    return y.reshape(x.shape).astype(x.dtype)


def layer_forward(x, ctx, c, c_ctx, rows, cols, w_ada, b_ada, w_in, ret_decay, ret_gn_w,
                  mla_q_norm, mla_kv_norm, w_uq, w_ukv, w_o, ln1_w, ln1_b,
                  router_group_w, router_group_b, router_expert_w, router_expert_b,
                  expert_w_gate, expert_w_up, expert_w_down, ln2_w, ln2_b, update_ctx):
    def moe(t):
        return hier_moe(t, router_group_w, router_group_b, router_expert_w, router_expert_b,
                        expert_w_gate, expert_w_up, expert_w_down)

    mod_x = jnp.split(jax.nn.silu(c) @ w_ada + b_ada, 6, axis=-1)
    mod_c = jnp.split(jax.nn.silu(c_ctx)[None] @ w_ada + b_ada, 6, axis=-1)
    log_gamma = jax.nn.log_sigmoid(ret_decay.astype(F32))
    k_scale = RET_DK ** -0.5

    qr, kr, vr, gr, cq, ckv, kpe = split_projection(modulate(x, mod_x[0], mod_x[1]) @ w_in)
    qr_c, kr_c, vr_c, gr_c, cq_c, ckv_c, kpe_c = split_projection(modulate(ctx, mod_c[0], mod_c[1]) @ w_in)

    kh_c = to_heads(kr_c, RET_HEADS) * k_scale
    vh_c = to_heads(vr_c, RET_HEADS)
    s_fwd, s_bwd = context_retention_states(kh_c, vh_c, log_gamma)
    k_ctx, v_ctx = mla_keys_values(ckv_c, kpe_c, mla_kv_norm, w_ukv)

    qh = axial_rope(to_heads(qr, RET_HEADS), rows, cols)
    kh = axial_rope(to_heads(kr, RET_HEADS), rows, cols) * k_scale
    o_ret = bidirectional_retention(qh, kh, to_heads(vr, RET_HEADS), log_gamma, s_fwd, s_bwd)
    ret = retention_output(o_ret, gr, ret_gn_w)

    q = mla_queries(cq, mla_q_norm, w_uq)
    q = jnp.concatenate([q[..., :MLA_NOPE], axial_rope(q[..., MLA_NOPE:], rows, cols)], -1)
    k_lat, v_lat = mla_keys_values(ckv, axial_rope(kpe, rows, cols), mla_kv_norm, w_ukv)
    att = block_attention(q, jnp.concatenate([k_ctx, k_lat], 2), jnp.concatenate([v_ctx, v_lat], 2))

    mix = jnp.concatenate([ret, from_heads(att)], -1) @ w_o
    h = layer_norm(DEEPNORM_ALPHA * x + mod_x[2][:, None] * mix, ln1_w, ln1_b)
    y = moe(modulate(h, mod_x[3], mod_x[4]))
    x_out = layer_norm(DEEPNORM_ALPHA * h + mod_x[5][:, None] * y, ln2_w, ln2_b)

    if not update_ctx:
        return x_out, ctx

    z = jnp.zeros((ctx.shape[0], RET_HEADS, RET_DK, RET_DV), F32)
    o_ret_c = bidirectional_retention(to_heads(qr_c, RET_HEADS), kh_c, vh_c, log_gamma, z, z)
    ret_c = retention_output(o_ret_c, gr_c, ret_gn_w)
    att_c = block_attention(mla_queries(cq_c, mla_q_norm, w_uq), k_ctx, v_ctx)
    mix_c = jnp.concatenate([ret_c, from_heads(att_c)], -1) @ w_o
    hc = layer_norm(DEEPNORM_ALPHA * ctx + mod_c[2][:, None] * mix_c, ln1_w, ln1_b)
    yc = moe(modulate(hc, mod_c[3], mod_c[4]))
    ctx_out = layer_norm(DEEPNORM_ALPHA * hc + mod_c[5][:, None] * yc, ln2_w, ln2_b)
    return x_out, ctx_out


def setup_inputs(seed: int = 0) -> dict:
    key = jax.random.key(seed)
    ks = jax.random.split(key, 25)

    def nrm(k, shape, s):
        return jax.random.normal(k, shape, F32) * s

    L = DEPTH
    E = EXPERTS_PER_GROUP
    base_decay = jnp.log(2.0 ** (5.0 + jnp.arange(RET_HEADS, dtype=F32)) - 1.0)
    return {
        'x': nrm(ks[0], (BATCH, SEQ, D_MODEL), 1.0),
        'c': nrm(ks[1], (BATCH, D_MODEL), 1.0),
        'ctx': nrm(ks[2], (BATCH, CTX_LEN, D_MODEL), 1.0),
        'c_ctx': nrm(ks[3], (D_MODEL,), 1.0),
        'w_ada': nrm(ks[4], (L, D_MODEL, 6 * D_MODEL), 0.5 * D_MODEL ** -0.5),
        'b_ada': nrm(ks[5], (L, 6 * D_MODEL), 0.02),
        'w_in': nrm(ks[6], (L, D_MODEL, IN_W), D_MODEL ** -0.5),
        'ret_decay': base_decay + nrm(ks[7], (L, 2, RET_HEADS), 0.1),
        'ret_gn_w': 1.0 + nrm(ks[8], (L, RET_W), 0.02),
        'mla_q_norm': 1.0 + nrm(ks[9], (L, MLA_Q_LORA), 0.02),
        'mla_kv_norm': 1.0 + nrm(ks[10], (L, MLA_KV_LORA), 0.02),
        'w_uq': nrm(ks[11], (L, MLA_Q_LORA, MLA_HEADS * (MLA_NOPE + MLA_ROPE)), MLA_Q_LORA ** -0.5),
        'w_ukv': nrm(ks[12], (L, MLA_KV_LORA, MLA_HEADS * (MLA_NOPE + MLA_DV)), MLA_KV_LORA ** -0.5),
        'w_o': nrm(ks[13], (L, MIX_W, D_MODEL), DEEPNORM_BETA * MIX_W ** -0.5),
        'ln1_w': 1.0 + nrm(ks[14], (L, D_MODEL), 0.02),
        'ln1_b': nrm(ks[15], (L, D_MODEL), 0.02),
        'router_group_w': nrm(ks[16], (L, D_MODEL, N_GROUPS), D_MODEL ** -0.5),
        'router_group_b': nrm(ks[17], (L, N_GROUPS), 0.01),
        'router_expert_w': nrm(ks[18], (L, N_GROUPS, D_MODEL, E), D_MODEL ** -0.5),
        'router_expert_b': nrm(ks[19], (L, N_GROUPS, E), 0.01),
        'expert_w_gate': nrm(ks[20], (L, N_GROUPS, E, D_MODEL, D_EXPERT), D_MODEL ** -0.5),
        'expert_w_up': nrm(ks[21], (L, N_GROUPS, E, D_MODEL, D_EXPERT), D_MODEL ** -0.5),
        'expert_w_down': nrm(ks[22], (L, N_GROUPS, E, D_EXPERT, D_MODEL), DEEPNORM_BETA * D_EXPERT ** -0.5),
        'ln2_w': 1.0 + nrm(ks[23], (L, D_MODEL), 0.02),
        'ln2_b': nrm(ks[24], (L, D_MODEL), 0.02),
    }


def reference(x, c, ctx, c_ctx, w_ada, b_ada, w_in, ret_decay, ret_gn_w, mla_q_norm, mla_kv_norm,
              w_uq, w_ukv, w_o, ln1_w, ln1_b, router_group_w, router_group_b, router_expert_w,
              router_expert_b, expert_w_gate, expert_w_up, expert_w_down, ln2_w, ln2_b):
    n_rows = x.shape[1] // GRID_W
    rows = jnp.repeat(jnp.arange(n_rows, dtype=jnp.int32), GRID_W)
    cols = jnp.tile(jnp.arange(GRID_W, dtype=jnp.int32), n_rows)
    for l in range(DEPTH):
        x, ctx = layer_forward(
            x, ctx, c, c_ctx, rows, cols, w_ada[l], b_ada[l], w_in[l], ret_decay[l], ret_gn_w[l],
            mla_q_norm[l], mla_kv_norm[l], w_uq[l], w_ukv[l], w_o[l], ln1_w[l], ln1_b[l],
            router_group_w[l], router_group_b[l], router_expert_w[l], router_expert_b[l],
            expert_w_gate[l], expert_w_up[l], expert_w_down[l], ln2_w[l], ln2_b[l],
            l < DEPTH - 1)
    return x
```

```python
import functools

import numpy as np
import jax
import jax.numpy as jnp
from jax import lax
from jax.experimental import pallas as pl
from jax.experimental.pallas import tpu as pltpu

F32 = jnp.float32
BF16 = jnp.bfloat16
U32 = jnp.uint32
I32 = jnp.int32

D_MODEL = 2048
BATCH = 4
SEQ = 2048
GRID_W = 64
CTX_LEN = 256
N_TOK = BATCH * SEQ
N_CTX = BATCH * CTX_LEN

RET_HEADS = 8
RET_DK = 128
RET_DV = 128
RET_W = RET_HEADS * RET_DV
CHUNK = 128
N_CHUNKS = SEQ // CHUNK

MLA_HEADS = 8
MLA_Q_LORA = 512
MLA_KV_LORA = 256
MLA_NOPE = 128
MLA_ROPE = 64
MLA_DV = 128
MLA_W = MLA_HEADS * MLA_DV
MLA_DQ = MLA_NOPE + MLA_ROPE
QK_PAD = 256

IN_SIZES = (RET_HEADS * RET_DK, RET_HEADS * RET_DK, RET_W, RET_W, MLA_Q_LORA, MLA_KV_LORA, MLA_ROPE)
IN_W = sum(IN_SIZES)
OFF_Q, OFF_K, OFF_V, OFF_G, OFF_CQ, OFF_CKV, OFF_KPE = (int(v) for v in np.cumsum((0,) + IN_SIZES[:-1]))

N_GROUPS = 4
EXPERTS_PER_GROUP = 8
N_EXPERTS = N_GROUPS * EXPERTS_PER_GROUP
D_EXPERT = 512
N_ROUTE = N_GROUPS + N_EXPERTS

ROPE_BASE = 10000.0
EPS = 1e-6
DEPTH = 1
DEEPNORM_ALPHA = (2.0 * DEPTH) ** 0.25

LANES = 128
ROW_TILE = 256
N_PAIRS = 2 * N_TOK
MAX_TILES = N_PAIRS // ROW_TILE + N_EXPERTS
MAX_ROWS = MAX_TILES * ROW_TILE
PACK_W = D_MODEL // 2

VMEM_LIMIT = 56 * 1024 * 1024


def _params(sem, vmem=VMEM_LIMIT):
    return pltpu.CompilerParams(dimension_semantics=sem, vmem_limit_bytes=vmem)


def _silu(x):
    return x * (1.0 / (1.0 + jnp.exp(-x)))


def _plain_norm(x):
    mu = jnp.mean(x, -1, keepdims=True)
    xc = x - mu
    var = jnp.mean(xc * xc, -1, keepdims=True)
    return xc * lax.rsqrt(var + EPS)


def _pack_bf16_pairs(x):
    w = x.shape[1] // 2
    lo = lax.bitcast_convert_type(x[:, :w].astype(BF16).astype(F32), U32)
    hi = lax.bitcast_convert_type(x[:, w:].astype(BF16).astype(F32), U32)
    return (lo >> 16) | (hi & jnp.uint32(0xFFFF0000))


def _unpack_bf16_pairs(p):
    lo = lax.bitcast_convert_type(p << 16, F32)
    hi = lax.bitcast_convert_type(p & jnp.uint32(0xFFFF0000), F32)
    return lo, hi


ADA_TN = 1024


def _ada_kernel(cc_ref, w_ref, b_ref, o_ref):
    s = _silu(cc_ref[...])
    o_ref[...] = jnp.dot(s, w_ref[...], preferred_element_type=F32,
                         precision=lax.Precision.HIGHEST) + b_ref[...]


def _ada(cc, w_ada, b_ada):
    n = w_ada.shape[1]
    return pl.pallas_call(
        _ada_kernel,
        grid=(n // ADA_TN,),
        in_specs=[
            pl.BlockSpec((8, D_MODEL), lambda j: (0, 0)),
            pl.BlockSpec((D_MODEL, ADA_TN), lambda j: (0, j)),
            pl.BlockSpec((1, ADA_TN), lambda j: (0, j)),
        ],
        out_specs=pl.BlockSpec((8, ADA_TN), lambda j: (0, j)),
        out_shape=jax.ShapeDtypeStruct((8, n), F32),
        compiler_params=_params(("arbitrary",)),
        name="ada",
    )(cc, w_ada, b_ada)


INPROJ_TM = 512
INPROJ_TN = 512


def _inproj_kernel(x_ref, shift_ref, scale_ref, w_ref, o_ref, xn_ref):
    @pl.when(pl.program_id(1) == 0)
    def _():
        y = _plain_norm(x_ref[...]) * (1.0 + scale_ref[0]) + shift_ref[0]
        xn_ref[...] = y.astype(BF16)

    o_ref[...] = jnp.dot(xn_ref[...], w_ref[...], preferred_element_type=F32).astype(BF16)


def _inproj(x2d, mod_rows, w_in_bf, rows_per_sample, sample_row0):
    n = x2d.shape[0]
    tiles_per_sample = rows_per_sample // INPROJ_TM

    def mod_map(j):
        return lambda i, k: ((sample_row0 + i // tiles_per_sample) * 6 + j, 0, 0)

    return pl.pallas_call(
        _inproj_kernel,
        grid=(n // INPROJ_TM, pl.cdiv(IN_W, INPROJ_TN)),
        in_specs=[
            pl.BlockSpec((INPROJ_TM, D_MODEL), lambda i, k: (i, 0)),
            pl.BlockSpec((1, 1, D_MODEL), mod_map(0)),
            pl.BlockSpec((1, 1, D_MODEL), mod_map(1)),
            pl.BlockSpec((D_MODEL, INPROJ_TN), lambda i, k: (0, k)),
        ],
        out_specs=pl.BlockSpec((INPROJ_TM, INPROJ_TN), lambda i, k: (i, k)),
        out_shape=jax.ShapeDtypeStruct((n, IN_W), BF16),
        scratch_shapes=[pltpu.VMEM((INPROJ_TM, D_MODEL), BF16)],
        compiler_params=_params(("arbitrary", "arbitrary")),
        name="inproj",
    )(x2d, mod_rows, mod_rows, w_in_bf)


def _rope_tables(width):
    half = width // 2
    quarter = half // 2
    inv_freq = ROPE_BASE ** (-np.arange(0, half, 2, dtype=np.float64) / half)
    t = np.arange(SEQ)
    cos_parts, sin_parts = [], []
    for pos in (t // GRID_W, t % GRID_W):
        ang = pos[:, None].astype(np.float64) * inv_freq[None, :]
        c, s = np.cos(ang), np.sin(ang)
        cos_parts += [c, c]
        sin_parts += [-s, s]
    assert cos_parts[0].shape[1] == quarter
    return (np.concatenate(cos_parts, 1).astype(np.float32), np.concatenate(sin_parts, 1).astype(np.float32))


def _rope(x, cos, sin, quarter):
    lane = lax.broadcasted_iota(I32, x.shape, 1)
    first = (lane % (2 * quarter)) < quarter
    swapped = jnp.where(first, pltpu.roll(x, LANES - quarter, 1), pltpu.roll(x, quarter, 1))
    return x * cos + swapped * sin


def _dot_tn(a, b):
    return lax.dot_general(a, b, (((0,), (0,)), ((), ())), preferred_element_type=F32)


def _dot_nt(a, b):
    return lax.dot_general(a, b, (((1,), (1,)), ((), ())), preferred_element_type=F32)


def _retention_kernel(q_ref, k_ref, v_ref, g_ref, kc_ref, vc_ref, cos_ref, sin_ref, df_ref, db_ref, gn_ref,
                      o_ref, qs_ref, ks_ref, st_ref):
    k_scale = RET_DK ** -0.5
    lgf = jax.nn.log_sigmoid(df_ref[0])
    lgb = jax.nn.log_sigmoid(db_ref[0])

    cos = cos_ref[...]
    sin = sin_ref[...]
    qs_ref[...] = _rope(q_ref[...].astype(F32), cos, sin, RET_DK // 4).astype(BF16)
    ks_ref[...] = _rope(k_ref[...].astype(F32), cos, sin, RET_DK // 4) * k_scale

    rowi = lax.broadcasted_iota(I32, (CHUNK, LANES), 0).astype(F32)
    coli = lax.broadcasted_iota(I32, (CHUNK, LANES), 1).astype(F32)
    diff = rowi - coli
    decay = jnp.exp(jnp.where(diff >= 0, lgf * diff, -lgb * diff)) * jnp.where(diff == 0, 2.0, 1.0)
    zeta_f = jnp.exp(lgf * (CHUNK - 1.0 - rowi))
    eta_b = jnp.exp(lgb * rowi)
    xi_f = jnp.exp(lgf * (rowi + 1.0))
    xi_b = jnp.exp(lgb * (CHUNK - rowi))
    cdec_f = jnp.exp(lgf * float(CHUNK))
    cdec_b = jnp.exp(lgb * float(CHUNK))

    crow = lax.broadcasted_iota(I32, (CTX_LEN, LANES), 0).astype(F32)
    kc = kc_ref[...].astype(F32) * k_scale
    vc = vc_ref[...]
    s_f = _dot_tn((kc * jnp.exp(lgf * (CTX_LEN - 1.0 - crow))).astype(BF16), vc)
    s_b = _dot_tn((kc * jnp.exp(lgb * crow)).astype(BF16), vc)

    upd_f, upd_b = [], []
    for i in range(N_CHUNKS):
        rows = pl.ds(i * CHUNK, CHUNK)
        kch = ks_ref[rows, :]
        vch = v_ref[rows, :]
        upd_f.append(_dot_tn((kch * zeta_f).astype(BF16), vch))
        upd_b.append(_dot_tn((kch * eta_b).astype(BF16), vch))
    state = s_f
    for i in range(N_CHUNKS):
        st_ref[i, :, :RET_DV] = state.astype(BF16)
        state = cdec_f * state + upd_f[i]
    state = s_b
    for i in reversed(range(N_CHUNKS)):
        st_ref[i, :, RET_DV:] = state.astype(BF16)
        state = cdec_b * state + upd_b[i]

    gn_w = gn_ref[...]
    for i in range(N_CHUNKS):
        rows = pl.ds(i * CHUNK, CHUNK)
        qch = qs_ref[rows, :]
        scores = _dot_nt(qch, ks_ref[rows, :].astype(BF16)) * decay
        o = jnp.dot(scores.astype(BF16), v_ref[rows, :], preferred_element_type=F32)
        cross = jnp.dot(qch, st_ref[i], preferred_element_type=F32)
        o = o + xi_f * cross[:, :RET_DV] + xi_b * cross[:, RET_DV:]
        y = _plain_norm(o) * gn_w
        o_ref[rows, :] = (_silu(g_ref[rows, :].astype(F32)) * y).astype(BF16)


def _retention(proj, proj_c, cos, sin, decay_rows, gn_w):
    blk = lambda off: pl.BlockSpec((SEQ, LANES), lambda b, h: (b, off // LANES + h))
    blk_c = lambda off: pl.BlockSpec((CTX_LEN, LANES), lambda b, h: (b, off // LANES + h))
    table = pl.BlockSpec((SEQ, LANES), lambda b, h: (0, 0))
    return pl.pallas_call(
        _retention_kernel,
        grid=(BATCH, RET_HEADS),
        in_specs=[
            blk(OFF_Q), blk(OFF_K), blk(OFF_V), blk(OFF_G), blk_c(OFF_K), blk_c(OFF_V), table, table,
            pl.BlockSpec((1, 1, LANES), lambda b, h: (h, 0, 0)),
            pl.BlockSpec((1, 1, LANES), lambda b, h: (RET_HEADS + h, 0, 0)),
            pl.BlockSpec((1, LANES), lambda b, h: (0, h)),
        ],
        out_specs=pl.BlockSpec((SEQ, LANES), lambda b, h: (b, h)),
        out_shape=jax.ShapeDtypeStruct((N_TOK, RET_W), BF16),
        scratch_shapes=[
            pltpu.VMEM((SEQ, RET_DK), BF16),
            pltpu.VMEM((SEQ, RET_DK), F32),
            pltpu.VMEM((N_CHUNKS, RET_DK, 2 * RET_DV), BF16),
        ],
        compiler_params=_params(("arbitrary", "arbitrary")),
        name="retention",
    )(proj, proj, proj, proj, proj_c, proj_c, cos, sin, decay_rows, decay_rows, gn_w)


MLA_TM = 512


def _rms_norm(x, w):
    return x * lax.rsqrt(jnp.mean(x * x, -1, keepdims=True) + EPS) * w


def _mla_kv(ckv_ref, kpe_ref, kvn_ref, wkv_ref, cos_ref, sin_ref, k_ref, v_ref, rotate):
    ckv = _rms_norm(ckv_ref[...].astype(F32), kvn_ref[...]).astype(BF16)
    kv = jnp.dot(ckv, wkv_ref[...], preferred_element_type=F32)
    lane = lax.broadcasted_iota(I32, (ckv.shape[0], LANES), 1)
    kpe = jnp.where(lane < MLA_ROPE, kpe_ref[...].astype(F32), 0.0)
    if rotate:
        kpe = _rope(kpe, cos_ref[...], sin_ref[...], MLA_ROPE // 4)
    kpe = kpe.astype(BF16)
    for h in range(MLA_HEADS):
        k_ref[:, h * QK_PAD:h * QK_PAD + MLA_NOPE] = kv[:, 2 * h * LANES:(2 * h + 1) * LANES].astype(BF16)
        k_ref[:, h * QK_PAD + MLA_NOPE:(h + 1) * QK_PAD] = kpe
        v_ref[:, h * MLA_DV:(h + 1) * MLA_DV] = kv[:, (2 * h + 1) * LANES:(2 * h + 2) * LANES].astype(BF16)


def _mla_latent_kernel(cq_ref, ckv_ref, kpe_ref, qn_ref, kvn_ref, wq_ref, wkv_ref, cos_ref, sin_ref,
                       q_ref, k_ref, v_ref):
    cq = _rms_norm(cq_ref[...].astype(F32), qn_ref[...]).astype(BF16)
    q = jnp.dot(cq, wq_ref[...], preferred_element_type=F32)
    cos = cos_ref[...]
    sin = sin_ref[...]
    scale = MLA_DQ ** -0.5
    for h in range(MLA_HEADS):
        lo = h * QK_PAD
        q_ref[:, lo:lo + MLA_NOPE] = (q[:, lo:lo + MLA_NOPE] * scale).astype(BF16)
        qpe = _rope(q[:, lo + MLA_NOPE:lo + QK_PAD], cos, sin, MLA_ROPE // 4)
        q_ref[:, lo + MLA_NOPE:lo + QK_PAD] = (qpe * scale).astype(BF16)
    _mla_kv(ckv_ref, kpe_ref, kvn_ref, wkv_ref, cos_ref, sin_ref, k_ref, v_ref, rotate=True)


def _mla_context_kernel(ckv_ref, kpe_ref, kvn_ref, wkv_ref, k_ref, v_ref):
    _mla_kv(ckv_ref, kpe_ref, kvn_ref, wkv_ref, None, None, k_ref, v_ref, rotate=False)


def _mla_latent(proj, q_norm, kv_norm, wq_pad, wkv, cos, sin):
    row = lambda w, off: pl.BlockSpec((MLA_TM, w), lambda i: (i, off // w))
    full = lambda a: pl.BlockSpec(a.shape, lambda i: (0, 0))
    table = pl.BlockSpec((MLA_TM, LANES), lambda i: (i % (SEQ // MLA_TM), 0))
    return pl.pallas_call(
        _mla_latent_kernel,
        grid=(N_TOK // MLA_TM,),
        in_specs=[row(MLA_Q_LORA, OFF_CQ), row(MLA_KV_LORA, OFF_CKV), row(LANES, OFF_KPE),
                  full(q_norm), full(kv_norm), full(wq_pad), full(wkv), table, table],
        out_specs=[
            pl.BlockSpec((MLA_TM, MLA_HEADS * QK_PAD), lambda i: (i, 0)),
            pl.BlockSpec((MLA_TM, MLA_HEADS * QK_PAD), lambda i: (i, 0)),
            pl.BlockSpec((MLA_TM, MLA_W), lambda i: (i, 0)),
        ],
        out_shape=[
            jax.ShapeDtypeStruct((N_TOK, MLA_HEADS * QK_PAD), BF16),
            jax.ShapeDtypeStruct((N_TOK, MLA_HEADS * QK_PAD), BF16),
            jax.ShapeDtypeStruct((N_TOK, MLA_W), BF16),
        ],
        compiler_params=_params(("arbitrary",)),
        name="mla_latent",
    )(proj, proj, proj, q_norm, kv_norm, wq_pad, wkv, cos, sin)


def _mla_context(proj_c, kv_norm, wkv):
    row = lambda w, off: pl.BlockSpec((MLA_TM, w), lambda i: (i, off // w))
    full = lambda a: pl.BlockSpec(a.shape, lambda i: (0, 0))
    return pl.pallas_call(
        _mla_context_kernel,
        grid=(N_CTX // MLA_TM,),
        in_specs=[row(MLA_KV_LORA, OFF_CKV), row(LANES, OFF_KPE), full(kv_norm), full(wkv)],
        out_specs=[
            pl.BlockSpec((MLA_TM, MLA_HEADS * QK_PAD), lambda i: (i, 0)),
            pl.BlockSpec((MLA_TM, MLA_W), lambda i: (i, 0)),
        ],
        out_shape=[
            jax.ShapeDtypeStruct((N_CTX, MLA_HEADS * QK_PAD), BF16),
            jax.ShapeDtypeStruct((N_CTX, MLA_W), BF16),
        ],
        compiler_params=_params(("arbitrary",)),
        name="mla_context",
    )(proj_c, proj_c, kv_norm, wkv)


ATT_TQ = 256


def _attention_kernel(q_ref, kc_ref, kl_ref, vc_ref, vl_ref, o_ref):
    for h in range(MLA_HEADS):
        qk = slice(h * QK_PAD, (h + 1) * QK_PAD)
        dv = slice(h * MLA_DV, (h + 1) * MLA_DV)
        q = q_ref[:, qk]
        s_c = _dot_nt(q, kc_ref[:, qk])
        s_l = _dot_nt(q, kl_ref[:, qk])
        m = jnp.maximum(jnp.max(s_c, -1, keepdims=True), jnp.max(s_l, -1, keepdims=True))
        p_c = jnp.exp(s_c - m)
        p_l = jnp.exp(s_l - m)
        denom = jnp.sum(p_c, -1, keepdims=True) + jnp.sum(p_l, -1, keepdims=True)
        o = (jnp.dot(p_c.astype(BF16), vc_ref[:, dv], preferred_element_type=F32)
             + jnp.dot(p_l.astype(BF16), vl_ref[:, dv], preferred_element_type=F32))
        o_ref[:, dv] = (o / denom).astype(BF16)


def _attention(q, k_ctx, k_lat, v_ctx, v_lat):
    tiles = SEQ // ATT_TQ
    return pl.pallas_call(
        _attention_kernel,
        grid=(BATCH, tiles),
        in_specs=[
            pl.BlockSpec((ATT_TQ, MLA_HEADS * QK_PAD), lambda b, i: (b * tiles + i, 0)),
            pl.BlockSpec((CTX_LEN, MLA_HEADS * QK_PAD), lambda b, i: (b, 0)),
            pl.BlockSpec((SEQ, MLA_HEADS * QK_PAD), lambda b, i: (b, 0)),
            pl.BlockSpec((CTX_LEN, MLA_W), lambda b, i: (b, 0)),
            pl.BlockSpec((SEQ, MLA_W), lambda b, i: (b, 0)),
        ],
        out_specs=pl.BlockSpec((ATT_TQ, MLA_W), lambda b, i: (b * tiles + i, 0)),
        out_shape=jax.ShapeDtypeStruct((N_TOK, MLA_W), BF16),
        compiler_params=_params(("arbitrary", "arbitrary")),
        name="attention",
    )(q, k_ctx, k_lat, v_ctx, v_lat)


OUT_TM = 256
ROUTE_E1, ROUTE_E2, ROUTE_R1, ROUTE_R2, ROUTE_W1, ROUTE_W2 = range(6)


def _outproj_kernel(ret_ref, att_ref, x_ref, wo_ref, gate_ref, shift_ref, scale_ref, lnw_ref, lnb_ref,
                    wr_ref, br_ref, h_ref, t_ref, route_ref, count_ref, carry_ref):
    @pl.when(pl.program_id(0) == 0)
    def _():
        carry_ref[...] = jnp.zeros_like(carry_ref)

    mix = (jnp.dot(ret_ref[...], wo_ref[:RET_W, :], preferred_element_type=F32)
           + jnp.dot(att_ref[...], wo_ref[RET_W:, :], preferred_element_type=F32))
    h = _plain_norm(DEEPNORM_ALPHA * x_ref[...] + gate_ref[0] * mix) * lnw_ref[...] + lnb_ref[...]
    h_ref[...] = h
    t = _plain_norm(h) * (1.0 + scale_ref[0]) + shift_ref[0]
    t_ref[...] = _pack_bf16_pairs(t)

    logits = jnp.dot(t, wr_ref[...], preferred_element_type=F32, precision=lax.Precision.HIGHEST) + br_ref[...]
    lane = lax.broadcasted_iota(I32, logits.shape, 1).astype(F32)
    neg = -jnp.inf
    big = float(LANES)

    def first_lane_of(mask):
        return jnp.min(jnp.where(mask, lane, big), -1, keepdims=True)

    is_group = lane < N_GROUPS
    gl = jnp.where(is_group, logits, neg)
    g_max = jnp.max(gl, -1, keepdims=True)
    g_idx = first_lane_of(is_group & (gl == g_max))
    g_prob = 1.0 / jnp.sum(jnp.where(is_group, jnp.exp(logits - g_max), 0.0), -1, keepdims=True)

    lo = N_GROUPS + g_idx * EXPERTS_PER_GROUP
    in_group = (lane >= lo) & (lane < lo + EXPERTS_PER_GROUP)
    el = jnp.where(in_group, logits, neg)
    v1 = jnp.max(el, -1, keepdims=True)
    i1 = first_lane_of(in_group & (el == v1))
    rest = in_group & (lane != i1)
    el2 = jnp.where(rest, logits, neg)
    v2 = jnp.max(el2, -1, keepdims=True)
    i2 = first_lane_of(rest & (el2 == v2))
    d = jnp.exp(v2 - v1)
    w1 = g_prob / (1.0 + d)
    w2 = g_prob * d / (1.0 + d)

    onehot = jnp.where((lane == i1) | (lane == i2), 1.0, 0.0)
    r = lax.broadcasted_iota(I32, (OUT_TM, OUT_TM), 0)
    c = lax.broadcasted_iota(I32, (OUT_TM, OUT_TM), 1)
    tri = jnp.where(c < r, 1.0, 0.0).astype(BF16)
    before = jnp.dot(tri, onehot.astype(BF16), preferred_element_type=F32) + carry_ref[0:1, :]
    r1 = jnp.sum(jnp.where(lane == i1, before, 0.0), -1, keepdims=True)
    r2 = jnp.sum(jnp.where(lane == i2, before, 0.0), -1, keepdims=True)
    carry_ref[...] = carry_ref[...] + jnp.sum(onehot, 0, keepdims=True)
    count_ref[...] = carry_ref[...]

    rec = jnp.zeros_like(logits)
    for slot, val in ((ROUTE_E1, i1 - N_GROUPS), (ROUTE_E2, i2 - N_GROUPS), (ROUTE_R1, r1), (ROUTE_R2, r2),
                      (ROUTE_W1, w1), (ROUTE_W2, w2)):
        rec = jnp.where(lane == slot, val, rec)
    route_ref[...] = rec


def _outproj(ret, att, x2d, wo_bf, mod_rows, ln_w, ln_b, w_route, b_route):
    tiles_per_sample = SEQ // OUT_TM
    row = lambda w: pl.BlockSpec((OUT_TM, w), lambda i: (i, 0))
    full = lambda a: pl.BlockSpec(a.shape, lambda i: (0, 0))
    mod = lambda j: pl.BlockSpec((1, 1, D_MODEL), lambda i: ((i // tiles_per_sample) * 6 + j, 0, 0))
    return pl.pallas_call(
        _outproj_kernel,
        grid=(N_TOK // OUT_TM,),
        in_specs=[row(RET_W), row(MLA_W), row(D_MODEL), full(wo_bf), mod(2), mod(3), mod(4),
                  full(ln_w), full(ln_b), full(w_route), full(b_route)],
        out_specs=[row(D_MODEL), row(PACK_W), row(LANES), pl.BlockSpec((8, LANES), lambda i: (0, 0))],
        out_shape=[
            jax.ShapeDtypeStruct((N_TOK, D_MODEL), F32),
            jax.ShapeDtypeStruct((N_TOK, PACK_W), U32),
            jax.ShapeDtypeStruct((N_TOK, LANES), F32),
            jax.ShapeDtypeStruct((8, LANES), F32),
        ],
        scratch_shapes=[pltpu.VMEM((8, LANES), F32)],
        compiler_params=_params(("arbitrary",)),
        name="outproj_route",
    )(ret, att, x2d, wo_bf, mod_rows, mod_rows, mod_rows, ln_w, ln_b, w_route, b_route)


DISPATCH_BATCH = 64


def _dispatch_kernel(pos_ref, tend_ref, nt_ref, t_ref, xs_ref, zero_ref, sem, zsem):
    zero_ref[...] = jnp.zeros_like(zero_ref)

    def zero_tile(tile):
        return pltpu.make_async_copy(zero_ref, xs_ref.at[pl.ds(tile * ROW_TILE, ROW_TILE)], zsem)

    def has_tiles(e):
        return tend_ref[e] > (tend_ref[e - 1] if e else 0)

    for e in range(N_EXPERTS):
        pl.when(has_tiles(e))(lambda e=e: zero_tile(tend_ref[e] - 1).start())
    lax.fori_loop(nt_ref[0], MAX_TILES, lambda j, c: (zero_tile(j).start(), c)[1], 0)
    for e in range(N_EXPERTS):
        pl.when(has_tiles(e))(lambda e=e: zero_tile(tend_ref[e] - 1).wait())
    lax.fori_loop(nt_ref[0], MAX_TILES, lambda j, c: (zero_tile(j).wait(), c)[1], 0)

    def copy(tok, slot):
        return pltpu.make_async_copy(t_ref.at[pl.ds(tok, 1)], xs_ref.at[pl.ds(pos_ref[2 * tok + slot], 1)], sem)

    def batch(b, carry):
        base = b * DISPATCH_BATCH

        def start(j, c):
            copy(base + j, 0).start()
            copy(base + j, 1).start()
            return c

        def wait(j, c):
            copy(base + j, 0).wait()
            copy(base + j, 1).wait()
            return c

        lax.fori_loop(0, DISPATCH_BATCH, start, 0)
        lax.fori_loop(0, DISPATCH_BATCH, wait, 0)
        return carry

    lax.fori_loop(0, N_TOK // DISPATCH_BATCH, batch, 0)


def _dispatch(pos, tile_end, n_tiles, t_packed):
    return pl.pallas_call(
        _dispatch_kernel,
        grid_spec=pltpu.PrefetchScalarGridSpec(
            num_scalar_prefetch=3,
            grid=(1,),
            in_specs=[pl.BlockSpec(memory_space=pl.ANY)],
            out_specs=pl.BlockSpec(memory_space=pl.ANY),
            scratch_shapes=[pltpu.VMEM((ROW_TILE, PACK_W), U32), pltpu.SemaphoreType.DMA(()),
                            pltpu.SemaphoreType.DMA(())],
        ),
        out_shape=jax.ShapeDtypeStruct((MAX_ROWS, PACK_W), U32),
        compiler_params=_params(("arbitrary",)),
        name="dispatch",
    )(pos, tile_end, n_tiles, t_packed)


def _experts_kernel(te_ref, nt_ref, xs_ref, wg_ref, wu_ref, wd_ref, ys_ref, wgu_ref, wdn_ref):
    i = pl.program_id(0)
    active = i < nt_ref[0]
    prev = te_ref[jnp.maximum(i - 1, 0)]

    @pl.when(active & ((i == 0) | (te_ref[i] != prev)))
    def _():
        wgu_ref[:, :D_EXPERT] = wg_ref[0].astype(BF16)
        wgu_ref[:, D_EXPERT:] = wu_ref[0].astype(BF16)
        wdn_ref[...] = wd_ref[0].astype(BF16)

    @pl.when(active)
    def _():
        lo, hi = _unpack_bf16_pairs(xs_ref[...])
        gu = (jnp.dot(lo.astype(BF16), wgu_ref[:PACK_W, :], preferred_element_type=F32)
              + jnp.dot(hi.astype(BF16), wgu_ref[PACK_W:, :], preferred_element_type=F32))
        hid = _silu(gu[:, :D_EXPERT]) * gu[:, D_EXPERT:]
        y = jnp.dot(hid.astype(BF16), wdn_ref[...], preferred_element_type=F32)
        ys_ref[...] = _pack_bf16_pairs(y)

    @pl.when(jnp.logical_not(active))
    def _():
        ys_ref[...] = jnp.zeros_like(ys_ref)


def _experts(tile_expert, n_tiles, xs, w_gate, w_up, w_down):
    w_gate = w_gate.reshape(N_EXPERTS, D_MODEL, D_EXPERT)
    w_up = w_up.reshape(N_EXPERTS, D_MODEL, D_EXPERT)
    w_down = w_down.reshape(N_EXPERTS, D_EXPERT, D_MODEL)
    rows = lambda i, te, nt: (jnp.minimum(i, nt[0] - 1), 0)
    expert = lambda i, te, nt: (te[i], 0, 0)
    return pl.pallas_call(
        _experts_kernel,
        grid_spec=pltpu.PrefetchScalarGridSpec(
            num_scalar_prefetch=2,
            grid=(MAX_TILES,),
            in_specs=[
                pl.BlockSpec((ROW_TILE, PACK_W), rows),
                pl.BlockSpec((1, D_MODEL, D_EXPERT), expert),
                pl.BlockSpec((1, D_MODEL, D_EXPERT), expert),
                pl.BlockSpec((1, D_EXPERT, D_MODEL), expert),
            ],
            out_specs=pl.BlockSpec((ROW_TILE, PACK_W), lambda i, te, nt: (i, 0)),
            scratch_shapes=[pltpu.VMEM((D_MODEL, 2 * D_EXPERT), BF16), pltpu.VMEM((D_EXPERT, D_MODEL), BF16)],
        ),
        out_shape=jax.ShapeDtypeStruct((MAX_ROWS, PACK_W), U32),
        compiler_params=_params(("arbitrary",)),
        name="experts",
    )(tile_expert, n_tiles, xs, w_gate, w_up, w_down)


COMB_TM = 256


def _combine_kernel(pos_ref, ys_ref, h_ref, route_ref, gate_ref, lnw_ref, lnb_ref, o_ref, buf_ref, sem):
    i = pl.program_id(0)
    n = pl.num_programs(0)

    def copy(step, slot, j):
        src = pos_ref[step * (2 * COMB_TM) + j]
        dst = (j % 2) * COMB_TM + j // 2
        return pltpu.make_async_copy(ys_ref.at[pl.ds(src, 1)], buf_ref.at[slot, pl.ds(dst, 1)], sem.at[slot])

    def start_all(step, slot):
        lax.fori_loop(0, 2 * COMB_TM, lambda j, c: (copy(step, slot, j).start(), c)[1], 0)

    def wait_all(step, slot):
        lax.fori_loop(0, 2 * COMB_TM, lambda j, c: (copy(step, slot, j).wait(), c)[1], 0)

    @pl.when(i == 0)
    def _():
        start_all(0, 0)

    @pl.when(i + 1 < n)
    def _():
        start_all(i + 1, (i + 1) % 2)

    slot = i % 2
    wait_all(i, slot)

    lo1, hi1 = _unpack_bf16_pairs(buf_ref[slot, :COMB_TM, :])
    lo2, hi2 = _unpack_bf16_pairs(buf_ref[slot, COMB_TM:, :])
    w1 = route_ref[:, ROUTE_W1:ROUTE_W1 + 1]
    w2 = route_ref[:, ROUTE_W2:ROUTE_W2 + 1]
    y = jnp.concatenate([w1 * lo1 + w2 * lo2, w1 * hi1 + w2 * hi2], -1)
    z = DEEPNORM_ALPHA * h_ref[...] + gate_ref[0] * y
    o_ref[...] = _plain_norm(z) * lnw_ref[...] + lnb_ref[...]


def _combine(pos, ys, h, route, mod_rows, ln_w, ln_b):
    tiles_per_sample = SEQ // COMB_TM
    row = lambda w: pl.BlockSpec((COMB_TM, w), lambda i, pos: (i, 0))
    full = lambda a: pl.BlockSpec(a.shape, lambda i, pos: (0, 0))
    return pl.pallas_call(
        _combine_kernel,
        grid_spec=pltpu.PrefetchScalarGridSpec(
            num_scalar_prefetch=1,
            grid=(N_TOK // COMB_TM,),
            in_specs=[
                pl.BlockSpec(memory_space=pl.ANY), row(D_MODEL), row(LANES),
                pl.BlockSpec((1, 1, D_MODEL), lambda i, pos: ((i // tiles_per_sample) * 6 + 5, 0, 0)),
                full(ln_w), full(ln_b),
            ],
            out_specs=row(D_MODEL),
            scratch_shapes=[pltpu.VMEM((2, 2 * COMB_TM, PACK_W), U32), pltpu.SemaphoreType.DMA((2,))],
        ),
        out_shape=jax.ShapeDtypeStruct((N_TOK, D_MODEL), F32),
        compiler_params=_params(("arbitrary",)),
        name="combine",
    )(pos, ys, h, route, mod_rows, ln_w, ln_b)


def _routing_tables(route, counts):
    cnt = counts[0, N_GROUPS:N_ROUTE].astype(I32)
    tiles = (cnt + ROW_TILE - 1) // ROW_TILE
    tile_end = jnp.cumsum(tiles)
    tile_start = tile_end - tiles
    n_tiles = tile_end[-1:]
    e = route[:, ROUTE_E1:ROUTE_E2 + 1].astype(I32)
    rank = route[:, ROUTE_R1:ROUTE_R2 + 1].astype(I32)
    pos = (tile_start[e] * ROW_TILE + rank).reshape(-1)
    j = jnp.minimum(jnp.arange(MAX_TILES, dtype=I32), n_tiles - 1)
    tile_expert = jnp.sum((j[:, None] >= tile_end[None, :]).astype(I32), -1)
    return pos, tile_expert, tile_end, n_tiles


def kernel(x, c, ctx, c_ctx, w_ada, b_ada, w_in, ret_decay, ret_gn_w, mla_q_norm, mla_kv_norm, w_uq, w_ukv, w_o,
           ln1_w, ln1_b, router_group_w, router_group_b, router_expert_w, router_expert_b, expert_w_gate,
           expert_w_up, expert_w_down, ln2_w, ln2_b):
    x2d = x.reshape(N_TOK, D_MODEL)
    ctx2d = ctx.reshape(N_CTX, D_MODEL)

    cc = jnp.zeros((8, D_MODEL), F32).at[:BATCH].set(c).at[BATCH].set(c_ctx)
    mod = _ada(cc, w_ada[0], b_ada)
    mod_rows = mod.reshape(8 * 6, 1, D_MODEL)

    w_in_bf = w_in[0].astype(BF16)
    proj = _inproj(x2d, mod_rows, w_in_bf, SEQ, 0)
    proj_c = _inproj(ctx2d, mod_rows, w_in_bf, N_CTX, BATCH)

    cos_r, sin_r = _rope_tables(RET_DK)
    decay_rows = jnp.broadcast_to(ret_decay[0].reshape(2 * RET_HEADS, 1, 1), (2 * RET_HEADS, 1, LANES))
    ret = _retention(proj, proj_c, jnp.asarray(cos_r), jnp.asarray(sin_r), decay_rows, ret_gn_w)

    cos_m, sin_m = _rope_tables(MLA_ROPE)
    cos_m = np.concatenate([cos_m, np.ones_like(cos_m)], 1)
    sin_m = np.concatenate([sin_m, np.zeros_like(sin_m)], 1)
    wq = w_uq[0].reshape(MLA_Q_LORA, MLA_HEADS, MLA_DQ)
    wq_pad = jnp.pad(wq, ((0, 0), (0, 0), (0, QK_PAD - MLA_DQ))).reshape(MLA_Q_LORA, MLA_HEADS * QK_PAD).astype(BF16)
    wkv = w_ukv[0].astype(BF16)
    q, k_lat, v_lat = _mla_latent(proj, mla_q_norm, mla_kv_norm, wq_pad, wkv, jnp.asarray(cos_m), jnp.asarray(sin_m))
    k_ctx, v_ctx = _mla_context(proj_c, mla_kv_norm, wkv)
    att = _attention(q, k_ctx, k_lat, v_ctx, v_lat)

    w_route = jnp.concatenate(
        [router_group_w[0], router_expert_w[0].transpose(1, 0, 2).reshape(D_MODEL, N_EXPERTS),
         jnp.zeros((D_MODEL, LANES - N_ROUTE), F32)], 1)
    b_route = jnp.concatenate(
        [router_group_b[0], router_expert_b[0].reshape(N_EXPERTS), jnp.zeros((LANES - N_ROUTE,), F32)])[None]
    h, t_packed, route, counts = _outproj(ret, att, x2d, w_o[0].astype(BF16), mod_rows, ln1_w, ln1_b,
                                          w_route, b_route)

    pos, tile_expert, tile_end, n_tiles = _routing_tables(route, counts)
    xs = _dispatch(pos, tile_end, n_tiles, t_packed)
    ys = _experts(tile_expert, n_tiles, xs, expert_w_gate[0], expert_w_up[0], expert_w_down[0])
    out = _combine(pos, ys, h, route, mod_rows, ln2_w, ln2_b)
    return out.reshape(BATCH, SEQ, D_MODEL)
```

```python
import functools

import numpy as np
import jax
import jax.numpy as jnp
from jax import lax
from jax.experimental import pallas as pl
from jax.experimental.pallas import tpu as pltpu

F32 = jnp.float32
BF16 = jnp.bfloat16
I32 = jnp.int32

D_MODEL = 2048
BATCH = 4
SEQ = 2048
GRID_W = 64
CTX_LEN = 256
N_TOK = BATCH * SEQ
N_CTX = BATCH * CTX_LEN

RET_HEADS = 8
RET_DK = 128
RET_DV = 128
RET_W = RET_HEADS * RET_DV
CHUNK = 128
N_CHUNKS = SEQ // CHUNK

MLA_HEADS = 8
MLA_Q_LORA = 512
MLA_KV_LORA = 256
MLA_NOPE = 128
MLA_ROPE = 64
MLA_DV = 128
MLA_W = MLA_HEADS * MLA_DV
MLA_DQ = MLA_NOPE + MLA_ROPE
QK_PAD = 256

IN_SIZES = (RET_HEADS * RET_DK, RET_HEADS * RET_DK, RET_W, RET_W, MLA_Q_LORA, MLA_KV_LORA, MLA_ROPE)
IN_W = sum(IN_SIZES)
OFF_Q, OFF_K, OFF_V, OFF_G, OFF_CQ, OFF_CKV, OFF_KPE = (int(v) for v in np.cumsum((0,) + IN_SIZES[:-1]))

N_GROUPS = 4
EXPERTS_PER_GROUP = 8
N_EXPERTS = N_GROUPS * EXPERTS_PER_GROUP
D_EXPERT = 512
N_ROUTE = N_GROUPS + N_EXPERTS

ROPE_BASE = 10000.0
EPS = 1e-6
DEPTH = 1
DEEPNORM_ALPHA = (2.0 * DEPTH) ** 0.25

LANES = 128
ROW_TILE = 256
N_PAIRS = 2 * N_TOK
MAX_TILES = N_PAIRS // ROW_TILE + N_EXPERTS
MAX_ROWS = MAX_TILES * ROW_TILE
TOKEN_SUB = D_MODEL // LANES

VMEM_LIMIT = 56 * 1024 * 1024


def _params(sem, vmem=VMEM_LIMIT):
    return pltpu.CompilerParams(dimension_semantics=sem, vmem_limit_bytes=vmem)


def _silu(x):
    return x * (1.0 / (1.0 + jnp.exp(-x)))


def _plain_norm(x):
    mu = jnp.mean(x, -1, keepdims=True)
    xc = x - mu
    var = jnp.mean(xc * xc, -1, keepdims=True)
    return xc * lax.rsqrt(var + EPS)


def _to_token_tiles(x):
    return x.astype(BF16).reshape(x.shape[0], TOKEN_SUB, LANES)


def _from_token_tiles(x):
    return x.reshape(x.shape[0], D_MODEL)


ADA_TN = 1024


def _ada_kernel(cc_ref, w_ref, b_ref, o_ref):
    s = _silu(cc_ref[...])
    o_ref[...] = jnp.dot(s, w_ref[...], preferred_element_type=F32,
                         precision=lax.Precision.HIGHEST) + b_ref[...]


def _ada(cc, w_ada, b_ada):
    n = w_ada.shape[1]
    return pl.pallas_call(
        _ada_kernel,
        grid=(n // ADA_TN,),
        in_specs=[
            pl.BlockSpec((8, D_MODEL), lambda j: (0, 0)),
            pl.BlockSpec((D_MODEL, ADA_TN), lambda j: (0, j)),
            pl.BlockSpec((1, ADA_TN), lambda j: (0, j)),
        ],
        out_specs=pl.BlockSpec((8, ADA_TN), lambda j: (0, j)),
        out_shape=jax.ShapeDtypeStruct((8, n), F32),
        compiler_params=_params(("arbitrary",)),
        name="ada",
    )(cc, w_ada, b_ada)


INPROJ_TM = 1024
INPROJ_TN = 1024


def _inproj_kernel(x_ref, shift_ref, scale_ref, w_ref, o_ref, xn_ref):
    @pl.when(pl.program_id(1) == 0)
    def _():
        y = _plain_norm(x_ref[...]) * (1.0 + scale_ref[0]) + shift_ref[0]
        xn_ref[...] = y.astype(BF16)

    o_ref[...] = jnp.dot(xn_ref[...], w_ref[...], preferred_element_type=F32).astype(BF16)


def _inproj(x2d, mod_rows, w_in_bf, rows_per_sample, sample_row0):
    n = x2d.shape[0]
    tiles_per_sample = rows_per_sample // INPROJ_TM

    def mod_map(j):
        return lambda i, k: ((sample_row0 + i // tiles_per_sample) * 6 + j, 0, 0)

    return pl.pallas_call(
        _inproj_kernel,
        grid=(n // INPROJ_TM, pl.cdiv(IN_W, INPROJ_TN)),
        in_specs=[
            pl.BlockSpec((INPROJ_TM, D_MODEL), lambda i, k: (i, 0)),
            pl.BlockSpec((1, 1, D_MODEL), mod_map(0)),
            pl.BlockSpec((1, 1, D_MODEL), mod_map(1)),
            pl.BlockSpec((D_MODEL, INPROJ_TN), lambda i, k: (0, k)),
        ],
        out_specs=pl.BlockSpec((INPROJ_TM, INPROJ_TN), lambda i, k: (i, k)),
        out_shape=jax.ShapeDtypeStruct((n, IN_W), BF16),
        scratch_shapes=[pltpu.VMEM((INPROJ_TM, D_MODEL), BF16)],
        compiler_params=_params(("arbitrary", "arbitrary")),
        name="inproj",
    )(x2d, mod_rows, mod_rows, w_in_bf)


def _rope_tables(width):
    half = width // 2
    quarter = half // 2
    inv_freq = ROPE_BASE ** (-np.arange(0, half, 2, dtype=np.float64) / half)
    t = np.arange(SEQ)
    cos_parts, sin_parts = [], []
    for pos in (t // GRID_W, t % GRID_W):
        ang = pos[:, None].astype(np.float64) * inv_freq[None, :]
        c, s = np.cos(ang), np.sin(ang)
        cos_parts += [c, c]
        sin_parts += [-s, s]
    assert cos_parts[0].shape[1] == quarter
    return (np.concatenate(cos_parts, 1).astype(np.float32), np.concatenate(sin_parts, 1).astype(np.float32))


def _rope(x, cos, sin, quarter):
    lane = lax.broadcasted_iota(I32, x.shape, 1)
    first = (lane % (2 * quarter)) < quarter
    swapped = jnp.where(first, pltpu.roll(x, LANES - quarter, 1), pltpu.roll(x, quarter, 1))
    return x * cos + swapped * sin


def _dot_tn(a, b):
    return lax.dot_general(a, b, (((0,), (0,)), ((), ())), preferred_element_type=F32)


def _dot_nt(a, b):
    return lax.dot_general(a, b, (((1,), (1,)), ((), ())), preferred_element_type=F32)


def _retention_kernel(q_ref, k_ref, v_ref, g_ref, kc_ref, vc_ref, cos_ref, sin_ref, df_ref, db_ref, gn_ref,
                      o_ref, qs_ref, ks_ref, st_ref):
    k_scale = RET_DK ** -0.5
    lgf = jax.nn.log_sigmoid(df_ref[0])
    lgb = jax.nn.log_sigmoid(db_ref[0])

    cos = cos_ref[...]
    sin = sin_ref[...]
    qs_ref[...] = _rope(q_ref[...].astype(F32), cos, sin, RET_DK // 4).astype(BF16)
    ks_ref[...] = _rope(k_ref[...].astype(F32), cos, sin, RET_DK // 4) * k_scale

    rowi = lax.broadcasted_iota(I32, (CHUNK, LANES), 0).astype(F32)
    coli = lax.broadcasted_iota(I32, (CHUNK, LANES), 1).astype(F32)
    diff = rowi - coli
    decay = jnp.exp(jnp.where(diff >= 0, lgf * diff, -lgb * diff)) * jnp.where(diff == 0, 2.0, 1.0)
    zeta_f = jnp.exp(lgf * (CHUNK - 1.0 - rowi))
    eta_b = jnp.exp(lgb * rowi)
    xi_f = jnp.exp(lgf * (rowi + 1.0))
    xi_b = jnp.exp(lgb * (CHUNK - rowi))
    cdec_f = jnp.exp(lgf * float(CHUNK))
    cdec_b = jnp.exp(lgb * float(CHUNK))

    crow = lax.broadcasted_iota(I32, (CTX_LEN, LANES), 0).astype(F32)
    kc = kc_ref[...].astype(F32) * k_scale
    vc = vc_ref[...]
    s_f = _dot_tn((kc * jnp.exp(lgf * (CTX_LEN - 1.0 - crow))).astype(BF16), vc)
    s_b = _dot_tn((kc * jnp.exp(lgb * crow)).astype(BF16), vc)

    upd_f, upd_b = [], []
    for i in range(N_CHUNKS):
        rows = pl.ds(i * CHUNK, CHUNK)
        kch = ks_ref[rows, :]
        vch = v_ref[rows, :]
        upd_f.append(_dot_tn((kch * zeta_f).astype(BF16), vch))
        upd_b.append(_dot_tn((kch * eta_b).astype(BF16), vch))
    state = s_f
    for i in range(N_CHUNKS):
        st_ref[i, :, :RET_DV] = state.astype(BF16)
        state = cdec_f * state + upd_f[i]
    state = s_b
    for i in reversed(range(N_CHUNKS)):
        st_ref[i, :, RET_DV:] = state.astype(BF16)
        state = cdec_b * state + upd_b[i]

    gn_w = gn_ref[...]
    for i in range(N_CHUNKS):
        rows = pl.ds(i * CHUNK, CHUNK)
        qch = qs_ref[rows, :]
        scores = _dot_nt(qch, ks_ref[rows, :].astype(BF16)) * decay
        o = jnp.dot(scores.astype(BF16), v_ref[rows, :], preferred_element_type=F32)
        cross = jnp.dot(qch, st_ref[i], preferred_element_type=F32)
        o = o + xi_f * cross[:, :RET_DV] + xi_b * cross[:, RET_DV:]
        y = _plain_norm(o) * gn_w
        o_ref[rows, :] = (_silu(g_ref[rows, :].astype(F32)) * y).astype(BF16)


def _retention(proj, proj_c, cos, sin, decay_rows, gn_w):
    blk = lambda off: pl.BlockSpec((SEQ, LANES), lambda b, h: (b, off // LANES + h))
    blk_c = lambda off: pl.BlockSpec((CTX_LEN, LANES), lambda b, h: (b, off // LANES + h))
    table = pl.BlockSpec((SEQ, LANES), lambda b, h: (0, 0))
    return pl.pallas_call(
        _retention_kernel,
        grid=(BATCH, RET_HEADS),
        in_specs=[
            blk(OFF_Q), blk(OFF_K), blk(OFF_V), blk(OFF_G), blk_c(OFF_K), blk_c(OFF_V), table, table,
            pl.BlockSpec((1, 1, LANES), lambda b, h: (h, 0, 0)),
            pl.BlockSpec((1, 1, LANES), lambda b, h: (RET_HEADS + h, 0, 0)),
            pl.BlockSpec((1, LANES), lambda b, h: (0, h)),
        ],
        out_specs=pl.BlockSpec((SEQ, LANES), lambda b, h: (b, h)),
        out_shape=jax.ShapeDtypeStruct((N_TOK, RET_W), BF16),
        scratch_shapes=[
            pltpu.VMEM((SEQ, RET_DK), BF16),
            pltpu.VMEM((SEQ, RET_DK), F32),
            pltpu.VMEM((N_CHUNKS, RET_DK, 2 * RET_DV), BF16),
        ],
        compiler_params=_params(("arbitrary", "arbitrary")),
        name="retention",
    )(proj, proj, proj, proj, proj_c, proj_c, cos, sin, decay_rows, decay_rows, gn_w)


MLA_TM = 512


def _rms_norm(x, w):
    return x * lax.rsqrt(jnp.mean(x * x, -1, keepdims=True) + EPS) * w


def _mla_kv(ckv_ref, kpe_ref, kvn_ref, wkv_ref, cos_ref, sin_ref, k_ref, v_ref, rotate):
    ckv = _rms_norm(ckv_ref[...].astype(F32), kvn_ref[...]).astype(BF16)
    kv = jnp.dot(ckv, wkv_ref[...], preferred_element_type=F32)
    lane = lax.broadcasted_iota(I32, (ckv.shape[0], LANES), 1)
    kpe = jnp.where(lane < MLA_ROPE, kpe_ref[...].astype(F32), 0.0)
    if rotate:
        kpe = _rope(kpe, cos_ref[...], sin_ref[...], MLA_ROPE // 4)
    kpe = kpe.astype(BF16)
    for h in range(MLA_HEADS):
        k_ref[:, h * QK_PAD:h * QK_PAD + MLA_NOPE] = kv[:, 2 * h * LANES:(2 * h + 1) * LANES].astype(BF16)
        k_ref[:, h * QK_PAD + MLA_NOPE:(h + 1) * QK_PAD] = kpe
        v_ref[:, h * MLA_DV:(h + 1) * MLA_DV] = kv[:, (2 * h + 1) * LANES:(2 * h + 2) * LANES].astype(BF16)


def _mla_latent_kernel(cq_ref, ckv_ref, kpe_ref, qn_ref, kvn_ref, wq_ref, wkv_ref, cos_ref, sin_ref,
                       q_ref, k_ref, v_ref):
    cq = _rms_norm(cq_ref[...].astype(F32), qn_ref[...]).astype(BF16)
    q = jnp.dot(cq, wq_ref[...], preferred_element_type=F32)
    cos = cos_ref[...]
    sin = sin_ref[...]
    scale = MLA_DQ ** -0.5
    for h in range(MLA_HEADS):
        lo = h * QK_PAD
        q_ref[:, lo:lo + MLA_NOPE] = (q[:, lo:lo + MLA_NOPE] * scale).astype(BF16)
        qpe = _rope(q[:, lo + MLA_NOPE:lo + QK_PAD], cos, sin, MLA_ROPE // 4)
        q_ref[:, lo + MLA_NOPE:lo + QK_PAD] = (qpe * scale).astype(BF16)
    _mla_kv(ckv_ref, kpe_ref, kvn_ref, wkv_ref, cos_ref, sin_ref, k_ref, v_ref, rotate=True)


def _mla_context_kernel(ckv_ref, kpe_ref, kvn_ref, wkv_ref, k_ref, v_ref):
    _mla_kv(ckv_ref, kpe_ref, kvn_ref, wkv_ref, None, None, k_ref, v_ref, rotate=False)


def _mla_latent(proj, q_norm, kv_norm, wq_pad, wkv, cos, sin):
    row = lambda w, off: pl.BlockSpec((MLA_TM, w), lambda i: (i, off // w))
    full = lambda a: pl.BlockSpec(a.shape, lambda i: (0, 0))
    table = pl.BlockSpec((MLA_TM, LANES), lambda i: (i % (SEQ // MLA_TM), 0))
    return pl.pallas_call(
        _mla_latent_kernel,
        grid=(N_TOK // MLA_TM,),
        in_specs=[row(MLA_Q_LORA, OFF_CQ), row(MLA_KV_LORA, OFF_CKV), row(LANES, OFF_KPE),
                  full(q_norm), full(kv_norm), full(wq_pad), full(wkv), table, table],
        out_specs=[
            pl.BlockSpec((MLA_TM, MLA_HEADS * QK_PAD), lambda i: (i, 0)),
            pl.BlockSpec((MLA_TM, MLA_HEADS * QK_PAD), lambda i: (i, 0)),
            pl.BlockSpec((MLA_TM, MLA_W), lambda i: (i, 0)),
        ],
        out_shape=[
            jax.ShapeDtypeStruct((N_TOK, MLA_HEADS * QK_PAD), BF16),
            jax.ShapeDtypeStruct((N_TOK, MLA_HEADS * QK_PAD), BF16),
            jax.ShapeDtypeStruct((N_TOK, MLA_W), BF16),
        ],
        compiler_params=_params(("arbitrary",)),
        name="mla_latent",
    )(proj, proj, proj, q_norm, kv_norm, wq_pad, wkv, cos, sin)


def _mla_context(proj_c, kv_norm, wkv):
    row = lambda w, off: pl.BlockSpec((MLA_TM, w), lambda i: (i, off // w))
    full = lambda a: pl.BlockSpec(a.shape, lambda i: (0, 0))
    return pl.pallas_call(
        _mla_context_kernel,
        grid=(N_CTX // MLA_TM,),
        in_specs=[row(MLA_KV_LORA, OFF_CKV), row(LANES, OFF_KPE), full(kv_norm), full(wkv)],
        out_specs=[
            pl.BlockSpec((MLA_TM, MLA_HEADS * QK_PAD), lambda i: (i, 0)),
            pl.BlockSpec((MLA_TM, MLA_W), lambda i: (i, 0)),
        ],
        out_shape=[
            jax.ShapeDtypeStruct((N_CTX, MLA_HEADS * QK_PAD), BF16),
            jax.ShapeDtypeStruct((N_CTX, MLA_W), BF16),
        ],
        compiler_params=_params(("arbitrary",)),
        name="mla_context",
    )(proj_c, proj_c, kv_norm, wkv)


ATT_TQ = 256


def _attention_kernel(q_ref, kc_ref, kl_ref, vc_ref, vl_ref, o_ref):
    for h in range(MLA_HEADS):
        qk = slice(h * QK_PAD, (h + 1) * QK_PAD)
        dv = slice(h * MLA_DV, (h + 1) * MLA_DV)
        q = q_ref[:, qk]
        s_c = _dot_nt(q, kc_ref[:, qk])
        s_l = _dot_nt(q, kl_ref[:, qk])
        m = jnp.maximum(jnp.max(s_c, -1, keepdims=True), jnp.max(s_l, -1, keepdims=True))
        p_c = jnp.exp(s_c - m)
        p_l = jnp.exp(s_l - m)
        denom = jnp.sum(p_c, -1, keepdims=True) + jnp.sum(p_l, -1, keepdims=True)
        o = (jnp.dot(p_c.astype(BF16), vc_ref[:, dv], preferred_element_type=F32)
             + jnp.dot(p_l.astype(BF16), vl_ref[:, dv], preferred_element_type=F32))
        o_ref[:, dv] = (o / denom).astype(BF16)


def _attention(q, k_ctx, k_lat, v_ctx, v_lat):
    tiles = SEQ // ATT_TQ
    return pl.pallas_call(
        _attention_kernel,
        grid=(BATCH, tiles),
        in_specs=[
            pl.BlockSpec((ATT_TQ, MLA_HEADS * QK_PAD), lambda b, i: (b * tiles + i, 0)),
            pl.BlockSpec((CTX_LEN, MLA_HEADS * QK_PAD), lambda b, i: (b, 0)),
            pl.BlockSpec((SEQ, MLA_HEADS * QK_PAD), lambda b, i: (b, 0)),
            pl.BlockSpec((CTX_LEN, MLA_W), lambda b, i: (b, 0)),
            pl.BlockSpec((SEQ, MLA_W), lambda b, i: (b, 0)),
        ],
        out_specs=pl.BlockSpec((ATT_TQ, MLA_W), lambda b, i: (b * tiles + i, 0)),
        out_shape=jax.ShapeDtypeStruct((N_TOK, MLA_W), BF16),
        compiler_params=_params(("arbitrary", "arbitrary")),
        name="attention",
    )(q, k_ctx, k_lat, v_ctx, v_lat)


OUT_TM = 256
ROUTE_E1, ROUTE_E2, ROUTE_R1, ROUTE_R2, ROUTE_W1, ROUTE_W2 = range(6)


def _outproj_kernel(ret_ref, att_ref, x_ref, wo_ref, gate_ref, shift_ref, scale_ref, lnw_ref, lnb_ref,
                    wr_ref, br_ref, h_ref, t_ref, route_ref, count_ref, carry_ref):
    @pl.when(pl.program_id(0) == 0)
    def _():
        carry_ref[...] = jnp.zeros_like(carry_ref)

    mix = (jnp.dot(ret_ref[...], wo_ref[:RET_W, :], preferred_element_type=F32)
           + jnp.dot(att_ref[...], wo_ref[RET_W:, :], preferred_element_type=F32))
    h = _plain_norm(DEEPNORM_ALPHA * x_ref[...] + gate_ref[0] * mix) * lnw_ref[...] + lnb_ref[...]
    h_ref[...] = h
    t = _plain_norm(h) * (1.0 + scale_ref[0]) + shift_ref[0]
    t_ref[...] = _to_token_tiles(t)

    t_hi = t.astype(BF16)
    t_lo = (t - t_hi.astype(F32)).astype(BF16)
    main = jnp.dot(t_hi, wr_ref[...], preferred_element_type=F32)
    corr = jnp.dot(t_lo, wr_ref[:, :LANES], preferred_element_type=F32)
    logits = main[:, :LANES] + (main[:, LANES:] + corr) + br_ref[...]
    lane = lax.broadcasted_iota(I32, logits.shape, 1).astype(F32)
    neg = -jnp.inf
    big = float(LANES)

    def first_lane_of(mask):
        return jnp.min(jnp.where(mask, lane, big), -1, keepdims=True)

    is_group = lane < N_GROUPS
    gl = jnp.where(is_group, logits, neg)
    g_max = jnp.max(gl, -1, keepdims=True)
    g_idx = first_lane_of(is_group & (gl == g_max))
    g_prob = 1.0 / jnp.sum(jnp.where(is_group, jnp.exp(logits - g_max), 0.0), -1, keepdims=True)

    lo = N_GROUPS + g_idx * EXPERTS_PER_GROUP
    in_group = (lane >= lo) & (lane < lo + EXPERTS_PER_GROUP)
    el = jnp.where(in_group, logits, neg)
    v1 = jnp.max(el, -1, keepdims=True)
    i1 = first_lane_of(in_group & (el == v1))
    rest = in_group & (lane != i1)
    el2 = jnp.where(rest, logits, neg)
    v2 = jnp.max(el2, -1, keepdims=True)
    i2 = first_lane_of(rest & (el2 == v2))
    d = jnp.exp(v2 - v1)
    w1 = g_prob / (1.0 + d)
    w2 = g_prob * d / (1.0 + d)

    onehot = jnp.where((lane == i1) | (lane == i2), 1.0, 0.0)
    r = lax.broadcasted_iota(I32, (OUT_TM, OUT_TM), 0)
    c = lax.broadcasted_iota(I32, (OUT_TM, OUT_TM), 1)
    tri = jnp.where(c < r, 1.0, 0.0).astype(BF16)
    before = jnp.dot(tri, onehot.astype(BF16), preferred_element_type=F32) + carry_ref[0:1, :]
    r1 = jnp.sum(jnp.where(lane == i1, before, 0.0), -1, keepdims=True)
    r2 = jnp.sum(jnp.where(lane == i2, before, 0.0), -1, keepdims=True)
    carry_ref[...] = carry_ref[...] + jnp.sum(onehot, 0, keepdims=True)
    count_ref[...] = carry_ref[...]

    rec = jnp.zeros_like(logits)
    for slot, val in ((ROUTE_E1, i1 - N_GROUPS), (ROUTE_E2, i2 - N_GROUPS), (ROUTE_R1, r1), (ROUTE_R2, r2),
                      (ROUTE_W1, w1), (ROUTE_W2, w2)):
        rec = jnp.where(lane == slot, val, rec)
    route_ref[...] = rec


def _outproj(ret, att, x2d, wo_bf, mod_rows, ln_w, ln_b, w_route, b_route):
    tiles_per_sample = SEQ // OUT_TM
    row = lambda w: pl.BlockSpec((OUT_TM, w), lambda i: (i, 0))
    full = lambda a: pl.BlockSpec(a.shape, lambda i: (0, 0))
    mod = lambda j: pl.BlockSpec((1, 1, D_MODEL), lambda i: ((i // tiles_per_sample) * 6 + j, 0, 0))
    return pl.pallas_call(
        _outproj_kernel,
        grid=(N_TOK // OUT_TM,),
        in_specs=[row(RET_W), row(MLA_W), row(D_MODEL), full(wo_bf), mod(2), mod(3), mod(4),
                  full(ln_w), full(ln_b), full(w_route), full(b_route)],
        out_specs=[row(D_MODEL), pl.BlockSpec((OUT_TM, TOKEN_SUB, LANES), lambda i: (i, 0, 0)), row(LANES),
                   pl.BlockSpec((8, LANES), lambda i: (0, 0))],
        out_shape=[
            jax.ShapeDtypeStruct((N_TOK, D_MODEL), F32),
            jax.ShapeDtypeStruct((N_TOK, TOKEN_SUB, LANES), BF16),
            jax.ShapeDtypeStruct((N_TOK, LANES), F32),
            jax.ShapeDtypeStruct((8, LANES), F32),
        ],
        scratch_shapes=[pltpu.VMEM((8, LANES), F32)],
        compiler_params=_params(("arbitrary",)),
        name="outproj_route",
    )(ret, att, x2d, wo_bf, mod_rows, mod_rows, mod_rows, ln_w, ln_b, w_route, b_route)


DISPATCH_BATCH = 256


def _dispatch_kernel(pos_ref, tend_ref, nt_ref, t_ref, xs_ref, zero_ref, sem, zsem):
    zero_ref[...] = jnp.zeros_like(zero_ref)

    def zero_tile(tile):
        return pltpu.make_async_copy(zero_ref, xs_ref.at[pl.ds(tile * ROW_TILE, ROW_TILE)], zsem)

    def has_tiles(e):
        return tend_ref[e] > (tend_ref[e - 1] if e else 0)

    for e in range(N_EXPERTS):
        pl.when(has_tiles(e))(lambda e=e: zero_tile(tend_ref[e] - 1).start())
    lax.fori_loop(nt_ref[0], MAX_TILES, lambda j, c: (zero_tile(j).start(), c)[1], 0)
    for e in range(N_EXPERTS):
        pl.when(has_tiles(e))(lambda e=e: zero_tile(tend_ref[e] - 1).wait())
    lax.fori_loop(nt_ref[0], MAX_TILES, lambda j, c: (zero_tile(j).wait(), c)[1], 0)

    def copy(tok, slot):
        return pltpu.make_async_copy(t_ref.at[tok], xs_ref.at[pos_ref[2 * tok + slot]], sem)

    def batch(b, carry):
        base = b * DISPATCH_BATCH

        def start(j, c):
            copy(base + j, 0).start()
            copy(base + j, 1).start()
            return c

        def wait(j, c):
            copy(base + j, 0).wait()
            copy(base + j, 1).wait()
            return c

        lax.fori_loop(0, DISPATCH_BATCH, start, 0, unroll=8)
        lax.fori_loop(0, DISPATCH_BATCH, wait, 0, unroll=8)
        return carry

    lax.fori_loop(0, N_TOK // DISPATCH_BATCH, batch, 0)


def _dispatch(pos, tile_end, n_tiles, t_packed):
    return pl.pallas_call(
        _dispatch_kernel,
        grid_spec=pltpu.PrefetchScalarGridSpec(
            num_scalar_prefetch=3,
            grid=(1,),
            in_specs=[pl.BlockSpec(memory_space=pl.ANY)],
            out_specs=pl.BlockSpec(memory_space=pl.ANY),
            scratch_shapes=[pltpu.VMEM((ROW_TILE, TOKEN_SUB, LANES), BF16), pltpu.SemaphoreType.DMA(()),
                            pltpu.SemaphoreType.DMA(())],
        ),
        out_shape=jax.ShapeDtypeStruct((MAX_ROWS, TOKEN_SUB, LANES), BF16),
        compiler_params=_params(("arbitrary",)),
        name="dispatch",
    )(pos, tile_end, n_tiles, t_packed)


def _experts_kernel(te_ref, nt_ref, xs_ref, wg_ref, wu_ref, wd_ref, ys_ref, wgu_ref, wdn_ref):
    i = pl.program_id(0)
    active = i < nt_ref[0]
    prev = te_ref[jnp.maximum(i - 1, 0)]

    @pl.when(active & ((i == 0) | (te_ref[i] != prev)))
    def _():
        wgu_ref[:, :D_EXPERT] = wg_ref[0].astype(BF16)
        wgu_ref[:, D_EXPERT:] = wu_ref[0].astype(BF16)
        wdn_ref[...] = wd_ref[0].astype(BF16)

    @pl.when(active)
    def _():
        gu = jnp.dot(_from_token_tiles(xs_ref[...]), wgu_ref[...], preferred_element_type=F32)
        hid = _silu(gu[:, :D_EXPERT]) * gu[:, D_EXPERT:]
        y = jnp.dot(hid.astype(BF16), wdn_ref[...], preferred_element_type=F32)
        ys_ref[...] = _to_token_tiles(y)

    @pl.when(jnp.logical_not(active))
    def _():
        ys_ref[...] = jnp.zeros_like(ys_ref)


def _experts(tile_expert, n_tiles, xs, w_gate, w_up, w_down):
    w_gate = w_gate.reshape(N_EXPERTS, D_MODEL, D_EXPERT)
    w_up = w_up.reshape(N_EXPERTS, D_MODEL, D_EXPERT)
    w_down = w_down.reshape(N_EXPERTS, D_EXPERT, D_MODEL)
    rows = lambda i, te, nt: (jnp.minimum(i, nt[0] - 1), 0, 0)
    expert = lambda i, te, nt: (te[i], 0, 0)
    return pl.pallas_call(
        _experts_kernel,
        grid_spec=pltpu.PrefetchScalarGridSpec(
            num_scalar_prefetch=2,
            grid=(MAX_TILES,),
            in_specs=[
                pl.BlockSpec((ROW_TILE, TOKEN_SUB, LANES), rows),
                pl.BlockSpec((1, D_MODEL, D_EXPERT), expert),
                pl.BlockSpec((1, D_MODEL, D_EXPERT), expert),
                pl.BlockSpec((1, D_EXPERT, D_MODEL), expert),
            ],
            out_specs=pl.BlockSpec((ROW_TILE, TOKEN_SUB, LANES), lambda i, te, nt: (i, 0, 0)),
            scratch_shapes=[pltpu.VMEM((D_MODEL, 2 * D_EXPERT), BF16), pltpu.VMEM((D_EXPERT, D_MODEL), BF16)],
        ),
        out_shape=jax.ShapeDtypeStruct((MAX_ROWS, TOKEN_SUB, LANES), BF16),
        compiler_params=_params(("arbitrary",)),
        name="experts",
    )(tile_expert, n_tiles, xs, w_gate, w_up, w_down)


COMB_TM = 256


def _combine_kernel(pos_ref, ys_ref, h_ref, route_ref, gate_ref, lnw_ref, lnb_ref, o_ref, buf_ref, sem):
    i = pl.program_id(0)
    n = pl.num_programs(0)

    def copy(step, slot, j):
        src = pos_ref[step * (2 * COMB_TM) + j]
        dst = (j % 2) * COMB_TM + j // 2
        return pltpu.make_async_copy(ys_ref.at[src], buf_ref.at[slot, dst], sem.at[slot])

    def start_all(step, slot):
        lax.fori_loop(0, 2 * COMB_TM, lambda j, c: (copy(step, slot, j).start(), c)[1], 0, unroll=8)

    def wait_all(step, slot):
        lax.fori_loop(0, 2 * COMB_TM, lambda j, c: (copy(step, slot, j).wait(), c)[1], 0, unroll=8)

    @pl.when(i == 0)
    def _():
        start_all(0, 0)

    @pl.when(i + 1 < n)
    def _():
        start_all(i + 1, (i + 1) % 2)

    slot = i % 2
    wait_all(i, slot)

    y1 = _from_token_tiles(buf_ref[slot, :COMB_TM]).astype(F32)
    y2 = _from_token_tiles(buf_ref[slot, COMB_TM:]).astype(F32)
    y = route_ref[:, ROUTE_W1:ROUTE_W1 + 1] * y1 + route_ref[:, ROUTE_W2:ROUTE_W2 + 1] * y2
    z = DEEPNORM_ALPHA * h_ref[...] + gate_ref[0] * y
    o_ref[...] = _plain_norm(z) * lnw_ref[...] + lnb_ref[...]


def _combine(pos, ys, h, route, mod_rows, ln_w, ln_b):
    tiles_per_sample = SEQ // COMB_TM
    row = lambda w: pl.BlockSpec((COMB_TM, w), lambda i, pos: (i, 0))
    full = lambda a: pl.BlockSpec(a.shape, lambda i, pos: (0, 0))
    return pl.pallas_call(
        _combine_kernel,
        grid_spec=pltpu.PrefetchScalarGridSpec(
            num_scalar_prefetch=1,
            grid=(N_TOK // COMB_TM,),
            in_specs=[
                pl.BlockSpec(memory_space=pl.ANY), row(D_MODEL), row(LANES),
                pl.BlockSpec((1, 1, D_MODEL), lambda i, pos: ((i // tiles_per_sample) * 6 + 5, 0, 0)),
                full(ln_w), full(ln_b),
            ],
            out_specs=row(D_MODEL),
            scratch_shapes=[pltpu.VMEM((2, 2 * COMB_TM, TOKEN_SUB, LANES), BF16), pltpu.SemaphoreType.DMA((2,))],
        ),
        out_shape=jax.ShapeDtypeStruct((N_TOK, D_MODEL), F32),
        compiler_params=_params(("arbitrary",)),
        name="combine",
    )(pos, ys, h, route, mod_rows, ln_w, ln_b)


def _routing_tables(route, counts):
    cnt = counts[0, N_GROUPS:N_ROUTE].astype(I32)
    tiles = (cnt + ROW_TILE - 1) // ROW_TILE
    tile_end = jnp.cumsum(tiles)
    tile_start = tile_end - tiles
    n_tiles = tile_end[-1:]
    e = route[:, ROUTE_E1:ROUTE_E2 + 1].astype(I32)
    rank = route[:, ROUTE_R1:ROUTE_R2 + 1].astype(I32)
    pos = (tile_start[e] * ROW_TILE + rank).reshape(-1)
    j = jnp.minimum(jnp.arange(MAX_TILES, dtype=I32), n_tiles - 1)
    tile_expert = jnp.sum((j[:, None] >= tile_end[None, :]).astype(I32), -1)
    return pos, tile_expert, tile_end, n_tiles


def kernel(x, c, ctx, c_ctx, w_ada, b_ada, w_in, ret_decay, ret_gn_w, mla_q_norm, mla_kv_norm, w_uq, w_ukv, w_o,
           ln1_w, ln1_b, router_group_w, router_group_b, router_expert_w, router_expert_b, expert_w_gate,
           expert_w_up, expert_w_down, ln2_w, ln2_b):
    x2d = x.reshape(N_TOK, D_MODEL)
    ctx2d = ctx.reshape(N_CTX, D_MODEL)

    cc = jnp.zeros((8, D_MODEL), F32).at[:BATCH].set(c).at[BATCH].set(c_ctx)
    mod = _ada(cc, w_ada[0], b_ada)
    mod_rows = mod.reshape(8 * 6, 1, D_MODEL)

    w_in_bf = w_in[0].astype(BF16)
    proj = _inproj(x2d, mod_rows, w_in_bf, SEQ, 0)
    proj_c = _inproj(ctx2d, mod_rows, w_in_bf, N_CTX, BATCH)

    cos_r, sin_r = _rope_tables(RET_DK)
    decay_rows = jnp.broadcast_to(ret_decay[0].reshape(2 * RET_HEADS, 1, 1), (2 * RET_HEADS, 1, LANES))
    ret = _retention(proj, proj_c, jnp.asarray(cos_r), jnp.asarray(sin_r), decay_rows, ret_gn_w)

    cos_m, sin_m = _rope_tables(MLA_ROPE)
    cos_m = np.concatenate([cos_m, np.ones_like(cos_m)], 1)
    sin_m = np.concatenate([sin_m, np.zeros_like(sin_m)], 1)
    wq = w_uq[0].reshape(MLA_Q_LORA, MLA_HEADS, MLA_DQ)
    wq_pad = jnp.pad(wq, ((0, 0), (0, 0), (0, QK_PAD - MLA_DQ))).reshape(MLA_Q_LORA, MLA_HEADS * QK_PAD).astype(BF16)
    wkv = w_ukv[0].astype(BF16)
    q, k_lat, v_lat = _mla_latent(proj, mla_q_norm, mla_kv_norm, wq_pad, wkv, jnp.asarray(cos_m), jnp.asarray(sin_m))
    k_ctx, v_ctx = _mla_context(proj_c, mla_kv_norm, wkv)
    att = _attention(q, k_ctx, k_lat, v_ctx, v_lat)

    w_route = jnp.concatenate(
        [router_group_w[0], router_expert_w[0].transpose(1, 0, 2).reshape(D_MODEL, N_EXPERTS),
         jnp.zeros((D_MODEL, LANES - N_ROUTE), F32)], 1)
    w_route_hi = w_route.astype(BF16)
    w_route = jnp.concatenate([w_route_hi, (w_route - w_route_hi.astype(F32)).astype(BF16)], 1)
    b_route = jnp.concatenate(
        [router_group_b[0], router_expert_b[0].reshape(N_EXPERTS), jnp.zeros((LANES - N_ROUTE,), F32)])[None]
    h, t_packed, route, counts = _outproj(ret, att, x2d, w_o[0].astype(BF16), mod_rows, ln1_w, ln1_b,
                                          w_route, b_route)

    pos, tile_expert, tile_end, n_tiles = _routing_tables(route, counts)
    xs = _dispatch(pos, tile_end, n_tiles, t_packed)
    ys = _experts(tile_expert, n_tiles, xs, expert_w_gate[0], expert_w_up[0], expert_w_down[0])
    out = _combine(pos, ys, h, route, mod_rows, ln2_w, ln2_b)
    return out.reshape(BATCH, SEQ, D_MODEL)
```

```python
import numpy as np
import jax
import jax.numpy as jnp
from jax import lax
from jax.experimental import pallas as pl
from jax.experimental.pallas import tpu as pltpu

F32 = jnp.float32
BF16 = jnp.bfloat16
I32 = jnp.int32

D_MODEL = 2048
BATCH = 4
SEQ = 2048
GRID_W = 64
CTX_LEN = 256
N_TOK = BATCH * SEQ
N_CTX = BATCH * CTX_LEN

RET_HEADS = 8
RET_DK = 128
RET_DV = 128
RET_W = RET_HEADS * RET_DV
CHUNK = 128
N_CHUNKS = SEQ // CHUNK

MLA_HEADS = 8
MLA_Q_LORA = 512
MLA_KV_LORA = 256
MLA_NOPE = 128
MLA_ROPE = 64
MLA_DV = 128
MLA_W = MLA_HEADS * MLA_DV
MLA_DQ = MLA_NOPE + MLA_ROPE
QK_PAD = 256

IN_SIZES = (RET_HEADS * RET_DK, RET_HEADS * RET_DK, RET_W, RET_W, MLA_Q_LORA, MLA_KV_LORA, MLA_ROPE)
IN_W = sum(IN_SIZES)
OFF_Q, OFF_K, OFF_V, OFF_G, OFF_CQ, OFF_CKV, OFF_KPE = (int(v) for v in np.cumsum((0,) + IN_SIZES[:-1]))

N_GROUPS = 4
EXPERTS_PER_GROUP = 8
N_EXPERTS = N_GROUPS * EXPERTS_PER_GROUP
D_EXPERT = 512
N_ROUTE = N_GROUPS + N_EXPERTS

ROPE_BASE = 10000.0
EPS = 1e-6
DEPTH = 1
DEEPNORM_ALPHA = (2.0 * DEPTH) ** 0.25

LANES = 128
ROW_TILE = 256
N_PAIRS = 2 * N_TOK
MAX_TILES = N_PAIRS // ROW_TILE + N_EXPERTS
MAX_ROWS = MAX_TILES * ROW_TILE
TOKEN_SUB = D_MODEL // LANES

VMEM_LIMIT = 56 * 1024 * 1024


def _params(sem, vmem=VMEM_LIMIT):
    return pltpu.CompilerParams(dimension_semantics=sem, vmem_limit_bytes=vmem)


def _silu(x):
    return x * (1.0 / (1.0 + jnp.exp(-x)))


def _plain_norm(x):
    mu = jnp.mean(x, -1, keepdims=True)
    xc = x - mu
    var = jnp.mean(xc * xc, -1, keepdims=True)
    return xc * lax.rsqrt(var + EPS)


def _to_token_tiles(x):
    return x.astype(BF16).reshape(x.shape[0], TOKEN_SUB, LANES)


def _from_token_tiles(x):
    return x.reshape(x.shape[0], D_MODEL)


ADA_TN = 1024


def _ada_kernel(cc_ref, w_ref, b_ref, o_ref):
    s = _silu(cc_ref[...])
    o_ref[...] = jnp.dot(s, w_ref[...], preferred_element_type=F32,
                         precision=lax.Precision.HIGHEST) + b_ref[...]


def _ada(cc, w_ada, b_ada):
    n = w_ada.shape[1]
    return pl.pallas_call(
        _ada_kernel,
        grid=(n // ADA_TN,),
        in_specs=[
            pl.BlockSpec((8, D_MODEL), lambda j: (0, 0)),
            pl.BlockSpec((D_MODEL, ADA_TN), lambda j: (0, j)),
            pl.BlockSpec((1, ADA_TN), lambda j: (0, j)),
        ],
        out_specs=pl.BlockSpec((8, ADA_TN), lambda j: (0, j)),
        out_shape=jax.ShapeDtypeStruct((8, n), F32),
        compiler_params=_params(("arbitrary",)),
        name="ada",
    )(cc, w_ada, b_ada)


INPROJ_TM = 1024
INPROJ_TN = 1024


def _inproj_kernel(x_ref, shift_ref, scale_ref, w_ref, o_ref, xn_ref):
    @pl.when(pl.program_id(1) == 0)
    def _():
        y = _plain_norm(x_ref[...]) * (1.0 + scale_ref[0]) + shift_ref[0]
        xn_ref[...] = y.astype(BF16)

    o_ref[...] = jnp.dot(xn_ref[...], w_ref[...], preferred_element_type=F32).astype(BF16)


def _inproj(x2d, mod_rows, w_in_bf, rows_per_sample, sample_row0):
    n = x2d.shape[0]
    tiles_per_sample = rows_per_sample // INPROJ_TM

    def mod_map(j):
        return lambda i, k: ((sample_row0 + i // tiles_per_sample) * 6 + j, 0, 0)

    return pl.pallas_call(
        _inproj_kernel,
        grid=(n // INPROJ_TM, pl.cdiv(IN_W, INPROJ_TN)),
        in_specs=[
            pl.BlockSpec((INPROJ_TM, D_MODEL), lambda i, k: (i, 0)),
            pl.BlockSpec((1, 1, D_MODEL), mod_map(0)),
            pl.BlockSpec((1, 1, D_MODEL), mod_map(1)),
            pl.BlockSpec((D_MODEL, INPROJ_TN), lambda i, k: (0, k)),
        ],
        out_specs=pl.BlockSpec((INPROJ_TM, INPROJ_TN), lambda i, k: (i, k)),
        out_shape=jax.ShapeDtypeStruct((n, IN_W), BF16),
        scratch_shapes=[pltpu.VMEM((INPROJ_TM, D_MODEL), BF16)],
        compiler_params=_params(("arbitrary", "arbitrary")),
        name="inproj",
    )(x2d, mod_rows, mod_rows, w_in_bf)


def _rope_tables(width):
    half = width // 2
    quarter = half // 2
    inv_freq = ROPE_BASE ** (-np.arange(0, half, 2, dtype=np.float64) / half)
    t = np.arange(SEQ)
    cos_parts, sin_parts = [], []
    for pos in (t // GRID_W, t % GRID_W):
        ang = pos[:, None].astype(np.float64) * inv_freq[None, :]
        c, s = np.cos(ang), np.sin(ang)
        cos_parts += [c, c]
        sin_parts += [-s, s]
    assert cos_parts[0].shape[1] == quarter
    return (np.concatenate(cos_parts, 1).astype(np.float32), np.concatenate(sin_parts, 1).astype(np.float32))


def _rope(x, cos, sin, quarter):
    lane = lax.broadcasted_iota(I32, x.shape, 1)
    first = (lane % (2 * quarter)) < quarter
    swapped = jnp.where(first, pltpu.roll(x, LANES - quarter, 1), pltpu.roll(x, quarter, 1))
    return x * cos + swapped * sin


def _dot_tn(a, b):
    return lax.dot_general(a, b, (((0,), (0,)), ((), ())), preferred_element_type=F32)


def _dot_nt(a, b):
    return lax.dot_general(a, b, (((1,), (1,)), ((), ())), preferred_element_type=F32)


def _retention_kernel(q_ref, k_ref, v_ref, g_ref, kc_ref, vc_ref, cos_ref, sin_ref, df_ref, db_ref, gn_ref,
                      o_ref, qs_ref, ks_ref, st_ref):
    k_scale = RET_DK ** -0.5
    lgf = jax.nn.log_sigmoid(df_ref[0])
    lgb = jax.nn.log_sigmoid(db_ref[0])

    cos = cos_ref[...]
    sin = sin_ref[...]
    qs_ref[...] = _rope(q_ref[...].astype(F32), cos, sin, RET_DK // 4).astype(BF16)
    ks_ref[...] = _rope(k_ref[...].astype(F32), cos, sin, RET_DK // 4) * k_scale

    rowi = lax.broadcasted_iota(I32, (CHUNK, LANES), 0).astype(F32)
    coli = lax.broadcasted_iota(I32, (CHUNK, LANES), 1).astype(F32)
    diff = rowi - coli
    decay = jnp.exp(jnp.where(diff >= 0, lgf * diff, -lgb * diff)) * jnp.where(diff == 0, 2.0, 1.0)
    zeta_f = jnp.exp(lgf * (CHUNK - 1.0 - rowi))
    eta_b = jnp.exp(lgb * rowi)
    xi_f = jnp.exp(lgf * (rowi + 1.0))
    xi_b = jnp.exp(lgb * (CHUNK - rowi))
    cdec_f = jnp.exp(lgf * float(CHUNK))
    cdec_b = jnp.exp(lgb * float(CHUNK))

    crow = lax.broadcasted_iota(I32, (CTX_LEN, LANES), 0).astype(F32)
    kc = kc_ref[...].astype(F32) * k_scale
    vc = vc_ref[...]
    s_f = _dot_tn((kc * jnp.exp(lgf * (CTX_LEN - 1.0 - crow))).astype(BF16), vc)
    s_b = _dot_tn((kc * jnp.exp(lgb * crow)).astype(BF16), vc)

    upd_f, upd_b = [], []
    for i in range(N_CHUNKS):
        rows = pl.ds(i * CHUNK, CHUNK)
        kch = ks_ref[rows, :]
        vch = v_ref[rows, :]
        upd_f.append(_dot_tn((kch * zeta_f).astype(BF16), vch))
        upd_b.append(_dot_tn((kch * eta_b).astype(BF16), vch))
    state = s_f
    for i in range(N_CHUNKS):
        st_ref[i, :, :RET_DV] = state.astype(BF16)
        state = cdec_f * state + upd_f[i]
    state = s_b
    for i in reversed(range(N_CHUNKS)):
        st_ref[i, :, RET_DV:] = state.astype(BF16)
        state = cdec_b * state + upd_b[i]

    gn_w = gn_ref[...]
    for i in range(N_CHUNKS):
        rows = pl.ds(i * CHUNK, CHUNK)
        qch = qs_ref[rows, :]
        scores = _dot_nt(qch, ks_ref[rows, :].astype(BF16)) * decay
        o = jnp.dot(scores.astype(BF16), v_ref[rows, :], preferred_element_type=F32)
        cross = jnp.dot(qch, st_ref[i], preferred_element_type=F32)
        o = o + xi_f * cross[:, :RET_DV] + xi_b * cross[:, RET_DV:]
        y = _plain_norm(o) * gn_w
        o_ref[rows, :] = (_silu(g_ref[rows, :].astype(F32)) * y).astype(BF16)


def _retention(proj, proj_c, cos, sin, decay_rows, gn_w):
    blk = lambda off: pl.BlockSpec((SEQ, LANES), lambda b, h: (b, off // LANES + h))
    blk_c = lambda off: pl.BlockSpec((CTX_LEN, LANES), lambda b, h: (b, off // LANES + h))
    table = pl.BlockSpec((SEQ, LANES), lambda b, h: (0, 0))
    return pl.pallas_call(
        _retention_kernel,
        grid=(BATCH, RET_HEADS),
        in_specs=[
            blk(OFF_Q), blk(OFF_K), blk(OFF_V), blk(OFF_G), blk_c(OFF_K), blk_c(OFF_V), table, table,
            pl.BlockSpec((1, 1, LANES), lambda b, h: (h, 0, 0)),
            pl.BlockSpec((1, 1, LANES), lambda b, h: (RET_HEADS + h, 0, 0)),
            pl.BlockSpec((1, LANES), lambda b, h: (0, h)),
        ],
        out_specs=pl.BlockSpec((SEQ, LANES), lambda b, h: (b, h)),
        out_shape=jax.ShapeDtypeStruct((N_TOK, RET_W), BF16),
        scratch_shapes=[
            pltpu.VMEM((SEQ, RET_DK), BF16),
            pltpu.VMEM((SEQ, RET_DK), F32),
            pltpu.VMEM((N_CHUNKS, RET_DK, 2 * RET_DV), BF16),
        ],
        compiler_params=_params(("arbitrary", "arbitrary")),
        name="retention",
    )(proj, proj, proj, proj, proj_c, proj_c, cos, sin, decay_rows, decay_rows, gn_w)


MLA_TM = 512


def _rms_norm(x, w):
    return x * lax.rsqrt(jnp.mean(x * x, -1, keepdims=True) + EPS) * w


def _mla_kv(ckv_ref, kpe_ref, kvn_ref, wkv_ref, cos_ref, sin_ref, k_ref, v_ref, rotate):
    ckv = _rms_norm(ckv_ref[...].astype(F32), kvn_ref[...]).astype(BF16)
    kv = jnp.dot(ckv, wkv_ref[...], preferred_element_type=F32)
    lane = lax.broadcasted_iota(I32, (ckv.shape[0], LANES), 1)
    kpe = jnp.where(lane < MLA_ROPE, kpe_ref[...].astype(F32), 0.0)
    if rotate:
        kpe = _rope(kpe, cos_ref[...], sin_ref[...], MLA_ROPE // 4)
    kpe = kpe.astype(BF16)
    for h in range(MLA_HEADS):
        k_ref[:, h * QK_PAD:h * QK_PAD + MLA_NOPE] = kv[:, 2 * h * LANES:(2 * h + 1) * LANES].astype(BF16)
        k_ref[:, h * QK_PAD + MLA_NOPE:(h + 1) * QK_PAD] = kpe
        v_ref[:, h * MLA_DV:(h + 1) * MLA_DV] = kv[:, (2 * h + 1) * LANES:(2 * h + 2) * LANES].astype(BF16)


def _mla_latent_kernel(cq_ref, ckv_ref, kpe_ref, qn_ref, kvn_ref, wq_ref, wkv_ref, cos_ref, sin_ref,
                       q_ref, k_ref, v_ref):
    cq = _rms_norm(cq_ref[...].astype(F32), qn_ref[...]).astype(BF16)
    q = jnp.dot(cq, wq_ref[...], preferred_element_type=F32)
    cos = cos_ref[...]
    sin = sin_ref[...]
    scale = MLA_DQ ** -0.5
    for h in range(MLA_HEADS):
        lo = h * QK_PAD
        q_ref[:, lo:lo + MLA_NOPE] = (q[:, lo:lo + MLA_NOPE] * scale).astype(BF16)
        qpe = _rope(q[:, lo + MLA_NOPE:lo + QK_PAD], cos, sin, MLA_ROPE // 4)
        q_ref[:, lo + MLA_NOPE:lo + QK_PAD] = (qpe * scale).astype(BF16)
    _mla_kv(ckv_ref, kpe_ref, kvn_ref, wkv_ref, cos_ref, sin_ref, k_ref, v_ref, rotate=True)


def _mla_context_kernel(ckv_ref, kpe_ref, kvn_ref, wkv_ref, k_ref, v_ref):
    _mla_kv(ckv_ref, kpe_ref, kvn_ref, wkv_ref, None, None, k_ref, v_ref, rotate=False)


def _mla_latent(proj, q_norm, kv_norm, wq_pad, wkv, cos, sin):
    row = lambda w, off: pl.BlockSpec((MLA_TM, w), lambda i: (i, off // w))
    full = lambda a: pl.BlockSpec(a.shape, lambda i: (0, 0))
    table = pl.BlockSpec((MLA_TM, LANES), lambda i: (i % (SEQ // MLA_TM), 0))
    return pl.pallas_call(
        _mla_latent_kernel,
        grid=(N_TOK // MLA_TM,),
        in_specs=[row(MLA_Q_LORA, OFF_CQ), row(MLA_KV_LORA, OFF_CKV), row(LANES, OFF_KPE),
                  full(q_norm), full(kv_norm), full(wq_pad), full(wkv), table, table],
        out_specs=[
            pl.BlockSpec((MLA_TM, MLA_HEADS * QK_PAD), lambda i: (i, 0)),
            pl.BlockSpec((MLA_TM, MLA_HEADS * QK_PAD), lambda i: (i, 0)),
            pl.BlockSpec((MLA_TM, MLA_W), lambda i: (i, 0)),
        ],
        out_shape=[
            jax.ShapeDtypeStruct((N_TOK, MLA_HEADS * QK_PAD), BF16),
            jax.ShapeDtypeStruct((N_TOK, MLA_HEADS * QK_PAD), BF16),
            jax.ShapeDtypeStruct((N_TOK, MLA_W), BF16),
        ],
        compiler_params=_params(("arbitrary",)),
        name="mla_latent",
    )(proj, proj, proj, q_norm, kv_norm, wq_pad, wkv, cos, sin)


def _mla_context(proj_c, kv_norm, wkv):
    row = lambda w, off: pl.BlockSpec((MLA_TM, w), lambda i: (i, off // w))
    full = lambda a: pl.BlockSpec(a.shape, lambda i: (0, 0))
    return pl.pallas_call(
        _mla_context_kernel,
        grid=(N_CTX // MLA_TM,),
        in_specs=[row(MLA_KV_LORA, OFF_CKV), row(LANES, OFF_KPE), full(kv_norm), full(wkv)],
        out_specs=[
            pl.BlockSpec((MLA_TM, MLA_HEADS * QK_PAD), lambda i: (i, 0)),
            pl.BlockSpec((MLA_TM, MLA_W), lambda i: (i, 0)),
        ],
        out_shape=[
            jax.ShapeDtypeStruct((N_CTX, MLA_HEADS * QK_PAD), BF16),
            jax.ShapeDtypeStruct((N_CTX, MLA_W), BF16),
        ],
        compiler_params=_params(("arbitrary",)),
        name="mla_context",
    )(proj_c, proj_c, kv_norm, wkv)


ATT_TQ = 256


def _attention_kernel(q_ref, kc_ref, kl_ref, vc_ref, vl_ref, o_ref):
    for h in range(MLA_HEADS):
        qk = slice(h * QK_PAD, (h + 1) * QK_PAD)
        dv = slice(h * MLA_DV, (h + 1) * MLA_DV)
        q = q_ref[:, qk]
        s_c = _dot_nt(q, kc_ref[:, qk])
        s_l = _dot_nt(q, kl_ref[:, qk])
        m = jnp.maximum(jnp.max(s_c, -1, keepdims=True), jnp.max(s_l, -1, keepdims=True))
        p_c = jnp.exp(s_c - m)
        p_l = jnp.exp(s_l - m)
        denom = jnp.sum(p_c, -1, keepdims=True) + jnp.sum(p_l, -1, keepdims=True)
        o = (jnp.dot(p_c.astype(BF16), vc_ref[:, dv], preferred_element_type=F32)
             + jnp.dot(p_l.astype(BF16), vl_ref[:, dv], preferred_element_type=F32))
        o_ref[:, dv] = (o / denom).astype(BF16)


def _attention(q, k_ctx, k_lat, v_ctx, v_lat):
    tiles = SEQ // ATT_TQ
    return pl.pallas_call(
        _attention_kernel,
        grid=(BATCH, tiles),
        in_specs=[
            pl.BlockSpec((ATT_TQ, MLA_HEADS * QK_PAD), lambda b, i: (b * tiles + i, 0)),
            pl.BlockSpec((CTX_LEN, MLA_HEADS * QK_PAD), lambda b, i: (b, 0)),
            pl.BlockSpec((SEQ, MLA_HEADS * QK_PAD), lambda b, i: (b, 0)),
            pl.BlockSpec((CTX_LEN, MLA_W), lambda b, i: (b, 0)),
            pl.BlockSpec((SEQ, MLA_W), lambda b, i: (b, 0)),
        ],
        out_specs=pl.BlockSpec((ATT_TQ, MLA_W), lambda b, i: (b * tiles + i, 0)),
        out_shape=jax.ShapeDtypeStruct((N_TOK, MLA_W), BF16),
        compiler_params=_params(("arbitrary", "arbitrary")),
        name="attention",
    )(q, k_ctx, k_lat, v_ctx, v_lat)


OUT_TM = 256
ROUTE_E1, ROUTE_E2, ROUTE_R1, ROUTE_R2, ROUTE_W1, ROUTE_W2 = range(6)


def _outproj_kernel(ret_ref, att_ref, x_ref, wo_ref, gate_ref, shift_ref, scale_ref, lnw_ref, lnb_ref,
                    wr_ref, br_ref, h_ref, t_ref, route_ref, count_ref, carry_ref):
    @pl.when(pl.program_id(0) == 0)
    def _():
        carry_ref[...] = jnp.zeros_like(carry_ref)

    mix = (jnp.dot(ret_ref[...], wo_ref[:RET_W, :], preferred_element_type=F32)
           + jnp.dot(att_ref[...], wo_ref[RET_W:, :], preferred_element_type=F32))
    h = _plain_norm(DEEPNORM_ALPHA * x_ref[...] + gate_ref[0] * mix) * lnw_ref[...] + lnb_ref[...]
    h_ref[...] = h
    t = _plain_norm(h) * (1.0 + scale_ref[0]) + shift_ref[0]
    t_ref[...] = _to_token_tiles(t)

    t_hi = t.astype(BF16)
    t_lo = (t - t_hi.astype(F32)).astype(BF16)
    main = jnp.dot(t_hi, wr_ref[...], preferred_element_type=F32)
    corr = jnp.dot(t_lo, wr_ref[:, :LANES], preferred_element_type=F32)
    logits = main[:, :LANES] + (main[:, LANES:] + corr) + br_ref[...]
    lane = lax.broadcasted_iota(I32, logits.shape, 1).astype(F32)
    neg = -jnp.inf
    big = float(LANES)

    def first_lane_of(mask):
        return jnp.min(jnp.where(mask, lane, big), -1, keepdims=True)

    is_group = lane < N_GROUPS
    gl = jnp.where(is_group, logits, neg)
    g_max = jnp.max(gl, -1, keepdims=True)
    g_idx = first_lane_of(is_group & (gl == g_max))
    g_prob = 1.0 / jnp.sum(jnp.where(is_group, jnp.exp(logits - g_max), 0.0), -1, keepdims=True)

    lo = N_GROUPS + g_idx * EXPERTS_PER_GROUP
    in_group = (lane >= lo) & (lane < lo + EXPERTS_PER_GROUP)
    el = jnp.where(in_group, logits, neg)
    v1 = jnp.max(el, -1, keepdims=True)
    i1 = first_lane_of(in_group & (el == v1))
    rest = in_group & (lane != i1)
    el2 = jnp.where(rest, logits, neg)
    v2 = jnp.max(el2, -1, keepdims=True)
    i2 = first_lane_of(rest & (el2 == v2))
    d = jnp.exp(v2 - v1)
    w1 = g_prob / (1.0 + d)
    w2 = g_prob * d / (1.0 + d)

    onehot = jnp.where((lane == i1) | (lane == i2), 1.0, 0.0)
    r = lax.broadcasted_iota(I32, (OUT_TM, OUT_TM), 0)
    c = lax.broadcasted_iota(I32, (OUT_TM, OUT_TM), 1)
    tri = jnp.where(c < r, 1.0, 0.0).astype(BF16)
    before = jnp.dot(tri, onehot.astype(BF16), preferred_element_type=F32) + carry_ref[0:1, :]
    r1 = jnp.sum(jnp.where(lane == i1, before, 0.0), -1, keepdims=True)
    r2 = jnp.sum(jnp.where(lane == i2, before, 0.0), -1, keepdims=True)
    carry_ref[...] = carry_ref[...] + jnp.sum(onehot, 0, keepdims=True)
    count_ref[...] = carry_ref[...]

    rec = jnp.zeros_like(logits)
    for slot, val in ((ROUTE_E1, i1 - N_GROUPS), (ROUTE_E2, i2 - N_GROUPS), (ROUTE_R1, r1), (ROUTE_R2, r2),
                      (ROUTE_W1, w1), (ROUTE_W2, w2)):
        rec = jnp.where(lane == slot, val, rec)
    route_ref[...] = rec


def _outproj(ret, att, x2d, wo_bf, mod_rows, ln_w, ln_b, w_route, b_route):
    tiles_per_sample = SEQ // OUT_TM
    row = lambda w: pl.BlockSpec((OUT_TM, w), lambda i: (i, 0))
    full = lambda a: pl.BlockSpec(a.shape, lambda i: (0, 0))
    mod = lambda j: pl.BlockSpec((1, 1, D_MODEL), lambda i: ((i // tiles_per_sample) * 6 + j, 0, 0))
    return pl.pallas_call(
        _outproj_kernel,
        grid=(N_TOK // OUT_TM,),
        in_specs=[row(RET_W), row(MLA_W), row(D_MODEL), full(wo_bf), mod(2), mod(3), mod(4),
                  full(ln_w), full(ln_b), full(w_route), full(b_route)],
        out_specs=[row(D_MODEL), pl.BlockSpec((OUT_TM, TOKEN_SUB, LANES), lambda i: (i, 0, 0)), row(LANES),
                   pl.BlockSpec((8, LANES), lambda i: (0, 0))],
        out_shape=[
            jax.ShapeDtypeStruct((N_TOK, D_MODEL), F32),
            jax.ShapeDtypeStruct((N_TOK, TOKEN_SUB, LANES), BF16),
            jax.ShapeDtypeStruct((N_TOK, LANES), F32),
            jax.ShapeDtypeStruct((8, LANES), F32),
        ],
        scratch_shapes=[pltpu.VMEM((8, LANES), F32)],
        compiler_params=_params(("arbitrary",)),
        name="outproj_route",
    )(ret, att, x2d, wo_bf, mod_rows, mod_rows, mod_rows, ln_w, ln_b, w_route, b_route)


DISPATCH_TM = 256


def _dispatch_kernel(pos_ref, tend_ref, nt_ref, t_ref, xs_ref, zero_ref, sem, zsem):
    i = pl.program_id(0)

    @pl.when(i == 0)
    def _():
        zero_ref[...] = jnp.zeros_like(zero_ref)

        def zero_tile(tile):
            return pltpu.make_async_copy(zero_ref, xs_ref.at[pl.ds(tile * ROW_TILE, ROW_TILE)], zsem)

        def has_tiles(e):
            return tend_ref[e] > (tend_ref[e - 1] if e else 0)

        for e in range(N_EXPERTS):
            pl.when(has_tiles(e))(lambda e=e: zero_tile(tend_ref[e] - 1).start())
        lax.fori_loop(nt_ref[0], MAX_TILES, lambda j, c: (zero_tile(j).start(), c)[1], 0)
        for e in range(N_EXPERTS):
            pl.when(has_tiles(e))(lambda e=e: zero_tile(tend_ref[e] - 1).wait())
        lax.fori_loop(nt_ref[0], MAX_TILES, lambda j, c: (zero_tile(j).wait(), c)[1], 0)

    def copy(j):
        return pltpu.make_async_copy(t_ref.at[j // 2], xs_ref.at[pos_ref[i * (2 * DISPATCH_TM) + j]], sem)

    lax.fori_loop(0, 2 * DISPATCH_TM, lambda j, c: (copy(j).start(), c)[1], 0, unroll=8)
    lax.fori_loop(0, 2 * DISPATCH_TM, lambda j, c: (copy(j).wait(), c)[1], 0, unroll=8)


def _dispatch(pos, tile_end, n_tiles, t_tiles):
    return pl.pallas_call(
        _dispatch_kernel,
        grid_spec=pltpu.PrefetchScalarGridSpec(
            num_scalar_prefetch=3,
            grid=(N_TOK // DISPATCH_TM,),
            in_specs=[pl.BlockSpec((DISPATCH_TM, TOKEN_SUB, LANES), lambda i, pos, te, nt: (i, 0, 0))],
            out_specs=pl.BlockSpec(memory_space=pl.ANY),
            scratch_shapes=[pltpu.VMEM((ROW_TILE, TOKEN_SUB, LANES), BF16), pltpu.SemaphoreType.DMA(()),
                            pltpu.SemaphoreType.DMA(())],
        ),
        out_shape=jax.ShapeDtypeStruct((MAX_ROWS, TOKEN_SUB, LANES), BF16),
        compiler_params=_params(("arbitrary",)),
        name="dispatch",
    )(pos, tile_end, n_tiles, t_tiles)


def _experts_kernel(tend_ref, nt_ref, xs_ref, wg_ref, wu_ref, wd_ref, ys_ref, wgu_ref, wdn_ref, xbuf_ref, ybuf_ref,
                    xsem, ysem):
    e = pl.program_id(0)
    n_total = nt_ref[0]
    first = jnp.where(e == 0, 0, tend_ref[jnp.maximum(e - 1, 0)])
    count = tend_ref[e] - first

    def x_copy(g):
        return pltpu.make_async_copy(xs_ref.at[pl.ds(g * ROW_TILE, ROW_TILE)], xbuf_ref.at[g % 2], xsem.at[g % 2])

    def y_copy(g):
        return pltpu.make_async_copy(ybuf_ref.at[g % 2], ys_ref.at[pl.ds(g * ROW_TILE, ROW_TILE)], ysem.at[g % 2])

    @pl.when(e == 0)
    def _():
        x_copy(0).start()

    @pl.when(count > 0)
    def _():
        wgu_ref[:, :D_EXPERT] = wg_ref[0].astype(BF16)
        wgu_ref[:, D_EXPERT:] = wu_ref[0].astype(BF16)
        wdn_ref[...] = wd_ref[0].astype(BF16)

        def tile(j, c):
            g = first + j

            @pl.when(g + 1 < n_total)
            def _():
                x_copy(g + 1).start()

            x_copy(g).wait()
            gu = jnp.dot(_from_token_tiles(xbuf_ref[g % 2]), wgu_ref[...], preferred_element_type=F32)
            hid = _silu(gu[:, :D_EXPERT]) * gu[:, D_EXPERT:]
            y = jnp.dot(hid.astype(BF16), wdn_ref[...], preferred_element_type=F32)

            @pl.when(g >= 2)
            def _():
                y_copy(g - 2).wait()

            ybuf_ref[g % 2] = _to_token_tiles(y)
            y_copy(g).start()
            return c

        lax.fori_loop(0, count, tile, 0)

    @pl.when(e == N_EXPERTS - 1)
    def _():
        @pl.when(n_total >= 2)
        def _():
            y_copy(n_total - 2).wait()

        y_copy(n_total - 1).wait()
        ybuf_ref[0] = jnp.zeros(ybuf_ref.shape[1:], BF16)

        def zero_copy(g):
            return pltpu.make_async_copy(ybuf_ref.at[0], ys_ref.at[pl.ds(g * ROW_TILE, ROW_TILE)], ysem.at[0])

        lax.fori_loop(n_total, MAX_TILES, lambda g, c: (zero_copy(g).start(), c)[1], 0)
        lax.fori_loop(n_total, MAX_TILES, lambda g, c: (zero_copy(g).wait(), c)[1], 0)


def _experts(tile_end, n_tiles, xs, w_gate, w_up, w_down):
    w_gate = w_gate.reshape(N_EXPERTS, D_MODEL, D_EXPERT)
    w_up = w_up.reshape(N_EXPERTS, D_MODEL, D_EXPERT)
    w_down = w_down.reshape(N_EXPERTS, D_EXPERT, D_MODEL)
    expert = lambda e, tend, nt: (e, 0, 0)
    tile_buf = pltpu.VMEM((2, ROW_TILE, TOKEN_SUB, LANES), BF16)
    return pl.pallas_call(
        _experts_kernel,
        grid_spec=pltpu.PrefetchScalarGridSpec(
            num_scalar_prefetch=2,
            grid=(N_EXPERTS,),
            in_specs=[
                pl.BlockSpec(memory_space=pl.ANY),
                pl.BlockSpec((1, D_MODEL, D_EXPERT), expert),
                pl.BlockSpec((1, D_MODEL, D_EXPERT), expert),
                pl.BlockSpec((1, D_EXPERT, D_MODEL), expert),
            ],
            out_specs=pl.BlockSpec(memory_space=pl.ANY),
            scratch_shapes=[pltpu.VMEM((D_MODEL, 2 * D_EXPERT), BF16), pltpu.VMEM((D_EXPERT, D_MODEL), BF16),
                            tile_buf, tile_buf, pltpu.SemaphoreType.DMA((2,)), pltpu.SemaphoreType.DMA((2,))],
        ),
        out_shape=jax.ShapeDtypeStruct((MAX_ROWS, TOKEN_SUB, LANES), BF16),
        compiler_params=_params(("arbitrary",)),
        name="experts",
    )(tile_end, n_tiles, xs, w_gate, w_up, w_down)


COMB_TM = 256


def _combine_kernel(pos_ref, ys_ref, h_ref, route_ref, gate_ref, lnw_ref, lnb_ref, o_ref, buf_ref, sem):
    i = pl.program_id(0)
    n = pl.num_programs(0)

    def copy(step, slot, j):
        src = pos_ref[step * (2 * COMB_TM) + j]
        dst = (j % 2) * COMB_TM + j // 2
        return pltpu.make_async_copy(ys_ref.at[src], buf_ref.at[slot, dst], sem.at[slot])

    def start_all(step, slot):
        lax.fori_loop(0, 2 * COMB_TM, lambda j, c: (copy(step, slot, j).start(), c)[1], 0, unroll=8)

    def wait_all(step, slot):
        lax.fori_loop(0, 2 * COMB_TM, lambda j, c: (copy(step, slot, j).wait(), c)[1], 0, unroll=8)

    @pl.when(i == 0)
    def _():
        start_all(0, 0)

    @pl.when(i + 1 < n)
    def _():
        start_all(i + 1, (i + 1) % 2)

    slot = i % 2
    wait_all(i, slot)

    y1 = _from_token_tiles(buf_ref[slot, :COMB_TM]).astype(F32)
    y2 = _from_token_tiles(buf_ref[slot, COMB_TM:]).astype(F32)
    y = route_ref[:, ROUTE_W1:ROUTE_W1 + 1] * y1 + route_ref[:, ROUTE_W2:ROUTE_W2 + 1] * y2
    z = DEEPNORM_ALPHA * h_ref[...] + gate_ref[0] * y
    o_ref[...] = _plain_norm(z) * lnw_ref[...] + lnb_ref[...]


def _combine(pos, ys, h, route, mod_rows, ln_w, ln_b):
    tiles_per_sample = SEQ // COMB_TM
    row = lambda w: pl.BlockSpec((COMB_TM, w), lambda i, pos: (i, 0))
    full = lambda a: pl.BlockSpec(a.shape, lambda i, pos: (0, 0))
    return pl.pallas_call(
        _combine_kernel,
        grid_spec=pltpu.PrefetchScalarGridSpec(
            num_scalar_prefetch=1,
            grid=(N_TOK // COMB_TM,),
            in_specs=[
                pl.BlockSpec(memory_space=pl.ANY), row(D_MODEL), row(LANES),
                pl.BlockSpec((1, 1, D_MODEL), lambda i, pos: ((i // tiles_per_sample) * 6 + 5, 0, 0)),
                full(ln_w), full(ln_b),
            ],
            out_specs=row(D_MODEL),
            scratch_shapes=[pltpu.VMEM((2, 2 * COMB_TM, TOKEN_SUB, LANES), BF16), pltpu.SemaphoreType.DMA((2,))],
        ),
        out_shape=jax.ShapeDtypeStruct((N_TOK, D_MODEL), F32),
        compiler_params=_params(("arbitrary",)),
        name="combine",
    )(pos, ys, h, route, mod_rows, ln_w, ln_b)


def _routing_tables(route, counts):
    cnt = counts[0, N_GROUPS:N_ROUTE].astype(I32)
    tiles = (cnt + ROW_TILE - 1) // ROW_TILE
    tile_end = jnp.cumsum(tiles)
    tile_start = tile_end - tiles
    n_tiles = tile_end[-1:]
    e = route[:, ROUTE_E1:ROUTE_E2 + 1].astype(I32)
    rank = route[:, ROUTE_R1:ROUTE_R2 + 1].astype(I32)
    pos = (tile_start[e] * ROW_TILE + rank).reshape(-1)
    return pos, tile_end, n_tiles


def kernel(x, c, ctx, c_ctx, w_ada, b_ada, w_in, ret_decay, ret_gn_w, mla_q_norm, mla_kv_norm, w_uq, w_ukv, w_o,
           ln1_w, ln1_b, router_group_w, router_group_b, router_expert_w, router_expert_b, expert_w_gate,
           expert_w_up, expert_w_down, ln2_w, ln2_b):
    x2d = x.reshape(N_TOK, D_MODEL)
    ctx2d = ctx.reshape(N_CTX, D_MODEL)

    cc = jnp.zeros((8, D_MODEL), F32).at[:BATCH].set(c).at[BATCH].set(c_ctx)
    mod = _ada(cc, w_ada[0], b_ada)
    mod_rows = mod.reshape(8 * 6, 1, D_MODEL)

    w_in_bf = w_in[0].astype(BF16)
    proj = _inproj(x2d, mod_rows, w_in_bf, SEQ, 0)
    proj_c = _inproj(ctx2d, mod_rows, w_in_bf, N_CTX, BATCH)

    cos_r, sin_r = _rope_tables(RET_DK)
    decay_rows = jnp.broadcast_to(ret_decay[0].reshape(2 * RET_HEADS, 1, 1), (2 * RET_HEADS, 1, LANES))
    ret = _retention(proj, proj_c, jnp.asarray(cos_r), jnp.asarray(sin_r), decay_rows, ret_gn_w)

    cos_m, sin_m = _rope_tables(MLA_ROPE)
    cos_m = np.concatenate([cos_m, np.ones_like(cos_m)], 1)
    sin_m = np.concatenate([sin_m, np.zeros_like(sin_m)], 1)
    wq = w_uq[0].reshape(MLA_Q_LORA, MLA_HEADS, MLA_DQ)
    wq_pad = jnp.pad(wq, ((0, 0), (0, 0), (0, QK_PAD - MLA_DQ))).reshape(MLA_Q_LORA, MLA_HEADS * QK_PAD).astype(BF16)
    wkv = w_ukv[0].astype(BF16)
    q, k_lat, v_lat = _mla_latent(proj, mla_q_norm, mla_kv_norm, wq_pad, wkv, jnp.asarray(cos_m), jnp.asarray(sin_m))
    k_ctx, v_ctx = _mla_context(proj_c, mla_kv_norm, wkv)
    att = _attention(q, k_ctx, k_lat, v_ctx, v_lat)

    w_route = jnp.concatenate(
        [router_group_w[0], router_expert_w[0].transpose(1, 0, 2).reshape(D_MODEL, N_EXPERTS),
         jnp.zeros((D_MODEL, LANES - N_ROUTE), F32)], 1)
    w_route_hi = w_route.astype(BF16)
    w_route = jnp.concatenate([w_route_hi, (w_route - w_route_hi.astype(F32)).astype(BF16)], 1)
    b_route = jnp.concatenate(
        [router_group_b[0], router_expert_b[0].reshape(N_EXPERTS), jnp.zeros((LANES - N_ROUTE,), F32)])[None]
    h, t_tiles, route, counts = _outproj(ret, att, x2d, w_o[0].astype(BF16), mod_rows, ln1_w, ln1_b,
                                          w_route, b_route)

    pos, tile_end, n_tiles = _routing_tables(route, counts)
    xs = _dispatch(pos, tile_end, n_tiles, t_tiles)
    ys = _experts(tile_end, n_tiles, xs, expert_w_gate[0], expert_w_up[0], expert_w_down[0])
    out = _combine(pos, ys, h, route, mod_rows, ln2_w, ln2_b)
    return out.reshape(BATCH, SEQ, D_MODEL)
```

```python
import numpy as np
import jax
import jax.numpy as jnp
from jax import lax
from jax.experimental import pallas as pl
from jax.experimental.pallas import tpu as pltpu

F32 = jnp.float32
BF16 = jnp.bfloat16
I32 = jnp.int32

D_MODEL = 2048
BATCH = 4
SEQ = 2048
GRID_W = 64
CTX_LEN = 256
N_TOK = BATCH * SEQ
N_CTX = BATCH * CTX_LEN

RET_HEADS = 8
RET_DK = 128
RET_DV = 128
RET_W = RET_HEADS * RET_DV
CHUNK = 128
N_CHUNKS = SEQ // CHUNK

MLA_HEADS = 8
MLA_Q_LORA = 512
MLA_KV_LORA = 256
MLA_NOPE = 128
MLA_ROPE = 64
MLA_DV = 128
MLA_W = MLA_HEADS * MLA_DV
MLA_DQ = MLA_NOPE + MLA_ROPE
QK_PAD = 256

IN_SIZES = (RET_HEADS * RET_DK, RET_HEADS * RET_DK, RET_W, RET_W, MLA_Q_LORA, MLA_KV_LORA, MLA_ROPE)
IN_W = sum(IN_SIZES)
OFF_Q, OFF_K, OFF_V, OFF_G, OFF_CQ, OFF_CKV, OFF_KPE = (int(v) for v in np.cumsum((0,) + IN_SIZES[:-1]))

N_GROUPS = 4
EXPERTS_PER_GROUP = 8
N_EXPERTS = N_GROUPS * EXPERTS_PER_GROUP
D_EXPERT = 512
N_ROUTE = N_GROUPS + N_EXPERTS

ROPE_BASE = 10000.0
EPS = 1e-6
DEPTH = 1
DEEPNORM_ALPHA = (2.0 * DEPTH) ** 0.25

LANES = 128
ROW_TILE = 256
N_PAIRS = 2 * N_TOK
MAX_TILES = N_PAIRS // ROW_TILE + N_EXPERTS
MAX_ROWS = MAX_TILES * ROW_TILE
TOKEN_SUB = D_MODEL // LANES

VMEM_LIMIT = 56 * 1024 * 1024


def _params(sem, vmem=VMEM_LIMIT):
    return pltpu.CompilerParams(dimension_semantics=sem, vmem_limit_bytes=vmem)


def _silu(x):
    return x * (1.0 / (1.0 + jnp.exp(-x)))


def _plain_norm(x):
    mu = jnp.mean(x, -1, keepdims=True)
    xc = x - mu
    var = jnp.mean(xc * xc, -1, keepdims=True)
    return xc * lax.rsqrt(var + EPS)


def _to_token_tiles(x):
    return x.astype(BF16).reshape(x.shape[0], TOKEN_SUB, LANES)


def _from_token_tiles(x):
    return x.reshape(x.shape[0], D_MODEL)


ADA_TN = 2048


def _ada_kernel(cc_ref, w_ref, b_ref, o_ref):
    s = _silu(cc_ref[...])
    o_ref[...] = jnp.dot(s, w_ref[...], preferred_element_type=F32,
                         precision=lax.Precision.HIGHEST) + b_ref[...]


def _ada(cc, w_ada, b_ada):
    n = w_ada.shape[1]
    return pl.pallas_call(
        _ada_kernel,
        grid=(n // ADA_TN,),
        in_specs=[
            pl.BlockSpec((8, D_MODEL), lambda j: (0, 0)),
            pl.BlockSpec((D_MODEL, ADA_TN), lambda j: (0, j)),
            pl.BlockSpec((1, ADA_TN), lambda j: (0, j)),
        ],
        out_specs=pl.BlockSpec((8, ADA_TN), lambda j: (0, j)),
        out_shape=jax.ShapeDtypeStruct((8, n), F32),
        compiler_params=_params(("arbitrary",)),
        name="ada",
    )(cc, w_ada, b_ada)


INPROJ_TM = 1024
INPROJ_TN = 1024


def _inproj_kernel(x_ref, shift_ref, scale_ref, w_ref, o_ref, xn_ref):
    @pl.when(pl.program_id(1) == 0)
    def _():
        y = _plain_norm(x_ref[...]) * (1.0 + scale_ref[0]) + shift_ref[0]
        xn_ref[...] = y.astype(BF16)

    o_ref[...] = jnp.dot(xn_ref[...], w_ref[...], preferred_element_type=F32).astype(BF16)


def _inproj(x2d, mod_rows, w_in_bf, rows_per_sample, sample_row0):
    n = x2d.shape[0]
    tiles_per_sample = rows_per_sample // INPROJ_TM

    def mod_map(j):
        return lambda i, k: ((sample_row0 + i // tiles_per_sample) * 6 + j, 0, 0)

    return pl.pallas_call(
        _inproj_kernel,
        grid=(n // INPROJ_TM, pl.cdiv(IN_W, INPROJ_TN)),
        in_specs=[
            pl.BlockSpec((INPROJ_TM, D_MODEL), lambda i, k: (i, 0)),
            pl.BlockSpec((1, 1, D_MODEL), mod_map(0)),
            pl.BlockSpec((1, 1, D_MODEL), mod_map(1)),
            pl.BlockSpec((D_MODEL, INPROJ_TN), lambda i, k: (0, k)),
        ],
        out_specs=pl.BlockSpec((INPROJ_TM, INPROJ_TN), lambda i, k: (i, k)),
        out_shape=jax.ShapeDtypeStruct((n, IN_W), BF16),
        scratch_shapes=[pltpu.VMEM((INPROJ_TM, D_MODEL), BF16)],
        compiler_params=_params(("arbitrary", "arbitrary")),
        name="inproj",
    )(x2d, mod_rows, mod_rows, w_in_bf)


def _rope_tables(width):
    half = width // 2
    quarter = half // 2
    inv_freq = ROPE_BASE ** (-np.arange(0, half, 2, dtype=np.float64) / half)
    t = np.arange(SEQ)
    cos_parts, sin_parts = [], []
    for pos in (t // GRID_W, t % GRID_W):
        ang = pos[:, None].astype(np.float64) * inv_freq[None, :]
        c, s = np.cos(ang), np.sin(ang)
        cos_parts += [c, c]
        sin_parts += [-s, s]
    assert cos_parts[0].shape[1] == quarter
    return (np.concatenate(cos_parts, 1).astype(np.float32), np.concatenate(sin_parts, 1).astype(np.float32))


def _rope(x, cos, sin, quarter):
    lane = lax.broadcasted_iota(I32, x.shape, 1)
    first = (lane % (2 * quarter)) < quarter
    swapped = jnp.where(first, pltpu.roll(x, LANES - quarter, 1), pltpu.roll(x, quarter, 1))
    return x * cos + swapped * sin


def _dot_tn(a, b):
    return lax.dot_general(a, b, (((0,), (0,)), ((), ())), preferred_element_type=F32)


def _dot_nt(a, b):
    return lax.dot_general(a, b, (((1,), (1,)), ((), ())), preferred_element_type=F32)


def _retention_kernel(q_ref, k_ref, v_ref, g_ref, kc_ref, vc_ref, cos_ref, sin_ref, df_ref, db_ref, gn_ref,
                      o_ref, qs_ref, ks_ref, st_ref):
    k_scale = RET_DK ** -0.5
    lgf = jax.nn.log_sigmoid(df_ref[0])
    lgb = jax.nn.log_sigmoid(db_ref[0])

    cos = cos_ref[...]
    sin = sin_ref[...]
    qs_ref[...] = _rope(q_ref[...].astype(F32), cos, sin, RET_DK // 4).astype(BF16)
    ks_ref[...] = _rope(k_ref[...].astype(F32), cos, sin, RET_DK // 4) * k_scale

    rowi = lax.broadcasted_iota(I32, (CHUNK, LANES), 0).astype(F32)
    coli = lax.broadcasted_iota(I32, (CHUNK, LANES), 1).astype(F32)
    diff = rowi - coli
    decay = jnp.exp(jnp.where(diff >= 0, lgf * diff, -lgb * diff)) * jnp.where(diff == 0, 2.0, 1.0)
    zeta_f = jnp.exp(lgf * (CHUNK - 1.0 - rowi))
    eta_b = jnp.exp(lgb * rowi)
    xi_f = jnp.exp(lgf * (rowi + 1.0))
    xi_b = jnp.exp(lgb * (CHUNK - rowi))
    cdec_f = jnp.exp(lgf * float(CHUNK))
    cdec_b = jnp.exp(lgb * float(CHUNK))

    crow = lax.broadcasted_iota(I32, (CTX_LEN, LANES), 0).astype(F32)
    kc = kc_ref[...].astype(F32) * k_scale
    vc = vc_ref[...]
    s_f = _dot_tn((kc * jnp.exp(lgf * (CTX_LEN - 1.0 - crow))).astype(BF16), vc)
    s_b = _dot_tn((kc * jnp.exp(lgb * crow)).astype(BF16), vc)

    upd_f, upd_b = [], []
    for i in range(N_CHUNKS):
        rows = pl.ds(i * CHUNK, CHUNK)
        kch = ks_ref[rows, :]
        vch = v_ref[rows, :]
        upd_f.append(_dot_tn((kch * zeta_f).astype(BF16), vch))
        upd_b.append(_dot_tn((kch * eta_b).astype(BF16), vch))
    state = s_f
    for i in range(N_CHUNKS):
        st_ref[i, :, :RET_DV] = state.astype(BF16)
        state = cdec_f * state + upd_f[i]
    state = s_b
    for i in reversed(range(N_CHUNKS)):
        st_ref[i, :, RET_DV:] = state.astype(BF16)
        state = cdec_b * state + upd_b[i]

    gn_w = gn_ref[...]
    for i in range(N_CHUNKS):
        rows = pl.ds(i * CHUNK, CHUNK)
        qch = qs_ref[rows, :]
        scores = _dot_nt(qch, ks_ref[rows, :].astype(BF16)) * decay
        o = jnp.dot(scores.astype(BF16), v_ref[rows, :], preferred_element_type=F32)
        cross = jnp.dot(qch, st_ref[i], preferred_element_type=F32)
        o = o + xi_f * cross[:, :RET_DV] + xi_b * cross[:, RET_DV:]
        y = _plain_norm(o) * gn_w
        o_ref[rows, :] = (_silu(g_ref[rows, :].astype(F32)) * y).astype(BF16)


def _retention(proj, proj_c, cos, sin, decay_rows, gn_w):
    blk = lambda off: pl.BlockSpec((SEQ, LANES), lambda b, h: (b, off // LANES + h))
    blk_c = lambda off: pl.BlockSpec((CTX_LEN, LANES), lambda b, h: (b, off // LANES + h))
    table = pl.BlockSpec((SEQ, LANES), lambda b, h: (0, 0))
    return pl.pallas_call(
        _retention_kernel,
        grid=(BATCH, RET_HEADS),
        in_specs=[
            blk(OFF_Q), blk(OFF_K), blk(OFF_V), blk(OFF_G), blk_c(OFF_K), blk_c(OFF_V), table, table,
            pl.BlockSpec((1, 1, LANES), lambda b, h: (h, 0, 0)),
            pl.BlockSpec((1, 1, LANES), lambda b, h: (RET_HEADS + h, 0, 0)),
            pl.BlockSpec((1, LANES), lambda b, h: (0, h)),
        ],
        out_specs=pl.BlockSpec((SEQ, LANES), lambda b, h: (b, h)),
        out_shape=jax.ShapeDtypeStruct((N_TOK, RET_W), BF16),
        scratch_shapes=[
            pltpu.VMEM((SEQ, RET_DK), BF16),
            pltpu.VMEM((SEQ, RET_DK), F32),
            pltpu.VMEM((N_CHUNKS, RET_DK, 2 * RET_DV), BF16),
        ],
        compiler_params=_params(("arbitrary", "arbitrary")),
        name="retention",
    )(proj, proj, proj, proj, proj_c, proj_c, cos, sin, decay_rows, decay_rows, gn_w)


MLA_TM = 512


def _rms_norm(x, w):
    return x * lax.rsqrt(jnp.mean(x * x, -1, keepdims=True) + EPS) * w


def _mla_kv(ckv_ref, kpe_ref, kvn_ref, wkv_ref, cos_ref, sin_ref, k_ref, v_ref, rotate):
    ckv = _rms_norm(ckv_ref[...].astype(F32), kvn_ref[...]).astype(BF16)
    kv = jnp.dot(ckv, wkv_ref[...], preferred_element_type=F32)
    lane = lax.broadcasted_iota(I32, (ckv.shape[0], LANES), 1)
    kpe = jnp.where(lane < MLA_ROPE, kpe_ref[...].astype(F32), 0.0)
    if rotate:
        kpe = _rope(kpe, cos_ref[...], sin_ref[...], MLA_ROPE // 4)
    kpe = kpe.astype(BF16)
    for h in range(MLA_HEADS):
        k_ref[:, h * QK_PAD:h * QK_PAD + MLA_NOPE] = kv[:, 2 * h * LANES:(2 * h + 1) * LANES].astype(BF16)
        k_ref[:, h * QK_PAD + MLA_NOPE:(h + 1) * QK_PAD] = kpe
        v_ref[:, h * MLA_DV:(h + 1) * MLA_DV] = kv[:, (2 * h + 1) * LANES:(2 * h + 2) * LANES].astype(BF16)


def _mla_latent_kernel(cq_ref, ckv_ref, kpe_ref, qn_ref, kvn_ref, wq_ref, wkv_ref, cos_ref, sin_ref,
                       q_ref, k_ref, v_ref):
    cq = _rms_norm(cq_ref[...].astype(F32), qn_ref[...]).astype(BF16)
    q = jnp.dot(cq, wq_ref[...], preferred_element_type=F32)
    cos = cos_ref[...]
    sin = sin_ref[...]
    scale = MLA_DQ ** -0.5
    for h in range(MLA_HEADS):
        lo = h * QK_PAD
        q_ref[:, lo:lo + MLA_NOPE] = (q[:, lo:lo + MLA_NOPE] * scale).astype(BF16)
        qpe = _rope(q[:, lo + MLA_NOPE:lo + QK_PAD], cos, sin, MLA_ROPE // 4)
        q_ref[:, lo + MLA_NOPE:lo + QK_PAD] = (qpe * scale).astype(BF16)
    _mla_kv(ckv_ref, kpe_ref, kvn_ref, wkv_ref, cos_ref, sin_ref, k_ref, v_ref, rotate=True)


def _mla_context_kernel(ckv_ref, kpe_ref, kvn_ref, wkv_ref, k_ref, v_ref):
    _mla_kv(ckv_ref, kpe_ref, kvn_ref, wkv_ref, None, None, k_ref, v_ref, rotate=False)


def _mla_latent(proj, q_norm, kv_norm, wq_pad, wkv, cos, sin):
    row = lambda w, off: pl.BlockSpec((MLA_TM, w), lambda i: (i, off // w))
    full = lambda a: pl.BlockSpec(a.shape, lambda i: (0, 0))
    table = pl.BlockSpec((MLA_TM, LANES), lambda i: (i % (SEQ // MLA_TM), 0))
    return pl.pallas_call(
        _mla_latent_kernel,
        grid=(N_TOK // MLA_TM,),
        in_specs=[row(MLA_Q_LORA, OFF_CQ), row(MLA_KV_LORA, OFF_CKV), row(LANES, OFF_KPE),
                  full(q_norm), full(kv_norm), full(wq_pad), full(wkv), table, table],
        out_specs=[
            pl.BlockSpec((MLA_TM, MLA_HEADS * QK_PAD), lambda i: (i, 0)),
            pl.BlockSpec((MLA_TM, MLA_HEADS * QK_PAD), lambda i: (i, 0)),
            pl.BlockSpec((MLA_TM, MLA_W), lambda i: (i, 0)),
        ],
        out_shape=[
            jax.ShapeDtypeStruct((N_TOK, MLA_HEADS * QK_PAD), BF16),
            jax.ShapeDtypeStruct((N_TOK, MLA_HEADS * QK_PAD), BF16),
            jax.ShapeDtypeStruct((N_TOK, MLA_W), BF16),
        ],
        compiler_params=_params(("arbitrary",)),
        name="mla_latent",
    )(proj, proj, proj, q_norm, kv_norm, wq_pad, wkv, cos, sin)


def _mla_context(proj_c, kv_norm, wkv):
    row = lambda w, off: pl.BlockSpec((MLA_TM, w), lambda i: (i, off // w))
    full = lambda a: pl.BlockSpec(a.shape, lambda i: (0, 0))
    return pl.pallas_call(
        _mla_context_kernel,
        grid=(N_CTX // MLA_TM,),
        in_specs=[row(MLA_KV_LORA, OFF_CKV), row(LANES, OFF_KPE), full(kv_norm), full(wkv)],
        out_specs=[
            pl.BlockSpec((MLA_TM, MLA_HEADS * QK_PAD), lambda i: (i, 0)),
            pl.BlockSpec((MLA_TM, MLA_W), lambda i: (i, 0)),
        ],
        out_shape=[
            jax.ShapeDtypeStruct((N_CTX, MLA_HEADS * QK_PAD), BF16),
            jax.ShapeDtypeStruct((N_CTX, MLA_W), BF16),
        ],
        compiler_params=_params(("arbitrary",)),
        name="mla_context",
    )(proj_c, proj_c, kv_norm, wkv)


ATT_TQ = 256


def _attention_kernel(q_ref, kc_ref, kl_ref, vc_ref, vl_ref, o_ref):
    for h in range(MLA_HEADS):
        qk = slice(h * QK_PAD, (h + 1) * QK_PAD)
        dv = slice(h * MLA_DV, (h + 1) * MLA_DV)
        q = q_ref[:, qk]
        s_c = _dot_nt(q, kc_ref[:, qk])
        s_l = _dot_nt(q, kl_ref[:, qk])
        m = jnp.maximum(jnp.max(s_c, -1, keepdims=True), jnp.max(s_l, -1, keepdims=True))
        p_c = jnp.exp(s_c - m)
        p_l = jnp.exp(s_l - m)
        denom = jnp.sum(p_c, -1, keepdims=True) + jnp.sum(p_l, -1, keepdims=True)
        o = (jnp.dot(p_c.astype(BF16), vc_ref[:, dv], preferred_element_type=F32)
             + jnp.dot(p_l.astype(BF16), vl_ref[:, dv], preferred_element_type=F32))
        o_ref[:, dv] = (o / denom).astype(BF16)


def _attention(q, k_ctx, k_lat, v_ctx, v_lat):
    tiles = SEQ // ATT_TQ
    return pl.pallas_call(
        _attention_kernel,
        grid=(BATCH, tiles),
        in_specs=[
            pl.BlockSpec((ATT_TQ, MLA_HEADS * QK_PAD), lambda b, i: (b * tiles + i, 0)),
            pl.BlockSpec((CTX_LEN, MLA_HEADS * QK_PAD), lambda b, i: (b, 0)),
            pl.BlockSpec((SEQ, MLA_HEADS * QK_PAD), lambda b, i: (b, 0)),
            pl.BlockSpec((CTX_LEN, MLA_W), lambda b, i: (b, 0)),
            pl.BlockSpec((SEQ, MLA_W), lambda b, i: (b, 0)),
        ],
        out_specs=pl.BlockSpec((ATT_TQ, MLA_W), lambda b, i: (b * tiles + i, 0)),
        out_shape=jax.ShapeDtypeStruct((N_TOK, MLA_W), BF16),
        compiler_params=_params(("arbitrary", "arbitrary")),
        name="attention",
    )(q, k_ctx, k_lat, v_ctx, v_lat)


OUT_TM = 256
ROUTE_E1, ROUTE_E2, ROUTE_R1, ROUTE_R2, ROUTE_W1, ROUTE_W2 = range(6)


def _outproj_kernel(ret_ref, att_ref, x_ref, wo_ref, gate_ref, shift_ref, scale_ref, lnw_ref, lnb_ref,
                    wr_ref, br_ref, h_ref, t_ref, route_ref, route_t_ref, count_ref, carry_ref, wsplit_ref):
    @pl.when(pl.program_id(0) == 0)
    def _():
        carry_ref[...] = jnp.zeros_like(carry_ref)
        w = wr_ref[...]
        w_hi = w.astype(BF16)
        wsplit_ref[:, :LANES] = w_hi
        wsplit_ref[:, LANES:] = (w - w_hi.astype(F32)).astype(BF16)

    mix = (jnp.dot(ret_ref[...], wo_ref[:RET_W, :], preferred_element_type=F32)
           + jnp.dot(att_ref[...], wo_ref[RET_W:, :], preferred_element_type=F32))
    h = _plain_norm(DEEPNORM_ALPHA * x_ref[...] + gate_ref[0] * mix) * lnw_ref[...] + lnb_ref[...]
    h_ref[...] = h
    t = _plain_norm(h) * (1.0 + scale_ref[0]) + shift_ref[0]
    t_ref[...] = _to_token_tiles(t)

    t_hi = t.astype(BF16)
    t_lo = (t - t_hi.astype(F32)).astype(BF16)
    main = jnp.dot(t_hi, wsplit_ref[...], preferred_element_type=F32)
    corr = jnp.dot(t_lo, wsplit_ref[:, :LANES], preferred_element_type=F32)
    logits = main[:, :LANES] + (main[:, LANES:] + corr) + br_ref[...]
    lane = lax.broadcasted_iota(I32, logits.shape, 1).astype(F32)
    neg = -jnp.inf
    big = float(LANES)

    def first_lane_of(mask):
        return jnp.min(jnp.where(mask, lane, big), -1, keepdims=True)

    is_group = lane < N_GROUPS
    gl = jnp.where(is_group, logits, neg)
    g_max = jnp.max(gl, -1, keepdims=True)
    g_idx = first_lane_of(is_group & (gl == g_max))
    g_prob = 1.0 / jnp.sum(jnp.where(is_group, jnp.exp(logits - g_max), 0.0), -1, keepdims=True)

    lo = N_GROUPS + g_idx * EXPERTS_PER_GROUP
    in_group = (lane >= lo) & (lane < lo + EXPERTS_PER_GROUP)
    el = jnp.where(in_group, logits, neg)
    v1 = jnp.max(el, -1, keepdims=True)
    i1 = first_lane_of(in_group & (el == v1))
    rest = in_group & (lane != i1)
    el2 = jnp.where(rest, logits, neg)
    v2 = jnp.max(el2, -1, keepdims=True)
    i2 = first_lane_of(rest & (el2 == v2))
    d = jnp.exp(v2 - v1)
    w1 = g_prob / (1.0 + d)
    w2 = g_prob * d / (1.0 + d)

    onehot = jnp.where((lane == i1) | (lane == i2), 1.0, 0.0)
    r = lax.broadcasted_iota(I32, (OUT_TM, OUT_TM), 0)
    c = lax.broadcasted_iota(I32, (OUT_TM, OUT_TM), 1)
    tri = jnp.where(c < r, 1.0, 0.0).astype(BF16)
    before = jnp.dot(tri, onehot.astype(BF16), preferred_element_type=F32) + carry_ref[0:1, :]
    r1 = jnp.sum(jnp.where(lane == i1, before, 0.0), -1, keepdims=True)
    r2 = jnp.sum(jnp.where(lane == i2, before, 0.0), -1, keepdims=True)
    carry_ref[...] = carry_ref[...] + jnp.sum(onehot, 0, keepdims=True)
    count_ref[...] = carry_ref[...]

    rec = jnp.zeros_like(logits)
    for slot, val in ((ROUTE_E1, i1 - N_GROUPS), (ROUTE_E2, i2 - N_GROUPS), (ROUTE_R1, r1), (ROUTE_R2, r2),
                      (ROUTE_W1, w1), (ROUTE_W2, w2)):
        rec = jnp.where(lane == slot, val, rec)
    route_ref[...] = rec
    route_t_ref[...] = rec.T[:8, :]


def _outproj(ret, att, x2d, wo_bf, mod_rows, ln_w, ln_b, w_route, b_route):
    tiles_per_sample = SEQ // OUT_TM
    row = lambda w: pl.BlockSpec((OUT_TM, w), lambda i: (i, 0))
    full = lambda a: pl.BlockSpec(a.shape, lambda i: (0, 0))
    mod = lambda j: pl.BlockSpec((1, 1, D_MODEL), lambda i: ((i // tiles_per_sample) * 6 + j, 0, 0))
    return pl.pallas_call(
        _outproj_kernel,
        grid=(N_TOK // OUT_TM,),
        in_specs=[row(RET_W), row(MLA_W), row(D_MODEL), full(wo_bf), mod(2), mod(3), mod(4),
                  full(ln_w), full(ln_b), full(w_route), full(b_route)],
        out_specs=[row(D_MODEL), pl.BlockSpec((OUT_TM, TOKEN_SUB, LANES), lambda i: (i, 0, 0)), row(LANES),
                   pl.BlockSpec((8, OUT_TM), lambda i: (0, i)), pl.BlockSpec((8, LANES), lambda i: (0, 0))],
        out_shape=[
            jax.ShapeDtypeStruct((N_TOK, D_MODEL), F32),
            jax.ShapeDtypeStruct((N_TOK, TOKEN_SUB, LANES), BF16),
            jax.ShapeDtypeStruct((N_TOK, LANES), F32),
            jax.ShapeDtypeStruct((8, N_TOK), F32),
            jax.ShapeDtypeStruct((8, LANES), F32),
        ],
        scratch_shapes=[pltpu.VMEM((8, LANES), F32), pltpu.VMEM((D_MODEL, 2 * LANES), BF16)],
        compiler_params=_params(("arbitrary",)),
        name="outproj_route",
    )(ret, att, x2d, wo_bf, mod_rows, mod_rows, mod_rows, ln_w, ln_b, w_route, b_route)


DISPATCH_TM = 256


def _dispatch_kernel(pos_ref, tend_ref, nt_ref, t_ref, xs_ref, zero_ref, sem, zsem):
    i = pl.program_id(0)

    @pl.when(i == 0)
    def _():
        zero_ref[...] = jnp.zeros_like(zero_ref)

        def zero_tile(tile):
            return pltpu.make_async_copy(zero_ref, xs_ref.at[pl.ds(tile * ROW_TILE, ROW_TILE)], zsem)

        def has_tiles(e):
            return tend_ref[e] > (tend_ref[e - 1] if e else 0)

        for e in range(N_EXPERTS):
            pl.when(has_tiles(e))(lambda e=e: zero_tile(tend_ref[e] - 1).start())
        lax.fori_loop(nt_ref[0], MAX_TILES, lambda j, c: (zero_tile(j).start(), c)[1], 0)
        for e in range(N_EXPERTS):
            pl.when(has_tiles(e))(lambda e=e: zero_tile(tend_ref[e] - 1).wait())
        lax.fori_loop(nt_ref[0], MAX_TILES, lambda j, c: (zero_tile(j).wait(), c)[1], 0)

    def copy(slot, j):
        return pltpu.make_async_copy(t_ref.at[j], xs_ref.at[pos_ref[slot * N_TOK + i * DISPATCH_TM + j]], sem)

    for slot in range(2):
        lax.fori_loop(0, DISPATCH_TM, lambda j, c, slot=slot: (copy(slot, j).start(), c)[1], 0, unroll=8)
    for slot in range(2):
        pltpu.make_async_copy(t_ref, xs_ref.at[pl.ds(0, DISPATCH_TM)], sem).wait()


def _dispatch(pos, tile_end, n_tiles, t_tiles):
    return pl.pallas_call(
        _dispatch_kernel,
        grid_spec=pltpu.PrefetchScalarGridSpec(
            num_scalar_prefetch=3,
            grid=(N_TOK // DISPATCH_TM,),
            in_specs=[pl.BlockSpec((DISPATCH_TM, TOKEN_SUB, LANES), lambda i, pos, te, nt: (i, 0, 0))],
            out_specs=pl.BlockSpec(memory_space=pl.ANY),
            scratch_shapes=[pltpu.VMEM((ROW_TILE, TOKEN_SUB, LANES), BF16), pltpu.SemaphoreType.DMA(()),
                            pltpu.SemaphoreType.DMA(())],
        ),
        out_shape=jax.ShapeDtypeStruct((MAX_ROWS, TOKEN_SUB, LANES), BF16),
        compiler_params=_params(("arbitrary",)),
        name="dispatch",
    )(pos, tile_end, n_tiles, t_tiles)


def _experts_kernel(tend_ref, nt_ref, xs_ref, wg_ref, wu_ref, wd_ref, ys_ref, wgu_ref, wdn_ref, xbuf_ref, ybuf_ref,
                    xsem, ysem):
    e = pl.program_id(0)
    n_total = nt_ref[0]
    first = jnp.where(e == 0, 0, tend_ref[jnp.maximum(e - 1, 0)])
    count = tend_ref[e] - first

    def x_copy(g):
        return pltpu.make_async_copy(xs_ref.at[pl.ds(g * ROW_TILE, ROW_TILE)], xbuf_ref.at[g % 2], xsem.at[g % 2])

    def y_copy(g):
        return pltpu.make_async_copy(ybuf_ref.at[g % 2], ys_ref.at[pl.ds(g * ROW_TILE, ROW_TILE)], ysem.at[g % 2])

    @pl.when(e == 0)
    def _():
        x_copy(0).start()

    @pl.when(count > 0)
    def _():
        wgu_ref[:, :D_EXPERT] = wg_ref[0].astype(BF16)
        wgu_ref[:, D_EXPERT:] = wu_ref[0].astype(BF16)
        wdn_ref[...] = wd_ref[0].astype(BF16)

        def tile(j, c):
            g = first + j

            @pl.when(g + 1 < n_total)
            def _():
                x_copy(g + 1).start()

            x_copy(g).wait()
            gu = jnp.dot(_from_token_tiles(xbuf_ref[g % 2]), wgu_ref[...], preferred_element_type=F32)
            hid = _silu(gu[:, :D_EXPERT]) * gu[:, D_EXPERT:]
            y = jnp.dot(hid.astype(BF16), wdn_ref[...], preferred_element_type=F32)

            @pl.when(g >= 2)
            def _():
                y_copy(g - 2).wait()

            ybuf_ref[g % 2] = _to_token_tiles(y)
            y_copy(g).start()
            return c

        lax.fori_loop(0, count, tile, 0)

    @pl.when(e == N_EXPERTS - 1)
    def _():
        @pl.when(n_total >= 2)
        def _():
            y_copy(n_total - 2).wait()

        y_copy(n_total - 1).wait()
        ybuf_ref[0] = jnp.zeros(ybuf_ref.shape[1:], BF16)

        def zero_copy(g):
            return pltpu.make_async_copy(ybuf_ref.at[0], ys_ref.at[pl.ds(g * ROW_TILE, ROW_TILE)], ysem.at[0])

        lax.fori_loop(n_total, MAX_TILES, lambda g, c: (zero_copy(g).start(), c)[1], 0)
        lax.fori_loop(n_total, MAX_TILES, lambda g, c: (zero_copy(g).wait(), c)[1], 0)


def _experts(tile_end, n_tiles, xs, w_gate, w_up, w_down):
    w_gate = w_gate.reshape(N_EXPERTS, D_MODEL, D_EXPERT)
    w_up = w_up.reshape(N_EXPERTS, D_MODEL, D_EXPERT)
    w_down = w_down.reshape(N_EXPERTS, D_EXPERT, D_MODEL)
    expert = lambda e, tend, nt: (e, 0, 0)
    tile_buf = pltpu.VMEM((2, ROW_TILE, TOKEN_SUB, LANES), BF16)
    return pl.pallas_call(
        _experts_kernel,
        grid_spec=pltpu.PrefetchScalarGridSpec(
            num_scalar_prefetch=2,
            grid=(N_EXPERTS,),
            in_specs=[
                pl.BlockSpec(memory_space=pl.ANY),
                pl.BlockSpec((1, D_MODEL, D_EXPERT), expert),
                pl.BlockSpec((1, D_MODEL, D_EXPERT), expert),
                pl.BlockSpec((1, D_EXPERT, D_MODEL), expert),
            ],
            out_specs=pl.BlockSpec(memory_space=pl.ANY),
            scratch_shapes=[pltpu.VMEM((D_MODEL, 2 * D_EXPERT), BF16), pltpu.VMEM((D_EXPERT, D_MODEL), BF16),
                            tile_buf, tile_buf, pltpu.SemaphoreType.DMA((2,)), pltpu.SemaphoreType.DMA((2,))],
        ),
        out_shape=jax.ShapeDtypeStruct((MAX_ROWS, TOKEN_SUB, LANES), BF16),
        compiler_params=_params(("arbitrary",)),
        name="experts",
    )(tile_end, n_tiles, xs, w_gate, w_up, w_down)


COMB_TM = 256


def _combine_kernel(pos_ref, ys_ref, h_ref, route_ref, gate_ref, lnw_ref, lnb_ref, o_ref, buf_ref, sem):
    i = pl.program_id(0)
    n = pl.num_programs(0)

    def copy(step, slot, pair, j):
        src = pos_ref[pair * N_TOK + step * COMB_TM + j]
        return pltpu.make_async_copy(ys_ref.at[src], buf_ref.at[slot, pair * COMB_TM + j], sem.at[slot])

    def start_all(step, slot):
        for pair in range(2):
            lax.fori_loop(0, COMB_TM, lambda j, c, pair=pair: (copy(step, slot, pair, j).start(), c)[1], 0,
                          unroll=8)

    def wait_all(step, slot):
        pltpu.make_async_copy(ys_ref.at[pl.ds(0, 2 * COMB_TM)], buf_ref.at[slot], sem.at[slot]).wait()

    @pl.when(i == 0)
    def _():
        start_all(0, 0)

    @pl.when(i + 1 < n)
    def _():
        start_all(i + 1, (i + 1) % 2)

    slot = i % 2
    wait_all(i, slot)

    y1 = _from_token_tiles(buf_ref[slot, :COMB_TM]).astype(F32)
    y2 = _from_token_tiles(buf_ref[slot, COMB_TM:]).astype(F32)
    y = route_ref[:, ROUTE_W1:ROUTE_W1 + 1] * y1 + route_ref[:, ROUTE_W2:ROUTE_W2 + 1] * y2
    z = DEEPNORM_ALPHA * h_ref[...] + gate_ref[0] * y
    o_ref[...] = _plain_norm(z) * lnw_ref[...] + lnb_ref[...]


def _combine(pos, ys, h, route, mod_rows, ln_w, ln_b):
    tiles_per_sample = SEQ // COMB_TM
    row = lambda w: pl.BlockSpec((COMB_TM, w), lambda i, pos: (i, 0))
    full = lambda a: pl.BlockSpec(a.shape, lambda i, pos: (0, 0))
    return pl.pallas_call(
        _combine_kernel,
        grid_spec=pltpu.PrefetchScalarGridSpec(
            num_scalar_prefetch=1,
            grid=(N_TOK // COMB_TM,),
            in_specs=[
                pl.BlockSpec(memory_space=pl.ANY), row(D_MODEL), row(LANES),
                pl.BlockSpec((1, 1, D_MODEL), lambda i, pos: ((i // tiles_per_sample) * 6 + 5, 0, 0)),
                full(ln_w), full(ln_b),
            ],
            out_specs=row(D_MODEL),
            scratch_shapes=[pltpu.VMEM((2, 2 * COMB_TM, TOKEN_SUB, LANES), BF16), pltpu.SemaphoreType.DMA((2,))],
        ),
        out_shape=jax.ShapeDtypeStruct((N_TOK, D_MODEL), F32),
        compiler_params=_params(("arbitrary",)),
        name="combine",
    )(pos, ys, h, route, mod_rows, ln_w, ln_b)


def _routing_tables(route_t, counts):
    cnt = counts[0, N_GROUPS:N_ROUTE].astype(I32)
    tiles = (cnt + ROW_TILE - 1) // ROW_TILE
    tile_end = jnp.cumsum(tiles)
    n_tiles = tile_end[-1:]
    e = route_t[ROUTE_E1:ROUTE_E2 + 1].astype(I32)
    rank = route_t[ROUTE_R1:ROUTE_R2 + 1].astype(I32)
    earlier = jnp.arange(N_EXPERTS, dtype=I32)[:, None, None] < e[None]
    base = jnp.sum(jnp.where(earlier, (tiles * ROW_TILE)[:, None, None], 0), 0)
    pos = (base + rank).reshape(-1)
    return pos, tile_end, n_tiles


def kernel(x, c, ctx, c_ctx, w_ada, b_ada, w_in, ret_decay, ret_gn_w, mla_q_norm, mla_kv_norm, w_uq, w_ukv, w_o,
           ln1_w, ln1_b, router_group_w, router_group_b, router_expert_w, router_expert_b, expert_w_gate,
           expert_w_up, expert_w_down, ln2_w, ln2_b):
    x2d = x.reshape(N_TOK, D_MODEL)
    ctx2d = ctx.reshape(N_CTX, D_MODEL)

    cc = jnp.zeros((8, D_MODEL), F32).at[:BATCH].set(c).at[BATCH].set(c_ctx)
    mod = _ada(cc, w_ada[0], b_ada)
    mod_rows = mod.reshape(8 * 6, 1, D_MODEL)

    w_in_bf = w_in[0].astype(BF16)
    proj = _inproj(x2d, mod_rows, w_in_bf, SEQ, 0)
    proj_c = _inproj(ctx2d, mod_rows, w_in_bf, N_CTX, BATCH)

    cos_r, sin_r = _rope_tables(RET_DK)
    decay_rows = jnp.broadcast_to(ret_decay[0].reshape(2 * RET_HEADS, 1, 1), (2 * RET_HEADS, 1, LANES))
    ret = _retention(proj, proj_c, jnp.asarray(cos_r), jnp.asarray(sin_r), decay_rows, ret_gn_w)

    cos_m, sin_m = _rope_tables(MLA_ROPE)
    cos_m = np.concatenate([cos_m, np.ones_like(cos_m)], 1)
    sin_m = np.concatenate([sin_m, np.zeros_like(sin_m)], 1)
    wq = w_uq[0].reshape(MLA_Q_LORA, MLA_HEADS, MLA_DQ)
    wq_pad = jnp.pad(wq, ((0, 0), (0, 0), (0, QK_PAD - MLA_DQ))).reshape(MLA_Q_LORA, MLA_HEADS * QK_PAD).astype(BF16)
    wkv = w_ukv[0].astype(BF16)
    q, k_lat, v_lat = _mla_latent(proj, mla_q_norm, mla_kv_norm, wq_pad, wkv, jnp.asarray(cos_m), jnp.asarray(sin_m))
    k_ctx, v_ctx = _mla_context(proj_c, mla_kv_norm, wkv)
    att = _attention(q, k_ctx, k_lat, v_ctx, v_lat)

    w_route = jnp.concatenate(
        [router_group_w[0], router_expert_w[0].transpose(1, 0, 2).reshape(D_MODEL, N_EXPERTS),
         jnp.zeros((D_MODEL, LANES - N_ROUTE), F32)], 1)
    b_route = jnp.concatenate(
        [router_group_b[0], router_expert_b[0].reshape(N_EXPERTS), jnp.zeros((LANES - N_ROUTE,), F32)])[None]
    h, t_tiles, route, route_t, counts = _outproj(ret, att, x2d, w_o[0].astype(BF16), mod_rows, ln1_w, ln1_b,
                                                   w_route, b_route)

    pos, tile_end, n_tiles = _routing_tables(route_t, counts)
    xs = _dispatch(pos, tile_end, n_tiles, t_tiles)
    ys = _experts(tile_end, n_tiles, xs, expert_w_gate[0], expert_w_up[0], expert_w_down[0])
    out = _combine(pos, ys, h, route, mod_rows, ln2_w, ln2_b)
    return out.reshape(BATCH, SEQ, D_MODEL)
```

```python
import numpy as np
import jax
import jax.numpy as jnp
from jax import lax
from jax.experimental import pallas as pl
from jax.experimental.pallas import tpu as pltpu

F32 = jnp.float32
BF16 = jnp.bfloat16
I32 = jnp.int32

D_MODEL = 2048
BATCH = 4
SEQ = 2048
GRID_W = 64
CTX_LEN = 256
N_TOK = BATCH * SEQ
N_CTX = BATCH * CTX_LEN

RET_HEADS = 8
RET_DK = 128
RET_DV = 128
RET_W = RET_HEADS * RET_DV
CHUNK = 128
N_CHUNKS = SEQ // CHUNK

MLA_HEADS = 8
MLA_Q_LORA = 512
MLA_KV_LORA = 256
MLA_NOPE = 128
MLA_ROPE = 64
MLA_DV = 128
MLA_W = MLA_HEADS * MLA_DV
MLA_DQ = MLA_NOPE + MLA_ROPE
QK_PAD = 256

IN_SIZES = (RET_HEADS * RET_DK, RET_HEADS * RET_DK, RET_W, RET_W, MLA_Q_LORA, MLA_KV_LORA, MLA_ROPE)
IN_W = sum(IN_SIZES)
OFF_Q, OFF_K, OFF_V, OFF_G, OFF_CQ, OFF_CKV, OFF_KPE = (int(v) for v in np.cumsum((0,) + IN_SIZES[:-1]))

N_GROUPS = 4
EXPERTS_PER_GROUP = 8
N_EXPERTS = N_GROUPS * EXPERTS_PER_GROUP
D_EXPERT = 512
N_ROUTE = N_GROUPS + N_EXPERTS

LOG2_E = float(np.log2(np.e))
ROPE_BASE = 10000.0
EPS = 1e-6
DEPTH = 1
DEEPNORM_ALPHA = (2.0 * DEPTH) ** 0.25

LANES = 128
ROW_TILE = 256
N_PAIRS = 2 * N_TOK
MAX_TILES = N_PAIRS // ROW_TILE + N_EXPERTS
MAX_ROWS = MAX_TILES * ROW_TILE
TOKEN_SUB = D_MODEL // LANES

VMEM_LIMIT = 56 * 1024 * 1024


def _params(sem, vmem=VMEM_LIMIT):
    return pltpu.CompilerParams(dimension_semantics=sem, vmem_limit_bytes=vmem)


def _silu(x):
    return x * (1.0 / (1.0 + jnp.exp(-x)))


def _plain_norm(x):
    mu = jnp.mean(x, -1, keepdims=True)
    xc = x - mu
    var = jnp.mean(xc * xc, -1, keepdims=True)
    return xc * lax.rsqrt(var + EPS)


def _to_token_tiles(x):
    return x.astype(BF16).reshape(x.shape[0], TOKEN_SUB, LANES)


def _from_token_tiles(x):
    return x.reshape(x.shape[0], D_MODEL)


ADA_TN = 1024


def _ada_kernel(cc_ref, w_ref, b_ref, o_ref):
    s = _silu(cc_ref[...])
    o_ref[...] = jnp.dot(s, w_ref[...], preferred_element_type=F32,
                         precision=lax.Precision.HIGHEST) + b_ref[...]


def _ada(cc, w_ada, b_ada):
    n = w_ada.shape[1]
    return pl.pallas_call(
        _ada_kernel,
        grid=(n // ADA_TN,),
        in_specs=[
            pl.BlockSpec((8, D_MODEL), lambda j: (0, 0)),
            pl.BlockSpec((D_MODEL, ADA_TN), lambda j: (0, j)),
            pl.BlockSpec((1, ADA_TN), lambda j: (0, j)),
        ],
        out_specs=pl.BlockSpec((8, ADA_TN), lambda j: (0, j)),
        out_shape=jax.ShapeDtypeStruct((8, n), F32),
        compiler_params=_params(("arbitrary",)),
        name="ada",
    )(cc, w_ada, b_ada)


INPROJ_TM = 1024
INPROJ_TN = 1024


def _inproj_kernel(x_ref, shift_ref, scale_ref, w_ref, o_ref, xn_ref):
    @pl.when(pl.program_id(1) == 0)
    def _():
        y = _plain_norm(x_ref[...]) * (1.0 + scale_ref[0]) + shift_ref[0]
        xn_ref[...] = y.astype(BF16)

    o_ref[...] = jnp.dot(xn_ref[...], w_ref[...], preferred_element_type=F32).astype(BF16)


def _inproj(x2d, mod_rows, w_in_bf, rows_per_sample, sample_row0):
    n = x2d.shape[0]
    tiles_per_sample = rows_per_sample // INPROJ_TM

    def mod_map(j):
        return lambda i, k: ((sample_row0 + i // tiles_per_sample) * 6 + j, 0, 0)

    return pl.pallas_call(
        _inproj_kernel,
        grid=(n // INPROJ_TM, pl.cdiv(IN_W, INPROJ_TN)),
        in_specs=[
            pl.BlockSpec((INPROJ_TM, D_MODEL), lambda i, k: (i, 0)),
            pl.BlockSpec((1, 1, D_MODEL), mod_map(0)),
            pl.BlockSpec((1, 1, D_MODEL), mod_map(1)),
            pl.BlockSpec((D_MODEL, INPROJ_TN), lambda i, k: (0, k)),
        ],
        out_specs=pl.BlockSpec((INPROJ_TM, INPROJ_TN), lambda i, k: (i, k)),
        out_shape=jax.ShapeDtypeStruct((n, IN_W), BF16),
        scratch_shapes=[pltpu.VMEM((INPROJ_TM, D_MODEL), BF16)],
        compiler_params=_params(("arbitrary", "arbitrary")),
        name="inproj",
    )(x2d, mod_rows, mod_rows, w_in_bf)


def _rope_tables(width):
    half = width // 2
    quarter = half // 2
    inv_freq = ROPE_BASE ** (-np.arange(0, half, 2, dtype=np.float64) / half)
    t = np.arange(SEQ)
    cos_parts, sin_parts = [], []
    for pos in (t // GRID_W, t % GRID_W):
        ang = pos[:, None].astype(np.float64) * inv_freq[None, :]
        c, s = np.cos(ang), np.sin(ang)
        cos_parts += [c, c]
        sin_parts += [-s, s]
    assert cos_parts[0].shape[1] == quarter
    return (np.concatenate(cos_parts, 1).astype(np.float32), np.concatenate(sin_parts, 1).astype(np.float32))


def _rope(x, cos, sin, quarter):
    lane = lax.broadcasted_iota(I32, x.shape, 1)
    first = (lane % (2 * quarter)) < quarter
    swapped = jnp.where(first, pltpu.roll(x, LANES - quarter, 1), pltpu.roll(x, quarter, 1))
    return x * cos + swapped * sin


def _dot_tn(a, b):
    return lax.dot_general(a, b, (((0,), (0,)), ((), ())), preferred_element_type=F32)


def _dot_nt(a, b):
    return lax.dot_general(a, b, (((1,), (1,)), ((), ())), preferred_element_type=F32)


def _retention_kernel(q_ref, k_ref, v_ref, g_ref, kc_ref, vc_ref, cos_ref, sin_ref, df_ref, db_ref, gn_ref,
                      o_ref, qs_ref, ks_ref, st_ref):
    k_scale = RET_DK ** -0.5
    lgf = jax.nn.log_sigmoid(df_ref[0])
    lgb = jax.nn.log_sigmoid(db_ref[0])

    cos = cos_ref[...]
    sin = sin_ref[...]
    qs_ref[...] = _rope(q_ref[...].astype(F32), cos, sin, RET_DK // 4).astype(BF16)
    ks_ref[...] = _rope(k_ref[...].astype(F32), cos, sin, RET_DK // 4) * k_scale

    rowi = lax.broadcasted_iota(I32, (CHUNK, LANES), 0).astype(F32)
    coli = lax.broadcasted_iota(I32, (CHUNK, LANES), 1).astype(F32)
    diff = rowi - coli
    decay = jnp.exp(jnp.where(diff >= 0, lgf * diff, -lgb * diff)) * jnp.where(diff == 0, 2.0, 1.0)
    zeta_f = jnp.exp(lgf * (CHUNK - 1.0 - rowi))
    eta_b = jnp.exp(lgb * rowi)
    xi_f = jnp.exp(lgf * (rowi + 1.0))
    xi_b = jnp.exp(lgb * (CHUNK - rowi))
    cdec_f = jnp.exp(lgf * float(CHUNK))
    cdec_b = jnp.exp(lgb * float(CHUNK))

    crow = lax.broadcasted_iota(I32, (CTX_LEN, LANES), 0).astype(F32)
    kc = kc_ref[...].astype(F32) * k_scale
    vc = vc_ref[...]
    s_f = _dot_tn((kc * jnp.exp(lgf * (CTX_LEN - 1.0 - crow))).astype(BF16), vc)
    s_b = _dot_tn((kc * jnp.exp(lgb * crow)).astype(BF16), vc)

    upd_f, upd_b = [], []
    for i in range(N_CHUNKS):
        rows = pl.ds(i * CHUNK, CHUNK)
        kch = ks_ref[rows, :]
        vch = v_ref[rows, :]
        upd_f.append(_dot_tn((kch * zeta_f).astype(BF16), vch))
        upd_b.append(_dot_tn((kch * eta_b).astype(BF16), vch))
    state = s_f
    for i in range(N_CHUNKS):
        st_ref[i, :, :RET_DV] = state.astype(BF16)
        state = cdec_f * state + upd_f[i]
    state = s_b
    for i in reversed(range(N_CHUNKS)):
        st_ref[i, :, RET_DV:] = state.astype(BF16)
        state = cdec_b * state + upd_b[i]

    gn_w = gn_ref[...]
    for i in range(N_CHUNKS):
        rows = pl.ds(i * CHUNK, CHUNK)
        qch = qs_ref[rows, :]
        scores = _dot_nt(qch, ks_ref[rows, :].astype(BF16)) * decay
        o = jnp.dot(scores.astype(BF16), v_ref[rows, :], preferred_element_type=F32)
        cross = jnp.dot(qch, st_ref[i], preferred_element_type=F32)
        o = o + xi_f * cross[:, :RET_DV] + xi_b * cross[:, RET_DV:]
        y = _plain_norm(o) * gn_w
        o_ref[rows, :] = (_silu(g_ref[rows, :].astype(F32)) * y).astype(BF16)


def _retention(proj, proj_c, cos, sin, decay_rows, gn_w):
    blk = lambda off: pl.BlockSpec((SEQ, LANES), lambda b, h: (b, off // LANES + h))
    blk_c = lambda off: pl.BlockSpec((CTX_LEN, LANES), lambda b, h: (b, off // LANES + h))
    table = pl.BlockSpec((SEQ, LANES), lambda b, h: (0, 0))
    return pl.pallas_call(
        _retention_kernel,
        grid=(BATCH, RET_HEADS),
        in_specs=[
            blk(OFF_Q), blk(OFF_K), blk(OFF_V), blk(OFF_G), blk_c(OFF_K), blk_c(OFF_V), table, table,
            pl.BlockSpec((1, 1, LANES), lambda b, h: (h, 0, 0)),
            pl.BlockSpec((1, 1, LANES), lambda b, h: (RET_HEADS + h, 0, 0)),
            pl.BlockSpec((1, LANES), lambda b, h: (0, h)),
        ],
        out_specs=pl.BlockSpec((SEQ, LANES), lambda b, h: (b, h)),
        out_shape=jax.ShapeDtypeStruct((N_TOK, RET_W), BF16),
        scratch_shapes=[
            pltpu.VMEM((SEQ, RET_DK), BF16),
            pltpu.VMEM((SEQ, RET_DK), F32),
            pltpu.VMEM((N_CHUNKS, RET_DK, 2 * RET_DV), BF16),
        ],
        compiler_params=_params(("arbitrary", "arbitrary")),
        name="retention",
    )(proj, proj, proj, proj, proj_c, proj_c, cos, sin, decay_rows, decay_rows, gn_w)


MLA_TM = 512


def _rms_norm(x, w):
    return x * lax.rsqrt(jnp.mean(x * x, -1, keepdims=True) + EPS) * w


def _mla_kv(ckv_ref, kpe_ref, kvn_ref, wkv_ref, cos_ref, sin_ref, k_ref, v_ref, rotate):
    ckv = _rms_norm(ckv_ref[...].astype(F32), kvn_ref[...]).astype(BF16)
    kv = jnp.dot(ckv, wkv_ref[...], preferred_element_type=F32)
    lane = lax.broadcasted_iota(I32, (ckv.shape[0], LANES), 1)
    kpe = jnp.where(lane < MLA_ROPE, kpe_ref[...].astype(F32), 0.0)
    if rotate:
        kpe = _rope(kpe, cos_ref[...], sin_ref[...], MLA_ROPE // 4)
    kpe = kpe.astype(BF16)
    for h in range(MLA_HEADS):
        k_ref[:, h * QK_PAD:h * QK_PAD + MLA_NOPE] = kv[:, 2 * h * LANES:(2 * h + 1) * LANES].astype(BF16)
        k_ref[:, h * QK_PAD + MLA_NOPE:(h + 1) * QK_PAD] = kpe
        v_ref[:, h * MLA_DV:(h + 1) * MLA_DV] = kv[:, (2 * h + 1) * LANES:(2 * h + 2) * LANES].astype(BF16)


def _mla_latent_kernel(cq_ref, ckv_ref, kpe_ref, qn_ref, kvn_ref, wq_ref, wkv_ref, cos_ref, sin_ref,
                       q_ref, k_ref, v_ref):
    cq = _rms_norm(cq_ref[...].astype(F32), qn_ref[...]).astype(BF16)
    q = jnp.dot(cq, wq_ref[...], preferred_element_type=F32)
    cos = cos_ref[...]
    sin = sin_ref[...]
    scale = MLA_DQ ** -0.5 * LOG2_E
    for h in range(MLA_HEADS):
        lo = h * QK_PAD
        q_ref[:, lo:lo + MLA_NOPE] = (q[:, lo:lo + MLA_NOPE] * scale).astype(BF16)
        qpe = _rope(q[:, lo + MLA_NOPE:lo + QK_PAD], cos, sin, MLA_ROPE // 4)
        q_ref[:, lo + MLA_NOPE:lo + QK_PAD] = (qpe * scale).astype(BF16)
    _mla_kv(ckv_ref, kpe_ref, kvn_ref, wkv_ref, cos_ref, sin_ref, k_ref, v_ref, rotate=True)


def _mla_context_kernel(ckv_ref, kpe_ref, kvn_ref, wkv_ref, k_ref, v_ref):
    _mla_kv(ckv_ref, kpe_ref, kvn_ref, wkv_ref, None, None, k_ref, v_ref, rotate=False)


def _mla_latent(proj, q_norm, kv_norm, wq_pad, wkv, cos, sin):
    row = lambda w, off: pl.BlockSpec((MLA_TM, w), lambda i: (i, off // w))
    full = lambda a: pl.BlockSpec(a.shape, lambda i: (0, 0))
    table = pl.BlockSpec((MLA_TM, LANES), lambda i: (i % (SEQ // MLA_TM), 0))
    return pl.pallas_call(
        _mla_latent_kernel,
        grid=(N_TOK // MLA_TM,),
        in_specs=[row(MLA_Q_LORA, OFF_CQ), row(MLA_KV_LORA, OFF_CKV), row(LANES, OFF_KPE),
                  full(q_norm), full(kv_norm), full(wq_pad), full(wkv), table, table],
        out_specs=[
            pl.BlockSpec((MLA_TM, MLA_HEADS * QK_PAD), lambda i: (i, 0)),
            pl.BlockSpec((MLA_TM, MLA_HEADS * QK_PAD), lambda i: (i, 0)),
            pl.BlockSpec((MLA_TM, MLA_W), lambda i: (i, 0)),
        ],
        out_shape=[
            jax.ShapeDtypeStruct((N_TOK, MLA_HEADS * QK_PAD), BF16),
            jax.ShapeDtypeStruct((N_TOK, MLA_HEADS * QK_PAD), BF16),
            jax.ShapeDtypeStruct((N_TOK, MLA_W), BF16),
        ],
        compiler_params=_params(("arbitrary",)),
        name="mla_latent",
    )(proj, proj, proj, q_norm, kv_norm, wq_pad, wkv, cos, sin)


def _mla_context(proj_c, kv_norm, wkv):
    row = lambda w, off: pl.BlockSpec((MLA_TM, w), lambda i: (i, off // w))
    full = lambda a: pl.BlockSpec(a.shape, lambda i: (0, 0))
    return pl.pallas_call(
        _mla_context_kernel,
        grid=(N_CTX // MLA_TM,),
        in_specs=[row(MLA_KV_LORA, OFF_CKV), row(LANES, OFF_KPE), full(kv_norm), full(wkv)],
        out_specs=[
            pl.BlockSpec((MLA_TM, MLA_HEADS * QK_PAD), lambda i: (i, 0)),
            pl.BlockSpec((MLA_TM, MLA_W), lambda i: (i, 0)),
        ],
        out_shape=[
            jax.ShapeDtypeStruct((N_CTX, MLA_HEADS * QK_PAD), BF16),
            jax.ShapeDtypeStruct((N_CTX, MLA_W), BF16),
        ],
        compiler_params=_params(("arbitrary",)),
        name="mla_context",
    )(proj_c, proj_c, kv_norm, wkv)


ATT_TQ = 256


def _attention_kernel(q_ref, kc_ref, kl_ref, vc_ref, vl_ref, o_ref, vext_ref):
    @pl.when(pl.program_id(1) == 0)
    def _():
        lane = lax.broadcasted_iota(I32, (CTX_LEN + SEQ, MLA_DV), 1)
        ones_col = jnp.where(lane == 0, 1.0, 0.0).astype(BF16)
        for h in range(MLA_HEADS):
            dv = slice(h * MLA_DV, (h + 1) * MLA_DV)
            vext_ref[h, :CTX_LEN, :MLA_DV] = vc_ref[:, dv]
            vext_ref[h, CTX_LEN:, :MLA_DV] = vl_ref[:, dv]
            vext_ref[h, :, MLA_DV:] = ones_col

    for h in range(MLA_HEADS):
        qk = slice(h * QK_PAD, (h + 1) * QK_PAD)
        q = q_ref[:, qk]
        s_c = _dot_nt(q, kc_ref[:, qk])
        s_l = _dot_nt(q, kl_ref[:, qk])
        m = jnp.maximum(jnp.max(s_c, -1, keepdims=True), jnp.max(s_l, -1, keepdims=True))
        p_c = jnp.exp2(s_c - m).astype(BF16)
        p_l = jnp.exp2(s_l - m).astype(BF16)
        o = (jnp.dot(p_c, vext_ref[h, :CTX_LEN, :], preferred_element_type=F32)
             + jnp.dot(p_l, vext_ref[h, CTX_LEN:, :], preferred_element_type=F32))
        o_ref[:, h * MLA_DV:(h + 1) * MLA_DV] = (o[:, :MLA_DV] / o[:, MLA_DV:MLA_DV + 1]).astype(BF16)


def _attention(q, k_ctx, k_lat, v_ctx, v_lat):
    tiles = SEQ // ATT_TQ
    return pl.pallas_call(
        _attention_kernel,
        grid=(BATCH, tiles),
        in_specs=[
            pl.BlockSpec((ATT_TQ, MLA_HEADS * QK_PAD), lambda b, i: (b * tiles + i, 0)),
            pl.BlockSpec((CTX_LEN, MLA_HEADS * QK_PAD), lambda b, i: (b, 0)),
            pl.BlockSpec((SEQ, MLA_HEADS * QK_PAD), lambda b, i: (b, 0)),
            pl.BlockSpec((CTX_LEN, MLA_W), lambda b, i: (b, 0)),
            pl.BlockSpec((SEQ, MLA_W), lambda b, i: (b, 0)),
        ],
        out_specs=pl.BlockSpec((ATT_TQ, MLA_W), lambda b, i: (b * tiles + i, 0)),
        out_shape=jax.ShapeDtypeStruct((N_TOK, MLA_W), BF16),
        scratch_shapes=[pltpu.VMEM((MLA_HEADS, CTX_LEN + SEQ, 2 * MLA_DV), BF16)],
        compiler_params=_params(("arbitrary", "arbitrary")),
        name="attention",
    )(q, k_ctx, k_lat, v_ctx, v_lat)


OUT_TM = 512
OUT_SUB = 256
ROUTE_E1, ROUTE_E2, ROUTE_R1, ROUTE_R2, ROUTE_W1, ROUTE_W2 = range(6)


def _outproj_kernel(ret_ref, att_ref, x_ref, wo_ref, gate_ref, shift_ref, scale_ref, lnw_ref, lnb_ref,
                    wr_ref, br_ref, h_ref, t_ref, route_ref, route_t_ref, count_ref, carry_ref, wsplit_ref):
    @pl.when(pl.program_id(0) == 0)
    def _():
        carry_ref[...] = jnp.zeros_like(carry_ref)
        w = wr_ref[...]
        w_hi = w.astype(BF16)
        wsplit_ref[:, :LANES] = w_hi
        wsplit_ref[:, LANES:] = (w - w_hi.astype(F32)).astype(BF16)

    subtiles = [slice(s * OUT_SUB, (s + 1) * OUT_SUB) for s in range(OUT_TM // OUT_SUB)]
    mixes = [jnp.dot(ret_ref[rows, :], wo_ref[:RET_W, :], preferred_element_type=F32)
             + jnp.dot(att_ref[rows, :], wo_ref[RET_W:, :], preferred_element_type=F32) for rows in subtiles]
    carry = carry_ref[0:1, :]
    for rows, mix in zip(subtiles, mixes):
        carry = _outproj_subtile(rows, mix, carry, x_ref, gate_ref, shift_ref, scale_ref, lnw_ref, lnb_ref, br_ref,
                                 h_ref, t_ref, route_ref, route_t_ref, wsplit_ref)
    carry_ref[...] = jnp.broadcast_to(carry, carry_ref.shape)
    count_ref[...] = jnp.broadcast_to(carry, count_ref.shape)


def _outproj_subtile(rows, mix, carry, x_ref, gate_ref, shift_ref, scale_ref, lnw_ref, lnb_ref, br_ref, h_ref, t_ref,
                     route_ref, route_t_ref, wsplit_ref):
    h = _plain_norm(DEEPNORM_ALPHA * x_ref[rows, :] + gate_ref[0] * mix) * lnw_ref[...] + lnb_ref[...]
    h_ref[rows, :] = h
    t = _plain_norm(h) * (1.0 + scale_ref[0]) + shift_ref[0]
    t_ref[rows] = _to_token_tiles(t)

    t_hi = t.astype(BF16)
    t_lo = (t - t_hi.astype(F32)).astype(BF16)
    main = jnp.dot(t_hi, wsplit_ref[...], preferred_element_type=F32)
    corr = jnp.dot(t_lo, wsplit_ref[:, :LANES], preferred_element_type=F32)
    logits = main[:, :LANES] + (main[:, LANES:] + corr) + br_ref[...]
    lane = lax.broadcasted_iota(I32, logits.shape, 1).astype(F32)
    neg = -jnp.inf
    big = float(LANES)

    def first_lane_of(mask):
        return jnp.min(jnp.where(mask, lane, big), -1, keepdims=True)

    is_group = lane < N_GROUPS
    gl = jnp.where(is_group, logits, neg)
    g_max = jnp.max(gl, -1, keepdims=True)
    g_idx = first_lane_of(is_group & (gl == g_max))
    g_prob = 1.0 / jnp.sum(jnp.where(is_group, jnp.exp(logits - g_max), 0.0), -1, keepdims=True)

    lo = N_GROUPS + g_idx * EXPERTS_PER_GROUP
    in_group = (lane >= lo) & (lane < lo + EXPERTS_PER_GROUP)
    el = jnp.where(in_group, logits, neg)
    v1 = jnp.max(el, -1, keepdims=True)
    i1 = first_lane_of(in_group & (el == v1))
    rest = in_group & (lane != i1)
    el2 = jnp.where(rest, logits, neg)
    v2 = jnp.max(el2, -1, keepdims=True)
    i2 = first_lane_of(rest & (el2 == v2))
    d = jnp.exp(v2 - v1)
    w1 = g_prob / (1.0 + d)
    w2 = g_prob * d / (1.0 + d)

    onehot = jnp.where((lane == i1) | (lane == i2), 1.0, 0.0)
    r = lax.broadcasted_iota(I32, (OUT_SUB, OUT_SUB), 0)
    c = lax.broadcasted_iota(I32, (OUT_SUB, OUT_SUB), 1)
    tri = jnp.where(c < r, 1.0, 0.0).astype(BF16)
    before = jnp.dot(tri, onehot.astype(BF16), preferred_element_type=F32) + carry
    r1 = jnp.sum(jnp.where(lane == i1, before, 0.0), -1, keepdims=True)
    r2 = jnp.sum(jnp.where(lane == i2, before, 0.0), -1, keepdims=True)

    rec = jnp.zeros_like(logits)
    for slot, val in ((ROUTE_E1, i1 - N_GROUPS), (ROUTE_E2, i2 - N_GROUPS), (ROUTE_R1, r1), (ROUTE_R2, r2),
                      (ROUTE_W1, w1), (ROUTE_W2, w2)):
        rec = jnp.where(lane == slot, val, rec)
    route_ref[rows, :] = rec
    route_t_ref[:, rows] = rec.T[:8, :]
    return carry + jnp.sum(onehot, 0, keepdims=True)


def _outproj(ret, att, x2d, wo_bf, mod_rows, ln_w, ln_b, w_route, b_route):
    tiles_per_sample = SEQ // OUT_TM
    row = lambda w: pl.BlockSpec((OUT_TM, w), lambda i: (i, 0))
    full = lambda a: pl.BlockSpec(a.shape, lambda i: (0, 0))
    mod = lambda j: pl.BlockSpec((1, 1, D_MODEL), lambda i: ((i // tiles_per_sample) * 6 + j, 0, 0))
    return pl.pallas_call(
        _outproj_kernel,
        grid=(N_TOK // OUT_TM,),
        in_specs=[row(RET_W), row(MLA_W), row(D_MODEL), full(wo_bf), mod(2), mod(3), mod(4),
                  full(ln_w), full(ln_b), full(w_route), full(b_route)],
        out_specs=[row(D_MODEL), pl.BlockSpec((OUT_TM, TOKEN_SUB, LANES), lambda i: (i, 0, 0)), row(LANES),
                   pl.BlockSpec((8, OUT_TM), lambda i: (0, i)), pl.BlockSpec((8, LANES), lambda i: (0, 0))],
        out_shape=[
            jax.ShapeDtypeStruct((N_TOK, D_MODEL), F32),
            jax.ShapeDtypeStruct((N_TOK, TOKEN_SUB, LANES), BF16),
            jax.ShapeDtypeStruct((N_TOK, LANES), F32),
            jax.ShapeDtypeStruct((8, N_TOK), F32),
            jax.ShapeDtypeStruct((8, LANES), F32),
        ],
        scratch_shapes=[pltpu.VMEM((8, LANES), F32), pltpu.VMEM((D_MODEL, 2 * LANES), BF16)],
        compiler_params=_params(("arbitrary",)),
        name="outproj_route",
    )(ret, att, x2d, wo_bf, mod_rows, mod_rows, mod_rows, ln_w, ln_b, w_route, b_route)


DISPATCH_TM = 256


def _dispatch_kernel(pos_ref, tend_ref, nt_ref, t_ref, xs_ref, zero_ref, sem, zsem):
    i = pl.program_id(0)

    @pl.when(i == 0)
    def _():
        zero_ref[...] = jnp.zeros_like(zero_ref)

        def zero_tile(tile):
            return pltpu.make_async_copy(zero_ref, xs_ref.at[pl.ds(tile * ROW_TILE, ROW_TILE)], zsem)

        def has_tiles(e):
            return tend_ref[e] > (tend_ref[e - 1] if e else 0)

        for e in range(N_EXPERTS):
            pl.when(has_tiles(e))(lambda e=e: zero_tile(tend_ref[e] - 1).start())
        lax.fori_loop(nt_ref[0], MAX_TILES, lambda j, c: (zero_tile(j).start(), c)[1], 0)
        for e in range(N_EXPERTS):
            pl.when(has_tiles(e))(lambda e=e: zero_tile(tend_ref[e] - 1).wait())
        lax.fori_loop(nt_ref[0], MAX_TILES, lambda j, c: (zero_tile(j).wait(), c)[1], 0)

    def copy(slot, j):
        return pltpu.make_async_copy(t_ref.at[j], xs_ref.at[pos_ref[slot * N_TOK + i * DISPATCH_TM + j]], sem)

    for slot in range(2):
        lax.fori_loop(0, DISPATCH_TM, lambda j, c, slot=slot: (copy(slot, j).start(), c)[1], 0, unroll=8)
    for slot in range(2):
        pltpu.make_async_copy(t_ref, xs_ref.at[pl.ds(0, DISPATCH_TM)], sem).wait()


def _dispatch(pos, tile_end, n_tiles, t_tiles):
    return pl.pallas_call(
        _dispatch_kernel,
        grid_spec=pltpu.PrefetchScalarGridSpec(
            num_scalar_prefetch=3,
            grid=(N_TOK // DISPATCH_TM,),
            in_specs=[pl.BlockSpec((DISPATCH_TM, TOKEN_SUB, LANES), lambda i, pos, te, nt: (i, 0, 0))],
            out_specs=pl.BlockSpec(memory_space=pl.ANY),
            scratch_shapes=[pltpu.VMEM((ROW_TILE, TOKEN_SUB, LANES), BF16), pltpu.SemaphoreType.DMA(()),
                            pltpu.SemaphoreType.DMA(())],
        ),
        out_shape=jax.ShapeDtypeStruct((MAX_ROWS, TOKEN_SUB, LANES), BF16),
        compiler_params=_params(("arbitrary",)),
        name="dispatch",
    )(pos, tile_end, n_tiles, t_tiles)


def _experts_kernel(tend_ref, nt_ref, xs_ref, wg_ref, wu_ref, wd_ref, ys_ref, wgu_ref, wdn_ref, xbuf_ref, ybuf_ref,
                    xsem, ysem):
    e = pl.program_id(0)
    n_total = nt_ref[0]
    first = jnp.where(e == 0, 0, tend_ref[jnp.maximum(e - 1, 0)])
    count = tend_ref[e] - first

    def x_copy(g):
        return pltpu.make_async_copy(xs_ref.at[pl.ds(g * ROW_TILE, ROW_TILE)], xbuf_ref.at[g % 2], xsem.at[g % 2])

    def y_copy(g):
        return pltpu.make_async_copy(ybuf_ref.at[g % 2], ys_ref.at[pl.ds(g * ROW_TILE, ROW_TILE)], ysem.at[g % 2])

    @pl.when(e == 0)
    def _():
        x_copy(0).start()

    @pl.when(count > 0)
    def _():
        wgu_ref[:, :D_EXPERT] = wg_ref[0].astype(BF16)
        wgu_ref[:, D_EXPERT:] = wu_ref[0].astype(BF16)
        wdn_ref[...] = wd_ref[0].astype(BF16)

        def tile(j, c):
            g = first + j

            @pl.when(g + 1 < n_total)
            def _():
                x_copy(g + 1).start()

            x_copy(g).wait()
            gu = jnp.dot(_from_token_tiles(xbuf_ref[g % 2]), wgu_ref[...], preferred_element_type=F32)
            hid = _silu(gu[:, :D_EXPERT]) * gu[:, D_EXPERT:]
            y = jnp.dot(hid.astype(BF16), wdn_ref[...], preferred_element_type=F32)

            @pl.when(g >= 2)
            def _():
                y_copy(g - 2).wait()

            ybuf_ref[g % 2] = _to_token_tiles(y)
            y_copy(g).start()
            return c

        lax.fori_loop(0, count, tile, 0)

    @pl.when(e == N_EXPERTS - 1)
    def _():
        @pl.when(n_total >= 2)
        def _():
            y_copy(n_total - 2).wait()

        y_copy(n_total - 1).wait()
        ybuf_ref[0] = jnp.zeros(ybuf_ref.shape[1:], BF16)

        def zero_copy(g):
            return pltpu.make_async_copy(ybuf_ref.at[0], ys_ref.at[pl.ds(g * ROW_TILE, ROW_TILE)], ysem.at[0])

        lax.fori_loop(n_total, MAX_TILES, lambda g, c: (zero_copy(g).start(), c)[1], 0)
        lax.fori_loop(n_total, MAX_TILES, lambda g, c: (zero_copy(g).wait(), c)[1], 0)


def _experts(tile_end, n_tiles, xs, w_gate, w_up, w_down):
    w_gate = w_gate.reshape(N_EXPERTS, D_MODEL, D_EXPERT)
    w_up = w_up.reshape(N_EXPERTS, D_MODEL, D_EXPERT)
    w_down = w_down.reshape(N_EXPERTS, D_EXPERT, D_MODEL)
    expert = lambda e, tend, nt: (e, 0, 0)
    tile_buf = pltpu.VMEM((2, ROW_TILE, TOKEN_SUB, LANES), BF16)
    return pl.pallas_call(
        _experts_kernel,
        grid_spec=pltpu.PrefetchScalarGridSpec(
            num_scalar_prefetch=2,
            grid=(N_EXPERTS,),
            in_specs=[
                pl.BlockSpec(memory_space=pl.ANY),
                pl.BlockSpec((1, D_MODEL, D_EXPERT), expert),
                pl.BlockSpec((1, D_MODEL, D_EXPERT), expert),
                pl.BlockSpec((1, D_EXPERT, D_MODEL), expert),
            ],
            out_specs=pl.BlockSpec(memory_space=pl.ANY),
            scratch_shapes=[pltpu.VMEM((D_MODEL, 2 * D_EXPERT), BF16), pltpu.VMEM((D_EXPERT, D_MODEL), BF16),
                            tile_buf, tile_buf, pltpu.SemaphoreType.DMA((2,)), pltpu.SemaphoreType.DMA((2,))],
        ),
        out_shape=jax.ShapeDtypeStruct((MAX_ROWS, TOKEN_SUB, LANES), BF16),
        compiler_params=_params(("arbitrary",)),
        name="experts",
    )(tile_end, n_tiles, xs, w_gate, w_up, w_down)


COMB_TM = 256


def _combine_kernel(pos_ref, ys_ref, h_ref, route_ref, gate_ref, lnw_ref, lnb_ref, o_ref, buf_ref, sem):
    i = pl.program_id(0)
    n = pl.num_programs(0)

    def copy(step, slot, pair, j):
        src = pos_ref[pair * N_TOK + step * COMB_TM + j]
        return pltpu.make_async_copy(ys_ref.at[src], buf_ref.at[slot, pair * COMB_TM + j], sem.at[slot])

    def start_all(step, slot):
        for pair in range(2):
            lax.fori_loop(0, COMB_TM, lambda j, c, pair=pair: (copy(step, slot, pair, j).start(), c)[1], 0,
                          unroll=8)

    def wait_all(step, slot):
        pltpu.make_async_copy(ys_ref.at[pl.ds(0, 2 * COMB_TM)], buf_ref.at[slot], sem.at[slot]).wait()

    @pl.when(i == 0)
    def _():
        start_all(0, 0)

    @pl.when(i + 1 < n)
    def _():
        start_all(i + 1, (i + 1) % 2)

    slot = i % 2
    wait_all(i, slot)

    y1 = _from_token_tiles(buf_ref[slot, :COMB_TM]).astype(F32)
    y2 = _from_token_tiles(buf_ref[slot, COMB_TM:]).astype(F32)
    y = route_ref[:, ROUTE_W1:ROUTE_W1 + 1] * y1 + route_ref[:, ROUTE_W2:ROUTE_W2 + 1] * y2
    z = DEEPNORM_ALPHA * h_ref[...] + gate_ref[0] * y
    o_ref[...] = _plain_norm(z) * lnw_ref[...] + lnb_ref[...]


def _combine(pos, ys, h, route, mod_rows, ln_w, ln_b):
    tiles_per_sample = SEQ // COMB_TM
    row = lambda w: pl.BlockSpec((COMB_TM, w), lambda i, pos: (i, 0))
    full = lambda a: pl.BlockSpec(a.shape, lambda i, pos: (0, 0))
    return pl.pallas_call(
        _combine_kernel,
        grid_spec=pltpu.PrefetchScalarGridSpec(
            num_scalar_prefetch=1,
            grid=(N_TOK // COMB_TM,),
            in_specs=[
                pl.BlockSpec(memory_space=pl.ANY), row(D_MODEL), row(LANES),
                pl.BlockSpec((1, 1, D_MODEL), lambda i, pos: ((i // tiles_per_sample) * 6 + 5, 0, 0)),
                full(ln_w), full(ln_b),
            ],
            out_specs=row(D_MODEL),
            scratch_shapes=[pltpu.VMEM((2, 2 * COMB_TM, TOKEN_SUB, LANES), BF16), pltpu.SemaphoreType.DMA((2,))],
        ),
        out_shape=jax.ShapeDtypeStruct((N_TOK, D_MODEL), F32),
        compiler_params=_params(("arbitrary",)),
        name="combine",
    )(pos, ys, h, route, mod_rows, ln_w, ln_b)


def _routing_tables(route_t, counts):
    cnt = counts[0, N_GROUPS:N_ROUTE].astype(I32)
    tiles = (cnt + ROW_TILE - 1) // ROW_TILE
    tile_end = jnp.cumsum(tiles)
    n_tiles = tile_end[-1:]
    e = route_t[ROUTE_E1:ROUTE_E2 + 1].astype(I32)
    rank = route_t[ROUTE_R1:ROUTE_R2 + 1].astype(I32)
    earlier = jnp.arange(N_EXPERTS, dtype=I32)[:, None, None] < e[None]
    base = jnp.sum(jnp.where(earlier, (tiles * ROW_TILE)[:, None, None], 0), 0)
    pos = (base + rank).reshape(-1)
    return pos, tile_end, n_tiles


def kernel(x, c, ctx, c_ctx, w_ada, b_ada, w_in, ret_decay, ret_gn_w, mla_q_norm, mla_kv_norm, w_uq, w_ukv, w_o,
           ln1_w, ln1_b, router_group_w, router_group_b, router_expert_w, router_expert_b, expert_w_gate,
           expert_w_up, expert_w_down, ln2_w, ln2_b):
    x2d = x.reshape(N_TOK, D_MODEL)
    ctx2d = ctx.reshape(N_CTX, D_MODEL)

    cc = jnp.zeros((8, D_MODEL), F32).at[:BATCH].set(c).at[BATCH].set(c_ctx)
    mod = _ada(cc, w_ada[0], b_ada)
    mod_rows = mod.reshape(8 * 6, 1, D_MODEL)

    w_in_bf = w_in[0].astype(BF16)
    proj = _inproj(x2d, mod_rows, w_in_bf, SEQ, 0)
    proj_c = _inproj(ctx2d, mod_rows, w_in_bf, N_CTX, BATCH)

    cos_r, sin_r = _rope_tables(RET_DK)
    decay_rows = jnp.broadcast_to(ret_decay[0].reshape(2 * RET_HEADS, 1, 1), (2 * RET_HEADS, 1, LANES))
    ret = _retention(proj, proj_c, jnp.asarray(cos_r), jnp.asarray(sin_r), decay_rows, ret_gn_w)

    cos_m, sin_m = _rope_tables(MLA_ROPE)
    cos_m = np.concatenate([cos_m, np.ones_like(cos_m)], 1)
    sin_m = np.concatenate([sin_m, np.zeros_like(sin_m)], 1)
    wq = w_uq[0].reshape(MLA_Q_LORA, MLA_HEADS, MLA_DQ)
    wq_pad = jnp.pad(wq, ((0, 0), (0, 0), (0, QK_PAD - MLA_DQ))).reshape(MLA_Q_LORA, MLA_HEADS * QK_PAD).astype(BF16)
    wkv = w_ukv[0].astype(BF16)
    q, k_lat, v_lat = _mla_latent(proj, mla_q_norm, mla_kv_norm, wq_pad, wkv, jnp.asarray(cos_m), jnp.asarray(sin_m))
    k_ctx, v_ctx = _mla_context(proj_c, mla_kv_norm, wkv)
    att = _attention(q, k_ctx, k_lat, v_ctx, v_lat)

    w_route = jnp.concatenate(
        [router_group_w[0], router_expert_w[0].transpose(1, 0, 2).reshape(D_MODEL, N_EXPERTS),
         jnp.zeros((D_MODEL, LANES - N_ROUTE), F32)], 1)
    b_route = jnp.concatenate(
        [router_group_b[0], router_expert_b[0].reshape(N_EXPERTS), jnp.zeros((LANES - N_ROUTE,), F32)])[None]
    h, t_tiles, route, route_t, counts = _outproj(ret, att, x2d, w_o[0].astype(BF16), mod_rows, ln1_w, ln1_b,
                                                   w_route, b_route)

    pos, tile_end, n_tiles = _routing_tables(route_t, counts)
    xs = _dispatch(pos, tile_end, n_tiles, t_tiles)
    ys = _experts(tile_end, n_tiles, xs, expert_w_gate[0], expert_w_up[0], expert_w_down[0])
    out = _combine(pos, ys, h, route, mod_rows, ln2_w, ln2_b)
    return out.reshape(BATCH, SEQ, D_MODEL)
```

```python
import numpy as np
import jax
import jax.numpy as jnp
from jax import lax
from jax.experimental import pallas as pl
from jax.experimental.pallas import tpu as pltpu

F32 = jnp.float32
BF16 = jnp.bfloat16
I32 = jnp.int32

D_MODEL = 2048
BATCH = 4
SEQ = 2048
GRID_W = 64
CTX_LEN = 256
N_TOK = BATCH * SEQ
N_CTX = BATCH * CTX_LEN

RET_HEADS = 8
RET_DK = 128
RET_DV = 128
RET_W = RET_HEADS * RET_DV
CHUNK = 128
N_CHUNKS = SEQ // CHUNK

MLA_HEADS = 8
MLA_Q_LORA = 512
MLA_KV_LORA = 256
MLA_NOPE = 128
MLA_ROPE = 64
MLA_DV = 128
MLA_W = MLA_HEADS * MLA_DV
MLA_DQ = MLA_NOPE + MLA_ROPE
QK_PAD = 256

IN_SIZES = (RET_HEADS * RET_DK, RET_HEADS * RET_DK, RET_W, RET_W, MLA_Q_LORA, MLA_KV_LORA, MLA_ROPE)
IN_W = sum(IN_SIZES)
OFF_Q, OFF_K, OFF_V, OFF_G, OFF_CQ, OFF_CKV, OFF_KPE = (int(v) for v in np.cumsum((0,) + IN_SIZES[:-1]))

N_GROUPS = 4
EXPERTS_PER_GROUP = 8
N_EXPERTS = N_GROUPS * EXPERTS_PER_GROUP
D_EXPERT = 512
N_ROUTE = N_GROUPS + N_EXPERTS

LOG2_E = float(np.log2(np.e))
ROPE_BASE = 10000.0
EPS = 1e-6
DEPTH = 1
DEEPNORM_ALPHA = (2.0 * DEPTH) ** 0.25

LANES = 128
ROW_TILE = 256
N_PAIRS = 2 * N_TOK
MAX_TILES = N_PAIRS // ROW_TILE + N_EXPERTS
MAX_ROWS = MAX_TILES * ROW_TILE
TOKEN_SUB = D_MODEL // LANES

VMEM_LIMIT = 56 * 1024 * 1024


def _params(sem, vmem=VMEM_LIMIT):
    return pltpu.CompilerParams(dimension_semantics=sem, vmem_limit_bytes=vmem)


def _silu(x):
    return x * (1.0 / (1.0 + jnp.exp(-x)))


def _plain_norm(x):
    mu = jnp.mean(x, -1, keepdims=True)
    xc = x - mu
    var = jnp.mean(xc * xc, -1, keepdims=True)
    return xc * lax.rsqrt(var + EPS)


def _to_token_tiles(x):
    return x.astype(BF16).reshape(x.shape[0], TOKEN_SUB, LANES)


def _from_token_tiles(x):
    return x.reshape(x.shape[0], D_MODEL)


ADA_TN = 1024


ADA_BANDS = 4


def _ada_kernel(cc_ref, *refs):
    w_refs, b_ref, o_ref = refs[:ADA_BANDS], refs[ADA_BANDS], refs[ADA_BANDS + 1]
    s = _silu(cc_ref[...])
    band = D_MODEL // ADA_BANDS
    acc = b_ref[...]
    for q, w_ref in enumerate(w_refs):
        acc = acc + jnp.dot(s[:, q * band:(q + 1) * band], w_ref[...], preferred_element_type=F32,
                            precision=lax.Precision.HIGHEST)
    o_ref[...] = acc


def _ada(cc, w_ada, b_ada):
    n = w_ada.shape[1]
    band = D_MODEL // ADA_BANDS
    return pl.pallas_call(
        _ada_kernel,
        grid=(n // ADA_TN,),
        in_specs=[pl.BlockSpec((8, D_MODEL), lambda j: (0, 0))]
        + [pl.BlockSpec((band, ADA_TN), lambda j, q=q: (q, j)) for q in range(ADA_BANDS)]
        + [pl.BlockSpec((1, ADA_TN), lambda j: (0, j))],
        out_specs=pl.BlockSpec((8, ADA_TN), lambda j: (0, j)),
        out_shape=jax.ShapeDtypeStruct((8, n), F32),
        compiler_params=_params(("arbitrary",)),
        name="ada",
    )(cc, *([w_ada] * ADA_BANDS), b_ada)


INPROJ_TM = 1024
INPROJ_TN = 1024
assert OFF_K == INPROJ_TN and OFF_V == 2 * INPROJ_TN and OFF_CQ == 4 * INPROJ_TN
CTX_OFF_K, CTX_OFF_V = 0, INPROJ_TN
CTX_OFF_CKV = 2 * INPROJ_TN + (OFF_CKV - OFF_CQ)
CTX_OFF_KPE = 2 * INPROJ_TN + (OFF_KPE - OFF_CQ)


def _inproj_kernel(x_ref, shift_ref, scale_ref, w_ref, o_ref, xn_ref):
    @pl.when(pl.program_id(1) == 0)
    def _():
        y = _plain_norm(x_ref[...]) * (1.0 + scale_ref[0]) + shift_ref[0]
        xn_ref[...] = y.astype(BF16)

    o_ref[...] = jnp.dot(xn_ref[...], w_ref[...], preferred_element_type=F32).astype(BF16)


def _inproj(x2d, mod_rows, w_in_bf, rows_per_sample, sample_row0, context):
    n = x2d.shape[0]
    tiles_per_sample = rows_per_sample // INPROJ_TM
    n_blocks = pl.cdiv(IN_W, INPROJ_TN)
    if context:
        grid_n, out_w = 3, 2 * INPROJ_TN + (IN_W - (n_blocks - 1) * INPROJ_TN)
        w_block = lambda i, k: (0, jnp.where(k == 2, n_blocks - 1, k + 1))
    else:
        grid_n, out_w = n_blocks, IN_W
        w_block = lambda i, k: (0, k)

    def mod_map(j):
        return lambda i, k: ((sample_row0 + i // tiles_per_sample) * 6 + j, 0, 0)

    return pl.pallas_call(
        _inproj_kernel,
        grid=(n // INPROJ_TM, grid_n),
        in_specs=[
            pl.BlockSpec((INPROJ_TM, D_MODEL), lambda i, k: (i, 0)),
            pl.BlockSpec((1, 1, D_MODEL), mod_map(0)),
            pl.BlockSpec((1, 1, D_MODEL), mod_map(1)),
            pl.BlockSpec((D_MODEL, INPROJ_TN), w_block),
        ],
        out_specs=pl.BlockSpec((INPROJ_TM, INPROJ_TN), lambda i, k: (i, k)),
        out_shape=jax.ShapeDtypeStruct((n, out_w), BF16),
        scratch_shapes=[pltpu.VMEM((INPROJ_TM, D_MODEL), BF16)],
        compiler_params=_params(("arbitrary", "arbitrary")),
        name="inproj",
    )(x2d, mod_rows, mod_rows, w_in_bf)


def _rope_tables(width):
    half = width // 2
    quarter = half // 2
    inv_freq = ROPE_BASE ** (-np.arange(0, half, 2, dtype=np.float64) / half)
    t = np.arange(SEQ)
    cos_parts, sin_parts = [], []
    for pos in (t // GRID_W, t % GRID_W):
        ang = pos[:, None].astype(np.float64) * inv_freq[None, :]
        c, s = np.cos(ang), np.sin(ang)
        cos_parts += [c, c]
        sin_parts += [-s, s]
    assert cos_parts[0].shape[1] == quarter
    return (np.concatenate(cos_parts, 1).astype(np.float32), np.concatenate(sin_parts, 1).astype(np.float32))


def _rope(x, cos, sin, quarter):
    lane = lax.broadcasted_iota(I32, x.shape, 1)
    first = (lane % (2 * quarter)) < quarter
    swapped = jnp.where(first, pltpu.roll(x, LANES - quarter, 1), pltpu.roll(x, quarter, 1))
    return x * cos + swapped * sin


def _dot_tn(a, b):
    return lax.dot_general(a, b, (((0,), (0,)), ((), ())), preferred_element_type=F32)


def _dot_nt(a, b):
    return lax.dot_general(a, b, (((1,), (1,)), ((), ())), preferred_element_type=F32)


def _retention_kernel(q_ref, k_ref, v_ref, g_ref, kc_ref, vc_ref, cos_ref, sin_ref, df_ref, db_ref, gn_ref,
                      o_ref, qs_ref, ks_ref, st_ref):
    k_scale = RET_DK ** -0.5
    lgf = jax.nn.log_sigmoid(df_ref[0])
    lgb = jax.nn.log_sigmoid(db_ref[0])

    cos = cos_ref[...]
    sin = sin_ref[...]
    qs_ref[...] = _rope(q_ref[...].astype(F32), cos, sin, RET_DK // 4).astype(BF16)
    ks_ref[...] = _rope(k_ref[...].astype(F32), cos, sin, RET_DK // 4) * k_scale

    rowi = lax.broadcasted_iota(I32, (CHUNK, LANES), 0).astype(F32)
    coli = lax.broadcasted_iota(I32, (CHUNK, LANES), 1).astype(F32)
    diff = rowi - coli
    decay = jnp.exp(jnp.where(diff >= 0, lgf * diff, -lgb * diff)) * jnp.where(diff == 0, 2.0, 1.0)
    zeta_f = jnp.exp(lgf * (CHUNK - 1.0 - rowi))
    eta_b = jnp.exp(lgb * rowi)
    xi_f = jnp.exp(lgf * (rowi + 1.0))
    xi_b = jnp.exp(lgb * (CHUNK - rowi))
    cdec_f = jnp.exp(lgf * float(CHUNK))
    cdec_b = jnp.exp(lgb * float(CHUNK))

    crow = lax.broadcasted_iota(I32, (CTX_LEN, LANES), 0).astype(F32)
    kc = kc_ref[...].astype(F32) * k_scale
    vc = vc_ref[...]
    s_f = _dot_tn((kc * jnp.exp(lgf * (CTX_LEN - 1.0 - crow))).astype(BF16), vc)
    s_b = _dot_tn((kc * jnp.exp(lgb * crow)).astype(BF16), vc)

    upd_f, upd_b = [], []
    for i in range(N_CHUNKS):
        rows = pl.ds(i * CHUNK, CHUNK)
        kch = ks_ref[rows, :]
        vch = v_ref[rows, :]
        upd_f.append(_dot_tn((kch * zeta_f).astype(BF16), vch))
        upd_b.append(_dot_tn((kch * eta_b).astype(BF16), vch))
    state = s_f
    for i in range(N_CHUNKS):
        st_ref[i, :, :RET_DV] = state.astype(BF16)
        state = cdec_f * state + upd_f[i]
    state = s_b
    for i in reversed(range(N_CHUNKS)):
        st_ref[i, :, RET_DV:] = state.astype(BF16)
        state = cdec_b * state + upd_b[i]

    gn_w = gn_ref[...]
    for i in range(N_CHUNKS):
        rows = pl.ds(i * CHUNK, CHUNK)
        qch = qs_ref[rows, :]
        scores = _dot_nt(qch, ks_ref[rows, :].astype(BF16)) * decay
        o = jnp.dot(scores.astype(BF16), v_ref[rows, :], preferred_element_type=F32)
        cross = jnp.dot(qch, st_ref[i], preferred_element_type=F32)
        o = o + xi_f * cross[:, :RET_DV] + xi_b * cross[:, RET_DV:]
        y = _plain_norm(o) * gn_w
        o_ref[rows, :] = (_silu(g_ref[rows, :].astype(F32)) * y).astype(BF16)


def _retention(proj, proj_c, cos, sin, decay_rows, gn_w):
    blk = lambda off: pl.BlockSpec((SEQ, LANES), lambda b, h: (b, off // LANES + h))
    blk_c = lambda off: pl.BlockSpec((CTX_LEN, LANES), lambda b, h: (b, off // LANES + h))
    table = pl.BlockSpec((SEQ, LANES), lambda b, h: (0, 0))
    return pl.pallas_call(
        _retention_kernel,
        grid=(BATCH, RET_HEADS),
        in_specs=[
            blk(OFF_Q), blk(OFF_K), blk(OFF_V), blk(OFF_G), blk_c(CTX_OFF_K), blk_c(CTX_OFF_V), table, table,
            pl.BlockSpec((1, 1, LANES), lambda b, h: (h, 0, 0)),
            pl.BlockSpec((1, 1, LANES), lambda b, h: (RET_HEADS + h, 0, 0)),
            pl.BlockSpec((1, LANES), lambda b, h: (0, h)),
        ],
        out_specs=pl.BlockSpec((SEQ, LANES), lambda b, h: (b, h)),
        out_shape=jax.ShapeDtypeStruct((N_TOK, RET_W), BF16),
        scratch_shapes=[
            pltpu.VMEM((SEQ, RET_DK), BF16),
            pltpu.VMEM((SEQ, RET_DK), F32),
            pltpu.VMEM((N_CHUNKS, RET_DK, 2 * RET_DV), BF16),
        ],
        compiler_params=_params(("arbitrary", "arbitrary")),
        name="retention",
    )(proj, proj, proj, proj, proj_c, proj_c, cos, sin, decay_rows, decay_rows, gn_w)


MLA_TM = 512


def _rms_norm(x, w):
    return x * lax.rsqrt(jnp.mean(x * x, -1, keepdims=True) + EPS) * w


def _mla_kv(ckv_ref, kpe_ref, kvn_ref, wkv_ref, cos_ref, sin_ref, k_ref, v_ref, rotate):
    ckv = _rms_norm(ckv_ref[...].astype(F32), kvn_ref[...]).astype(BF16)
    kv = jnp.dot(ckv, wkv_ref[...], preferred_element_type=F32)
    lane = lax.broadcasted_iota(I32, (ckv.shape[0], LANES), 1)
    kpe = jnp.where(lane < MLA_ROPE, kpe_ref[...].astype(F32), 0.0)
    if rotate:
        kpe = _rope(kpe, cos_ref[...], sin_ref[...], MLA_ROPE // 4)
    kpe = kpe.astype(BF16)
    for h in range(MLA_HEADS):
        k_ref[:, h * QK_PAD:h * QK_PAD + MLA_NOPE] = kv[:, 2 * h * LANES:(2 * h + 1) * LANES].astype(BF16)
        k_ref[:, h * QK_PAD + MLA_NOPE:(h + 1) * QK_PAD] = kpe
        v_ref[:, h * MLA_DV:(h + 1) * MLA_DV] = kv[:, (2 * h + 1) * LANES:(2 * h + 2) * LANES].astype(BF16)


def _mla_latent_kernel(cq_ref, ckv_ref, kpe_ref, qn_ref, kvn_ref, wq_ref, wkv_ref, cos_ref, sin_ref,
                       q_ref, k_ref, v_ref):
    cq = _rms_norm(cq_ref[...].astype(F32), qn_ref[...]).astype(BF16)
    q = jnp.dot(cq, wq_ref[...], preferred_element_type=F32)
    cos = cos_ref[...]
    sin = sin_ref[...]
    scale = MLA_DQ ** -0.5 * LOG2_E
    for h in range(MLA_HEADS):
        lo = h * QK_PAD
        q_ref[:, lo:lo + MLA_NOPE] = (q[:, lo:lo + MLA_NOPE] * scale).astype(BF16)
        qpe = _rope(q[:, lo + MLA_NOPE:lo + QK_PAD], cos, sin, MLA_ROPE // 4)
        q_ref[:, lo + MLA_NOPE:lo + QK_PAD] = (qpe * scale).astype(BF16)
    _mla_kv(ckv_ref, kpe_ref, kvn_ref, wkv_ref, cos_ref, sin_ref, k_ref, v_ref, rotate=True)


def _mla_context_kernel(ckv_ref, kpe_ref, kvn_ref, wkv_ref, k_ref, v_ref):
    _mla_kv(ckv_ref, kpe_ref, kvn_ref, wkv_ref, None, None, k_ref, v_ref, rotate=False)


def _mla_latent(proj, q_norm, kv_norm, wq_pad, wkv, cos, sin):
    row = lambda w, off: pl.BlockSpec((MLA_TM, w), lambda i: (i, off // w))
    full = lambda a: pl.BlockSpec(a.shape, lambda i: (0, 0))
    table = pl.BlockSpec((MLA_TM, LANES), lambda i: (i % (SEQ // MLA_TM), 0))
    return pl.pallas_call(
        _mla_latent_kernel,
        grid=(N_TOK // MLA_TM,),
        in_specs=[row(MLA_Q_LORA, OFF_CQ), row(MLA_KV_LORA, OFF_CKV), row(LANES, OFF_KPE),
                  full(q_norm), full(kv_norm), full(wq_pad), full(wkv), table, table],
        out_specs=[
            pl.BlockSpec((MLA_TM, MLA_HEADS * QK_PAD), lambda i: (i, 0)),
            pl.BlockSpec((MLA_TM, MLA_HEADS * QK_PAD), lambda i: (i, 0)),
            pl.BlockSpec((MLA_TM, MLA_W), lambda i: (i, 0)),
        ],
        out_shape=[
            jax.ShapeDtypeStruct((N_TOK, MLA_HEADS * QK_PAD), BF16),
            jax.ShapeDtypeStruct((N_TOK, MLA_HEADS * QK_PAD), BF16),
            jax.ShapeDtypeStruct((N_TOK, MLA_W), BF16),
        ],
        compiler_params=_params(("arbitrary",)),
        name="mla_latent",
    )(proj, proj, proj, q_norm, kv_norm, wq_pad, wkv, cos, sin)


def _mla_context(proj_c, kv_norm, wkv):
    row = lambda w, off: pl.BlockSpec((MLA_TM, w), lambda i: (i, off // w))
    full = lambda a: pl.BlockSpec(a.shape, lambda i: (0, 0))
    return pl.pallas_call(
        _mla_context_kernel,
        grid=(N_CTX // MLA_TM,),
        in_specs=[row(MLA_KV_LORA, CTX_OFF_CKV), row(LANES, CTX_OFF_KPE), full(kv_norm), full(wkv)],
        out_specs=[
            pl.BlockSpec((MLA_TM, MLA_HEADS * QK_PAD), lambda i: (i, 0)),
            pl.BlockSpec((MLA_TM, MLA_W), lambda i: (i, 0)),
        ],
        out_shape=[
            jax.ShapeDtypeStruct((N_CTX, MLA_HEADS * QK_PAD), BF16),
            jax.ShapeDtypeStruct((N_CTX, MLA_W), BF16),
        ],
        compiler_params=_params(("arbitrary",)),
        name="mla_context",
    )(proj_c, proj_c, kv_norm, wkv)


ATT_TQ = 256


def _attention_kernel(q_ref, kc_ref, kl_ref, vc_ref, vl_ref, o_ref, vext_ref):
    @pl.when(pl.program_id(1) == 0)
    def _():
        lane = lax.broadcasted_iota(I32, (CTX_LEN + SEQ, MLA_DV), 1)
        ones_col = jnp.where(lane == 0, 1.0, 0.0).astype(BF16)
        for h in range(MLA_HEADS):
            dv = slice(h * MLA_DV, (h + 1) * MLA_DV)
            vext_ref[h, :CTX_LEN, :MLA_DV] = vc_ref[:, dv]
            vext_ref[h, CTX_LEN:, :MLA_DV] = vl_ref[:, dv]
            vext_ref[h, :, MLA_DV:] = ones_col

    for h in range(MLA_HEADS):
        qk = slice(h * QK_PAD, (h + 1) * QK_PAD)
        q = q_ref[:, qk]
        s_c = _dot_nt(q, kc_ref[:, qk])
        s_l = _dot_nt(q, kl_ref[:, qk])
        m = jnp.maximum(jnp.max(s_c, -1, keepdims=True), jnp.max(s_l, -1, keepdims=True))
        p_c = jnp.exp2(s_c - m).astype(BF16)
        p_l = jnp.exp2(s_l - m).astype(BF16)
        o = (jnp.dot(p_c, vext_ref[h, :CTX_LEN, :], preferred_element_type=F32)
             + jnp.dot(p_l, vext_ref[h, CTX_LEN:, :], preferred_element_type=F32))
        o_ref[:, h * MLA_DV:(h + 1) * MLA_DV] = (o[:, :MLA_DV] / o[:, MLA_DV:MLA_DV + 1]).astype(BF16)


def _attention(q, k_ctx, k_lat, v_ctx, v_lat):
    tiles = SEQ // ATT_TQ
    return pl.pallas_call(
        _attention_kernel,
        grid=(BATCH, tiles),
        in_specs=[
            pl.BlockSpec((ATT_TQ, MLA_HEADS * QK_PAD), lambda b, i: (b * tiles + i, 0)),
            pl.BlockSpec((CTX_LEN, MLA_HEADS * QK_PAD), lambda b, i: (b, 0)),
            pl.BlockSpec((SEQ, MLA_HEADS * QK_PAD), lambda b, i: (b, 0)),
            pl.BlockSpec((CTX_LEN, MLA_W), lambda b, i: (b, 0)),
            pl.BlockSpec((SEQ, MLA_W), lambda b, i: (b, 0)),
        ],
        out_specs=pl.BlockSpec((ATT_TQ, MLA_W), lambda b, i: (b * tiles + i, 0)),
        out_shape=jax.ShapeDtypeStruct((N_TOK, MLA_W), BF16),
        scratch_shapes=[pltpu.VMEM((MLA_HEADS, CTX_LEN + SEQ, 2 * MLA_DV), BF16)],
        compiler_params=_params(("arbitrary", "arbitrary")),
        name="attention",
    )(q, k_ctx, k_lat, v_ctx, v_lat)


OUT_TM = 512
OUT_SUB = 256
ROUTE_E1, ROUTE_E2, ROUTE_R1, ROUTE_R2, ROUTE_W1, ROUTE_W2 = range(6)


def _outproj_kernel(ret_ref, att_ref, x_ref, wo_ref, gate_ref, shift_ref, scale_ref, lnw_ref, lnb_ref,
                    wr_ref, br_ref, h_ref, t_ref, route_ref, route_t_ref, count_ref, carry_ref, wsplit_ref):
    @pl.when(pl.program_id(0) == 0)
    def _():
        carry_ref[...] = jnp.zeros_like(carry_ref)
        w = wr_ref[...]
        w_hi = w.astype(BF16)
        wsplit_ref[:, :LANES] = w_hi
        wsplit_ref[:, LANES:] = (w - w_hi.astype(F32)).astype(BF16)

    subtiles = [slice(s * OUT_SUB, (s + 1) * OUT_SUB) for s in range(OUT_TM // OUT_SUB)]
    mixes = [jnp.dot(ret_ref[rows, :], wo_ref[:RET_W, :], preferred_element_type=F32)
             + jnp.dot(att_ref[rows, :], wo_ref[RET_W:, :], preferred_element_type=F32) for rows in subtiles]
    carry = carry_ref[0:1, :]
    for rows, mix in zip(subtiles, mixes):
        carry = _outproj_subtile(rows, mix, carry, x_ref, gate_ref, shift_ref, scale_ref, lnw_ref, lnb_ref, br_ref,
                                 h_ref, t_ref, route_ref, route_t_ref, wsplit_ref)
    carry_ref[...] = jnp.broadcast_to(carry, carry_ref.shape)
    count_ref[...] = jnp.broadcast_to(carry, count_ref.shape)


def _outproj_subtile(rows, mix, carry, x_ref, gate_ref, shift_ref, scale_ref, lnw_ref, lnb_ref, br_ref, h_ref, t_ref,
                     route_ref, route_t_ref, wsplit_ref):
    h = _plain_norm(DEEPNORM_ALPHA * x_ref[rows, :] + gate_ref[0] * mix) * lnw_ref[...] + lnb_ref[...]
    h_ref[rows, :] = h
    t = _plain_norm(h) * (1.0 + scale_ref[0]) + shift_ref[0]
    t_ref[rows] = _to_token_tiles(t)

    t_hi = t.astype(BF16)
    t_lo = (t - t_hi.astype(F32)).astype(BF16)
    main = jnp.dot(t_hi, wsplit_ref[...], preferred_element_type=F32)
    corr = jnp.dot(t_lo, wsplit_ref[:, :LANES], preferred_element_type=F32)
    logits = main[:, :LANES] + (main[:, LANES:] + corr) + br_ref[...]
    lane = lax.broadcasted_iota(I32, logits.shape, 1).astype(F32)
    neg = -jnp.inf
    big = float(LANES)

    def first_lane_of(mask):
        return jnp.min(jnp.where(mask, lane, big), -1, keepdims=True)

    is_group = lane < N_GROUPS
    gl = jnp.where(is_group, logits, neg)
    g_max = jnp.max(gl, -1, keepdims=True)
    g_idx = first_lane_of(is_group & (gl == g_max))
    g_prob = 1.0 / jnp.sum(jnp.where(is_group, jnp.exp(logits - g_max), 0.0), -1, keepdims=True)

    lo = N_GROUPS + g_idx * EXPERTS_PER_GROUP
    in_group = (lane >= lo) & (lane < lo + EXPERTS_PER_GROUP)
    el = jnp.where(in_group, logits, neg)
    v1 = jnp.max(el, -1, keepdims=True)
    i1 = first_lane_of(in_group & (el == v1))
    rest = in_group & (lane != i1)
    el2 = jnp.where(rest, logits, neg)
    v2 = jnp.max(el2, -1, keepdims=True)
    i2 = first_lane_of(rest & (el2 == v2))
    d = jnp.exp(v2 - v1)
    w1 = g_prob / (1.0 + d)
    w2 = g_prob * d / (1.0 + d)

    onehot = jnp.where((lane == i1) | (lane == i2), 1.0, 0.0)
    r = lax.broadcasted_iota(I32, (OUT_SUB, OUT_SUB), 0)
    c = lax.broadcasted_iota(I32, (OUT_SUB, OUT_SUB), 1)
    tri = jnp.where(c < r, 1.0, 0.0).astype(BF16)
    before = jnp.dot(tri, onehot.astype(BF16), preferred_element_type=F32) + carry
    r1 = jnp.sum(jnp.where(lane == i1, before, 0.0), -1, keepdims=True)
    r2 = jnp.sum(jnp.where(lane == i2, before, 0.0), -1, keepdims=True)

    rec = jnp.zeros_like(logits)
    for slot, val in ((ROUTE_E1, i1 - N_GROUPS), (ROUTE_E2, i2 - N_GROUPS), (ROUTE_R1, r1), (ROUTE_R2, r2),
                      (ROUTE_W1, w1), (ROUTE_W2, w2)):
        rec = jnp.where(lane == slot, val, rec)
    route_ref[rows, :] = rec
    route_t_ref[:, rows] = rec.T[:8, :]
    return carry + jnp.sum(onehot, 0, keepdims=True)


def _outproj(ret, att, x2d, wo_bf, mod_rows, ln_w, ln_b, w_route, b_route):
    tiles_per_sample = SEQ // OUT_TM
    row = lambda w: pl.BlockSpec((OUT_TM, w), lambda i: (i, 0))
    full = lambda a: pl.BlockSpec(a.shape, lambda i: (0, 0))
    mod = lambda j: pl.BlockSpec((1, 1, D_MODEL), lambda i: ((i // tiles_per_sample) * 6 + j, 0, 0))
    return pl.pallas_call(
        _outproj_kernel,
        grid=(N_TOK // OUT_TM,),
        in_specs=[row(RET_W), row(MLA_W), row(D_MODEL), full(wo_bf), mod(2), mod(3), mod(4),
                  full(ln_w), full(ln_b), full(w_route), full(b_route)],
        out_specs=[row(D_MODEL), pl.BlockSpec((OUT_TM, TOKEN_SUB, LANES), lambda i: (i, 0, 0)), row(LANES),
                   pl.BlockSpec((8, OUT_TM), lambda i: (0, i)), pl.BlockSpec((8, LANES), lambda i: (0, 0))],
        out_shape=[
            jax.ShapeDtypeStruct((N_TOK, D_MODEL), F32),
            jax.ShapeDtypeStruct((N_TOK, TOKEN_SUB, LANES), BF16),
            jax.ShapeDtypeStruct((N_TOK, LANES), F32),
            jax.ShapeDtypeStruct((8, N_TOK), F32),
            jax.ShapeDtypeStruct((8, LANES), F32),
        ],
        scratch_shapes=[pltpu.VMEM((8, LANES), F32), pltpu.VMEM((D_MODEL, 2 * LANES), BF16)],
        compiler_params=_params(("arbitrary",)),
        name="outproj_route",
    )(ret, att, x2d, wo_bf, mod_rows, mod_rows, mod_rows, ln_w, ln_b, w_route, b_route)


DISPATCH_TM = 256


def _dispatch_kernel(pos_ref, tend_ref, nt_ref, t_ref, xs_ref, zero_ref, sem, zsem):
    i = pl.program_id(0)

    @pl.when(i == 0)
    def _():
        zero_ref[...] = jnp.zeros_like(zero_ref)

        def zero_tile(tile):
            return pltpu.make_async_copy(zero_ref, xs_ref.at[pl.ds(tile * ROW_TILE, ROW_TILE)], zsem)

        def has_tiles(e):
            return tend_ref[e] > (tend_ref[e - 1] if e else 0)

        for e in range(N_EXPERTS):
            pl.when(has_tiles(e))(lambda e=e: zero_tile(tend_ref[e] - 1).start())
        lax.fori_loop(nt_ref[0], MAX_TILES, lambda j, c: (zero_tile(j).start(), c)[1], 0)
        for e in range(N_EXPERTS):
            pl.when(has_tiles(e))(lambda e=e: zero_tile(tend_ref[e] - 1).wait())
        lax.fori_loop(nt_ref[0], MAX_TILES, lambda j, c: (zero_tile(j).wait(), c)[1], 0)

    def copy(slot, j):
        return pltpu.make_async_copy(t_ref.at[j], xs_ref.at[pos_ref[slot * N_TOK + i * DISPATCH_TM + j]], sem)

    for slot in range(2):
        lax.fori_loop(0, DISPATCH_TM, lambda j, c, slot=slot: (copy(slot, j).start(), c)[1], 0, unroll=8)
    for slot in range(2):
        pltpu.make_async_copy(t_ref, xs_ref.at[pl.ds(0, DISPATCH_TM)], sem).wait()


def _dispatch(pos, tile_end, n_tiles, t_tiles):
    return pl.pallas_call(
        _dispatch_kernel,
        grid_spec=pltpu.PrefetchScalarGridSpec(
            num_scalar_prefetch=3,
            grid=(N_TOK // DISPATCH_TM,),
            in_specs=[pl.BlockSpec((DISPATCH_TM, TOKEN_SUB, LANES), lambda i, pos, te, nt: (i, 0, 0))],
            out_specs=pl.BlockSpec(memory_space=pl.ANY),
            scratch_shapes=[pltpu.VMEM((ROW_TILE, TOKEN_SUB, LANES), BF16), pltpu.SemaphoreType.DMA(()),
                            pltpu.SemaphoreType.DMA(())],
        ),
        out_shape=jax.ShapeDtypeStruct((MAX_ROWS, TOKEN_SUB, LANES), BF16),
        compiler_params=_params(("arbitrary",)),
        name="dispatch",
    )(pos, tile_end, n_tiles, t_tiles)


def _experts_kernel(tend_ref, nt_ref, xs_ref, wg_ref, wu_ref, wd_ref, ys_ref, wgu_ref, wdn_ref, xbuf_ref, ybuf_ref,
                    xsem, ysem):
    e = pl.program_id(0)
    n_total = nt_ref[0]
    first = jnp.where(e == 0, 0, tend_ref[jnp.maximum(e - 1, 0)])
    count = tend_ref[e] - first

    def x_copy(g):
        return pltpu.make_async_copy(xs_ref.at[pl.ds(g * ROW_TILE, ROW_TILE)], xbuf_ref.at[g % 2], xsem.at[g % 2])

    def y_copy(g):
        return pltpu.make_async_copy(ybuf_ref.at[g % 2], ys_ref.at[pl.ds(g * ROW_TILE, ROW_TILE)], ysem.at[g % 2])

    @pl.when(e == 0)
    def _():
        x_copy(0).start()

    @pl.when(count > 0)
    def _():
        wgu_ref[:, :D_EXPERT] = wg_ref[0].astype(BF16)
        wgu_ref[:, D_EXPERT:] = wu_ref[0].astype(BF16)
        wdn_ref[...] = wd_ref[0].astype(BF16)

        def tile(j, c):
            g = first + j

            @pl.when(g + 1 < n_total)
            def _():
                x_copy(g + 1).start()

            x_copy(g).wait()
            gu = jnp.dot(_from_token_tiles(xbuf_ref[g % 2]), wgu_ref[...], preferred_element_type=F32)
            hid = _silu(gu[:, :D_EXPERT]) * gu[:, D_EXPERT:]
            y = jnp.dot(hid.astype(BF16), wdn_ref[...], preferred_element_type=F32)

            @pl.when(g >= 2)
            def _():
                y_copy(g - 2).wait()

            ybuf_ref[g % 2] = _to_token_tiles(y)
            y_copy(g).start()
            return c

        lax.fori_loop(0, count, tile, 0)

    @pl.when(e == N_EXPERTS - 1)
    def _():
        @pl.when(n_total >= 2)
        def _():
            y_copy(n_total - 2).wait()

        y_copy(n_total - 1).wait()
        ybuf_ref[0] = jnp.zeros(ybuf_ref.shape[1:], BF16)

        def zero_copy(g):
            return pltpu.make_async_copy(ybuf_ref.at[0], ys_ref.at[pl.ds(g * ROW_TILE, ROW_TILE)], ysem.at[0])

        lax.fori_loop(n_total, MAX_TILES, lambda g, c: (zero_copy(g).start(), c)[1], 0)
        lax.fori_loop(n_total, MAX_TILES, lambda g, c: (zero_copy(g).wait(), c)[1], 0)


def _experts(tile_end, n_tiles, xs, w_gate, w_up, w_down):
    w_gate = w_gate.reshape(N_EXPERTS, D_MODEL, D_EXPERT)
    w_up = w_up.reshape(N_EXPERTS, D_MODEL, D_EXPERT)
    w_down = w_down.reshape(N_EXPERTS, D_EXPERT, D_MODEL)
    expert = lambda e, tend, nt: (e, 0, 0)
    tile_buf = pltpu.VMEM((2, ROW_TILE, TOKEN_SUB, LANES), BF16)
    return pl.pallas_call(
        _experts_kernel,
        grid_spec=pltpu.PrefetchScalarGridSpec(
            num_scalar_prefetch=2,
            grid=(N_EXPERTS,),
            in_specs=[
                pl.BlockSpec(memory_space=pl.ANY),
                pl.BlockSpec((1, D_MODEL, D_EXPERT), expert),
                pl.BlockSpec((1, D_MODEL, D_EXPERT), expert),
                pl.BlockSpec((1, D_EXPERT, D_MODEL), expert),
            ],
            out_specs=pl.BlockSpec(memory_space=pl.ANY),
            scratch_shapes=[pltpu.VMEM((D_MODEL, 2 * D_EXPERT), BF16), pltpu.VMEM((D_EXPERT, D_MODEL), BF16),
                            tile_buf, tile_buf, pltpu.SemaphoreType.DMA((2,)), pltpu.SemaphoreType.DMA((2,))],
        ),
        out_shape=jax.ShapeDtypeStruct((MAX_ROWS, TOKEN_SUB, LANES), BF16),
        compiler_params=_params(("arbitrary",)),
        name="experts",
    )(tile_end, n_tiles, xs, w_gate, w_up, w_down)


COMB_TM = 256


def _combine_kernel(pos_ref, ys_ref, h_ref, route_ref, gate_ref, lnw_ref, lnb_ref, o_ref, buf_ref, sem):
    i = pl.program_id(0)
    n = pl.num_programs(0)

    def copy(step, slot, pair, j):
        src = pos_ref[pair * N_TOK + step * COMB_TM + j]
        return pltpu.make_async_copy(ys_ref.at[src], buf_ref.at[slot, pair * COMB_TM + j], sem.at[slot])

    def start_all(step, slot):
        for pair in range(2):
            lax.fori_loop(0, COMB_TM, lambda j, c, pair=pair: (copy(step, slot, pair, j).start(), c)[1], 0,
                          unroll=8)

    def wait_all(step, slot):
        pltpu.make_async_copy(ys_ref.at[pl.ds(0, 2 * COMB_TM)], buf_ref.at[slot], sem.at[slot]).wait()

    @pl.when(i == 0)
    def _():
        start_all(0, 0)

    @pl.when(i + 1 < n)
    def _():
        start_all(i + 1, (i + 1) % 2)

    slot = i % 2
    wait_all(i, slot)

    y1 = _from_token_tiles(buf_ref[slot, :COMB_TM]).astype(F32)
    y2 = _from_token_tiles(buf_ref[slot, COMB_TM:]).astype(F32)
    y = route_ref[:, ROUTE_W1:ROUTE_W1 + 1] * y1 + route_ref[:, ROUTE_W2:ROUTE_W2 + 1] * y2
    z = DEEPNORM_ALPHA * h_ref[...] + gate_ref[0] * y
    o_ref[...] = _plain_norm(z) * lnw_ref[...] + lnb_ref[...]


def _combine(pos, ys, h, route, mod_rows, ln_w, ln_b):
    tiles_per_sample = SEQ // COMB_TM
    row = lambda w: pl.BlockSpec((COMB_TM, w), lambda i, pos: (i, 0))
    full = lambda a: pl.BlockSpec(a.shape, lambda i, pos: (0, 0))
    return pl.pallas_call(
        _combine_kernel,
        grid_spec=pltpu.PrefetchScalarGridSpec(
            num_scalar_prefetch=1,
            grid=(N_TOK // COMB_TM,),
            in_specs=[
                pl.BlockSpec(memory_space=pl.ANY), row(D_MODEL), row(LANES),
                pl.BlockSpec((1, 1, D_MODEL), lambda i, pos: ((i // tiles_per_sample) * 6 + 5, 0, 0)),
                full(ln_w), full(ln_b),
            ],
            out_specs=row(D_MODEL),
            scratch_shapes=[pltpu.VMEM((2, 2 * COMB_TM, TOKEN_SUB, LANES), BF16), pltpu.SemaphoreType.DMA((2,))],
        ),
        out_shape=jax.ShapeDtypeStruct((N_TOK, D_MODEL), F32),
        compiler_params=_params(("arbitrary",)),
        name="combine",
    )(pos, ys, h, route, mod_rows, ln_w, ln_b)


def _routing_tables(route_t, counts):
    cnt = counts[0, N_GROUPS:N_ROUTE].astype(I32)
    tiles = (cnt + ROW_TILE - 1) // ROW_TILE
    tile_end = jnp.cumsum(tiles)
    n_tiles = tile_end[-1:]
    e = route_t[ROUTE_E1:ROUTE_E2 + 1].astype(I32)
    rank = route_t[ROUTE_R1:ROUTE_R2 + 1].astype(I32)
    earlier = jnp.arange(N_EXPERTS, dtype=I32)[:, None, None] < e[None]
    base = jnp.sum(jnp.where(earlier, (tiles * ROW_TILE)[:, None, None], 0), 0)
    pos = (base + rank).reshape(-1)
    return pos, tile_end, n_tiles


def kernel(x, c, ctx, c_ctx, w_ada, b_ada, w_in, ret_decay, ret_gn_w, mla_q_norm, mla_kv_norm, w_uq, w_ukv, w_o,
           ln1_w, ln1_b, router_group_w, router_group_b, router_expert_w, router_expert_b, expert_w_gate,
           expert_w_up, expert_w_down, ln2_w, ln2_b):
    x2d = x.reshape(N_TOK, D_MODEL)
    ctx2d = ctx.reshape(N_CTX, D_MODEL)

    cc = jnp.zeros((8, D_MODEL), F32).at[:BATCH].set(c).at[BATCH].set(c_ctx)
    mod = _ada(cc, w_ada[0], b_ada)
    mod_rows = mod.reshape(8 * 6, 1, D_MODEL)

    w_in_bf = w_in[0].astype(BF16)
    proj = _inproj(x2d, mod_rows, w_in_bf, SEQ, 0, context=False)
    proj_c = _inproj(ctx2d, mod_rows, w_in_bf, N_CTX, BATCH, context=True)

    cos_r, sin_r = _rope_tables(RET_DK)
    decay_rows = jnp.broadcast_to(ret_decay[0].reshape(2 * RET_HEADS, 1, 1), (2 * RET_HEADS, 1, LANES))
    ret = _retention(proj, proj_c, jnp.asarray(cos_r), jnp.asarray(sin_r), decay_rows, ret_gn_w)

    cos_m, sin_m = _rope_tables(MLA_ROPE)
    cos_m = np.concatenate([cos_m, np.ones_like(cos_m)], 1)
    sin_m = np.concatenate([sin_m, np.zeros_like(sin_m)], 1)
    wq = w_uq[0].reshape(MLA_Q_LORA, MLA_HEADS, MLA_DQ)
    wq_pad = jnp.pad(wq, ((0, 0), (0, 0), (0, QK_PAD - MLA_DQ))).reshape(MLA_Q_LORA, MLA_HEADS * QK_PAD).astype(BF16)
    wkv = w_ukv[0].astype(BF16)
    q, k_lat, v_lat = _mla_latent(proj, mla_q_norm, mla_kv_norm, wq_pad, wkv, jnp.asarray(cos_m), jnp.asarray(sin_m))
    k_ctx, v_ctx = _mla_context(proj_c, mla_kv_norm, wkv)
    att = _attention(q, k_ctx, k_lat, v_ctx, v_lat)

    w_route = jnp.concatenate(
        [router_group_w[0], router_expert_w[0].transpose(1, 0, 2).reshape(D_MODEL, N_EXPERTS),
         jnp.zeros((D_MODEL, LANES - N_ROUTE), F32)], 1)
    b_route = jnp.concatenate(
        [router_group_b[0], router_expert_b[0].reshape(N_EXPERTS), jnp.zeros((LANES - N_ROUTE,), F32)])[None]
    h, t_tiles, route, route_t, counts = _outproj(ret, att, x2d, w_o[0].astype(BF16), mod_rows, ln1_w, ln1_b,
                                                   w_route, b_route)

    pos, tile_end, n_tiles = _routing_tables(route_t, counts)
    xs = _dispatch(pos, tile_end, n_tiles, t_tiles)
    ys = _experts(tile_end, n_tiles, xs, expert_w_gate[0], expert_w_up[0], expert_w_down[0])
    out = _combine(pos, ys, h, route, mod_rows, ln2_w, ln2_b)
    return out.reshape(BATCH, SEQ, D_MODEL)
```

```python
import numpy as np
import jax
import jax.numpy as jnp
from jax import lax
from jax.experimental import pallas as pl
from jax.experimental.pallas import tpu as pltpu

F32 = jnp.float32
BF16 = jnp.bfloat16
I32 = jnp.int32

D_MODEL = 2048
BATCH = 4
SEQ = 2048
GRID_W = 64
CTX_LEN = 256
N_TOK = BATCH * SEQ
N_CTX = BATCH * CTX_LEN

RET_HEADS = 8
RET_DK = 128
RET_DV = 128
RET_W = RET_HEADS * RET_DV
CHUNK = 128
N_CHUNKS = SEQ // CHUNK

MLA_HEADS = 8
MLA_Q_LORA = 512
MLA_KV_LORA = 256
MLA_NOPE = 128
MLA_ROPE = 64
MLA_DV = 128
MLA_W = MLA_HEADS * MLA_DV
MLA_DQ = MLA_NOPE + MLA_ROPE
QK_PAD = 256

IN_SIZES = (RET_HEADS * RET_DK, RET_HEADS * RET_DK, RET_W, RET_W, MLA_Q_LORA, MLA_KV_LORA, MLA_ROPE)
IN_W = sum(IN_SIZES)
OFF_Q, OFF_K, OFF_V, OFF_G, OFF_CQ, OFF_CKV, OFF_KPE = (int(v) for v in np.cumsum((0,) + IN_SIZES[:-1]))

N_GROUPS = 4
EXPERTS_PER_GROUP = 8
N_EXPERTS = N_GROUPS * EXPERTS_PER_GROUP
D_EXPERT = 512
N_ROUTE = N_GROUPS + N_EXPERTS

LOG2_E = float(np.log2(np.e))
ROPE_BASE = 10000.0
EPS = 1e-6
DEPTH = 1
DEEPNORM_ALPHA = (2.0 * DEPTH) ** 0.25

LANES = 128
ROW_TILE = 256
N_PAIRS = 2 * N_TOK
MAX_TILES = N_PAIRS // ROW_TILE + N_EXPERTS
MAX_ROWS = MAX_TILES * ROW_TILE
TOKEN_SUB = D_MODEL // LANES

VMEM_LIMIT = 56 * 1024 * 1024


def _params(sem, vmem=VMEM_LIMIT):
    return pltpu.CompilerParams(dimension_semantics=sem, vmem_limit_bytes=vmem)


def _silu(x):
    return x * (1.0 / (1.0 + jnp.exp(-x)))


def _plain_norm(x):
    mu = jnp.mean(x, -1, keepdims=True)
    xc = x - mu
    var = jnp.mean(xc * xc, -1, keepdims=True)
    return xc * lax.rsqrt(var + EPS)


def _to_token_tiles(x):
    return x.astype(BF16).reshape(x.shape[0], TOKEN_SUB, LANES)


def _from_token_tiles(x):
    return x.reshape(x.shape[0], D_MODEL)


ADA_TN = 1024


ADA_BANDS = 4


def _ada_kernel(cc_ref, *refs):
    w_refs, b_ref, o_ref = refs[:ADA_BANDS], refs[ADA_BANDS], refs[ADA_BANDS + 1]
    s = _silu(cc_ref[...])
    band = D_MODEL // ADA_BANDS
    acc = b_ref[...]
    for q, w_ref in enumerate(w_refs):
        acc = acc + jnp.dot(s[:, q * band:(q + 1) * band], w_ref[...], preferred_element_type=F32,
                            precision=lax.Precision.HIGHEST)
    o_ref[...] = acc


def _ada(cc, w_ada, b_ada):
    n = w_ada.shape[1]
    band = D_MODEL // ADA_BANDS
    return pl.pallas_call(
        _ada_kernel,
        grid=(n // ADA_TN,),
        in_specs=[pl.BlockSpec((8, D_MODEL), lambda j: (0, 0))]
        + [pl.BlockSpec((band, ADA_TN), lambda j, q=q: (q, j)) for q in range(ADA_BANDS)]
        + [pl.BlockSpec((1, ADA_TN), lambda j: (0, j))],
        out_specs=pl.BlockSpec((8, ADA_TN), lambda j: (0, j)),
        out_shape=jax.ShapeDtypeStruct((8, n), F32),
        compiler_params=_params(("arbitrary",)),
        name="ada",
    )(cc, *([w_ada] * ADA_BANDS), b_ada)


INPROJ_TM = 1024
INPROJ_TN = 1024
assert OFF_K == INPROJ_TN and OFF_V == 2 * INPROJ_TN and OFF_CQ == 4 * INPROJ_TN
CTX_OFF_K, CTX_OFF_V = 0, INPROJ_TN
CTX_OFF_CKV = 2 * INPROJ_TN + (OFF_CKV - OFF_CQ)
CTX_OFF_KPE = 2 * INPROJ_TN + (OFF_KPE - OFF_CQ)


def _inproj_kernel(x_ref, shift_ref, scale_ref, w_ref, o_ref, xn_ref):
    @pl.when(pl.program_id(1) == 0)
    def _():
        y = _plain_norm(x_ref[...]) * (1.0 + scale_ref[0]) + shift_ref[0]
        xn_ref[...] = y.astype(BF16)

    o_ref[...] = jnp.dot(xn_ref[...], w_ref[...], preferred_element_type=F32).astype(BF16)


def _inproj(x2d, mod_rows, w_in_bf, rows_per_sample, sample_row0, context):
    n = x2d.shape[0]
    tiles_per_sample = rows_per_sample // INPROJ_TM
    n_blocks = pl.cdiv(IN_W, INPROJ_TN)
    if context:
        grid_n, out_w = 3, 2 * INPROJ_TN + (IN_W - (n_blocks - 1) * INPROJ_TN)
        w_block = lambda i, k: (0, jnp.where(k == 2, n_blocks - 1, k + 1))
    else:
        grid_n, out_w = n_blocks, IN_W
        w_block = lambda i, k: (0, k)

    def mod_map(j):
        return lambda i, k: ((sample_row0 + i // tiles_per_sample) * 6 + j, 0, 0)

    return pl.pallas_call(
        _inproj_kernel,
        grid=(n // INPROJ_TM, grid_n),
        in_specs=[
            pl.BlockSpec((INPROJ_TM, D_MODEL), lambda i, k: (i, 0)),
            pl.BlockSpec((1, 1, D_MODEL), mod_map(0)),
            pl.BlockSpec((1, 1, D_MODEL), mod_map(1)),
            pl.BlockSpec((D_MODEL, INPROJ_TN), w_block),
        ],
        out_specs=pl.BlockSpec((INPROJ_TM, INPROJ_TN), lambda i, k: (i, k)),
        out_shape=jax.ShapeDtypeStruct((n, out_w), BF16),
        scratch_shapes=[pltpu.VMEM((INPROJ_TM, D_MODEL), BF16)],
        compiler_params=_params(("arbitrary", "arbitrary")),
        name="inproj",
    )(x2d, mod_rows, mod_rows, w_in_bf)


def _rope_tables(width):
    half = width // 2
    quarter = half // 2
    inv_freq = ROPE_BASE ** (-np.arange(0, half, 2, dtype=np.float64) / half)
    t = np.arange(SEQ)
    cos_parts, sin_parts = [], []
    for pos in (t // GRID_W, t % GRID_W):
        ang = pos[:, None].astype(np.float64) * inv_freq[None, :]
        c, s = np.cos(ang), np.sin(ang)
        cos_parts += [c, c]
        sin_parts += [-s, s]
    assert cos_parts[0].shape[1] == quarter
    return (np.concatenate(cos_parts, 1).astype(np.float32), np.concatenate(sin_parts, 1).astype(np.float32))


def _rope(x, cos, sin, quarter):
    lane = lax.broadcasted_iota(I32, x.shape, 1)
    first = (lane % (2 * quarter)) < quarter
    swapped = jnp.where(first, pltpu.roll(x, LANES - quarter, 1), pltpu.roll(x, quarter, 1))
    return x * cos + swapped * sin


def _dot_tn(a, b):
    return lax.dot_general(a, b, (((0,), (0,)), ((), ())), preferred_element_type=F32)


def _dot_nt(a, b):
    return lax.dot_general(a, b, (((1,), (1,)), ((), ())), preferred_element_type=F32)


def _retention_kernel(q_ref, k_ref, v_ref, g_ref, kc_ref, vc_ref, cos_ref, sin_ref, df_ref, db_ref, gn_ref,
                      o_ref, qs_ref, ks_ref, st_ref):
    k_scale = RET_DK ** -0.5
    lgf = jax.nn.log_sigmoid(df_ref[0])
    lgb = jax.nn.log_sigmoid(db_ref[0])

    cos = cos_ref[...]
    sin = sin_ref[...]
    qs_ref[...] = _rope(q_ref[...].astype(F32), cos, sin, RET_DK // 4).astype(BF16)
    ks_ref[...] = _rope(k_ref[...].astype(F32), cos, sin, RET_DK // 4) * k_scale

    rowi = lax.broadcasted_iota(I32, (CHUNK, LANES), 0).astype(F32)
    coli = lax.broadcasted_iota(I32, (CHUNK, LANES), 1).astype(F32)
    diff = rowi - coli
    decay = jnp.exp(jnp.where(diff >= 0, lgf * diff, -lgb * diff)) * jnp.where(diff == 0, 2.0, 1.0)
    zeta_f = jnp.exp(lgf * (CHUNK - 1.0 - rowi))
    eta_b = jnp.exp(lgb * rowi)
    xi_f = jnp.exp(lgf * (rowi + 1.0))
    xi_b = jnp.exp(lgb * (CHUNK - rowi))
    cdec_f = jnp.exp(lgf * float(CHUNK))
    cdec_b = jnp.exp(lgb * float(CHUNK))

    crow = lax.broadcasted_iota(I32, (CTX_LEN, LANES), 0).astype(F32)
    kc = kc_ref[...].astype(F32) * k_scale
    vc = vc_ref[...]
    s_f = _dot_tn((kc * jnp.exp(lgf * (CTX_LEN - 1.0 - crow))).astype(BF16), vc)
    s_b = _dot_tn((kc * jnp.exp(lgb * crow)).astype(BF16), vc)

    upd_f, upd_b = [], []
    for i in range(N_CHUNKS):
        rows = pl.ds(i * CHUNK, CHUNK)
        kch = ks_ref[rows, :]
        vch = v_ref[rows, :]
        upd_f.append(_dot_tn((kch * zeta_f).astype(BF16), vch))
        upd_b.append(_dot_tn((kch * eta_b).astype(BF16), vch))
    state = s_f
    for i in range(N_CHUNKS):
        st_ref[i, :, :RET_DV] = state.astype(BF16)
        state = cdec_f * state + upd_f[i]
    state = s_b
    for i in reversed(range(N_CHUNKS)):
        st_ref[i, :, RET_DV:] = state.astype(BF16)
        state = cdec_b * state + upd_b[i]

    gn_w = gn_ref[...]
    for i in range(N_CHUNKS):
        rows = pl.ds(i * CHUNK, CHUNK)
        qch = qs_ref[rows, :]
        scores = _dot_nt(qch, ks_ref[rows, :].astype(BF16)) * decay
        o = jnp.dot(scores.astype(BF16), v_ref[rows, :], preferred_element_type=F32)
        cross = jnp.dot(qch, st_ref[i], preferred_element_type=F32)
        o = o + xi_f * cross[:, :RET_DV] + xi_b * cross[:, RET_DV:]
        y = _plain_norm(o) * gn_w
        o_ref[rows, :] = (_silu(g_ref[rows, :].astype(F32)) * y).astype(BF16)


def _retention(proj, proj_c, cos, sin, decay_rows, gn_w):
    blk = lambda off: pl.BlockSpec((SEQ, LANES), lambda b, h: (b, off // LANES + h))
    blk_c = lambda off: pl.BlockSpec((CTX_LEN, LANES), lambda b, h: (b, off // LANES + h))
    table = pl.BlockSpec((SEQ, LANES), lambda b, h: (0, 0))
    return pl.pallas_call(
        _retention_kernel,
        grid=(BATCH, RET_HEADS),
        in_specs=[
            blk(OFF_Q), blk(OFF_K), blk(OFF_V), blk(OFF_G), blk_c(CTX_OFF_K), blk_c(CTX_OFF_V), table, table,
            pl.BlockSpec((1, 1, LANES), lambda b, h: (h, 0, 0)),
            pl.BlockSpec((1, 1, LANES), lambda b, h: (RET_HEADS + h, 0, 0)),
            pl.BlockSpec((1, LANES), lambda b, h: (0, h)),
        ],
        out_specs=pl.BlockSpec((SEQ, LANES), lambda b, h: (b, h)),
        out_shape=jax.ShapeDtypeStruct((N_TOK, RET_W), BF16),
        scratch_shapes=[
            pltpu.VMEM((SEQ, RET_DK), BF16),
            pltpu.VMEM((SEQ, RET_DK), F32),
            pltpu.VMEM((N_CHUNKS, RET_DK, 2 * RET_DV), BF16),
        ],
        compiler_params=_params(("arbitrary", "arbitrary")),
        name="retention",
    )(proj, proj, proj, proj, proj_c, proj_c, cos, sin, decay_rows, decay_rows, gn_w)


MLA_TM = 512


def _rms_norm(x, w):
    return x * lax.rsqrt(jnp.mean(x * x, -1, keepdims=True) + EPS) * w


def _mla_kv(ckv_ref, kpe_ref, kvn_ref, wkv_ref, cos_ref, sin_ref, k_ref, v_ref, rotate):
    ckv = _rms_norm(ckv_ref[...].astype(F32), kvn_ref[...]).astype(BF16)
    kv = jnp.dot(ckv, wkv_ref[...], preferred_element_type=F32)
    lane = lax.broadcasted_iota(I32, (ckv.shape[0], LANES), 1)
    kpe = jnp.where(lane < MLA_ROPE, kpe_ref[...].astype(F32), 0.0)
    if rotate:
        kpe = _rope(kpe, cos_ref[...], sin_ref[...], MLA_ROPE // 4)
    kpe = kpe.astype(BF16)
    for h in range(MLA_HEADS):
        k_ref[:, h * QK_PAD:h * QK_PAD + MLA_NOPE] = kv[:, 2 * h * LANES:(2 * h + 1) * LANES].astype(BF16)
        k_ref[:, h * QK_PAD + MLA_NOPE:(h + 1) * QK_PAD] = kpe
        v_ref[:, h * MLA_DV:(h + 1) * MLA_DV] = kv[:, (2 * h + 1) * LANES:(2 * h + 2) * LANES].astype(BF16)


def _mla_latent_kernel(cq_ref, ckv_ref, kpe_ref, qn_ref, kvn_ref, wq_ref, wkv_ref, cos_ref, sin_ref,
                       q_ref, k_ref, v_ref):
    cq = _rms_norm(cq_ref[...].astype(F32), qn_ref[...]).astype(BF16)
    q = jnp.dot(cq, wq_ref[...], preferred_element_type=F32)
    cos = cos_ref[...]
    sin = sin_ref[...]
    scale = MLA_DQ ** -0.5 * LOG2_E
    for h in range(MLA_HEADS):
        lo = h * QK_PAD
        q_ref[:, lo:lo + MLA_NOPE] = (q[:, lo:lo + MLA_NOPE] * scale).astype(BF16)
        qpe = _rope(q[:, lo + MLA_NOPE:lo + QK_PAD], cos, sin, MLA_ROPE // 4)
        q_ref[:, lo + MLA_NOPE:lo + QK_PAD] = (qpe * scale).astype(BF16)
    _mla_kv(ckv_ref, kpe_ref, kvn_ref, wkv_ref, cos_ref, sin_ref, k_ref, v_ref, rotate=True)


def _mla_context_kernel(ckv_ref, kpe_ref, kvn_ref, wkv_ref, k_ref, v_ref):
    _mla_kv(ckv_ref, kpe_ref, kvn_ref, wkv_ref, None, None, k_ref, v_ref, rotate=False)


def _mla_latent(proj, q_norm, kv_norm, wq_pad, wkv, cos, sin):
    row = lambda w, off: pl.BlockSpec((MLA_TM, w), lambda i: (i, off // w))
    full = lambda a: pl.BlockSpec(a.shape, lambda i: (0, 0))
    table = pl.BlockSpec((MLA_TM, LANES), lambda i: (i % (SEQ // MLA_TM), 0))
    return pl.pallas_call(
        _mla_latent_kernel,
        grid=(N_TOK // MLA_TM,),
        in_specs=[row(MLA_Q_LORA, OFF_CQ), row(MLA_KV_LORA, OFF_CKV), row(LANES, OFF_KPE),
                  full(q_norm), full(kv_norm), full(wq_pad), full(wkv), table, table],
        out_specs=[
            pl.BlockSpec((MLA_TM, MLA_HEADS * QK_PAD), lambda i: (i, 0)),
            pl.BlockSpec((MLA_TM, MLA_HEADS * QK_PAD), lambda i: (i, 0)),
            pl.BlockSpec((MLA_TM, MLA_W), lambda i: (i, 0)),
        ],
        out_shape=[
            jax.ShapeDtypeStruct((N_TOK, MLA_HEADS * QK_PAD), BF16),
            jax.ShapeDtypeStruct((N_TOK, MLA_HEADS * QK_PAD), BF16),
            jax.ShapeDtypeStruct((N_TOK, MLA_W), BF16),
        ],
        compiler_params=_params(("arbitrary",)),
        name="mla_latent",
    )(proj, proj, proj, q_norm, kv_norm, wq_pad, wkv, cos, sin)


def _mla_context(proj_c, kv_norm, wkv):
    row = lambda w, off: pl.BlockSpec((MLA_TM, w), lambda i: (i, off // w))
    full = lambda a: pl.BlockSpec(a.shape, lambda i: (0, 0))
    return pl.pallas_call(
        _mla_context_kernel,
        grid=(N_CTX // MLA_TM,),
        in_specs=[row(MLA_KV_LORA, CTX_OFF_CKV), row(LANES, CTX_OFF_KPE), full(kv_norm), full(wkv)],
        out_specs=[
            pl.BlockSpec((MLA_TM, MLA_HEADS * QK_PAD), lambda i: (i, 0)),
            pl.BlockSpec((MLA_TM, MLA_W), lambda i: (i, 0)),
        ],
        out_shape=[
            jax.ShapeDtypeStruct((N_CTX, MLA_HEADS * QK_PAD), BF16),
            jax.ShapeDtypeStruct((N_CTX, MLA_W), BF16),
        ],
        compiler_params=_params(("arbitrary",)),
        name="mla_context",
    )(proj_c, proj_c, kv_norm, wkv)


ATT_TQ = 256


def _attention_kernel(q_ref, kc_ref, kl_ref, vc_ref, vl_ref, o_ref, vext_ref):
    @pl.when(pl.program_id(1) == 0)
    def _():
        lane = lax.broadcasted_iota(I32, (CTX_LEN + SEQ, MLA_DV), 1)
        ones_col = jnp.where(lane == 0, 1.0, 0.0).astype(BF16)
        for h in range(MLA_HEADS):
            dv = slice(h * MLA_DV, (h + 1) * MLA_DV)
            vext_ref[h, :CTX_LEN, :MLA_DV] = vc_ref[:, dv]
            vext_ref[h, CTX_LEN:, :MLA_DV] = vl_ref[:, dv]
            vext_ref[h, :, MLA_DV:] = ones_col

    for h in range(MLA_HEADS):
        qk = slice(h * QK_PAD, (h + 1) * QK_PAD)
        q = q_ref[:, qk]
        s_c = _dot_nt(q, kc_ref[:, qk])
        s_l = _dot_nt(q, kl_ref[:, qk])
        m = jnp.maximum(jnp.max(s_c, -1, keepdims=True), jnp.max(s_l, -1, keepdims=True))
        p_c = jnp.exp2(s_c - m).astype(BF16)
        p_l = jnp.exp2(s_l - m).astype(BF16)
        o = (jnp.dot(p_c, vext_ref[h, :CTX_LEN, :], preferred_element_type=F32)
             + jnp.dot(p_l, vext_ref[h, CTX_LEN:, :], preferred_element_type=F32))
        o_ref[:, h * MLA_DV:(h + 1) * MLA_DV] = (o[:, :MLA_DV] / o[:, MLA_DV:MLA_DV + 1]).astype(BF16)


def _attention(q, k_ctx, k_lat, v_ctx, v_lat):
    tiles = SEQ // ATT_TQ
    return pl.pallas_call(
        _attention_kernel,
        grid=(BATCH, tiles),
        in_specs=[
            pl.BlockSpec((ATT_TQ, MLA_HEADS * QK_PAD), lambda b, i: (b * tiles + i, 0)),
            pl.BlockSpec((CTX_LEN, MLA_HEADS * QK_PAD), lambda b, i: (b, 0)),
            pl.BlockSpec((SEQ, MLA_HEADS * QK_PAD), lambda b, i: (b, 0)),
            pl.BlockSpec((CTX_LEN, MLA_W), lambda b, i: (b, 0)),
            pl.BlockSpec((SEQ, MLA_W), lambda b, i: (b, 0)),
        ],
        out_specs=pl.BlockSpec((ATT_TQ, MLA_W), lambda b, i: (b * tiles + i, 0)),
        out_shape=jax.ShapeDtypeStruct((N_TOK, MLA_W), BF16),
        scratch_shapes=[pltpu.VMEM((MLA_HEADS, CTX_LEN + SEQ, 2 * MLA_DV), BF16)],
        compiler_params=_params(("arbitrary", "arbitrary")),
        name="attention",
    )(q, k_ctx, k_lat, v_ctx, v_lat)


OUT_TM = 512
OUT_SUB = 256
ROUTE_E1, ROUTE_E2, ROUTE_R1, ROUTE_R2, ROUTE_W1, ROUTE_W2 = range(6)


def _outproj_kernel(ret_ref, att_ref, x_ref, wo_ref, gate_ref, shift_ref, scale_ref, lnw_ref, lnb_ref,
                    wr_ref, br_ref, h_ref, t_ref, route_ref, route_t_ref, count_ref, carry_ref, wsplit_ref):
    @pl.when(pl.program_id(0) == 0)
    def _():
        carry_ref[...] = jnp.zeros_like(carry_ref)
        w = wr_ref[...]
        w_hi = w.astype(BF16)
        wsplit_ref[:, :LANES] = w_hi
        wsplit_ref[:, LANES:] = (w - w_hi.astype(F32)).astype(BF16)

    subtiles = [slice(s * OUT_SUB, (s + 1) * OUT_SUB) for s in range(OUT_TM // OUT_SUB)]
    mixes = [jnp.dot(ret_ref[rows, :], wo_ref[:RET_W, :], preferred_element_type=F32)
             + jnp.dot(att_ref[rows, :], wo_ref[RET_W:, :], preferred_element_type=F32) for rows in subtiles]
    carry = carry_ref[0:1, :]
    for rows, mix in zip(subtiles, mixes):
        carry = _outproj_subtile(rows, mix, carry, x_ref, gate_ref, shift_ref, scale_ref, lnw_ref, lnb_ref, br_ref,
                                 h_ref, t_ref, route_ref, route_t_ref, wsplit_ref)
    carry_ref[...] = jnp.broadcast_to(carry, carry_ref.shape)
    count_ref[...] = jnp.broadcast_to(carry, count_ref.shape)


def _outproj_subtile(rows, mix, carry, x_ref, gate_ref, shift_ref, scale_ref, lnw_ref, lnb_ref, br_ref, h_ref, t_ref,
                     route_ref, route_t_ref, wsplit_ref):
    h = _plain_norm(DEEPNORM_ALPHA * x_ref[rows, :] + gate_ref[0] * mix) * lnw_ref[...] + lnb_ref[...]
    h_ref[rows, :] = h
    t = _plain_norm(h) * (1.0 + scale_ref[0]) + shift_ref[0]
    t_ref[rows] = _to_token_tiles(t)

    t_hi = t.astype(BF16)
    t_lo = (t - t_hi.astype(F32)).astype(BF16)
    main = jnp.dot(t_hi, wsplit_ref[...], preferred_element_type=F32)
    corr = jnp.dot(t_lo, wsplit_ref[:, :LANES], preferred_element_type=F32)
    logits = main[:, :LANES] + (main[:, LANES:] + corr) + br_ref[...]
    lane = lax.broadcasted_iota(I32, logits.shape, 1).astype(F32)
    neg = -jnp.inf
    big = float(LANES)

    def first_lane_of(mask):
        return jnp.min(jnp.where(mask, lane, big), -1, keepdims=True)

    is_group = lane < N_GROUPS
    gl = jnp.where(is_group, logits, neg)
    g_max = jnp.max(gl, -1, keepdims=True)
    g_idx = first_lane_of(is_group & (gl == g_max))
    g_prob = 1.0 / jnp.sum(jnp.where(is_group, jnp.exp(logits - g_max), 0.0), -1, keepdims=True)

    lo = N_GROUPS + g_idx * EXPERTS_PER_GROUP
    in_group = (lane >= lo) & (lane < lo + EXPERTS_PER_GROUP)
    el = jnp.where(in_group, logits, neg)
    v1 = jnp.max(el, -1, keepdims=True)
    i1 = first_lane_of(in_group & (el == v1))
    rest = in_group & (lane != i1)
    el2 = jnp.where(rest, logits, neg)
    v2 = jnp.max(el2, -1, keepdims=True)
    i2 = first_lane_of(rest & (el2 == v2))
    d = jnp.exp(v2 - v1)
    w1 = g_prob / (1.0 + d)
    w2 = g_prob * d / (1.0 + d)

    onehot = jnp.where((lane == i1) | (lane == i2), 1.0, 0.0)
    r = lax.broadcasted_iota(I32, (OUT_SUB, OUT_SUB), 0)
    c = lax.broadcasted_iota(I32, (OUT_SUB, OUT_SUB), 1)
    tri = jnp.where(c < r, 1.0, 0.0).astype(BF16)
    before = jnp.dot(tri, onehot.astype(BF16), preferred_element_type=F32) + carry
    r1 = jnp.sum(jnp.where(lane == i1, before, 0.0), -1, keepdims=True)
    r2 = jnp.sum(jnp.where(lane == i2, before, 0.0), -1, keepdims=True)

    rec = jnp.zeros_like(logits)
    for slot, val in ((ROUTE_E1, i1 - N_GROUPS), (ROUTE_E2, i2 - N_GROUPS), (ROUTE_R1, r1), (ROUTE_R2, r2),
                      (ROUTE_W1, w1), (ROUTE_W2, w2)):
        rec = jnp.where(lane == slot, val, rec)
    route_ref[rows, :] = rec
    route_t_ref[:, rows] = rec.T[:8, :]
    return carry + jnp.sum(onehot, 0, keepdims=True)


def _outproj(ret, att, x2d, wo_bf, mod_rows, ln_w, ln_b, w_route, b_route):
    tiles_per_sample = SEQ // OUT_TM
    row = lambda w: pl.BlockSpec((OUT_TM, w), lambda i: (i, 0))
    full = lambda a: pl.BlockSpec(a.shape, lambda i: (0, 0))
    mod = lambda j: pl.BlockSpec((1, 1, D_MODEL), lambda i: ((i // tiles_per_sample) * 6 + j, 0, 0))
    return pl.pallas_call(
        _outproj_kernel,
        grid=(N_TOK // OUT_TM,),
        in_specs=[row(RET_W), row(MLA_W), row(D_MODEL), full(wo_bf), mod(2), mod(3), mod(4),
                  full(ln_w), full(ln_b), full(w_route), full(b_route)],
        out_specs=[row(D_MODEL), pl.BlockSpec((OUT_TM, TOKEN_SUB, LANES), lambda i: (i, 0, 0)), row(LANES),
                   pl.BlockSpec((8, OUT_TM), lambda i: (0, i)), pl.BlockSpec((8, LANES), lambda i: (0, 0))],
        out_shape=[
            jax.ShapeDtypeStruct((N_TOK, D_MODEL), F32),
            jax.ShapeDtypeStruct((N_TOK, TOKEN_SUB, LANES), BF16),
            jax.ShapeDtypeStruct((N_TOK, LANES), F32),
            jax.ShapeDtypeStruct((8, N_TOK), F32),
            jax.ShapeDtypeStruct((8, LANES), F32),
        ],
        scratch_shapes=[pltpu.VMEM((8, LANES), F32), pltpu.VMEM((D_MODEL, 2 * LANES), BF16)],
        compiler_params=_params(("arbitrary",)),
        name="outproj_route",
    )(ret, att, x2d, wo_bf, mod_rows, mod_rows, mod_rows, ln_w, ln_b, w_route, b_route)


def _slot_tokens_kernel(pos_ref, tend_ref, cnt_ref, src_ref):
    def clear(s, c):
        src_ref[s] = 0
        return c

    for e in range(N_EXPERTS):
        first = (tend_ref[e - 1] if e else 0) * ROW_TILE
        lax.fori_loop(first + cnt_ref[e], tend_ref[e] * ROW_TILE, clear, 0)
    lax.fori_loop(tend_ref[N_EXPERTS - 1] * ROW_TILE, MAX_ROWS, clear, 0)

    def place(tok, c):
        src_ref[pos_ref[tok]] = tok
        src_ref[pos_ref[N_TOK + tok]] = tok
        return c

    lax.fori_loop(0, N_TOK, place, 0, unroll=8)


def _slot_tokens(pos, tile_end, counts):
    return pl.pallas_call(
        _slot_tokens_kernel,
        grid_spec=pltpu.PrefetchScalarGridSpec(
            num_scalar_prefetch=3,
            grid=(1,),
            in_specs=[],
            out_specs=pl.BlockSpec(memory_space=pltpu.SMEM),
        ),
        out_shape=jax.ShapeDtypeStruct((MAX_ROWS,), I32),
        compiler_params=_params(("arbitrary",)),
        name="slot_tokens",
    )(pos, tile_end, counts)


def _experts_kernel(tend_ref, nt_ref, src_ref, t_ref, wg_ref, wu_ref, wd_ref, ys_ref, wgu_ref, wdn_ref, xbuf_ref,
                    ybuf_ref, xsem, ysem):
    e = pl.program_id(0)
    n_total = nt_ref[0]
    first = jnp.where(e == 0, 0, tend_ref[jnp.maximum(e - 1, 0)])
    count = tend_ref[e] - first

    def start_gather(g):
        def start(r, c):
            pltpu.make_async_copy(t_ref.at[src_ref[g * ROW_TILE + r]], xbuf_ref.at[g % 2, r], xsem.at[g % 2]).start()
            return c

        lax.fori_loop(0, ROW_TILE, start, 0, unroll=8)

    def wait_gather(g):
        pltpu.make_async_copy(t_ref.at[pl.ds(0, ROW_TILE)], xbuf_ref.at[g % 2], xsem.at[g % 2]).wait()

    def y_copy(g):
        return pltpu.make_async_copy(ybuf_ref.at[g % 2], ys_ref.at[pl.ds(g * ROW_TILE, ROW_TILE)], ysem.at[g % 2])

    @pl.when(e == 0)
    def _():
        start_gather(0)

    @pl.when(count > 0)
    def _():
        wgu_ref[:, :D_EXPERT] = wg_ref[0].astype(BF16)
        wgu_ref[:, D_EXPERT:] = wu_ref[0].astype(BF16)
        wdn_ref[...] = wd_ref[0].astype(BF16)

        def tile(j, c):
            g = first + j

            @pl.when(g + 1 < n_total)
            def _():
                start_gather(g + 1)

            wait_gather(g)
            gu = jnp.dot(_from_token_tiles(xbuf_ref[g % 2]), wgu_ref[...], preferred_element_type=F32)
            hid = _silu(gu[:, :D_EXPERT]) * gu[:, D_EXPERT:]
            y = jnp.dot(hid.astype(BF16), wdn_ref[...], preferred_element_type=F32)

            @pl.when(g >= 2)
            def _():
                y_copy(g - 2).wait()

            ybuf_ref[g % 2] = _to_token_tiles(y)
            y_copy(g).start()
            return c

        lax.fori_loop(0, count, tile, 0)

    @pl.when(e == N_EXPERTS - 1)
    def _():
        @pl.when(n_total >= 2)
        def _():
            y_copy(n_total - 2).wait()

        y_copy(n_total - 1).wait()
        ybuf_ref[0] = jnp.zeros(ybuf_ref.shape[1:], BF16)

        def zero_copy(g):
            return pltpu.make_async_copy(ybuf_ref.at[0], ys_ref.at[pl.ds(g * ROW_TILE, ROW_TILE)], ysem.at[0])

        lax.fori_loop(n_total, MAX_TILES, lambda g, c: (zero_copy(g).start(), c)[1], 0)
        lax.fori_loop(n_total, MAX_TILES, lambda g, c: (zero_copy(g).wait(), c)[1], 0)


def _experts(tile_end, n_tiles, slot_tokens, t_tiles, w_gate, w_up, w_down):
    w_gate = w_gate.reshape(N_EXPERTS, D_MODEL, D_EXPERT)
    w_up = w_up.reshape(N_EXPERTS, D_MODEL, D_EXPERT)
    w_down = w_down.reshape(N_EXPERTS, D_EXPERT, D_MODEL)
    expert = lambda e, tend, nt, src: (e, 0, 0)
    tile_buf = pltpu.VMEM((2, ROW_TILE, TOKEN_SUB, LANES), BF16)
    return pl.pallas_call(
        _experts_kernel,
        grid_spec=pltpu.PrefetchScalarGridSpec(
            num_scalar_prefetch=3,
            grid=(N_EXPERTS,),
            in_specs=[
                pl.BlockSpec(memory_space=pl.ANY),
                pl.BlockSpec((1, D_MODEL, D_EXPERT), expert),
                pl.BlockSpec((1, D_MODEL, D_EXPERT), expert),
                pl.BlockSpec((1, D_EXPERT, D_MODEL), expert),
            ],
            out_specs=pl.BlockSpec(memory_space=pl.ANY),
            scratch_shapes=[pltpu.VMEM((D_MODEL, 2 * D_EXPERT), BF16), pltpu.VMEM((D_EXPERT, D_MODEL), BF16),
                            tile_buf, tile_buf, pltpu.SemaphoreType.DMA((2,)), pltpu.SemaphoreType.DMA((2,))],
        ),
        out_shape=jax.ShapeDtypeStruct((MAX_ROWS, TOKEN_SUB, LANES), BF16),
        compiler_params=_params(("arbitrary",)),
        name="experts",
    )(tile_end, n_tiles, slot_tokens, t_tiles, w_gate, w_up, w_down)


COMB_TM = 256


def _combine_kernel(pos_ref, ys_ref, h_ref, route_ref, gate_ref, lnw_ref, lnb_ref, o_ref, buf_ref, sem):
    i = pl.program_id(0)
    n = pl.num_programs(0)

    def copy(step, slot, pair, j):
        src = pos_ref[pair * N_TOK + step * COMB_TM + j]
        return pltpu.make_async_copy(ys_ref.at[src], buf_ref.at[slot, pair * COMB_TM + j], sem.at[slot])

    def start_all(step, slot):
        for pair in range(2):
            lax.fori_loop(0, COMB_TM, lambda j, c, pair=pair: (copy(step, slot, pair, j).start(), c)[1], 0,
                          unroll=8)

    def wait_all(step, slot):
        pltpu.make_async_copy(ys_ref.at[pl.ds(0, 2 * COMB_TM)], buf_ref.at[slot], sem.at[slot]).wait()

    @pl.when(i == 0)
    def _():
        start_all(0, 0)

    @pl.when(i + 1 < n)
    def _():
        start_all(i + 1, (i + 1) % 2)

    slot = i % 2
    wait_all(i, slot)

    y1 = _from_token_tiles(buf_ref[slot, :COMB_TM]).astype(F32)
    y2 = _from_token_tiles(buf_ref[slot, COMB_TM:]).astype(F32)
    y = route_ref[:, ROUTE_W1:ROUTE_W1 + 1] * y1 + route_ref[:, ROUTE_W2:ROUTE_W2 + 1] * y2
    z = DEEPNORM_ALPHA * h_ref[...] + gate_ref[0] * y
    o_ref[...] = _plain_norm(z) * lnw_ref[...] + lnb_ref[...]


def _combine(pos, ys, h, route, mod_rows, ln_w, ln_b):
    tiles_per_sample = SEQ // COMB_TM
    row = lambda w: pl.BlockSpec((COMB_TM, w), lambda i, pos: (i, 0))
    full = lambda a: pl.BlockSpec(a.shape, lambda i, pos: (0, 0))
    return pl.pallas_call(
        _combine_kernel,
        grid_spec=pltpu.PrefetchScalarGridSpec(
            num_scalar_prefetch=1,
            grid=(N_TOK // COMB_TM,),
            in_specs=[
                pl.BlockSpec(memory_space=pl.ANY), row(D_MODEL), row(LANES),
                pl.BlockSpec((1, 1, D_MODEL), lambda i, pos: ((i // tiles_per_sample) * 6 + 5, 0, 0)),
                full(ln_w), full(ln_b),
            ],
            out_specs=row(D_MODEL),
            scratch_shapes=[pltpu.VMEM((2, 2 * COMB_TM, TOKEN_SUB, LANES), BF16), pltpu.SemaphoreType.DMA((2,))],
        ),
        out_shape=jax.ShapeDtypeStruct((N_TOK, D_MODEL), F32),
        compiler_params=_params(("arbitrary",)),
        name="combine",
    )(pos, ys, h, route, mod_rows, ln_w, ln_b)


def _routing_tables(route_t, counts):
    cnt = counts[0, N_GROUPS:N_ROUTE].astype(I32)
    tiles = (cnt + ROW_TILE - 1) // ROW_TILE
    tile_end = jnp.cumsum(tiles)
    n_tiles = tile_end[-1:]
    e = route_t[ROUTE_E1:ROUTE_E2 + 1].astype(I32)
    rank = route_t[ROUTE_R1:ROUTE_R2 + 1].astype(I32)
    earlier = jnp.arange(N_EXPERTS, dtype=I32)[:, None, None] < e[None]
    base = jnp.sum(jnp.where(earlier, (tiles * ROW_TILE)[:, None, None], 0), 0)
    pos = (base + rank).reshape(-1)
    return pos, cnt, tile_end, n_tiles


def kernel(x, c, ctx, c_ctx, w_ada, b_ada, w_in, ret_decay, ret_gn_w, mla_q_norm, mla_kv_norm, w_uq, w_ukv, w_o,
           ln1_w, ln1_b, router_group_w, router_group_b, router_expert_w, router_expert_b, expert_w_gate,
           expert_w_up, expert_w_down, ln2_w, ln2_b):
    x2d = x.reshape(N_TOK, D_MODEL)
    ctx2d = ctx.reshape(N_CTX, D_MODEL)

    cc = jnp.zeros((8, D_MODEL), F32).at[:BATCH].set(c).at[BATCH].set(c_ctx)
    mod = _ada(cc, w_ada[0], b_ada)
    mod_rows = mod.reshape(8 * 6, 1, D_MODEL)

    w_in_bf = w_in[0].astype(BF16)
    proj = _inproj(x2d, mod_rows, w_in_bf, SEQ, 0, context=False)
    proj_c = _inproj(ctx2d, mod_rows, w_in_bf, N_CTX, BATCH, context=True)

    cos_r, sin_r = _rope_tables(RET_DK)
    decay_rows = jnp.broadcast_to(ret_decay[0].reshape(2 * RET_HEADS, 1, 1), (2 * RET_HEADS, 1, LANES))
    ret = _retention(proj, proj_c, jnp.asarray(cos_r), jnp.asarray(sin_r), decay_rows, ret_gn_w)

    cos_m, sin_m = _rope_tables(MLA_ROPE)
    cos_m = np.concatenate([cos_m, np.ones_like(cos_m)], 1)
    sin_m = np.concatenate([sin_m, np.zeros_like(sin_m)], 1)
    wq = w_uq[0].reshape(MLA_Q_LORA, MLA_HEADS, MLA_DQ)
    wq_pad = jnp.pad(wq, ((0, 0), (0, 0), (0, QK_PAD - MLA_DQ))).reshape(MLA_Q_LORA, MLA_HEADS * QK_PAD).astype(BF16)
    wkv = w_ukv[0].astype(BF16)
    q, k_lat, v_lat = _mla_latent(proj, mla_q_norm, mla_kv_norm, wq_pad, wkv, jnp.asarray(cos_m), jnp.asarray(sin_m))
    k_ctx, v_ctx = _mla_context(proj_c, mla_kv_norm, wkv)
    att = _attention(q, k_ctx, k_lat, v_ctx, v_lat)

    w_route = jnp.concatenate(
        [router_group_w[0], router_expert_w[0].transpose(1, 0, 2).reshape(D_MODEL, N_EXPERTS),
         jnp.zeros((D_MODEL, LANES - N_ROUTE), F32)], 1)
    b_route = jnp.concatenate(
        [router_group_b[0], router_expert_b[0].reshape(N_EXPERTS), jnp.zeros((LANES - N_ROUTE,), F32)])[None]
    h, t_tiles, route, route_t, counts = _outproj(ret, att, x2d, w_o[0].astype(BF16), mod_rows, ln1_w, ln1_b,
                                                   w_route, b_route)

    pos, cnt, tile_end, n_tiles = _routing_tables(route_t, counts)
    slot_tokens = _slot_tokens(pos, tile_end, cnt)
    ys = _experts(tile_end, n_tiles, slot_tokens, t_tiles, expert_w_gate[0], expert_w_up[0], expert_w_down[0])
    out = _combine(pos, ys, h, route, mod_rows, ln2_w, ln2_b)
    return out.reshape(BATCH, SEQ, D_MODEL)
```

```python
import numpy as np
import jax
import jax.numpy as jnp
from jax import lax
from jax.experimental import pallas as pl
from jax.experimental.pallas import tpu as pltpu

F32 = jnp.float32
BF16 = jnp.bfloat16
I32 = jnp.int32

D_MODEL = 2048
BATCH = 4
SEQ = 2048
GRID_W = 64
CTX_LEN = 256
N_TOK = BATCH * SEQ
N_CTX = BATCH * CTX_LEN

RET_HEADS = 8
RET_DK = 128
RET_DV = 128
RET_W = RET_HEADS * RET_DV
CHUNK = 128
N_CHUNKS = SEQ // CHUNK

MLA_HEADS = 8
MLA_Q_LORA = 512
MLA_KV_LORA = 256
MLA_NOPE = 128
MLA_ROPE = 64
MLA_DV = 128
MLA_W = MLA_HEADS * MLA_DV
MLA_DQ = MLA_NOPE + MLA_ROPE
QK_PAD = 256

IN_SIZES = (RET_HEADS * RET_DK, RET_HEADS * RET_DK, RET_W, RET_W, MLA_Q_LORA, MLA_KV_LORA, MLA_ROPE)
IN_W = sum(IN_SIZES)
OFF_Q, OFF_K, OFF_V, OFF_G, OFF_CQ, OFF_CKV, OFF_KPE = (int(v) for v in np.cumsum((0,) + IN_SIZES[:-1]))

N_GROUPS = 4
EXPERTS_PER_GROUP = 8
N_EXPERTS = N_GROUPS * EXPERTS_PER_GROUP
D_EXPERT = 512
N_ROUTE = N_GROUPS + N_EXPERTS

LOG2_E = float(np.log2(np.e))
ROPE_BASE = 10000.0
EPS = 1e-6
DEPTH = 1
DEEPNORM_ALPHA = (2.0 * DEPTH) ** 0.25

LANES = 128
ROW_TILE = 256
N_PAIRS = 2 * N_TOK
MAX_TILES = N_PAIRS // ROW_TILE + N_EXPERTS
MAX_ROWS = MAX_TILES * ROW_TILE
TOKEN_SUB = D_MODEL // LANES

VMEM_LIMIT = 56 * 1024 * 1024


def _params(sem, vmem=VMEM_LIMIT):
    return pltpu.CompilerParams(dimension_semantics=sem, vmem_limit_bytes=vmem)


def _silu(x):
    return x * (1.0 / (1.0 + jnp.exp(-x)))


def _plain_norm(x):
    mu = jnp.mean(x, -1, keepdims=True)
    xc = x - mu
    var = jnp.mean(xc * xc, -1, keepdims=True)
    return xc * lax.rsqrt(var + EPS)


def _to_token_tiles(x):
    return x.astype(BF16).reshape(x.shape[0], TOKEN_SUB, LANES)


def _from_token_tiles(x):
    return x.reshape(x.shape[0], D_MODEL)


ADA_TN = 1024


ADA_BANDS = 4


def _ada_kernel(cc_ref, *refs):
    w_refs, b_ref, o_ref = refs[:ADA_BANDS], refs[ADA_BANDS], refs[ADA_BANDS + 1]
    s = _silu(cc_ref[...])
    band = D_MODEL // ADA_BANDS
    acc = b_ref[...]
    for q, w_ref in enumerate(w_refs):
        acc = acc + jnp.dot(s[:, q * band:(q + 1) * band], w_ref[...], preferred_element_type=F32,
                            precision=lax.Precision.HIGHEST)
    o_ref[...] = acc


def _ada(cc, w_ada, b_ada):
    n = w_ada.shape[1]
    band = D_MODEL // ADA_BANDS
    return pl.pallas_call(
        _ada_kernel,
        grid=(n // ADA_TN,),
        in_specs=[pl.BlockSpec((8, D_MODEL), lambda j: (0, 0))]
        + [pl.BlockSpec((band, ADA_TN), lambda j, q=q: (q, j)) for q in range(ADA_BANDS)]
        + [pl.BlockSpec((1, ADA_TN), lambda j: (0, j))],
        out_specs=pl.BlockSpec((8, ADA_TN), lambda j: (0, j)),
        out_shape=jax.ShapeDtypeStruct((8, n), F32),
        compiler_params=_params(("arbitrary",)),
        name="ada",
    )(cc, *([w_ada] * ADA_BANDS), b_ada)


INPROJ_TM = 1024
INPROJ_TN = 1024
assert OFF_K == INPROJ_TN and OFF_V == 2 * INPROJ_TN and OFF_CQ == 4 * INPROJ_TN
CTX_OFF_K, CTX_OFF_V = 0, INPROJ_TN
CTX_OFF_CKV = 2 * INPROJ_TN + (OFF_CKV - OFF_CQ)
CTX_OFF_KPE = 2 * INPROJ_TN + (OFF_KPE - OFF_CQ)


def _inproj_kernel(x_ref, shift_ref, scale_ref, w_ref, o_ref, xn_ref):
    @pl.when(pl.program_id(1) == 0)
    def _():
        y = _plain_norm(x_ref[...]) * (1.0 + scale_ref[0]) + shift_ref[0]
        xn_ref[...] = y.astype(BF16)

    o_ref[...] = jnp.dot(xn_ref[...], w_ref[...], preferred_element_type=F32).astype(BF16)


def _inproj(x2d, mod_rows, w_in_bf, rows_per_sample, sample_row0, context):
    n = x2d.shape[0]
    tiles_per_sample = rows_per_sample // INPROJ_TM
    n_blocks = pl.cdiv(IN_W, INPROJ_TN)
    if context:
        grid_n, out_w = 3, 2 * INPROJ_TN + (IN_W - (n_blocks - 1) * INPROJ_TN)
        w_block = lambda i, k: (0, jnp.where(k == 2, n_blocks - 1, k + 1))
    else:
        grid_n, out_w = n_blocks, IN_W
        w_block = lambda i, k: (0, k)

    def mod_map(j):
        return lambda i, k: ((sample_row0 + i // tiles_per_sample) * 6 + j, 0, 0)

    return pl.pallas_call(
        _inproj_kernel,
        grid=(n // INPROJ_TM, grid_n),
        in_specs=[
            pl.BlockSpec((INPROJ_TM, D_MODEL), lambda i, k: (i, 0)),
            pl.BlockSpec((1, 1, D_MODEL), mod_map(0)),
            pl.BlockSpec((1, 1, D_MODEL), mod_map(1)),
            pl.BlockSpec((D_MODEL, INPROJ_TN), w_block),
        ],
        out_specs=pl.BlockSpec((INPROJ_TM, INPROJ_TN), lambda i, k: (i, k)),
        out_shape=jax.ShapeDtypeStruct((n, out_w), BF16),
        scratch_shapes=[pltpu.VMEM((INPROJ_TM, D_MODEL), BF16)],
        compiler_params=_params(("arbitrary", "arbitrary")),
        name="inproj",
    )(x2d, mod_rows, mod_rows, w_in_bf)


def _rope_tables(width):
    half = width // 2
    quarter = half // 2
    inv_freq = ROPE_BASE ** (-np.arange(0, half, 2, dtype=np.float64) / half)
    t = np.arange(SEQ)
    cos_parts, sin_parts = [], []
    for pos in (t // GRID_W, t % GRID_W):
        ang = pos[:, None].astype(np.float64) * inv_freq[None, :]
        c, s = np.cos(ang), np.sin(ang)
        cos_parts += [c, c]
        sin_parts += [-s, s]
    assert cos_parts[0].shape[1] == quarter
    return (np.concatenate(cos_parts, 1).astype(np.float32), np.concatenate(sin_parts, 1).astype(np.float32))


def _rope(x, cos, sin, quarter):
    lane = lax.broadcasted_iota(I32, x.shape, 1)
    first = (lane % (2 * quarter)) < quarter
    swapped = jnp.where(first, pltpu.roll(x, LANES - quarter, 1), pltpu.roll(x, quarter, 1))
    return x * cos + swapped * sin


def _dot_tn(a, b):
    return lax.dot_general(a, b, (((0,), (0,)), ((), ())), preferred_element_type=F32)


def _dot_nt(a, b):
    return lax.dot_general(a, b, (((1,), (1,)), ((), ())), preferred_element_type=F32)


RET_HG = 4


def _retention_kernel(q_ref, k_ref, v_ref, g_ref, kc_ref, vc_ref, cos_ref, sin_ref, df_ref, db_ref, gn_ref,
                      o_ref, qs_ref, ks_ref, st_ref):
    for hh in range(RET_HG):
        _retention_head(hh, slice(hh * LANES, (hh + 1) * LANES), q_ref, k_ref, v_ref, g_ref, kc_ref, vc_ref,
                        cos_ref, sin_ref, df_ref, db_ref, gn_ref, o_ref, qs_ref, ks_ref, st_ref)


def _retention_head(hh, hs, q_ref, k_ref, v_ref, g_ref, kc_ref, vc_ref, cos_ref, sin_ref, df_ref, db_ref, gn_ref,
                    o_ref, qs_ref, ks_ref, st_ref):
    k_scale = RET_DK ** -0.5
    lgf = jax.nn.log_sigmoid(df_ref[hh])
    lgb = jax.nn.log_sigmoid(db_ref[hh])

    cos = cos_ref[...]
    sin = sin_ref[...]
    qs_ref[hh] = _rope(q_ref[:, hs].astype(F32), cos, sin, RET_DK // 4).astype(BF16)
    ks_ref[hh] = _rope(k_ref[:, hs].astype(F32), cos, sin, RET_DK // 4) * k_scale

    rowi = lax.broadcasted_iota(I32, (CHUNK, LANES), 0).astype(F32)
    coli = lax.broadcasted_iota(I32, (CHUNK, LANES), 1).astype(F32)
    diff = rowi - coli
    decay = jnp.exp(jnp.where(diff >= 0, lgf * diff, -lgb * diff)) * jnp.where(diff == 0, 2.0, 1.0)
    zeta_f = jnp.exp(lgf * (CHUNK - 1.0 - rowi))
    eta_b = jnp.exp(lgb * rowi)
    xi_f = jnp.exp(lgf * (rowi + 1.0))
    xi_b = jnp.exp(lgb * (CHUNK - rowi))
    cdec_f = jnp.exp(lgf * float(CHUNK))
    cdec_b = jnp.exp(lgb * float(CHUNK))

    crow = lax.broadcasted_iota(I32, (CTX_LEN, LANES), 0).astype(F32)
    kc = kc_ref[:, hs].astype(F32) * k_scale
    vc = vc_ref[:, hs]
    s_f = _dot_tn((kc * jnp.exp(lgf * (CTX_LEN - 1.0 - crow))).astype(BF16), vc)
    s_b = _dot_tn((kc * jnp.exp(lgb * crow)).astype(BF16), vc)

    upd_f, upd_b = [], []
    for i in range(N_CHUNKS):
        rows = pl.ds(i * CHUNK, CHUNK)
        kch = ks_ref[hh, rows, :]
        vch = v_ref[rows, hs]
        upd_f.append(_dot_tn((kch * zeta_f).astype(BF16), vch))
        upd_b.append(_dot_tn((kch * eta_b).astype(BF16), vch))
    state = s_f
    for i in range(N_CHUNKS):
        st_ref[hh, i, :, :RET_DV] = state.astype(BF16)
        state = cdec_f * state + upd_f[i]
    state = s_b
    for i in reversed(range(N_CHUNKS)):
        st_ref[hh, i, :, RET_DV:] = state.astype(BF16)
        state = cdec_b * state + upd_b[i]

    gn_w = gn_ref[:, hs]
    for i in range(N_CHUNKS):
        rows = pl.ds(i * CHUNK, CHUNK)
        qch = qs_ref[hh, rows, :]
        scores = _dot_nt(qch, ks_ref[hh, rows, :].astype(BF16)) * decay
        o = jnp.dot(scores.astype(BF16), v_ref[rows, hs], preferred_element_type=F32)
        cross = jnp.dot(qch, st_ref[hh, i], preferred_element_type=F32)
        o = o + xi_f * cross[:, :RET_DV] + xi_b * cross[:, RET_DV:]
        y = _plain_norm(o) * gn_w
        o_ref[rows, hs] = (_silu(g_ref[rows, hs].astype(F32)) * y).astype(BF16)


def _retention(proj, proj_c, cos, sin, decay_rows, gn_w):
    width = RET_HG * LANES
    groups = RET_HEADS // RET_HG
    blk = lambda off: pl.BlockSpec((SEQ, width), lambda b, hg: (b, off // width + hg))
    blk_c = lambda off: pl.BlockSpec((CTX_LEN, width), lambda b, hg: (b, off // width + hg))
    table = pl.BlockSpec((SEQ, LANES), lambda b, hg: (0, 0))
    return pl.pallas_call(
        _retention_kernel,
        grid=(BATCH, groups),
        in_specs=[
            blk(OFF_Q), blk(OFF_K), blk(OFF_V), blk(OFF_G), blk_c(CTX_OFF_K), blk_c(CTX_OFF_V), table, table,
            pl.BlockSpec((RET_HG, 1, LANES), lambda b, hg: (hg, 0, 0)),
            pl.BlockSpec((RET_HG, 1, LANES), lambda b, hg: (groups + hg, 0, 0)),
            pl.BlockSpec((1, width), lambda b, hg: (0, hg)),
        ],
        out_specs=pl.BlockSpec((SEQ, width), lambda b, hg: (b, hg)),
        out_shape=jax.ShapeDtypeStruct((N_TOK, RET_W), BF16),
        scratch_shapes=[
            pltpu.VMEM((RET_HG, SEQ, RET_DK), BF16),
            pltpu.VMEM((RET_HG, SEQ, RET_DK), F32),
            pltpu.VMEM((RET_HG, N_CHUNKS, RET_DK, 2 * RET_DV), BF16),
        ],
        compiler_params=_params(("arbitrary", "arbitrary")),
        name="retention",
    )(proj, proj, proj, proj, proj_c, proj_c, cos, sin, decay_rows, decay_rows, gn_w)


MLA_TM = 512


def _rms_norm(x, w):
    return x * lax.rsqrt(jnp.mean(x * x, -1, keepdims=True) + EPS) * w


def _mla_kv(ckv_ref, kpe_ref, kvn_ref, wkv_ref, cos_ref, sin_ref, k_ref, v_ref, rotate):
    ckv = _rms_norm(ckv_ref[...].astype(F32), kvn_ref[...]).astype(BF16)
    kv = jnp.dot(ckv, wkv_ref[...], preferred_element_type=F32)
    lane = lax.broadcasted_iota(I32, (ckv.shape[0], LANES), 1)
    kpe = jnp.where(lane < MLA_ROPE, kpe_ref[...].astype(F32), 0.0)
    if rotate:
        kpe = _rope(kpe, cos_ref[...], sin_ref[...], MLA_ROPE // 4)
    kpe = kpe.astype(BF16)
    for h in range(MLA_HEADS):
        k_ref[:, h * QK_PAD:h * QK_PAD + MLA_NOPE] = kv[:, 2 * h * LANES:(2 * h + 1) * LANES].astype(BF16)
        k_ref[:, h * QK_PAD + MLA_NOPE:(h + 1) * QK_PAD] = kpe
        v_ref[:, h * MLA_DV:(h + 1) * MLA_DV] = kv[:, (2 * h + 1) * LANES:(2 * h + 2) * LANES].astype(BF16)


def _mla_latent_kernel(cq_ref, ckv_ref, kpe_ref, qn_ref, kvn_ref, wq_ref, wkv_ref, cos_ref, sin_ref,
                       q_ref, k_ref, v_ref):
    cq = _rms_norm(cq_ref[...].astype(F32), qn_ref[...]).astype(BF16)
    q = jnp.dot(cq, wq_ref[...], preferred_element_type=F32)
    cos = cos_ref[...]
    sin = sin_ref[...]
    scale = MLA_DQ ** -0.5 * LOG2_E
    for h in range(MLA_HEADS):
        lo = h * QK_PAD
        q_ref[:, lo:lo + MLA_NOPE] = (q[:, lo:lo + MLA_NOPE] * scale).astype(BF16)
        qpe = _rope(q[:, lo + MLA_NOPE:lo + QK_PAD], cos, sin, MLA_ROPE // 4)
        q_ref[:, lo + MLA_NOPE:lo + QK_PAD] = (qpe * scale).astype(BF16)
    _mla_kv(ckv_ref, kpe_ref, kvn_ref, wkv_ref, cos_ref, sin_ref, k_ref, v_ref, rotate=True)


def _mla_context_kernel(ckv_ref, kpe_ref, kvn_ref, wkv_ref, k_ref, v_ref):
    _mla_kv(ckv_ref, kpe_ref, kvn_ref, wkv_ref, None, None, k_ref, v_ref, rotate=False)


def _mla_latent(proj, q_norm, kv_norm, wq_pad, wkv, cos, sin):
    row = lambda w, off: pl.BlockSpec((MLA_TM, w), lambda i: (i, off // w))
    full = lambda a: pl.BlockSpec(a.shape, lambda i: (0, 0))
    table = pl.BlockSpec((MLA_TM, LANES), lambda i: (i % (SEQ // MLA_TM), 0))
    return pl.pallas_call(
        _mla_latent_kernel,
        grid=(N_TOK // MLA_TM,),
        in_specs=[row(MLA_Q_LORA, OFF_CQ), row(MLA_KV_LORA, OFF_CKV), row(LANES, OFF_KPE),
                  full(q_norm), full(kv_norm), full(wq_pad), full(wkv), table, table],
        out_specs=[
            pl.BlockSpec((MLA_TM, MLA_HEADS * QK_PAD), lambda i: (i, 0)),
            pl.BlockSpec((MLA_TM, MLA_HEADS * QK_PAD), lambda i: (i, 0)),
            pl.BlockSpec((MLA_TM, MLA_W), lambda i: (i, 0)),
        ],
        out_shape=[
            jax.ShapeDtypeStruct((N_TOK, MLA_HEADS * QK_PAD), BF16),
            jax.ShapeDtypeStruct((N_TOK, MLA_HEADS * QK_PAD), BF16),
            jax.ShapeDtypeStruct((N_TOK, MLA_W), BF16),
        ],
        compiler_params=_params(("arbitrary",)),
        name="mla_latent",
    )(proj, proj, proj, q_norm, kv_norm, wq_pad, wkv, cos, sin)


def _mla_context(proj_c, kv_norm, wkv):
    row = lambda w, off: pl.BlockSpec((MLA_TM, w), lambda i: (i, off // w))
    full = lambda a: pl.BlockSpec(a.shape, lambda i: (0, 0))
    return pl.pallas_call(
        _mla_context_kernel,
        grid=(N_CTX // MLA_TM,),
        in_specs=[row(MLA_KV_LORA, CTX_OFF_CKV), row(LANES, CTX_OFF_KPE), full(kv_norm), full(wkv)],
        out_specs=[
            pl.BlockSpec((MLA_TM, MLA_HEADS * QK_PAD), lambda i: (i, 0)),
            pl.BlockSpec((MLA_TM, MLA_W), lambda i: (i, 0)),
        ],
        out_shape=[
            jax.ShapeDtypeStruct((N_CTX, MLA_HEADS * QK_PAD), BF16),
            jax.ShapeDtypeStruct((N_CTX, MLA_W), BF16),
        ],
        compiler_params=_params(("arbitrary",)),
        name="mla_context",
    )(proj_c, proj_c, kv_norm, wkv)


ATT_TQ = 256


def _attention_kernel(q_ref, kc_ref, kl_ref, vc_ref, vl_ref, o_ref, vext_ref):
    @pl.when(pl.program_id(1) == 0)
    def _():
        lane = lax.broadcasted_iota(I32, (CTX_LEN + SEQ, MLA_DV), 1)
        ones_col = jnp.where(lane == 0, 1.0, 0.0).astype(BF16)
        for h in range(MLA_HEADS):
            dv = slice(h * MLA_DV, (h + 1) * MLA_DV)
            vext_ref[h, :CTX_LEN, :MLA_DV] = vc_ref[:, dv]
            vext_ref[h, CTX_LEN:, :MLA_DV] = vl_ref[:, dv]
            vext_ref[h, :, MLA_DV:] = ones_col

    for h in range(MLA_HEADS):
        qk = slice(h * QK_PAD, (h + 1) * QK_PAD)
        q = q_ref[:, qk]
        s_c = _dot_nt(q, kc_ref[:, qk])
        s_l = _dot_nt(q, kl_ref[:, qk])
        m = jnp.maximum(jnp.max(s_c, -1, keepdims=True), jnp.max(s_l, -1, keepdims=True))
        p_c = jnp.exp2(s_c - m).astype(BF16)
        p_l = jnp.exp2(s_l - m).astype(BF16)
        o = (jnp.dot(p_c, vext_ref[h, :CTX_LEN, :], preferred_element_type=F32)
             + jnp.dot(p_l, vext_ref[h, CTX_LEN:, :], preferred_element_type=F32))
        o_ref[:, h * MLA_DV:(h + 1) * MLA_DV] = (o[:, :MLA_DV] / o[:, MLA_DV:MLA_DV + 1]).astype(BF16)


def _attention(q, k_ctx, k_lat, v_ctx, v_lat):
    tiles = SEQ // ATT_TQ
    return pl.pallas_call(
        _attention_kernel,
        grid=(BATCH, tiles),
        in_specs=[
            pl.BlockSpec((ATT_TQ, MLA_HEADS * QK_PAD), lambda b, i: (b * tiles + i, 0)),
            pl.BlockSpec((CTX_LEN, MLA_HEADS * QK_PAD), lambda b, i: (b, 0)),
            pl.BlockSpec((SEQ, MLA_HEADS * QK_PAD), lambda b, i: (b, 0)),
            pl.BlockSpec((CTX_LEN, MLA_W), lambda b, i: (b, 0)),
            pl.BlockSpec((SEQ, MLA_W), lambda b, i: (b, 0)),
        ],
        out_specs=pl.BlockSpec((ATT_TQ, MLA_W), lambda b, i: (b * tiles + i, 0)),
        out_shape=jax.ShapeDtypeStruct((N_TOK, MLA_W), BF16),
        scratch_shapes=[pltpu.VMEM((MLA_HEADS, CTX_LEN + SEQ, 2 * MLA_DV), BF16)],
        compiler_params=_params(("arbitrary", "arbitrary")),
        name="attention",
    )(q, k_ctx, k_lat, v_ctx, v_lat)


OUT_TM = 512
OUT_SUB = 256
ROUTE_E1, ROUTE_E2, ROUTE_R1, ROUTE_R2, ROUTE_W1, ROUTE_W2 = range(6)


def _outproj_kernel(ret_ref, att_ref, x_ref, wo_ref, gate_ref, shift_ref, scale_ref, lnw_ref, lnb_ref,
                    wr_ref, br_ref, h_ref, t_ref, route_ref, route_t_ref, count_ref, carry_ref, wsplit_ref):
    @pl.when(pl.program_id(0) == 0)
    def _():
        carry_ref[...] = jnp.zeros_like(carry_ref)
        w = wr_ref[...]
        w_hi = w.astype(BF16)
        wsplit_ref[:, :LANES] = w_hi
        wsplit_ref[:, LANES:] = (w - w_hi.astype(F32)).astype(BF16)

    subtiles = [slice(s * OUT_SUB, (s + 1) * OUT_SUB) for s in range(OUT_TM // OUT_SUB)]
    mixes = [jnp.dot(ret_ref[rows, :], wo_ref[:RET_W, :], preferred_element_type=F32)
             + jnp.dot(att_ref[rows, :], wo_ref[RET_W:, :], preferred_element_type=F32) for rows in subtiles]
    carry = carry_ref[0:1, :]
    for rows, mix in zip(subtiles, mixes):
        carry = _outproj_subtile(rows, mix, carry, x_ref, gate_ref, shift_ref, scale_ref, lnw_ref, lnb_ref, br_ref,
                                 h_ref, t_ref, route_ref, route_t_ref, wsplit_ref)
    carry_ref[...] = jnp.broadcast_to(carry, carry_ref.shape)
    count_ref[...] = jnp.broadcast_to(carry, count_ref.shape)


def _outproj_subtile(rows, mix, carry, x_ref, gate_ref, shift_ref, scale_ref, lnw_ref, lnb_ref, br_ref, h_ref, t_ref,
                     route_ref, route_t_ref, wsplit_ref):
    h = _plain_norm(DEEPNORM_ALPHA * x_ref[rows, :] + gate_ref[0] * mix) * lnw_ref[...] + lnb_ref[...]
    h_ref[rows, :] = h
    t = _plain_norm(h) * (1.0 + scale_ref[0]) + shift_ref[0]
    t_ref[rows] = _to_token_tiles(t)

    t_hi = t.astype(BF16)
    t_lo = (t - t_hi.astype(F32)).astype(BF16)
    main = jnp.dot(t_hi, wsplit_ref[...], preferred_element_type=F32)
    corr = jnp.dot(t_lo, wsplit_ref[:, :LANES], preferred_element_type=F32)
    logits = main[:, :LANES] + (main[:, LANES:] + corr) + br_ref[...]
    lane = lax.broadcasted_iota(I32, logits.shape, 1).astype(F32)
    neg = -jnp.inf
    big = float(LANES)

    def first_lane_of(mask):
        return jnp.min(jnp.where(mask, lane, big), -1, keepdims=True)

    is_group = lane < N_GROUPS
    gl = jnp.where(is_group, logits, neg)
    g_max = jnp.max(gl, -1, keepdims=True)
    g_idx = first_lane_of(is_group & (gl == g_max))
    g_prob = 1.0 / jnp.sum(jnp.where(is_group, jnp.exp(logits - g_max), 0.0), -1, keepdims=True)

    lo = N_GROUPS + g_idx * EXPERTS_PER_GROUP
    in_group = (lane >= lo) & (lane < lo + EXPERTS_PER_GROUP)
    el = jnp.where(in_group, logits, neg)
    v1 = jnp.max(el, -1, keepdims=True)
    i1 = first_lane_of(in_group & (el == v1))
    rest = in_group & (lane != i1)
    el2 = jnp.where(rest, logits, neg)
    v2 = jnp.max(el2, -1, keepdims=True)
    i2 = first_lane_of(rest & (el2 == v2))
    d = jnp.exp(v2 - v1)
    w1 = g_prob / (1.0 + d)
    w2 = g_prob * d / (1.0 + d)

    onehot = jnp.where((lane == i1) | (lane == i2), 1.0, 0.0)
    r = lax.broadcasted_iota(I32, (OUT_SUB, OUT_SUB), 0)
    c = lax.broadcasted_iota(I32, (OUT_SUB, OUT_SUB), 1)
    tri = jnp.where(c < r, 1.0, 0.0).astype(BF16)
    before = jnp.dot(tri, onehot.astype(BF16), preferred_element_type=F32) + carry
    r1 = jnp.sum(jnp.where(lane == i1, before, 0.0), -1, keepdims=True)
    r2 = jnp.sum(jnp.where(lane == i2, before, 0.0), -1, keepdims=True)

    rec = jnp.zeros_like(logits)
    for slot, val in ((ROUTE_E1, i1 - N_GROUPS), (ROUTE_E2, i2 - N_GROUPS), (ROUTE_R1, r1), (ROUTE_R2, r2),
                      (ROUTE_W1, w1), (ROUTE_W2, w2)):
        rec = jnp.where(lane == slot, val, rec)
    route_ref[rows, :] = rec
    route_t_ref[:, rows] = rec.T[:8, :]
    return carry + jnp.sum(onehot, 0, keepdims=True)


def _outproj(ret, att, x2d, wo_bf, mod_rows, ln_w, ln_b, w_route, b_route):
    tiles_per_sample = SEQ // OUT_TM
    row = lambda w: pl.BlockSpec((OUT_TM, w), lambda i: (i, 0))
    full = lambda a: pl.BlockSpec(a.shape, lambda i: (0, 0))
    mod = lambda j: pl.BlockSpec((1, 1, D_MODEL), lambda i: ((i // tiles_per_sample) * 6 + j, 0, 0))
    return pl.pallas_call(
        _outproj_kernel,
        grid=(N_TOK // OUT_TM,),
        in_specs=[row(RET_W), row(MLA_W), row(D_MODEL), full(wo_bf), mod(2), mod(3), mod(4),
                  full(ln_w), full(ln_b), full(w_route), full(b_route)],
        out_specs=[row(D_MODEL), pl.BlockSpec((OUT_TM, TOKEN_SUB, LANES), lambda i: (i, 0, 0)), row(LANES),
                   pl.BlockSpec((8, OUT_TM), lambda i: (0, i)), pl.BlockSpec((8, LANES), lambda i: (0, 0))],
        out_shape=[
            jax.ShapeDtypeStruct((N_TOK, D_MODEL), F32),
            jax.ShapeDtypeStruct((N_TOK, TOKEN_SUB, LANES), BF16),
            jax.ShapeDtypeStruct((N_TOK, LANES), F32),
            jax.ShapeDtypeStruct((8, N_TOK), F32),
            jax.ShapeDtypeStruct((8, LANES), F32),
        ],
        scratch_shapes=[pltpu.VMEM((8, LANES), F32), pltpu.VMEM((D_MODEL, 2 * LANES), BF16)],
        compiler_params=_params(("arbitrary",)),
        name="outproj_route",
    )(ret, att, x2d, wo_bf, mod_rows, mod_rows, mod_rows, ln_w, ln_b, w_route, b_route)


DISPATCH_TM = 256


def _dispatch_kernel(pos_ref, tend_ref, nt_ref, t_ref, xs_ref, zero_ref, sem, zsem):
    i = pl.program_id(0)

    @pl.when(i == 0)
    def _():
        zero_ref[...] = jnp.zeros_like(zero_ref)

        def zero_tile(tile):
            return pltpu.make_async_copy(zero_ref, xs_ref.at[pl.ds(tile * ROW_TILE, ROW_TILE)], zsem)

        def has_tiles(e):
            return tend_ref[e] > (tend_ref[e - 1] if e else 0)

        for e in range(N_EXPERTS):
            pl.when(has_tiles(e))(lambda e=e: zero_tile(tend_ref[e] - 1).start())
        lax.fori_loop(nt_ref[0], MAX_TILES, lambda j, c: (zero_tile(j).start(), c)[1], 0)
        for e in range(N_EXPERTS):
            pl.when(has_tiles(e))(lambda e=e: zero_tile(tend_ref[e] - 1).wait())
        lax.fori_loop(nt_ref[0], MAX_TILES, lambda j, c: (zero_tile(j).wait(), c)[1], 0)

    def copy(slot, j):
        return pltpu.make_async_copy(t_ref.at[j], xs_ref.at[pos_ref[slot * N_TOK + i * DISPATCH_TM + j]], sem)

    for slot in range(2):
        lax.fori_loop(0, DISPATCH_TM, lambda j, c, slot=slot: (copy(slot, j).start(), c)[1], 0, unroll=8)
    for slot in range(2):
        pltpu.make_async_copy(t_ref, xs_ref.at[pl.ds(0, DISPATCH_TM)], sem).wait()


def _dispatch(pos, tile_end, n_tiles, t_tiles):
    return pl.pallas_call(
        _dispatch_kernel,
        grid_spec=pltpu.PrefetchScalarGridSpec(
            num_scalar_prefetch=3,
            grid=(N_TOK // DISPATCH_TM,),
            in_specs=[pl.BlockSpec((DISPATCH_TM, TOKEN_SUB, LANES), lambda i, pos, te, nt: (i, 0, 0))],
            out_specs=pl.BlockSpec(memory_space=pl.ANY),
            scratch_shapes=[pltpu.VMEM((ROW_TILE, TOKEN_SUB, LANES), BF16), pltpu.SemaphoreType.DMA(()),
                            pltpu.SemaphoreType.DMA(())],
        ),
        out_shape=jax.ShapeDtypeStruct((MAX_ROWS, TOKEN_SUB, LANES), BF16),
        compiler_params=_params(("arbitrary",)),
        name="dispatch",
    )(pos, tile_end, n_tiles, t_tiles)


def _experts_kernel(tend_ref, nt_ref, xs_ref, wg_ref, wu_ref, wd_ref, ys_ref, wgu_ref, wdn_ref, xbuf_ref, ybuf_ref,
                    xsem, ysem):
    e = pl.program_id(0)
    n_total = nt_ref[0]
    first = jnp.where(e == 0, 0, tend_ref[jnp.maximum(e - 1, 0)])
    count = tend_ref[e] - first

    def x_copy(g):
        return pltpu.make_async_copy(xs_ref.at[pl.ds(g * ROW_TILE, ROW_TILE)], xbuf_ref.at[g % 2], xsem.at[g % 2])

    def y_copy(g):
        return pltpu.make_async_copy(ybuf_ref.at[g % 2], ys_ref.at[pl.ds(g * ROW_TILE, ROW_TILE)], ysem.at[g % 2])

    @pl.when(e == 0)
    def _():
        x_copy(0).start()

    @pl.when(count > 0)
    def _():
        wgu_ref[:, :D_EXPERT] = wg_ref[0].astype(BF16)
        wgu_ref[:, D_EXPERT:] = wu_ref[0].astype(BF16)
        wdn_ref[...] = wd_ref[0].astype(BF16)

        def tile(j, c):
            g = first + j

            @pl.when(g + 1 < n_total)
            def _():
                x_copy(g + 1).start()

            x_copy(g).wait()
            gu = jnp.dot(_from_token_tiles(xbuf_ref[g % 2]), wgu_ref[...], preferred_element_type=F32)
            hid = _silu(gu[:, :D_EXPERT]) * gu[:, D_EXPERT:]
            y = jnp.dot(hid.astype(BF16), wdn_ref[...], preferred_element_type=F32)

            @pl.when(g >= 2)
            def _():
                y_copy(g - 2).wait()

            ybuf_ref[g % 2] = _to_token_tiles(y)
            y_copy(g).start()
            return c

        lax.fori_loop(0, count, tile, 0)

    @pl.when(e == N_EXPERTS - 1)
    def _():
        @pl.when(n_total >= 2)
        def _():
            y_copy(n_total - 2).wait()

        y_copy(n_total - 1).wait()
        ybuf_ref[0] = jnp.zeros(ybuf_ref.shape[1:], BF16)

        def zero_copy(g):
            return pltpu.make_async_copy(ybuf_ref.at[0], ys_ref.at[pl.ds(g * ROW_TILE, ROW_TILE)], ysem.at[0])

        lax.fori_loop(n_total, MAX_TILES, lambda g, c: (zero_copy(g).start(), c)[1], 0)
        lax.fori_loop(n_total, MAX_TILES, lambda g, c: (zero_copy(g).wait(), c)[1], 0)


def _experts(tile_end, n_tiles, xs, w_gate, w_up, w_down):
    w_gate = w_gate.reshape(N_EXPERTS, D_MODEL, D_EXPERT)
    w_up = w_up.reshape(N_EXPERTS, D_MODEL, D_EXPERT)
    w_down = w_down.reshape(N_EXPERTS, D_EXPERT, D_MODEL)
    expert = lambda e, tend, nt: (e, 0, 0)
    tile_buf = pltpu.VMEM((2, ROW_TILE, TOKEN_SUB, LANES), BF16)
    return pl.pallas_call(
        _experts_kernel,
        grid_spec=pltpu.PrefetchScalarGridSpec(
            num_scalar_prefetch=2,
            grid=(N_EXPERTS,),
            in_specs=[
                pl.BlockSpec(memory_space=pl.ANY),
                pl.BlockSpec((1, D_MODEL, D_EXPERT), expert),
                pl.BlockSpec((1, D_MODEL, D_EXPERT), expert),
                pl.BlockSpec((1, D_EXPERT, D_MODEL), expert),
            ],
            out_specs=pl.BlockSpec(memory_space=pl.ANY),
            scratch_shapes=[pltpu.VMEM((D_MODEL, 2 * D_EXPERT), BF16), pltpu.VMEM((D_EXPERT, D_MODEL), BF16),
                            tile_buf, tile_buf, pltpu.SemaphoreType.DMA((2,)), pltpu.SemaphoreType.DMA((2,))],
        ),
        out_shape=jax.ShapeDtypeStruct((MAX_ROWS, TOKEN_SUB, LANES), BF16),
        compiler_params=_params(("arbitrary",)),
        name="experts",
    )(tile_end, n_tiles, xs, w_gate, w_up, w_down)


COMB_TM = 256


def _combine_kernel(pos_ref, ys_ref, h_ref, route_ref, gate_ref, lnw_ref, lnb_ref, o_ref, buf_ref, sem):
    i = pl.program_id(0)
    n = pl.num_programs(0)

    def copy(step, slot, pair, j):
        src = pos_ref[pair * N_TOK + step * COMB_TM + j]
        return pltpu.make_async_copy(ys_ref.at[src], buf_ref.at[slot, pair * COMB_TM + j], sem.at[slot])

    def start_all(step, slot):
        for pair in range(2):
            lax.fori_loop(0, COMB_TM, lambda j, c, pair=pair: (copy(step, slot, pair, j).start(), c)[1], 0,
                          unroll=8)

    def wait_all(step, slot):
        pltpu.make_async_copy(ys_ref.at[pl.ds(0, 2 * COMB_TM)], buf_ref.at[slot], sem.at[slot]).wait()

    @pl.when(i == 0)
    def _():
        start_all(0, 0)

    @pl.when(i + 1 < n)
    def _():
        start_all(i + 1, (i + 1) % 2)

    slot = i % 2
    wait_all(i, slot)

    y1 = _from_token_tiles(buf_ref[slot, :COMB_TM]).astype(F32)
    y2 = _from_token_tiles(buf_ref[slot, COMB_TM:]).astype(F32)
    y = route_ref[:, ROUTE_W1:ROUTE_W1 + 1] * y1 + route_ref[:, ROUTE_W2:ROUTE_W2 + 1] * y2
    z = DEEPNORM_ALPHA * h_ref[...] + gate_ref[0] * y
    o_ref[...] = _plain_norm(z) * lnw_ref[...] + lnb_ref[...]


def _combine(pos, ys, h, route, mod_rows, ln_w, ln_b):
    tiles_per_sample = SEQ // COMB_TM
    row = lambda w: pl.BlockSpec((COMB_TM, w), lambda i, pos: (i, 0))
    full = lambda a: pl.BlockSpec(a.shape, lambda i, pos: (0, 0))
    return pl.pallas_call(
        _combine_kernel,
        grid_spec=pltpu.PrefetchScalarGridSpec(
            num_scalar_prefetch=1,
            grid=(N_TOK // COMB_TM,),
            in_specs=[
                pl.BlockSpec(memory_space=pl.ANY), row(D_MODEL), row(LANES),
                pl.BlockSpec((1, 1, D_MODEL), lambda i, pos: ((i // tiles_per_sample) * 6 + 5, 0, 0)),
                full(ln_w), full(ln_b),
            ],
            out_specs=row(D_MODEL),
            scratch_shapes=[pltpu.VMEM((2, 2 * COMB_TM, TOKEN_SUB, LANES), BF16), pltpu.SemaphoreType.DMA((2,))],
        ),
        out_shape=jax.ShapeDtypeStruct((N_TOK, D_MODEL), F32),
        compiler_params=_params(("arbitrary",)),
        name="combine",
    )(pos, ys, h, route, mod_rows, ln_w, ln_b)


def _routing_tables(route_t, counts):
    cnt = counts[0, N_GROUPS:N_ROUTE].astype(I32)
    tiles = (cnt + ROW_TILE - 1) // ROW_TILE
    tile_end = jnp.cumsum(tiles)
    n_tiles = tile_end[-1:]
    e = route_t[ROUTE_E1:ROUTE_E2 + 1].astype(I32)
    rank = route_t[ROUTE_R1:ROUTE_R2 + 1].astype(I32)
    earlier = jnp.arange(N_EXPERTS, dtype=I32)[:, None, None] < e[None]
    base = jnp.sum(jnp.where(earlier, (tiles * ROW_TILE)[:, None, None], 0), 0)
    pos = (base + rank).reshape(-1)
    return pos, tile_end, n_tiles


def kernel(x, c, ctx, c_ctx, w_ada, b_ada, w_in, ret_decay, ret_gn_w, mla_q_norm, mla_kv_norm, w_uq, w_ukv, w_o,
           ln1_w, ln1_b, router_group_w, router_group_b, router_expert_w, router_expert_b, expert_w_gate,
           expert_w_up, expert_w_down, ln2_w, ln2_b):
    x2d = x.reshape(N_TOK, D_MODEL)
    ctx2d = ctx.reshape(N_CTX, D_MODEL)

    cc = jnp.zeros((8, D_MODEL), F32).at[:BATCH].set(c).at[BATCH].set(c_ctx)
    mod = _ada(cc, w_ada[0], b_ada)
    mod_rows = mod.reshape(8 * 6, 1, D_MODEL)

    w_in_bf = w_in[0].astype(BF16)
    proj = _inproj(x2d, mod_rows, w_in_bf, SEQ, 0, context=False)
    proj_c = _inproj(ctx2d, mod_rows, w_in_bf, N_CTX, BATCH, context=True)

    cos_r, sin_r = _rope_tables(RET_DK)
    decay_rows = jnp.broadcast_to(ret_decay[0].reshape(2 * RET_HEADS, 1, 1), (2 * RET_HEADS, 1, LANES))
    ret = _retention(proj, proj_c, jnp.asarray(cos_r), jnp.asarray(sin_r), decay_rows, ret_gn_w)

    cos_m, sin_m = _rope_tables(MLA_ROPE)
    cos_m = np.concatenate([cos_m, np.ones_like(cos_m)], 1)
    sin_m = np.concatenate([sin_m, np.zeros_like(sin_m)], 1)
    wq = w_uq[0].reshape(MLA_Q_LORA, MLA_HEADS, MLA_DQ)
    wq_pad = jnp.pad(wq, ((0, 0), (0, 0), (0, QK_PAD - MLA_DQ))).reshape(MLA_Q_LORA, MLA_HEADS * QK_PAD).astype(BF16)
    wkv = w_ukv[0].astype(BF16)
    q, k_lat, v_lat = _mla_latent(proj, mla_q_norm, mla_kv_norm, wq_pad, wkv, jnp.asarray(cos_m), jnp.asarray(sin_m))
    k_ctx, v_ctx = _mla_context(proj_c, mla_kv_norm, wkv)
    att = _attention(q, k_ctx, k_lat, v_ctx, v_lat)

    w_route = jnp.concatenate(
        [router_group_w[0], router_expert_w[0].transpose(1, 0, 2).reshape(D_MODEL, N_EXPERTS),
         jnp.zeros((D_MODEL, LANES - N_ROUTE), F32)], 1)
    b_route = jnp.concatenate(
        [router_group_b[0], router_expert_b[0].reshape(N_EXPERTS), jnp.zeros((LANES - N_ROUTE,), F32)])[None]
    h, t_tiles, route, route_t, counts = _outproj(ret, att, x2d, w_o[0].astype(BF16), mod_rows, ln1_w, ln1_b,
                                                   w_route, b_route)

    pos, tile_end, n_tiles = _routing_tables(route_t, counts)
    xs = _dispatch(pos, tile_end, n_tiles, t_tiles)
    ys = _experts(tile_end, n_tiles, xs, expert_w_gate[0], expert_w_up[0], expert_w_down[0])
    out = _combine(pos, ys, h, route, mod_rows, ln2_w, ln2_b)
    return out.reshape(BATCH, SEQ, D_MODEL)
```

```python
import numpy as np
import jax
import jax.numpy as jnp
from jax import lax
from jax.experimental import pallas as pl
from jax.experimental.pallas import tpu as pltpu

F32 = jnp.float32
BF16 = jnp.bfloat16
I32 = jnp.int32

D_MODEL = 2048
BATCH = 4
SEQ = 2048
GRID_W = 64
CTX_LEN = 256
N_TOK = BATCH * SEQ
N_CTX = BATCH * CTX_LEN

RET_HEADS = 8
RET_DK = 128
RET_DV = 128
RET_W = RET_HEADS * RET_DV
CHUNK = 128
N_CHUNKS = SEQ // CHUNK

MLA_HEADS = 8
MLA_Q_LORA = 512
MLA_KV_LORA = 256
MLA_NOPE = 128
MLA_ROPE = 64
MLA_DV = 128
MLA_W = MLA_HEADS * MLA_DV
MLA_DQ = MLA_NOPE + MLA_ROPE
QK_PAD = 256

IN_SIZES = (RET_HEADS * RET_DK, RET_HEADS * RET_DK, RET_W, RET_W, MLA_Q_LORA, MLA_KV_LORA, MLA_ROPE)
IN_W = sum(IN_SIZES)
OFF_Q, OFF_K, OFF_V, OFF_G, OFF_CQ, OFF_CKV, OFF_KPE = (int(v) for v in np.cumsum((0,) + IN_SIZES[:-1]))

N_GROUPS = 4
EXPERTS_PER_GROUP = 8
N_EXPERTS = N_GROUPS * EXPERTS_PER_GROUP
D_EXPERT = 512
N_ROUTE = N_GROUPS + N_EXPERTS

LOG2_E = float(np.log2(np.e))
ROPE_BASE = 10000.0
EPS = 1e-6
DEPTH = 1
DEEPNORM_ALPHA = (2.0 * DEPTH) ** 0.25

LANES = 128
ROW_TILE = 256
N_PAIRS = 2 * N_TOK
MAX_TILES = N_PAIRS // ROW_TILE + N_EXPERTS
MAX_ROWS = MAX_TILES * ROW_TILE
TOKEN_SUB = D_MODEL // LANES

VMEM_LIMIT = 56 * 1024 * 1024


def _params(sem, vmem=VMEM_LIMIT):
    return pltpu.CompilerParams(dimension_semantics=sem, vmem_limit_bytes=vmem)


def _silu(x):
    return x * (1.0 / (1.0 + jnp.exp(-x)))


def _plain_norm(x):
    mu = jnp.mean(x, -1, keepdims=True)
    xc = x - mu
    var = jnp.mean(xc * xc, -1, keepdims=True)
    return xc * lax.rsqrt(var + EPS)


def _to_token_tiles(x):
    return x.astype(BF16).reshape(x.shape[0], TOKEN_SUB, LANES)


def _from_token_tiles(x):
    return x.reshape(x.shape[0], D_MODEL)


ADA_TN = 1024


ADA_BANDS = 4


def _ada_kernel(cc_ref, *refs):
    w_refs, b_ref, o_ref = refs[:ADA_BANDS], refs[ADA_BANDS], refs[ADA_BANDS + 1]
    s = _silu(cc_ref[...])
    band = D_MODEL // ADA_BANDS
    acc = b_ref[...]
    for q, w_ref in enumerate(w_refs):
        acc = acc + jnp.dot(s[:, q * band:(q + 1) * band], w_ref[...], preferred_element_type=F32,
                            precision=lax.Precision.HIGHEST)
    o_ref[...] = acc


def _ada(cc, w_ada, b_ada):
    n = w_ada.shape[1]
    band = D_MODEL // ADA_BANDS
    return pl.pallas_call(
        _ada_kernel,
        grid=(n // ADA_TN,),
        in_specs=[pl.BlockSpec((8, D_MODEL), lambda j: (0, 0))]
        + [pl.BlockSpec((band, ADA_TN), lambda j, q=q: (q, j)) for q in range(ADA_BANDS)]
        + [pl.BlockSpec((1, ADA_TN), lambda j: (0, j))],
        out_specs=pl.BlockSpec((8, ADA_TN), lambda j: (0, j)),
        out_shape=jax.ShapeDtypeStruct((8, n), F32),
        compiler_params=_params(("arbitrary",)),
        name="ada",
    )(cc, *([w_ada] * ADA_BANDS), b_ada)


INPROJ_TM = 1024
INPROJ_TN = 1024
assert OFF_K == INPROJ_TN and OFF_V == 2 * INPROJ_TN and OFF_CQ == 4 * INPROJ_TN
CTX_OFF_K, CTX_OFF_V = 0, INPROJ_TN
CTX_OFF_CKV = 2 * INPROJ_TN + (OFF_CKV - OFF_CQ)
CTX_OFF_KPE = 2 * INPROJ_TN + (OFF_KPE - OFF_CQ)


INPROJ_SUB = 256


def _inproj_kernel(x_ref, shift_ref, scale_ref, w_ref, o_ref, xn_ref):
    first = pl.program_id(1) == 0

    @pl.when(first)
    def _():
        for s in range(INPROJ_TM // INPROJ_SUB):
            rows = slice(s * INPROJ_SUB, (s + 1) * INPROJ_SUB)
            y = (_plain_norm(x_ref[rows, :]) * (1.0 + scale_ref[0]) + shift_ref[0]).astype(BF16)
            xn_ref[rows, :] = y
            o_ref[rows, :] = jnp.dot(y, w_ref[...], preferred_element_type=F32).astype(BF16)

    @pl.when(jnp.logical_not(first))
    def _():
        o_ref[...] = jnp.dot(xn_ref[...], w_ref[...], preferred_element_type=F32).astype(BF16)


def _inproj(x2d, mod_rows, w_in_bf, rows_per_sample, sample_row0, context):
    n = x2d.shape[0]
    tiles_per_sample = rows_per_sample // INPROJ_TM
    n_blocks = pl.cdiv(IN_W, INPROJ_TN)
    if context:
        grid_n, out_w = 3, 2 * INPROJ_TN + (IN_W - (n_blocks - 1) * INPROJ_TN)
        w_block = lambda i, k: (0, jnp.where(k == 2, n_blocks - 1, k + 1))
    else:
        grid_n, out_w = n_blocks, IN_W
        w_block = lambda i, k: (0, k)

    def mod_map(j):
        return lambda i, k: ((sample_row0 + i // tiles_per_sample) * 6 + j, 0, 0)

    return pl.pallas_call(
        _inproj_kernel,
        grid=(n // INPROJ_TM, grid_n),
        in_specs=[
            pl.BlockSpec((INPROJ_TM, D_MODEL), lambda i, k: (i, 0)),
            pl.BlockSpec((1, 1, D_MODEL), mod_map(0)),
            pl.BlockSpec((1, 1, D_MODEL), mod_map(1)),
            pl.BlockSpec((D_MODEL, INPROJ_TN), w_block),
        ],
        out_specs=pl.BlockSpec((INPROJ_TM, INPROJ_TN), lambda i, k: (i, k)),
        out_shape=jax.ShapeDtypeStruct((n, out_w), BF16),
        scratch_shapes=[pltpu.VMEM((INPROJ_TM, D_MODEL), BF16)],
        compiler_params=_params(("arbitrary", "arbitrary")),
        name="inproj",
    )(x2d, mod_rows, mod_rows, w_in_bf)


def _rope_tables(width):
    half = width // 2
    quarter = half // 2
    inv_freq = ROPE_BASE ** (-np.arange(0, half, 2, dtype=np.float64) / half)
    t = np.arange(SEQ)
    cos_parts, sin_parts = [], []
    for pos in (t // GRID_W, t % GRID_W):
        ang = pos[:, None].astype(np.float64) * inv_freq[None, :]
        c, s = np.cos(ang), np.sin(ang)
        cos_parts += [c, c]
        sin_parts += [-s, s]
    assert cos_parts[0].shape[1] == quarter
    return (np.concatenate(cos_parts, 1).astype(np.float32), np.concatenate(sin_parts, 1).astype(np.float32))


def _rope(x, cos, sin, quarter):
    lane = lax.broadcasted_iota(I32, x.shape, 1)
    first = (lane % (2 * quarter)) < quarter
    swapped = jnp.where(first, pltpu.roll(x, LANES - quarter, 1), pltpu.roll(x, quarter, 1))
    return x * cos + swapped * sin


def _dot_tn(a, b):
    return lax.dot_general(a, b, (((0,), (0,)), ((), ())), preferred_element_type=F32)


def _dot_nt(a, b):
    return lax.dot_general(a, b, (((1,), (1,)), ((), ())), preferred_element_type=F32)


RET_HG = 4


def _retention_kernel(q_ref, k_ref, v_ref, g_ref, kc_ref, vc_ref, cos_ref, sin_ref, df_ref, db_ref, gn_ref,
                      o_ref, qs_ref, ks_ref, st_ref):
    for hh in range(RET_HG):
        _retention_head(hh, slice(hh * LANES, (hh + 1) * LANES), q_ref, k_ref, v_ref, g_ref, kc_ref, vc_ref,
                        cos_ref, sin_ref, df_ref, db_ref, gn_ref, o_ref, qs_ref, ks_ref, st_ref)


def _retention_head(hh, hs, q_ref, k_ref, v_ref, g_ref, kc_ref, vc_ref, cos_ref, sin_ref, df_ref, db_ref, gn_ref,
                    o_ref, qs_ref, ks_ref, st_ref):
    k_scale = RET_DK ** -0.5
    lgf = jax.nn.log_sigmoid(df_ref[hh])
    lgb = jax.nn.log_sigmoid(db_ref[hh])

    cos = cos_ref[...]
    sin = sin_ref[...]
    qs_ref[hh] = _rope(q_ref[:, hs].astype(F32), cos, sin, RET_DK // 4).astype(BF16)
    ks_ref[hh] = _rope(k_ref[:, hs].astype(F32), cos, sin, RET_DK // 4) * k_scale

    rowi = lax.broadcasted_iota(I32, (CHUNK, LANES), 0).astype(F32)
    coli = lax.broadcasted_iota(I32, (CHUNK, LANES), 1).astype(F32)
    diff = rowi - coli
    decay = jnp.exp(jnp.where(diff >= 0, lgf * diff, -lgb * diff)) * jnp.where(diff == 0, 2.0, 1.0)
    zeta_f = jnp.exp(lgf * (CHUNK - 1.0 - rowi))
    eta_b = jnp.exp(lgb * rowi)
    xi_f = jnp.exp(lgf * (rowi + 1.0))
    xi_b = jnp.exp(lgb * (CHUNK - rowi))
    cdec_f = jnp.exp(lgf * float(CHUNK))
    cdec_b = jnp.exp(lgb * float(CHUNK))

    crow = lax.broadcasted_iota(I32, (CTX_LEN, LANES), 0).astype(F32)
    kc = kc_ref[:, hs].astype(F32) * k_scale
    vc = vc_ref[:, hs]
    s_f = _dot_tn((kc * jnp.exp(lgf * (CTX_LEN - 1.0 - crow))).astype(BF16), vc)
    s_b = _dot_tn((kc * jnp.exp(lgb * crow)).astype(BF16), vc)

    upd_f, upd_b = [], []
    for i in range(N_CHUNKS):
        rows = pl.ds(i * CHUNK, CHUNK)
        kch = ks_ref[hh, rows, :]
        vch = v_ref[rows, hs]
        upd_f.append(_dot_tn((kch * zeta_f).astype(BF16), vch))
        upd_b.append(_dot_tn((kch * eta_b).astype(BF16), vch))
    state = s_f
    for i in range(N_CHUNKS):
        st_ref[hh, i, :, :RET_DV] = state.astype(BF16)
        state = cdec_f * state + upd_f[i]
    state = s_b
    for i in reversed(range(N_CHUNKS)):
        st_ref[hh, i, :, RET_DV:] = state.astype(BF16)
        state = cdec_b * state + upd_b[i]

    gn_w = gn_ref[:, hs]
    for i in range(N_CHUNKS):
        rows = pl.ds(i * CHUNK, CHUNK)
        qch = qs_ref[hh, rows, :]
        scores = _dot_nt(qch, ks_ref[hh, rows, :].astype(BF16)) * decay
        o = jnp.dot(scores.astype(BF16), v_ref[rows, hs], preferred_element_type=F32)
        cross = jnp.dot(qch, st_ref[hh, i], preferred_element_type=F32)
        o = o + xi_f * cross[:, :RET_DV] + xi_b * cross[:, RET_DV:]
        y = _plain_norm(o) * gn_w
        o_ref[rows, hs] = (_silu(g_ref[rows, hs].astype(F32)) * y).astype(BF16)


def _retention(proj, proj_c, cos, sin, decay_rows, gn_w):
    width = RET_HG * LANES
    groups = RET_HEADS // RET_HG
    blk = lambda off: pl.BlockSpec((SEQ, width), lambda b, hg: (b, off // width + hg))
    blk_c = lambda off: pl.BlockSpec((CTX_LEN, width), lambda b, hg: (b, off // width + hg))
    table = pl.BlockSpec((SEQ, LANES), lambda b, hg: (0, 0))
    return pl.pallas_call(
        _retention_kernel,
        grid=(BATCH, groups),
        in_specs=[
            blk(OFF_Q), blk(OFF_K), blk(OFF_V), blk(OFF_G), blk_c(CTX_OFF_K), blk_c(CTX_OFF_V), table, table,
            pl.BlockSpec((RET_HG, 1, LANES), lambda b, hg: (hg, 0, 0)),
            pl.BlockSpec((RET_HG, 1, LANES), lambda b, hg: (groups + hg, 0, 0)),
            pl.BlockSpec((1, width), lambda b, hg: (0, hg)),
        ],
        out_specs=pl.BlockSpec((SEQ, width), lambda b, hg: (b, hg)),
        out_shape=jax.ShapeDtypeStruct((N_TOK, RET_W), BF16),
        scratch_shapes=[
            pltpu.VMEM((RET_HG, SEQ, RET_DK), BF16),
            pltpu.VMEM((RET_HG, SEQ, RET_DK), F32),
            pltpu.VMEM((RET_HG, N_CHUNKS, RET_DK, 2 * RET_DV), BF16),
        ],
        compiler_params=_params(("arbitrary", "arbitrary")),
        name="retention",
    )(proj, proj, proj, proj, proj_c, proj_c, cos, sin, decay_rows, decay_rows, gn_w)


MLA_TM = 512


def _rms_norm(x, w):
    return x * lax.rsqrt(jnp.mean(x * x, -1, keepdims=True) + EPS) * w


def _mla_kv(ckv_ref, kpe_ref, kvn_ref, wkv_ref, cos_ref, sin_ref, k_ref, v_ref, rotate):
    ckv = _rms_norm(ckv_ref[...].astype(F32), kvn_ref[...]).astype(BF16)
    kv = jnp.dot(ckv, wkv_ref[...], preferred_element_type=F32)
    lane = lax.broadcasted_iota(I32, (ckv.shape[0], LANES), 1)
    kpe = jnp.where(lane < MLA_ROPE, kpe_ref[...].astype(F32), 0.0)
    if rotate:
        kpe = _rope(kpe, cos_ref[...], sin_ref[...], MLA_ROPE // 4)
    kpe = kpe.astype(BF16)
    for h in range(MLA_HEADS):
        k_ref[:, h * QK_PAD:h * QK_PAD + MLA_NOPE] = kv[:, 2 * h * LANES:(2 * h + 1) * LANES].astype(BF16)
        k_ref[:, h * QK_PAD + MLA_NOPE:(h + 1) * QK_PAD] = kpe
        v_ref[:, h * MLA_DV:(h + 1) * MLA_DV] = kv[:, (2 * h + 1) * LANES:(2 * h + 2) * LANES].astype(BF16)


def _mla_latent_kernel(cq_ref, ckv_ref, kpe_ref, qn_ref, kvn_ref, wq_ref, wkv_ref, cos_ref, sin_ref,
                       q_ref, k_ref, v_ref):
    cq = _rms_norm(cq_ref[...].astype(F32), qn_ref[...]).astype(BF16)
    q = jnp.dot(cq, wq_ref[...], preferred_element_type=F32)
    cos = cos_ref[...]
    sin = sin_ref[...]
    scale = MLA_DQ ** -0.5 * LOG2_E
    for h in range(MLA_HEADS):
        lo = h * QK_PAD
        q_ref[:, lo:lo + MLA_NOPE] = (q[:, lo:lo + MLA_NOPE] * scale).astype(BF16)
        qpe = _rope(q[:, lo + MLA_NOPE:lo + QK_PAD], cos, sin, MLA_ROPE // 4)
        q_ref[:, lo + MLA_NOPE:lo + QK_PAD] = (qpe * scale).astype(BF16)
    _mla_kv(ckv_ref, kpe_ref, kvn_ref, wkv_ref, cos_ref, sin_ref, k_ref, v_ref, rotate=True)


def _mla_context_kernel(ckv_ref, kpe_ref, kvn_ref, wkv_ref, k_ref, v_ref):
    _mla_kv(ckv_ref, kpe_ref, kvn_ref, wkv_ref, None, None, k_ref, v_ref, rotate=False)


def _mla_latent(proj, q_norm, kv_norm, wq_pad, wkv, cos, sin):
    row = lambda w, off: pl.BlockSpec((MLA_TM, w), lambda i: (i, off // w))
    full = lambda a: pl.BlockSpec(a.shape, lambda i: (0, 0))
    table = pl.BlockSpec((MLA_TM, LANES), lambda i: (i % (SEQ // MLA_TM), 0))
    return pl.pallas_call(
        _mla_latent_kernel,
        grid=(N_TOK // MLA_TM,),
        in_specs=[row(MLA_Q_LORA, OFF_CQ), row(MLA_KV_LORA, OFF_CKV), row(LANES, OFF_KPE),
                  full(q_norm), full(kv_norm), full(wq_pad), full(wkv), table, table],
        out_specs=[
            pl.BlockSpec((MLA_TM, MLA_HEADS * QK_PAD), lambda i: (i, 0)),
            pl.BlockSpec((MLA_TM, MLA_HEADS * QK_PAD), lambda i: (i, 0)),
            pl.BlockSpec((MLA_TM, MLA_W), lambda i: (i, 0)),
        ],
        out_shape=[
            jax.ShapeDtypeStruct((N_TOK, MLA_HEADS * QK_PAD), BF16),
            jax.ShapeDtypeStruct((N_TOK, MLA_HEADS * QK_PAD), BF16),
            jax.ShapeDtypeStruct((N_TOK, MLA_W), BF16),
        ],
        compiler_params=_params(("arbitrary",)),
        name="mla_latent",
    )(proj, proj, proj, q_norm, kv_norm, wq_pad, wkv, cos, sin)


def _mla_context(proj_c, kv_norm, wkv):
    row = lambda w, off: pl.BlockSpec((MLA_TM, w), lambda i: (i, off // w))
    full = lambda a: pl.BlockSpec(a.shape, lambda i: (0, 0))
    return pl.pallas_call(
        _mla_context_kernel,
        grid=(N_CTX // MLA_TM,),
        in_specs=[row(MLA_KV_LORA, CTX_OFF_CKV), row(LANES, CTX_OFF_KPE), full(kv_norm), full(wkv)],
        out_specs=[
            pl.BlockSpec((MLA_TM, MLA_HEADS * QK_PAD), lambda i: (i, 0)),
            pl.BlockSpec((MLA_TM, MLA_W), lambda i: (i, 0)),
        ],
        out_shape=[
            jax.ShapeDtypeStruct((N_CTX, MLA_HEADS * QK_PAD), BF16),
            jax.ShapeDtypeStruct((N_CTX, MLA_W), BF16),
        ],
        compiler_params=_params(("arbitrary",)),
        name="mla_context",
    )(proj_c, proj_c, kv_norm, wkv)


ATT_TQ = 256


def _attention_kernel(q_ref, kc_ref, kl_ref, vc_ref, vl_ref, o_ref, vext_ref):
    @pl.when(pl.program_id(1) == 0)
    def _():
        lane = lax.broadcasted_iota(I32, (CTX_LEN + SEQ, MLA_DV), 1)
        ones_col = jnp.where(lane == 0, 1.0, 0.0).astype(BF16)
        for h in range(MLA_HEADS):
            dv = slice(h * MLA_DV, (h + 1) * MLA_DV)
            vext_ref[h, :CTX_LEN, :MLA_DV] = vc_ref[:, dv]
            vext_ref[h, CTX_LEN:, :MLA_DV] = vl_ref[:, dv]
            vext_ref[h, :, MLA_DV:] = ones_col

    for h in range(MLA_HEADS):
        qk = slice(h * QK_PAD, (h + 1) * QK_PAD)
        q = q_ref[:, qk]
        s_c = _dot_nt(q, kc_ref[:, qk])
        s_l = _dot_nt(q, kl_ref[:, qk])
        m = jnp.maximum(jnp.max(s_c, -1, keepdims=True), jnp.max(s_l, -1, keepdims=True))
        p_c = jnp.exp2(s_c - m).astype(BF16)
        p_l = jnp.exp2(s_l - m).astype(BF16)
        o = (jnp.dot(p_c, vext_ref[h, :CTX_LEN, :], preferred_element_type=F32)
             + jnp.dot(p_l, vext_ref[h, CTX_LEN:, :], preferred_element_type=F32))
        o_ref[:, h * MLA_DV:(h + 1) * MLA_DV] = (o[:, :MLA_DV] / o[:, MLA_DV:MLA_DV + 1]).astype(BF16)


def _attention(q, k_ctx, k_lat, v_ctx, v_lat):
    tiles = SEQ // ATT_TQ
    return pl.pallas_call(
        _attention_kernel,
        grid=(BATCH, tiles),
        in_specs=[
            pl.BlockSpec((ATT_TQ, MLA_HEADS * QK_PAD), lambda b, i: (b * tiles + i, 0)),
            pl.BlockSpec((CTX_LEN, MLA_HEADS * QK_PAD), lambda b, i: (b, 0)),
            pl.BlockSpec((SEQ, MLA_HEADS * QK_PAD), lambda b, i: (b, 0)),
            pl.BlockSpec((CTX_LEN, MLA_W), lambda b, i: (b, 0)),
            pl.BlockSpec((SEQ, MLA_W), lambda b, i: (b, 0)),
        ],
        out_specs=pl.BlockSpec((ATT_TQ, MLA_W), lambda b, i: (b * tiles + i, 0)),
        out_shape=jax.ShapeDtypeStruct((N_TOK, MLA_W), BF16),
        scratch_shapes=[pltpu.VMEM((MLA_HEADS, CTX_LEN + SEQ, 2 * MLA_DV), BF16)],
        compiler_params=_params(("arbitrary", "arbitrary")),
        name="attention",
    )(q, k_ctx, k_lat, v_ctx, v_lat)


OUT_TM = 512
OUT_SUB = 256
ROUTE_E1, ROUTE_E2, ROUTE_R1, ROUTE_R2, ROUTE_W1, ROUTE_W2 = range(6)


def _outproj_kernel(ret_ref, att_ref, x_ref, wo_ref, gate_ref, shift_ref, scale_ref, lnw_ref, lnb_ref,
                    wr_ref, br_ref, h_ref, t_ref, route_ref, route_t_ref, count_ref, carry_ref, wsplit_ref):
    @pl.when(pl.program_id(0) == 0)
    def _():
        carry_ref[...] = jnp.zeros_like(carry_ref)
        w = wr_ref[...]
        w_hi = w.astype(BF16)
        wsplit_ref[:, :LANES] = w_hi
        wsplit_ref[:, LANES:] = (w - w_hi.astype(F32)).astype(BF16)

    subtiles = [slice(s * OUT_SUB, (s + 1) * OUT_SUB) for s in range(OUT_TM // OUT_SUB)]
    mixes = [jnp.dot(ret_ref[rows, :], wo_ref[:RET_W, :], preferred_element_type=F32)
             + jnp.dot(att_ref[rows, :], wo_ref[RET_W:, :], preferred_element_type=F32) for rows in subtiles]
    logits = [_outproj_norms(rows, mix, x_ref, gate_ref, shift_ref, scale_ref, lnw_ref, lnb_ref, br_ref, h_ref, t_ref,
                             wsplit_ref) for rows, mix in zip(subtiles, mixes)]
    _route(jnp.concatenate(logits, 0), carry_ref, count_ref, route_ref, route_t_ref)


def _outproj_norms(rows, mix, x_ref, gate_ref, shift_ref, scale_ref, lnw_ref, lnb_ref, br_ref, h_ref, t_ref,
                   wsplit_ref):
    h = _plain_norm(DEEPNORM_ALPHA * x_ref[rows, :] + gate_ref[0] * mix) * lnw_ref[...] + lnb_ref[...]
    h_ref[rows, :] = h
    t = _plain_norm(h) * (1.0 + scale_ref[0]) + shift_ref[0]
    t_ref[rows] = _to_token_tiles(t)

    t_hi = t.astype(BF16)
    t_lo = (t - t_hi.astype(F32)).astype(BF16)
    main = jnp.dot(t_hi, wsplit_ref[...], preferred_element_type=F32)
    corr = jnp.dot(t_lo, wsplit_ref[:, :LANES], preferred_element_type=F32)
    return main[:, :LANES] + (main[:, LANES:] + corr) + br_ref[...]


def _route(logits, carry_ref, count_ref, route_ref, route_t_ref):
    lane = lax.broadcasted_iota(I32, logits.shape, 1).astype(F32)
    neg = -jnp.inf
    big = float(LANES)

    def first_lane_of(mask):
        return jnp.min(jnp.where(mask, lane, big), -1, keepdims=True)

    is_group = lane < N_GROUPS
    gl = jnp.where(is_group, logits, neg)
    g_max = jnp.max(gl, -1, keepdims=True)
    g_idx = first_lane_of(is_group & (gl == g_max))
    g_prob = 1.0 / jnp.sum(jnp.where(is_group, jnp.exp(logits - g_max), 0.0), -1, keepdims=True)

    lo = N_GROUPS + g_idx * EXPERTS_PER_GROUP
    in_group = (lane >= lo) & (lane < lo + EXPERTS_PER_GROUP)
    el = jnp.where(in_group, logits, neg)
    v1 = jnp.max(el, -1, keepdims=True)
    i1 = first_lane_of(in_group & (el == v1))
    rest = in_group & (lane != i1)
    el2 = jnp.where(rest, logits, neg)
    v2 = jnp.max(el2, -1, keepdims=True)
    i2 = first_lane_of(rest & (el2 == v2))
    d = jnp.exp(v2 - v1)
    w1 = g_prob / (1.0 + d)
    w2 = g_prob * d / (1.0 + d)

    onehot = jnp.where((lane == i1) | (lane == i2), 1.0, 0.0)
    r = lax.broadcasted_iota(I32, (OUT_TM, OUT_TM), 0)
    c = lax.broadcasted_iota(I32, (OUT_TM, OUT_TM), 1)
    tri = jnp.where(c < r, 1.0, 0.0).astype(BF16)
    carry = carry_ref[0:1, :]
    before = jnp.dot(tri, onehot.astype(BF16), preferred_element_type=F32) + carry
    r1 = jnp.sum(jnp.where(lane == i1, before, 0.0), -1, keepdims=True)
    r2 = jnp.sum(jnp.where(lane == i2, before, 0.0), -1, keepdims=True)
    carry = carry + jnp.sum(onehot, 0, keepdims=True)
    carry_ref[...] = jnp.broadcast_to(carry, carry_ref.shape)
    count_ref[...] = jnp.broadcast_to(carry, count_ref.shape)

    rec = jnp.zeros_like(logits)
    for slot, val in ((ROUTE_E1, i1 - N_GROUPS), (ROUTE_E2, i2 - N_GROUPS), (ROUTE_R1, r1), (ROUTE_R2, r2),
                      (ROUTE_W1, w1), (ROUTE_W2, w2)):
        rec = jnp.where(lane == slot, val, rec)
    route_ref[...] = rec
    route_t_ref[...] = rec.T[:8, :]


def _outproj(ret, att, x2d, wo_bf, mod_rows, ln_w, ln_b, w_route, b_route):
    tiles_per_sample = SEQ // OUT_TM
    row = lambda w: pl.BlockSpec((OUT_TM, w), lambda i: (i, 0))
    full = lambda a: pl.BlockSpec(a.shape, lambda i: (0, 0))
    mod = lambda j: pl.BlockSpec((1, 1, D_MODEL), lambda i: ((i // tiles_per_sample) * 6 + j, 0, 0))
    return pl.pallas_call(
        _outproj_kernel,
        grid=(N_TOK // OUT_TM,),
        in_specs=[row(RET_W), row(MLA_W), row(D_MODEL), full(wo_bf), mod(2), mod(3), mod(4),
                  full(ln_w), full(ln_b), full(w_route), full(b_route)],
        out_specs=[row(D_MODEL), pl.BlockSpec((OUT_TM, TOKEN_SUB, LANES), lambda i: (i, 0, 0)), row(LANES),
                   pl.BlockSpec((8, OUT_TM), lambda i: (0, i)), pl.BlockSpec((8, LANES), lambda i: (0, 0))],
        out_shape=[
            jax.ShapeDtypeStruct((N_TOK, D_MODEL), F32),
            jax.ShapeDtypeStruct((N_TOK, TOKEN_SUB, LANES), BF16),
            jax.ShapeDtypeStruct((N_TOK, LANES), F32),
            jax.ShapeDtypeStruct((8, N_TOK), F32),
            jax.ShapeDtypeStruct((8, LANES), F32),
        ],
        scratch_shapes=[pltpu.VMEM((8, LANES), F32), pltpu.VMEM((D_MODEL, 2 * LANES), BF16)],
        compiler_params=_params(("arbitrary",)),
        name="outproj_route",
    )(ret, att, x2d, wo_bf, mod_rows, mod_rows, mod_rows, ln_w, ln_b, w_route, b_route)


DISPATCH_TM = 256


def _dispatch_kernel(pos_ref, tend_ref, nt_ref, t_ref, xs_ref, zero_ref, sem, zsem):
    i = pl.program_id(0)

    @pl.when(i == 0)
    def _():
        zero_ref[...] = jnp.zeros_like(zero_ref)

        def zero_tile(tile):
            return pltpu.make_async_copy(zero_ref, xs_ref.at[pl.ds(tile * ROW_TILE, ROW_TILE)], zsem)

        def has_tiles(e):
            return tend_ref[e] > (tend_ref[e - 1] if e else 0)

        for e in range(N_EXPERTS):
            pl.when(has_tiles(e))(lambda e=e: zero_tile(tend_ref[e] - 1).start())
        lax.fori_loop(nt_ref[0], MAX_TILES, lambda j, c: (zero_tile(j).start(), c)[1], 0)
        for e in range(N_EXPERTS):
            pl.when(has_tiles(e))(lambda e=e: zero_tile(tend_ref[e] - 1).wait())
        lax.fori_loop(nt_ref[0], MAX_TILES, lambda j, c: (zero_tile(j).wait(), c)[1], 0)

    def copy(slot, j):
        return pltpu.make_async_copy(t_ref.at[j], xs_ref.at[pos_ref[slot * N_TOK + i * DISPATCH_TM + j]], sem)

    for slot in range(2):
        lax.fori_loop(0, DISPATCH_TM, lambda j, c, slot=slot: (copy(slot, j).start(), c)[1], 0, unroll=8)
    for slot in range(2):
        pltpu.make_async_copy(t_ref, xs_ref.at[pl.ds(0, DISPATCH_TM)], sem).wait()


def _dispatch(pos, tile_end, n_tiles, t_tiles):
    return pl.pallas_call(
        _dispatch_kernel,
        grid_spec=pltpu.PrefetchScalarGridSpec(
            num_scalar_prefetch=3,
            grid=(N_TOK // DISPATCH_TM,),
            in_specs=[pl.BlockSpec((DISPATCH_TM, TOKEN_SUB, LANES), lambda i, pos, te, nt: (i, 0, 0))],
            out_specs=pl.BlockSpec(memory_space=pl.ANY),
            scratch_shapes=[pltpu.VMEM((ROW_TILE, TOKEN_SUB, LANES), BF16), pltpu.SemaphoreType.DMA(()),
                            pltpu.SemaphoreType.DMA(())],
        ),
        out_shape=jax.ShapeDtypeStruct((MAX_ROWS, TOKEN_SUB, LANES), BF16),
        compiler_params=_params(("arbitrary",)),
        name="dispatch",
    )(pos, tile_end, n_tiles, t_tiles)


def _experts_kernel(tend_ref, nt_ref, xs_ref, wg_ref, wu_ref, wd_ref, ys_ref, wgu_ref, wdn_ref, xbuf_ref, ybuf_ref,
                    xsem, ysem):
    e = pl.program_id(0)
    n_total = nt_ref[0]
    first = jnp.where(e == 0, 0, tend_ref[jnp.maximum(e - 1, 0)])
    count = tend_ref[e] - first

    def x_copy(g):
        return pltpu.make_async_copy(xs_ref.at[pl.ds(g * ROW_TILE, ROW_TILE)], xbuf_ref.at[g % 2], xsem.at[g % 2])

    def y_copy(g):
        return pltpu.make_async_copy(ybuf_ref.at[g % 2], ys_ref.at[pl.ds(g * ROW_TILE, ROW_TILE)], ysem.at[g % 2])

    @pl.when(e == 0)
    def _():
        x_copy(0).start()

    @pl.when(count > 0)
    def _():
        wgu_ref[:, :D_EXPERT] = wg_ref[0].astype(BF16)
        wgu_ref[:, D_EXPERT:] = wu_ref[0].astype(BF16)
        wdn_ref[...] = wd_ref[0].astype(BF16)

        def tile(j, c):
            g = first + j

            @pl.when(g + 1 < n_total)
            def _():
                x_copy(g + 1).start()

            x_copy(g).wait()
            gu = jnp.dot(_from_token_tiles(xbuf_ref[g % 2]), wgu_ref[...], preferred_element_type=F32)
            hid = _silu(gu[:, :D_EXPERT]) * gu[:, D_EXPERT:]
            y = jnp.dot(hid.astype(BF16), wdn_ref[...], preferred_element_type=F32)

            @pl.when(g >= 2)
            def _():
                y_copy(g - 2).wait()

            ybuf_ref[g % 2] = _to_token_tiles(y)
            y_copy(g).start()
            return c

        lax.fori_loop(0, count, tile, 0)

    @pl.when(e == N_EXPERTS - 1)
    def _():
        @pl.when(n_total >= 2)
        def _():
            y_copy(n_total - 2).wait()

        y_copy(n_total - 1).wait()
        ybuf_ref[0] = jnp.zeros(ybuf_ref.shape[1:], BF16)

        def zero_copy(g):
            return pltpu.make_async_copy(ybuf_ref.at[0], ys_ref.at[pl.ds(g * ROW_TILE, ROW_TILE)], ysem.at[0])

        lax.fori_loop(n_total, MAX_TILES, lambda g, c: (zero_copy(g).start(), c)[1], 0)
        lax.fori_loop(n_total, MAX_TILES, lambda g, c: (zero_copy(g).wait(), c)[1], 0)


def _experts(tile_end, n_tiles, xs, w_gate, w_up, w_down):
    w_gate = w_gate.reshape(N_EXPERTS, D_MODEL, D_EXPERT)
    w_up = w_up.reshape(N_EXPERTS, D_MODEL, D_EXPERT)
    w_down = w_down.reshape(N_EXPERTS, D_EXPERT, D_MODEL)
    expert = lambda e, tend, nt: (e, 0, 0)
    tile_buf = pltpu.VMEM((2, ROW_TILE, TOKEN_SUB, LANES), BF16)
    return pl.pallas_call(
        _experts_kernel,
        grid_spec=pltpu.PrefetchScalarGridSpec(
            num_scalar_prefetch=2,
            grid=(N_EXPERTS,),
            in_specs=[
                pl.BlockSpec(memory_space=pl.ANY),
                pl.BlockSpec((1, D_MODEL, D_EXPERT), expert),
                pl.BlockSpec((1, D_MODEL, D_EXPERT), expert),
                pl.BlockSpec((1, D_EXPERT, D_MODEL), expert),
            ],
            out_specs=pl.BlockSpec(memory_space=pl.ANY),
            scratch_shapes=[pltpu.VMEM((D_MODEL, 2 * D_EXPERT), BF16), pltpu.VMEM((D_EXPERT, D_MODEL), BF16),
                            tile_buf, tile_buf, pltpu.SemaphoreType.DMA((2,)), pltpu.SemaphoreType.DMA((2,))],
        ),
        out_shape=jax.ShapeDtypeStruct((MAX_ROWS, TOKEN_SUB, LANES), BF16),
        compiler_params=_params(("arbitrary",)),
        name="experts",
    )(tile_end, n_tiles, xs, w_gate, w_up, w_down)


COMB_TM = 256


def _combine_kernel(pos_ref, ys_ref, h_ref, route_ref, gate_ref, lnw_ref, lnb_ref, o_ref, buf_ref, sem):
    i = pl.program_id(0)
    n = pl.num_programs(0)

    def copy(step, slot, pair, j):
        src = pos_ref[pair * N_TOK + step * COMB_TM + j]
        return pltpu.make_async_copy(ys_ref.at[src], buf_ref.at[slot, pair * COMB_TM + j], sem.at[slot])

    def start_all(step, slot):
        for pair in range(2):
            lax.fori_loop(0, COMB_TM, lambda j, c, pair=pair: (copy(step, slot, pair, j).start(), c)[1], 0,
                          unroll=8)

    def wait_all(step, slot):
        pltpu.make_async_copy(ys_ref.at[pl.ds(0, 2 * COMB_TM)], buf_ref.at[slot], sem.at[slot]).wait()

    @pl.when(i == 0)
    def _():
        start_all(0, 0)

    @pl.when(i + 1 < n)
    def _():
        start_all(i + 1, (i + 1) % 2)

    slot = i % 2
    wait_all(i, slot)

    y1 = _from_token_tiles(buf_ref[slot, :COMB_TM]).astype(F32)
    y2 = _from_token_tiles(buf_ref[slot, COMB_TM:]).astype(F32)
    y = route_ref[:, ROUTE_W1:ROUTE_W1 + 1] * y1 + route_ref[:, ROUTE_W2:ROUTE_W2 + 1] * y2
    z = DEEPNORM_ALPHA * h_ref[...] + gate_ref[0] * y
    o_ref[...] = _plain_norm(z) * lnw_ref[...] + lnb_ref[...]


def _combine(pos, ys, h, route, mod_rows, ln_w, ln_b):
    tiles_per_sample = SEQ // COMB_TM
    row = lambda w: pl.BlockSpec((COMB_TM, w), lambda i, pos: (i, 0))
    full = lambda a: pl.BlockSpec(a.shape, lambda i, pos: (0, 0))
    return pl.pallas_call(
        _combine_kernel,
        grid_spec=pltpu.PrefetchScalarGridSpec(
            num_scalar_prefetch=1,
            grid=(N_TOK // COMB_TM,),
            in_specs=[
                pl.BlockSpec(memory_space=pl.ANY), row(D_MODEL), row(LANES),
                pl.BlockSpec((1, 1, D_MODEL), lambda i, pos: ((i // tiles_per_sample) * 6 + 5, 0, 0)),
                full(ln_w), full(ln_b),
            ],
            out_specs=row(D_MODEL),
            scratch_shapes=[pltpu.VMEM((2, 2 * COMB_TM, TOKEN_SUB, LANES), BF16), pltpu.SemaphoreType.DMA((2,))],
        ),
        out_shape=jax.ShapeDtypeStruct((N_TOK, D_MODEL), F32),
        compiler_params=_params(("arbitrary",)),
        name="combine",
    )(pos, ys, h, route, mod_rows, ln_w, ln_b)


def _routing_tables(route_t, counts):
    cnt = counts[0, N_GROUPS:N_ROUTE].astype(I32)
    tiles = (cnt + ROW_TILE - 1) // ROW_TILE
    tile_end = jnp.cumsum(tiles)
    n_tiles = tile_end[-1:]
    e = route_t[ROUTE_E1:ROUTE_E2 + 1].astype(I32)
    rank = route_t[ROUTE_R1:ROUTE_R2 + 1].astype(I32)
    earlier = jnp.arange(N_EXPERTS, dtype=I32)[:, None, None] < e[None]
    base = jnp.sum(jnp.where(earlier, (tiles * ROW_TILE)[:, None, None], 0), 0)
    pos = (base + rank).reshape(-1)
    return pos, tile_end, n_tiles


def kernel(x, c, ctx, c_ctx, w_ada, b_ada, w_in, ret_decay, ret_gn_w, mla_q_norm, mla_kv_norm, w_uq, w_ukv, w_o,
           ln1_w, ln1_b, router_group_w, router_group_b, router_expert_w, router_expert_b, expert_w_gate,
           expert_w_up, expert_w_down, ln2_w, ln2_b):
    x2d = x.reshape(N_TOK, D_MODEL)
    ctx2d = ctx.reshape(N_CTX, D_MODEL)

    cc = jnp.zeros((8, D_MODEL), F32).at[:BATCH].set(c).at[BATCH].set(c_ctx)
    mod = _ada(cc, w_ada[0], b_ada)
    mod_rows = mod.reshape(8 * 6, 1, D_MODEL)

    w_in_bf = w_in[0].astype(BF16)
    proj = _inproj(x2d, mod_rows, w_in_bf, SEQ, 0, context=False)
    proj_c = _inproj(ctx2d, mod_rows, w_in_bf, N_CTX, BATCH, context=True)

    cos_r, sin_r = _rope_tables(RET_DK)
    decay_rows = jnp.broadcast_to(ret_decay[0].reshape(2 * RET_HEADS, 1, 1), (2 * RET_HEADS, 1, LANES))
    ret = _retention(proj, proj_c, jnp.asarray(cos_r), jnp.asarray(sin_r), decay_rows, ret_gn_w)

    cos_m, sin_m = _rope_tables(MLA_ROPE)
    cos_m = np.concatenate([cos_m, np.ones_like(cos_m)], 1)
    sin_m = np.concatenate([sin_m, np.zeros_like(sin_m)], 1)
    wq = w_uq[0].reshape(MLA_Q_LORA, MLA_HEADS, MLA_DQ)
    wq_pad = jnp.pad(wq, ((0, 0), (0, 0), (0, QK_PAD - MLA_DQ))).reshape(MLA_Q_LORA, MLA_HEADS * QK_PAD).astype(BF16)
    wkv = w_ukv[0].astype(BF16)
    q, k_lat, v_lat = _mla_latent(proj, mla_q_norm, mla_kv_norm, wq_pad, wkv, jnp.asarray(cos_m), jnp.asarray(sin_m))
    k_ctx, v_ctx = _mla_context(proj_c, mla_kv_norm, wkv)
    att = _attention(q, k_ctx, k_lat, v_ctx, v_lat)

    w_route = jnp.concatenate(
        [router_group_w[0], router_expert_w[0].transpose(1, 0, 2).reshape(D_MODEL, N_EXPERTS),
         jnp.zeros((D_MODEL, LANES - N_ROUTE), F32)], 1)
    b_route = jnp.concatenate(
        [router_group_b[0], router_expert_b[0].reshape(N_EXPERTS), jnp.zeros((LANES - N_ROUTE,), F32)])[None]
    h, t_tiles, route, route_t, counts = _outproj(ret, att, x2d, w_o[0].astype(BF16), mod_rows, ln1_w, ln1_b,
                                                   w_route, b_route)

    pos, tile_end, n_tiles = _routing_tables(route_t, counts)
    xs = _dispatch(pos, tile_end, n_tiles, t_tiles)
    ys = _experts(tile_end, n_tiles, xs, expert_w_gate[0], expert_w_up[0], expert_w_down[0])
    out = _combine(pos, ys, h, route, mod_rows, ln2_w, ln2_b)
    return out.reshape(BATCH, SEQ, D_MODEL)
```

```python
import numpy as np
import jax
import jax.numpy as jnp
from jax import lax
from jax.experimental import pallas as pl
from jax.experimental.pallas import tpu as pltpu

F32 = jnp.float32
BF16 = jnp.bfloat16
I32 = jnp.int32

D_MODEL = 2048
BATCH = 4
SEQ = 2048
GRID_W = 64
CTX_LEN = 256
N_TOK = BATCH * SEQ
N_CTX = BATCH * CTX_LEN

RET_HEADS = 8
RET_DK = 128
RET_DV = 128
RET_W = RET_HEADS * RET_DV
CHUNK = 128
N_CHUNKS = SEQ // CHUNK

MLA_HEADS = 8
MLA_Q_LORA = 512
MLA_KV_LORA = 256
MLA_NOPE = 128
MLA_ROPE = 64
MLA_DV = 128
MLA_W = MLA_HEADS * MLA_DV
MLA_DQ = MLA_NOPE + MLA_ROPE
QK_PAD = 256

IN_SIZES = (RET_HEADS * RET_DK, RET_HEADS * RET_DK, RET_W, RET_W, MLA_Q_LORA, MLA_KV_LORA, MLA_ROPE)
IN_W = sum(IN_SIZES)
OFF_Q, OFF_K, OFF_V, OFF_G, OFF_CQ, OFF_CKV, OFF_KPE = (int(v) for v in np.cumsum((0,) + IN_SIZES[:-1]))

N_GROUPS = 4
EXPERTS_PER_GROUP = 8
N_EXPERTS = N_GROUPS * EXPERTS_PER_GROUP
D_EXPERT = 512
N_ROUTE = N_GROUPS + N_EXPERTS

LOG2_E = float(np.log2(np.e))
ROPE_BASE = 10000.0
EPS = 1e-6
DEPTH = 1
DEEPNORM_ALPHA = (2.0 * DEPTH) ** 0.25

LANES = 128
ROW_TILE = 256
N_PAIRS = 2 * N_TOK
N_TILES = N_PAIRS // ROW_TILE
TOKEN_SUB = D_MODEL // LANES

VMEM_LIMIT = 56 * 1024 * 1024


def _params(sem, vmem=VMEM_LIMIT):
    return pltpu.CompilerParams(dimension_semantics=sem, vmem_limit_bytes=vmem)


def _silu(x):
    return x * (1.0 / (1.0 + jnp.exp(-x)))


def _plain_norm(x):
    mu = jnp.mean(x, -1, keepdims=True)
    xc = x - mu
    var = jnp.mean(xc * xc, -1, keepdims=True)
    return xc * lax.rsqrt(var + EPS)


def _to_token_tiles(x):
    return x.astype(BF16).reshape(x.shape[0], TOKEN_SUB, LANES)


def _from_token_tiles(x):
    return x.reshape(x.shape[0], D_MODEL)


ADA_TN = 1024


ADA_BANDS = 4


def _ada_kernel(cc_ref, *refs):
    w_refs, b_ref, o_ref = refs[:ADA_BANDS], refs[ADA_BANDS], refs[ADA_BANDS + 1]
    s = _silu(cc_ref[...])
    band = D_MODEL // ADA_BANDS
    acc = b_ref[...]
    for q, w_ref in enumerate(w_refs):
        acc = acc + jnp.dot(s[:, q * band:(q + 1) * band], w_ref[...], preferred_element_type=F32,
                            precision=lax.Precision.HIGHEST)
    o_ref[...] = acc


def _ada(cc, w_ada, b_ada):
    n = w_ada.shape[1]
    band = D_MODEL // ADA_BANDS
    return pl.pallas_call(
        _ada_kernel,
        grid=(n // ADA_TN,),
        in_specs=[pl.BlockSpec((8, D_MODEL), lambda j: (0, 0))]
        + [pl.BlockSpec((band, ADA_TN), lambda j, q=q: (q, j)) for q in range(ADA_BANDS)]
        + [pl.BlockSpec((1, ADA_TN), lambda j: (0, j))],
        out_specs=pl.BlockSpec((8, ADA_TN), lambda j: (0, j)),
        out_shape=jax.ShapeDtypeStruct((8, n), F32),
        compiler_params=_params(("arbitrary",)),
        name="ada",
    )(cc, *([w_ada] * ADA_BANDS), b_ada)


INPROJ_TM = 1024
INPROJ_TN = 1024
assert OFF_K == INPROJ_TN and OFF_V == 2 * INPROJ_TN and OFF_CQ == 4 * INPROJ_TN
CTX_OFF_K, CTX_OFF_V = 0, INPROJ_TN
CTX_OFF_CKV = 2 * INPROJ_TN + (OFF_CKV - OFF_CQ)
CTX_OFF_KPE = 2 * INPROJ_TN + (OFF_KPE - OFF_CQ)


INPROJ_SUB = 256


def _inproj_kernel(x_ref, shift_ref, scale_ref, w_ref, o_ref, xn_ref):
    first = pl.program_id(1) == 0

    @pl.when(first)
    def _():
        for s in range(INPROJ_TM // INPROJ_SUB):
            rows = slice(s * INPROJ_SUB, (s + 1) * INPROJ_SUB)
            y = (_plain_norm(x_ref[rows, :]) * (1.0 + scale_ref[0]) + shift_ref[0]).astype(BF16)
            xn_ref[rows, :] = y
            o_ref[rows, :] = jnp.dot(y, w_ref[...], preferred_element_type=F32).astype(BF16)

    @pl.when(jnp.logical_not(first))
    def _():
        o_ref[...] = jnp.dot(xn_ref[...], w_ref[...], preferred_element_type=F32).astype(BF16)


def _inproj(x2d, mod_rows, w_in_bf, rows_per_sample, sample_row0, context):
    n = x2d.shape[0]
    tiles_per_sample = rows_per_sample // INPROJ_TM
    n_blocks = pl.cdiv(IN_W, INPROJ_TN)
    if context:
        grid_n, out_w = 3, 2 * INPROJ_TN + (IN_W - (n_blocks - 1) * INPROJ_TN)
        w_block = lambda i, k: (0, jnp.where(k == 2, n_blocks - 1, k + 1))
    else:
        grid_n, out_w = n_blocks, IN_W
        w_block = lambda i, k: (0, k)

    def mod_map(j):
        return lambda i, k: ((sample_row0 + i // tiles_per_sample) * 6 + j, 0, 0)

    return pl.pallas_call(
        _inproj_kernel,
        grid=(n // INPROJ_TM, grid_n),
        in_specs=[
            pl.BlockSpec((INPROJ_TM, D_MODEL), lambda i, k: (i, 0)),
            pl.BlockSpec((1, 1, D_MODEL), mod_map(0)),
            pl.BlockSpec((1, 1, D_MODEL), mod_map(1)),
            pl.BlockSpec((D_MODEL, INPROJ_TN), w_block),
        ],
        out_specs=pl.BlockSpec((INPROJ_TM, INPROJ_TN), lambda i, k: (i, k)),
        out_shape=jax.ShapeDtypeStruct((n, out_w), BF16),
        scratch_shapes=[pltpu.VMEM((INPROJ_TM, D_MODEL), BF16)],
        compiler_params=_params(("arbitrary", "arbitrary")),
        name="inproj",
    )(x2d, mod_rows, mod_rows, w_in_bf)


def _rope_tables(width):
    half = width // 2
    quarter = half // 2
    inv_freq = ROPE_BASE ** (-np.arange(0, half, 2, dtype=np.float64) / half)
    t = np.arange(SEQ)
    cos_parts, sin_parts = [], []
    for pos in (t // GRID_W, t % GRID_W):
        ang = pos[:, None].astype(np.float64) * inv_freq[None, :]
        c, s = np.cos(ang), np.sin(ang)
        cos_parts += [c, c]
        sin_parts += [-s, s]
    assert cos_parts[0].shape[1] == quarter
    return (np.concatenate(cos_parts, 1).astype(np.float32), np.concatenate(sin_parts, 1).astype(np.float32))


def _rope(x, cos, sin, quarter):
    lane = lax.broadcasted_iota(I32, x.shape, 1)
    first = (lane % (2 * quarter)) < quarter
    swapped = jnp.where(first, pltpu.roll(x, LANES - quarter, 1), pltpu.roll(x, quarter, 1))
    return x * cos + swapped * sin


def _dot_tn(a, b):
    return lax.dot_general(a, b, (((0,), (0,)), ((), ())), preferred_element_type=F32)


def _dot_nt(a, b):
    return lax.dot_general(a, b, (((1,), (1,)), ((), ())), preferred_element_type=F32)


RET_HG = 4


def _retention_kernel(q_ref, k_ref, v_ref, g_ref, kc_ref, vc_ref, cos_ref, sin_ref, df_ref, db_ref, gn_ref,
                      o_ref, qs_ref, ks_ref, st_ref):
    for hh in range(RET_HG):
        _retention_head(hh, slice(hh * LANES, (hh + 1) * LANES), q_ref, k_ref, v_ref, g_ref, kc_ref, vc_ref,
                        cos_ref, sin_ref, df_ref, db_ref, gn_ref, o_ref, qs_ref, ks_ref, st_ref)


def _retention_head(hh, hs, q_ref, k_ref, v_ref, g_ref, kc_ref, vc_ref, cos_ref, sin_ref, df_ref, db_ref, gn_ref,
                    o_ref, qs_ref, ks_ref, st_ref):
    k_scale = RET_DK ** -0.5
    lgf = jax.nn.log_sigmoid(df_ref[hh])
    lgb = jax.nn.log_sigmoid(db_ref[hh])

    cos = cos_ref[...]
    sin = sin_ref[...]
    qs_ref[hh] = _rope(q_ref[:, hs].astype(F32), cos, sin, RET_DK // 4).astype(BF16)
    ks_ref[hh] = _rope(k_ref[:, hs].astype(F32), cos, sin, RET_DK // 4) * k_scale

    rowi = lax.broadcasted_iota(I32, (CHUNK, LANES), 0).astype(F32)
    coli = lax.broadcasted_iota(I32, (CHUNK, LANES), 1).astype(F32)
    diff = rowi - coli
    decay = jnp.exp(jnp.where(diff >= 0, lgf * diff, -lgb * diff)) * jnp.where(diff == 0, 2.0, 1.0)
    zeta_f = jnp.exp(lgf * (CHUNK - 1.0 - rowi))
    eta_b = jnp.exp(lgb * rowi)
    xi_f = jnp.exp(lgf * (rowi + 1.0))
    xi_b = jnp.exp(lgb * (CHUNK - rowi))
    cdec_f = jnp.exp(lgf * float(CHUNK))
    cdec_b = jnp.exp(lgb * float(CHUNK))

    crow = lax.broadcasted_iota(I32, (CTX_LEN, LANES), 0).astype(F32)
    kc = kc_ref[:, hs].astype(F32) * k_scale
    vc = vc_ref[:, hs]
    s_f = _dot_tn((kc * jnp.exp(lgf * (CTX_LEN - 1.0 - crow))).astype(BF16), vc)
    s_b = _dot_tn((kc * jnp.exp(lgb * crow)).astype(BF16), vc)

    upd_f, upd_b = [], []
    for i in range(N_CHUNKS):
        rows = pl.ds(i * CHUNK, CHUNK)
        kch = ks_ref[hh, rows, :]
        vch = v_ref[rows, hs]
        upd_f.append(_dot_tn((kch * zeta_f).astype(BF16), vch))
        upd_b.append(_dot_tn((kch * eta_b).astype(BF16), vch))
    state = s_f
    for i in range(N_CHUNKS):
        st_ref[hh, i, :, :RET_DV] = state.astype(BF16)
        state = cdec_f * state + upd_f[i]
    state = s_b
    for i in reversed(range(N_CHUNKS)):
        st_ref[hh, i, :, RET_DV:] = state.astype(BF16)
        state = cdec_b * state + upd_b[i]

    gn_w = gn_ref[:, hs]
    for i in range(N_CHUNKS):
        rows = pl.ds(i * CHUNK, CHUNK)
        qch = qs_ref[hh, rows, :]
        scores = _dot_nt(qch, ks_ref[hh, rows, :].astype(BF16)) * decay
        o = jnp.dot(scores.astype(BF16), v_ref[rows, hs], preferred_element_type=F32)
        cross = jnp.dot(qch, st_ref[hh, i], preferred_element_type=F32)
        o = o + xi_f * cross[:, :RET_DV] + xi_b * cross[:, RET_DV:]
        y = _plain_norm(o) * gn_w
        o_ref[rows, hs] = (_silu(g_ref[rows, hs].astype(F32)) * y).astype(BF16)


def _retention(proj, proj_c, cos, sin, decay_rows, gn_w):
    width = RET_HG * LANES
    groups = RET_HEADS // RET_HG
    blk = lambda off: pl.BlockSpec((SEQ, width), lambda b, hg: (b, off // width + hg))
    blk_c = lambda off: pl.BlockSpec((CTX_LEN, width), lambda b, hg: (b, off // width + hg))
    table = pl.BlockSpec((SEQ, LANES), lambda b, hg: (0, 0))
    return pl.pallas_call(
        _retention_kernel,
        grid=(BATCH, groups),
        in_specs=[
            blk(OFF_Q), blk(OFF_K), blk(OFF_V), blk(OFF_G), blk_c(CTX_OFF_K), blk_c(CTX_OFF_V), table, table,
            pl.BlockSpec((RET_HG, 1, LANES), lambda b, hg: (hg, 0, 0)),
            pl.BlockSpec((RET_HG, 1, LANES), lambda b, hg: (groups + hg, 0, 0)),
            pl.BlockSpec((1, width), lambda b, hg: (0, hg)),
        ],
        out_specs=pl.BlockSpec((SEQ, width), lambda b, hg: (b, hg)),
        out_shape=jax.ShapeDtypeStruct((N_TOK, RET_W), BF16),
        scratch_shapes=[
            pltpu.VMEM((RET_HG, SEQ, RET_DK), BF16),
            pltpu.VMEM((RET_HG, SEQ, RET_DK), F32),
            pltpu.VMEM((RET_HG, N_CHUNKS, RET_DK, 2 * RET_DV), BF16),
        ],
        compiler_params=_params(("arbitrary", "arbitrary")),
        name="retention",
    )(proj, proj, proj, proj, proj_c, proj_c, cos, sin, decay_rows, decay_rows, gn_w)


MLA_TM = 512


def _rms_norm(x, w):
    return x * lax.rsqrt(jnp.mean(x * x, -1, keepdims=True) + EPS) * w


def _mla_kv(ckv_ref, kpe_ref, kvn_ref, wkv_ref, cos_ref, sin_ref, k_ref, v_ref, rotate):
    ckv = _rms_norm(ckv_ref[...].astype(F32), kvn_ref[...]).astype(BF16)
    kv = jnp.dot(ckv, wkv_ref[...], preferred_element_type=F32)
    lane = lax.broadcasted_iota(I32, (ckv.shape[0], LANES), 1)
    kpe = jnp.where(lane < MLA_ROPE, kpe_ref[...].astype(F32), 0.0)
    if rotate:
        kpe = _rope(kpe, cos_ref[...], sin_ref[...], MLA_ROPE // 4)
    kpe = kpe.astype(BF16)
    for h in range(MLA_HEADS):
        k_ref[:, h * QK_PAD:h * QK_PAD + MLA_NOPE] = kv[:, 2 * h * LANES:(2 * h + 1) * LANES].astype(BF16)
        k_ref[:, h * QK_PAD + MLA_NOPE:(h + 1) * QK_PAD] = kpe
        v_ref[:, h * MLA_DV:(h + 1) * MLA_DV] = kv[:, (2 * h + 1) * LANES:(2 * h + 2) * LANES].astype(BF16)


def _mla_latent_kernel(cq_ref, ckv_ref, kpe_ref, qn_ref, kvn_ref, wq_ref, wkv_ref, cos_ref, sin_ref,
                       q_ref, k_ref, v_ref):
    cq = _rms_norm(cq_ref[...].astype(F32), qn_ref[...]).astype(BF16)
    q = jnp.dot(cq, wq_ref[...], preferred_element_type=F32)
    cos = cos_ref[...]
    sin = sin_ref[...]
    scale = MLA_DQ ** -0.5 * LOG2_E
    for h in range(MLA_HEADS):
        lo = h * QK_PAD
        q_ref[:, lo:lo + MLA_NOPE] = (q[:, lo:lo + MLA_NOPE] * scale).astype(BF16)
        qpe = _rope(q[:, lo + MLA_NOPE:lo + QK_PAD], cos, sin, MLA_ROPE // 4)
        q_ref[:, lo + MLA_NOPE:lo + QK_PAD] = (qpe * scale).astype(BF16)
    _mla_kv(ckv_ref, kpe_ref, kvn_ref, wkv_ref, cos_ref, sin_ref, k_ref, v_ref, rotate=True)


def _mla_context_kernel(ckv_ref, kpe_ref, kvn_ref, wkv_ref, k_ref, v_ref):
    _mla_kv(ckv_ref, kpe_ref, kvn_ref, wkv_ref, None, None, k_ref, v_ref, rotate=False)


def _mla_latent(proj, q_norm, kv_norm, wq_pad, wkv, cos, sin):
    row = lambda w, off: pl.BlockSpec((MLA_TM, w), lambda i: (i, off // w))
    full = lambda a: pl.BlockSpec(a.shape, lambda i: (0, 0))
    table = pl.BlockSpec((MLA_TM, LANES), lambda i: (i % (SEQ // MLA_TM), 0))
    return pl.pallas_call(
        _mla_latent_kernel,
        grid=(N_TOK // MLA_TM,),
        in_specs=[row(MLA_Q_LORA, OFF_CQ), row(MLA_KV_LORA, OFF_CKV), row(LANES, OFF_KPE),
                  full(q_norm), full(kv_norm), full(wq_pad), full(wkv), table, table],
        out_specs=[
            pl.BlockSpec((MLA_TM, MLA_HEADS * QK_PAD), lambda i: (i, 0)),
            pl.BlockSpec((MLA_TM, MLA_HEADS * QK_PAD), lambda i: (i, 0)),
            pl.BlockSpec((MLA_TM, MLA_W), lambda i: (i, 0)),
        ],
        out_shape=[
            jax.ShapeDtypeStruct((N_TOK, MLA_HEADS * QK_PAD), BF16),
            jax.ShapeDtypeStruct((N_TOK, MLA_HEADS * QK_PAD), BF16),
            jax.ShapeDtypeStruct((N_TOK, MLA_W), BF16),
        ],
        compiler_params=_params(("arbitrary",)),
        name="mla_latent",
    )(proj, proj, proj, q_norm, kv_norm, wq_pad, wkv, cos, sin)


def _mla_context(proj_c, kv_norm, wkv):
    row = lambda w, off: pl.BlockSpec((MLA_TM, w), lambda i: (i, off // w))
    full = lambda a: pl.BlockSpec(a.shape, lambda i: (0, 0))
    return pl.pallas_call(
        _mla_context_kernel,
        grid=(N_CTX // MLA_TM,),
        in_specs=[row(MLA_KV_LORA, CTX_OFF_CKV), row(LANES, CTX_OFF_KPE), full(kv_norm), full(wkv)],
        out_specs=[
            pl.BlockSpec((MLA_TM, MLA_HEADS * QK_PAD), lambda i: (i, 0)),
            pl.BlockSpec((MLA_TM, MLA_W), lambda i: (i, 0)),
        ],
        out_shape=[
            jax.ShapeDtypeStruct((N_CTX, MLA_HEADS * QK_PAD), BF16),
            jax.ShapeDtypeStruct((N_CTX, MLA_W), BF16),
        ],
        compiler_params=_params(("arbitrary",)),
        name="mla_context",
    )(proj_c, proj_c, kv_norm, wkv)


ATT_TQ = 256


def _attention_kernel(q_ref, kc_ref, kl_ref, vc_ref, vl_ref, o_ref, vext_ref):
    @pl.when(pl.program_id(1) == 0)
    def _():
        lane = lax.broadcasted_iota(I32, (CTX_LEN + SEQ, MLA_DV), 1)
        ones_col = jnp.where(lane == 0, 1.0, 0.0).astype(BF16)
        for h in range(MLA_HEADS):
            dv = slice(h * MLA_DV, (h + 1) * MLA_DV)
            vext_ref[h, :CTX_LEN, :MLA_DV] = vc_ref[:, dv]
            vext_ref[h, CTX_LEN:, :MLA_DV] = vl_ref[:, dv]
            vext_ref[h, :, MLA_DV:] = ones_col

    for h in range(MLA_HEADS):
        qk = slice(h * QK_PAD, (h + 1) * QK_PAD)
        q = q_ref[:, qk]
        s_c = _dot_nt(q, kc_ref[:, qk])
        s_l = _dot_nt(q, kl_ref[:, qk])
        m = jnp.maximum(jnp.max(s_c, -1, keepdims=True), jnp.max(s_l, -1, keepdims=True))
        p_c = jnp.exp2(s_c - m).astype(BF16)
        p_l = jnp.exp2(s_l - m).astype(BF16)
        o = (jnp.dot(p_c, vext_ref[h, :CTX_LEN, :], preferred_element_type=F32)
             + jnp.dot(p_l, vext_ref[h, CTX_LEN:, :], preferred_element_type=F32))
        o_ref[:, h * MLA_DV:(h + 1) * MLA_DV] = (o[:, :MLA_DV] / o[:, MLA_DV:MLA_DV + 1]).astype(BF16)


def _attention(q, k_ctx, k_lat, v_ctx, v_lat):
    tiles = SEQ // ATT_TQ
    return pl.pallas_call(
        _attention_kernel,
        grid=(BATCH, tiles),
        in_specs=[
            pl.BlockSpec((ATT_TQ, MLA_HEADS * QK_PAD), lambda b, i: (b * tiles + i, 0)),
            pl.BlockSpec((CTX_LEN, MLA_HEADS * QK_PAD), lambda b, i: (b, 0)),
            pl.BlockSpec((SEQ, MLA_HEADS * QK_PAD), lambda b, i: (b, 0)),
            pl.BlockSpec((CTX_LEN, MLA_W), lambda b, i: (b, 0)),
            pl.BlockSpec((SEQ, MLA_W), lambda b, i: (b, 0)),
        ],
        out_specs=pl.BlockSpec((ATT_TQ, MLA_W), lambda b, i: (b * tiles + i, 0)),
        out_shape=jax.ShapeDtypeStruct((N_TOK, MLA_W), BF16),
        scratch_shapes=[pltpu.VMEM((MLA_HEADS, CTX_LEN + SEQ, 2 * MLA_DV), BF16)],
        compiler_params=_params(("arbitrary", "arbitrary")),
        name="attention",
    )(q, k_ctx, k_lat, v_ctx, v_lat)


OUT_TM = 512
OUT_SUB = 256
ROUTE_E1, ROUTE_E2, ROUTE_R1, ROUTE_R2, ROUTE_W1, ROUTE_W2 = range(6)


def _outproj_kernel(ret_ref, att_ref, x_ref, wo_ref, gate_ref, shift_ref, scale_ref, lnw_ref, lnb_ref,
                    wr_ref, br_ref, h_ref, t_ref, route_ref, route_t_ref, count_ref, carry_ref, wsplit_ref):
    @pl.when(pl.program_id(0) == 0)
    def _():
        carry_ref[...] = jnp.zeros_like(carry_ref)
        w = wr_ref[...]
        w_hi = w.astype(BF16)
        wsplit_ref[:, :LANES] = w_hi
        wsplit_ref[:, LANES:] = (w - w_hi.astype(F32)).astype(BF16)

    subtiles = [slice(s * OUT_SUB, (s + 1) * OUT_SUB) for s in range(OUT_TM // OUT_SUB)]
    mixes = [jnp.dot(ret_ref[rows, :], wo_ref[:RET_W, :], preferred_element_type=F32)
             + jnp.dot(att_ref[rows, :], wo_ref[RET_W:, :], preferred_element_type=F32) for rows in subtiles]
    logits = [_outproj_norms(rows, mix, x_ref, gate_ref, shift_ref, scale_ref, lnw_ref, lnb_ref, br_ref, h_ref, t_ref,
                             wsplit_ref) for rows, mix in zip(subtiles, mixes)]
    _route(jnp.concatenate(logits, 0), carry_ref, count_ref, route_ref, route_t_ref)


def _outproj_norms(rows, mix, x_ref, gate_ref, shift_ref, scale_ref, lnw_ref, lnb_ref, br_ref, h_ref, t_ref,
                   wsplit_ref):
    h = _plain_norm(DEEPNORM_ALPHA * x_ref[rows, :] + gate_ref[0] * mix) * lnw_ref[...] + lnb_ref[...]
    h_ref[rows, :] = h
    t = _plain_norm(h) * (1.0 + scale_ref[0]) + shift_ref[0]
    t_ref[rows] = _to_token_tiles(t)

    t_hi = t.astype(BF16)
    t_lo = (t - t_hi.astype(F32)).astype(BF16)
    main = jnp.dot(t_hi, wsplit_ref[...], preferred_element_type=F32)
    corr = jnp.dot(t_lo, wsplit_ref[:, :LANES], preferred_element_type=F32)
    return main[:, :LANES] + (main[:, LANES:] + corr) + br_ref[...]


def _route(logits, carry_ref, count_ref, route_ref, route_t_ref):
    lane = lax.broadcasted_iota(I32, logits.shape, 1).astype(F32)
    neg = -jnp.inf
    big = float(LANES)

    def first_lane_of(mask):
        return jnp.min(jnp.where(mask, lane, big), -1, keepdims=True)

    is_group = lane < N_GROUPS
    gl = jnp.where(is_group, logits, neg)
    g_max = jnp.max(gl, -1, keepdims=True)
    g_idx = first_lane_of(is_group & (gl == g_max))
    g_prob = 1.0 / jnp.sum(jnp.where(is_group, jnp.exp(logits - g_max), 0.0), -1, keepdims=True)

    lo = N_GROUPS + g_idx * EXPERTS_PER_GROUP
    in_group = (lane >= lo) & (lane < lo + EXPERTS_PER_GROUP)
    el = jnp.where(in_group, logits, neg)
    v1 = jnp.max(el, -1, keepdims=True)
    i1 = first_lane_of(in_group & (el == v1))
    rest = in_group & (lane != i1)
    el2 = jnp.where(rest, logits, neg)
    v2 = jnp.max(el2, -1, keepdims=True)
    i2 = first_lane_of(rest & (el2 == v2))
    d = jnp.exp(v2 - v1)
    w1 = g_prob / (1.0 + d)
    w2 = g_prob * d / (1.0 + d)

    onehot = jnp.where((lane == i1) | (lane == i2), 1.0, 0.0)
    r = lax.broadcasted_iota(I32, (OUT_TM, OUT_TM), 0)
    c = lax.broadcasted_iota(I32, (OUT_TM, OUT_TM), 1)
    tri = jnp.where(c < r, 1.0, 0.0).astype(BF16)
    carry = carry_ref[0:1, :]
    before = jnp.dot(tri, onehot.astype(BF16), preferred_element_type=F32) + carry
    r1 = jnp.sum(jnp.where(lane == i1, before, 0.0), -1, keepdims=True)
    r2 = jnp.sum(jnp.where(lane == i2, before, 0.0), -1, keepdims=True)
    carry = carry + jnp.sum(onehot, 0, keepdims=True)
    carry_ref[...] = jnp.broadcast_to(carry, carry_ref.shape)
    count_ref[...] = jnp.broadcast_to(carry, count_ref.shape)

    rec = jnp.zeros_like(logits)
    for slot, val in ((ROUTE_E1, i1 - N_GROUPS), (ROUTE_E2, i2 - N_GROUPS), (ROUTE_R1, r1), (ROUTE_R2, r2),
                      (ROUTE_W1, w1), (ROUTE_W2, w2)):
        rec = jnp.where(lane == slot, val, rec)
    route_ref[...] = rec
    route_t_ref[...] = rec.T[:8, :]


def _outproj(ret, att, x2d, wo_bf, mod_rows, ln_w, ln_b, w_route, b_route):
    tiles_per_sample = SEQ // OUT_TM
    row = lambda w: pl.BlockSpec((OUT_TM, w), lambda i: (i, 0))
    full = lambda a: pl.BlockSpec(a.shape, lambda i: (0, 0))
    mod = lambda j: pl.BlockSpec((1, 1, D_MODEL), lambda i: ((i // tiles_per_sample) * 6 + j, 0, 0))
    return pl.pallas_call(
        _outproj_kernel,
        grid=(N_TOK // OUT_TM,),
        in_specs=[row(RET_W), row(MLA_W), row(D_MODEL), full(wo_bf), mod(2), mod(3), mod(4),
                  full(ln_w), full(ln_b), full(w_route), full(b_route)],
        out_specs=[row(D_MODEL), pl.BlockSpec((OUT_TM, TOKEN_SUB, LANES), lambda i: (i, 0, 0)), row(LANES),
                   pl.BlockSpec((8, OUT_TM), lambda i: (0, i)), pl.BlockSpec((8, LANES), lambda i: (0, 0))],
        out_shape=[
            jax.ShapeDtypeStruct((N_TOK, D_MODEL), F32),
            jax.ShapeDtypeStruct((N_TOK, TOKEN_SUB, LANES), BF16),
            jax.ShapeDtypeStruct((N_TOK, LANES), F32),
            jax.ShapeDtypeStruct((8, N_TOK), F32),
            jax.ShapeDtypeStruct((8, LANES), F32),
        ],
        scratch_shapes=[pltpu.VMEM((8, LANES), F32), pltpu.VMEM((D_MODEL, 2 * LANES), BF16)],
        compiler_params=_params(("arbitrary",)),
        name="outproj_route",
    )(ret, att, x2d, wo_bf, mod_rows, mod_rows, mod_rows, ln_w, ln_b, w_route, b_route)


DISPATCH_TM = 256


def _dispatch_kernel(pos_ref, t_ref, xs_ref, sem):
    i = pl.program_id(0)

    def copy(slot, j):
        return pltpu.make_async_copy(t_ref.at[j], xs_ref.at[pos_ref[slot * N_TOK + i * DISPATCH_TM + j]], sem)

    for slot in range(2):
        lax.fori_loop(0, DISPATCH_TM, lambda j, c, slot=slot: (copy(slot, j).start(), c)[1], 0, unroll=8)
    for slot in range(2):
        pltpu.make_async_copy(t_ref, xs_ref.at[pl.ds(0, DISPATCH_TM)], sem).wait()


def _dispatch(pos, t_tiles):
    return pl.pallas_call(
        _dispatch_kernel,
        grid_spec=pltpu.PrefetchScalarGridSpec(
            num_scalar_prefetch=1,
            grid=(N_TOK // DISPATCH_TM,),
            in_specs=[pl.BlockSpec((DISPATCH_TM, TOKEN_SUB, LANES), lambda i, pos: (i, 0, 0))],
            out_specs=pl.BlockSpec(memory_space=pl.ANY),
            scratch_shapes=[pltpu.SemaphoreType.DMA(())],
        ),
        out_shape=jax.ShapeDtypeStruct((N_PAIRS, TOKEN_SUB, LANES), BF16),
        compiler_params=_params(("arbitrary",)),
        name="dispatch",
    )(pos, t_tiles)


def _experts_kernel(rend_ref, xs_ref, wg_ref, wu_ref, wd_ref, ys_ref, wgu_ref, wdn_ref, xbuf_ref, ybuf_ref,
                    xsem, ysem):
    e = pl.program_id(0)
    lo = jnp.where(e == 0, 0, rend_ref[jnp.maximum(e - 1, 0)])
    hi = rend_ref[e]

    def x_copy(g):
        return pltpu.make_async_copy(xs_ref.at[pl.ds(g * ROW_TILE, ROW_TILE)], xbuf_ref.at[g % 2], xsem.at[g % 2])

    def y_copy(g):
        return pltpu.make_async_copy(ybuf_ref.at[g % 2], ys_ref.at[pl.ds(g * ROW_TILE, ROW_TILE)], ysem.at[g % 2])

    @pl.when(e == 0)
    def _():
        x_copy(0).start()

    @pl.when(hi > lo)
    def _():
        wgu_ref[:, :D_EXPERT] = wg_ref[0].astype(BF16)
        wgu_ref[:, D_EXPERT:] = wu_ref[0].astype(BF16)
        wdn_ref[...] = wd_ref[0].astype(BF16)

        def tile(g, c):
            row0 = g * ROW_TILE
            owns_first_row = lo <= row0
            owns_last_row = hi >= row0 + ROW_TILE

            @pl.when(owns_first_row)
            def _():
                @pl.when(g + 1 < N_TILES)
                def _():
                    x_copy(g + 1).start()

                x_copy(g).wait()

            gu = jnp.dot(_from_token_tiles(xbuf_ref[g % 2]), wgu_ref[...], preferred_element_type=F32)
            hid = _silu(gu[:, :D_EXPERT]) * gu[:, D_EXPERT:]
            y = jnp.dot(hid.astype(BF16), wdn_ref[...], preferred_element_type=F32)

            @pl.when(owns_first_row)
            def _():
                @pl.when(g >= 2)
                def _():
                    y_copy(g - 2).wait()

                ybuf_ref[g % 2] = _to_token_tiles(y)

            @pl.when(jnp.logical_not(owns_first_row))
            def _():
                row = lax.broadcasted_iota(I32, y.shape, 0)
                earlier = _from_token_tiles(ybuf_ref[g % 2]).astype(F32)
                ybuf_ref[g % 2] = _to_token_tiles(jnp.where(row >= lo - row0, y, earlier))

            @pl.when(owns_last_row)
            def _():
                y_copy(g).start()

            return c

        lax.fori_loop(lo // ROW_TILE, (hi - 1) // ROW_TILE + 1, tile, 0)

    @pl.when(e == N_EXPERTS - 1)
    def _():
        y_copy(N_TILES - 2).wait()
        y_copy(N_TILES - 1).wait()


def _experts(row_end, xs, w_gate, w_up, w_down):
    w_gate = w_gate.reshape(N_EXPERTS, D_MODEL, D_EXPERT)
    w_up = w_up.reshape(N_EXPERTS, D_MODEL, D_EXPERT)
    w_down = w_down.reshape(N_EXPERTS, D_EXPERT, D_MODEL)
    expert = lambda e, rend: (e, 0, 0)
    tile_buf = pltpu.VMEM((2, ROW_TILE, TOKEN_SUB, LANES), BF16)
    return pl.pallas_call(
        _experts_kernel,
        grid_spec=pltpu.PrefetchScalarGridSpec(
            num_scalar_prefetch=1,
            grid=(N_EXPERTS,),
            in_specs=[
                pl.BlockSpec(memory_space=pl.ANY),
                pl.BlockSpec((1, D_MODEL, D_EXPERT), expert),
                pl.BlockSpec((1, D_MODEL, D_EXPERT), expert),
                pl.BlockSpec((1, D_EXPERT, D_MODEL), expert),
            ],
            out_specs=pl.BlockSpec(memory_space=pl.ANY),
            scratch_shapes=[pltpu.VMEM((D_MODEL, 2 * D_EXPERT), BF16), pltpu.VMEM((D_EXPERT, D_MODEL), BF16),
                            tile_buf, tile_buf, pltpu.SemaphoreType.DMA((2,)), pltpu.SemaphoreType.DMA((2,))],
        ),
        out_shape=jax.ShapeDtypeStruct((N_PAIRS, TOKEN_SUB, LANES), BF16),
        compiler_params=_params(("arbitrary",)),
        name="experts",
    )(row_end, xs, w_gate, w_up, w_down)


COMB_TM = 256


def _combine_kernel(pos_ref, ys_ref, h_ref, route_ref, gate_ref, lnw_ref, lnb_ref, o_ref, buf_ref, sem):
    i = pl.program_id(0)
    n = pl.num_programs(0)

    def copy(step, slot, pair, j):
        src = pos_ref[pair * N_TOK + step * COMB_TM + j]
        return pltpu.make_async_copy(ys_ref.at[src], buf_ref.at[slot, pair * COMB_TM + j], sem.at[slot])

    def start_all(step, slot):
        for pair in range(2):
            lax.fori_loop(0, COMB_TM, lambda j, c, pair=pair: (copy(step, slot, pair, j).start(), c)[1], 0,
                          unroll=8)

    def wait_all(step, slot):
        pltpu.make_async_copy(ys_ref.at[pl.ds(0, 2 * COMB_TM)], buf_ref.at[slot], sem.at[slot]).wait()

    @pl.when(i == 0)
    def _():
        start_all(0, 0)

    @pl.when(i + 1 < n)
    def _():
        start_all(i + 1, (i + 1) % 2)

    slot = i % 2
    wait_all(i, slot)

    y1 = _from_token_tiles(buf_ref[slot, :COMB_TM]).astype(F32)
    y2 = _from_token_tiles(buf_ref[slot, COMB_TM:]).astype(F32)
    y = route_ref[:, ROUTE_W1:ROUTE_W1 + 1] * y1 + route_ref[:, ROUTE_W2:ROUTE_W2 + 1] * y2
    z = DEEPNORM_ALPHA * h_ref[...] + gate_ref[0] * y
    o_ref[...] = _plain_norm(z) * lnw_ref[...] + lnb_ref[...]


def _combine(pos, ys, h, route, mod_rows, ln_w, ln_b):
    tiles_per_sample = SEQ // COMB_TM
    row = lambda w: pl.BlockSpec((COMB_TM, w), lambda i, pos: (i, 0))
    full = lambda a: pl.BlockSpec(a.shape, lambda i, pos: (0, 0))
    return pl.pallas_call(
        _combine_kernel,
        grid_spec=pltpu.PrefetchScalarGridSpec(
            num_scalar_prefetch=1,
            grid=(N_TOK // COMB_TM,),
            in_specs=[
                pl.BlockSpec(memory_space=pl.ANY), row(D_MODEL), row(LANES),
                pl.BlockSpec((1, 1, D_MODEL), lambda i, pos: ((i // tiles_per_sample) * 6 + 5, 0, 0)),
                full(ln_w), full(ln_b),
            ],
            out_specs=row(D_MODEL),
            scratch_shapes=[pltpu.VMEM((2, 2 * COMB_TM, TOKEN_SUB, LANES), BF16), pltpu.SemaphoreType.DMA((2,))],
        ),
        out_shape=jax.ShapeDtypeStruct((N_TOK, D_MODEL), F32),
        compiler_params=_params(("arbitrary",)),
        name="combine",
    )(pos, ys, h, route, mod_rows, ln_w, ln_b)


def _routing_tables(route_t, counts):
    cnt = counts[0, N_GROUPS:N_ROUTE].astype(I32)
    row_end = jnp.cumsum(cnt)
    e = route_t[ROUTE_E1:ROUTE_E2 + 1].astype(I32)
    rank = route_t[ROUTE_R1:ROUTE_R2 + 1].astype(I32)
    earlier = jnp.arange(N_EXPERTS, dtype=I32)[:, None, None] < e[None]
    base = jnp.sum(jnp.where(earlier, cnt[:, None, None], 0), 0)
    pos = (base + rank).reshape(-1)
    return pos, row_end


def kernel(x, c, ctx, c_ctx, w_ada, b_ada, w_in, ret_decay, ret_gn_w, mla_q_norm, mla_kv_norm, w_uq, w_ukv, w_o,
           ln1_w, ln1_b, router_group_w, router_group_b, router_expert_w, router_expert_b, expert_w_gate,
           expert_w_up, expert_w_down, ln2_w, ln2_b):
    x2d = x.reshape(N_TOK, D_MODEL)
    ctx2d = ctx.reshape(N_CTX, D_MODEL)

    cc = jnp.zeros((8, D_MODEL), F32).at[:BATCH].set(c).at[BATCH].set(c_ctx)
    mod = _ada(cc, w_ada[0], b_ada)
    mod_rows = mod.reshape(8 * 6, 1, D_MODEL)

    w_in_bf = w_in[0].astype(BF16)
    proj = _inproj(x2d, mod_rows, w_in_bf, SEQ, 0, context=False)
    proj_c = _inproj(ctx2d, mod_rows, w_in_bf, N_CTX, BATCH, context=True)

    cos_r, sin_r = _rope_tables(RET_DK)
    decay_rows = jnp.broadcast_to(ret_decay[0].reshape(2 * RET_HEADS, 1, 1), (2 * RET_HEADS, 1, LANES))
    ret = _retention(proj, proj_c, jnp.asarray(cos_r), jnp.asarray(sin_r), decay_rows, ret_gn_w)

    cos_m, sin_m = _rope_tables(MLA_ROPE)
    cos_m = np.concatenate([cos_m, np.ones_like(cos_m)], 1)
    sin_m = np.concatenate([sin_m, np.zeros_like(sin_m)], 1)
    wq = w_uq[0].reshape(MLA_Q_LORA, MLA_HEADS, MLA_DQ)
    wq_pad = jnp.pad(wq, ((0, 0), (0, 0), (0, QK_PAD - MLA_DQ))).reshape(MLA_Q_LORA, MLA_HEADS * QK_PAD).astype(BF16)
    wkv = w_ukv[0].astype(BF16)
    q, k_lat, v_lat = _mla_latent(proj, mla_q_norm, mla_kv_norm, wq_pad, wkv, jnp.asarray(cos_m), jnp.asarray(sin_m))
    k_ctx, v_ctx = _mla_context(proj_c, mla_kv_norm, wkv)
    att = _attention(q, k_ctx, k_lat, v_ctx, v_lat)

    w_route = jnp.concatenate(
        [router_group_w[0], router_expert_w[0].transpose(1, 0, 2).reshape(D_MODEL, N_EXPERTS),
         jnp.zeros((D_MODEL, LANES - N_ROUTE), F32)], 1)
    b_route = jnp.concatenate(
        [router_group_b[0], router_expert_b[0].reshape(N_EXPERTS), jnp.zeros((LANES - N_ROUTE,), F32)])[None]
    h, t_tiles, route, route_t, counts = _outproj(ret, att, x2d, w_o[0].astype(BF16), mod_rows, ln1_w, ln1_b,
                                                   w_route, b_route)

    pos, row_end = _routing_tables(route_t, counts)
    xs = _dispatch(pos, t_tiles)
    ys = _experts(row_end, xs, expert_w_gate[0], expert_w_up[0], expert_w_down[0])
    out = _combine(pos, ys, h, route, mod_rows, ln2_w, ln2_b)
    return out.reshape(BATCH, SEQ, D_MODEL)
```

```python
import numpy as np
import jax
import jax.numpy as jnp
from jax import lax
from jax.experimental import pallas as pl
from jax.experimental.pallas import tpu as pltpu

F32 = jnp.float32
BF16 = jnp.bfloat16
I32 = jnp.int32

D_MODEL = 2048
BATCH = 4
SEQ = 2048
GRID_W = 64
CTX_LEN = 256
N_TOK = BATCH * SEQ
N_CTX = BATCH * CTX_LEN

RET_HEADS = 8
RET_DK = 128
RET_DV = 128
RET_W = RET_HEADS * RET_DV
CHUNK = 128
N_CHUNKS = SEQ // CHUNK

MLA_HEADS = 8
MLA_Q_LORA = 512
MLA_KV_LORA = 256
MLA_NOPE = 128
MLA_ROPE = 64
MLA_DV = 128
MLA_W = MLA_HEADS * MLA_DV
MLA_DQ = MLA_NOPE + MLA_ROPE
QK_PAD = 256

IN_SIZES = (RET_HEADS * RET_DK, RET_HEADS * RET_DK, RET_W, RET_W, MLA_Q_LORA, MLA_KV_LORA, MLA_ROPE)
IN_W = sum(IN_SIZES)
OFF_Q, OFF_K, OFF_V, OFF_G, OFF_CQ, OFF_CKV, OFF_KPE = (int(v) for v in np.cumsum((0,) + IN_SIZES[:-1]))

N_GROUPS = 4
EXPERTS_PER_GROUP = 8
N_EXPERTS = N_GROUPS * EXPERTS_PER_GROUP
D_EXPERT = 512
N_ROUTE = N_GROUPS + N_EXPERTS

LOG2_E = float(np.log2(np.e))
ROPE_BASE = 10000.0
EPS = 1e-6
DEPTH = 1
DEEPNORM_ALPHA = (2.0 * DEPTH) ** 0.25

LANES = 128
ROW_TILE = 256
N_PAIRS = 2 * N_TOK
N_TILES = N_PAIRS // ROW_TILE
TOKEN_SUB = D_MODEL // LANES

VMEM_LIMIT = 56 * 1024 * 1024


def _params(sem, vmem=VMEM_LIMIT):
    return pltpu.CompilerParams(dimension_semantics=sem, vmem_limit_bytes=vmem)


def _silu(x):
    return x * (1.0 / (1.0 + jnp.exp(-x)))


def _plain_norm(x):
    mu = jnp.mean(x, -1, keepdims=True)
    xc = x - mu
    var = jnp.mean(xc * xc, -1, keepdims=True)
    return xc * lax.rsqrt(var + EPS)


def _to_token_tiles(x):
    return x.astype(BF16).reshape(x.shape[0], TOKEN_SUB, LANES)


def _from_token_tiles(x):
    return x.reshape(x.shape[0], D_MODEL)


ADA_TN = 1024


ADA_BANDS = 4


def _ada_kernel(cc_ref, *refs):
    w_refs, b_ref, o_ref = refs[:ADA_BANDS], refs[ADA_BANDS], refs[ADA_BANDS + 1]
    s = _silu(cc_ref[...])
    band = D_MODEL // ADA_BANDS
    acc = b_ref[...]
    for q, w_ref in enumerate(w_refs):
        acc = acc + jnp.dot(s[:, q * band:(q + 1) * band], w_ref[...], preferred_element_type=F32,
                            precision=lax.Precision.HIGHEST)
    o_ref[...] = acc


def _ada(cc, w_ada, b_ada):
    n = w_ada.shape[1]
    band = D_MODEL // ADA_BANDS
    return pl.pallas_call(
        _ada_kernel,
        grid=(n // ADA_TN,),
        in_specs=[pl.BlockSpec((8, D_MODEL), lambda j: (0, 0))]
        + [pl.BlockSpec((band, ADA_TN), lambda j, q=q: (q, j)) for q in range(ADA_BANDS)]
        + [pl.BlockSpec((1, ADA_TN), lambda j: (0, j))],
        out_specs=pl.BlockSpec((8, ADA_TN), lambda j: (0, j)),
        out_shape=jax.ShapeDtypeStruct((8, n), F32),
        compiler_params=_params(("arbitrary",)),
        name="ada",
    )(cc, *([w_ada] * ADA_BANDS), b_ada)


INPROJ_TM = 1024
INPROJ_TN = 1024
assert OFF_K == INPROJ_TN and OFF_V == 2 * INPROJ_TN and OFF_CQ == 4 * INPROJ_TN
CTX_OFF_K, CTX_OFF_V = 0, INPROJ_TN
CTX_OFF_CKV = 2 * INPROJ_TN + (OFF_CKV - OFF_CQ)
CTX_OFF_KPE = 2 * INPROJ_TN + (OFF_KPE - OFF_CQ)


INPROJ_SUB = 256
INPROJ_BLOCKS = -(-IN_W // INPROJ_TN)


def _inproj_products(x_ref, shift_ref, scale_ref, w, o_ref, xn_ref):
    first = pl.program_id(1) == 0

    @pl.when(first)
    def _():
        for s in range(INPROJ_TM // INPROJ_SUB):
            rows = slice(s * INPROJ_SUB, (s + 1) * INPROJ_SUB)
            y = (_plain_norm(x_ref[rows, :]) * (1.0 + scale_ref[0]) + shift_ref[0]).astype(BF16)
            xn_ref[rows, :] = y
            o_ref[rows, :] = jnp.dot(y, w[...], preferred_element_type=F32).astype(BF16)

    @pl.when(jnp.logical_not(first))
    def _():
        o_ref[...] = jnp.dot(xn_ref[...], w[...], preferred_element_type=F32).astype(BF16)


def _inproj_latent_kernel(x_ref, shift_ref, scale_ref, wf_ref, o_ref, wbf_ref, xn_ref, wbuf_ref, in_sem, out_sem):
    i = pl.program_id(0)
    k = pl.program_id(1)
    t = i * INPROJ_BLOCKS + k
    slot = t % 2
    n_steps = pl.num_programs(0) * INPROJ_BLOCKS

    def out_copy(block, s):
        return pltpu.make_async_copy(wbuf_ref.at[s], wbf_ref.at[block], out_sem.at[s])

    def in_copy(block, s):
        return pltpu.make_async_copy(wbf_ref.at[block], wbuf_ref.at[s], in_sem.at[s])

    @pl.when(i == 0)
    def _():
        @pl.when(k >= 2)
        def _():
            out_copy(k - 2, slot).wait()

        col = lax.broadcasted_iota(I32, (D_MODEL, INPROJ_TN), 1)
        wbuf_ref[slot] = jnp.where(col < IN_W - k * INPROJ_TN, wf_ref[...], 0.0).astype(BF16)
        out_copy(k, slot).start()

    @pl.when((t + 1 >= INPROJ_BLOCKS) & (t + 1 < n_steps))
    def _():
        @pl.when(t == INPROJ_BLOCKS - 1)
        def _():
            out_copy(INPROJ_BLOCKS - 2, 1 - slot).wait()

        @pl.when(t == INPROJ_BLOCKS)
        def _():
            out_copy(INPROJ_BLOCKS - 1, 1 - slot).wait()

        in_copy((k + 1) % INPROJ_BLOCKS, 1 - slot).start()

    @pl.when(i > 0)
    def _():
        in_copy(k, slot).wait()

    _inproj_products(x_ref, shift_ref, scale_ref, wbuf_ref.at[slot], o_ref, xn_ref)


def _inproj_context_kernel(x_ref, shift_ref, scale_ref, w_ref, o_ref, xn_ref):
    _inproj_products(x_ref, shift_ref, scale_ref, w_ref.at[0], o_ref, xn_ref)


def _inproj_mod_map(j, tiles_per_sample, sample_row0):
    return lambda i, k: ((sample_row0 + i // tiles_per_sample) * 6 + j, 0, 0)


def _inproj_latent(x2d, mod_rows, w_in):
    assert INPROJ_BLOCKS % 2 == 1
    last = INPROJ_BLOCKS - 1
    return pl.pallas_call(
        _inproj_latent_kernel,
        grid=(N_TOK // INPROJ_TM, INPROJ_BLOCKS),
        in_specs=[
            pl.BlockSpec((INPROJ_TM, D_MODEL), lambda i, k: (i, 0)),
            pl.BlockSpec((1, 1, D_MODEL), _inproj_mod_map(0, SEQ // INPROJ_TM, 0)),
            pl.BlockSpec((1, 1, D_MODEL), _inproj_mod_map(1, SEQ // INPROJ_TM, 0)),
            pl.BlockSpec((D_MODEL, INPROJ_TN), lambda i, k: (0, jnp.where(i == 0, k, last))),
        ],
        out_specs=[pl.BlockSpec((INPROJ_TM, INPROJ_TN), lambda i, k: (i, k)), pl.BlockSpec(memory_space=pl.ANY)],
        out_shape=[jax.ShapeDtypeStruct((N_TOK, IN_W), BF16),
                   jax.ShapeDtypeStruct((INPROJ_BLOCKS, D_MODEL, INPROJ_TN), BF16)],
        scratch_shapes=[pltpu.VMEM((INPROJ_TM, D_MODEL), BF16), pltpu.VMEM((2, D_MODEL, INPROJ_TN), BF16),
                        pltpu.SemaphoreType.DMA((2,)), pltpu.SemaphoreType.DMA((2,))],
        compiler_params=_params(("arbitrary", "arbitrary")),
        name="inproj",
    )(x2d, mod_rows, mod_rows, w_in)


def _inproj_context(ctx2d, mod_rows, w_blocks):
    last = INPROJ_BLOCKS - 1
    out_w = 2 * INPROJ_TN + (IN_W - last * INPROJ_TN)
    return pl.pallas_call(
        _inproj_context_kernel,
        grid=(N_CTX // INPROJ_TM, 3),
        in_specs=[
            pl.BlockSpec((INPROJ_TM, D_MODEL), lambda i, k: (i, 0)),
            pl.BlockSpec((1, 1, D_MODEL), _inproj_mod_map(0, N_CTX // INPROJ_TM, BATCH)),
            pl.BlockSpec((1, 1, D_MODEL), _inproj_mod_map(1, N_CTX // INPROJ_TM, BATCH)),
            pl.BlockSpec((1, D_MODEL, INPROJ_TN), lambda i, k: (jnp.where(k == 2, last, k + 1), 0, 0)),
        ],
        out_specs=pl.BlockSpec((INPROJ_TM, INPROJ_TN), lambda i, k: (i, k)),
        out_shape=jax.ShapeDtypeStruct((N_CTX, out_w), BF16),
        scratch_shapes=[pltpu.VMEM((INPROJ_TM, D_MODEL), BF16)],
        compiler_params=_params(("arbitrary", "arbitrary")),
        name="inproj_ctx",
    )(ctx2d, mod_rows, mod_rows, w_blocks)


def _rope_tables(width):
    half = width // 2
    quarter = half // 2
    inv_freq = ROPE_BASE ** (-np.arange(0, half, 2, dtype=np.float64) / half)
    t = np.arange(SEQ)
    cos_parts, sin_parts = [], []
    for pos in (t // GRID_W, t % GRID_W):
        ang = pos[:, None].astype(np.float64) * inv_freq[None, :]
        c, s = np.cos(ang), np.sin(ang)
        cos_parts += [c, c]
        sin_parts += [-s, s]
    assert cos_parts[0].shape[1] == quarter
    return (np.concatenate(cos_parts, 1).astype(np.float32), np.concatenate(sin_parts, 1).astype(np.float32))


def _rope(x, cos, sin, quarter):
    lane = lax.broadcasted_iota(I32, x.shape, 1)
    first = (lane % (2 * quarter)) < quarter
    swapped = jnp.where(first, pltpu.roll(x, LANES - quarter, 1), pltpu.roll(x, quarter, 1))
    return x * cos + swapped * sin


def _dot_tn(a, b):
    return lax.dot_general(a, b, (((0,), (0,)), ((), ())), preferred_element_type=F32)


def _dot_nt(a, b):
    return lax.dot_general(a, b, (((1,), (1,)), ((), ())), preferred_element_type=F32)


RET_HG = 4


def _retention_kernel(q_ref, k_ref, v_ref, g_ref, kc_ref, vc_ref, cos_ref, sin_ref, df_ref, db_ref, gn_ref,
                      o_ref, qs_ref, ks_ref, st_ref):
    for hh in range(RET_HG):
        _retention_head(hh, slice(hh * LANES, (hh + 1) * LANES), q_ref, k_ref, v_ref, g_ref, kc_ref, vc_ref,
                        cos_ref, sin_ref, df_ref, db_ref, gn_ref, o_ref, qs_ref, ks_ref, st_ref)


def _retention_head(hh, hs, q_ref, k_ref, v_ref, g_ref, kc_ref, vc_ref, cos_ref, sin_ref, df_ref, db_ref, gn_ref,
                    o_ref, qs_ref, ks_ref, st_ref):
    k_scale = RET_DK ** -0.5
    lgf = jax.nn.log_sigmoid(df_ref[hh])
    lgb = jax.nn.log_sigmoid(db_ref[hh])

    cos = cos_ref[...]
    sin = sin_ref[...]
    qs_ref[hh] = _rope(q_ref[:, hs].astype(F32), cos, sin, RET_DK // 4).astype(BF16)
    ks_ref[hh] = _rope(k_ref[:, hs].astype(F32), cos, sin, RET_DK // 4) * k_scale

    rowi = lax.broadcasted_iota(I32, (CHUNK, LANES), 0).astype(F32)
    coli = lax.broadcasted_iota(I32, (CHUNK, LANES), 1).astype(F32)
    diff = rowi - coli
    decay = jnp.exp(jnp.where(diff >= 0, lgf * diff, -lgb * diff)) * jnp.where(diff == 0, 2.0, 1.0)
    zeta_f = jnp.exp(lgf * (CHUNK - 1.0 - rowi))
    eta_b = jnp.exp(lgb * rowi)
    xi_f = jnp.exp(lgf * (rowi + 1.0))
    xi_b = jnp.exp(lgb * (CHUNK - rowi))
    cdec_f = jnp.exp(lgf * float(CHUNK))
    cdec_b = jnp.exp(lgb * float(CHUNK))

    crow = lax.broadcasted_iota(I32, (CTX_LEN, LANES), 0).astype(F32)
    kc = kc_ref[:, hs].astype(F32) * k_scale
    vc = vc_ref[:, hs]
    s_f = _dot_tn((kc * jnp.exp(lgf * (CTX_LEN - 1.0 - crow))).astype(BF16), vc)
    s_b = _dot_tn((kc * jnp.exp(lgb * crow)).astype(BF16), vc)

    upd_f, upd_b = [], []
    for i in range(N_CHUNKS):
        rows = pl.ds(i * CHUNK, CHUNK)
        kch = ks_ref[hh, rows, :]
        vch = v_ref[rows, hs]
        upd_f.append(_dot_tn((kch * zeta_f).astype(BF16), vch))
        upd_b.append(_dot_tn((kch * eta_b).astype(BF16), vch))
    state = s_f
    for i in range(N_CHUNKS):
        st_ref[hh, i, :, :RET_DV] = state.astype(BF16)
        state = cdec_f * state + upd_f[i]
    state = s_b
    for i in reversed(range(N_CHUNKS)):
        st_ref[hh, i, :, RET_DV:] = state.astype(BF16)
        state = cdec_b * state + upd_b[i]

    gn_w = gn_ref[:, hs]
    for i in range(N_CHUNKS):
        rows = pl.ds(i * CHUNK, CHUNK)
        qch = qs_ref[hh, rows, :]
        scores = _dot_nt(qch, ks_ref[hh, rows, :].astype(BF16)) * decay
        o = jnp.dot(scores.astype(BF16), v_ref[rows, hs], preferred_element_type=F32)
        cross = jnp.dot(qch, st_ref[hh, i], preferred_element_type=F32)
        o = o + xi_f * cross[:, :RET_DV] + xi_b * cross[:, RET_DV:]
        y = _plain_norm(o) * gn_w
        o_ref[rows, hs] = (_silu(g_ref[rows, hs].astype(F32)) * y).astype(BF16)


def _retention(proj, proj_c, cos, sin, decay_rows, gn_w):
    width = RET_HG * LANES
    groups = RET_HEADS // RET_HG
    blk = lambda off: pl.BlockSpec((SEQ, width), lambda b, hg: (b, off // width + hg))
    blk_c = lambda off: pl.BlockSpec((CTX_LEN, width), lambda b, hg: (b, off // width + hg))
    table = pl.BlockSpec((SEQ, LANES), lambda b, hg: (0, 0))
    return pl.pallas_call(
        _retention_kernel,
        grid=(BATCH, groups),
        in_specs=[
            blk(OFF_Q), blk(OFF_K), blk(OFF_V), blk(OFF_G), blk_c(CTX_OFF_K), blk_c(CTX_OFF_V), table, table,
            pl.BlockSpec((RET_HG, 1, LANES), lambda b, hg: (hg, 0, 0)),
            pl.BlockSpec((RET_HG, 1, LANES), lambda b, hg: (groups + hg, 0, 0)),
            pl.BlockSpec((1, width), lambda b, hg: (0, hg)),
        ],
        out_specs=pl.BlockSpec((SEQ, width), lambda b, hg: (b, hg)),
        out_shape=jax.ShapeDtypeStruct((N_TOK, RET_W), BF16),
        scratch_shapes=[
            pltpu.VMEM((RET_HG, SEQ, RET_DK), BF16),
            pltpu.VMEM((RET_HG, SEQ, RET_DK), F32),
            pltpu.VMEM((RET_HG, N_CHUNKS, RET_DK, 2 * RET_DV), BF16),
        ],
        compiler_params=_params(("arbitrary", "arbitrary")),
        name="retention",
    )(proj, proj, proj, proj, proj_c, proj_c, cos, sin, decay_rows, decay_rows, gn_w)


MLA_TM = 512


def _rms_norm(x, w):
    return x * lax.rsqrt(jnp.mean(x * x, -1, keepdims=True) + EPS) * w


def _mla_kv(ckv_ref, kpe_ref, kvn_ref, wkv_ref, cos_ref, sin_ref, k_ref, v_ref, rotate):
    ckv = _rms_norm(ckv_ref[...].astype(F32), kvn_ref[...]).astype(BF16)
    kv = jnp.dot(ckv, wkv_ref[...], preferred_element_type=F32)
    lane = lax.broadcasted_iota(I32, (ckv.shape[0], LANES), 1)
    kpe = jnp.where(lane < MLA_ROPE, kpe_ref[...].astype(F32), 0.0)
    if rotate:
        kpe = _rope(kpe, cos_ref[...], sin_ref[...], MLA_ROPE // 4)
    kpe = kpe.astype(BF16)
    for h in range(MLA_HEADS):
        k_ref[:, h * QK_PAD:h * QK_PAD + MLA_NOPE] = kv[:, 2 * h * LANES:(2 * h + 1) * LANES].astype(BF16)
        k_ref[:, h * QK_PAD + MLA_NOPE:(h + 1) * QK_PAD] = kpe
        v_ref[:, h * MLA_DV:(h + 1) * MLA_DV] = kv[:, (2 * h + 1) * LANES:(2 * h + 2) * LANES].astype(BF16)


def _mla_latent_kernel(cq_ref, ckv_ref, kpe_ref, qn_ref, kvn_ref, wq_ref, wkv_ref, cos_ref, sin_ref,
                       q_ref, k_ref, v_ref):
    cq = _rms_norm(cq_ref[...].astype(F32), qn_ref[...]).astype(BF16)
    q = jnp.dot(cq, wq_ref[...], preferred_element_type=F32)
    cos = cos_ref[...]
    sin = sin_ref[...]
    scale = MLA_DQ ** -0.5 * LOG2_E
    for h in range(MLA_HEADS):
        lo = h * QK_PAD
        q_ref[:, lo:lo + MLA_NOPE] = (q[:, lo:lo + MLA_NOPE] * scale).astype(BF16)
        qpe = _rope(q[:, lo + MLA_NOPE:lo + QK_PAD], cos, sin, MLA_ROPE // 4)
        q_ref[:, lo + MLA_NOPE:lo + QK_PAD] = (qpe * scale).astype(BF16)
    _mla_kv(ckv_ref, kpe_ref, kvn_ref, wkv_ref, cos_ref, sin_ref, k_ref, v_ref, rotate=True)


def _mla_context_kernel(ckv_ref, kpe_ref, kvn_ref, wkv_ref, k_ref, v_ref):
    _mla_kv(ckv_ref, kpe_ref, kvn_ref, wkv_ref, None, None, k_ref, v_ref, rotate=False)


def _mla_latent(proj, q_norm, kv_norm, wq_pad, wkv, cos, sin):
    row = lambda w, off: pl.BlockSpec((MLA_TM, w), lambda i: (i, off // w))
    full = lambda a: pl.BlockSpec(a.shape, lambda i: (0, 0))
    table = pl.BlockSpec((MLA_TM, LANES), lambda i: (i % (SEQ // MLA_TM), 0))
    return pl.pallas_call(
        _mla_latent_kernel,
        grid=(N_TOK // MLA_TM,),
        in_specs=[row(MLA_Q_LORA, OFF_CQ), row(MLA_KV_LORA, OFF_CKV), row(LANES, OFF_KPE),
                  full(q_norm), full(kv_norm), full(wq_pad), full(wkv), table, table],
        out_specs=[
            pl.BlockSpec((MLA_TM, MLA_HEADS * QK_PAD), lambda i: (i, 0)),
            pl.BlockSpec((MLA_TM, MLA_HEADS * QK_PAD), lambda i: (i, 0)),
            pl.BlockSpec((MLA_TM, MLA_W), lambda i: (i, 0)),
        ],
        out_shape=[
            jax.ShapeDtypeStruct((N_TOK, MLA_HEADS * QK_PAD), BF16),
            jax.ShapeDtypeStruct((N_TOK, MLA_HEADS * QK_PAD), BF16),
            jax.ShapeDtypeStruct((N_TOK, MLA_W), BF16),
        ],
        compiler_params=_params(("arbitrary",)),
        name="mla_latent",
    )(proj, proj, proj, q_norm, kv_norm, wq_pad, wkv, cos, sin)


def _mla_context(proj_c, kv_norm, wkv):
    row = lambda w, off: pl.BlockSpec((MLA_TM, w), lambda i: (i, off // w))
    full = lambda a: pl.BlockSpec(a.shape, lambda i: (0, 0))
    return pl.pallas_call(
        _mla_context_kernel,
        grid=(N_CTX // MLA_TM,),
        in_specs=[row(MLA_KV_LORA, CTX_OFF_CKV), row(LANES, CTX_OFF_KPE), full(kv_norm), full(wkv)],
        out_specs=[
            pl.BlockSpec((MLA_TM, MLA_HEADS * QK_PAD), lambda i: (i, 0)),
            pl.BlockSpec((MLA_TM, MLA_W), lambda i: (i, 0)),
        ],
        out_shape=[
            jax.ShapeDtypeStruct((N_CTX, MLA_HEADS * QK_PAD), BF16),
            jax.ShapeDtypeStruct((N_CTX, MLA_W), BF16),
        ],
        compiler_params=_params(("arbitrary",)),
        name="mla_context",
    )(proj_c, proj_c, kv_norm, wkv)


ATT_TQ = 256


def _attention_kernel(q_ref, kc_ref, kl_ref, vc_ref, vl_ref, wo_ref, o_ref, wo_bf_ref, vext_ref):
    wo_bf_ref[...] = wo_ref[...].astype(BF16)

    @pl.when(pl.program_id(1) == 0)
    def _():
        lane = lax.broadcasted_iota(I32, (CTX_LEN + SEQ, MLA_DV), 1)
        ones_col = jnp.where(lane == 0, 1.0, 0.0).astype(BF16)
        for h in range(MLA_HEADS):
            dv = slice(h * MLA_DV, (h + 1) * MLA_DV)
            vext_ref[h, :CTX_LEN, :MLA_DV] = vc_ref[:, dv]
            vext_ref[h, CTX_LEN:, :MLA_DV] = vl_ref[:, dv]
            vext_ref[h, :, MLA_DV:] = ones_col

    for h in range(MLA_HEADS):
        qk = slice(h * QK_PAD, (h + 1) * QK_PAD)
        q = q_ref[:, qk]
        s_c = _dot_nt(q, kc_ref[:, qk])
        s_l = _dot_nt(q, kl_ref[:, qk])
        m = jnp.maximum(jnp.max(s_c, -1, keepdims=True), jnp.max(s_l, -1, keepdims=True))
        p_c = jnp.exp2(s_c - m).astype(BF16)
        p_l = jnp.exp2(s_l - m).astype(BF16)
        o = (jnp.dot(p_c, vext_ref[h, :CTX_LEN, :], preferred_element_type=F32)
             + jnp.dot(p_l, vext_ref[h, CTX_LEN:, :], preferred_element_type=F32))
        o_ref[:, h * MLA_DV:(h + 1) * MLA_DV] = (o[:, :MLA_DV] / o[:, MLA_DV:MLA_DV + 1]).astype(BF16)


def _attention(q, k_ctx, k_lat, v_ctx, v_lat, w_o):
    tiles = SEQ // ATT_TQ
    band = w_o.shape[0] // (BATCH * tiles)
    step = lambda b, i: (b * tiles + i, 0)
    return pl.pallas_call(
        _attention_kernel,
        grid=(BATCH, tiles),
        in_specs=[
            pl.BlockSpec((ATT_TQ, MLA_HEADS * QK_PAD), step),
            pl.BlockSpec((CTX_LEN, MLA_HEADS * QK_PAD), lambda b, i: (b, 0)),
            pl.BlockSpec((SEQ, MLA_HEADS * QK_PAD), lambda b, i: (b, 0)),
            pl.BlockSpec((CTX_LEN, MLA_W), lambda b, i: (b, 0)),
            pl.BlockSpec((SEQ, MLA_W), lambda b, i: (b, 0)),
            pl.BlockSpec((band, D_MODEL), step),
        ],
        out_specs=[pl.BlockSpec((ATT_TQ, MLA_W), step), pl.BlockSpec((band, D_MODEL), step)],
        out_shape=[jax.ShapeDtypeStruct((N_TOK, MLA_W), BF16), jax.ShapeDtypeStruct(w_o.shape, BF16)],
        scratch_shapes=[pltpu.VMEM((MLA_HEADS, CTX_LEN + SEQ, 2 * MLA_DV), BF16)],
        compiler_params=_params(("arbitrary", "arbitrary")),
        name="attention",
    )(q, k_ctx, k_lat, v_ctx, v_lat, w_o)


OUT_TM = 512
OUT_SUB = 256
ROUTE_E1, ROUTE_E2, ROUTE_R1, ROUTE_R2, ROUTE_W1, ROUTE_W2 = range(6)


def _outproj_kernel(ret_ref, att_ref, x_ref, wo_ref, gate_ref, shift_ref, scale_ref, lnw_ref, lnb_ref,
                    wr_ref, br_ref, h_ref, t_ref, route_ref, route_t_ref, count_ref, carry_ref, wsplit_ref):
    @pl.when(pl.program_id(0) == 0)
    def _():
        carry_ref[...] = jnp.zeros_like(carry_ref)
        w = wr_ref[...]
        w_hi = w.astype(BF16)
        wsplit_ref[:, :LANES] = w_hi
        wsplit_ref[:, LANES:] = (w - w_hi.astype(F32)).astype(BF16)

    subtiles = [slice(s * OUT_SUB, (s + 1) * OUT_SUB) for s in range(OUT_TM // OUT_SUB)]
    mixes = [jnp.dot(ret_ref[rows, :], wo_ref[:RET_W, :], preferred_element_type=F32)
             + jnp.dot(att_ref[rows, :], wo_ref[RET_W:, :], preferred_element_type=F32) for rows in subtiles]
    logits = [_outproj_norms(rows, mix, x_ref, gate_ref, shift_ref, scale_ref, lnw_ref, lnb_ref, br_ref, h_ref, t_ref,
                             wsplit_ref) for rows, mix in zip(subtiles, mixes)]
    _route(jnp.concatenate(logits, 0), carry_ref, count_ref, route_ref, route_t_ref)


def _outproj_norms(rows, mix, x_ref, gate_ref, shift_ref, scale_ref, lnw_ref, lnb_ref, br_ref, h_ref, t_ref,
                   wsplit_ref):
    h = _plain_norm(DEEPNORM_ALPHA * x_ref[rows, :] + gate_ref[0] * mix) * lnw_ref[...] + lnb_ref[...]
    h_ref[rows, :] = h
    t = _plain_norm(h) * (1.0 + scale_ref[0]) + shift_ref[0]
    t_ref[rows] = _to_token_tiles(t)

    t_hi = t.astype(BF16)
    t_lo = (t - t_hi.astype(F32)).astype(BF16)
    main = jnp.dot(t_hi, wsplit_ref[...], preferred_element_type=F32)
    corr = jnp.dot(t_lo, wsplit_ref[:, :LANES], preferred_element_type=F32)
    return main[:, :LANES] + (main[:, LANES:] + corr) + br_ref[...]


def _route(logits, carry_ref, count_ref, route_ref, route_t_ref):
    lane = lax.broadcasted_iota(I32, logits.shape, 1).astype(F32)
    neg = -jnp.inf
    big = float(LANES)

    def first_lane_of(mask):
        return jnp.min(jnp.where(mask, lane, big), -1, keepdims=True)

    is_group = lane < N_GROUPS
    gl = jnp.where(is_group, logits, neg)
    g_max = jnp.max(gl, -1, keepdims=True)
    g_idx = first_lane_of(is_group & (gl == g_max))
    g_prob = 1.0 / jnp.sum(jnp.where(is_group, jnp.exp(logits - g_max), 0.0), -1, keepdims=True)

    lo = N_GROUPS + g_idx * EXPERTS_PER_GROUP
    in_group = (lane >= lo) & (lane < lo + EXPERTS_PER_GROUP)
    el = jnp.where(in_group, logits, neg)
    v1 = jnp.max(el, -1, keepdims=True)
    i1 = first_lane_of(in_group & (el == v1))
    rest = in_group & (lane != i1)
    el2 = jnp.where(rest, logits, neg)
    v2 = jnp.max(el2, -1, keepdims=True)
    i2 = first_lane_of(rest & (el2 == v2))
    d = jnp.exp(v2 - v1)
    w1 = g_prob / (1.0 + d)
    w2 = g_prob * d / (1.0 + d)

    onehot = jnp.where((lane == i1) | (lane == i2), 1.0, 0.0)
    r = lax.broadcasted_iota(I32, (OUT_TM, OUT_TM), 0)
    c = lax.broadcasted_iota(I32, (OUT_TM, OUT_TM), 1)
    tri = jnp.where(c < r, 1.0, 0.0).astype(BF16)
    carry = carry_ref[0:1, :]
    before = jnp.dot(tri, onehot.astype(BF16), preferred_element_type=F32) + carry
    r1 = jnp.sum(jnp.where(lane == i1, before, 0.0), -1, keepdims=True)
    r2 = jnp.sum(jnp.where(lane == i2, before, 0.0), -1, keepdims=True)
    carry = carry + jnp.sum(onehot, 0, keepdims=True)
    carry_ref[...] = jnp.broadcast_to(carry, carry_ref.shape)
    count_ref[...] = jnp.broadcast_to(carry, count_ref.shape)

    rec = jnp.zeros_like(logits)
    for slot, val in ((ROUTE_E1, i1 - N_GROUPS), (ROUTE_E2, i2 - N_GROUPS), (ROUTE_R1, r1), (ROUTE_R2, r2),
                      (ROUTE_W1, w1), (ROUTE_W2, w2)):
        rec = jnp.where(lane == slot, val, rec)
    route_ref[...] = rec
    route_t_ref[...] = rec.T[:8, :]


def _outproj(ret, att, x2d, wo_bf, mod_rows, ln_w, ln_b, w_route, b_route):
    tiles_per_sample = SEQ // OUT_TM
    row = lambda w: pl.BlockSpec((OUT_TM, w), lambda i: (i, 0))
    full = lambda a: pl.BlockSpec(a.shape, lambda i: (0, 0))
    mod = lambda j: pl.BlockSpec((1, 1, D_MODEL), lambda i: ((i // tiles_per_sample) * 6 + j, 0, 0))
    return pl.pallas_call(
        _outproj_kernel,
        grid=(N_TOK // OUT_TM,),
        in_specs=[row(RET_W), row(MLA_W), row(D_MODEL), full(wo_bf), mod(2), mod(3), mod(4),
                  full(ln_w), full(ln_b), full(w_route), full(b_route)],
        out_specs=[row(D_MODEL), pl.BlockSpec((OUT_TM, TOKEN_SUB, LANES), lambda i: (i, 0, 0)), row(LANES),
                   pl.BlockSpec((8, OUT_TM), lambda i: (0, i)), pl.BlockSpec((8, LANES), lambda i: (0, 0))],
        out_shape=[
            jax.ShapeDtypeStruct((N_TOK, D_MODEL), F32),
            jax.ShapeDtypeStruct((N_TOK, TOKEN_SUB, LANES), BF16),
            jax.ShapeDtypeStruct((N_TOK, LANES), F32),
            jax.ShapeDtypeStruct((8, N_TOK), F32),
            jax.ShapeDtypeStruct((8, LANES), F32),
        ],
        scratch_shapes=[pltpu.VMEM((8, LANES), F32), pltpu.VMEM((D_MODEL, 2 * LANES), BF16)],
        compiler_params=_params(("arbitrary",)),
        name="outproj_route",
    )(ret, att, x2d, wo_bf, mod_rows, mod_rows, mod_rows, ln_w, ln_b, w_route, b_route)


DISPATCH_TM = 256


def _dispatch_kernel(pos_ref, t_ref, xs_ref, sem):
    i = pl.program_id(0)

    def copy(slot, j):
        return pltpu.make_async_copy(t_ref.at[j], xs_ref.at[pos_ref[slot * N_TOK + i * DISPATCH_TM + j]], sem)

    for slot in range(2):
        lax.fori_loop(0, DISPATCH_TM, lambda j, c, slot=slot: (copy(slot, j).start(), c)[1], 0, unroll=8)
    for slot in range(2):
        pltpu.make_async_copy(t_ref, xs_ref.at[pl.ds(0, DISPATCH_TM)], sem).wait()


def _dispatch(pos, t_tiles):
    return pl.pallas_call(
        _dispatch_kernel,
        grid_spec=pltpu.PrefetchScalarGridSpec(
            num_scalar_prefetch=1,
            grid=(N_TOK // DISPATCH_TM,),
            in_specs=[pl.BlockSpec((DISPATCH_TM, TOKEN_SUB, LANES), lambda i, pos: (i, 0, 0))],
            out_specs=pl.BlockSpec(memory_space=pl.ANY),
            scratch_shapes=[pltpu.SemaphoreType.DMA(())],
        ),
        out_shape=jax.ShapeDtypeStruct((N_PAIRS, TOKEN_SUB, LANES), BF16),
        compiler_params=_params(("arbitrary",)),
        name="dispatch",
    )(pos, t_tiles)


def _experts_kernel(rend_ref, xs_ref, wg_ref, wu_ref, wd_ref, ys_ref, wgu_ref, wdn_ref, xbuf_ref, ybuf_ref,
                    xsem, ysem):
    e = pl.program_id(0)
    lo = jnp.where(e == 0, 0, rend_ref[jnp.maximum(e - 1, 0)])
    hi = rend_ref[e]

    def x_copy(g):
        return pltpu.make_async_copy(xs_ref.at[pl.ds(g * ROW_TILE, ROW_TILE)], xbuf_ref.at[g % 2], xsem.at[g % 2])

    def y_copy(g):
        return pltpu.make_async_copy(ybuf_ref.at[g % 2], ys_ref.at[pl.ds(g * ROW_TILE, ROW_TILE)], ysem.at[g % 2])

    @pl.when(e == 0)
    def _():
        x_copy(0).start()

    @pl.when(hi > lo)
    def _():
        wgu_ref[:, :D_EXPERT] = wg_ref[0].astype(BF16)
        wgu_ref[:, D_EXPERT:] = wu_ref[0].astype(BF16)
        wdn_ref[...] = wd_ref[0].astype(BF16)

        def tile(g, c):
            row0 = g * ROW_TILE
            owns_first_row = lo <= row0
            owns_last_row = hi >= row0 + ROW_TILE

            @pl.when(owns_first_row)
            def _():
                @pl.when(g + 1 < N_TILES)
                def _():
                    x_copy(g + 1).start()

                x_copy(g).wait()

            gu = jnp.dot(_from_token_tiles(xbuf_ref[g % 2]), wgu_ref[...], preferred_element_type=F32)
            hid = _silu(gu[:, :D_EXPERT]) * gu[:, D_EXPERT:]
            y = jnp.dot(hid.astype(BF16), wdn_ref[...], preferred_element_type=F32)

            @pl.when(owns_first_row)
            def _():
                @pl.when(g >= 2)
                def _():
                    y_copy(g - 2).wait()

                ybuf_ref[g % 2] = _to_token_tiles(y)

            @pl.when(jnp.logical_not(owns_first_row))
            def _():
                row = lax.broadcasted_iota(I32, y.shape, 0)
                earlier = _from_token_tiles(ybuf_ref[g % 2]).astype(F32)
                ybuf_ref[g % 2] = _to_token_tiles(jnp.where(row >= lo - row0, y, earlier))

            @pl.when(owns_last_row)
            def _():
                y_copy(g).start()

            return c

        lax.fori_loop(lo // ROW_TILE, (hi - 1) // ROW_TILE + 1, tile, 0)

    @pl.when(e == N_EXPERTS - 1)
    def _():
        y_copy(N_TILES - 2).wait()
        y_copy(N_TILES - 1).wait()


def _experts(row_end, xs, w_gate, w_up, w_down):
    w_gate = w_gate.reshape(N_EXPERTS, D_MODEL, D_EXPERT)
    w_up = w_up.reshape(N_EXPERTS, D_MODEL, D_EXPERT)
    w_down = w_down.reshape(N_EXPERTS, D_EXPERT, D_MODEL)
    expert = lambda e, rend: (e, 0, 0)
    tile_buf = pltpu.VMEM((2, ROW_TILE, TOKEN_SUB, LANES), BF16)
    return pl.pallas_call(
        _experts_kernel,
        grid_spec=pltpu.PrefetchScalarGridSpec(
            num_scalar_prefetch=1,
            grid=(N_EXPERTS,),
            in_specs=[
                pl.BlockSpec(memory_space=pl.ANY),
                pl.BlockSpec((1, D_MODEL, D_EXPERT), expert),
                pl.BlockSpec((1, D_MODEL, D_EXPERT), expert),
                pl.BlockSpec((1, D_EXPERT, D_MODEL), expert),
            ],
            out_specs=pl.BlockSpec(memory_space=pl.ANY),
            scratch_shapes=[pltpu.VMEM((D_MODEL, 2 * D_EXPERT), BF16), pltpu.VMEM((D_EXPERT, D_MODEL), BF16),
                            tile_buf, tile_buf, pltpu.SemaphoreType.DMA((2,)), pltpu.SemaphoreType.DMA((2,))],
        ),
        out_shape=jax.ShapeDtypeStruct((N_PAIRS, TOKEN_SUB, LANES), BF16),
        compiler_params=_params(("arbitrary",)),
        name="experts",
    )(row_end, xs, w_gate, w_up, w_down)


COMB_TM = 256


def _combine_kernel(pos_ref, ys_ref, h_ref, route_ref, gate_ref, lnw_ref, lnb_ref, o_ref, buf_ref, sem):
    i = pl.program_id(0)
    n = pl.num_programs(0)

    def copy(step, slot, pair, j):
        src = pos_ref[pair * N_TOK + step * COMB_TM + j]
        return pltpu.make_async_copy(ys_ref.at[src], buf_ref.at[slot, pair * COMB_TM + j], sem.at[slot])

    def start_all(step, slot):
        for pair in range(2):
            lax.fori_loop(0, COMB_TM, lambda j, c, pair=pair: (copy(step, slot, pair, j).start(), c)[1], 0,
                          unroll=8)

    def wait_all(step, slot):
        pltpu.make_async_copy(ys_ref.at[pl.ds(0, 2 * COMB_TM)], buf_ref.at[slot], sem.at[slot]).wait()

    @pl.when(i == 0)
    def _():
        start_all(0, 0)

    @pl.when(i + 1 < n)
    def _():
        start_all(i + 1, (i + 1) % 2)

    slot = i % 2
    wait_all(i, slot)

    y1 = _from_token_tiles(buf_ref[slot, :COMB_TM]).astype(F32)
    y2 = _from_token_tiles(buf_ref[slot, COMB_TM:]).astype(F32)
    y = route_ref[:, ROUTE_W1:ROUTE_W1 + 1] * y1 + route_ref[:, ROUTE_W2:ROUTE_W2 + 1] * y2
    z = DEEPNORM_ALPHA * h_ref[...] + gate_ref[0] * y
    o_ref[...] = _plain_norm(z) * lnw_ref[...] + lnb_ref[...]


def _combine(pos, ys, h, route, mod_rows, ln_w, ln_b):
    tiles_per_sample = SEQ // COMB_TM
    row = lambda w: pl.BlockSpec((COMB_TM, w), lambda i, pos: (i, 0))
    full = lambda a: pl.BlockSpec(a.shape, lambda i, pos: (0, 0))
    return pl.pallas_call(
        _combine_kernel,
        grid_spec=pltpu.PrefetchScalarGridSpec(
            num_scalar_prefetch=1,
            grid=(N_TOK // COMB_TM,),
            in_specs=[
                pl.BlockSpec(memory_space=pl.ANY), row(D_MODEL), row(LANES),
                pl.BlockSpec((1, 1, D_MODEL), lambda i, pos: ((i // tiles_per_sample) * 6 + 5, 0, 0)),
                full(ln_w), full(ln_b),
            ],
            out_specs=row(D_MODEL),
            scratch_shapes=[pltpu.VMEM((2, 2 * COMB_TM, TOKEN_SUB, LANES), BF16), pltpu.SemaphoreType.DMA((2,))],
        ),
        out_shape=jax.ShapeDtypeStruct((N_TOK, D_MODEL), F32),
        compiler_params=_params(("arbitrary",)),
        name="combine",
    )(pos, ys, h, route, mod_rows, ln_w, ln_b)


def _routing_tables(route_t, counts):
    cnt = counts[0, N_GROUPS:N_ROUTE].astype(I32)
    row_end = jnp.cumsum(cnt)
    e = route_t[ROUTE_E1:ROUTE_E2 + 1].astype(I32)
    rank = route_t[ROUTE_R1:ROUTE_R2 + 1].astype(I32)
    earlier = jnp.arange(N_EXPERTS, dtype=I32)[:, None, None] < e[None]
    base = jnp.sum(jnp.where(earlier, cnt[:, None, None], 0), 0)
    pos = (base + rank).reshape(-1)
    return pos, row_end


def kernel(x, c, ctx, c_ctx, w_ada, b_ada, w_in, ret_decay, ret_gn_w, mla_q_norm, mla_kv_norm, w_uq, w_ukv, w_o,
           ln1_w, ln1_b, router_group_w, router_group_b, router_expert_w, router_expert_b, expert_w_gate,
           expert_w_up, expert_w_down, ln2_w, ln2_b):
    x2d = x.reshape(N_TOK, D_MODEL)
    ctx2d = ctx.reshape(N_CTX, D_MODEL)

    cc = jnp.zeros((8, D_MODEL), F32).at[:BATCH].set(c).at[BATCH].set(c_ctx)
    mod = _ada(cc, w_ada[0], b_ada)
    mod_rows = mod.reshape(8 * 6, 1, D_MODEL)

    proj, w_in_blocks = _inproj_latent(x2d, mod_rows, w_in[0])
    proj_c = _inproj_context(ctx2d, mod_rows, w_in_blocks)

    cos_r, sin_r = _rope_tables(RET_DK)
    decay_rows = jnp.broadcast_to(ret_decay[0].reshape(2 * RET_HEADS, 1, 1), (2 * RET_HEADS, 1, LANES))
    ret = _retention(proj, proj_c, jnp.asarray(cos_r), jnp.asarray(sin_r), decay_rows, ret_gn_w)

    cos_m, sin_m = _rope_tables(MLA_ROPE)
    cos_m = np.concatenate([cos_m, np.ones_like(cos_m)], 1)
    sin_m = np.concatenate([sin_m, np.zeros_like(sin_m)], 1)
    wq = w_uq[0].reshape(MLA_Q_LORA, MLA_HEADS, MLA_DQ)
    wq_pad = jnp.pad(wq, ((0, 0), (0, 0), (0, QK_PAD - MLA_DQ))).reshape(MLA_Q_LORA, MLA_HEADS * QK_PAD).astype(BF16)
    wkv = w_ukv[0].astype(BF16)
    q, k_lat, v_lat = _mla_latent(proj, mla_q_norm, mla_kv_norm, wq_pad, wkv, jnp.asarray(cos_m), jnp.asarray(sin_m))
    k_ctx, v_ctx = _mla_context(proj_c, mla_kv_norm, wkv)
    att, w_o_bf = _attention(q, k_ctx, k_lat, v_ctx, v_lat, w_o[0])

    w_route = jnp.concatenate(
        [router_group_w[0], router_expert_w[0].transpose(1, 0, 2).reshape(D_MODEL, N_EXPERTS),
         jnp.zeros((D_MODEL, LANES - N_ROUTE), F32)], 1)
    b_route = jnp.concatenate(
        [router_group_b[0], router_expert_b[0].reshape(N_EXPERTS), jnp.zeros((LANES - N_ROUTE,), F32)])[None]
    h, t_tiles, route, route_t, counts = _outproj(ret, att, x2d, w_o_bf, mod_rows, ln1_w, ln1_b,
                                                   w_route, b_route)

    pos, row_end = _routing_tables(route_t, counts)
    xs = _dispatch(pos, t_tiles)
    ys = _experts(row_end, xs, expert_w_gate[0], expert_w_up[0], expert_w_down[0])
    out = _combine(pos, ys, h, route, mod_rows, ln2_w, ln2_b)
    return out.reshape(BATCH, SEQ, D_MODEL)
```

```python
import numpy as np
import jax
import jax.numpy as jnp
from jax import lax
from jax.experimental import pallas as pl
from jax.experimental.pallas import tpu as pltpu

F32 = jnp.float32
BF16 = jnp.bfloat16
I32 = jnp.int32

D_MODEL = 2048
BATCH = 4
SEQ = 2048
GRID_W = 64
CTX_LEN = 256
N_TOK = BATCH * SEQ
N_CTX = BATCH * CTX_LEN

RET_HEADS = 8
RET_DK = 128
RET_DV = 128
RET_W = RET_HEADS * RET_DV
CHUNK = 128
N_CHUNKS = SEQ // CHUNK

MLA_HEADS = 8
MLA_Q_LORA = 512
MLA_KV_LORA = 256
MLA_NOPE = 128
MLA_ROPE = 64
MLA_DV = 128
MLA_W = MLA_HEADS * MLA_DV
MLA_DQ = MLA_NOPE + MLA_ROPE
QK_PAD = 256

IN_SIZES = (RET_HEADS * RET_DK, RET_HEADS * RET_DK, RET_W, RET_W, MLA_Q_LORA, MLA_KV_LORA, MLA_ROPE)
IN_W = sum(IN_SIZES)
OFF_Q, OFF_K, OFF_V, OFF_G, OFF_CQ, OFF_CKV, OFF_KPE = (int(v) for v in np.cumsum((0,) + IN_SIZES[:-1]))

N_GROUPS = 4
EXPERTS_PER_GROUP = 8
N_EXPERTS = N_GROUPS * EXPERTS_PER_GROUP
D_EXPERT = 512
N_ROUTE = N_GROUPS + N_EXPERTS

LOG2_E = float(np.log2(np.e))
ROPE_BASE = 10000.0
EPS = 1e-6
DEPTH = 1
DEEPNORM_ALPHA = (2.0 * DEPTH) ** 0.25

LANES = 128
ROW_TILE = 256
N_PAIRS = 2 * N_TOK
N_TILES = N_PAIRS // ROW_TILE
TOKEN_SUB = D_MODEL // LANES

VMEM_LIMIT = 56 * 1024 * 1024


def _params(sem, vmem=VMEM_LIMIT):
    return pltpu.CompilerParams(dimension_semantics=sem, vmem_limit_bytes=vmem)


def _silu(x):
    return x * (1.0 / (1.0 + jnp.exp(-x)))


def _plain_norm(x):
    mu = jnp.mean(x, -1, keepdims=True)
    xc = x - mu
    var = jnp.mean(xc * xc, -1, keepdims=True)
    return xc * lax.rsqrt(var + EPS)


def _to_token_tiles(x):
    return x.astype(BF16).reshape(x.shape[0], TOKEN_SUB, LANES)


def _from_token_tiles(x):
    return x.reshape(x.shape[0], D_MODEL)


ADA_TN = 1024


ADA_BANDS = 4


def _ada_kernel(cc_ref, *refs):
    w_refs, b_ref, o_ref = refs[:ADA_BANDS], refs[ADA_BANDS], refs[ADA_BANDS + 1]
    s = _silu(cc_ref[...])
    band = D_MODEL // ADA_BANDS
    acc = b_ref[...]
    for q, w_ref in enumerate(w_refs):
        acc = acc + jnp.dot(s[:, q * band:(q + 1) * band], w_ref[...], preferred_element_type=F32,
                            precision=lax.Precision.HIGHEST)
    o_ref[...] = acc


def _ada(cc, w_ada, b_ada):
    n = w_ada.shape[1]
    band = D_MODEL // ADA_BANDS
    return pl.pallas_call(
        _ada_kernel,
        grid=(n // ADA_TN,),
        in_specs=[pl.BlockSpec((8, D_MODEL), lambda j: (0, 0))]
        + [pl.BlockSpec((band, ADA_TN), lambda j, q=q: (q, j)) for q in range(ADA_BANDS)]
        + [pl.BlockSpec((1, ADA_TN), lambda j: (0, j))],
        out_specs=pl.BlockSpec((8, ADA_TN), lambda j: (0, j)),
        out_shape=jax.ShapeDtypeStruct((8, n), F32),
        compiler_params=_params(("arbitrary",)),
        name="ada",
    )(cc, *([w_ada] * ADA_BANDS), b_ada)


INPROJ_TM = 1024
INPROJ_TN = 1024
assert OFF_K == INPROJ_TN and OFF_V == 2 * INPROJ_TN and OFF_CQ == 4 * INPROJ_TN
CTX_OFF_K, CTX_OFF_V = 0, INPROJ_TN
CTX_OFF_CKV = 2 * INPROJ_TN + (OFF_CKV - OFF_CQ)
CTX_OFF_KPE = 2 * INPROJ_TN + (OFF_KPE - OFF_CQ)


INPROJ_SUB = 256
INPROJ_BLOCKS = -(-IN_W // INPROJ_TN)


def _inproj_products(x_ref, shift_ref, scale_ref, w, o_ref, xn_ref):
    first = pl.program_id(1) == 0

    @pl.when(first)
    def _():
        for s in range(INPROJ_TM // INPROJ_SUB):
            rows = slice(s * INPROJ_SUB, (s + 1) * INPROJ_SUB)
            y = (_plain_norm(x_ref[rows, :]) * (1.0 + scale_ref[0]) + shift_ref[0]).astype(BF16)
            xn_ref[rows, :] = y
            o_ref[rows, :] = jnp.dot(y, w[...], preferred_element_type=F32).astype(BF16)

    @pl.when(jnp.logical_not(first))
    def _():
        o_ref[...] = jnp.dot(xn_ref[...], w[...], preferred_element_type=F32).astype(BF16)


def _inproj_latent_kernel(x_ref, shift_ref, scale_ref, wf_ref, o_ref, wbf_ref, xn_ref, wbuf_ref, in_sem, out_sem):
    i = pl.program_id(0)
    k = pl.program_id(1)
    t = i * INPROJ_BLOCKS + k
    slot = t % 2
    n_steps = pl.num_programs(0) * INPROJ_BLOCKS

    def out_copy(block, s):
        return pltpu.make_async_copy(wbuf_ref.at[s], wbf_ref.at[block], out_sem.at[s])

    def in_copy(block, s):
        return pltpu.make_async_copy(wbf_ref.at[block], wbuf_ref.at[s], in_sem.at[s])

    @pl.when(i == 0)
    def _():
        @pl.when(k >= 2)
        def _():
            out_copy(k - 2, slot).wait()

        col = lax.broadcasted_iota(I32, (D_MODEL, INPROJ_TN), 1)
        wbuf_ref[slot] = jnp.where(col < IN_W - k * INPROJ_TN, wf_ref[0], 0.0).astype(BF16)
        out_copy(k, slot).start()

    @pl.when((t + 1 >= INPROJ_BLOCKS) & (t + 1 < n_steps))
    def _():
        @pl.when(t == INPROJ_BLOCKS - 1)
        def _():
            out_copy(INPROJ_BLOCKS - 2, 1 - slot).wait()

        @pl.when(t == INPROJ_BLOCKS)
        def _():
            out_copy(INPROJ_BLOCKS - 1, 1 - slot).wait()

        in_copy((k + 1) % INPROJ_BLOCKS, 1 - slot).start()

    @pl.when(i > 0)
    def _():
        in_copy(k, slot).wait()

    _inproj_products(x_ref, shift_ref, scale_ref, wbuf_ref.at[slot], o_ref, xn_ref)


def _inproj_context_kernel(x_ref, shift_ref, scale_ref, w_ref, o_ref, xn_ref):
    _inproj_products(x_ref, shift_ref, scale_ref, w_ref.at[0], o_ref, xn_ref)


def _inproj_mod_map(j, tiles_per_sample, sample_row0):
    return lambda i, k: ((sample_row0 + i // tiles_per_sample) * 6 + j, 0, 0)


def _inproj_latent(x2d, mod_rows, w_in):
    assert INPROJ_BLOCKS % 2 == 1
    last = INPROJ_BLOCKS - 1
    return pl.pallas_call(
        _inproj_latent_kernel,
        grid=(N_TOK // INPROJ_TM, INPROJ_BLOCKS),
        in_specs=[
            pl.BlockSpec((INPROJ_TM, D_MODEL), lambda i, k: (i, 0)),
            pl.BlockSpec((1, 1, D_MODEL), _inproj_mod_map(0, SEQ // INPROJ_TM, 0)),
            pl.BlockSpec((1, 1, D_MODEL), _inproj_mod_map(1, SEQ // INPROJ_TM, 0)),
            pl.BlockSpec((1, D_MODEL, INPROJ_TN), lambda i, k: (0, 0, jnp.where(i == 0, k, last))),
        ],
        out_specs=[pl.BlockSpec((INPROJ_TM, INPROJ_TN), lambda i, k: (i, k)), pl.BlockSpec(memory_space=pl.ANY)],
        out_shape=[jax.ShapeDtypeStruct((N_TOK, IN_W), BF16),
                   jax.ShapeDtypeStruct((INPROJ_BLOCKS, D_MODEL, INPROJ_TN), BF16)],
        scratch_shapes=[pltpu.VMEM((INPROJ_TM, D_MODEL), BF16), pltpu.VMEM((2, D_MODEL, INPROJ_TN), BF16),
                        pltpu.SemaphoreType.DMA((2,)), pltpu.SemaphoreType.DMA((2,))],
        compiler_params=_params(("arbitrary", "arbitrary")),
        name="inproj",
    )(x2d, mod_rows, mod_rows, w_in)


def _inproj_context(ctx2d, mod_rows, w_blocks):
    last = INPROJ_BLOCKS - 1
    out_w = 2 * INPROJ_TN + (IN_W - last * INPROJ_TN)
    return pl.pallas_call(
        _inproj_context_kernel,
        grid=(N_CTX // INPROJ_TM, 3),
        in_specs=[
            pl.BlockSpec((INPROJ_TM, D_MODEL), lambda i, k: (i, 0)),
            pl.BlockSpec((1, 1, D_MODEL), _inproj_mod_map(0, N_CTX // INPROJ_TM, BATCH)),
            pl.BlockSpec((1, 1, D_MODEL), _inproj_mod_map(1, N_CTX // INPROJ_TM, BATCH)),
            pl.BlockSpec((1, D_MODEL, INPROJ_TN), lambda i, k: (jnp.where(k == 2, last, k + 1), 0, 0)),
        ],
        out_specs=pl.BlockSpec((INPROJ_TM, INPROJ_TN), lambda i, k: (i, k)),
        out_shape=jax.ShapeDtypeStruct((N_CTX, out_w), BF16),
        scratch_shapes=[pltpu.VMEM((INPROJ_TM, D_MODEL), BF16)],
        compiler_params=_params(("arbitrary", "arbitrary")),
        name="inproj_ctx",
    )(ctx2d, mod_rows, mod_rows, w_blocks)


def _rope_tables(width):
    half = width // 2
    quarter = half // 2
    inv_freq = ROPE_BASE ** (-np.arange(0, half, 2, dtype=np.float64) / half)
    t = np.arange(SEQ)
    cos_parts, sin_parts = [], []
    for pos in (t // GRID_W, t % GRID_W):
        ang = pos[:, None].astype(np.float64) * inv_freq[None, :]
        c, s = np.cos(ang), np.sin(ang)
        cos_parts += [c, c]
        sin_parts += [-s, s]
    assert cos_parts[0].shape[1] == quarter
    return (np.concatenate(cos_parts, 1).astype(np.float32), np.concatenate(sin_parts, 1).astype(np.float32))


def _rope(x, cos, sin, quarter):
    lane = lax.broadcasted_iota(I32, x.shape, 1)
    first = (lane % (2 * quarter)) < quarter
    swapped = jnp.where(first, pltpu.roll(x, LANES - quarter, 1), pltpu.roll(x, quarter, 1))
    return x * cos + swapped * sin


def _dot_tn(a, b):
    return lax.dot_general(a, b, (((0,), (0,)), ((), ())), preferred_element_type=F32)


def _dot_nt(a, b):
    return lax.dot_general(a, b, (((1,), (1,)), ((), ())), preferred_element_type=F32)


RET_HG = 4


def _retention_kernel(q_ref, k_ref, v_ref, g_ref, kc_ref, vc_ref, cos_ref, sin_ref, df_ref, db_ref, gn_ref,
                      o_ref, qs_ref, ks_ref, st_ref):
    for hh in range(RET_HG):
        _retention_head(hh, slice(hh * LANES, (hh + 1) * LANES), q_ref, k_ref, v_ref, g_ref, kc_ref, vc_ref,
                        cos_ref, sin_ref, df_ref, db_ref, gn_ref, o_ref, qs_ref, ks_ref, st_ref)


def _retention_head(hh, hs, q_ref, k_ref, v_ref, g_ref, kc_ref, vc_ref, cos_ref, sin_ref, df_ref, db_ref, gn_ref,
                    o_ref, qs_ref, ks_ref, st_ref):
    k_scale = RET_DK ** -0.5
    lgf = jax.nn.log_sigmoid(df_ref[hh])
    lgb = jax.nn.log_sigmoid(db_ref[hh])

    cos = cos_ref[...]
    sin = sin_ref[...]
    qs_ref[hh] = _rope(q_ref[:, hs].astype(F32), cos, sin, RET_DK // 4).astype(BF16)
    ks_ref[hh] = _rope(k_ref[:, hs].astype(F32), cos, sin, RET_DK // 4) * k_scale

    rowi = lax.broadcasted_iota(I32, (CHUNK, LANES), 0).astype(F32)
    coli = lax.broadcasted_iota(I32, (CHUNK, LANES), 1).astype(F32)
    diff = rowi - coli
    decay = jnp.exp(jnp.where(diff >= 0, lgf * diff, -lgb * diff)) * jnp.where(diff == 0, 2.0, 1.0)
    zeta_f = jnp.exp(lgf * (CHUNK - 1.0 - rowi))
    eta_b = jnp.exp(lgb * rowi)
    xi_f = jnp.exp(lgf * (rowi + 1.0))
    xi_b = jnp.exp(lgb * (CHUNK - rowi))
    cdec_f = jnp.exp(lgf * float(CHUNK))
    cdec_b = jnp.exp(lgb * float(CHUNK))

    crow = lax.broadcasted_iota(I32, (CTX_LEN, LANES), 0).astype(F32)
    kc = kc_ref[:, hs].astype(F32) * k_scale
    vc = vc_ref[:, hs]
    s_f = _dot_tn((kc * jnp.exp(lgf * (CTX_LEN - 1.0 - crow))).astype(BF16), vc)
    s_b = _dot_tn((kc * jnp.exp(lgb * crow)).astype(BF16), vc)

    upd_f, upd_b = [], []
    for i in range(N_CHUNKS):
        rows = pl.ds(i * CHUNK, CHUNK)
        kch = ks_ref[hh, rows, :]
        vch = v_ref[rows, hs]
        upd_f.append(_dot_tn((kch * zeta_f).astype(BF16), vch))
        upd_b.append(_dot_tn((kch * eta_b).astype(BF16), vch))
    state = s_f
    for i in range(N_CHUNKS):
        st_ref[hh, i, :, :RET_DV] = state.astype(BF16)
        state = cdec_f * state + upd_f[i]
    state = s_b
    for i in reversed(range(N_CHUNKS)):
        st_ref[hh, i, :, RET_DV:] = state.astype(BF16)
        state = cdec_b * state + upd_b[i]

    gn_w = gn_ref[:, hs]
    for i in range(N_CHUNKS):
        rows = pl.ds(i * CHUNK, CHUNK)
        qch = qs_ref[hh, rows, :]
        scores = _dot_nt(qch, ks_ref[hh, rows, :].astype(BF16)) * decay
        o = jnp.dot(scores.astype(BF16), v_ref[rows, hs], preferred_element_type=F32)
        cross = jnp.dot(qch, st_ref[hh, i], preferred_element_type=F32)
        o = o + xi_f * cross[:, :RET_DV] + xi_b * cross[:, RET_DV:]
        y = _plain_norm(o) * gn_w
        o_ref[rows, hs] = (_silu(g_ref[rows, hs].astype(F32)) * y).astype(BF16)


def _retention(proj, proj_c, cos, sin, decay_rows, gn_w):
    width = RET_HG * LANES
    groups = RET_HEADS // RET_HG
    blk = lambda off: pl.BlockSpec((SEQ, width), lambda b, hg: (b, off // width + hg))
    blk_c = lambda off: pl.BlockSpec((CTX_LEN, width), lambda b, hg: (b, off // width + hg))
    table = pl.BlockSpec((SEQ, LANES), lambda b, hg: (0, 0))
    return pl.pallas_call(
        _retention_kernel,
        grid=(BATCH, groups),
        in_specs=[
            blk(OFF_Q), blk(OFF_K), blk(OFF_V), blk(OFF_G), blk_c(CTX_OFF_K), blk_c(CTX_OFF_V), table, table,
            pl.BlockSpec((RET_HG, 1, LANES), lambda b, hg: (hg, 0, 0)),
            pl.BlockSpec((RET_HG, 1, LANES), lambda b, hg: (groups + hg, 0, 0)),
            pl.BlockSpec((1, width), lambda b, hg: (0, hg)),
        ],
        out_specs=pl.BlockSpec((SEQ, width), lambda b, hg: (b, hg)),
        out_shape=jax.ShapeDtypeStruct((N_TOK, RET_W), BF16),
        scratch_shapes=[
            pltpu.VMEM((RET_HG, SEQ, RET_DK), BF16),
            pltpu.VMEM((RET_HG, SEQ, RET_DK), F32),
            pltpu.VMEM((RET_HG, N_CHUNKS, RET_DK, 2 * RET_DV), BF16),
        ],
        compiler_params=_params(("arbitrary", "arbitrary")),
        name="retention",
    )(proj, proj, proj, proj, proj_c, proj_c, cos, sin, decay_rows, decay_rows, gn_w)


MLA_TM = 512


def _rms_norm(x, w):
    return x * lax.rsqrt(jnp.mean(x * x, -1, keepdims=True) + EPS) * w


def _mla_kv(ckv_ref, kpe_ref, kvn_ref, wkv_ref, cos_ref, sin_ref, k_ref, v_ref, rotate):
    ckv = _rms_norm(ckv_ref[...].astype(F32), kvn_ref[...]).astype(BF16)
    kv = jnp.dot(ckv, wkv_ref[...], preferred_element_type=F32)
    lane = lax.broadcasted_iota(I32, (ckv.shape[0], LANES), 1)
    kpe = jnp.where(lane < MLA_ROPE, kpe_ref[...].astype(F32), 0.0)
    if rotate:
        kpe = _rope(kpe, cos_ref[...], sin_ref[...], MLA_ROPE // 4)
    kpe = kpe.astype(BF16)
    for h in range(MLA_HEADS):
        k_ref[:, h * QK_PAD:h * QK_PAD + MLA_NOPE] = kv[:, 2 * h * LANES:(2 * h + 1) * LANES].astype(BF16)
        k_ref[:, h * QK_PAD + MLA_NOPE:(h + 1) * QK_PAD] = kpe
        v_ref[:, h * MLA_DV:(h + 1) * MLA_DV] = kv[:, (2 * h + 1) * LANES:(2 * h + 2) * LANES].astype(BF16)


def _mla_latent_kernel(cq_ref, ckv_ref, kpe_ref, qn_ref, kvn_ref, wq_ref, wkv_ref, cos_ref, sin_ref,
                       q_ref, k_ref, v_ref):
    cq = _rms_norm(cq_ref[...].astype(F32), qn_ref[...]).astype(BF16)
    q = jnp.dot(cq, wq_ref[...], preferred_element_type=F32)
    cos = cos_ref[...]
    sin = sin_ref[...]
    scale = MLA_DQ ** -0.5 * LOG2_E
    for h in range(MLA_HEADS):
        lo = h * QK_PAD
        q_ref[:, lo:lo + MLA_NOPE] = (q[:, lo:lo + MLA_NOPE] * scale).astype(BF16)
        qpe = _rope(q[:, lo + MLA_NOPE:lo + QK_PAD], cos, sin, MLA_ROPE // 4)
        q_ref[:, lo + MLA_NOPE:lo + QK_PAD] = (qpe * scale).astype(BF16)
    _mla_kv(ckv_ref, kpe_ref, kvn_ref, wkv_ref, cos_ref, sin_ref, k_ref, v_ref, rotate=True)


def _mla_context_kernel(ckv_ref, kpe_ref, kvn_ref, wkv_ref, k_ref, v_ref):
    _mla_kv(ckv_ref, kpe_ref, kvn_ref, wkv_ref, None, None, k_ref, v_ref, rotate=False)


def _mla_latent(proj, q_norm, kv_norm, wq_pad, wkv, cos, sin):
    row = lambda w, off: pl.BlockSpec((MLA_TM, w), lambda i: (i, off // w))
    full = lambda a: pl.BlockSpec(a.shape, lambda i: (0, 0))
    table = pl.BlockSpec((MLA_TM, LANES), lambda i: (i % (SEQ // MLA_TM), 0))
    return pl.pallas_call(
        _mla_latent_kernel,
        grid=(N_TOK // MLA_TM,),
        in_specs=[row(MLA_Q_LORA, OFF_CQ), row(MLA_KV_LORA, OFF_CKV), row(LANES, OFF_KPE),
                  full(q_norm), full(kv_norm), full(wq_pad), full(wkv), table, table],
        out_specs=[
            pl.BlockSpec((MLA_TM, MLA_HEADS * QK_PAD), lambda i: (i, 0)),
            pl.BlockSpec((MLA_TM, MLA_HEADS * QK_PAD), lambda i: (i, 0)),
            pl.BlockSpec((MLA_TM, MLA_W), lambda i: (i, 0)),
        ],
        out_shape=[
            jax.ShapeDtypeStruct((N_TOK, MLA_HEADS * QK_PAD), BF16),
            jax.ShapeDtypeStruct((N_TOK, MLA_HEADS * QK_PAD), BF16),
            jax.ShapeDtypeStruct((N_TOK, MLA_W), BF16),
        ],
        compiler_params=_params(("arbitrary",)),
        name="mla_latent",
    )(proj, proj, proj, q_norm, kv_norm, wq_pad, wkv, cos, sin)


def _mla_context(proj_c, kv_norm, wkv):
    row = lambda w, off: pl.BlockSpec((MLA_TM, w), lambda i: (i, off // w))
    full = lambda a: pl.BlockSpec(a.shape, lambda i: (0, 0))
    return pl.pallas_call(
        _mla_context_kernel,
        grid=(N_CTX // MLA_TM,),
        in_specs=[row(MLA_KV_LORA, CTX_OFF_CKV), row(LANES, CTX_OFF_KPE), full(kv_norm), full(wkv)],
        out_specs=[
            pl.BlockSpec((MLA_TM, MLA_HEADS * QK_PAD), lambda i: (i, 0)),
            pl.BlockSpec((MLA_TM, MLA_W), lambda i: (i, 0)),
        ],
        out_shape=[
            jax.ShapeDtypeStruct((N_CTX, MLA_HEADS * QK_PAD), BF16),
            jax.ShapeDtypeStruct((N_CTX, MLA_W), BF16),
        ],
        compiler_params=_params(("arbitrary",)),
        name="mla_context",
    )(proj_c, proj_c, kv_norm, wkv)


ATT_TQ = 256


def _attention_kernel(q_ref, kc_ref, kl_ref, vc_ref, vl_ref, wo_ref, o_ref, wo_bf_ref, vext_ref):
    wo_bf_ref[...] = wo_ref[...].astype(BF16)

    @pl.when(pl.program_id(1) == 0)
    def _():
        lane = lax.broadcasted_iota(I32, (CTX_LEN + SEQ, MLA_DV), 1)
        ones_col = jnp.where(lane == 0, 1.0, 0.0).astype(BF16)
        for h in range(MLA_HEADS):
            dv = slice(h * MLA_DV, (h + 1) * MLA_DV)
            vext_ref[h, :CTX_LEN, :MLA_DV] = vc_ref[:, dv]
            vext_ref[h, CTX_LEN:, :MLA_DV] = vl_ref[:, dv]
            vext_ref[h, :, MLA_DV:] = ones_col

    for h in range(MLA_HEADS):
        qk = slice(h * QK_PAD, (h + 1) * QK_PAD)
        q = q_ref[:, qk]
        s_c = _dot_nt(q, kc_ref[:, qk])
        s_l = _dot_nt(q, kl_ref[:, qk])
        m = jnp.maximum(jnp.max(s_c, -1, keepdims=True), jnp.max(s_l, -1, keepdims=True))
        p_c = jnp.exp2(s_c - m).astype(BF16)
        p_l = jnp.exp2(s_l - m).astype(BF16)
        o = (jnp.dot(p_c, vext_ref[h, :CTX_LEN, :], preferred_element_type=F32)
             + jnp.dot(p_l, vext_ref[h, CTX_LEN:, :], preferred_element_type=F32))
        o_ref[:, h * MLA_DV:(h + 1) * MLA_DV] = (o[:, :MLA_DV] / o[:, MLA_DV:MLA_DV + 1]).astype(BF16)


def _attention(q, k_ctx, k_lat, v_ctx, v_lat, w_o):
    tiles = SEQ // ATT_TQ
    band = w_o.shape[0] // (BATCH * tiles)
    step = lambda b, i: (b * tiles + i, 0)
    return pl.pallas_call(
        _attention_kernel,
        grid=(BATCH, tiles),
        in_specs=[
            pl.BlockSpec((ATT_TQ, MLA_HEADS * QK_PAD), step),
            pl.BlockSpec((CTX_LEN, MLA_HEADS * QK_PAD), lambda b, i: (b, 0)),
            pl.BlockSpec((SEQ, MLA_HEADS * QK_PAD), lambda b, i: (b, 0)),
            pl.BlockSpec((CTX_LEN, MLA_W), lambda b, i: (b, 0)),
            pl.BlockSpec((SEQ, MLA_W), lambda b, i: (b, 0)),
            pl.BlockSpec((band, D_MODEL), step),
        ],
        out_specs=[pl.BlockSpec((ATT_TQ, MLA_W), step), pl.BlockSpec((band, D_MODEL), step)],
        out_shape=[jax.ShapeDtypeStruct((N_TOK, MLA_W), BF16), jax.ShapeDtypeStruct(w_o.shape, BF16)],
        scratch_shapes=[pltpu.VMEM((MLA_HEADS, CTX_LEN + SEQ, 2 * MLA_DV), BF16)],
        compiler_params=_params(("arbitrary", "arbitrary")),
        name="attention",
    )(q, k_ctx, k_lat, v_ctx, v_lat, w_o)


OUT_TM = 512
OUT_SUB = 256
ROUTE_E1, ROUTE_E2, ROUTE_R1, ROUTE_R2, ROUTE_W1, ROUTE_W2 = range(6)


def _outproj_kernel(ret_ref, att_ref, x_ref, wo_ref, gate_ref, shift_ref, scale_ref, lnw_ref, lnb_ref,
                    wr_ref, br_ref, h_ref, t_ref, route_ref, route_t_ref, count_ref, carry_ref, wsplit_ref):
    @pl.when(pl.program_id(0) == 0)
    def _():
        carry_ref[...] = jnp.zeros_like(carry_ref)
        w = wr_ref[...]
        w_hi = w.astype(BF16)
        wsplit_ref[:, :LANES] = w_hi
        wsplit_ref[:, LANES:] = (w - w_hi.astype(F32)).astype(BF16)

    subtiles = [slice(s * OUT_SUB, (s + 1) * OUT_SUB) for s in range(OUT_TM // OUT_SUB)]
    mixes = [jnp.dot(ret_ref[rows, :], wo_ref[:RET_W, :], preferred_element_type=F32)
             + jnp.dot(att_ref[rows, :], wo_ref[RET_W:, :], preferred_element_type=F32) for rows in subtiles]
    logits = [_outproj_norms(rows, mix, x_ref, gate_ref, shift_ref, scale_ref, lnw_ref, lnb_ref, br_ref, h_ref, t_ref,
                             wsplit_ref) for rows, mix in zip(subtiles, mixes)]
    _route(jnp.concatenate(logits, 0), carry_ref, count_ref, route_ref, route_t_ref)


def _outproj_norms(rows, mix, x_ref, gate_ref, shift_ref, scale_ref, lnw_ref, lnb_ref, br_ref, h_ref, t_ref,
                   wsplit_ref):
    h = _plain_norm(DEEPNORM_ALPHA * x_ref[rows, :] + gate_ref[0] * mix) * lnw_ref[...] + lnb_ref[...]
    h_ref[rows, :] = h
    t = _plain_norm(h) * (1.0 + scale_ref[0]) + shift_ref[0]
    t_ref[rows] = _to_token_tiles(t)

    t_hi = t.astype(BF16)
    t_lo = (t - t_hi.astype(F32)).astype(BF16)
    main = jnp.dot(t_hi, wsplit_ref[...], preferred_element_type=F32)
    corr = jnp.dot(t_lo, wsplit_ref[:, :LANES], preferred_element_type=F32)
    return main[:, :LANES] + (main[:, LANES:] + corr) + br_ref[...]


def _route(logits, carry_ref, count_ref, route_ref, route_t_ref):
    lane = lax.broadcasted_iota(I32, logits.shape, 1).astype(F32)
    neg = -jnp.inf
    big = float(LANES)

    def first_lane_of(mask):
        return jnp.min(jnp.where(mask, lane, big), -1, keepdims=True)

    is_group = lane < N_GROUPS
    gl = jnp.where(is_group, logits, neg)
    g_max = jnp.max(gl, -1, keepdims=True)
    g_idx = first_lane_of(is_group & (gl == g_max))
    g_prob = 1.0 / jnp.sum(jnp.where(is_group, jnp.exp(logits - g_max), 0.0), -1, keepdims=True)

    lo = N_GROUPS + g_idx * EXPERTS_PER_GROUP
    in_group = (lane >= lo) & (lane < lo + EXPERTS_PER_GROUP)
    el = jnp.where(in_group, logits, neg)
    v1 = jnp.max(el, -1, keepdims=True)
    i1 = first_lane_of(in_group & (el == v1))
    rest = in_group & (lane != i1)
    el2 = jnp.where(rest, logits, neg)
    v2 = jnp.max(el2, -1, keepdims=True)
    i2 = first_lane_of(rest & (el2 == v2))
    d = jnp.exp(v2 - v1)
    w1 = g_prob / (1.0 + d)
    w2 = g_prob * d / (1.0 + d)

    onehot = jnp.where((lane == i1) | (lane == i2), 1.0, 0.0)
    r = lax.broadcasted_iota(I32, (OUT_TM, OUT_TM), 0)
    c = lax.broadcasted_iota(I32, (OUT_TM, OUT_TM), 1)
    tri = jnp.where(c < r, 1.0, 0.0).astype(BF16)
    carry = carry_ref[0:1, :]
    before = jnp.dot(tri, onehot.astype(BF16), preferred_element_type=F32) + carry
    r1 = jnp.sum(jnp.where(lane == i1, before, 0.0), -1, keepdims=True)
    r2 = jnp.sum(jnp.where(lane == i2, before, 0.0), -1, keepdims=True)
    carry = carry + jnp.sum(onehot, 0, keepdims=True)
    carry_ref[...] = jnp.broadcast_to(carry, carry_ref.shape)
    count_ref[...] = jnp.broadcast_to(carry, count_ref.shape)

    rec = jnp.zeros_like(logits)
    for slot, val in ((ROUTE_E1, i1 - N_GROUPS), (ROUTE_E2, i2 - N_GROUPS), (ROUTE_R1, r1), (ROUTE_R2, r2),
                      (ROUTE_W1, w1), (ROUTE_W2, w2)):
        rec = jnp.where(lane == slot, val, rec)
    route_ref[...] = rec
    route_t_ref[...] = rec.T[:8, :]


def _outproj(ret, att, x2d, wo_bf, mod_rows, ln_w, ln_b, w_route, b_route):
    tiles_per_sample = SEQ // OUT_TM
    row = lambda w: pl.BlockSpec((OUT_TM, w), lambda i: (i, 0))
    full = lambda a: pl.BlockSpec(a.shape, lambda i: (0, 0))
    mod = lambda j: pl.BlockSpec((1, 1, D_MODEL), lambda i: ((i // tiles_per_sample) * 6 + j, 0, 0))
    return pl.pallas_call(
        _outproj_kernel,
        grid=(N_TOK // OUT_TM,),
        in_specs=[row(RET_W), row(MLA_W), row(D_MODEL), full(wo_bf), mod(2), mod(3), mod(4),
                  full(ln_w), full(ln_b), full(w_route), full(b_route)],
        out_specs=[row(D_MODEL), pl.BlockSpec((OUT_TM, TOKEN_SUB, LANES), lambda i: (i, 0, 0)), row(LANES),
                   pl.BlockSpec((8, OUT_TM), lambda i: (0, i)), pl.BlockSpec((8, LANES), lambda i: (0, 0))],
        out_shape=[
            jax.ShapeDtypeStruct((N_TOK, D_MODEL), F32),
            jax.ShapeDtypeStruct((N_TOK, TOKEN_SUB, LANES), BF16),
            jax.ShapeDtypeStruct((N_TOK, LANES), F32),
            jax.ShapeDtypeStruct((8, N_TOK), F32),
            jax.ShapeDtypeStruct((8, LANES), F32),
        ],
        scratch_shapes=[pltpu.VMEM((8, LANES), F32), pltpu.VMEM((D_MODEL, 2 * LANES), BF16)],
        compiler_params=_params(("arbitrary",)),
        name="outproj_route",
    )(ret, att, x2d, wo_bf, mod_rows, mod_rows, mod_rows, ln_w, ln_b, w_route, b_route)


DISPATCH_TM = 256


def _dispatch_kernel(pos_ref, t_ref, xs_ref, sem):
    i = pl.program_id(0)

    def copy(slot, j):
        return pltpu.make_async_copy(t_ref.at[j], xs_ref.at[pos_ref[slot * N_TOK + i * DISPATCH_TM + j]], sem)

    for slot in range(2):
        lax.fori_loop(0, DISPATCH_TM, lambda j, c, slot=slot: (copy(slot, j).start(), c)[1], 0, unroll=8)
    for slot in range(2):
        pltpu.make_async_copy(t_ref, xs_ref.at[pl.ds(0, DISPATCH_TM)], sem).wait()


def _dispatch(pos, t_tiles):
    return pl.pallas_call(
        _dispatch_kernel,
        grid_spec=pltpu.PrefetchScalarGridSpec(
            num_scalar_prefetch=1,
            grid=(N_TOK // DISPATCH_TM,),
            in_specs=[pl.BlockSpec((DISPATCH_TM, TOKEN_SUB, LANES), lambda i, pos: (i, 0, 0))],
            out_specs=pl.BlockSpec(memory_space=pl.ANY),
            scratch_shapes=[pltpu.SemaphoreType.DMA(())],
        ),
        out_shape=jax.ShapeDtypeStruct((N_PAIRS, TOKEN_SUB, LANES), BF16),
        compiler_params=_params(("arbitrary",)),
        name="dispatch",
    )(pos, t_tiles)


def _experts_kernel(rend_ref, xs_ref, wg_ref, wu_ref, wd_ref, ys_ref, wgu_ref, wdn_ref, xbuf_ref, ybuf_ref,
                    xsem, ysem):
    e = pl.program_id(0)
    lo = jnp.where(e == 0, 0, rend_ref[jnp.maximum(e - 1, 0)])
    hi = rend_ref[e]

    def x_copy(g):
        return pltpu.make_async_copy(xs_ref.at[pl.ds(g * ROW_TILE, ROW_TILE)], xbuf_ref.at[g % 2], xsem.at[g % 2])

    def y_copy(g):
        return pltpu.make_async_copy(ybuf_ref.at[g % 2], ys_ref.at[pl.ds(g * ROW_TILE, ROW_TILE)], ysem.at[g % 2])

    @pl.when(e == 0)
    def _():
        x_copy(0).start()

    @pl.when(hi > lo)
    def _():
        wgu_ref[:, :D_EXPERT] = wg_ref[0].astype(BF16)
        wgu_ref[:, D_EXPERT:] = wu_ref[0].astype(BF16)
        wdn_ref[...] = wd_ref[0].astype(BF16)

        def tile(g, c):
            row0 = g * ROW_TILE
            owns_first_row = lo <= row0
            owns_last_row = hi >= row0 + ROW_TILE

            @pl.when(owns_first_row)
            def _():
                @pl.when(g + 1 < N_TILES)
                def _():
                    x_copy(g + 1).start()

                x_copy(g).wait()

            gu = jnp.dot(_from_token_tiles(xbuf_ref[g % 2]), wgu_ref[...], preferred_element_type=F32)
            hid = _silu(gu[:, :D_EXPERT]) * gu[:, D_EXPERT:]
            y = jnp.dot(hid.astype(BF16), wdn_ref[...], preferred_element_type=F32)

            @pl.when(owns_first_row)
            def _():
                @pl.when(g >= 2)
                def _():
                    y_copy(g - 2).wait()

                ybuf_ref[g % 2] = _to_token_tiles(y)

            @pl.when(jnp.logical_not(owns_first_row))
            def _():
                row = lax.broadcasted_iota(I32, y.shape, 0)
                earlier = _from_token_tiles(ybuf_ref[g % 2]).astype(F32)
                ybuf_ref[g % 2] = _to_token_tiles(jnp.where(row >= lo - row0, y, earlier))

            @pl.when(owns_last_row)
            def _():
                y_copy(g).start()

            return c

        lax.fori_loop(lo // ROW_TILE, (hi - 1) // ROW_TILE + 1, tile, 0)

    @pl.when(e == N_EXPERTS - 1)
    def _():
        y_copy(N_TILES - 2).wait()
        y_copy(N_TILES - 1).wait()


def _experts(row_end, xs, w_gate, w_up, w_down):
    w_gate = w_gate.reshape(N_EXPERTS, D_MODEL, D_EXPERT)
    w_up = w_up.reshape(N_EXPERTS, D_MODEL, D_EXPERT)
    w_down = w_down.reshape(N_EXPERTS, D_EXPERT, D_MODEL)
    expert = lambda e, rend: (e, 0, 0)
    tile_buf = pltpu.VMEM((2, ROW_TILE, TOKEN_SUB, LANES), BF16)
    return pl.pallas_call(
        _experts_kernel,
        grid_spec=pltpu.PrefetchScalarGridSpec(
            num_scalar_prefetch=1,
            grid=(N_EXPERTS,),
            in_specs=[
                pl.BlockSpec(memory_space=pl.ANY),
                pl.BlockSpec((1, D_MODEL, D_EXPERT), expert),
                pl.BlockSpec((1, D_MODEL, D_EXPERT), expert),
                pl.BlockSpec((1, D_EXPERT, D_MODEL), expert),
            ],
            out_specs=pl.BlockSpec(memory_space=pl.ANY),
            scratch_shapes=[pltpu.VMEM((D_MODEL, 2 * D_EXPERT), BF16), pltpu.VMEM((D_EXPERT, D_MODEL), BF16),
                            tile_buf, tile_buf, pltpu.SemaphoreType.DMA((2,)), pltpu.SemaphoreType.DMA((2,))],
        ),
        out_shape=jax.ShapeDtypeStruct((N_PAIRS, TOKEN_SUB, LANES), BF16),
        compiler_params=_params(("arbitrary",)),
        name="experts",
    )(row_end, xs, w_gate, w_up, w_down)


COMB_TM = 256


def _combine_kernel(pos_ref, ys_ref, h_ref, route_ref, gate_ref, lnw_ref, lnb_ref, o_ref, buf_ref, sem):
    i = pl.program_id(0)
    n = pl.num_programs(0)

    def copy(step, slot, pair, j):
        src = pos_ref[pair * N_TOK + step * COMB_TM + j]
        return pltpu.make_async_copy(ys_ref.at[src], buf_ref.at[slot, pair * COMB_TM + j], sem.at[slot])

    def start_all(step, slot):
        for pair in range(2):
            lax.fori_loop(0, COMB_TM, lambda j, c, pair=pair: (copy(step, slot, pair, j).start(), c)[1], 0,
                          unroll=8)

    def wait_all(step, slot):
        pltpu.make_async_copy(ys_ref.at[pl.ds(0, 2 * COMB_TM)], buf_ref.at[slot], sem.at[slot]).wait()

    @pl.when(i == 0)
    def _():
        start_all(0, 0)

    @pl.when(i + 1 < n)
    def _():
        start_all(i + 1, (i + 1) % 2)

    slot = i % 2
    wait_all(i, slot)

    y1 = _from_token_tiles(buf_ref[slot, :COMB_TM]).astype(F32)
    y2 = _from_token_tiles(buf_ref[slot, COMB_TM:]).astype(F32)
    y = route_ref[:, ROUTE_W1:ROUTE_W1 + 1] * y1 + route_ref[:, ROUTE_W2:ROUTE_W2 + 1] * y2
    z = DEEPNORM_ALPHA * h_ref[...] + gate_ref[0] * y
    o_ref[...] = _plain_norm(z) * lnw_ref[...] + lnb_ref[...]


def _combine(pos, ys, h, route, mod_rows, ln_w, ln_b):
    tiles_per_sample = SEQ // COMB_TM
    row = lambda w: pl.BlockSpec((COMB_TM, w), lambda i, pos: (i, 0))
    full = lambda a: pl.BlockSpec(a.shape, lambda i, pos: (0, 0))
    return pl.pallas_call(
        _combine_kernel,
        grid_spec=pltpu.PrefetchScalarGridSpec(
            num_scalar_prefetch=1,
            grid=(N_TOK // COMB_TM,),
            in_specs=[
                pl.BlockSpec(memory_space=pl.ANY), row(D_MODEL), row(LANES),
                pl.BlockSpec((1, 1, D_MODEL), lambda i, pos: ((i // tiles_per_sample) * 6 + 5, 0, 0)),
                full(ln_w), full(ln_b),
            ],
            out_specs=row(D_MODEL),
            scratch_shapes=[pltpu.VMEM((2, 2 * COMB_TM, TOKEN_SUB, LANES), BF16), pltpu.SemaphoreType.DMA((2,))],
        ),
        out_shape=jax.ShapeDtypeStruct((N_TOK, D_MODEL), F32),
        compiler_params=_params(("arbitrary",)),
        name="combine",
    )(pos, ys, h, route, mod_rows, ln_w, ln_b)


def _routing_tables(route_t, counts):
    cnt = counts[0, N_GROUPS:N_ROUTE].astype(I32)
    row_end = jnp.cumsum(cnt)
    e = route_t[ROUTE_E1:ROUTE_E2 + 1].astype(I32)
    rank = route_t[ROUTE_R1:ROUTE_R2 + 1].astype(I32)
    earlier = jnp.arange(N_EXPERTS, dtype=I32)[:, None, None] < e[None]
    base = jnp.sum(jnp.where(earlier, cnt[:, None, None], 0), 0)
    pos = (base + rank).reshape(-1)
    return pos, row_end


def kernel(x, c, ctx, c_ctx, w_ada, b_ada, w_in, ret_decay, ret_gn_w, mla_q_norm, mla_kv_norm, w_uq, w_ukv, w_o,
           ln1_w, ln1_b, router_group_w, router_group_b, router_expert_w, router_expert_b, expert_w_gate,
           expert_w_up, expert_w_down, ln2_w, ln2_b):
    x2d = x.reshape(N_TOK, D_MODEL)
    ctx2d = ctx.reshape(N_CTX, D_MODEL)

    cc = jnp.zeros((8, D_MODEL), F32).at[:BATCH].set(c).at[BATCH].set(c_ctx)
    mod = _ada(cc, w_ada[0], b_ada)
    mod_rows = mod.reshape(8 * 6, 1, D_MODEL)

    proj, w_in_blocks = _inproj_latent(x2d, mod_rows, w_in)
    proj_c = _inproj_context(ctx2d, mod_rows, w_in_blocks)

    cos_r, sin_r = _rope_tables(RET_DK)
    decay_rows = jnp.broadcast_to(ret_decay[0].reshape(2 * RET_HEADS, 1, 1), (2 * RET_HEADS, 1, LANES))
    ret = _retention(proj, proj_c, jnp.asarray(cos_r), jnp.asarray(sin_r), decay_rows, ret_gn_w)

    cos_m, sin_m = _rope_tables(MLA_ROPE)
    cos_m = np.concatenate([cos_m, np.ones_like(cos_m)], 1)
    sin_m = np.concatenate([sin_m, np.zeros_like(sin_m)], 1)
    wq = w_uq[0].reshape(MLA_Q_LORA, MLA_HEADS, MLA_DQ)
    wq_pad = jnp.pad(wq, ((0, 0), (0, 0), (0, QK_PAD - MLA_DQ))).reshape(MLA_Q_LORA, MLA_HEADS * QK_PAD).astype(BF16)
    wkv = w_ukv[0].astype(BF16)
    q, k_lat, v_lat = _mla_latent(proj, mla_q_norm, mla_kv_norm, wq_pad, wkv, jnp.asarray(cos_m), jnp.asarray(sin_m))
    k_ctx, v_ctx = _mla_context(proj_c, mla_kv_norm, wkv)
    att, w_o_bf = _attention(q, k_ctx, k_lat, v_ctx, v_lat, w_o[0])

    w_route = jnp.concatenate(
        [router_group_w[0], router_expert_w[0].transpose(1, 0, 2).reshape(D_MODEL, N_EXPERTS),
         jnp.zeros((D_MODEL, LANES - N_ROUTE), F32)], 1)
    b_route = jnp.concatenate(
        [router_group_b[0], router_expert_b[0].reshape(N_EXPERTS), jnp.zeros((LANES - N_ROUTE,), F32)])[None]
    h, t_tiles, route, route_t, counts = _outproj(ret, att, x2d, w_o_bf, mod_rows, ln1_w, ln1_b,
                                                   w_route, b_route)

    pos, row_end = _routing_tables(route_t, counts)
    xs = _dispatch(pos, t_tiles)
    ys = _experts(row_end, xs, expert_w_gate[0], expert_w_up[0], expert_w_down[0])
    out = _combine(pos, ys, h, route, mod_rows, ln2_w, ln2_b)
    return out.reshape(BATCH, SEQ, D_MODEL)
```

```python
import numpy as np
import jax
import jax.numpy as jnp
from jax import lax
from jax.experimental import pallas as pl
from jax.experimental.pallas import tpu as pltpu

F32 = jnp.float32
BF16 = jnp.bfloat16
I32 = jnp.int32

D_MODEL = 2048
BATCH = 4
SEQ = 2048
GRID_W = 64
CTX_LEN = 256
N_TOK = BATCH * SEQ
N_CTX = BATCH * CTX_LEN

RET_HEADS = 8
RET_DK = 128
RET_DV = 128
RET_W = RET_HEADS * RET_DV
CHUNK = 128
N_CHUNKS = SEQ // CHUNK

MLA_HEADS = 8
MLA_Q_LORA = 512
MLA_KV_LORA = 256
MLA_NOPE = 128
MLA_ROPE = 64
MLA_DV = 128
MLA_W = MLA_HEADS * MLA_DV
MLA_DQ = MLA_NOPE + MLA_ROPE
QK_PAD = 256

IN_SIZES = (RET_HEADS * RET_DK, RET_HEADS * RET_DK, RET_W, RET_W, MLA_Q_LORA, MLA_KV_LORA, MLA_ROPE)
IN_W = sum(IN_SIZES)
OFF_Q, OFF_K, OFF_V, OFF_G, OFF_CQ, OFF_CKV, OFF_KPE = (int(v) for v in np.cumsum((0,) + IN_SIZES[:-1]))

N_GROUPS = 4
EXPERTS_PER_GROUP = 8
N_EXPERTS = N_GROUPS * EXPERTS_PER_GROUP
D_EXPERT = 512
N_ROUTE = N_GROUPS + N_EXPERTS

LOG2_E = float(np.log2(np.e))
ROPE_BASE = 10000.0
EPS = 1e-6
DEPTH = 1
DEEPNORM_ALPHA = (2.0 * DEPTH) ** 0.25

LANES = 128
ROW_TILE = 256
N_PAIRS = 2 * N_TOK
N_TILES = N_PAIRS // ROW_TILE
TOKEN_SUB = D_MODEL // LANES

VMEM_LIMIT = 56 * 1024 * 1024


def _params(sem, vmem=VMEM_LIMIT):
    return pltpu.CompilerParams(dimension_semantics=sem, vmem_limit_bytes=vmem)


def _silu(x):
    return x * (1.0 / (1.0 + jnp.exp(-x)))


def _plain_norm(x):
    mu = jnp.mean(x, -1, keepdims=True)
    xc = x - mu
    var = jnp.mean(xc * xc, -1, keepdims=True)
    return xc * lax.rsqrt(var + EPS)


def _to_token_tiles(x):
    return x.astype(BF16).reshape(x.shape[0], TOKEN_SUB, LANES)


def _from_token_tiles(x):
    return x.reshape(x.shape[0], D_MODEL)


ADA_TN = 1024


ADA_BANDS = 4


def _ada_kernel(cc_ref, *refs):
    w_refs, b_ref, o_ref = refs[:ADA_BANDS], refs[ADA_BANDS], refs[ADA_BANDS + 1]
    s = _silu(cc_ref[...])
    band = D_MODEL // ADA_BANDS
    acc = b_ref[...]
    for q, w_ref in enumerate(w_refs):
        acc = acc + jnp.dot(s[:, q * band:(q + 1) * band], w_ref[...], preferred_element_type=F32,
                            precision=lax.Precision.HIGHEST)
    o_ref[...] = acc


def _ada(cc, w_ada, b_ada):
    n = w_ada.shape[1]
    band = D_MODEL // ADA_BANDS
    return pl.pallas_call(
        _ada_kernel,
        grid=(n // ADA_TN,),
        in_specs=[pl.BlockSpec((8, D_MODEL), lambda j: (0, 0))]
        + [pl.BlockSpec((band, ADA_TN), lambda j, q=q: (q, j)) for q in range(ADA_BANDS)]
        + [pl.BlockSpec((1, ADA_TN), lambda j: (0, j))],
        out_specs=pl.BlockSpec((8, ADA_TN), lambda j: (0, j)),
        out_shape=jax.ShapeDtypeStruct((8, n), F32),
        compiler_params=_params(("arbitrary",)),
        name="ada",
    )(cc, *([w_ada] * ADA_BANDS), b_ada)


INPROJ_TM = 1024
INPROJ_TN = 1024
assert OFF_K == INPROJ_TN and OFF_V == 2 * INPROJ_TN and OFF_CQ == 4 * INPROJ_TN
CTX_OFF_K, CTX_OFF_V = 0, INPROJ_TN
CTX_OFF_CKV = 2 * INPROJ_TN + (OFF_CKV - OFF_CQ)
CTX_OFF_KPE = 2 * INPROJ_TN + (OFF_KPE - OFF_CQ)


INPROJ_SUB = 256
INPROJ_BLOCKS = -(-IN_W // INPROJ_TN)


def _inproj_products(x_ref, shift_ref, scale_ref, w, o_ref, xn_ref):
    first = pl.program_id(1) == 0

    @pl.when(first)
    def _():
        for s in range(INPROJ_TM // INPROJ_SUB):
            rows = slice(s * INPROJ_SUB, (s + 1) * INPROJ_SUB)
            y = (_plain_norm(x_ref[rows, :]) * (1.0 + scale_ref[0]) + shift_ref[0]).astype(BF16)
            xn_ref[rows, :] = y
            o_ref[rows, :] = _dot_nt(y, w[...]).astype(BF16)

    @pl.when(jnp.logical_not(first))
    def _():
        o_ref[...] = _dot_nt(xn_ref[...], w[...]).astype(BF16)


def _inproj_latent_kernel(x_ref, shift_ref, scale_ref, wf_ref, o_ref, wbf_ref, xn_ref, wbuf_ref, in_sem, out_sem):
    i = pl.program_id(0)
    k = pl.program_id(1)
    t = i * INPROJ_BLOCKS + k
    slot = t % 2
    n_steps = pl.num_programs(0) * INPROJ_BLOCKS

    def out_copy(block, s):
        return pltpu.make_async_copy(wbuf_ref.at[s], wbf_ref.at[block], out_sem.at[s])

    def in_copy(block, s):
        return pltpu.make_async_copy(wbf_ref.at[block], wbuf_ref.at[s], in_sem.at[s])

    @pl.when(i == 0)
    def _():
        @pl.when(k >= 2)
        def _():
            out_copy(k - 2, slot).wait()

        col = lax.broadcasted_iota(I32, (INPROJ_TN, D_MODEL), 0)
        wbuf_ref[slot] = jnp.where(col < IN_W - k * INPROJ_TN, wf_ref[0], 0.0).astype(BF16)
        out_copy(k, slot).start()

    @pl.when((t + 1 >= INPROJ_BLOCKS) & (t + 1 < n_steps))
    def _():
        @pl.when(t == INPROJ_BLOCKS - 1)
        def _():
            out_copy(INPROJ_BLOCKS - 2, 1 - slot).wait()

        @pl.when(t == INPROJ_BLOCKS)
        def _():
            out_copy(INPROJ_BLOCKS - 1, 1 - slot).wait()

        in_copy((k + 1) % INPROJ_BLOCKS, 1 - slot).start()

    @pl.when(i > 0)
    def _():
        in_copy(k, slot).wait()

    _inproj_products(x_ref, shift_ref, scale_ref, wbuf_ref.at[slot], o_ref, xn_ref)


def _inproj_context_kernel(x_ref, shift_ref, scale_ref, w_ref, o_ref, xn_ref):
    _inproj_products(x_ref, shift_ref, scale_ref, w_ref.at[0], o_ref, xn_ref)


def _inproj_mod_map(j, tiles_per_sample, sample_row0):
    return lambda i, k: ((sample_row0 + i // tiles_per_sample) * 6 + j, 0, 0)


def _inproj_latent(x2d, mod_rows, w_in_t):
    assert INPROJ_BLOCKS % 2 == 1
    last = INPROJ_BLOCKS - 1
    return pl.pallas_call(
        _inproj_latent_kernel,
        grid=(N_TOK // INPROJ_TM, INPROJ_BLOCKS),
        in_specs=[
            pl.BlockSpec((INPROJ_TM, D_MODEL), lambda i, k: (i, 0)),
            pl.BlockSpec((1, 1, D_MODEL), _inproj_mod_map(0, SEQ // INPROJ_TM, 0)),
            pl.BlockSpec((1, 1, D_MODEL), _inproj_mod_map(1, SEQ // INPROJ_TM, 0)),
            pl.BlockSpec((1, INPROJ_TN, D_MODEL), lambda i, k: (0, jnp.where(i == 0, k, last), 0)),
        ],
        out_specs=[pl.BlockSpec((INPROJ_TM, INPROJ_TN), lambda i, k: (i, k)), pl.BlockSpec(memory_space=pl.ANY)],
        out_shape=[jax.ShapeDtypeStruct((N_TOK, IN_W), BF16),
                   jax.ShapeDtypeStruct((INPROJ_BLOCKS, INPROJ_TN, D_MODEL), BF16)],
        scratch_shapes=[pltpu.VMEM((INPROJ_TM, D_MODEL), BF16), pltpu.VMEM((2, INPROJ_TN, D_MODEL), BF16),
                        pltpu.SemaphoreType.DMA((2,)), pltpu.SemaphoreType.DMA((2,))],
        compiler_params=_params(("arbitrary", "arbitrary")),
        name="inproj",
    )(x2d, mod_rows, mod_rows, w_in_t)


def _inproj_context(ctx2d, mod_rows, w_blocks):
    last = INPROJ_BLOCKS - 1
    out_w = 2 * INPROJ_TN + (IN_W - last * INPROJ_TN)
    return pl.pallas_call(
        _inproj_context_kernel,
        grid=(N_CTX // INPROJ_TM, 3),
        in_specs=[
            pl.BlockSpec((INPROJ_TM, D_MODEL), lambda i, k: (i, 0)),
            pl.BlockSpec((1, 1, D_MODEL), _inproj_mod_map(0, N_CTX // INPROJ_TM, BATCH)),
            pl.BlockSpec((1, 1, D_MODEL), _inproj_mod_map(1, N_CTX // INPROJ_TM, BATCH)),
            pl.BlockSpec((1, INPROJ_TN, D_MODEL), lambda i, k: (jnp.where(k == 2, last, k + 1), 0, 0)),
        ],
        out_specs=pl.BlockSpec((INPROJ_TM, INPROJ_TN), lambda i, k: (i, k)),
        out_shape=jax.ShapeDtypeStruct((N_CTX, out_w), BF16),
        scratch_shapes=[pltpu.VMEM((INPROJ_TM, D_MODEL), BF16)],
        compiler_params=_params(("arbitrary", "arbitrary")),
        name="inproj_ctx",
    )(ctx2d, mod_rows, mod_rows, w_blocks)


def _rope_tables(width):
    half = width // 2
    quarter = half // 2
    inv_freq = ROPE_BASE ** (-np.arange(0, half, 2, dtype=np.float64) / half)
    t = np.arange(SEQ)
    cos_parts, sin_parts = [], []
    for pos in (t // GRID_W, t % GRID_W):
        ang = pos[:, None].astype(np.float64) * inv_freq[None, :]
        c, s = np.cos(ang), np.sin(ang)
        cos_parts += [c, c]
        sin_parts += [-s, s]
    assert cos_parts[0].shape[1] == quarter
    return (np.concatenate(cos_parts, 1).astype(np.float32), np.concatenate(sin_parts, 1).astype(np.float32))


def _rope(x, cos, sin, quarter):
    lane = lax.broadcasted_iota(I32, x.shape, 1)
    first = (lane % (2 * quarter)) < quarter
    swapped = jnp.where(first, pltpu.roll(x, LANES - quarter, 1), pltpu.roll(x, quarter, 1))
    return x * cos + swapped * sin


def _dot_tn(a, b):
    return lax.dot_general(a, b, (((0,), (0,)), ((), ())), preferred_element_type=F32)


def _dot_nt(a, b):
    return lax.dot_general(a, b, (((1,), (1,)), ((), ())), preferred_element_type=F32)


RET_HG = 4


def _retention_kernel(q_ref, k_ref, v_ref, g_ref, kc_ref, vc_ref, cos_ref, sin_ref, df_ref, db_ref, gn_ref,
                      o_ref, qs_ref, ks_ref, st_ref):
    for hh in range(RET_HG):
        _retention_head(hh, slice(hh * LANES, (hh + 1) * LANES), q_ref, k_ref, v_ref, g_ref, kc_ref, vc_ref,
                        cos_ref, sin_ref, df_ref, db_ref, gn_ref, o_ref, qs_ref, ks_ref, st_ref)


def _retention_head(hh, hs, q_ref, k_ref, v_ref, g_ref, kc_ref, vc_ref, cos_ref, sin_ref, df_ref, db_ref, gn_ref,
                    o_ref, qs_ref, ks_ref, st_ref):
    k_scale = RET_DK ** -0.5
    lgf = jax.nn.log_sigmoid(df_ref[hh])
    lgb = jax.nn.log_sigmoid(db_ref[hh])

    cos = cos_ref[...]
    sin = sin_ref[...]
    qs_ref[hh] = _rope(q_ref[:, hs].astype(F32), cos, sin, RET_DK // 4).astype(BF16)
    ks_ref[hh] = _rope(k_ref[:, hs].astype(F32), cos, sin, RET_DK // 4) * k_scale

    rowi = lax.broadcasted_iota(I32, (CHUNK, LANES), 0).astype(F32)
    coli = lax.broadcasted_iota(I32, (CHUNK, LANES), 1).astype(F32)
    diff = rowi - coli
    decay = jnp.exp(jnp.where(diff >= 0, lgf * diff, -lgb * diff)) * jnp.where(diff == 0, 2.0, 1.0)
    zeta_f = jnp.exp(lgf * (CHUNK - 1.0 - rowi))
    eta_b = jnp.exp(lgb * rowi)
    xi_f = jnp.exp(lgf * (rowi + 1.0))
    xi_b = jnp.exp(lgb * (CHUNK - rowi))
    cdec_f = jnp.exp(lgf * float(CHUNK))
    cdec_b = jnp.exp(lgb * float(CHUNK))

    crow = lax.broadcasted_iota(I32, (CTX_LEN, LANES), 0).astype(F32)
    kc = kc_ref[:, hs].astype(F32) * k_scale
    vc = vc_ref[:, hs]
    s_f = _dot_tn((kc * jnp.exp(lgf * (CTX_LEN - 1.0 - crow))).astype(BF16), vc)
    s_b = _dot_tn((kc * jnp.exp(lgb * crow)).astype(BF16), vc)

    upd_f, upd_b = [], []
    for i in range(N_CHUNKS):
        rows = pl.ds(i * CHUNK, CHUNK)
        kch = ks_ref[hh, rows, :]
        vch = v_ref[rows, hs]
        upd_f.append(_dot_tn((kch * zeta_f).astype(BF16), vch))
        upd_b.append(_dot_tn((kch * eta_b).astype(BF16), vch))
    state = s_f
    for i in range(N_CHUNKS):
        st_ref[hh, i, :, :RET_DV] = state.astype(BF16)
        state = cdec_f * state + upd_f[i]
    state = s_b
    for i in reversed(range(N_CHUNKS)):
        st_ref[hh, i, :, RET_DV:] = state.astype(BF16)
        state = cdec_b * state + upd_b[i]

    gn_w = gn_ref[:, hs]
    for i in range(N_CHUNKS):
        rows = pl.ds(i * CHUNK, CHUNK)
        qch = qs_ref[hh, rows, :]
        scores = _dot_nt(qch, ks_ref[hh, rows, :].astype(BF16)) * decay
        o = jnp.dot(scores.astype(BF16), v_ref[rows, hs], preferred_element_type=F32)
        cross = jnp.dot(qch, st_ref[hh, i], preferred_element_type=F32)
        o = o + xi_f * cross[:, :RET_DV] + xi_b * cross[:, RET_DV:]
        y = _plain_norm(o) * gn_w
        o_ref[rows, hs] = (_silu(g_ref[rows, hs].astype(F32)) * y).astype(BF16)


def _retention(proj, proj_c, cos, sin, decay_rows, gn_w):
    width = RET_HG * LANES
    groups = RET_HEADS // RET_HG
    blk = lambda off: pl.BlockSpec((SEQ, width), lambda b, hg: (b, off // width + hg))
    blk_c = lambda off: pl.BlockSpec((CTX_LEN, width), lambda b, hg: (b, off // width + hg))
    table = pl.BlockSpec((SEQ, LANES), lambda b, hg: (0, 0))
    return pl.pallas_call(
        _retention_kernel,
        grid=(BATCH, groups),
        in_specs=[
            blk(OFF_Q), blk(OFF_K), blk(OFF_V), blk(OFF_G), blk_c(CTX_OFF_K), blk_c(CTX_OFF_V), table, table,
            pl.BlockSpec((RET_HG, 1, LANES), lambda b, hg: (hg, 0, 0)),
            pl.BlockSpec((RET_HG, 1, LANES), lambda b, hg: (groups + hg, 0, 0)),
            pl.BlockSpec((1, width), lambda b, hg: (0, hg)),
        ],
        out_specs=pl.BlockSpec((SEQ, width), lambda b, hg: (b, hg)),
        out_shape=jax.ShapeDtypeStruct((N_TOK, RET_W), BF16),
        scratch_shapes=[
            pltpu.VMEM((RET_HG, SEQ, RET_DK), BF16),
            pltpu.VMEM((RET_HG, SEQ, RET_DK), F32),
            pltpu.VMEM((RET_HG, N_CHUNKS, RET_DK, 2 * RET_DV), BF16),
        ],
        compiler_params=_params(("arbitrary", "arbitrary")),
        name="retention",
    )(proj, proj, proj, proj, proj_c, proj_c, cos, sin, decay_rows, decay_rows, gn_w)


MLA_TM = 512


def _rms_norm(x, w):
    return x * lax.rsqrt(jnp.mean(x * x, -1, keepdims=True) + EPS) * w


def _mla_kv(ckv_ref, kpe_ref, kvn_ref, wkv_ref, cos_ref, sin_ref, k_ref, v_ref, rotate):
    ckv = _rms_norm(ckv_ref[...].astype(F32), kvn_ref[...]).astype(BF16)
    kv = jnp.dot(ckv, wkv_ref[...], preferred_element_type=F32)
    lane = lax.broadcasted_iota(I32, (ckv.shape[0], LANES), 1)
    kpe = jnp.where(lane < MLA_ROPE, kpe_ref[...].astype(F32), 0.0)
    if rotate:
        kpe = _rope(kpe, cos_ref[...], sin_ref[...], MLA_ROPE // 4)
    kpe = kpe.astype(BF16)
    for h in range(MLA_HEADS):
        k_ref[:, h * QK_PAD:h * QK_PAD + MLA_NOPE] = kv[:, 2 * h * LANES:(2 * h + 1) * LANES].astype(BF16)
        k_ref[:, h * QK_PAD + MLA_NOPE:(h + 1) * QK_PAD] = kpe
        v_ref[:, h * MLA_DV:(h + 1) * MLA_DV] = kv[:, (2 * h + 1) * LANES:(2 * h + 2) * LANES].astype(BF16)


def _mla_latent_kernel(cq_ref, ckv_ref, kpe_ref, qn_ref, kvn_ref, wq_ref, wkv_ref, cos_ref, sin_ref,
                       q_ref, k_ref, v_ref):
    cq = _rms_norm(cq_ref[...].astype(F32), qn_ref[...]).astype(BF16)
    q = jnp.dot(cq, wq_ref[...], preferred_element_type=F32)
    cos = cos_ref[...]
    sin = sin_ref[...]
    scale = MLA_DQ ** -0.5 * LOG2_E
    for h in range(MLA_HEADS):
        lo = h * QK_PAD
        q_ref[:, lo:lo + MLA_NOPE] = (q[:, lo:lo + MLA_NOPE] * scale).astype(BF16)
        qpe = _rope(q[:, lo + MLA_NOPE:lo + QK_PAD], cos, sin, MLA_ROPE // 4)
        q_ref[:, lo + MLA_NOPE:lo + QK_PAD] = (qpe * scale).astype(BF16)
    _mla_kv(ckv_ref, kpe_ref, kvn_ref, wkv_ref, cos_ref, sin_ref, k_ref, v_ref, rotate=True)


def _mla_context_kernel(ckv_ref, kpe_ref, kvn_ref, wkv_ref, k_ref, v_ref):
    _mla_kv(ckv_ref, kpe_ref, kvn_ref, wkv_ref, None, None, k_ref, v_ref, rotate=False)


def _mla_latent(proj, q_norm, kv_norm, wq_pad, wkv, cos, sin):
    row = lambda w, off: pl.BlockSpec((MLA_TM, w), lambda i: (i, off // w))
    full = lambda a: pl.BlockSpec(a.shape, lambda i: (0, 0))
    table = pl.BlockSpec((MLA_TM, LANES), lambda i: (i % (SEQ // MLA_TM), 0))
    return pl.pallas_call(
        _mla_latent_kernel,
        grid=(N_TOK // MLA_TM,),
        in_specs=[row(MLA_Q_LORA, OFF_CQ), row(MLA_KV_LORA, OFF_CKV), row(LANES, OFF_KPE),
                  full(q_norm), full(kv_norm), full(wq_pad), full(wkv), table, table],
        out_specs=[
            pl.BlockSpec((MLA_TM, MLA_HEADS * QK_PAD), lambda i: (i, 0)),
            pl.BlockSpec((MLA_TM, MLA_HEADS * QK_PAD), lambda i: (i, 0)),
            pl.BlockSpec((MLA_TM, MLA_W), lambda i: (i, 0)),
        ],
        out_shape=[
            jax.ShapeDtypeStruct((N_TOK, MLA_HEADS * QK_PAD), BF16),
            jax.ShapeDtypeStruct((N_TOK, MLA_HEADS * QK_PAD), BF16),
            jax.ShapeDtypeStruct((N_TOK, MLA_W), BF16),
        ],
        compiler_params=_params(("arbitrary",)),
        name="mla_latent",
    )(proj, proj, proj, q_norm, kv_norm, wq_pad, wkv, cos, sin)


def _mla_context(proj_c, kv_norm, wkv):
    row = lambda w, off: pl.BlockSpec((MLA_TM, w), lambda i: (i, off // w))
    full = lambda a: pl.BlockSpec(a.shape, lambda i: (0, 0))
    return pl.pallas_call(
        _mla_context_kernel,
        grid=(N_CTX // MLA_TM,),
        in_specs=[row(MLA_KV_LORA, CTX_OFF_CKV), row(LANES, CTX_OFF_KPE), full(kv_norm), full(wkv)],
        out_specs=[
            pl.BlockSpec((MLA_TM, MLA_HEADS * QK_PAD), lambda i: (i, 0)),
            pl.BlockSpec((MLA_TM, MLA_W), lambda i: (i, 0)),
        ],
        out_shape=[
            jax.ShapeDtypeStruct((N_CTX, MLA_HEADS * QK_PAD), BF16),
            jax.ShapeDtypeStruct((N_CTX, MLA_W), BF16),
        ],
        compiler_params=_params(("arbitrary",)),
        name="mla_context",
    )(proj_c, proj_c, kv_norm, wkv)


ATT_TQ = 256


def _attention_kernel(q_ref, kc_ref, kl_ref, vc_ref, vl_ref, wo_ref, o_ref, wo_bf_ref, vext_ref):
    wo_bf_ref[...] = wo_ref[...].astype(BF16)

    @pl.when(pl.program_id(1) == 0)
    def _():
        lane = lax.broadcasted_iota(I32, (CTX_LEN + SEQ, MLA_DV), 1)
        ones_col = jnp.where(lane == 0, 1.0, 0.0).astype(BF16)
        for h in range(MLA_HEADS):
            dv = slice(h * MLA_DV, (h + 1) * MLA_DV)
            vext_ref[h, :CTX_LEN, :MLA_DV] = vc_ref[:, dv]
            vext_ref[h, CTX_LEN:, :MLA_DV] = vl_ref[:, dv]
            vext_ref[h, :, MLA_DV:] = ones_col

    for h in range(MLA_HEADS):
        qk = slice(h * QK_PAD, (h + 1) * QK_PAD)
        q = q_ref[:, qk]
        s_c = _dot_nt(q, kc_ref[:, qk])
        s_l = _dot_nt(q, kl_ref[:, qk])
        m = jnp.maximum(jnp.max(s_c, -1, keepdims=True), jnp.max(s_l, -1, keepdims=True))
        p_c = jnp.exp2(s_c - m).astype(BF16)
        p_l = jnp.exp2(s_l - m).astype(BF16)
        o = (jnp.dot(p_c, vext_ref[h, :CTX_LEN, :], preferred_element_type=F32)
             + jnp.dot(p_l, vext_ref[h, CTX_LEN:, :], preferred_element_type=F32))
        o_ref[:, h * MLA_DV:(h + 1) * MLA_DV] = (o[:, :MLA_DV] / o[:, MLA_DV:MLA_DV + 1]).astype(BF16)


def _attention(q, k_ctx, k_lat, v_ctx, v_lat, w_o):
    tiles = SEQ // ATT_TQ
    band = w_o.shape[0] // (BATCH * tiles)
    step = lambda b, i: (b * tiles + i, 0)
    return pl.pallas_call(
        _attention_kernel,
        grid=(BATCH, tiles),
        in_specs=[
            pl.BlockSpec((ATT_TQ, MLA_HEADS * QK_PAD), step),
            pl.BlockSpec((CTX_LEN, MLA_HEADS * QK_PAD), lambda b, i: (b, 0)),
            pl.BlockSpec((SEQ, MLA_HEADS * QK_PAD), lambda b, i: (b, 0)),
            pl.BlockSpec((CTX_LEN, MLA_W), lambda b, i: (b, 0)),
            pl.BlockSpec((SEQ, MLA_W), lambda b, i: (b, 0)),
            pl.BlockSpec((band, D_MODEL), step),
        ],
        out_specs=[pl.BlockSpec((ATT_TQ, MLA_W), step), pl.BlockSpec((band, D_MODEL), step)],
        out_shape=[jax.ShapeDtypeStruct((N_TOK, MLA_W), BF16), jax.ShapeDtypeStruct(w_o.shape, BF16)],
        scratch_shapes=[pltpu.VMEM((MLA_HEADS, CTX_LEN + SEQ, 2 * MLA_DV), BF16)],
        compiler_params=_params(("arbitrary", "arbitrary")),
        name="attention",
    )(q, k_ctx, k_lat, v_ctx, v_lat, w_o)


OUT_TM = 512
OUT_SUB = 256
ROUTE_E1, ROUTE_E2, ROUTE_R1, ROUTE_R2, ROUTE_W1, ROUTE_W2 = range(6)


def _outproj_kernel(ret_ref, att_ref, x_ref, wo_ref, gate_ref, shift_ref, scale_ref, lnw_ref, lnb_ref,
                    wr_ref, br_ref, h_ref, t_ref, route_ref, route_t_ref, count_ref, carry_ref, wsplit_ref):
    @pl.when(pl.program_id(0) == 0)
    def _():
        carry_ref[...] = jnp.zeros_like(carry_ref)
        w = wr_ref[...]
        w_hi = w.astype(BF16)
        wsplit_ref[:, :LANES] = w_hi
        wsplit_ref[:, LANES:] = (w - w_hi.astype(F32)).astype(BF16)

    subtiles = [slice(s * OUT_SUB, (s + 1) * OUT_SUB) for s in range(OUT_TM // OUT_SUB)]
    mixes = [jnp.dot(ret_ref[rows, :], wo_ref[:RET_W, :], preferred_element_type=F32)
             + jnp.dot(att_ref[rows, :], wo_ref[RET_W:, :], preferred_element_type=F32) for rows in subtiles]
    logits = [_outproj_norms(rows, mix, x_ref, gate_ref, shift_ref, scale_ref, lnw_ref, lnb_ref, br_ref, h_ref, t_ref,
                             wsplit_ref) for rows, mix in zip(subtiles, mixes)]
    _route(jnp.concatenate(logits, 0), carry_ref, count_ref, route_ref, route_t_ref)


def _outproj_norms(rows, mix, x_ref, gate_ref, shift_ref, scale_ref, lnw_ref, lnb_ref, br_ref, h_ref, t_ref,
                   wsplit_ref):
    h = _plain_norm(DEEPNORM_ALPHA * x_ref[rows, :] + gate_ref[0] * mix) * lnw_ref[...] + lnb_ref[...]
    h_ref[rows, :] = h
    t = _plain_norm(h) * (1.0 + scale_ref[0]) + shift_ref[0]
    t_ref[rows] = _to_token_tiles(t)

    t_hi = t.astype(BF16)
    t_lo = (t - t_hi.astype(F32)).astype(BF16)
    main = jnp.dot(t_hi, wsplit_ref[...], preferred_element_type=F32)
    corr = jnp.dot(t_lo, wsplit_ref[:, :LANES], preferred_element_type=F32)
    return main[:, :LANES] + (main[:, LANES:] + corr) + br_ref[...]


def _route(logits, carry_ref, count_ref, route_ref, route_t_ref):
    lane = lax.broadcasted_iota(I32, logits.shape, 1).astype(F32)
    neg = -jnp.inf
    big = float(LANES)

    def first_lane_of(mask):
        return jnp.min(jnp.where(mask, lane, big), -1, keepdims=True)

    is_group = lane < N_GROUPS
    gl = jnp.where(is_group, logits, neg)
    g_max = jnp.max(gl, -1, keepdims=True)
    g_idx = first_lane_of(is_group & (gl == g_max))
    g_prob = 1.0 / jnp.sum(jnp.where(is_group, jnp.exp(logits - g_max), 0.0), -1, keepdims=True)

    lo = N_GROUPS + g_idx * EXPERTS_PER_GROUP
    in_group = (lane >= lo) & (lane < lo + EXPERTS_PER_GROUP)
    el = jnp.where(in_group, logits, neg)
    v1 = jnp.max(el, -1, keepdims=True)
    i1 = first_lane_of(in_group & (el == v1))
    rest = in_group & (lane != i1)
    el2 = jnp.where(rest, logits, neg)
    v2 = jnp.max(el2, -1, keepdims=True)
    i2 = first_lane_of(rest & (el2 == v2))
    d = jnp.exp(v2 - v1)
    w1 = g_prob / (1.0 + d)
    w2 = g_prob * d / (1.0 + d)

    onehot = jnp.where((lane == i1) | (lane == i2), 1.0, 0.0)
    r = lax.broadcasted_iota(I32, (OUT_TM, OUT_TM), 0)
    c = lax.broadcasted_iota(I32, (OUT_TM, OUT_TM), 1)
    tri = jnp.where(c < r, 1.0, 0.0).astype(BF16)
    carry = carry_ref[0:1, :]
    before = jnp.dot(tri, onehot.astype(BF16), preferred_element_type=F32) + carry
    r1 = jnp.sum(jnp.where(lane == i1, before, 0.0), -1, keepdims=True)
    r2 = jnp.sum(jnp.where(lane == i2, before, 0.0), -1, keepdims=True)
    carry = carry + jnp.sum(onehot, 0, keepdims=True)
    carry_ref[...] = jnp.broadcast_to(carry, carry_ref.shape)
    count_ref[...] = jnp.broadcast_to(carry, count_ref.shape)

    rec = jnp.zeros_like(logits)
    for slot, val in ((ROUTE_E1, i1 - N_GROUPS), (ROUTE_E2, i2 - N_GROUPS), (ROUTE_R1, r1), (ROUTE_R2, r2),
                      (ROUTE_W1, w1), (ROUTE_W2, w2)):
        rec = jnp.where(lane == slot, val, rec)
    route_ref[...] = rec
    route_t_ref[...] = rec.T[:8, :]


def _outproj(ret, att, x2d, wo_bf, mod_rows, ln_w, ln_b, w_route, b_route):
    tiles_per_sample = SEQ // OUT_TM
    row = lambda w: pl.BlockSpec((OUT_TM, w), lambda i: (i, 0))
    full = lambda a: pl.BlockSpec(a.shape, lambda i: (0, 0))
    mod = lambda j: pl.BlockSpec((1, 1, D_MODEL), lambda i: ((i // tiles_per_sample) * 6 + j, 0, 0))
    return pl.pallas_call(
        _outproj_kernel,
        grid=(N_TOK // OUT_TM,),
        in_specs=[row(RET_W), row(MLA_W), row(D_MODEL), full(wo_bf), mod(2), mod(3), mod(4),
                  full(ln_w), full(ln_b), full(w_route), full(b_route)],
        out_specs=[row(D_MODEL), pl.BlockSpec((OUT_TM, TOKEN_SUB, LANES), lambda i: (i, 0, 0)), row(LANES),
                   pl.BlockSpec((8, OUT_TM), lambda i: (0, i)), pl.BlockSpec((8, LANES), lambda i: (0, 0))],
        out_shape=[
            jax.ShapeDtypeStruct((N_TOK, D_MODEL), F32),
            jax.ShapeDtypeStruct((N_TOK, TOKEN_SUB, LANES), BF16),
            jax.ShapeDtypeStruct((N_TOK, LANES), F32),
            jax.ShapeDtypeStruct((8, N_TOK), F32),
            jax.ShapeDtypeStruct((8, LANES), F32),
        ],
        scratch_shapes=[pltpu.VMEM((8, LANES), F32), pltpu.VMEM((D_MODEL, 2 * LANES), BF16)],
        compiler_params=_params(("arbitrary",)),
        name="outproj_route",
    )(ret, att, x2d, wo_bf, mod_rows, mod_rows, mod_rows, ln_w, ln_b, w_route, b_route)


DISPATCH_TM = 256


def _dispatch_kernel(pos_ref, t_ref, xs_ref, sem):
    i = pl.program_id(0)

    def copy(slot, j):
        return pltpu.make_async_copy(t_ref.at[j], xs_ref.at[pos_ref[slot * N_TOK + i * DISPATCH_TM + j]], sem)

    for slot in range(2):
        lax.fori_loop(0, DISPATCH_TM, lambda j, c, slot=slot: (copy(slot, j).start(), c)[1], 0, unroll=8)
    for slot in range(2):
        pltpu.make_async_copy(t_ref, xs_ref.at[pl.ds(0, DISPATCH_TM)], sem).wait()


def _dispatch(pos, t_tiles):
    return pl.pallas_call(
        _dispatch_kernel,
        grid_spec=pltpu.PrefetchScalarGridSpec(
            num_scalar_prefetch=1,
            grid=(N_TOK // DISPATCH_TM,),
            in_specs=[pl.BlockSpec((DISPATCH_TM, TOKEN_SUB, LANES), lambda i, pos: (i, 0, 0))],
            out_specs=pl.BlockSpec(memory_space=pl.ANY),
            scratch_shapes=[pltpu.SemaphoreType.DMA(())],
        ),
        out_shape=jax.ShapeDtypeStruct((N_PAIRS, TOKEN_SUB, LANES), BF16),
        compiler_params=_params(("arbitrary",)),
        name="dispatch",
    )(pos, t_tiles)


def _experts_kernel(rend_ref, xs_ref, wg_ref, wu_ref, wd_ref, ys_ref, wgu_ref, wdn_ref, xbuf_ref, ybuf_ref,
                    xsem, ysem):
    e = pl.program_id(0)
    lo = jnp.where(e == 0, 0, rend_ref[jnp.maximum(e - 1, 0)])
    hi = rend_ref[e]

    def x_copy(g):
        return pltpu.make_async_copy(xs_ref.at[pl.ds(g * ROW_TILE, ROW_TILE)], xbuf_ref.at[g % 2], xsem.at[g % 2])

    def y_copy(g):
        return pltpu.make_async_copy(ybuf_ref.at[g % 2], ys_ref.at[pl.ds(g * ROW_TILE, ROW_TILE)], ysem.at[g % 2])

    @pl.when(e == 0)
    def _():
        x_copy(0).start()

    @pl.when(hi > lo)
    def _():
        wgu_ref[:, :D_EXPERT] = wg_ref[0].astype(BF16)
        wgu_ref[:, D_EXPERT:] = wu_ref[0].astype(BF16)
        wdn_ref[...] = wd_ref[0].astype(BF16)

        def tile(g, c):
            row0 = g * ROW_TILE
            owns_first_row = lo <= row0
            owns_last_row = hi >= row0 + ROW_TILE

            @pl.when(owns_first_row)
            def _():
                @pl.when(g + 1 < N_TILES)
                def _():
                    x_copy(g + 1).start()

                x_copy(g).wait()

            gu = jnp.dot(_from_token_tiles(xbuf_ref[g % 2]), wgu_ref[...], preferred_element_type=F32)
            hid = _silu(gu[:, :D_EXPERT]) * gu[:, D_EXPERT:]
            y = jnp.dot(hid.astype(BF16), wdn_ref[...], preferred_element_type=F32)

            @pl.when(owns_first_row)
            def _():
                @pl.when(g >= 2)
                def _():
                    y_copy(g - 2).wait()

                ybuf_ref[g % 2] = _to_token_tiles(y)

            @pl.when(jnp.logical_not(owns_first_row))
            def _():
                row = lax.broadcasted_iota(I32, y.shape, 0)
                earlier = _from_token_tiles(ybuf_ref[g % 2]).astype(F32)
                ybuf_ref[g % 2] = _to_token_tiles(jnp.where(row >= lo - row0, y, earlier))

            @pl.when(owns_last_row)
            def _():
                y_copy(g).start()

            return c

        lax.fori_loop(lo // ROW_TILE, (hi - 1) // ROW_TILE + 1, tile, 0)

    @pl.when(e == N_EXPERTS - 1)
    def _():
        y_copy(N_TILES - 2).wait()
        y_copy(N_TILES - 1).wait()


def _experts(row_end, xs, w_gate, w_up, w_down):
    w_gate = w_gate.reshape(N_EXPERTS, D_MODEL, D_EXPERT)
    w_up = w_up.reshape(N_EXPERTS, D_MODEL, D_EXPERT)
    w_down = w_down.reshape(N_EXPERTS, D_EXPERT, D_MODEL)
    expert = lambda e, rend: (e, 0, 0)
    tile_buf = pltpu.VMEM((2, ROW_TILE, TOKEN_SUB, LANES), BF16)
    return pl.pallas_call(
        _experts_kernel,
        grid_spec=pltpu.PrefetchScalarGridSpec(
            num_scalar_prefetch=1,
            grid=(N_EXPERTS,),
            in_specs=[
                pl.BlockSpec(memory_space=pl.ANY),
                pl.BlockSpec((1, D_MODEL, D_EXPERT), expert),
                pl.BlockSpec((1, D_MODEL, D_EXPERT), expert),
                pl.BlockSpec((1, D_EXPERT, D_MODEL), expert),
            ],
            out_specs=pl.BlockSpec(memory_space=pl.ANY),
            scratch_shapes=[pltpu.VMEM((D_MODEL, 2 * D_EXPERT), BF16), pltpu.VMEM((D_EXPERT, D_MODEL), BF16),
                            tile_buf, tile_buf, pltpu.SemaphoreType.DMA((2,)), pltpu.SemaphoreType.DMA((2,))],
        ),
        out_shape=jax.ShapeDtypeStruct((N_PAIRS, TOKEN_SUB, LANES), BF16),
        compiler_params=_params(("arbitrary",)),
        name="experts",
    )(row_end, xs, w_gate, w_up, w_down)


COMB_TM = 256


def _combine_kernel(pos_ref, ys_ref, h_ref, route_ref, gate_ref, lnw_ref, lnb_ref, o_ref, buf_ref, sem):
    i = pl.program_id(0)
    n = pl.num_programs(0)

    def copy(step, slot, pair, j):
        src = pos_ref[pair * N_TOK + step * COMB_TM + j]
        return pltpu.make_async_copy(ys_ref.at[src], buf_ref.at[slot, pair * COMB_TM + j], sem.at[slot])

    def start_all(step, slot):
        for pair in range(2):
            lax.fori_loop(0, COMB_TM, lambda j, c, pair=pair: (copy(step, slot, pair, j).start(), c)[1], 0,
                          unroll=8)

    def wait_all(step, slot):
        pltpu.make_async_copy(ys_ref.at[pl.ds(0, 2 * COMB_TM)], buf_ref.at[slot], sem.at[slot]).wait()

    @pl.when(i == 0)
    def _():
        start_all(0, 0)

    @pl.when(i + 1 < n)
    def _():
        start_all(i + 1, (i + 1) % 2)

    slot = i % 2
    wait_all(i, slot)

    y1 = _from_token_tiles(buf_ref[slot, :COMB_TM]).astype(F32)
    y2 = _from_token_tiles(buf_ref[slot, COMB_TM:]).astype(F32)
    y = route_ref[:, ROUTE_W1:ROUTE_W1 + 1] * y1 + route_ref[:, ROUTE_W2:ROUTE_W2 + 1] * y2
    z = DEEPNORM_ALPHA * h_ref[...] + gate_ref[0] * y
    o_ref[...] = _plain_norm(z) * lnw_ref[...] + lnb_ref[...]


def _combine(pos, ys, h, route, mod_rows, ln_w, ln_b):
    tiles_per_sample = SEQ // COMB_TM
    row = lambda w: pl.BlockSpec((COMB_TM, w), lambda i, pos: (i, 0))
    full = lambda a: pl.BlockSpec(a.shape, lambda i, pos: (0, 0))
    return pl.pallas_call(
        _combine_kernel,
        grid_spec=pltpu.PrefetchScalarGridSpec(
            num_scalar_prefetch=1,
            grid=(N_TOK // COMB_TM,),
            in_specs=[
                pl.BlockSpec(memory_space=pl.ANY), row(D_MODEL), row(LANES),
                pl.BlockSpec((1, 1, D_MODEL), lambda i, pos: ((i // tiles_per_sample) * 6 + 5, 0, 0)),
                full(ln_w), full(ln_b),
            ],
            out_specs=row(D_MODEL),
            scratch_shapes=[pltpu.VMEM((2, 2 * COMB_TM, TOKEN_SUB, LANES), BF16), pltpu.SemaphoreType.DMA((2,))],
        ),
        out_shape=jax.ShapeDtypeStruct((N_TOK, D_MODEL), F32),
        compiler_params=_params(("arbitrary",)),
        name="combine",
    )(pos, ys, h, route, mod_rows, ln_w, ln_b)


def _routing_tables(route_t, counts):
    cnt = counts[0, N_GROUPS:N_ROUTE].astype(I32)
    row_end = jnp.cumsum(cnt)
    e = route_t[ROUTE_E1:ROUTE_E2 + 1].astype(I32)
    rank = route_t[ROUTE_R1:ROUTE_R2 + 1].astype(I32)
    earlier = jnp.arange(N_EXPERTS, dtype=I32)[:, None, None] < e[None]
    base = jnp.sum(jnp.where(earlier, cnt[:, None, None], 0), 0)
    pos = (base + rank).reshape(-1)
    return pos, row_end


def kernel(x, c, ctx, c_ctx, w_ada, b_ada, w_in, ret_decay, ret_gn_w, mla_q_norm, mla_kv_norm, w_uq, w_ukv, w_o,
           ln1_w, ln1_b, router_group_w, router_group_b, router_expert_w, router_expert_b, expert_w_gate,
           expert_w_up, expert_w_down, ln2_w, ln2_b):
    x2d = x.reshape(N_TOK, D_MODEL)
    ctx2d = ctx.reshape(N_CTX, D_MODEL)

    cc = jnp.zeros((8, D_MODEL), F32).at[:BATCH].set(c).at[BATCH].set(c_ctx)
    mod = _ada(cc, w_ada[0], b_ada)
    mod_rows = mod.reshape(8 * 6, 1, D_MODEL)

    proj, w_in_blocks = _inproj_latent(x2d, mod_rows, jnp.swapaxes(w_in, 1, 2))
    proj_c = _inproj_context(ctx2d, mod_rows, w_in_blocks)

    cos_r, sin_r = _rope_tables(RET_DK)
    decay_rows = jnp.broadcast_to(ret_decay[0].reshape(2 * RET_HEADS, 1, 1), (2 * RET_HEADS, 1, LANES))
    ret = _retention(proj, proj_c, jnp.asarray(cos_r), jnp.asarray(sin_r), decay_rows, ret_gn_w)

    cos_m, sin_m = _rope_tables(MLA_ROPE)
    cos_m = np.concatenate([cos_m, np.ones_like(cos_m)], 1)
    sin_m = np.concatenate([sin_m, np.zeros_like(sin_m)], 1)
    wq = w_uq[0].reshape(MLA_Q_LORA, MLA_HEADS, MLA_DQ)
    wq_pad = jnp.pad(wq, ((0, 0), (0, 0), (0, QK_PAD - MLA_DQ))).reshape(MLA_Q_LORA, MLA_HEADS * QK_PAD).astype(BF16)
    wkv = w_ukv[0].astype(BF16)
    q, k_lat, v_lat = _mla_latent(proj, mla_q_norm, mla_kv_norm, wq_pad, wkv, jnp.asarray(cos_m), jnp.asarray(sin_m))
    k_ctx, v_ctx = _mla_context(proj_c, mla_kv_norm, wkv)
    att, w_o_bf = _attention(q, k_ctx, k_lat, v_ctx, v_lat, w_o[0])

    w_route = jnp.concatenate(
        [router_group_w[0], router_expert_w[0].transpose(1, 0, 2).reshape(D_MODEL, N_EXPERTS),
         jnp.zeros((D_MODEL, LANES - N_ROUTE), F32)], 1)
    b_route = jnp.concatenate(
        [router_group_b[0], router_expert_b[0].reshape(N_EXPERTS), jnp.zeros((LANES - N_ROUTE,), F32)])[None]
    h, t_tiles, route, route_t, counts = _outproj(ret, att, x2d, w_o_bf, mod_rows, ln1_w, ln1_b,
                                                   w_route, b_route)

    pos, row_end = _routing_tables(route_t, counts)
    xs = _dispatch(pos, t_tiles)
    ys = _experts(row_end, xs, expert_w_gate[0], expert_w_up[0], expert_w_down[0])
    out = _combine(pos, ys, h, route, mod_rows, ln2_w, ln2_b)
    return out.reshape(BATCH, SEQ, D_MODEL)
```

```python
import numpy as np
import jax
import jax.numpy as jnp
from jax import lax
from jax.experimental import pallas as pl
from jax.experimental.pallas import tpu as pltpu

F32 = jnp.float32
BF16 = jnp.bfloat16
I32 = jnp.int32

D_MODEL = 2048
BATCH = 4
SEQ = 2048
GRID_W = 64
CTX_LEN = 256
N_TOK = BATCH * SEQ
N_CTX = BATCH * CTX_LEN

RET_HEADS = 8
RET_DK = 128
RET_DV = 128
RET_W = RET_HEADS * RET_DV
CHUNK = 128
N_CHUNKS = SEQ // CHUNK

MLA_HEADS = 8
MLA_Q_LORA = 512
MLA_KV_LORA = 256
MLA_NOPE = 128
MLA_ROPE = 64
MLA_DV = 128
MLA_W = MLA_HEADS * MLA_DV
MLA_DQ = MLA_NOPE + MLA_ROPE
QK_PAD = 256

IN_SIZES = (RET_HEADS * RET_DK, RET_HEADS * RET_DK, RET_W, RET_W, MLA_Q_LORA, MLA_KV_LORA, MLA_ROPE)
IN_W = sum(IN_SIZES)
OFF_Q, OFF_K, OFF_V, OFF_G, OFF_CQ, OFF_CKV, OFF_KPE = (int(v) for v in np.cumsum((0,) + IN_SIZES[:-1]))

N_GROUPS = 4
EXPERTS_PER_GROUP = 8
N_EXPERTS = N_GROUPS * EXPERTS_PER_GROUP
D_EXPERT = 512
N_ROUTE = N_GROUPS + N_EXPERTS

LOG2_E = float(np.log2(np.e))
ROPE_BASE = 10000.0
EPS = 1e-6
DEPTH = 1
DEEPNORM_ALPHA = (2.0 * DEPTH) ** 0.25

LANES = 128
ROW_TILE = 256
N_PAIRS = 2 * N_TOK
N_TILES = N_PAIRS // ROW_TILE
TOKEN_SUB = D_MODEL // LANES

VMEM_LIMIT = 56 * 1024 * 1024


def _params(sem, vmem=VMEM_LIMIT):
    return pltpu.CompilerParams(dimension_semantics=sem, vmem_limit_bytes=vmem)


def _silu(x):
    return x * (1.0 / (1.0 + jnp.exp(-x)))


def _plain_norm(x):
    mu = jnp.mean(x, -1, keepdims=True)
    xc = x - mu
    var = jnp.mean(xc * xc, -1, keepdims=True)
    return xc * lax.rsqrt(var + EPS)


def _to_token_tiles(x):
    return x.astype(BF16).reshape(x.shape[0], TOKEN_SUB, LANES)


def _from_token_tiles(x):
    return x.reshape(x.shape[0], D_MODEL)


ADA_TN = 1024


ADA_BANDS = 4


def _ada_kernel(cc_ref, *refs):
    w_refs, b_ref, o_ref = refs[:ADA_BANDS], refs[ADA_BANDS], refs[ADA_BANDS + 1]
    s = _silu(cc_ref[...])
    band = D_MODEL // ADA_BANDS
    acc = b_ref[...]
    for q, w_ref in enumerate(w_refs):
        acc = acc + jnp.dot(s[:, q * band:(q + 1) * band], w_ref[...], preferred_element_type=F32,
                            precision=lax.Precision.HIGHEST)
    o_ref[...] = acc


def _ada(cc, w_ada, b_ada):
    n = w_ada.shape[1]
    band = D_MODEL // ADA_BANDS
    return pl.pallas_call(
        _ada_kernel,
        grid=(n // ADA_TN,),
        in_specs=[pl.BlockSpec((8, D_MODEL), lambda j: (0, 0))]
        + [pl.BlockSpec((band, ADA_TN), lambda j, q=q: (q, j)) for q in range(ADA_BANDS)]
        + [pl.BlockSpec((1, ADA_TN), lambda j: (0, j))],
        out_specs=pl.BlockSpec((8, ADA_TN), lambda j: (0, j)),
        out_shape=jax.ShapeDtypeStruct((8, n), F32),
        compiler_params=_params(("arbitrary",)),
        name="ada",
    )(cc, *([w_ada] * ADA_BANDS), b_ada)


INPROJ_TM = 1024
INPROJ_TN = 1024
assert OFF_K == INPROJ_TN and OFF_V == 2 * INPROJ_TN and OFF_CQ == 4 * INPROJ_TN
CTX_OFF_K, CTX_OFF_V = 0, INPROJ_TN
CTX_OFF_CKV = 2 * INPROJ_TN + (OFF_CKV - OFF_CQ)
CTX_OFF_KPE = 2 * INPROJ_TN + (OFF_KPE - OFF_CQ)


INPROJ_SUB = 256
INPROJ_BLOCKS = -(-IN_W // INPROJ_TN)


def _inproj_products(x_ref, shift_ref, scale_ref, w, o_ref, xn_ref):
    first = pl.program_id(1) == 0

    @pl.when(first)
    def _():
        for s in range(INPROJ_TM // INPROJ_SUB):
            rows = slice(s * INPROJ_SUB, (s + 1) * INPROJ_SUB)
            y = (_plain_norm(x_ref[rows, :]) * (1.0 + scale_ref[0]) + shift_ref[0]).astype(BF16)
            xn_ref[rows, :] = y
            o_ref[rows, :] = _dot_nt(y, w[...]).astype(BF16)

    @pl.when(jnp.logical_not(first))
    def _():
        o_ref[...] = _dot_nt(xn_ref[...], w[...]).astype(BF16)


def _inproj_latent_kernel(x_ref, shift_ref, scale_ref, wf_ref, o_ref, wbf_ref, xn_ref, wbuf_ref, in_sem, out_sem):
    i = pl.program_id(0)
    k = pl.program_id(1)
    t = i * INPROJ_BLOCKS + k
    slot = t % 2
    n_steps = pl.num_programs(0) * INPROJ_BLOCKS

    def out_copy(block, s):
        return pltpu.make_async_copy(wbuf_ref.at[s], wbf_ref.at[block], out_sem.at[s])

    def in_copy(block, s):
        return pltpu.make_async_copy(wbf_ref.at[block], wbuf_ref.at[s], in_sem.at[s])

    @pl.when(i == 0)
    def _():
        @pl.when(k >= 2)
        def _():
            out_copy(k - 2, slot).wait()

        col = lax.broadcasted_iota(I32, (INPROJ_TN, D_MODEL), 0)
        wbuf_ref[slot] = jnp.where(col < IN_W - k * INPROJ_TN, wf_ref[0], 0.0).astype(BF16)
        out_copy(k, slot).start()

    @pl.when((t + 1 >= INPROJ_BLOCKS) & (t + 1 < n_steps))
    def _():
        @pl.when(t == INPROJ_BLOCKS - 1)
        def _():
            out_copy(INPROJ_BLOCKS - 2, 1 - slot).wait()

        @pl.when(t == INPROJ_BLOCKS)
        def _():
            out_copy(INPROJ_BLOCKS - 1, 1 - slot).wait()

        in_copy((k + 1) % INPROJ_BLOCKS, 1 - slot).start()

    @pl.when(i > 0)
    def _():
        in_copy(k, slot).wait()

    _inproj_products(x_ref, shift_ref, scale_ref, wbuf_ref.at[slot], o_ref, xn_ref)


def _inproj_context_kernel(x_ref, shift_ref, scale_ref, w_ref, o_ref, xn_ref):
    _inproj_products(x_ref, shift_ref, scale_ref, w_ref.at[0], o_ref, xn_ref)


def _inproj_mod_map(j, tiles_per_sample, sample_row0):
    return lambda i, k: ((sample_row0 + i // tiles_per_sample) * 6 + j, 0, 0)


def _inproj_latent(x2d, mod_rows, w_in_t):
    assert INPROJ_BLOCKS % 2 == 1
    last = INPROJ_BLOCKS - 1
    return pl.pallas_call(
        _inproj_latent_kernel,
        grid=(N_TOK // INPROJ_TM, INPROJ_BLOCKS),
        in_specs=[
            pl.BlockSpec((INPROJ_TM, D_MODEL), lambda i, k: (i, 0)),
            pl.BlockSpec((1, 1, D_MODEL), _inproj_mod_map(0, SEQ // INPROJ_TM, 0)),
            pl.BlockSpec((1, 1, D_MODEL), _inproj_mod_map(1, SEQ // INPROJ_TM, 0)),
            pl.BlockSpec((1, INPROJ_TN, D_MODEL), lambda i, k: (0, jnp.where(i == 0, k, last), 0)),
        ],
        out_specs=[pl.BlockSpec((INPROJ_TM, INPROJ_TN), lambda i, k: (i, k)), pl.BlockSpec(memory_space=pl.ANY)],
        out_shape=[jax.ShapeDtypeStruct((N_TOK, IN_W), BF16),
                   jax.ShapeDtypeStruct((INPROJ_BLOCKS, INPROJ_TN, D_MODEL), BF16)],
        scratch_shapes=[pltpu.VMEM((INPROJ_TM, D_MODEL), BF16), pltpu.VMEM((2, INPROJ_TN, D_MODEL), BF16),
                        pltpu.SemaphoreType.DMA((2,)), pltpu.SemaphoreType.DMA((2,))],
        compiler_params=_params(("arbitrary", "arbitrary")),
        name="inproj",
    )(x2d, mod_rows, mod_rows, w_in_t)


def _inproj_context(ctx2d, mod_rows, w_blocks):
    last = INPROJ_BLOCKS - 1
    out_w = 2 * INPROJ_TN + (IN_W - last * INPROJ_TN)
    return pl.pallas_call(
        _inproj_context_kernel,
        grid=(N_CTX // INPROJ_TM, 3),
        in_specs=[
            pl.BlockSpec((INPROJ_TM, D_MODEL), lambda i, k: (i, 0)),
            pl.BlockSpec((1, 1, D_MODEL), _inproj_mod_map(0, N_CTX // INPROJ_TM, BATCH)),
            pl.BlockSpec((1, 1, D_MODEL), _inproj_mod_map(1, N_CTX // INPROJ_TM, BATCH)),
            pl.BlockSpec((1, INPROJ_TN, D_MODEL), lambda i, k: (jnp.where(k == 2, last, k + 1), 0, 0)),
        ],
        out_specs=pl.BlockSpec((INPROJ_TM, INPROJ_TN), lambda i, k: (i, k)),
        out_shape=jax.ShapeDtypeStruct((N_CTX, out_w), BF16),
        scratch_shapes=[pltpu.VMEM((INPROJ_TM, D_MODEL), BF16)],
        compiler_params=_params(("arbitrary", "arbitrary")),
        name="inproj_ctx",
    )(ctx2d, mod_rows, mod_rows, w_blocks)


def _rope_tables(width):
    half = width // 2
    quarter = half // 2
    inv_freq = ROPE_BASE ** (-np.arange(0, half, 2, dtype=np.float64) / half)
    t = np.arange(SEQ)
    cos_parts, sin_parts = [], []
    for pos in (t // GRID_W, t % GRID_W):
        ang = pos[:, None].astype(np.float64) * inv_freq[None, :]
        c, s = np.cos(ang), np.sin(ang)
        cos_parts += [c, c]
        sin_parts += [-s, s]
    assert cos_parts[0].shape[1] == quarter
    return (np.concatenate(cos_parts, 1).astype(np.float32), np.concatenate(sin_parts, 1).astype(np.float32))


def _rope(x, cos, sin, quarter):
    lane = lax.broadcasted_iota(I32, x.shape, 1)
    first = (lane % (2 * quarter)) < quarter
    swapped = jnp.where(first, pltpu.roll(x, LANES - quarter, 1), pltpu.roll(x, quarter, 1))
    return x * cos + swapped * sin


def _dot_tn(a, b):
    return lax.dot_general(a, b, (((0,), (0,)), ((), ())), preferred_element_type=F32)


def _dot_nt(a, b):
    return lax.dot_general(a, b, (((1,), (1,)), ((), ())), preferred_element_type=F32)


RET_HG = 4


def _retention_kernel(q_ref, k_ref, v_ref, g_ref, kc_ref, vc_ref, cos_ref, sin_ref, df_ref, db_ref, gn_ref,
                      o_ref, qs_ref, ks_ref, st_ref):
    for hh in range(RET_HG):
        _retention_head(hh, slice(hh * LANES, (hh + 1) * LANES), q_ref, k_ref, v_ref, g_ref, kc_ref, vc_ref,
                        cos_ref, sin_ref, df_ref, db_ref, gn_ref, o_ref, qs_ref, ks_ref, st_ref)


def _retention_head(hh, hs, q_ref, k_ref, v_ref, g_ref, kc_ref, vc_ref, cos_ref, sin_ref, df_ref, db_ref, gn_ref,
                    o_ref, qs_ref, ks_ref, st_ref):
    k_scale = RET_DK ** -0.5
    lgf = jax.nn.log_sigmoid(df_ref[hh])
    lgb = jax.nn.log_sigmoid(db_ref[hh])

    cos = cos_ref[...]
    sin = sin_ref[...]
    qs_ref[hh] = _rope(q_ref[:, hs].astype(F32), cos, sin, RET_DK // 4).astype(BF16)
    ks_ref[hh] = _rope(k_ref[:, hs].astype(F32), cos, sin, RET_DK // 4) * k_scale

    rowi = lax.broadcasted_iota(I32, (CHUNK, LANES), 0).astype(F32)
    coli = lax.broadcasted_iota(I32, (CHUNK, LANES), 1).astype(F32)
    diff = rowi - coli
    decay = jnp.exp(jnp.where(diff >= 0, lgf * diff, -lgb * diff)) * jnp.where(diff == 0, 2.0, 1.0)
    zeta_f = jnp.exp(lgf * (CHUNK - 1.0 - rowi))
    eta_b = jnp.exp(lgb * rowi)
    xi_f = jnp.exp(lgf * (rowi + 1.0))
    xi_b = jnp.exp(lgb * (CHUNK - rowi))
    cdec_f = jnp.exp(lgf * float(CHUNK))
    cdec_b = jnp.exp(lgb * float(CHUNK))

    crow = lax.broadcasted_iota(I32, (CTX_LEN, LANES), 0).astype(F32)
    kc = kc_ref[:, hs].astype(F32) * k_scale
    vc = vc_ref[:, hs]
    s_f = _dot_tn((kc * jnp.exp(lgf * (CTX_LEN - 1.0 - crow))).astype(BF16), vc)
    s_b = _dot_tn((kc * jnp.exp(lgb * crow)).astype(BF16), vc)

    upd_f, upd_b = [], []
    for i in range(N_CHUNKS):
        rows = pl.ds(i * CHUNK, CHUNK)
        kch = ks_ref[hh, rows, :]
        vch = v_ref[rows, hs]
        upd_f.append(_dot_tn((kch * zeta_f).astype(BF16), vch))
        upd_b.append(_dot_tn((kch * eta_b).astype(BF16), vch))
    state = s_f
    for i in range(N_CHUNKS):
        st_ref[hh, i, :, :RET_DV] = state.astype(BF16)
        state = cdec_f * state + upd_f[i]
    state = s_b
    for i in reversed(range(N_CHUNKS)):
        st_ref[hh, i, :, RET_DV:] = state.astype(BF16)
        state = cdec_b * state + upd_b[i]

    gn_w = gn_ref[:, hs]
    for i in range(N_CHUNKS):
        rows = pl.ds(i * CHUNK, CHUNK)
        qch = qs_ref[hh, rows, :]
        scores = _dot_nt(qch, ks_ref[hh, rows, :].astype(BF16)) * decay
        o = jnp.dot(scores.astype(BF16), v_ref[rows, hs], preferred_element_type=F32)
        cross = jnp.dot(qch, st_ref[hh, i], preferred_element_type=F32)
        o = o + xi_f * cross[:, :RET_DV] + xi_b * cross[:, RET_DV:]
        y = _plain_norm(o) * gn_w
        o_ref[rows, hs] = (_silu(g_ref[rows, hs].astype(F32)) * y).astype(BF16)


def _retention(proj, proj_c, cos, sin, decay_rows, gn_w):
    width = RET_HG * LANES
    groups = RET_HEADS // RET_HG
    blk = lambda off: pl.BlockSpec((SEQ, width), lambda b, hg: (b, off // width + hg))
    blk_c = lambda off: pl.BlockSpec((CTX_LEN, width), lambda b, hg: (b, off // width + hg))
    table = pl.BlockSpec((SEQ, LANES), lambda b, hg: (0, 0))
    return pl.pallas_call(
        _retention_kernel,
        grid=(BATCH, groups),
        in_specs=[
            blk(OFF_Q), blk(OFF_K), blk(OFF_V), blk(OFF_G), blk_c(CTX_OFF_K), blk_c(CTX_OFF_V), table, table,
            pl.BlockSpec((RET_HG, 1, LANES), lambda b, hg: (hg, 0, 0)),
            pl.BlockSpec((RET_HG, 1, LANES), lambda b, hg: (groups + hg, 0, 0)),
            pl.BlockSpec((1, width), lambda b, hg: (0, hg)),
        ],
        out_specs=pl.BlockSpec((SEQ, width), lambda b, hg: (b, hg)),
        out_shape=jax.ShapeDtypeStruct((N_TOK, RET_W), BF16),
        scratch_shapes=[
            pltpu.VMEM((RET_HG, SEQ, RET_DK), BF16),
            pltpu.VMEM((RET_HG, SEQ, RET_DK), F32),
            pltpu.VMEM((RET_HG, N_CHUNKS, RET_DK, 2 * RET_DV), BF16),
        ],
        compiler_params=_params(("arbitrary", "arbitrary")),
        name="retention",
    )(proj, proj, proj, proj, proj_c, proj_c, cos, sin, decay_rows, decay_rows, gn_w)


MLA_TM = 512


def _rms_norm(x, w):
    return x * lax.rsqrt(jnp.mean(x * x, -1, keepdims=True) + EPS) * w


def _mla_kv(ckv_ref, kpe_ref, kvn_ref, wkv_ref, cos_ref, sin_ref, k_ref, v_ref, rotate):
    ckv = _rms_norm(ckv_ref[...].astype(F32), kvn_ref[...]).astype(BF16)
    kv = jnp.dot(ckv, wkv_ref[...], preferred_element_type=F32)
    lane = lax.broadcasted_iota(I32, (ckv.shape[0], LANES), 1)
    kpe = jnp.where(lane < MLA_ROPE, kpe_ref[...].astype(F32), 0.0)
    if rotate:
        kpe = _rope(kpe, cos_ref[...], sin_ref[...], MLA_ROPE // 4)
    kpe = kpe.astype(BF16)
    for h in range(MLA_HEADS):
        k_ref[:, h * QK_PAD:h * QK_PAD + MLA_NOPE] = kv[:, 2 * h * LANES:(2 * h + 1) * LANES].astype(BF16)
        k_ref[:, h * QK_PAD + MLA_NOPE:(h + 1) * QK_PAD] = kpe
        v_ref[:, h * MLA_DV:(h + 1) * MLA_DV] = kv[:, (2 * h + 1) * LANES:(2 * h + 2) * LANES].astype(BF16)


def _mla_latent_kernel(cq_ref, ckv_ref, kpe_ref, qn_ref, kvn_ref, wq_ref, wkv_ref, cos_ref, sin_ref,
                       q_ref, k_ref, v_ref):
    cq = _rms_norm(cq_ref[...].astype(F32), qn_ref[...]).astype(BF16)
    q = jnp.dot(cq, wq_ref[...], preferred_element_type=F32)
    cos = cos_ref[...]
    sin = sin_ref[...]
    scale = MLA_DQ ** -0.5 * LOG2_E
    for h in range(MLA_HEADS):
        lo = h * QK_PAD
        q_ref[:, lo:lo + MLA_NOPE] = (q[:, lo:lo + MLA_NOPE] * scale).astype(BF16)
        qpe = _rope(q[:, lo + MLA_NOPE:lo + QK_PAD], cos, sin, MLA_ROPE // 4)
        q_ref[:, lo + MLA_NOPE:lo + QK_PAD] = (qpe * scale).astype(BF16)
    _mla_kv(ckv_ref, kpe_ref, kvn_ref, wkv_ref, cos_ref, sin_ref, k_ref, v_ref, rotate=True)


def _mla_context_kernel(ckv_ref, kpe_ref, kvn_ref, wkv_ref, k_ref, v_ref):
    _mla_kv(ckv_ref, kpe_ref, kvn_ref, wkv_ref, None, None, k_ref, v_ref, rotate=False)


def _mla_latent(proj, q_norm, kv_norm, wq_pad, wkv, cos, sin):
    row = lambda w, off: pl.BlockSpec((MLA_TM, w), lambda i: (i, off // w))
    full = lambda a: pl.BlockSpec(a.shape, lambda i: (0, 0))
    table = pl.BlockSpec((MLA_TM, LANES), lambda i: (i % (SEQ // MLA_TM), 0))
    return pl.pallas_call(
        _mla_latent_kernel,
        grid=(N_TOK // MLA_TM,),
        in_specs=[row(MLA_Q_LORA, OFF_CQ), row(MLA_KV_LORA, OFF_CKV), row(LANES, OFF_KPE),
                  full(q_norm), full(kv_norm), full(wq_pad), full(wkv), table, table],
        out_specs=[
            pl.BlockSpec((MLA_TM, MLA_HEADS * QK_PAD), lambda i: (i, 0)),
            pl.BlockSpec((MLA_TM, MLA_HEADS * QK_PAD), lambda i: (i, 0)),
            pl.BlockSpec((MLA_TM, MLA_W), lambda i: (i, 0)),
        ],
        out_shape=[
            jax.ShapeDtypeStruct((N_TOK, MLA_HEADS * QK_PAD), BF16),
            jax.ShapeDtypeStruct((N_TOK, MLA_HEADS * QK_PAD), BF16),
            jax.ShapeDtypeStruct((N_TOK, MLA_W), BF16),
        ],
        compiler_params=_params(("arbitrary",)),
        name="mla_latent",
    )(proj, proj, proj, q_norm, kv_norm, wq_pad, wkv, cos, sin)


def _mla_context(proj_c, kv_norm, wkv):
    row = lambda w, off: pl.BlockSpec((MLA_TM, w), lambda i: (i, off // w))
    full = lambda a: pl.BlockSpec(a.shape, lambda i: (0, 0))
    return pl.pallas_call(
        _mla_context_kernel,
        grid=(N_CTX // MLA_TM,),
        in_specs=[row(MLA_KV_LORA, CTX_OFF_CKV), row(LANES, CTX_OFF_KPE), full(kv_norm), full(wkv)],
        out_specs=[
            pl.BlockSpec((MLA_TM, MLA_HEADS * QK_PAD), lambda i: (i, 0)),
            pl.BlockSpec((MLA_TM, MLA_W), lambda i: (i, 0)),
        ],
        out_shape=[
            jax.ShapeDtypeStruct((N_CTX, MLA_HEADS * QK_PAD), BF16),
            jax.ShapeDtypeStruct((N_CTX, MLA_W), BF16),
        ],
        compiler_params=_params(("arbitrary",)),
        name="mla_context",
    )(proj_c, proj_c, kv_norm, wkv)


ATT_TQ = 256


def _attention_kernel(q_ref, kc_ref, kl_ref, vc_ref, vl_ref, wo_ref, o_ref, wo_bf_ref, vext_ref):
    wo_bf_ref[...] = wo_ref[...].astype(BF16)

    @pl.when(pl.program_id(1) == 0)
    def _():
        lane = lax.broadcasted_iota(I32, (CTX_LEN + SEQ, MLA_DV), 1)
        ones_col = jnp.where(lane == 0, 1.0, 0.0).astype(BF16)
        for h in range(MLA_HEADS):
            dv = slice(h * MLA_DV, (h + 1) * MLA_DV)
            vext_ref[h, :CTX_LEN, :MLA_DV] = vc_ref[:, dv]
            vext_ref[h, CTX_LEN:, :MLA_DV] = vl_ref[:, dv]
            vext_ref[h, :, MLA_DV:] = ones_col

    for h in range(MLA_HEADS):
        qk = slice(h * QK_PAD, (h + 1) * QK_PAD)
        q = q_ref[:, qk]
        s_c = _dot_nt(q, kc_ref[:, qk])
        s_l = _dot_nt(q, kl_ref[:, qk])
        m = jnp.maximum(jnp.max(s_c, -1, keepdims=True), jnp.max(s_l, -1, keepdims=True))
        p_c = jnp.exp2(s_c - m).astype(BF16)
        p_l = jnp.exp2(s_l - m).astype(BF16)
        o = (jnp.dot(p_c, vext_ref[h, :CTX_LEN, :], preferred_element_type=F32)
             + jnp.dot(p_l, vext_ref[h, CTX_LEN:, :], preferred_element_type=F32))
        o_ref[:, h * MLA_DV:(h + 1) * MLA_DV] = (o[:, :MLA_DV] / o[:, MLA_DV:MLA_DV + 1]).astype(BF16)


def _attention(q, k_ctx, k_lat, v_ctx, v_lat, w_o):
    tiles = SEQ // ATT_TQ
    band = w_o.shape[0] // (BATCH * tiles)
    step = lambda b, i: (b * tiles + i, 0)
    return pl.pallas_call(
        _attention_kernel,
        grid=(BATCH, tiles),
        in_specs=[
            pl.BlockSpec((ATT_TQ, MLA_HEADS * QK_PAD), step),
            pl.BlockSpec((CTX_LEN, MLA_HEADS * QK_PAD), lambda b, i: (b, 0)),
            pl.BlockSpec((SEQ, MLA_HEADS * QK_PAD), lambda b, i: (b, 0)),
            pl.BlockSpec((CTX_LEN, MLA_W), lambda b, i: (b, 0)),
            pl.BlockSpec((SEQ, MLA_W), lambda b, i: (b, 0)),
            pl.BlockSpec((band, D_MODEL), step),
        ],
        out_specs=[pl.BlockSpec((ATT_TQ, MLA_W), step), pl.BlockSpec((band, D_MODEL), step)],
        out_shape=[jax.ShapeDtypeStruct((N_TOK, MLA_W), BF16), jax.ShapeDtypeStruct(w_o.shape, BF16)],
        scratch_shapes=[pltpu.VMEM((MLA_HEADS, CTX_LEN + SEQ, 2 * MLA_DV), BF16)],
        compiler_params=_params(("arbitrary", "arbitrary")),
        name="attention",
    )(q, k_ctx, k_lat, v_ctx, v_lat, w_o)


OUT_TM = 512
OUT_SUB = 256
ROUTE_E1, ROUTE_E2, ROUTE_R1, ROUTE_R2, ROUTE_W1, ROUTE_W2 = range(6)


def _outproj_kernel(ret_ref, att_ref, x_ref, wo_ref, gate_ref, shift_ref, scale_ref, lnw_ref, lnb_ref,
                    wr_ref, br_ref, h_ref, t_ref, route_ref, route_t_ref, count_ref, carry_ref, wsplit_ref):
    @pl.when(pl.program_id(0) == 0)
    def _():
        carry_ref[...] = jnp.zeros_like(carry_ref)
        w = wr_ref[...]
        w_hi = w.astype(BF16)
        wsplit_ref[:, :LANES] = w_hi
        wsplit_ref[:, LANES:] = (w - w_hi.astype(F32)).astype(BF16)

    subtiles = [slice(s * OUT_SUB, (s + 1) * OUT_SUB) for s in range(OUT_TM // OUT_SUB)]
    mixes = [jnp.dot(ret_ref[rows, :], wo_ref[:RET_W, :], preferred_element_type=F32)
             + jnp.dot(att_ref[rows, :], wo_ref[RET_W:, :], preferred_element_type=F32) for rows in subtiles]
    logits = [_outproj_norms(rows, mix, x_ref, gate_ref, shift_ref, scale_ref, lnw_ref, lnb_ref, br_ref, h_ref, t_ref,
                             wsplit_ref) for rows, mix in zip(subtiles, mixes)]
    _route(jnp.concatenate(logits, 0), carry_ref, count_ref, route_ref, route_t_ref)


def _outproj_norms(rows, mix, x_ref, gate_ref, shift_ref, scale_ref, lnw_ref, lnb_ref, br_ref, h_ref, t_ref,
                   wsplit_ref):
    h = _plain_norm(DEEPNORM_ALPHA * x_ref[rows, :] + gate_ref[0] * mix) * lnw_ref[...] + lnb_ref[...]
    h_ref[rows, :] = h
    t = _plain_norm(h) * (1.0 + scale_ref[0]) + shift_ref[0]
    t_ref[rows] = _to_token_tiles(t)

    t_hi = t.astype(BF16)
    t_lo = (t - t_hi.astype(F32)).astype(BF16)
    main = jnp.dot(t_hi, wsplit_ref[...], preferred_element_type=F32)
    corr = jnp.dot(t_lo, wsplit_ref[:, :LANES], preferred_element_type=F32)
    return main[:, :LANES] + (main[:, LANES:] + corr) + br_ref[...]


def _route(logits, carry_ref, count_ref, route_ref, route_t_ref):
    lane = lax.broadcasted_iota(I32, logits.shape, 1).astype(F32)
    neg = -jnp.inf
    big = float(LANES)

    def first_lane_of(mask):
        return jnp.min(jnp.where(mask, lane, big), -1, keepdims=True)

    is_group = lane < N_GROUPS
    gl = jnp.where(is_group, logits, neg)
    g_max = jnp.max(gl, -1, keepdims=True)
    g_idx = first_lane_of(is_group & (gl == g_max))
    g_prob = 1.0 / jnp.sum(jnp.where(is_group, jnp.exp(logits - g_max), 0.0), -1, keepdims=True)

    lo = N_GROUPS + g_idx * EXPERTS_PER_GROUP
    in_group = (lane >= lo) & (lane < lo + EXPERTS_PER_GROUP)
    el = jnp.where(in_group, logits, neg)
    v1 = jnp.max(el, -1, keepdims=True)
    i1 = first_lane_of(in_group & (el == v1))
    rest = in_group & (lane != i1)
    el2 = jnp.where(rest, logits, neg)
    v2 = jnp.max(el2, -1, keepdims=True)
    i2 = first_lane_of(rest & (el2 == v2))
    d = jnp.exp(v2 - v1)
    w1 = g_prob / (1.0 + d)
    w2 = g_prob * d / (1.0 + d)

    onehot = jnp.where((lane == i1) | (lane == i2), 1.0, 0.0)
    r = lax.broadcasted_iota(I32, (OUT_TM, OUT_TM), 0)
    c = lax.broadcasted_iota(I32, (OUT_TM, OUT_TM), 1)
    tri = jnp.where(c < r, 1.0, 0.0).astype(BF16)
    carry = carry_ref[0:1, :]
    before = jnp.dot(tri, onehot.astype(BF16), preferred_element_type=F32) + carry
    r1 = jnp.sum(jnp.where(lane == i1, before, 0.0), -1, keepdims=True)
    r2 = jnp.sum(jnp.where(lane == i2, before, 0.0), -1, keepdims=True)
    carry = carry + jnp.sum(onehot, 0, keepdims=True)
    carry_ref[...] = jnp.broadcast_to(carry, carry_ref.shape)
    count_ref[...] = jnp.broadcast_to(carry, count_ref.shape)

    rec = jnp.zeros_like(logits)
    for slot, val in ((ROUTE_E1, i1 - N_GROUPS), (ROUTE_E2, i2 - N_GROUPS), (ROUTE_R1, r1), (ROUTE_R2, r2),
                      (ROUTE_W1, w1), (ROUTE_W2, w2)):
        rec = jnp.where(lane == slot, val, rec)
    route_ref[...] = rec
    route_t_ref[...] = rec.T[:8, :]


def _outproj(ret, att, x2d, wo_bf, mod_rows, ln_w, ln_b, w_route, b_route):
    tiles_per_sample = SEQ // OUT_TM
    row = lambda w: pl.BlockSpec((OUT_TM, w), lambda i: (i, 0))
    full = lambda a: pl.BlockSpec(a.shape, lambda i: (0, 0))
    mod = lambda j: pl.BlockSpec((1, 1, D_MODEL), lambda i: ((i // tiles_per_sample) * 6 + j, 0, 0))
    return pl.pallas_call(
        _outproj_kernel,
        grid=(N_TOK // OUT_TM,),
        in_specs=[row(RET_W), row(MLA_W), row(D_MODEL), full(wo_bf), mod(2), mod(3), mod(4),
                  full(ln_w), full(ln_b), full(w_route), full(b_route)],
        out_specs=[row(D_MODEL), pl.BlockSpec((OUT_TM, TOKEN_SUB, LANES), lambda i: (i, 0, 0)), row(LANES),
                   pl.BlockSpec((8, OUT_TM), lambda i: (0, i)), pl.BlockSpec((8, LANES), lambda i: (0, 0))],
        out_shape=[
            jax.ShapeDtypeStruct((N_TOK, D_MODEL), F32),
            jax.ShapeDtypeStruct((N_TOK, TOKEN_SUB, LANES), BF16),
            jax.ShapeDtypeStruct((N_TOK, LANES), F32),
            jax.ShapeDtypeStruct((8, N_TOK), F32),
            jax.ShapeDtypeStruct((8, LANES), F32),
        ],
        scratch_shapes=[pltpu.VMEM((8, LANES), F32), pltpu.VMEM((D_MODEL, 2 * LANES), BF16)],
        compiler_params=_params(("arbitrary",)),
        name="outproj_route",
    )(ret, att, x2d, wo_bf, mod_rows, mod_rows, mod_rows, ln_w, ln_b, w_route, b_route)


DISPATCH_TM = 256


def _dispatch_kernel(pos_ref, t_ref, xs_ref, sem):
    i = pl.program_id(0)

    def copy(slot, j):
        return pltpu.make_async_copy(t_ref.at[j], xs_ref.at[pos_ref[slot * N_TOK + i * DISPATCH_TM + j]], sem)

    for slot in range(2):
        lax.fori_loop(0, DISPATCH_TM, lambda j, c, slot=slot: (copy(slot, j).start(priority=slot), c)[1], 0,
                      unroll=8)
    for slot in range(2):
        pltpu.make_async_copy(t_ref, xs_ref.at[pl.ds(0, DISPATCH_TM)], sem).wait()


def _dispatch(pos, t_tiles):
    return pl.pallas_call(
        _dispatch_kernel,
        grid_spec=pltpu.PrefetchScalarGridSpec(
            num_scalar_prefetch=1,
            grid=(N_TOK // DISPATCH_TM,),
            in_specs=[pl.BlockSpec((DISPATCH_TM, TOKEN_SUB, LANES), lambda i, pos: (i, 0, 0))],
            out_specs=pl.BlockSpec(memory_space=pl.ANY),
            scratch_shapes=[pltpu.SemaphoreType.DMA(())],
        ),
        out_shape=jax.ShapeDtypeStruct((N_PAIRS, TOKEN_SUB, LANES), BF16),
        compiler_params=_params(("arbitrary",)),
        name="dispatch",
    )(pos, t_tiles)


def _experts_kernel(rend_ref, xs_ref, wg_ref, wu_ref, wd_ref, ys_ref, wgu_ref, wdn_ref, xbuf_ref, ybuf_ref,
                    xsem, ysem):
    e = pl.program_id(0)
    lo = jnp.where(e == 0, 0, rend_ref[jnp.maximum(e - 1, 0)])
    hi = rend_ref[e]

    def x_copy(g):
        return pltpu.make_async_copy(xs_ref.at[pl.ds(g * ROW_TILE, ROW_TILE)], xbuf_ref.at[g % 2], xsem.at[g % 2])

    def y_copy(g):
        return pltpu.make_async_copy(ybuf_ref.at[g % 2], ys_ref.at[pl.ds(g * ROW_TILE, ROW_TILE)], ysem.at[g % 2])

    @pl.when(e == 0)
    def _():
        x_copy(0).start()

    @pl.when(hi > lo)
    def _():
        wgu_ref[:, :D_EXPERT] = wg_ref[0].astype(BF16)
        wgu_ref[:, D_EXPERT:] = wu_ref[0].astype(BF16)
        wdn_ref[...] = wd_ref[0].astype(BF16)

        def tile(g, c):
            row0 = g * ROW_TILE
            owns_first_row = lo <= row0
            owns_last_row = hi >= row0 + ROW_TILE

            @pl.when(owns_first_row)
            def _():
                @pl.when(g + 1 < N_TILES)
                def _():
                    x_copy(g + 1).start()

                x_copy(g).wait()

            gu = jnp.dot(_from_token_tiles(xbuf_ref[g % 2]), wgu_ref[...], preferred_element_type=F32)
            hid = _silu(gu[:, :D_EXPERT]) * gu[:, D_EXPERT:]
            y = jnp.dot(hid.astype(BF16), wdn_ref[...], preferred_element_type=F32)

            @pl.when(owns_first_row)
            def _():
                @pl.when(g >= 2)
                def _():
                    y_copy(g - 2).wait()

                ybuf_ref[g % 2] = _to_token_tiles(y)

            @pl.when(jnp.logical_not(owns_first_row))
            def _():
                row = lax.broadcasted_iota(I32, y.shape, 0)
                earlier = _from_token_tiles(ybuf_ref[g % 2]).astype(F32)
                ybuf_ref[g % 2] = _to_token_tiles(jnp.where(row >= lo - row0, y, earlier))

            @pl.when(owns_last_row)
            def _():
                y_copy(g).start()

            return c

        lax.fori_loop(lo // ROW_TILE, (hi - 1) // ROW_TILE + 1, tile, 0)

    @pl.when(e == N_EXPERTS - 1)
    def _():
        y_copy(N_TILES - 2).wait()
        y_copy(N_TILES - 1).wait()


def _experts(row_end, xs, w_gate, w_up, w_down):
    w_gate = w_gate.reshape(N_EXPERTS, D_MODEL, D_EXPERT)
    w_up = w_up.reshape(N_EXPERTS, D_MODEL, D_EXPERT)
    w_down = w_down.reshape(N_EXPERTS, D_EXPERT, D_MODEL)
    expert = lambda e, rend: (e, 0, 0)
    tile_buf = pltpu.VMEM((2, ROW_TILE, TOKEN_SUB, LANES), BF16)
    return pl.pallas_call(
        _experts_kernel,
        grid_spec=pltpu.PrefetchScalarGridSpec(
            num_scalar_prefetch=1,
            grid=(N_EXPERTS,),
            in_specs=[
                pl.BlockSpec(memory_space=pl.ANY),
                pl.BlockSpec((1, D_MODEL, D_EXPERT), expert),
                pl.BlockSpec((1, D_MODEL, D_EXPERT), expert),
                pl.BlockSpec((1, D_EXPERT, D_MODEL), expert),
            ],
            out_specs=pl.BlockSpec(memory_space=pl.ANY),
            scratch_shapes=[pltpu.VMEM((D_MODEL, 2 * D_EXPERT), BF16), pltpu.VMEM((D_EXPERT, D_MODEL), BF16),
                            tile_buf, tile_buf, pltpu.SemaphoreType.DMA((2,)), pltpu.SemaphoreType.DMA((2,))],
        ),
        out_shape=jax.ShapeDtypeStruct((N_PAIRS, TOKEN_SUB, LANES), BF16),
        compiler_params=_params(("arbitrary",)),
        name="experts",
    )(row_end, xs, w_gate, w_up, w_down)


COMB_TM = 256


def _combine_kernel(pos_ref, ys_ref, h_ref, route_ref, gate_ref, lnw_ref, lnb_ref, o_ref, buf_ref, sem):
    i = pl.program_id(0)
    n = pl.num_programs(0)

    def copy(step, slot, pair, j):
        src = pos_ref[pair * N_TOK + step * COMB_TM + j]
        return pltpu.make_async_copy(ys_ref.at[src], buf_ref.at[slot, pair * COMB_TM + j], sem.at[slot])

    def start_all(step, slot):
        for pair in range(2):
            lax.fori_loop(0, COMB_TM, lambda j, c, pair=pair: (copy(step, slot, pair, j).start(priority=pair), c)[1], 0,
                          unroll=8)

    def wait_all(step, slot):
        pltpu.make_async_copy(ys_ref.at[pl.ds(0, 2 * COMB_TM)], buf_ref.at[slot], sem.at[slot]).wait()

    @pl.when(i == 0)
    def _():
        start_all(0, 0)

    @pl.when(i + 1 < n)
    def _():
        start_all(i + 1, (i + 1) % 2)

    slot = i % 2
    wait_all(i, slot)

    y1 = _from_token_tiles(buf_ref[slot, :COMB_TM]).astype(F32)
    y2 = _from_token_tiles(buf_ref[slot, COMB_TM:]).astype(F32)
    y = route_ref[:, ROUTE_W1:ROUTE_W1 + 1] * y1 + route_ref[:, ROUTE_W2:ROUTE_W2 + 1] * y2
    z = DEEPNORM_ALPHA * h_ref[...] + gate_ref[0] * y
    o_ref[...] = _plain_norm(z) * lnw_ref[...] + lnb_ref[...]


def _combine(pos, ys, h, route, mod_rows, ln_w, ln_b):
    tiles_per_sample = SEQ // COMB_TM
    row = lambda w: pl.BlockSpec((COMB_TM, w), lambda i, pos: (i, 0))
    full = lambda a: pl.BlockSpec(a.shape, lambda i, pos: (0, 0))
    return pl.pallas_call(
        _combine_kernel,
        grid_spec=pltpu.PrefetchScalarGridSpec(
            num_scalar_prefetch=1,
            grid=(N_TOK // COMB_TM,),
            in_specs=[
                pl.BlockSpec(memory_space=pl.ANY), row(D_MODEL), row(LANES),
                pl.BlockSpec((1, 1, D_MODEL), lambda i, pos: ((i // tiles_per_sample) * 6 + 5, 0, 0)),
                full(ln_w), full(ln_b),
            ],
            out_specs=row(D_MODEL),
            scratch_shapes=[pltpu.VMEM((2, 2 * COMB_TM, TOKEN_SUB, LANES), BF16), pltpu.SemaphoreType.DMA((2,))],
        ),
        out_shape=jax.ShapeDtypeStruct((N_TOK, D_MODEL), F32),
        compiler_params=_params(("arbitrary",)),
        name="combine",
    )(pos, ys, h, route, mod_rows, ln_w, ln_b)


def _routing_tables(route_t, counts):
    cnt = counts[0, N_GROUPS:N_ROUTE].astype(I32)
    row_end = jnp.cumsum(cnt)
    e = route_t[ROUTE_E1:ROUTE_E2 + 1].astype(I32)
    rank = route_t[ROUTE_R1:ROUTE_R2 + 1].astype(I32)
    earlier = jnp.arange(N_EXPERTS, dtype=I32)[:, None, None] < e[None]
    base = jnp.sum(jnp.where(earlier, cnt[:, None, None], 0), 0)
    pos = (base + rank).reshape(-1)
    return pos, row_end


def kernel(x, c, ctx, c_ctx, w_ada, b_ada, w_in, ret_decay, ret_gn_w, mla_q_norm, mla_kv_norm, w_uq, w_ukv, w_o,
           ln1_w, ln1_b, router_group_w, router_group_b, router_expert_w, router_expert_b, expert_w_gate,
           expert_w_up, expert_w_down, ln2_w, ln2_b):
    x2d = x.reshape(N_TOK, D_MODEL)
    ctx2d = ctx.reshape(N_CTX, D_MODEL)

    cc = jnp.zeros((8, D_MODEL), F32).at[:BATCH].set(c).at[BATCH].set(c_ctx)
    mod = _ada(cc, w_ada[0], b_ada)
    mod_rows = mod.reshape(8 * 6, 1, D_MODEL)

    proj, w_in_blocks = _inproj_latent(x2d, mod_rows, jnp.swapaxes(w_in, 1, 2))
    proj_c = _inproj_context(ctx2d, mod_rows, w_in_blocks)

    cos_r, sin_r = _rope_tables(RET_DK)
    decay_rows = jnp.broadcast_to(ret_decay[0].reshape(2 * RET_HEADS, 1, 1), (2 * RET_HEADS, 1, LANES))
    ret = _retention(proj, proj_c, jnp.asarray(cos_r), jnp.asarray(sin_r), decay_rows, ret_gn_w)

    cos_m, sin_m = _rope_tables(MLA_ROPE)
    cos_m = np.concatenate([cos_m, np.ones_like(cos_m)], 1)
    sin_m = np.concatenate([sin_m, np.zeros_like(sin_m)], 1)
    wq = w_uq[0].reshape(MLA_Q_LORA, MLA_HEADS, MLA_DQ)
    wq_pad = jnp.pad(wq, ((0, 0), (0, 0), (0, QK_PAD - MLA_DQ))).reshape(MLA_Q_LORA, MLA_HEADS * QK_PAD).astype(BF16)
    wkv = w_ukv[0].astype(BF16)
    q, k_lat, v_lat = _mla_latent(proj, mla_q_norm, mla_kv_norm, wq_pad, wkv, jnp.asarray(cos_m), jnp.asarray(sin_m))
    k_ctx, v_ctx = _mla_context(proj_c, mla_kv_norm, wkv)
    att, w_o_bf = _attention(q, k_ctx, k_lat, v_ctx, v_lat, w_o[0])

    w_route = jnp.concatenate(
        [router_group_w[0], router_expert_w[0].transpose(1, 0, 2).reshape(D_MODEL, N_EXPERTS),
         jnp.zeros((D_MODEL, LANES - N_ROUTE), F32)], 1)
    b_route = jnp.concatenate(
        [router_group_b[0], router_expert_b[0].reshape(N_EXPERTS), jnp.zeros((LANES - N_ROUTE,), F32)])[None]
    h, t_tiles, route, route_t, counts = _outproj(ret, att, x2d, w_o_bf, mod_rows, ln1_w, ln1_b,
                                                   w_route, b_route)

    pos, row_end = _routing_tables(route_t, counts)
    xs = _dispatch(pos, t_tiles)
    ys = _experts(row_end, xs, expert_w_gate[0], expert_w_up[0], expert_w_down[0])
    out = _combine(pos, ys, h, route, mod_rows, ln2_w, ln2_b)
    return out.reshape(BATCH, SEQ, D_MODEL)
```

```python
import numpy as np
import jax
import jax.numpy as jnp
from jax import lax
from jax.experimental import pallas as pl
from jax.experimental.pallas import tpu as pltpu

F32 = jnp.float32
BF16 = jnp.bfloat16
I32 = jnp.int32

D_MODEL = 2048
BATCH = 4
SEQ = 2048
GRID_W = 64
CTX_LEN = 256
N_TOK = BATCH * SEQ
N_CTX = BATCH * CTX_LEN

RET_HEADS = 8
RET_DK = 128
RET_DV = 128
RET_W = RET_HEADS * RET_DV
CHUNK = 128
N_CHUNKS = SEQ // CHUNK

MLA_HEADS = 8
MLA_Q_LORA = 512
MLA_KV_LORA = 256
MLA_NOPE = 128
MLA_ROPE = 64
MLA_DV = 128
MLA_W = MLA_HEADS * MLA_DV
MLA_DQ = MLA_NOPE + MLA_ROPE
QK_PAD = 256

IN_SIZES = (RET_HEADS * RET_DK, RET_HEADS * RET_DK, RET_W, RET_W, MLA_Q_LORA, MLA_KV_LORA, MLA_ROPE)
IN_W = sum(IN_SIZES)
OFF_Q, OFF_K, OFF_V, OFF_G, OFF_CQ, OFF_CKV, OFF_KPE = (int(v) for v in np.cumsum((0,) + IN_SIZES[:-1]))

N_GROUPS = 4
EXPERTS_PER_GROUP = 8
N_EXPERTS = N_GROUPS * EXPERTS_PER_GROUP
D_EXPERT = 512
N_ROUTE = N_GROUPS + N_EXPERTS

LOG2_E = float(np.log2(np.e))
ROPE_BASE = 10000.0
EPS = 1e-6
DEPTH = 1
DEEPNORM_ALPHA = (2.0 * DEPTH) ** 0.25

LANES = 128
ROW_TILE = 256
N_PAIRS = 2 * N_TOK
N_TILES = N_PAIRS // ROW_TILE
TOKEN_SUB = D_MODEL // LANES

VMEM_LIMIT = 56 * 1024 * 1024


def _params(sem, vmem=VMEM_LIMIT):
    return pltpu.CompilerParams(dimension_semantics=sem, vmem_limit_bytes=vmem)


def _silu(x):
    return x * (1.0 / (1.0 + jnp.exp(-x)))


def _plain_norm(x):
    mu = jnp.mean(x, -1, keepdims=True)
    xc = x - mu
    var = jnp.mean(xc * xc, -1, keepdims=True)
    return xc * lax.rsqrt(var + EPS)


def _to_token_tiles(x):
    return x.astype(BF16).reshape(x.shape[0], TOKEN_SUB, LANES)


def _from_token_tiles(x):
    return x.reshape(x.shape[0], D_MODEL)


ADA_TN = 1024


ADA_BANDS = 4


def _ada_kernel(cc_ref, *refs):
    w_refs, b_ref, o_ref = refs[:ADA_BANDS], refs[ADA_BANDS], refs[ADA_BANDS + 1]
    s = _silu(cc_ref[...])
    band = D_MODEL // ADA_BANDS
    acc = b_ref[...]
    for q, w_ref in enumerate(w_refs):
        acc = acc + jnp.dot(s[:, q * band:(q + 1) * band], w_ref[...], preferred_element_type=F32,
                            precision=lax.Precision.HIGHEST)
    o_ref[...] = acc


def _ada(cc, w_ada, b_ada):
    n = w_ada.shape[1]
    band = D_MODEL // ADA_BANDS
    return pl.pallas_call(
        _ada_kernel,
        grid=(n // ADA_TN,),
        in_specs=[pl.BlockSpec((8, D_MODEL), lambda j: (0, 0))]
        + [pl.BlockSpec((band, ADA_TN), lambda j, q=q: (q, j)) for q in range(ADA_BANDS)]
        + [pl.BlockSpec((1, ADA_TN), lambda j: (0, j))],
        out_specs=pl.BlockSpec((8, ADA_TN), lambda j: (0, j)),
        out_shape=jax.ShapeDtypeStruct((8, n), F32),
        compiler_params=_params(("arbitrary",)),
        name="ada",
    )(cc, *([w_ada] * ADA_BANDS), b_ada)


INPROJ_TM = 1024
INPROJ_TN = 1024
assert OFF_K == INPROJ_TN and OFF_V == 2 * INPROJ_TN and OFF_CQ == 4 * INPROJ_TN
CTX_OFF_K, CTX_OFF_V = 0, INPROJ_TN
CTX_OFF_CKV = 2 * INPROJ_TN + (OFF_CKV - OFF_CQ)
CTX_OFF_KPE = 2 * INPROJ_TN + (OFF_KPE - OFF_CQ)


INPROJ_SUB = 256
INPROJ_BLOCKS = -(-IN_W // INPROJ_TN)


def _inproj_products(x_ref, shift_ref, scale_ref, w, o_ref, xn_ref):
    first = pl.program_id(1) == 0

    @pl.when(first)
    def _():
        for s in range(INPROJ_TM // INPROJ_SUB):
            rows = slice(s * INPROJ_SUB, (s + 1) * INPROJ_SUB)
            y = (_plain_norm(x_ref[rows, :]) * (1.0 + scale_ref[0]) + shift_ref[0]).astype(BF16)
            xn_ref[rows, :] = y
            o_ref[rows, :] = _dot_nt(y, w[...]).astype(BF16)

    @pl.when(jnp.logical_not(first))
    def _():
        o_ref[...] = _dot_nt(xn_ref[...], w[...]).astype(BF16)


def _inproj_latent_kernel(x_ref, shift_ref, scale_ref, wf_ref, o_ref, wbf_ref, xn_ref, wbuf_ref, in_sem, out_sem):
    i = pl.program_id(0)
    k = pl.program_id(1)
    t = i * INPROJ_BLOCKS + k
    slot = t % 2
    n_steps = pl.num_programs(0) * INPROJ_BLOCKS

    def out_copy(block, s):
        return pltpu.make_async_copy(wbuf_ref.at[s], wbf_ref.at[block], out_sem.at[s])

    def in_copy(block, s):
        return pltpu.make_async_copy(wbf_ref.at[block], wbuf_ref.at[s], in_sem.at[s])

    @pl.when(i == 0)
    def _():
        @pl.when(k >= 2)
        def _():
            out_copy(k - 2, slot).wait()

        col = lax.broadcasted_iota(I32, (INPROJ_TN, D_MODEL), 0)
        wbuf_ref[slot] = jnp.where(col < IN_W - k * INPROJ_TN, wf_ref[0], 0.0).astype(BF16)
        out_copy(k, slot).start()

    @pl.when((t + 1 >= INPROJ_BLOCKS) & (t + 1 < n_steps))
    def _():
        @pl.when(t == INPROJ_BLOCKS - 1)
        def _():
            out_copy(INPROJ_BLOCKS - 2, 1 - slot).wait()

        @pl.when(t == INPROJ_BLOCKS)
        def _():
            out_copy(INPROJ_BLOCKS - 1, 1 - slot).wait()

        in_copy((k + 1) % INPROJ_BLOCKS, 1 - slot).start()

    @pl.when(i > 0)
    def _():
        in_copy(k, slot).wait()

    _inproj_products(x_ref, shift_ref, scale_ref, wbuf_ref.at[slot], o_ref, xn_ref)


def _inproj_context_kernel(x_ref, shift_ref, scale_ref, w_ref, o_ref, xn_ref):
    _inproj_products(x_ref, shift_ref, scale_ref, w_ref.at[0], o_ref, xn_ref)


def _inproj_mod_map(j, tiles_per_sample, sample_row0):
    return lambda i, k: ((sample_row0 + i // tiles_per_sample) * 6 + j, 0, 0)


def _inproj_latent(x2d, mod_rows, w_in_t):
    assert INPROJ_BLOCKS % 2 == 1
    last = INPROJ_BLOCKS - 1
    return pl.pallas_call(
        _inproj_latent_kernel,
        grid=(N_TOK // INPROJ_TM, INPROJ_BLOCKS),
        in_specs=[
            pl.BlockSpec((INPROJ_TM, D_MODEL), lambda i, k: (i, 0)),
            pl.BlockSpec((1, 1, D_MODEL), _inproj_mod_map(0, SEQ // INPROJ_TM, 0)),
            pl.BlockSpec((1, 1, D_MODEL), _inproj_mod_map(1, SEQ // INPROJ_TM, 0)),
            pl.BlockSpec((1, INPROJ_TN, D_MODEL), lambda i, k: (0, jnp.where(i == 0, k, last), 0)),
        ],
        out_specs=[pl.BlockSpec((INPROJ_TM, INPROJ_TN), lambda i, k: (i, k)), pl.BlockSpec(memory_space=pl.ANY)],
        out_shape=[jax.ShapeDtypeStruct((N_TOK, IN_W), BF16),
                   jax.ShapeDtypeStruct((INPROJ_BLOCKS, INPROJ_TN, D_MODEL), BF16)],
        scratch_shapes=[pltpu.VMEM((INPROJ_TM, D_MODEL), BF16), pltpu.VMEM((2, INPROJ_TN, D_MODEL), BF16),
                        pltpu.SemaphoreType.DMA((2,)), pltpu.SemaphoreType.DMA((2,))],
        compiler_params=_params(("arbitrary", "arbitrary")),
        name="inproj",
    )(x2d, mod_rows, mod_rows, w_in_t)


def _inproj_context(ctx2d, mod_rows, w_blocks):
    last = INPROJ_BLOCKS - 1
    out_w = 2 * INPROJ_TN + (IN_W - last * INPROJ_TN)
    return pl.pallas_call(
        _inproj_context_kernel,
        grid=(N_CTX // INPROJ_TM, 3),
        in_specs=[
            pl.BlockSpec((INPROJ_TM, D_MODEL), lambda i, k: (i, 0)),
            pl.BlockSpec((1, 1, D_MODEL), _inproj_mod_map(0, N_CTX // INPROJ_TM, BATCH)),
            pl.BlockSpec((1, 1, D_MODEL), _inproj_mod_map(1, N_CTX // INPROJ_TM, BATCH)),
            pl.BlockSpec((1, INPROJ_TN, D_MODEL), lambda i, k: (jnp.where(k == 2, last, k + 1), 0, 0)),
        ],
        out_specs=pl.BlockSpec((INPROJ_TM, INPROJ_TN), lambda i, k: (i, k)),
        out_shape=jax.ShapeDtypeStruct((N_CTX, out_w), BF16),
        scratch_shapes=[pltpu.VMEM((INPROJ_TM, D_MODEL), BF16)],
        compiler_params=_params(("arbitrary", "arbitrary")),
        name="inproj_ctx",
    )(ctx2d, mod_rows, mod_rows, w_blocks)


def _rope_tables(width):
    half = width // 2
    quarter = half // 2
    inv_freq = ROPE_BASE ** (-np.arange(0, half, 2, dtype=np.float64) / half)
    t = np.arange(SEQ)
    cos_parts, sin_parts = [], []
    for pos in (t // GRID_W, t % GRID_W):
        ang = pos[:, None].astype(np.float64) * inv_freq[None, :]
        c, s = np.cos(ang), np.sin(ang)
        cos_parts += [c, c]
        sin_parts += [-s, s]
    assert cos_parts[0].shape[1] == quarter
    return (np.concatenate(cos_parts, 1).astype(np.float32), np.concatenate(sin_parts, 1).astype(np.float32))


def _rope(x, cos, sin, quarter):
    lane = lax.broadcasted_iota(I32, x.shape, 1)
    first = (lane % (2 * quarter)) < quarter
    swapped = jnp.where(first, pltpu.roll(x, LANES - quarter, 1), pltpu.roll(x, quarter, 1))
    return x * cos + swapped * sin


def _dot_tn(a, b):
    return lax.dot_general(a, b, (((0,), (0,)), ((), ())), preferred_element_type=F32)


def _dot_nt(a, b):
    return lax.dot_general(a, b, (((1,), (1,)), ((), ())), preferred_element_type=F32)


RET_HG = 4


def _retention_kernel(q_ref, k_ref, v_ref, g_ref, kc_ref, vc_ref, cos_ref, sin_ref, df_ref, db_ref, gn_ref,
                      o_ref, qs_ref, ks_ref, st_ref):
    for hh in range(RET_HG):
        _retention_head(hh, slice(hh * LANES, (hh + 1) * LANES), q_ref, k_ref, v_ref, g_ref, kc_ref, vc_ref,
                        cos_ref, sin_ref, df_ref, db_ref, gn_ref, o_ref, qs_ref, ks_ref, st_ref)


def _retention_head(hh, hs, q_ref, k_ref, v_ref, g_ref, kc_ref, vc_ref, cos_ref, sin_ref, df_ref, db_ref, gn_ref,
                    o_ref, qs_ref, ks_ref, st_ref):
    k_scale = RET_DK ** -0.5
    lgf = jax.nn.log_sigmoid(df_ref[hh])
    lgb = jax.nn.log_sigmoid(db_ref[hh])

    cos = cos_ref[...]
    sin = sin_ref[...]
    qs_ref[hh] = _rope(q_ref[:, hs].astype(F32), cos, sin, RET_DK // 4).astype(BF16)
    ks_ref[hh] = _rope(k_ref[:, hs].astype(F32), cos, sin, RET_DK // 4) * k_scale

    rowi = lax.broadcasted_iota(I32, (CHUNK, LANES), 0).astype(F32)
    coli = lax.broadcasted_iota(I32, (CHUNK, LANES), 1).astype(F32)
    diff = rowi - coli
    decay = jnp.exp(jnp.where(diff >= 0, lgf * diff, -lgb * diff)) * jnp.where(diff == 0, 2.0, 1.0)
    zeta_f = jnp.exp(lgf * (CHUNK - 1.0 - rowi))
    eta_b = jnp.exp(lgb * rowi)
    xi_f = jnp.exp(lgf * (rowi + 1.0))
    xi_b = jnp.exp(lgb * (CHUNK - rowi))
    cdec_f = jnp.exp(lgf * float(CHUNK))
    cdec_b = jnp.exp(lgb * float(CHUNK))

    crow = lax.broadcasted_iota(I32, (CTX_LEN, LANES), 0).astype(F32)
    kc = kc_ref[:, hs].astype(F32) * k_scale
    vc = vc_ref[:, hs]
    s_f = _dot_tn((kc * jnp.exp(lgf * (CTX_LEN - 1.0 - crow))).astype(BF16), vc)
    s_b = _dot_tn((kc * jnp.exp(lgb * crow)).astype(BF16), vc)

    upd_f, upd_b = [], []
    for i in range(N_CHUNKS):
        rows = pl.ds(i * CHUNK, CHUNK)
        kch = ks_ref[hh, rows, :]
        vch = v_ref[rows, hs]
        upd_f.append(_dot_tn((kch * zeta_f).astype(BF16), vch))
        upd_b.append(_dot_tn((kch * eta_b).astype(BF16), vch))
    state = s_f
    for i in range(N_CHUNKS):
        st_ref[hh, i, :, :RET_DV] = state.astype(BF16)
        state = cdec_f * state + upd_f[i]
    state = s_b
    for i in reversed(range(N_CHUNKS)):
        st_ref[hh, i, :, RET_DV:] = state.astype(BF16)
        state = cdec_b * state + upd_b[i]

    gn_w = gn_ref[:, hs]
    for i in range(N_CHUNKS):
        rows = pl.ds(i * CHUNK, CHUNK)
        qch = qs_ref[hh, rows, :]
        scores = _dot_nt(qch, ks_ref[hh, rows, :].astype(BF16)) * decay
        o = jnp.dot(scores.astype(BF16), v_ref[rows, hs], preferred_element_type=F32)
        cross = jnp.dot(qch, st_ref[hh, i], preferred_element_type=F32)
        o = o + xi_f * cross[:, :RET_DV] + xi_b * cross[:, RET_DV:]
        y = _plain_norm(o) * gn_w
        o_ref[rows, hs] = (_silu(g_ref[rows, hs].astype(F32)) * y).astype(BF16)


def _retention(proj, proj_c, cos, sin, decay_rows, gn_w):
    width = RET_HG * LANES
    groups = RET_HEADS // RET_HG
    blk = lambda off: pl.BlockSpec((SEQ, width), lambda b, hg: (b, off // width + hg))
    blk_c = lambda off: pl.BlockSpec((CTX_LEN, width), lambda b, hg: (b, off // width + hg))
    table = pl.BlockSpec((SEQ, LANES), lambda b, hg: (0, 0))
    return pl.pallas_call(
        _retention_kernel,
        grid=(BATCH, groups),
        in_specs=[
            blk(OFF_Q), blk(OFF_K), blk(OFF_V), blk(OFF_G), blk_c(CTX_OFF_K), blk_c(CTX_OFF_V), table, table,
            pl.BlockSpec((RET_HG, 1, LANES), lambda b, hg: (hg, 0, 0)),
            pl.BlockSpec((RET_HG, 1, LANES), lambda b, hg: (groups + hg, 0, 0)),
            pl.BlockSpec((1, width), lambda b, hg: (0, hg)),
        ],
        out_specs=pl.BlockSpec((SEQ, width), lambda b, hg: (b, hg)),
        out_shape=jax.ShapeDtypeStruct((N_TOK, RET_W), BF16),
        scratch_shapes=[
            pltpu.VMEM((RET_HG, SEQ, RET_DK), BF16),
            pltpu.VMEM((RET_HG, SEQ, RET_DK), F32),
            pltpu.VMEM((RET_HG, N_CHUNKS, RET_DK, 2 * RET_DV), BF16),
        ],
        compiler_params=_params(("arbitrary", "arbitrary")),
        name="retention",
    )(proj, proj, proj, proj, proj_c, proj_c, cos, sin, decay_rows, decay_rows, gn_w)


MLA_TM = 512


def _rms_norm(x, w):
    return x * lax.rsqrt(jnp.mean(x * x, -1, keepdims=True) + EPS) * w


def _mla_kv(ckv_ref, kpe_ref, kvn_ref, wkv_ref, cos_ref, sin_ref, k_ref, v_ref, rotate):
    ckv = _rms_norm(ckv_ref[...].astype(F32), kvn_ref[...]).astype(BF16)
    kv = jnp.dot(ckv, wkv_ref[...], preferred_element_type=F32)
    lane = lax.broadcasted_iota(I32, (ckv.shape[0], LANES), 1)
    kpe = jnp.where(lane < MLA_ROPE, kpe_ref[...].astype(F32), 0.0)
    if rotate:
        kpe = _rope(kpe, cos_ref[...], sin_ref[...], MLA_ROPE // 4)
    kpe = kpe.astype(BF16)
    for h in range(MLA_HEADS):
        k_ref[:, h * QK_PAD:h * QK_PAD + MLA_NOPE] = kv[:, 2 * h * LANES:(2 * h + 1) * LANES].astype(BF16)
        k_ref[:, h * QK_PAD + MLA_NOPE:(h + 1) * QK_PAD] = kpe
        v_ref[:, h * MLA_DV:(h + 1) * MLA_DV] = kv[:, (2 * h + 1) * LANES:(2 * h + 2) * LANES].astype(BF16)


def _mla_latent_kernel(cq_ref, ckv_ref, kpe_ref, qn_ref, kvn_ref, wq_ref, wkv_ref, cos_ref, sin_ref,
                       q_ref, k_ref, v_ref):
    cq = _rms_norm(cq_ref[...].astype(F32), qn_ref[...]).astype(BF16)
    q = jnp.dot(cq, wq_ref[...], preferred_element_type=F32)
    cos = cos_ref[...]
    sin = sin_ref[...]
    scale = MLA_DQ ** -0.5 * LOG2_E
    for h in range(MLA_HEADS):
        lo = h * QK_PAD
        q_ref[:, lo:lo + MLA_NOPE] = (q[:, lo:lo + MLA_NOPE] * scale).astype(BF16)
        qpe = _rope(q[:, lo + MLA_NOPE:lo + QK_PAD], cos, sin, MLA_ROPE // 4)
        q_ref[:, lo + MLA_NOPE:lo + QK_PAD] = (qpe * scale).astype(BF16)
    _mla_kv(ckv_ref, kpe_ref, kvn_ref, wkv_ref, cos_ref, sin_ref, k_ref, v_ref, rotate=True)


def _mla_context_kernel(ckv_ref, kpe_ref, kvn_ref, wkv_ref, k_ref, v_ref):
    _mla_kv(ckv_ref, kpe_ref, kvn_ref, wkv_ref, None, None, k_ref, v_ref, rotate=False)


def _mla_latent(proj, q_norm, kv_norm, wq_pad, wkv, cos, sin):
    row = lambda w, off: pl.BlockSpec((MLA_TM, w), lambda i: (i, off // w))
    full = lambda a: pl.BlockSpec(a.shape, lambda i: (0, 0))
    table = pl.BlockSpec((MLA_TM, LANES), lambda i: (i % (SEQ // MLA_TM), 0))
    return pl.pallas_call(
        _mla_latent_kernel,
        grid=(N_TOK // MLA_TM,),
        in_specs=[row(MLA_Q_LORA, OFF_CQ), row(MLA_KV_LORA, OFF_CKV), row(LANES, OFF_KPE),
                  full(q_norm), full(kv_norm), full(wq_pad), full(wkv), table, table],
        out_specs=[
            pl.BlockSpec((MLA_TM, MLA_HEADS * QK_PAD), lambda i: (i, 0)),
            pl.BlockSpec((MLA_TM, MLA_HEADS * QK_PAD), lambda i: (i, 0)),
            pl.BlockSpec((MLA_TM, MLA_W), lambda i: (i, 0)),
        ],
        out_shape=[
            jax.ShapeDtypeStruct((N_TOK, MLA_HEADS * QK_PAD), BF16),
            jax.ShapeDtypeStruct((N_TOK, MLA_HEADS * QK_PAD), BF16),
            jax.ShapeDtypeStruct((N_TOK, MLA_W), BF16),
        ],
        compiler_params=_params(("arbitrary",)),
        name="mla_latent",
    )(proj, proj, proj, q_norm, kv_norm, wq_pad, wkv, cos, sin)


def _mla_context(proj_c, kv_norm, wkv):
    row = lambda w, off: pl.BlockSpec((MLA_TM, w), lambda i: (i, off // w))
    full = lambda a: pl.BlockSpec(a.shape, lambda i: (0, 0))
    return pl.pallas_call(
        _mla_context_kernel,
        grid=(N_CTX // MLA_TM,),
        in_specs=[row(MLA_KV_LORA, CTX_OFF_CKV), row(LANES, CTX_OFF_KPE), full(kv_norm), full(wkv)],
        out_specs=[
            pl.BlockSpec((MLA_TM, MLA_HEADS * QK_PAD), lambda i: (i, 0)),
            pl.BlockSpec((MLA_TM, MLA_W), lambda i: (i, 0)),
        ],
        out_shape=[
            jax.ShapeDtypeStruct((N_CTX, MLA_HEADS * QK_PAD), BF16),
            jax.ShapeDtypeStruct((N_CTX, MLA_W), BF16),
        ],
        compiler_params=_params(("arbitrary",)),
        name="mla_context",
    )(proj_c, proj_c, kv_norm, wkv)


ATT_TQ = 256


def _attention_kernel(q_ref, kc_ref, kl_ref, vc_ref, vl_ref, wo_ref, o_ref, wo_bf_ref, vext_ref):
    wo_bf_ref[...] = wo_ref[...].astype(BF16)

    @pl.when(pl.program_id(1) == 0)
    def _():
        lane = lax.broadcasted_iota(I32, (CTX_LEN + SEQ, MLA_DV), 1)
        ones_col = jnp.where(lane == 0, 1.0, 0.0).astype(BF16)
        for h in range(MLA_HEADS):
            dv = slice(h * MLA_DV, (h + 1) * MLA_DV)
            vext_ref[h, :CTX_LEN, :MLA_DV] = vc_ref[:, dv]
            vext_ref[h, CTX_LEN:, :MLA_DV] = vl_ref[:, dv]
            vext_ref[h, :, MLA_DV:] = ones_col

    for h in range(MLA_HEADS):
        qk = slice(h * QK_PAD, (h + 1) * QK_PAD)
        q = q_ref[:, qk]
        s_c = _dot_nt(q, kc_ref[:, qk])
        s_l = _dot_nt(q, kl_ref[:, qk])
        m = jnp.maximum(jnp.max(s_c, -1, keepdims=True), jnp.max(s_l, -1, keepdims=True))
        p_c = jnp.exp2(s_c - m).astype(BF16)
        p_l = jnp.exp2(s_l - m).astype(BF16)
        o = (jnp.dot(p_c, vext_ref[h, :CTX_LEN, :], preferred_element_type=F32)
             + jnp.dot(p_l, vext_ref[h, CTX_LEN:, :], preferred_element_type=F32))
        o_ref[:, h * MLA_DV:(h + 1) * MLA_DV] = (o[:, :MLA_DV] / o[:, MLA_DV:MLA_DV + 1]).astype(BF16)


def _attention(q, k_ctx, k_lat, v_ctx, v_lat, w_o):
    tiles = SEQ // ATT_TQ
    band = w_o.shape[0] // (BATCH * tiles)
    step = lambda b, i: (b * tiles + i, 0)
    return pl.pallas_call(
        _attention_kernel,
        grid=(BATCH, tiles),
        in_specs=[
            pl.BlockSpec((ATT_TQ, MLA_HEADS * QK_PAD), step),
            pl.BlockSpec((CTX_LEN, MLA_HEADS * QK_PAD), lambda b, i: (b, 0)),
            pl.BlockSpec((SEQ, MLA_HEADS * QK_PAD), lambda b, i: (b, 0)),
            pl.BlockSpec((CTX_LEN, MLA_W), lambda b, i: (b, 0)),
            pl.BlockSpec((SEQ, MLA_W), lambda b, i: (b, 0)),
            pl.BlockSpec((band, D_MODEL), step),
        ],
        out_specs=[pl.BlockSpec((ATT_TQ, MLA_W), step), pl.BlockSpec((band, D_MODEL), step)],
        out_shape=[jax.ShapeDtypeStruct((N_TOK, MLA_W), BF16), jax.ShapeDtypeStruct(w_o.shape, BF16)],
        scratch_shapes=[pltpu.VMEM((MLA_HEADS, CTX_LEN + SEQ, 2 * MLA_DV), BF16)],
        compiler_params=_params(("arbitrary", "arbitrary")),
        name="attention",
    )(q, k_ctx, k_lat, v_ctx, v_lat, w_o)


OUT_TM = 512
OUT_SUB = 256
ROUTE_E1, ROUTE_E2, ROUTE_R1, ROUTE_R2, ROUTE_W1, ROUTE_W2 = range(6)


def _outproj_kernel(ret_ref, att_ref, x_ref, wo_ref, gate_ref, shift_ref, scale_ref, lnw_ref, lnb_ref,
                    wr_ref, br_ref, h_ref, t_ref, route_ref, route_t_ref, count_ref, carry_ref, wsplit_ref):
    @pl.when(pl.program_id(0) == 0)
    def _():
        carry_ref[...] = jnp.zeros_like(carry_ref)
        w = wr_ref[...]
        w_hi = w.astype(BF16)
        wsplit_ref[:, :LANES] = w_hi
        wsplit_ref[:, LANES:] = (w - w_hi.astype(F32)).astype(BF16)

    subtiles = [slice(s * OUT_SUB, (s + 1) * OUT_SUB) for s in range(OUT_TM // OUT_SUB)]
    mixes = [jnp.dot(ret_ref[rows, :], wo_ref[:RET_W, :], preferred_element_type=F32)
             + jnp.dot(att_ref[rows, :], wo_ref[RET_W:, :], preferred_element_type=F32) for rows in subtiles]
    logits = [_outproj_norms(rows, mix, x_ref, gate_ref, shift_ref, scale_ref, lnw_ref, lnb_ref, br_ref, h_ref, t_ref,
                             wsplit_ref) for rows, mix in zip(subtiles, mixes)]
    _route(jnp.concatenate(logits, 0), carry_ref, count_ref, route_ref, route_t_ref)


def _outproj_norms(rows, mix, x_ref, gate_ref, shift_ref, scale_ref, lnw_ref, lnb_ref, br_ref, h_ref, t_ref,
                   wsplit_ref):
    h = _plain_norm(DEEPNORM_ALPHA * x_ref[rows, :] + gate_ref[0] * mix) * lnw_ref[...] + lnb_ref[...]
    h_ref[rows, :] = h
    t = _plain_norm(h) * (1.0 + scale_ref[0]) + shift_ref[0]
    t_ref[rows] = _to_token_tiles(t)

    t_hi = t.astype(BF16)
    t_lo = (t - t_hi.astype(F32)).astype(BF16)
    main = jnp.dot(t_hi, wsplit_ref[...], preferred_element_type=F32)
    corr = jnp.dot(t_lo, wsplit_ref[:, :LANES], preferred_element_type=F32)
    return main[:, :LANES] + (main[:, LANES:] + corr) + br_ref[...]


def _route(logits, carry_ref, count_ref, route_ref, route_t_ref):
    lane = lax.broadcasted_iota(I32, logits.shape, 1).astype(F32)
    neg = -jnp.inf
    big = float(LANES)

    def first_lane_of(mask):
        return jnp.min(jnp.where(mask, lane, big), -1, keepdims=True)

    is_group = lane < N_GROUPS
    gl = jnp.where(is_group, logits, neg)
    g_max = jnp.max(gl, -1, keepdims=True)
    g_idx = first_lane_of(is_group & (gl == g_max))
    g_prob = 1.0 / jnp.sum(jnp.where(is_group, jnp.exp(logits - g_max), 0.0), -1, keepdims=True)

    lo = N_GROUPS + g_idx * EXPERTS_PER_GROUP
    in_group = (lane >= lo) & (lane < lo + EXPERTS_PER_GROUP)
    el = jnp.where(in_group, logits, neg)
    v1 = jnp.max(el, -1, keepdims=True)
    i1 = first_lane_of(in_group & (el == v1))
    rest = in_group & (lane != i1)
    el2 = jnp.where(rest, logits, neg)
    v2 = jnp.max(el2, -1, keepdims=True)
    i2 = first_lane_of(rest & (el2 == v2))
    d = jnp.exp(v2 - v1)
    w1 = g_prob / (1.0 + d)
    w2 = g_prob * d / (1.0 + d)

    onehot = jnp.where((lane == i1) | (lane == i2), 1.0, 0.0)
    r = lax.broadcasted_iota(I32, (OUT_TM, OUT_TM), 0)
    c = lax.broadcasted_iota(I32, (OUT_TM, OUT_TM), 1)
    tri = jnp.where(c < r, 1.0, 0.0).astype(BF16)
    carry = carry_ref[0:1, :]
    before = jnp.dot(tri, onehot.astype(BF16), preferred_element_type=F32) + carry
    r1 = jnp.sum(jnp.where(lane == i1, before, 0.0), -1, keepdims=True)
    r2 = jnp.sum(jnp.where(lane == i2, before, 0.0), -1, keepdims=True)
    carry = carry + jnp.sum(onehot, 0, keepdims=True)
    carry_ref[...] = jnp.broadcast_to(carry, carry_ref.shape)
    count_ref[...] = jnp.broadcast_to(carry, count_ref.shape)

    rec = jnp.zeros_like(logits)
    for slot, val in ((ROUTE_E1, i1 - N_GROUPS), (ROUTE_E2, i2 - N_GROUPS), (ROUTE_R1, r1), (ROUTE_R2, r2),
                      (ROUTE_W1, w1), (ROUTE_W2, w2)):
        rec = jnp.where(lane == slot, val, rec)
    route_ref[...] = rec
    route_t_ref[...] = rec.T[:8, :]


def _outproj(ret, att, x2d, wo_bf, mod_rows, ln_w, ln_b, w_route, b_route):
    tiles_per_sample = SEQ // OUT_TM
    row = lambda w: pl.BlockSpec((OUT_TM, w), lambda i: (i, 0))
    full = lambda a: pl.BlockSpec(a.shape, lambda i: (0, 0))
    mod = lambda j: pl.BlockSpec((1, 1, D_MODEL), lambda i: ((i // tiles_per_sample) * 6 + j, 0, 0))
    return pl.pallas_call(
        _outproj_kernel,
        grid=(N_TOK // OUT_TM,),
        in_specs=[row(RET_W), row(MLA_W), row(D_MODEL), full(wo_bf), mod(2), mod(3), mod(4),
                  full(ln_w), full(ln_b), full(w_route), full(b_route)],
        out_specs=[row(D_MODEL), pl.BlockSpec((OUT_TM, TOKEN_SUB, LANES), lambda i: (i, 0, 0)), row(LANES),
                   pl.BlockSpec((8, OUT_TM), lambda i: (0, i)), pl.BlockSpec((8, LANES), lambda i: (0, 0))],
        out_shape=[
            jax.ShapeDtypeStruct((N_TOK, D_MODEL), F32),
            jax.ShapeDtypeStruct((N_TOK, TOKEN_SUB, LANES), BF16),
            jax.ShapeDtypeStruct((N_TOK, LANES), F32),
            jax.ShapeDtypeStruct((8, N_TOK), F32),
            jax.ShapeDtypeStruct((8, LANES), F32),
        ],
        scratch_shapes=[pltpu.VMEM((8, LANES), F32), pltpu.VMEM((D_MODEL, 2 * LANES), BF16)],
        compiler_params=_params(("arbitrary",)),
        name="outproj_route",
    )(ret, att, x2d, wo_bf, mod_rows, mod_rows, mod_rows, ln_w, ln_b, w_route, b_route)


DISPATCH_TM = 1024


def _dispatch_kernel(pos_ref, t_ref, xs_ref, sem):
    i = pl.program_id(0)

    def copy(slot, j):
        return pltpu.make_async_copy(t_ref.at[j], xs_ref.at[pos_ref[slot * N_TOK + i * DISPATCH_TM + j]], sem)

    for slot in range(2):
        lax.fori_loop(0, DISPATCH_TM, lambda j, c, slot=slot: (copy(slot, j).start(priority=slot), c)[1], 0,
                      unroll=8)
    for slot in range(2):
        pltpu.make_async_copy(t_ref, xs_ref.at[pl.ds(0, DISPATCH_TM)], sem).wait()


def _dispatch(pos, t_tiles):
    return pl.pallas_call(
        _dispatch_kernel,
        grid_spec=pltpu.PrefetchScalarGridSpec(
            num_scalar_prefetch=1,
            grid=(N_TOK // DISPATCH_TM,),
            in_specs=[pl.BlockSpec((DISPATCH_TM, TOKEN_SUB, LANES), lambda i, pos: (i, 0, 0))],
            out_specs=pl.BlockSpec(memory_space=pl.ANY),
            scratch_shapes=[pltpu.SemaphoreType.DMA(())],
        ),
        out_shape=jax.ShapeDtypeStruct((N_PAIRS, TOKEN_SUB, LANES), BF16),
        compiler_params=_params(("arbitrary",)),
        name="dispatch",
    )(pos, t_tiles)


def _experts_kernel(rend_ref, xs_ref, wg_ref, wu_ref, wd_ref, ys_ref, wgu_ref, wdn_ref, xbuf_ref, ybuf_ref,
                    xsem, ysem):
    e = pl.program_id(0)
    lo = jnp.where(e == 0, 0, rend_ref[jnp.maximum(e - 1, 0)])
    hi = rend_ref[e]

    def x_copy(g):
        return pltpu.make_async_copy(xs_ref.at[pl.ds(g * ROW_TILE, ROW_TILE)], xbuf_ref.at[g % 2], xsem.at[g % 2])

    def y_copy(g):
        return pltpu.make_async_copy(ybuf_ref.at[g % 2], ys_ref.at[pl.ds(g * ROW_TILE, ROW_TILE)], ysem.at[g % 2])

    @pl.when(e == 0)
    def _():
        x_copy(0).start()

    @pl.when(hi > lo)
    def _():
        wgu_ref[:, :D_EXPERT] = wg_ref[0].astype(BF16)
        wgu_ref[:, D_EXPERT:] = wu_ref[0].astype(BF16)
        wdn_ref[...] = wd_ref[0].astype(BF16)

        def tile(g, c):
            row0 = g * ROW_TILE
            owns_first_row = lo <= row0
            owns_last_row = hi >= row0 + ROW_TILE

            @pl.when(owns_first_row)
            def _():
                @pl.when(g + 1 < N_TILES)
                def _():
                    x_copy(g + 1).start()

                x_copy(g).wait()

            gu = jnp.dot(_from_token_tiles(xbuf_ref[g % 2]), wgu_ref[...], preferred_element_type=F32)
            hid = _silu(gu[:, :D_EXPERT]) * gu[:, D_EXPERT:]
            y = jnp.dot(hid.astype(BF16), wdn_ref[...], preferred_element_type=F32)

            @pl.when(owns_first_row)
            def _():
                @pl.when(g >= 2)
                def _():
                    y_copy(g - 2).wait()

                ybuf_ref[g % 2] = _to_token_tiles(y)

            @pl.when(jnp.logical_not(owns_first_row))
            def _():
                row = lax.broadcasted_iota(I32, y.shape, 0)
                earlier = _from_token_tiles(ybuf_ref[g % 2]).astype(F32)
                ybuf_ref[g % 2] = _to_token_tiles(jnp.where(row >= lo - row0, y, earlier))

            @pl.when(owns_last_row)
            def _():
                y_copy(g).start()

            return c

        lax.fori_loop(lo // ROW_TILE, (hi - 1) // ROW_TILE + 1, tile, 0)

    @pl.when(e == N_EXPERTS - 1)
    def _():
        y_copy(N_TILES - 2).wait()
        y_copy(N_TILES - 1).wait()


def _experts(row_end, xs, w_gate, w_up, w_down):
    w_gate = w_gate.reshape(N_EXPERTS, D_MODEL, D_EXPERT)
    w_up = w_up.reshape(N_EXPERTS, D_MODEL, D_EXPERT)
    w_down = w_down.reshape(N_EXPERTS, D_EXPERT, D_MODEL)
    expert = lambda e, rend: (e, 0, 0)
    tile_buf = pltpu.VMEM((2, ROW_TILE, TOKEN_SUB, LANES), BF16)
    return pl.pallas_call(
        _experts_kernel,
        grid_spec=pltpu.PrefetchScalarGridSpec(
            num_scalar_prefetch=1,
            grid=(N_EXPERTS,),
            in_specs=[
                pl.BlockSpec(memory_space=pl.ANY),
                pl.BlockSpec((1, D_MODEL, D_EXPERT), expert),
                pl.BlockSpec((1, D_MODEL, D_EXPERT), expert),
                pl.BlockSpec((1, D_EXPERT, D_MODEL), expert),
            ],
            out_specs=pl.BlockSpec(memory_space=pl.ANY),
            scratch_shapes=[pltpu.VMEM((D_MODEL, 2 * D_EXPERT), BF16), pltpu.VMEM((D_EXPERT, D_MODEL), BF16),
                            tile_buf, tile_buf, pltpu.SemaphoreType.DMA((2,)), pltpu.SemaphoreType.DMA((2,))],
        ),
        out_shape=jax.ShapeDtypeStruct((N_PAIRS, TOKEN_SUB, LANES), BF16),
        compiler_params=_params(("arbitrary",)),
        name="experts",
    )(row_end, xs, w_gate, w_up, w_down)


COMB_TM = 256


def _combine_kernel(pos_ref, ys_ref, h_ref, route_ref, gate_ref, lnw_ref, lnb_ref, o_ref, buf_ref, sem):
    i = pl.program_id(0)
    n = pl.num_programs(0)

    def copy(step, slot, pair, j):
        src = pos_ref[pair * N_TOK + step * COMB_TM + j]
        return pltpu.make_async_copy(ys_ref.at[src], buf_ref.at[slot, pair * COMB_TM + j], sem.at[slot])

    def start_all(step, slot):
        for pair in range(2):
            lax.fori_loop(0, COMB_TM, lambda j, c, pair=pair: (copy(step, slot, pair, j).start(priority=pair), c)[1], 0,
                          unroll=8)

    def wait_all(step, slot):
        pltpu.make_async_copy(ys_ref.at[pl.ds(0, 2 * COMB_TM)], buf_ref.at[slot], sem.at[slot]).wait()

    @pl.when(i == 0)
    def _():
        start_all(0, 0)

    slot = i % 2
    wait_all(i, slot)

    nxt = jnp.minimum(i + 1, n - 1)
    for pair in range(2):
        for j in range(COMB_TM):
            copy(nxt, 1 - slot, pair, j).start(priority=pair)

    y1 = _from_token_tiles(buf_ref[slot, :COMB_TM]).astype(F32)
    y2 = _from_token_tiles(buf_ref[slot, COMB_TM:]).astype(F32)
    y = route_ref[:, ROUTE_W1:ROUTE_W1 + 1] * y1 + route_ref[:, ROUTE_W2:ROUTE_W2 + 1] * y2
    z = DEEPNORM_ALPHA * h_ref[...] + gate_ref[0] * y
    o_ref[...] = _plain_norm(z) * lnw_ref[...] + lnb_ref[...]

    @pl.when(i == n - 1)
    def _():
        wait_all(i, 1 - slot)


def _combine(pos, ys, h, route, mod_rows, ln_w, ln_b):
    tiles_per_sample = SEQ // COMB_TM
    row = lambda w: pl.BlockSpec((COMB_TM, w), lambda i, pos: (i, 0))
    full = lambda a: pl.BlockSpec(a.shape, lambda i, pos: (0, 0))
    return pl.pallas_call(
        _combine_kernel,
        grid_spec=pltpu.PrefetchScalarGridSpec(
            num_scalar_prefetch=1,
            grid=(N_TOK // COMB_TM,),
            in_specs=[
                pl.BlockSpec(memory_space=pl.ANY), row(D_MODEL), row(LANES),
                pl.BlockSpec((1, 1, D_MODEL), lambda i, pos: ((i // tiles_per_sample) * 6 + 5, 0, 0)),
                full(ln_w), full(ln_b),
            ],
            out_specs=row(D_MODEL),
            scratch_shapes=[pltpu.VMEM((2, 2 * COMB_TM, TOKEN_SUB, LANES), BF16), pltpu.SemaphoreType.DMA((2,))],
        ),
        out_shape=jax.ShapeDtypeStruct((N_TOK, D_MODEL), F32),
        compiler_params=_params(("arbitrary",)),
        name="combine",
    )(pos, ys, h, route, mod_rows, ln_w, ln_b)


def _routing_tables(route_t, counts):
    cnt = counts[0, N_GROUPS:N_ROUTE].astype(I32)
    row_end = jnp.cumsum(cnt)
    e = route_t[ROUTE_E1:ROUTE_E2 + 1].astype(I32)
    rank = route_t[ROUTE_R1:ROUTE_R2 + 1].astype(I32)
    earlier = jnp.arange(N_EXPERTS, dtype=I32)[:, None, None] < e[None]
    base = jnp.sum(jnp.where(earlier, cnt[:, None, None], 0), 0)
    pos = (base + rank).reshape(-1)
    return pos, row_end


def kernel(x, c, ctx, c_ctx, w_ada, b_ada, w_in, ret_decay, ret_gn_w, mla_q_norm, mla_kv_norm, w_uq, w_ukv, w_o,
           ln1_w, ln1_b, router_group_w, router_group_b, router_expert_w, router_expert_b, expert_w_gate,
           expert_w_up, expert_w_down, ln2_w, ln2_b):
    x2d = x.reshape(N_TOK, D_MODEL)
    ctx2d = ctx.reshape(N_CTX, D_MODEL)

    cc = jnp.zeros((8, D_MODEL), F32).at[:BATCH].set(c).at[BATCH].set(c_ctx)
    mod = _ada(cc, w_ada[0], b_ada)
    mod_rows = mod.reshape(8 * 6, 1, D_MODEL)

    proj, w_in_blocks = _inproj_latent(x2d, mod_rows, jnp.swapaxes(w_in, 1, 2))
    proj_c = _inproj_context(ctx2d, mod_rows, w_in_blocks)

    cos_r, sin_r = _rope_tables(RET_DK)
    decay_rows = jnp.broadcast_to(ret_decay[0].reshape(2 * RET_HEADS, 1, 1), (2 * RET_HEADS, 1, LANES))
    ret = _retention(proj, proj_c, jnp.asarray(cos_r), jnp.asarray(sin_r), decay_rows, ret_gn_w)

    cos_m, sin_m = _rope_tables(MLA_ROPE)
    cos_m = np.concatenate([cos_m, np.ones_like(cos_m)], 1)
    sin_m = np.concatenate([sin_m, np.zeros_like(sin_m)], 1)
    wq = w_uq[0].reshape(MLA_Q_LORA, MLA_HEADS, MLA_DQ)
    wq_pad = jnp.pad(wq, ((0, 0), (0, 0), (0, QK_PAD - MLA_DQ))).reshape(MLA_Q_LORA, MLA_HEADS * QK_PAD).astype(BF16)
    wkv = w_ukv[0].astype(BF16)
    q, k_lat, v_lat = _mla_latent(proj, mla_q_norm, mla_kv_norm, wq_pad, wkv, jnp.asarray(cos_m), jnp.asarray(sin_m))
    k_ctx, v_ctx = _mla_context(proj_c, mla_kv_norm, wkv)
    att, w_o_bf = _attention(q, k_ctx, k_lat, v_ctx, v_lat, w_o[0])

    w_route = jnp.concatenate(
        [router_group_w[0], router_expert_w[0].transpose(1, 0, 2).reshape(D_MODEL, N_EXPERTS),
         jnp.zeros((D_MODEL, LANES - N_ROUTE), F32)], 1)
    b_route = jnp.concatenate(
        [router_group_b[0], router_expert_b[0].reshape(N_EXPERTS), jnp.zeros((LANES - N_ROUTE,), F32)])[None]
    h, t_tiles, route, route_t, counts = _outproj(ret, att, x2d, w_o_bf, mod_rows, ln1_w, ln1_b,
                                                   w_route, b_route)

    pos, row_end = _routing_tables(route_t, counts)
    xs = _dispatch(pos, t_tiles)
    ys = _experts(row_end, xs, expert_w_gate[0], expert_w_up[0], expert_w_down[0])
    out = _combine(pos, ys, h, route, mod_rows, ln2_w, ln2_b)
    return out.reshape(BATCH, SEQ, D_MODEL)
```

```python
import numpy as np
import jax
import jax.numpy as jnp
from jax import lax
from jax.experimental import pallas as pl
from jax.experimental.pallas import tpu as pltpu

F32 = jnp.float32
BF16 = jnp.bfloat16
I32 = jnp.int32

D_MODEL = 2048
BATCH = 4
SEQ = 2048
GRID_W = 64
CTX_LEN = 256
N_TOK = BATCH * SEQ
N_CTX = BATCH * CTX_LEN

RET_HEADS = 8
RET_DK = 128
RET_DV = 128
RET_W = RET_HEADS * RET_DV
CHUNK = 128
N_CHUNKS = SEQ // CHUNK

MLA_HEADS = 8
MLA_Q_LORA = 512
MLA_KV_LORA = 256
MLA_NOPE = 128
MLA_ROPE = 64
MLA_DV = 128
MLA_W = MLA_HEADS * MLA_DV
MLA_DQ = MLA_NOPE + MLA_ROPE
QK_PAD = 256

IN_SIZES = (RET_HEADS * RET_DK, RET_HEADS * RET_DK, RET_W, RET_W, MLA_Q_LORA, MLA_KV_LORA, MLA_ROPE)
IN_W = sum(IN_SIZES)
OFF_Q, OFF_K, OFF_V, OFF_G, OFF_CQ, OFF_CKV, OFF_KPE = (int(v) for v in np.cumsum((0,) + IN_SIZES[:-1]))

N_GROUPS = 4
EXPERTS_PER_GROUP = 8
N_EXPERTS = N_GROUPS * EXPERTS_PER_GROUP
D_EXPERT = 512
N_ROUTE = N_GROUPS + N_EXPERTS

LOG2_E = float(np.log2(np.e))
ROPE_BASE = 10000.0
EPS = 1e-6
DEPTH = 1
DEEPNORM_ALPHA = (2.0 * DEPTH) ** 0.25

LANES = 128
ROW_TILE = 256
N_PAIRS = 2 * N_TOK
N_TILES = N_PAIRS // ROW_TILE
TOKEN_SUB = D_MODEL // LANES

VMEM_LIMIT = 56 * 1024 * 1024


def _params(sem, vmem=VMEM_LIMIT):
    return pltpu.CompilerParams(dimension_semantics=sem, vmem_limit_bytes=vmem)


def _silu(x):
    return x * (1.0 / (1.0 + jnp.exp(-x)))


def _plain_norm(x):
    mu = jnp.mean(x, -1, keepdims=True)
    xc = x - mu
    var = jnp.mean(xc * xc, -1, keepdims=True)
    return xc * lax.rsqrt(var + EPS)


def _to_token_tiles(x):
    return x.astype(BF16).reshape(x.shape[0], TOKEN_SUB, LANES)


def _from_token_tiles(x):
    return x.reshape(x.shape[0], D_MODEL)


ADA_TN = 1024


ADA_BANDS = 4


def _ada_kernel(cc_ref, *refs):
    w_refs, b_ref, o_ref = refs[:ADA_BANDS], refs[ADA_BANDS], refs[ADA_BANDS + 1]
    s = _silu(cc_ref[...])
    band = D_MODEL // ADA_BANDS
    acc = b_ref[...]
    for q, w_ref in enumerate(w_refs):
        acc = acc + jnp.dot(s[:, q * band:(q + 1) * band], w_ref[...], preferred_element_type=F32,
                            precision=lax.Precision.HIGHEST)
    o_ref[...] = acc


def _ada(cc, w_ada, b_ada):
    n = w_ada.shape[1]
    band = D_MODEL // ADA_BANDS
    return pl.pallas_call(
        _ada_kernel,
        grid=(n // ADA_TN,),
        in_specs=[pl.BlockSpec((8, D_MODEL), lambda j: (0, 0))]
        + [pl.BlockSpec((band, ADA_TN), lambda j, q=q: (q, j)) for q in range(ADA_BANDS)]
        + [pl.BlockSpec((1, ADA_TN), lambda j: (0, j))],
        out_specs=pl.BlockSpec((8, ADA_TN), lambda j: (0, j)),
        out_shape=jax.ShapeDtypeStruct((8, n), F32),
        compiler_params=_params(("arbitrary",)),
        name="ada",
    )(cc, *([w_ada] * ADA_BANDS), b_ada)


INPROJ_TM = 1024
INPROJ_TN = 1024
assert OFF_K == INPROJ_TN and OFF_V == 2 * INPROJ_TN and OFF_CQ == 4 * INPROJ_TN
CTX_OFF_K, CTX_OFF_V = 0, INPROJ_TN
CTX_OFF_CKV = 2 * INPROJ_TN + (OFF_CKV - OFF_CQ)
CTX_OFF_KPE = 2 * INPROJ_TN + (OFF_KPE - OFF_CQ)


INPROJ_SUB = 256
INPROJ_BLOCKS = -(-IN_W // INPROJ_TN)


def _inproj_products(x_ref, shift_ref, scale_ref, w, o_ref, xn_ref):
    first = pl.program_id(1) == 0

    @pl.when(first)
    def _():
        for s in range(INPROJ_TM // INPROJ_SUB):
            rows = slice(s * INPROJ_SUB, (s + 1) * INPROJ_SUB)
            y = (_plain_norm(x_ref[rows, :]) * (1.0 + scale_ref[0]) + shift_ref[0]).astype(BF16)
            xn_ref[rows, :] = y
            o_ref[rows, :] = _dot_nt(y, w[...]).astype(BF16)

    @pl.when(jnp.logical_not(first))
    def _():
        o_ref[...] = _dot_nt(xn_ref[...], w[...]).astype(BF16)


def _inproj_latent_kernel(x_ref, shift_ref, scale_ref, wf_ref, o_ref, wbf_ref, xn_ref, wbuf_ref, in_sem, out_sem):
    i = pl.program_id(0)
    k = pl.program_id(1)
    t = i * INPROJ_BLOCKS + k
    slot = t % 2
    n_steps = pl.num_programs(0) * INPROJ_BLOCKS

    def out_copy(block, s):
        return pltpu.make_async_copy(wbuf_ref.at[s], wbf_ref.at[block], out_sem.at[s])

    def in_copy(block, s):
        return pltpu.make_async_copy(wbf_ref.at[block], wbuf_ref.at[s], in_sem.at[s])

    @pl.when(i == 0)
    def _():
        @pl.when(k >= 2)
        def _():
            out_copy(k - 2, slot).wait()

        col = lax.broadcasted_iota(I32, (INPROJ_TN, D_MODEL), 0)
        wbuf_ref[slot] = jnp.where(col < IN_W - k * INPROJ_TN, wf_ref[0], 0.0).astype(BF16)
        out_copy(k, slot).start()

    @pl.when((t + 1 >= INPROJ_BLOCKS) & (t + 1 < n_steps))
    def _():
        @pl.when(t == INPROJ_BLOCKS - 1)
        def _():
            out_copy(INPROJ_BLOCKS - 2, 1 - slot).wait()

        @pl.when(t == INPROJ_BLOCKS)
        def _():
            out_copy(INPROJ_BLOCKS - 1, 1 - slot).wait()

        in_copy((k + 1) % INPROJ_BLOCKS, 1 - slot).start()

    @pl.when(i > 0)
    def _():
        in_copy(k, slot).wait()

    _inproj_products(x_ref, shift_ref, scale_ref, wbuf_ref.at[slot], o_ref, xn_ref)


def _inproj_context_kernel(x_ref, shift_ref, scale_ref, w_ref, o_ref, xn_ref):
    _inproj_products(x_ref, shift_ref, scale_ref, w_ref.at[0], o_ref, xn_ref)


def _inproj_mod_map(j, tiles_per_sample, sample_row0):
    return lambda i, k: ((sample_row0 + i // tiles_per_sample) * 6 + j, 0, 0)


def _inproj_latent(x2d, mod_rows, w_in_t):
    assert INPROJ_BLOCKS % 2 == 1
    last = INPROJ_BLOCKS - 1
    return pl.pallas_call(
        _inproj_latent_kernel,
        grid=(N_TOK // INPROJ_TM, INPROJ_BLOCKS),
        in_specs=[
            pl.BlockSpec((INPROJ_TM, D_MODEL), lambda i, k: (i, 0)),
            pl.BlockSpec((1, 1, D_MODEL), _inproj_mod_map(0, SEQ // INPROJ_TM, 0)),
            pl.BlockSpec((1, 1, D_MODEL), _inproj_mod_map(1, SEQ // INPROJ_TM, 0)),
            pl.BlockSpec((1, INPROJ_TN, D_MODEL), lambda i, k: (0, jnp.where(i == 0, k, last), 0)),
        ],
        out_specs=[pl.BlockSpec((INPROJ_TM, INPROJ_TN), lambda i, k: (i, k)), pl.BlockSpec(memory_space=pl.ANY)],
        out_shape=[jax.ShapeDtypeStruct((N_TOK, IN_W), BF16),
                   jax.ShapeDtypeStruct((INPROJ_BLOCKS, INPROJ_TN, D_MODEL), BF16)],
        scratch_shapes=[pltpu.VMEM((INPROJ_TM, D_MODEL), BF16), pltpu.VMEM((2, INPROJ_TN, D_MODEL), BF16),
                        pltpu.SemaphoreType.DMA((2,)), pltpu.SemaphoreType.DMA((2,))],
        compiler_params=_params(("arbitrary", "arbitrary")),
        name="inproj",
    )(x2d, mod_rows, mod_rows, w_in_t)


def _inproj_context(ctx2d, mod_rows, w_blocks):
    last = INPROJ_BLOCKS - 1
    out_w = 2 * INPROJ_TN + (IN_W - last * INPROJ_TN)
    return pl.pallas_call(
        _inproj_context_kernel,
        grid=(N_CTX // INPROJ_TM, 3),
        in_specs=[
            pl.BlockSpec((INPROJ_TM, D_MODEL), lambda i, k: (i, 0)),
            pl.BlockSpec((1, 1, D_MODEL), _inproj_mod_map(0, N_CTX // INPROJ_TM, BATCH)),
            pl.BlockSpec((1, 1, D_MODEL), _inproj_mod_map(1, N_CTX // INPROJ_TM, BATCH)),
            pl.BlockSpec((1, INPROJ_TN, D_MODEL), lambda i, k: (jnp.where(k == 2, last, k + 1), 0, 0)),
        ],
        out_specs=pl.BlockSpec((INPROJ_TM, INPROJ_TN), lambda i, k: (i, k)),
        out_shape=jax.ShapeDtypeStruct((N_CTX, out_w), BF16),
        scratch_shapes=[pltpu.VMEM((INPROJ_TM, D_MODEL), BF16)],
        compiler_params=_params(("arbitrary", "arbitrary")),
        name="inproj_ctx",
    )(ctx2d, mod_rows, mod_rows, w_blocks)


def _rope_tables(width):
    half = width // 2
    quarter = half // 2
    inv_freq = ROPE_BASE ** (-np.arange(0, half, 2, dtype=np.float64) / half)
    t = np.arange(SEQ)
    cos_parts, sin_parts = [], []
    for pos in (t // GRID_W, t % GRID_W):
        ang = pos[:, None].astype(np.float64) * inv_freq[None, :]
        c, s = np.cos(ang), np.sin(ang)
        cos_parts += [c, c]
        sin_parts += [-s, s]
    assert cos_parts[0].shape[1] == quarter
    return (np.concatenate(cos_parts, 1).astype(np.float32), np.concatenate(sin_parts, 1).astype(np.float32))


def _rope(x, cos, sin, quarter):
    lane = lax.broadcasted_iota(I32, x.shape, 1)
    first = (lane % (2 * quarter)) < quarter
    swapped = jnp.where(first, pltpu.roll(x, LANES - quarter, 1), pltpu.roll(x, quarter, 1))
    return x * cos + swapped * sin


def _dot_tn(a, b):
    return lax.dot_general(a, b, (((0,), (0,)), ((), ())), preferred_element_type=F32)


def _dot_nt(a, b):
    return lax.dot_general(a, b, (((1,), (1,)), ((), ())), preferred_element_type=F32)


RET_HG = 4


def _retention_kernel(q_ref, k_ref, v_ref, g_ref, kc_ref, vc_ref, cos_ref, sin_ref, df_ref, db_ref, gn_ref,
                      o_ref, qs_ref, ks_ref, st_ref):
    for hh in range(RET_HG):
        _retention_head(hh, slice(hh * LANES, (hh + 1) * LANES), q_ref, k_ref, v_ref, g_ref, kc_ref, vc_ref,
                        cos_ref, sin_ref, df_ref, db_ref, gn_ref, o_ref, qs_ref, ks_ref, st_ref)


def _retention_head(hh, hs, q_ref, k_ref, v_ref, g_ref, kc_ref, vc_ref, cos_ref, sin_ref, df_ref, db_ref, gn_ref,
                    o_ref, qs_ref, ks_ref, st_ref):
    k_scale = RET_DK ** -0.5
    lgf = jax.nn.log_sigmoid(df_ref[hh])
    lgb = jax.nn.log_sigmoid(db_ref[hh])

    cos = cos_ref[...]
    sin = sin_ref[...]
    qs_ref[hh] = _rope(q_ref[:, hs].astype(F32), cos, sin, RET_DK // 4).astype(BF16)
    ks_ref[hh] = _rope(k_ref[:, hs].astype(F32), cos, sin, RET_DK // 4) * k_scale

    rowi = lax.broadcasted_iota(I32, (CHUNK, LANES), 0).astype(F32)
    coli = lax.broadcasted_iota(I32, (CHUNK, LANES), 1).astype(F32)
    diff = rowi - coli
    decay = jnp.exp(jnp.where(diff >= 0, lgf * diff, -lgb * diff)) * jnp.where(diff == 0, 2.0, 1.0)
    zeta_f = jnp.exp(lgf * (CHUNK - 1.0 - rowi))
    eta_b = jnp.exp(lgb * rowi)
    xi_f = jnp.exp(lgf * (rowi + 1.0))
    xi_b = jnp.exp(lgb * (CHUNK - rowi))
    cdec_f = jnp.exp(lgf * float(CHUNK))
    cdec_b = jnp.exp(lgb * float(CHUNK))

    crow = lax.broadcasted_iota(I32, (CTX_LEN, LANES), 0).astype(F32)
    kc = kc_ref[:, hs].astype(F32) * k_scale
    vc = vc_ref[:, hs]
    s_f = _dot_tn((kc * jnp.exp(lgf * (CTX_LEN - 1.0 - crow))).astype(BF16), vc)
    s_b = _dot_tn((kc * jnp.exp(lgb * crow)).astype(BF16), vc)

    upd_f, upd_b = [], []
    for i in range(N_CHUNKS):
        rows = pl.ds(i * CHUNK, CHUNK)
        kch = ks_ref[hh, rows, :]
        vch = v_ref[rows, hs]
        upd_f.append(_dot_tn((kch * zeta_f).astype(BF16), vch))
        upd_b.append(_dot_tn((kch * eta_b).astype(BF16), vch))
    state = s_f
    for i in range(N_CHUNKS):
        st_ref[hh, i, :, :RET_DV] = state.astype(BF16)
        state = cdec_f * state + upd_f[i]
    state = s_b
    for i in reversed(range(N_CHUNKS)):
        st_ref[hh, i, :, RET_DV:] = state.astype(BF16)
        state = cdec_b * state + upd_b[i]

    gn_w = gn_ref[:, hs]
    for i in range(N_CHUNKS):
        rows = pl.ds(i * CHUNK, CHUNK)
        qch = qs_ref[hh, rows, :]
        scores = _dot_nt(qch, ks_ref[hh, rows, :].astype(BF16)) * decay
        o = jnp.dot(scores.astype(BF16), v_ref[rows, hs], preferred_element_type=F32)
        cross = jnp.dot(qch, st_ref[hh, i], preferred_element_type=F32)
        o = o + xi_f * cross[:, :RET_DV] + xi_b * cross[:, RET_DV:]
        y = _plain_norm(o) * gn_w
        o_ref[rows, hs] = (_silu(g_ref[rows, hs].astype(F32)) * y).astype(BF16)


def _retention(proj, proj_c, cos, sin, decay_rows, gn_w):
    width = RET_HG * LANES
    groups = RET_HEADS // RET_HG
    blk = lambda off: pl.BlockSpec((SEQ, width), lambda b, hg: (b, off // width + hg))
    blk_c = lambda off: pl.BlockSpec((CTX_LEN, width), lambda b, hg: (b, off // width + hg))
    table = pl.BlockSpec((SEQ, LANES), lambda b, hg: (0, 0))
    return pl.pallas_call(
        _retention_kernel,
        grid=(BATCH, groups),
        in_specs=[
            blk(OFF_Q), blk(OFF_K), blk(OFF_V), blk(OFF_G), blk_c(CTX_OFF_K), blk_c(CTX_OFF_V), table, table,
            pl.BlockSpec((RET_HG, 1, LANES), lambda b, hg: (hg, 0, 0)),
            pl.BlockSpec((RET_HG, 1, LANES), lambda b, hg: (groups + hg, 0, 0)),
            pl.BlockSpec((1, width), lambda b, hg: (0, hg)),
        ],
        out_specs=pl.BlockSpec((SEQ, width), lambda b, hg: (b, hg)),
        out_shape=jax.ShapeDtypeStruct((N_TOK, RET_W), BF16),
        scratch_shapes=[
            pltpu.VMEM((RET_HG, SEQ, RET_DK), BF16),
            pltpu.VMEM((RET_HG, SEQ, RET_DK), F32),
            pltpu.VMEM((RET_HG, N_CHUNKS, RET_DK, 2 * RET_DV), BF16),
        ],
        compiler_params=_params(("arbitrary", "arbitrary")),
        name="retention",
    )(proj, proj, proj, proj, proj_c, proj_c, cos, sin, decay_rows, decay_rows, gn_w)


MLA_TM = 512


def _rms_norm(x, w):
    return x * lax.rsqrt(jnp.mean(x * x, -1, keepdims=True) + EPS) * w


def _mla_kv(ckv_ref, kpe_ref, kvn_ref, wkv_ref, cos_ref, sin_ref, k_ref, v_ref, rotate):
    ckv = _rms_norm(ckv_ref[...].astype(F32), kvn_ref[...]).astype(BF16)
    kv = jnp.dot(ckv, wkv_ref[...], preferred_element_type=F32)
    lane = lax.broadcasted_iota(I32, (ckv.shape[0], LANES), 1)
    kpe = jnp.where(lane < MLA_ROPE, kpe_ref[...].astype(F32), 0.0)
    if rotate:
        kpe = _rope(kpe, cos_ref[...], sin_ref[...], MLA_ROPE // 4)
    kpe = kpe.astype(BF16)
    for h in range(MLA_HEADS):
        k_ref[:, h * QK_PAD:h * QK_PAD + MLA_NOPE] = kv[:, 2 * h * LANES:(2 * h + 1) * LANES].astype(BF16)
        k_ref[:, h * QK_PAD + MLA_NOPE:(h + 1) * QK_PAD] = kpe
        v_ref[:, h * MLA_DV:(h + 1) * MLA_DV] = kv[:, (2 * h + 1) * LANES:(2 * h + 2) * LANES].astype(BF16)


def _mla_latent_kernel(cq_ref, ckv_ref, kpe_ref, qn_ref, kvn_ref, wq_ref, wkv_ref, cos_ref, sin_ref,
                       q_ref, k_ref, v_ref):
    cq = _rms_norm(cq_ref[...].astype(F32), qn_ref[...]).astype(BF16)
    q = jnp.dot(cq, wq_ref[...], preferred_element_type=F32)
    cos = cos_ref[...]
    sin = sin_ref[...]
    scale = MLA_DQ ** -0.5 * LOG2_E
    for h in range(MLA_HEADS):
        lo = h * QK_PAD
        q_ref[:, lo:lo + MLA_NOPE] = (q[:, lo:lo + MLA_NOPE] * scale).astype(BF16)
        qpe = _rope(q[:, lo + MLA_NOPE:lo + QK_PAD], cos, sin, MLA_ROPE // 4)
        q_ref[:, lo + MLA_NOPE:lo + QK_PAD] = (qpe * scale).astype(BF16)
    _mla_kv(ckv_ref, kpe_ref, kvn_ref, wkv_ref, cos_ref, sin_ref, k_ref, v_ref, rotate=True)


def _mla_context_kernel(ckv_ref, kpe_ref, kvn_ref, wkv_ref, k_ref, v_ref):
    _mla_kv(ckv_ref, kpe_ref, kvn_ref, wkv_ref, None, None, k_ref, v_ref, rotate=False)


def _mla_latent(proj, q_norm, kv_norm, wq_pad, wkv, cos, sin):
    row = lambda w, off: pl.BlockSpec((MLA_TM, w), lambda i: (i, off // w))
    full = lambda a: pl.BlockSpec(a.shape, lambda i: (0, 0))
    table = pl.BlockSpec((MLA_TM, LANES), lambda i: (i % (SEQ // MLA_TM), 0))
    return pl.pallas_call(
        _mla_latent_kernel,
        grid=(N_TOK // MLA_TM,),
        in_specs=[row(MLA_Q_LORA, OFF_CQ), row(MLA_KV_LORA, OFF_CKV), row(LANES, OFF_KPE),
                  full(q_norm), full(kv_norm), full(wq_pad), full(wkv), table, table],
        out_specs=[
            pl.BlockSpec((MLA_TM, MLA_HEADS * QK_PAD), lambda i: (i, 0)),
            pl.BlockSpec((MLA_TM, MLA_HEADS * QK_PAD), lambda i: (i, 0)),
            pl.BlockSpec((MLA_TM, MLA_W), lambda i: (i, 0)),
        ],
        out_shape=[
            jax.ShapeDtypeStruct((N_TOK, MLA_HEADS * QK_PAD), BF16),
            jax.ShapeDtypeStruct((N_TOK, MLA_HEADS * QK_PAD), BF16),
            jax.ShapeDtypeStruct((N_TOK, MLA_W), BF16),
        ],
        compiler_params=_params(("arbitrary",)),
        name="mla_latent",
    )(proj, proj, proj, q_norm, kv_norm, wq_pad, wkv, cos, sin)


def _mla_context(proj_c, kv_norm, wkv):
    row = lambda w, off: pl.BlockSpec((MLA_TM, w), lambda i: (i, off // w))
    full = lambda a: pl.BlockSpec(a.shape, lambda i: (0, 0))
    return pl.pallas_call(
        _mla_context_kernel,
        grid=(N_CTX // MLA_TM,),
        in_specs=[row(MLA_KV_LORA, CTX_OFF_CKV), row(LANES, CTX_OFF_KPE), full(kv_norm), full(wkv)],
        out_specs=[
            pl.BlockSpec((MLA_TM, MLA_HEADS * QK_PAD), lambda i: (i, 0)),
            pl.BlockSpec((MLA_TM, MLA_W), lambda i: (i, 0)),
        ],
        out_shape=[
            jax.ShapeDtypeStruct((N_CTX, MLA_HEADS * QK_PAD), BF16),
            jax.ShapeDtypeStruct((N_CTX, MLA_W), BF16),
        ],
        compiler_params=_params(("arbitrary",)),
        name="mla_context",
    )(proj_c, proj_c, kv_norm, wkv)


ATT_TQ = 512


def _attention_kernel(q_ref, kc_ref, kl_ref, vc_ref, vl_ref, wo_ref, o_ref, wo_bf_ref, vext_ref):
    wo_bf_ref[...] = wo_ref[...].astype(BF16)

    @pl.when(pl.program_id(1) == 0)
    def _():
        lane = lax.broadcasted_iota(I32, (CTX_LEN + SEQ, MLA_DV), 1)
        ones_col = jnp.where(lane == 0, 1.0, 0.0).astype(BF16)
        for h in range(MLA_HEADS):
            dv = slice(h * MLA_DV, (h + 1) * MLA_DV)
            vext_ref[h, :CTX_LEN, :MLA_DV] = vc_ref[:, dv]
            vext_ref[h, CTX_LEN:, :MLA_DV] = vl_ref[:, dv]
            vext_ref[h, :, MLA_DV:] = ones_col

    for h in range(MLA_HEADS):
        qk = slice(h * QK_PAD, (h + 1) * QK_PAD)
        q = q_ref[:, qk]
        s_c = _dot_nt(q, kc_ref[:, qk])
        s_l = _dot_nt(q, kl_ref[:, qk])
        m = jnp.maximum(jnp.max(s_c, -1, keepdims=True), jnp.max(s_l, -1, keepdims=True))
        p_c = jnp.exp2(s_c - m).astype(BF16)
        p_l = jnp.exp2(s_l - m).astype(BF16)
        o = (jnp.dot(p_c, vext_ref[h, :CTX_LEN, :], preferred_element_type=F32)
             + jnp.dot(p_l, vext_ref[h, CTX_LEN:, :], preferred_element_type=F32))
        o_ref[:, h * MLA_DV:(h + 1) * MLA_DV] = (o[:, :MLA_DV] / o[:, MLA_DV:MLA_DV + 1]).astype(BF16)


def _attention(q, k_ctx, k_lat, v_ctx, v_lat, w_o):
    tiles = SEQ // ATT_TQ
    band = w_o.shape[0] // (BATCH * tiles)
    step = lambda b, i: (b * tiles + i, 0)
    return pl.pallas_call(
        _attention_kernel,
        grid=(BATCH, tiles),
        in_specs=[
            pl.BlockSpec((ATT_TQ, MLA_HEADS * QK_PAD), step),
            pl.BlockSpec((CTX_LEN, MLA_HEADS * QK_PAD), lambda b, i: (b, 0)),
            pl.BlockSpec((SEQ, MLA_HEADS * QK_PAD), lambda b, i: (b, 0)),
            pl.BlockSpec((CTX_LEN, MLA_W), lambda b, i: (b, 0)),
            pl.BlockSpec((SEQ, MLA_W), lambda b, i: (b, 0)),
            pl.BlockSpec((band, D_MODEL), step),
        ],
        out_specs=[pl.BlockSpec((ATT_TQ, MLA_W), step), pl.BlockSpec((band, D_MODEL), step)],
        out_shape=[jax.ShapeDtypeStruct((N_TOK, MLA_W), BF16), jax.ShapeDtypeStruct(w_o.shape, BF16)],
        scratch_shapes=[pltpu.VMEM((MLA_HEADS, CTX_LEN + SEQ, 2 * MLA_DV), BF16)],
        compiler_params=_params(("arbitrary", "arbitrary")),
        name="attention",
    )(q, k_ctx, k_lat, v_ctx, v_lat, w_o)


OUT_TM = 512
OUT_SUB = 256
ROUTE_E1, ROUTE_E2, ROUTE_R1, ROUTE_R2, ROUTE_W1, ROUTE_W2 = range(6)


def _outproj_kernel(ret_ref, att_ref, x_ref, wo_ref, gate_ref, shift_ref, scale_ref, lnw_ref, lnb_ref,
                    wr_ref, br_ref, h_ref, t_ref, route_ref, route_t_ref, count_ref, carry_ref, wsplit_ref):
    @pl.when(pl.program_id(0) == 0)
    def _():
        carry_ref[...] = jnp.zeros_like(carry_ref)
        w = wr_ref[...]
        w_hi = w.astype(BF16)
        wsplit_ref[:, :LANES] = w_hi
        wsplit_ref[:, LANES:] = (w - w_hi.astype(F32)).astype(BF16)

    subtiles = [slice(s * OUT_SUB, (s + 1) * OUT_SUB) for s in range(OUT_TM // OUT_SUB)]
    mixes = [jnp.dot(ret_ref[rows, :], wo_ref[:RET_W, :], preferred_element_type=F32)
             + jnp.dot(att_ref[rows, :], wo_ref[RET_W:, :], preferred_element_type=F32) for rows in subtiles]
    logits = [_outproj_norms(rows, mix, x_ref, gate_ref, shift_ref, scale_ref, lnw_ref, lnb_ref, br_ref, h_ref, t_ref,
                             wsplit_ref) for rows, mix in zip(subtiles, mixes)]
    _route(jnp.concatenate(logits, 0), carry_ref, count_ref, route_ref, route_t_ref)


def _outproj_norms(rows, mix, x_ref, gate_ref, shift_ref, scale_ref, lnw_ref, lnb_ref, br_ref, h_ref, t_ref,
                   wsplit_ref):
    h = _plain_norm(DEEPNORM_ALPHA * x_ref[rows, :] + gate_ref[0] * mix) * lnw_ref[...] + lnb_ref[...]
    h_ref[rows, :] = h
    t = _plain_norm(h) * (1.0 + scale_ref[0]) + shift_ref[0]
    t_ref[rows] = _to_token_tiles(t)

    t_hi = t.astype(BF16)
    t_lo = (t - t_hi.astype(F32)).astype(BF16)
    main = jnp.dot(t_hi, wsplit_ref[...], preferred_element_type=F32)
    corr = jnp.dot(t_lo, wsplit_ref[:, :LANES], preferred_element_type=F32)
    return main[:, :LANES] + (main[:, LANES:] + corr) + br_ref[...]


def _route(logits, carry_ref, count_ref, route_ref, route_t_ref):
    lane = lax.broadcasted_iota(I32, logits.shape, 1).astype(F32)
    neg = -jnp.inf
    big = float(LANES)

    def first_lane_of(mask):
        return jnp.min(jnp.where(mask, lane, big), -1, keepdims=True)

    is_group = lane < N_GROUPS
    gl = jnp.where(is_group, logits, neg)
    g_max = jnp.max(gl, -1, keepdims=True)
    g_idx = first_lane_of(is_group & (gl == g_max))
    g_prob = 1.0 / jnp.sum(jnp.where(is_group, jnp.exp(logits - g_max), 0.0), -1, keepdims=True)

    lo = N_GROUPS + g_idx * EXPERTS_PER_GROUP
    in_group = (lane >= lo) & (lane < lo + EXPERTS_PER_GROUP)
    el = jnp.where(in_group, logits, neg)
    v1 = jnp.max(el, -1, keepdims=True)
    i1 = first_lane_of(in_group & (el == v1))
    rest = in_group & (lane != i1)
    el2 = jnp.where(rest, logits, neg)
    v2 = jnp.max(el2, -1, keepdims=True)
    i2 = first_lane_of(rest & (el2 == v2))
    d = jnp.exp(v2 - v1)
    w1 = g_prob / (1.0 + d)
    w2 = g_prob * d / (1.0 + d)

    onehot = jnp.where((lane == i1) | (lane == i2), 1.0, 0.0)
    r = lax.broadcasted_iota(I32, (OUT_TM, OUT_TM), 0)
    c = lax.broadcasted_iota(I32, (OUT_TM, OUT_TM), 1)
    tri = jnp.where(c < r, 1.0, 0.0).astype(BF16)
    carry = carry_ref[0:1, :]
    before = jnp.dot(tri, onehot.astype(BF16), preferred_element_type=F32) + carry
    r1 = jnp.sum(jnp.where(lane == i1, before, 0.0), -1, keepdims=True)
    r2 = jnp.sum(jnp.where(lane == i2, before, 0.0), -1, keepdims=True)
    carry = carry + jnp.sum(onehot, 0, keepdims=True)
    carry_ref[...] = jnp.broadcast_to(carry, carry_ref.shape)
    count_ref[...] = jnp.broadcast_to(carry, count_ref.shape)

    rec = jnp.zeros_like(logits)
    for slot, val in ((ROUTE_E1, i1 - N_GROUPS), (ROUTE_E2, i2 - N_GROUPS), (ROUTE_R1, r1), (ROUTE_R2, r2),
                      (ROUTE_W1, w1), (ROUTE_W2, w2)):
        rec = jnp.where(lane == slot, val, rec)
    route_ref[...] = rec
    route_t_ref[...] = rec.T[:8, :]


def _outproj(ret, att, x2d, wo_bf, mod_rows, ln_w, ln_b, w_route, b_route):
    tiles_per_sample = SEQ // OUT_TM
    row = lambda w: pl.BlockSpec((OUT_TM, w), lambda i: (i, 0))
    full = lambda a: pl.BlockSpec(a.shape, lambda i: (0, 0))
    mod = lambda j: pl.BlockSpec((1, 1, D_MODEL), lambda i: ((i // tiles_per_sample) * 6 + j, 0, 0))
    return pl.pallas_call(
        _outproj_kernel,
        grid=(N_TOK // OUT_TM,),
        in_specs=[row(RET_W), row(MLA_W), row(D_MODEL), full(wo_bf), mod(2), mod(3), mod(4),
                  full(ln_w), full(ln_b), full(w_route), full(b_route)],
        out_specs=[row(D_MODEL), pl.BlockSpec((OUT_TM, TOKEN_SUB, LANES), lambda i: (i, 0, 0)), row(LANES),
                   pl.BlockSpec((8, OUT_TM), lambda i: (0, i)), pl.BlockSpec((8, LANES), lambda i: (0, 0))],
        out_shape=[
            jax.ShapeDtypeStruct((N_TOK, D_MODEL), F32),
            jax.ShapeDtypeStruct((N_TOK, TOKEN_SUB, LANES), BF16),
            jax.ShapeDtypeStruct((N_TOK, LANES), F32),
            jax.ShapeDtypeStruct((8, N_TOK), F32),
            jax.ShapeDtypeStruct((8, LANES), F32),
        ],
        scratch_shapes=[pltpu.VMEM((8, LANES), F32), pltpu.VMEM((D_MODEL, 2 * LANES), BF16)],
        compiler_params=_params(("arbitrary",)),
        name="outproj_route",
    )(ret, att, x2d, wo_bf, mod_rows, mod_rows, mod_rows, ln_w, ln_b, w_route, b_route)


DISPATCH_TM = 2048


def _dispatch_kernel(pos_ref, t_ref, xs_ref, sem):
    i = pl.program_id(0)

    def copy(slot, j):
        return pltpu.make_async_copy(t_ref.at[j], xs_ref.at[pos_ref[slot * N_TOK + i * DISPATCH_TM + j]], sem)

    for slot in range(2):
        lax.fori_loop(0, DISPATCH_TM, lambda j, c, slot=slot: (copy(slot, j).start(priority=slot), c)[1], 0,
                      unroll=8)
    for slot in range(2):
        pltpu.make_async_copy(t_ref, xs_ref.at[pl.ds(0, DISPATCH_TM)], sem).wait()


def _dispatch(pos, t_tiles):
    return pl.pallas_call(
        _dispatch_kernel,
        grid_spec=pltpu.PrefetchScalarGridSpec(
            num_scalar_prefetch=1,
            grid=(N_TOK // DISPATCH_TM,),
            in_specs=[pl.BlockSpec((DISPATCH_TM, TOKEN_SUB, LANES), lambda i, pos: (i, 0, 0))],
            out_specs=pl.BlockSpec(memory_space=pl.ANY),
            scratch_shapes=[pltpu.SemaphoreType.DMA(())],
        ),
        out_shape=jax.ShapeDtypeStruct((N_PAIRS, TOKEN_SUB, LANES), BF16),
        compiler_params=_params(("arbitrary",)),
        name="dispatch",
    )(pos, t_tiles)


def _experts_kernel(rend_ref, xs_ref, wg_ref, wu_ref, wd_ref, ys_ref, wgu_ref, wdn_ref, xbuf_ref, ybuf_ref,
                    xsem, ysem):
    e = pl.program_id(0)
    lo = jnp.where(e == 0, 0, rend_ref[jnp.maximum(e - 1, 0)])
    hi = rend_ref[e]

    def x_copy(g):
        return pltpu.make_async_copy(xs_ref.at[pl.ds(g * ROW_TILE, ROW_TILE)], xbuf_ref.at[g % 2], xsem.at[g % 2])

    def y_copy(g):
        return pltpu.make_async_copy(ybuf_ref.at[g % 2], ys_ref.at[pl.ds(g * ROW_TILE, ROW_TILE)], ysem.at[g % 2])

    @pl.when(e == 0)
    def _():
        x_copy(0).start()

    @pl.when(hi > lo)
    def _():
        wgu_ref[:, :D_EXPERT] = wg_ref[0].astype(BF16)
        wgu_ref[:, D_EXPERT:] = wu_ref[0].astype(BF16)
        wdn_ref[...] = wd_ref[0].astype(BF16)

        def tile(g, c):
            row0 = g * ROW_TILE
            owns_first_row = lo <= row0
            owns_last_row = hi >= row0 + ROW_TILE

            @pl.when(owns_first_row)
            def _():
                @pl.when(g + 1 < N_TILES)
                def _():
                    x_copy(g + 1).start()

                x_copy(g).wait()

            gu = jnp.dot(_from_token_tiles(xbuf_ref[g % 2]), wgu_ref[...], preferred_element_type=F32)
            hid = _silu(gu[:, :D_EXPERT]) * gu[:, D_EXPERT:]
            y = jnp.dot(hid.astype(BF16), wdn_ref[...], preferred_element_type=F32)

            @pl.when(owns_first_row)
            def _():
                @pl.when(g >= 2)
                def _():
                    y_copy(g - 2).wait()

                ybuf_ref[g % 2] = _to_token_tiles(y)

            @pl.when(jnp.logical_not(owns_first_row))
            def _():
                row = lax.broadcasted_iota(I32, y.shape, 0)
                earlier = _from_token_tiles(ybuf_ref[g % 2]).astype(F32)
                ybuf_ref[g % 2] = _to_token_tiles(jnp.where(row >= lo - row0, y, earlier))

            @pl.when(owns_last_row)
            def _():
                y_copy(g).start()

            return c

        lax.fori_loop(lo // ROW_TILE, (hi - 1) // ROW_TILE + 1, tile, 0)

    @pl.when(e == N_EXPERTS - 1)
    def _():
        y_copy(N_TILES - 2).wait()
        y_copy(N_TILES - 1).wait()


def _experts(row_end, xs, w_gate, w_up, w_down):
    w_gate = w_gate.reshape(N_EXPERTS, D_MODEL, D_EXPERT)
    w_up = w_up.reshape(N_EXPERTS, D_MODEL, D_EXPERT)
    w_down = w_down.reshape(N_EXPERTS, D_EXPERT, D_MODEL)
    expert = lambda e, rend: (e, 0, 0)
    tile_buf = pltpu.VMEM((2, ROW_TILE, TOKEN_SUB, LANES), BF16)
    return pl.pallas_call(
        _experts_kernel,
        grid_spec=pltpu.PrefetchScalarGridSpec(
            num_scalar_prefetch=1,
            grid=(N_EXPERTS,),
            in_specs=[
                pl.BlockSpec(memory_space=pl.ANY),
                pl.BlockSpec((1, D_MODEL, D_EXPERT), expert),
                pl.BlockSpec((1, D_MODEL, D_EXPERT), expert),
                pl.BlockSpec((1, D_EXPERT, D_MODEL), expert),
            ],
            out_specs=pl.BlockSpec(memory_space=pl.ANY),
            scratch_shapes=[pltpu.VMEM((D_MODEL, 2 * D_EXPERT), BF16), pltpu.VMEM((D_EXPERT, D_MODEL), BF16),
                            tile_buf, tile_buf, pltpu.SemaphoreType.DMA((2,)), pltpu.SemaphoreType.DMA((2,))],
        ),
        out_shape=jax.ShapeDtypeStruct((N_PAIRS, TOKEN_SUB, LANES), BF16),
        compiler_params=_params(("arbitrary",)),
        name="experts",
    )(row_end, xs, w_gate, w_up, w_down)


COMB_TM = 256


def _combine_kernel(pos_ref, ys_ref, h_ref, route_ref, gate_ref, lnw_ref, lnb_ref, o_ref, buf_ref, sem):
    i = pl.program_id(0)
    n = pl.num_programs(0)

    def copy(step, slot, pair, j):
        src = pos_ref[pair * N_TOK + step * COMB_TM + j]
        return pltpu.make_async_copy(ys_ref.at[src], buf_ref.at[slot, pair * COMB_TM + j], sem.at[slot])

    def start_all(step, slot):
        for pair in range(2):
            lax.fori_loop(0, COMB_TM, lambda j, c, pair=pair: (copy(step, slot, pair, j).start(priority=pair), c)[1], 0,
                          unroll=8)

    def wait_all(step, slot):
        pltpu.make_async_copy(ys_ref.at[pl.ds(0, 2 * COMB_TM)], buf_ref.at[slot], sem.at[slot]).wait()

    @pl.when(i == 0)
    def _():
        start_all(0, 0)

    slot = i % 2
    wait_all(i, slot)

    nxt = jnp.minimum(i + 1, n - 1)
    for pair in range(2):
        for j in range(COMB_TM):
            copy(nxt, 1 - slot, pair, j).start(priority=pair)

    y1 = _from_token_tiles(buf_ref[slot, :COMB_TM]).astype(F32)
    y2 = _from_token_tiles(buf_ref[slot, COMB_TM:]).astype(F32)
    y = route_ref[:, ROUTE_W1:ROUTE_W1 + 1] * y1 + route_ref[:, ROUTE_W2:ROUTE_W2 + 1] * y2
    z = DEEPNORM_ALPHA * h_ref[...] + gate_ref[0] * y
    o_ref[...] = _plain_norm(z) * lnw_ref[...] + lnb_ref[...]

    @pl.when(i == n - 1)
    def _():
        wait_all(i, 1 - slot)


def _combine(pos, ys, h, route, mod_rows, ln_w, ln_b):
    tiles_per_sample = SEQ // COMB_TM
    row = lambda w: pl.BlockSpec((COMB_TM, w), lambda i, pos: (i, 0))
    full = lambda a: pl.BlockSpec(a.shape, lambda i, pos: (0, 0))
    return pl.pallas_call(
        _combine_kernel,
        grid_spec=pltpu.PrefetchScalarGridSpec(
            num_scalar_prefetch=1,
            grid=(N_TOK // COMB_TM,),
            in_specs=[
                pl.BlockSpec(memory_space=pl.ANY), row(D_MODEL), row(LANES),
                pl.BlockSpec((1, 1, D_MODEL), lambda i, pos: ((i // tiles_per_sample) * 6 + 5, 0, 0)),
                full(ln_w), full(ln_b),
            ],
            out_specs=row(D_MODEL),
            scratch_shapes=[pltpu.VMEM((2, 2 * COMB_TM, TOKEN_SUB, LANES), BF16), pltpu.SemaphoreType.DMA((2,))],
        ),
        out_shape=jax.ShapeDtypeStruct((N_TOK, D_MODEL), F32),
        compiler_params=_params(("arbitrary",)),
        name="combine",
    )(pos, ys, h, route, mod_rows, ln_w, ln_b)


def _routing_tables(route_t, counts):
    cnt = counts[0, N_GROUPS:N_ROUTE].astype(I32)
    row_end = jnp.cumsum(cnt)
    e = route_t[ROUTE_E1:ROUTE_E2 + 1].astype(I32)
    rank = route_t[ROUTE_R1:ROUTE_R2 + 1].astype(I32)
    earlier = jnp.arange(N_EXPERTS, dtype=I32)[:, None, None] < e[None]
    base = jnp.sum(jnp.where(earlier, cnt[:, None, None], 0), 0)
    pos = (base + rank).reshape(-1)
    return pos, row_end


def kernel(x, c, ctx, c_ctx, w_ada, b_ada, w_in, ret_decay, ret_gn_w, mla_q_norm, mla_kv_norm, w_uq, w_ukv, w_o,
           ln1_w, ln1_b, router_group_w, router_group_b, router_expert_w, router_expert_b, expert_w_gate,
           expert_w_up, expert_w_down, ln2_w, ln2_b):
    x2d = x.reshape(N_TOK, D_MODEL)
    ctx2d = ctx.reshape(N_CTX, D_MODEL)

    cc = jnp.zeros((8, D_MODEL), F32).at[:BATCH].set(c).at[BATCH].set(c_ctx)
    mod = _ada(cc, w_ada[0], b_ada)
    mod_rows = mod.reshape(8 * 6, 1, D_MODEL)

    proj, w_in_blocks = _inproj_latent(x2d, mod_rows, jnp.swapaxes(w_in, 1, 2))
    proj_c = _inproj_context(ctx2d, mod_rows, w_in_blocks)

    cos_r, sin_r = _rope_tables(RET_DK)
    decay_rows = jnp.broadcast_to(ret_decay[0].reshape(2 * RET_HEADS, 1, 1), (2 * RET_HEADS, 1, LANES))
    ret = _retention(proj, proj_c, jnp.asarray(cos_r), jnp.asarray(sin_r), decay_rows, ret_gn_w)

    cos_m, sin_m = _rope_tables(MLA_ROPE)
    cos_m = np.concatenate([cos_m, np.ones_like(cos_m)], 1)
    sin_m = np.concatenate([sin_m, np.zeros_like(sin_m)], 1)
    wq = w_uq[0].reshape(MLA_Q_LORA, MLA_HEADS, MLA_DQ)
    wq_pad = jnp.pad(wq, ((0, 0), (0, 0), (0, QK_PAD - MLA_DQ))).reshape(MLA_Q_LORA, MLA_HEADS * QK_PAD).astype(BF16)
    wkv = w_ukv[0].astype(BF16)
    q, k_lat, v_lat = _mla_latent(proj, mla_q_norm, mla_kv_norm, wq_pad, wkv, jnp.asarray(cos_m), jnp.asarray(sin_m))
    k_ctx, v_ctx = _mla_context(proj_c, mla_kv_norm, wkv)
    att, w_o_bf = _attention(q, k_ctx, k_lat, v_ctx, v_lat, w_o[0])

    w_route = jnp.concatenate(
        [router_group_w[0], router_expert_w[0].transpose(1, 0, 2).reshape(D_MODEL, N_EXPERTS),
         jnp.zeros((D_MODEL, LANES - N_ROUTE), F32)], 1)
    b_route = jnp.concatenate(
        [router_group_b[0], router_expert_b[0].reshape(N_EXPERTS), jnp.zeros((LANES - N_ROUTE,), F32)])[None]
    h, t_tiles, route, route_t, counts = _outproj(ret, att, x2d, w_o_bf, mod_rows, ln1_w, ln1_b,
                                                   w_route, b_route)

    pos, row_end = _routing_tables(route_t, counts)
    xs = _dispatch(pos, t_tiles)
    ys = _experts(row_end, xs, expert_w_gate[0], expert_w_up[0], expert_w_down[0])
    out = _combine(pos, ys, h, route, mod_rows, ln2_w, ln2_b)
    return out.reshape(BATCH, SEQ, D_MODEL)
```

```python
import numpy as np
import jax
import jax.numpy as jnp
from jax import lax
from jax.experimental import pallas as pl
from jax.experimental.pallas import tpu as pltpu

F32 = jnp.float32
BF16 = jnp.bfloat16
I32 = jnp.int32

D_MODEL = 2048
BATCH = 4
SEQ = 2048
GRID_W = 64
CTX_LEN = 256
N_TOK = BATCH * SEQ
N_CTX = BATCH * CTX_LEN

RET_HEADS = 8
RET_DK = 128
RET_DV = 128
RET_W = RET_HEADS * RET_DV
CHUNK = 128
N_CHUNKS = SEQ // CHUNK

MLA_HEADS = 8
MLA_Q_LORA = 512
MLA_KV_LORA = 256
MLA_NOPE = 128
MLA_ROPE = 64
MLA_DV = 128
MLA_W = MLA_HEADS * MLA_DV
MLA_DQ = MLA_NOPE + MLA_ROPE
QK_PAD = 256

IN_SIZES = (RET_HEADS * RET_DK, RET_HEADS * RET_DK, RET_W, RET_W, MLA_Q_LORA, MLA_KV_LORA, MLA_ROPE)
IN_W = sum(IN_SIZES)
OFF_Q, OFF_K, OFF_V, OFF_G, OFF_CQ, OFF_CKV, OFF_KPE = (int(v) for v in np.cumsum((0,) + IN_SIZES[:-1]))

N_GROUPS = 4
EXPERTS_PER_GROUP = 8
N_EXPERTS = N_GROUPS * EXPERTS_PER_GROUP
D_EXPERT = 512
N_ROUTE = N_GROUPS + N_EXPERTS

N_MOD = 6
MOD_SHIFT1, MOD_SCALE1, MOD_GATE1, MOD_SHIFT2, MOD_SCALE2, MOD_GATE2 = range(N_MOD)

LOG2_E = float(np.log2(np.e))
ROPE_BASE = 10000.0
EPS = 1e-6
DEPTH = 1
DEEPNORM_ALPHA = (2.0 * DEPTH) ** 0.25

LANES = 128
SUBLANES = 8
ROW_TILE = 256
N_PAIRS = 2 * N_TOK
N_TILES = N_PAIRS // ROW_TILE
TOKEN_SUB = D_MODEL // LANES

V7X_VMEM_BYTES = 64 * 1024 * 1024
VMEM_LIMIT = V7X_VMEM_BYTES - V7X_VMEM_BYTES // 8


def _params(sem):
    return pltpu.CompilerParams(dimension_semantics=sem, vmem_limit_bytes=VMEM_LIMIT)


def _silu(x):
    return x * (1.0 / (1.0 + jnp.exp(-x)))


def _plain_norm(x):
    mu = jnp.mean(x, -1, keepdims=True)
    xc = x - mu
    var = jnp.mean(xc * xc, -1, keepdims=True)
    return xc * lax.rsqrt(var + EPS)


def _to_token_tiles(x):
    return x.astype(BF16).reshape(x.shape[0], TOKEN_SUB, LANES)


def _from_token_tiles(x):
    return x.reshape(x.shape[0], D_MODEL)


ADA_TN = 1024


def _ada_kernel(cc_ref, w_ref, b_ref, o_ref):
    s = _silu(cc_ref[...])
    o_ref[...] = jnp.dot(s, w_ref[...], preferred_element_type=F32,
                         precision=lax.Precision.HIGHEST) + b_ref[...]


def _ada(cc, w_ada, b_ada):
    n = w_ada.shape[1]
    return pl.pallas_call(
        _ada_kernel,
        grid=(n // ADA_TN,),
        in_specs=[
            pl.BlockSpec((SUBLANES, D_MODEL), lambda j: (0, 0)),
            pl.BlockSpec((D_MODEL, ADA_TN), lambda j: (0, j)),
            pl.BlockSpec((1, ADA_TN), lambda j: (0, j)),
        ],
        out_specs=pl.BlockSpec((SUBLANES, ADA_TN), lambda j: (0, j)),
        out_shape=jax.ShapeDtypeStruct((SUBLANES, n), F32),
        compiler_params=_params(("arbitrary",)),
        name="ada",
    )(cc, w_ada, b_ada)


INPROJ_TM = 1024
INPROJ_TN = 1024
assert OFF_K == INPROJ_TN and OFF_V == 2 * INPROJ_TN and OFF_CQ == 4 * INPROJ_TN
CTX_OFF_K, CTX_OFF_V = 0, INPROJ_TN
CTX_OFF_CKV = 2 * INPROJ_TN + (OFF_CKV - OFF_CQ)
CTX_OFF_KPE = 2 * INPROJ_TN + (OFF_KPE - OFF_CQ)


INPROJ_SUB = 256
INPROJ_BLOCKS = -(-IN_W // INPROJ_TN)


def _inproj_products(x_ref, shift_ref, scale_ref, w, o_ref, xn_ref):
    first = pl.program_id(1) == 0

    @pl.when(first)
    def _():
        for s in range(INPROJ_TM // INPROJ_SUB):
            rows = slice(s * INPROJ_SUB, (s + 1) * INPROJ_SUB)
            y = (_plain_norm(x_ref[rows, :]) * (1.0 + scale_ref[0]) + shift_ref[0]).astype(BF16)
            xn_ref[rows, :] = y
            o_ref[rows, :] = _dot_nt(y, w[...]).astype(BF16)

    @pl.when(jnp.logical_not(first))
    def _():
        o_ref[...] = _dot_nt(xn_ref[...], w[...]).astype(BF16)


def _inproj_latent_kernel(x_ref, shift_ref, scale_ref, wf_ref, o_ref, wbf_ref, xn_ref, wbuf_ref, in_sem, out_sem):
    i = pl.program_id(0)
    k = pl.program_id(1)
    t = i * INPROJ_BLOCKS + k
    slot = t % 2
    n_steps = pl.num_programs(0) * INPROJ_BLOCKS

    def out_copy(block, s):
        return pltpu.make_async_copy(wbuf_ref.at[s], wbf_ref.at[block], out_sem.at[s])

    def in_copy(block, s):
        return pltpu.make_async_copy(wbf_ref.at[block], wbuf_ref.at[s], in_sem.at[s])

    @pl.when(i == 0)
    def _():
        @pl.when(k >= 2)
        def _():
            out_copy(k - 2, slot).wait()

        col = lax.broadcasted_iota(I32, (INPROJ_TN, D_MODEL), 0)
        wbuf_ref[slot] = jnp.where(col < IN_W - k * INPROJ_TN, wf_ref[0], 0.0).astype(BF16)
        out_copy(k, slot).start()

    @pl.when((t + 1 >= INPROJ_BLOCKS) & (t + 1 < n_steps))
    def _():
        @pl.when(t == INPROJ_BLOCKS - 1)
        def _():
            out_copy(INPROJ_BLOCKS - 2, 1 - slot).wait()

        @pl.when(t == INPROJ_BLOCKS)
        def _():
            out_copy(INPROJ_BLOCKS - 1, 1 - slot).wait()

        in_copy((k + 1) % INPROJ_BLOCKS, 1 - slot).start()

    @pl.when(i > 0)
    def _():
        in_copy(k, slot).wait()

    _inproj_products(x_ref, shift_ref, scale_ref, wbuf_ref.at[slot], o_ref, xn_ref)


def _inproj_context_kernel(x_ref, shift_ref, scale_ref, w_ref, o_ref, xn_ref):
    _inproj_products(x_ref, shift_ref, scale_ref, w_ref.at[0], o_ref, xn_ref)


def _inproj_mod_map(j, tiles_per_sample, sample_row0):
    return lambda i, k: ((sample_row0 + i // tiles_per_sample) * N_MOD + j, 0, 0)


def _inproj_latent(x2d, mod_rows, w_in_t):
    assert INPROJ_BLOCKS % 2 == 1
    last = INPROJ_BLOCKS - 1
    return pl.pallas_call(
        _inproj_latent_kernel,
        grid=(N_TOK // INPROJ_TM, INPROJ_BLOCKS),
        in_specs=[
            pl.BlockSpec((INPROJ_TM, D_MODEL), lambda i, k: (i, 0)),
            pl.BlockSpec((1, 1, D_MODEL), _inproj_mod_map(MOD_SHIFT1, SEQ // INPROJ_TM, 0)),
            pl.BlockSpec((1, 1, D_MODEL), _inproj_mod_map(MOD_SCALE1, SEQ // INPROJ_TM, 0)),
            pl.BlockSpec((1, INPROJ_TN, D_MODEL), lambda i, k: (0, jnp.where(i == 0, k, last), 0)),
        ],
        out_specs=[pl.BlockSpec((INPROJ_TM, INPROJ_TN), lambda i, k: (i, k)), pl.BlockSpec(memory_space=pl.ANY)],
        out_shape=[jax.ShapeDtypeStruct((N_TOK, IN_W), BF16),
                   jax.ShapeDtypeStruct((INPROJ_BLOCKS, INPROJ_TN, D_MODEL), BF16)],
        scratch_shapes=[pltpu.VMEM((INPROJ_TM, D_MODEL), BF16), pltpu.VMEM((2, INPROJ_TN, D_MODEL), BF16),
                        pltpu.SemaphoreType.DMA((2,)), pltpu.SemaphoreType.DMA((2,))],
        compiler_params=_params(("arbitrary", "arbitrary")),
        name="inproj",
    )(x2d, mod_rows, mod_rows, w_in_t)


def _inproj_context(ctx2d, mod_rows, w_blocks):
    last = INPROJ_BLOCKS - 1
    out_w = 2 * INPROJ_TN + (IN_W - last * INPROJ_TN)
    return pl.pallas_call(
        _inproj_context_kernel,
        grid=(N_CTX // INPROJ_TM, 3),
        in_specs=[
            pl.BlockSpec((INPROJ_TM, D_MODEL), lambda i, k: (i, 0)),
            pl.BlockSpec((1, 1, D_MODEL), _inproj_mod_map(MOD_SHIFT1, N_CTX // INPROJ_TM, BATCH)),
            pl.BlockSpec((1, 1, D_MODEL), _inproj_mod_map(MOD_SCALE1, N_CTX // INPROJ_TM, BATCH)),
            pl.BlockSpec((1, INPROJ_TN, D_MODEL), lambda i, k: (jnp.where(k == 2, last, k + 1), 0, 0)),
        ],
        out_specs=pl.BlockSpec((INPROJ_TM, INPROJ_TN), lambda i, k: (i, k)),
        out_shape=jax.ShapeDtypeStruct((N_CTX, out_w), BF16),
        scratch_shapes=[pltpu.VMEM((INPROJ_TM, D_MODEL), BF16)],
        compiler_params=_params(("arbitrary", "arbitrary")),
        name="inproj_ctx",
    )(ctx2d, mod_rows, mod_rows, w_blocks)


def _rope_tables(width):
    half = width // 2
    quarter = half // 2
    inv_freq = ROPE_BASE ** (-np.arange(0, half, 2, dtype=np.float64) / half)
    t = np.arange(SEQ)
    cos_parts, sin_parts = [], []
    for pos in (t // GRID_W, t % GRID_W):
        ang = pos[:, None].astype(np.float64) * inv_freq[None, :]
        c, s = np.cos(ang), np.sin(ang)
        cos_parts += [c, c]
        sin_parts += [-s, s]
    assert cos_parts[0].shape[1] == quarter
    return (np.concatenate(cos_parts, 1).astype(np.float32), np.concatenate(sin_parts, 1).astype(np.float32))


def _rope(x, cos, sin, quarter):
    lane = lax.broadcasted_iota(I32, x.shape, 1)
    first = (lane % (2 * quarter)) < quarter
    swapped = jnp.where(first, pltpu.roll(x, LANES - quarter, 1), pltpu.roll(x, quarter, 1))
    return x * cos + swapped * sin


def _dot_tn(a, b):
    return lax.dot_general(a, b, (((0,), (0,)), ((), ())), preferred_element_type=F32)


def _dot_nt(a, b):
    return lax.dot_general(a, b, (((1,), (1,)), ((), ())), preferred_element_type=F32)


RET_HG = 4


def _retention_kernel(q_ref, k_ref, v_ref, g_ref, kc_ref, vc_ref, cos_ref, sin_ref, df_ref, db_ref, gn_ref,
                      o_ref, qs_ref, ks_ref, st_ref):
    for hh in range(RET_HG):
        _retention_head(hh, slice(hh * LANES, (hh + 1) * LANES), q_ref, k_ref, v_ref, g_ref, kc_ref, vc_ref,
                        cos_ref, sin_ref, df_ref, db_ref, gn_ref, o_ref, qs_ref, ks_ref, st_ref)


def _retention_head(hh, hs, q_ref, k_ref, v_ref, g_ref, kc_ref, vc_ref, cos_ref, sin_ref, df_ref, db_ref, gn_ref,
                    o_ref, qs_ref, ks_ref, st_ref):
    k_scale = RET_DK ** -0.5
    lgf = jax.nn.log_sigmoid(df_ref[hh])
    lgb = jax.nn.log_sigmoid(db_ref[hh])

    cos = cos_ref[...]
    sin = sin_ref[...]
    qs_ref[hh] = _rope(q_ref[:, hs].astype(F32), cos, sin, RET_DK // 4).astype(BF16)
    ks_ref[hh] = _rope(k_ref[:, hs].astype(F32), cos, sin, RET_DK // 4) * k_scale

    rowi = lax.broadcasted_iota(I32, (CHUNK, LANES), 0).astype(F32)
    coli = lax.broadcasted_iota(I32, (CHUNK, LANES), 1).astype(F32)
    diff = rowi - coli
    decay = jnp.exp(jnp.where(diff >= 0, lgf * diff, -lgb * diff)) * jnp.where(diff == 0, 2.0, 1.0)
    zeta_f = jnp.exp(lgf * (CHUNK - 1.0 - rowi))
    eta_b = jnp.exp(lgb * rowi)
    xi_f = jnp.exp(lgf * (rowi + 1.0))
    xi_b = jnp.exp(lgb * (CHUNK - rowi))
    cdec_f = jnp.exp(lgf * float(CHUNK))
    cdec_b = jnp.exp(lgb * float(CHUNK))

    crow = lax.broadcasted_iota(I32, (CTX_LEN, LANES), 0).astype(F32)
    kc = kc_ref[:, hs].astype(F32) * k_scale
    vc = vc_ref[:, hs]
    s_f = _dot_tn((kc * jnp.exp(lgf * (CTX_LEN - 1.0 - crow))).astype(BF16), vc)
    s_b = _dot_tn((kc * jnp.exp(lgb * crow)).astype(BF16), vc)

    upd_f, upd_b = [], []
    for i in range(N_CHUNKS):
        rows = pl.ds(i * CHUNK, CHUNK)
        kch = ks_ref[hh, rows, :]
        vch = v_ref[rows, hs]
        upd_f.append(_dot_tn((kch * zeta_f).astype(BF16), vch))
        upd_b.append(_dot_tn((kch * eta_b).astype(BF16), vch))
    state = s_f
    for i in range(N_CHUNKS):
        st_ref[hh, i, :, :RET_DV] = state.astype(BF16)
        state = cdec_f * state + upd_f[i]
    state = s_b
    for i in reversed(range(N_CHUNKS)):
        st_ref[hh, i, :, RET_DV:] = state.astype(BF16)
        state = cdec_b * state + upd_b[i]

    gn_w = gn_ref[:, hs]
    for i in range(N_CHUNKS):
        rows = pl.ds(i * CHUNK, CHUNK)
        qch = qs_ref[hh, rows, :]
        scores = _dot_nt(qch, ks_ref[hh, rows, :].astype(BF16)) * decay
        o = jnp.dot(scores.astype(BF16), v_ref[rows, hs], preferred_element_type=F32)
        cross = jnp.dot(qch, st_ref[hh, i], preferred_element_type=F32)
        o = o + xi_f * cross[:, :RET_DV] + xi_b * cross[:, RET_DV:]
        y = _plain_norm(o) * gn_w
        o_ref[rows, hs] = (_silu(g_ref[rows, hs].astype(F32)) * y).astype(BF16)


def _retention(proj, proj_c, cos, sin, decay_rows, gn_w):
    width = RET_HG * LANES
    groups = RET_HEADS // RET_HG
    blk = lambda off: pl.BlockSpec((SEQ, width), lambda b, hg: (b, off // width + hg))
    blk_c = lambda off: pl.BlockSpec((CTX_LEN, width), lambda b, hg: (b, off // width + hg))
    table = pl.BlockSpec((SEQ, LANES), lambda b, hg: (0, 0))
    return pl.pallas_call(
        _retention_kernel,
        grid=(BATCH, groups),
        in_specs=[
            blk(OFF_Q), blk(OFF_K), blk(OFF_V), blk(OFF_G), blk_c(CTX_OFF_K), blk_c(CTX_OFF_V), table, table,
            pl.BlockSpec((RET_HG, 1, LANES), lambda b, hg: (hg, 0, 0)),
            pl.BlockSpec((RET_HG, 1, LANES), lambda b, hg: (groups + hg, 0, 0)),
            pl.BlockSpec((1, width), lambda b, hg: (0, hg)),
        ],
        out_specs=pl.BlockSpec((SEQ, width), lambda b, hg: (b, hg)),
        out_shape=jax.ShapeDtypeStruct((N_TOK, RET_W), BF16),
        scratch_shapes=[
            pltpu.VMEM((RET_HG, SEQ, RET_DK), BF16),
            pltpu.VMEM((RET_HG, SEQ, RET_DK), F32),
            pltpu.VMEM((RET_HG, N_CHUNKS, RET_DK, 2 * RET_DV), BF16),
        ],
        compiler_params=_params(("arbitrary", "arbitrary")),
        name="retention",
    )(proj, proj, proj, proj, proj_c, proj_c, cos, sin, decay_rows, decay_rows, gn_w)


MLA_TM = 512


def _rms_norm(x, w):
    return x * lax.rsqrt(jnp.mean(x * x, -1, keepdims=True) + EPS) * w


def _mla_kv(ckv_ref, kpe_ref, kvn_ref, wkv_ref, cos_ref, sin_ref, k_ref, v_ref, rotate):
    ckv = _rms_norm(ckv_ref[...].astype(F32), kvn_ref[...]).astype(BF16)
    kv = jnp.dot(ckv, wkv_ref[...], preferred_element_type=F32)
    lane = lax.broadcasted_iota(I32, (ckv.shape[0], LANES), 1)
    kpe = jnp.where(lane < MLA_ROPE, kpe_ref[...].astype(F32), 0.0)
    if rotate:
        kpe = _rope(kpe, cos_ref[...], sin_ref[...], MLA_ROPE // 4)
    kpe = kpe.astype(BF16)
    for h in range(MLA_HEADS):
        k_ref[:, h * QK_PAD:h * QK_PAD + MLA_NOPE] = kv[:, 2 * h * LANES:(2 * h + 1) * LANES].astype(BF16)
        k_ref[:, h * QK_PAD + MLA_NOPE:(h + 1) * QK_PAD] = kpe
        v_ref[:, h * MLA_DV:(h + 1) * MLA_DV] = kv[:, (2 * h + 1) * LANES:(2 * h + 2) * LANES].astype(BF16)


def _mla_latent_kernel(cq_ref, ckv_ref, kpe_ref, qn_ref, kvn_ref, wq_ref, wkv_ref, cos_ref, sin_ref,
                       q_ref, k_ref, v_ref):
    cq = _rms_norm(cq_ref[...].astype(F32), qn_ref[...]).astype(BF16)
    q = jnp.dot(cq, wq_ref[...], preferred_element_type=F32)
    cos = cos_ref[...]
    sin = sin_ref[...]
    scale = MLA_DQ ** -0.5 * LOG2_E
    for h in range(MLA_HEADS):
        lo = h * QK_PAD
        q_ref[:, lo:lo + MLA_NOPE] = (q[:, lo:lo + MLA_NOPE] * scale).astype(BF16)
        qpe = _rope(q[:, lo + MLA_NOPE:lo + QK_PAD], cos, sin, MLA_ROPE // 4)
        q_ref[:, lo + MLA_NOPE:lo + QK_PAD] = (qpe * scale).astype(BF16)
    _mla_kv(ckv_ref, kpe_ref, kvn_ref, wkv_ref, cos_ref, sin_ref, k_ref, v_ref, rotate=True)


def _mla_context_kernel(ckv_ref, kpe_ref, kvn_ref, wkv_ref, k_ref, v_ref):
    _mla_kv(ckv_ref, kpe_ref, kvn_ref, wkv_ref, None, None, k_ref, v_ref, rotate=False)


def _mla_latent(proj, q_norm, kv_norm, wq_pad, wkv, cos, sin):
    row = lambda w, off: pl.BlockSpec((MLA_TM, w), lambda i: (i, off // w))
    full = lambda a: pl.BlockSpec(a.shape, lambda i: (0, 0))
    table = pl.BlockSpec((MLA_TM, LANES), lambda i: (i % (SEQ // MLA_TM), 0))
    return pl.pallas_call(
        _mla_latent_kernel,
        grid=(N_TOK // MLA_TM,),
        in_specs=[row(MLA_Q_LORA, OFF_CQ), row(MLA_KV_LORA, OFF_CKV), row(LANES, OFF_KPE),
                  full(q_norm), full(kv_norm), full(wq_pad), full(wkv), table, table],
        out_specs=[
            pl.BlockSpec((MLA_TM, MLA_HEADS * QK_PAD), lambda i: (i, 0)),
            pl.BlockSpec((MLA_TM, MLA_HEADS * QK_PAD), lambda i: (i, 0)),
            pl.BlockSpec((MLA_TM, MLA_W), lambda i: (i, 0)),
        ],
        out_shape=[
            jax.ShapeDtypeStruct((N_TOK, MLA_HEADS * QK_PAD), BF16),
            jax.ShapeDtypeStruct((N_TOK, MLA_HEADS * QK_PAD), BF16),
            jax.ShapeDtypeStruct((N_TOK, MLA_W), BF16),
        ],
        compiler_params=_params(("arbitrary",)),
        name="mla_latent",
    )(proj, proj, proj, q_norm, kv_norm, wq_pad, wkv, cos, sin)


def _mla_context(proj_c, kv_norm, wkv):
    row = lambda w, off: pl.BlockSpec((MLA_TM, w), lambda i: (i, off // w))
    full = lambda a: pl.BlockSpec(a.shape, lambda i: (0, 0))
    return pl.pallas_call(
        _mla_context_kernel,
        grid=(N_CTX // MLA_TM,),
        in_specs=[row(MLA_KV_LORA, CTX_OFF_CKV), row(LANES, CTX_OFF_KPE), full(kv_norm), full(wkv)],
        out_specs=[
            pl.BlockSpec((MLA_TM, MLA_HEADS * QK_PAD), lambda i: (i, 0)),
            pl.BlockSpec((MLA_TM, MLA_W), lambda i: (i, 0)),
        ],
        out_shape=[
            jax.ShapeDtypeStruct((N_CTX, MLA_HEADS * QK_PAD), BF16),
            jax.ShapeDtypeStruct((N_CTX, MLA_W), BF16),
        ],
        compiler_params=_params(("arbitrary",)),
        name="mla_context",
    )(proj_c, proj_c, kv_norm, wkv)


ATT_TQ = 512


def _attention_kernel(q_ref, kc_ref, kl_ref, vc_ref, vl_ref, wo_ref, o_ref, wo_bf_ref, vext_ref):
    wo_bf_ref[...] = wo_ref[...].astype(BF16)

    @pl.when(pl.program_id(1) == 0)
    def _():
        lane = lax.broadcasted_iota(I32, (CTX_LEN + SEQ, MLA_DV), 1)
        ones_col = jnp.where(lane == 0, 1.0, 0.0).astype(BF16)
        for h in range(MLA_HEADS):
            dv = slice(h * MLA_DV, (h + 1) * MLA_DV)
            vext_ref[h, :CTX_LEN, :MLA_DV] = vc_ref[:, dv]
            vext_ref[h, CTX_LEN:, :MLA_DV] = vl_ref[:, dv]
            vext_ref[h, :, MLA_DV:] = ones_col

    for h in range(MLA_HEADS):
        qk = slice(h * QK_PAD, (h + 1) * QK_PAD)
        q = q_ref[:, qk]
        s_c = _dot_nt(q, kc_ref[:, qk])
        s_l = _dot_nt(q, kl_ref[:, qk])
        m = jnp.maximum(jnp.max(s_c, -1, keepdims=True), jnp.max(s_l, -1, keepdims=True))
        p_c = jnp.exp2(s_c - m).astype(BF16)
        p_l = jnp.exp2(s_l - m).astype(BF16)
        o = (jnp.dot(p_c, vext_ref[h, :CTX_LEN, :], preferred_element_type=F32)
             + jnp.dot(p_l, vext_ref[h, CTX_LEN:, :], preferred_element_type=F32))
        o_ref[:, h * MLA_DV:(h + 1) * MLA_DV] = (o[:, :MLA_DV] / o[:, MLA_DV:MLA_DV + 1]).astype(BF16)


def _attention(q, k_ctx, k_lat, v_ctx, v_lat, w_o):
    tiles = SEQ // ATT_TQ
    band = w_o.shape[0] // (BATCH * tiles)
    step = lambda b, i: (b * tiles + i, 0)
    return pl.pallas_call(
        _attention_kernel,
        grid=(BATCH, tiles),
        in_specs=[
            pl.BlockSpec((ATT_TQ, MLA_HEADS * QK_PAD), step),
            pl.BlockSpec((CTX_LEN, MLA_HEADS * QK_PAD), lambda b, i: (b, 0)),
            pl.BlockSpec((SEQ, MLA_HEADS * QK_PAD), lambda b, i: (b, 0)),
            pl.BlockSpec((CTX_LEN, MLA_W), lambda b, i: (b, 0)),
            pl.BlockSpec((SEQ, MLA_W), lambda b, i: (b, 0)),
            pl.BlockSpec((band, D_MODEL), step),
        ],
        out_specs=[pl.BlockSpec((ATT_TQ, MLA_W), step), pl.BlockSpec((band, D_MODEL), step)],
        out_shape=[jax.ShapeDtypeStruct((N_TOK, MLA_W), BF16), jax.ShapeDtypeStruct(w_o.shape, BF16)],
        scratch_shapes=[pltpu.VMEM((MLA_HEADS, CTX_LEN + SEQ, 2 * MLA_DV), BF16)],
        compiler_params=_params(("arbitrary", "arbitrary")),
        name="attention",
    )(q, k_ctx, k_lat, v_ctx, v_lat, w_o)


OUT_TM = 512
OUT_SUB = 256
ROUTE_E1, ROUTE_E2, ROUTE_R1, ROUTE_R2, ROUTE_W1, ROUTE_W2 = range(6)


def _outproj_kernel(ret_ref, att_ref, x_ref, wo_ref, gate_ref, shift_ref, scale_ref, lnw_ref, lnb_ref,
                    wr_ref, br_ref, h_ref, t_ref, route_ref, route_t_ref, count_ref, carry_ref, wsplit_ref):
    @pl.when(pl.program_id(0) == 0)
    def _():
        carry_ref[...] = jnp.zeros_like(carry_ref)
        w = wr_ref[...]
        w_hi = w.astype(BF16)
        wsplit_ref[:, :LANES] = w_hi
        wsplit_ref[:, LANES:] = (w - w_hi.astype(F32)).astype(BF16)

    subtiles = [slice(s * OUT_SUB, (s + 1) * OUT_SUB) for s in range(OUT_TM // OUT_SUB)]
    mixes = [jnp.dot(ret_ref[rows, :], wo_ref[:RET_W, :], preferred_element_type=F32)
             + jnp.dot(att_ref[rows, :], wo_ref[RET_W:, :], preferred_element_type=F32) for rows in subtiles]
    logits = [_outproj_norms(rows, mix, x_ref, gate_ref, shift_ref, scale_ref, lnw_ref, lnb_ref, br_ref, h_ref, t_ref,
                             wsplit_ref) for rows, mix in zip(subtiles, mixes)]
    _route(jnp.concatenate(logits, 0), carry_ref, count_ref, route_ref, route_t_ref)


def _outproj_norms(rows, mix, x_ref, gate_ref, shift_ref, scale_ref, lnw_ref, lnb_ref, br_ref, h_ref, t_ref,
                   wsplit_ref):
    h = _plain_norm(DEEPNORM_ALPHA * x_ref[rows, :] + gate_ref[0] * mix) * lnw_ref[...] + lnb_ref[...]
    h_ref[rows, :] = h
    t = _plain_norm(h) * (1.0 + scale_ref[0]) + shift_ref[0]
    t_ref[rows] = _to_token_tiles(t)

    t_hi = t.astype(BF16)
    t_lo = (t - t_hi.astype(F32)).astype(BF16)
    main = jnp.dot(t_hi, wsplit_ref[...], preferred_element_type=F32)
    corr = jnp.dot(t_lo, wsplit_ref[:, :LANES], preferred_element_type=F32)
    return main[:, :LANES] + (main[:, LANES:] + corr) + br_ref[...]


def _route(logits, carry_ref, count_ref, route_ref, route_t_ref):
    lane = lax.broadcasted_iota(I32, logits.shape, 1).astype(F32)
    neg = -jnp.inf
    big = float(LANES)

    def first_lane_of(mask):
        return jnp.min(jnp.where(mask, lane, big), -1, keepdims=True)

    is_group = lane < N_GROUPS
    gl = jnp.where(is_group, logits, neg)
    g_max = jnp.max(gl, -1, keepdims=True)
    g_idx = first_lane_of(is_group & (gl == g_max))
    g_prob = 1.0 / jnp.sum(jnp.where(is_group, jnp.exp(logits - g_max), 0.0), -1, keepdims=True)

    lo = N_GROUPS + g_idx * EXPERTS_PER_GROUP
    in_group = (lane >= lo) & (lane < lo + EXPERTS_PER_GROUP)
    el = jnp.where(in_group, logits, neg)
    v1 = jnp.max(el, -1, keepdims=True)
    i1 = first_lane_of(in_group & (el == v1))
    rest = in_group & (lane != i1)
    el2 = jnp.where(rest, logits, neg)
    v2 = jnp.max(el2, -1, keepdims=True)
    i2 = first_lane_of(rest & (el2 == v2))
    d = jnp.exp(v2 - v1)
    w1 = g_prob / (1.0 + d)
    w2 = g_prob * d / (1.0 + d)

    onehot = jnp.where((lane == i1) | (lane == i2), 1.0, 0.0)
    r = lax.broadcasted_iota(I32, (OUT_TM, OUT_TM), 0)
    c = lax.broadcasted_iota(I32, (OUT_TM, OUT_TM), 1)
    tri = jnp.where(c < r, 1.0, 0.0).astype(BF16)
    carry = carry_ref[0:1, :]
    before = jnp.dot(tri, onehot.astype(BF16), preferred_element_type=F32) + carry
    r1 = jnp.sum(jnp.where(lane == i1, before, 0.0), -1, keepdims=True)
    r2 = jnp.sum(jnp.where(lane == i2, before, 0.0), -1, keepdims=True)
    carry = carry + jnp.sum(onehot, 0, keepdims=True)
    carry_ref[...] = jnp.broadcast_to(carry, carry_ref.shape)
    count_ref[...] = jnp.broadcast_to(carry, count_ref.shape)

    rec = jnp.zeros_like(logits)
    for slot, val in ((ROUTE_E1, i1 - N_GROUPS), (ROUTE_E2, i2 - N_GROUPS), (ROUTE_R1, r1), (ROUTE_R2, r2),
                      (ROUTE_W1, w1), (ROUTE_W2, w2)):
        rec = jnp.where(lane == slot, val, rec)
    route_ref[...] = rec
    route_t_ref[...] = rec.T[:SUBLANES, :]


def _outproj(ret, att, x2d, wo_bf, mod_rows, ln_w, ln_b, w_route, b_route):
    tiles_per_sample = SEQ // OUT_TM
    row = lambda w: pl.BlockSpec((OUT_TM, w), lambda i: (i, 0))
    full = lambda a: pl.BlockSpec(a.shape, lambda i: (0, 0))
    mod = lambda j: pl.BlockSpec((1, 1, D_MODEL), lambda i: ((i // tiles_per_sample) * N_MOD + j, 0, 0))
    return pl.pallas_call(
        _outproj_kernel,
        grid=(N_TOK // OUT_TM,),
        in_specs=[row(RET_W), row(MLA_W), row(D_MODEL), full(wo_bf), mod(MOD_GATE1), mod(MOD_SHIFT2), mod(MOD_SCALE2),
                  full(ln_w), full(ln_b), full(w_route), full(b_route)],
        out_specs=[row(D_MODEL), pl.BlockSpec((OUT_TM, TOKEN_SUB, LANES), lambda i: (i, 0, 0)), row(LANES),
                   pl.BlockSpec((SUBLANES, OUT_TM), lambda i: (0, i)),
                   pl.BlockSpec((SUBLANES, LANES), lambda i: (0, 0))],
        out_shape=[
            jax.ShapeDtypeStruct((N_TOK, D_MODEL), F32),
            jax.ShapeDtypeStruct((N_TOK, TOKEN_SUB, LANES), BF16),
            jax.ShapeDtypeStruct((N_TOK, LANES), F32),
            jax.ShapeDtypeStruct((SUBLANES, N_TOK), F32),
            jax.ShapeDtypeStruct((SUBLANES, LANES), F32),
        ],
        scratch_shapes=[pltpu.VMEM((SUBLANES, LANES), F32), pltpu.VMEM((D_MODEL, 2 * LANES), BF16)],
        compiler_params=_params(("arbitrary",)),
        name="outproj_route",
    )(ret, att, x2d, wo_bf, mod_rows, mod_rows, mod_rows, ln_w, ln_b, w_route, b_route)


DISPATCH_TM = 1024


def _dispatch_kernel(pos_ref, t_ref, xs_ref, sem):
    i = pl.program_id(0)

    def copy(slot, j):
        return pltpu.make_async_copy(t_ref.at[j], xs_ref.at[pos_ref[slot * N_TOK + i * DISPATCH_TM + j]], sem)

    for slot in range(2):
        lax.fori_loop(0, DISPATCH_TM, lambda j, c, slot=slot: (copy(slot, j).start(priority=slot), c)[1], 0,
                      unroll=8)
    for slot in range(2):
        pltpu.make_async_copy(t_ref, xs_ref.at[pl.ds(0, DISPATCH_TM)], sem).wait()


def _dispatch(pos, t_tiles):
    return pl.pallas_call(
        _dispatch_kernel,
        grid_spec=pltpu.PrefetchScalarGridSpec(
            num_scalar_prefetch=1,
            grid=(N_TOK // DISPATCH_TM,),
            in_specs=[pl.BlockSpec((DISPATCH_TM, TOKEN_SUB, LANES), lambda i, pos: (i, 0, 0))],
            out_specs=pl.BlockSpec(memory_space=pl.ANY),
            scratch_shapes=[pltpu.SemaphoreType.DMA(())],
        ),
        out_shape=jax.ShapeDtypeStruct((N_PAIRS, TOKEN_SUB, LANES), BF16),
        compiler_params=_params(("arbitrary",)),
        name="dispatch",
    )(pos, t_tiles)


def _experts_kernel(rend_ref, xs_ref, wg_ref, wu_ref, wd_ref, ys_ref, wgu_ref, wdn_ref, xbuf_ref, ybuf_ref,
                    xsem, ysem):
    e = pl.program_id(0)
    lo = jnp.where(e == 0, 0, rend_ref[jnp.maximum(e - 1, 0)])
    hi = rend_ref[e]

    def x_copy(g):
        return pltpu.make_async_copy(xs_ref.at[pl.ds(g * ROW_TILE, ROW_TILE)], xbuf_ref.at[g % 2], xsem.at[g % 2])

    def y_copy(g):
        return pltpu.make_async_copy(ybuf_ref.at[g % 2], ys_ref.at[pl.ds(g * ROW_TILE, ROW_TILE)], ysem.at[g % 2])

    @pl.when(e == 0)
    def _():
        x_copy(0).start()

    @pl.when(hi > lo)
    def _():
        wgu_ref[:, :D_EXPERT] = wg_ref[0].astype(BF16)
        wgu_ref[:, D_EXPERT:] = wu_ref[0].astype(BF16)
        wdn_ref[...] = wd_ref[0].astype(BF16)

        def tile(g, c):
            row0 = g * ROW_TILE
            owns_first_row = lo <= row0
            owns_last_row = hi >= row0 + ROW_TILE

            @pl.when(owns_first_row)
            def _():
                @pl.when(g + 1 < N_TILES)
                def _():
                    x_copy(g + 1).start()

                x_copy(g).wait()

            gu = jnp.dot(_from_token_tiles(xbuf_ref[g % 2]), wgu_ref[...], preferred_element_type=F32)
            hid = _silu(gu[:, :D_EXPERT]) * gu[:, D_EXPERT:]
            y = jnp.dot(hid.astype(BF16), wdn_ref[...], preferred_element_type=F32)

            @pl.when(owns_first_row)
            def _():
                @pl.when(g >= 2)
                def _():
                    y_copy(g - 2).wait()

                ybuf_ref[g % 2] = _to_token_tiles(y)

            @pl.when(jnp.logical_not(owns_first_row))
            def _():
                row = lax.broadcasted_iota(I32, y.shape, 0)
                earlier = _from_token_tiles(ybuf_ref[g % 2]).astype(F32)
                ybuf_ref[g % 2] = _to_token_tiles(jnp.where(row >= lo - row0, y, earlier))

            @pl.when(owns_last_row)
            def _():
                y_copy(g).start()

            return c

        lax.fori_loop(lo // ROW_TILE, (hi - 1) // ROW_TILE + 1, tile, 0)

    @pl.when(e == N_EXPERTS - 1)
    def _():
        y_copy(N_TILES - 2).wait()
        y_copy(N_TILES - 1).wait()


def _experts(row_end, xs, w_gate, w_up, w_down):
    w_gate = w_gate.reshape(N_EXPERTS, D_MODEL, D_EXPERT)
    w_up = w_up.reshape(N_EXPERTS, D_MODEL, D_EXPERT)
    w_down = w_down.reshape(N_EXPERTS, D_EXPERT, D_MODEL)
    expert = lambda e, rend: (e, 0, 0)
    tile_buf = pltpu.VMEM((2, ROW_TILE, TOKEN_SUB, LANES), BF16)
    return pl.pallas_call(
        _experts_kernel,
        grid_spec=pltpu.PrefetchScalarGridSpec(
            num_scalar_prefetch=1,
            grid=(N_EXPERTS,),
            in_specs=[
                pl.BlockSpec(memory_space=pl.ANY),
                pl.BlockSpec((1, D_MODEL, D_EXPERT), expert),
                pl.BlockSpec((1, D_MODEL, D_EXPERT), expert),
                pl.BlockSpec((1, D_EXPERT, D_MODEL), expert),
            ],
            out_specs=pl.BlockSpec(memory_space=pl.ANY),
            scratch_shapes=[pltpu.VMEM((D_MODEL, 2 * D_EXPERT), BF16), pltpu.VMEM((D_EXPERT, D_MODEL), BF16),
                            tile_buf, tile_buf, pltpu.SemaphoreType.DMA((2,)), pltpu.SemaphoreType.DMA((2,))],
        ),
        out_shape=jax.ShapeDtypeStruct((N_PAIRS, TOKEN_SUB, LANES), BF16),
        compiler_params=_params(("arbitrary",)),
        name="experts",
    )(row_end, xs, w_gate, w_up, w_down)


COMB_TM = 256


def _combine_kernel(pos_ref, ys_ref, h_ref, route_ref, gate_ref, lnw_ref, lnb_ref, o_ref, buf_ref, sem):
    i = pl.program_id(0)
    n = pl.num_programs(0)

    def copy(step, slot, pair, j):
        src = pos_ref[pair * N_TOK + step * COMB_TM + j]
        return pltpu.make_async_copy(ys_ref.at[src], buf_ref.at[slot, pair * COMB_TM + j], sem.at[slot])

    def start_all(step, slot):
        for pair in range(2):
            lax.fori_loop(0, COMB_TM, lambda j, c, pair=pair: (copy(step, slot, pair, j).start(priority=pair), c)[1], 0,
                          unroll=8)

    def wait_all(step, slot):
        pltpu.make_async_copy(ys_ref.at[pl.ds(0, 2 * COMB_TM)], buf_ref.at[slot], sem.at[slot]).wait()

    @pl.when(i == 0)
    def _():
        start_all(0, 0)

    slot = i % 2
    wait_all(i, slot)

    nxt = jnp.minimum(i + 1, n - 1)
    for pair in range(2):
        for j in range(COMB_TM):
            copy(nxt, 1 - slot, pair, j).start(priority=pair)

    y1 = _from_token_tiles(buf_ref[slot, :COMB_TM]).astype(F32)
    y2 = _from_token_tiles(buf_ref[slot, COMB_TM:]).astype(F32)
    y = route_ref[:, ROUTE_W1:ROUTE_W1 + 1] * y1 + route_ref[:, ROUTE_W2:ROUTE_W2 + 1] * y2
    z = DEEPNORM_ALPHA * h_ref[...] + gate_ref[0] * y
    o_ref[...] = _plain_norm(z) * lnw_ref[...] + lnb_ref[...]

    @pl.when(i == n - 1)
    def _():
        wait_all(i, 1 - slot)


def _combine(pos, ys, h, route, mod_rows, ln_w, ln_b):
    tiles_per_sample = SEQ // COMB_TM
    row = lambda w: pl.BlockSpec((COMB_TM, w), lambda i, pos: (i, 0))
    full = lambda a: pl.BlockSpec(a.shape, lambda i, pos: (0, 0))
    return pl.pallas_call(
        _combine_kernel,
        grid_spec=pltpu.PrefetchScalarGridSpec(
            num_scalar_prefetch=1,
            grid=(N_TOK // COMB_TM,),
            in_specs=[
                pl.BlockSpec(memory_space=pl.ANY), row(D_MODEL), row(LANES),
                pl.BlockSpec((1, 1, D_MODEL), lambda i, pos: ((i // tiles_per_sample) * N_MOD + MOD_GATE2, 0, 0)),
                full(ln_w), full(ln_b),
            ],
            out_specs=row(D_MODEL),
            scratch_shapes=[pltpu.VMEM((2, 2 * COMB_TM, TOKEN_SUB, LANES), BF16), pltpu.SemaphoreType.DMA((2,))],
        ),
        out_shape=jax.ShapeDtypeStruct((N_TOK, D_MODEL), F32),
        compiler_params=_params(("arbitrary",)),
        name="combine",
    )(pos, ys, h, route, mod_rows, ln_w, ln_b)


def _routing_tables(route_t, counts):
    cnt = counts[0, N_GROUPS:N_ROUTE].astype(I32)
    row_end = jnp.cumsum(cnt)
    e = route_t[ROUTE_E1:ROUTE_E2 + 1].astype(I32)
    rank = route_t[ROUTE_R1:ROUTE_R2 + 1].astype(I32)
    earlier = jnp.arange(N_EXPERTS, dtype=I32)[:, None, None] < e[None]
    base = jnp.sum(jnp.where(earlier, cnt[:, None, None], 0), 0)
    pos = (base + rank).reshape(-1)
    return pos, row_end


def kernel(x, c, ctx, c_ctx, w_ada, b_ada, w_in, ret_decay, ret_gn_w, mla_q_norm, mla_kv_norm, w_uq, w_ukv, w_o,
           ln1_w, ln1_b, router_group_w, router_group_b, router_expert_w, router_expert_b, expert_w_gate,
           expert_w_up, expert_w_down, ln2_w, ln2_b):
    x2d = x.reshape(N_TOK, D_MODEL)
    ctx2d = ctx.reshape(N_CTX, D_MODEL)

    cc = jnp.zeros((SUBLANES, D_MODEL), F32).at[:BATCH].set(c).at[BATCH].set(c_ctx)
    mod = _ada(cc, w_ada[0], b_ada)
    mod_rows = mod.reshape(SUBLANES * N_MOD, 1, D_MODEL)

    proj, w_in_blocks = _inproj_latent(x2d, mod_rows, jnp.swapaxes(w_in, 1, 2))
    proj_c = _inproj_context(ctx2d, mod_rows, w_in_blocks)

    cos_r, sin_r = _rope_tables(RET_DK)
    decay_rows = jnp.broadcast_to(ret_decay[0].reshape(2 * RET_HEADS, 1, 1), (2 * RET_HEADS, 1, LANES))
    ret = _retention(proj, proj_c, jnp.asarray(cos_r), jnp.asarray(sin_r), decay_rows, ret_gn_w)

    cos_m, sin_m = _rope_tables(MLA_ROPE)
    cos_m = np.concatenate([cos_m, np.ones_like(cos_m)], 1)
    sin_m = np.concatenate([sin_m, np.zeros_like(sin_m)], 1)
    wq = w_uq[0].reshape(MLA_Q_LORA, MLA_HEADS, MLA_DQ)
    wq_pad = jnp.pad(wq, ((0, 0), (0, 0), (0, QK_PAD - MLA_DQ))).reshape(MLA_Q_LORA, MLA_HEADS * QK_PAD).astype(BF16)
    wkv = w_ukv[0].astype(BF16)
    q, k_lat, v_lat = _mla_latent(proj, mla_q_norm, mla_kv_norm, wq_pad, wkv, jnp.asarray(cos_m), jnp.asarray(sin_m))
    k_ctx, v_ctx = _mla_context(proj_c, mla_kv_norm, wkv)
    att, w_o_bf = _attention(q, k_ctx, k_lat, v_ctx, v_lat, w_o[0])

    w_route = jnp.concatenate(
        [router_group_w[0], router_expert_w[0].transpose(1, 0, 2).reshape(D_MODEL, N_EXPERTS),
         jnp.zeros((D_MODEL, LANES - N_ROUTE), F32)], 1)
    b_route = jnp.concatenate(
        [router_group_b[0], router_expert_b[0].reshape(N_EXPERTS), jnp.zeros((LANES - N_ROUTE,), F32)])[None]
    h, t_tiles, route, route_t, counts = _outproj(ret, att, x2d, w_o_bf, mod_rows, ln1_w, ln1_b,
                                                   w_route, b_route)

    pos, row_end = _routing_tables(route_t, counts)
    xs = _dispatch(pos, t_tiles)
    ys = _experts(row_end, xs, expert_w_gate[0], expert_w_up[0], expert_w_down[0])
    out = _combine(pos, ys, h, route, mod_rows, ln2_w, ln2_b)
    return out.reshape(BATCH, SEQ, D_MODEL)
```

```python
import numpy as np
import jax
import jax.numpy as jnp
from jax import lax
from jax.experimental import pallas as pl
from jax.experimental.pallas import tpu as pltpu

F32 = jnp.float32
BF16 = jnp.bfloat16
I32 = jnp.int32

D_MODEL = 2048
BATCH = 4
SEQ = 2048
GRID_W = 64
CTX_LEN = 256
N_TOK = BATCH * SEQ
N_CTX = BATCH * CTX_LEN

RET_HEADS = 8
RET_DK = 128
RET_DV = 128
RET_W = RET_HEADS * RET_DV
CHUNK = 128
N_CHUNKS = SEQ // CHUNK

MLA_HEADS = 8
MLA_Q_LORA = 512
MLA_KV_LORA = 256
MLA_NOPE = 128
MLA_ROPE = 64
MLA_DV = 128
MLA_W = MLA_HEADS * MLA_DV
MLA_DQ = MLA_NOPE + MLA_ROPE
QK_PAD = 256

IN_SIZES = (RET_HEADS * RET_DK, RET_HEADS * RET_DK, RET_W, RET_W, MLA_Q_LORA, MLA_KV_LORA, MLA_ROPE)
IN_W = sum(IN_SIZES)
OFF_Q, OFF_K, OFF_V, OFF_G, OFF_CQ, OFF_CKV, OFF_KPE = (int(v) for v in np.cumsum((0,) + IN_SIZES[:-1]))

N_GROUPS = 4
EXPERTS_PER_GROUP = 8
N_EXPERTS = N_GROUPS * EXPERTS_PER_GROUP
D_EXPERT = 512
N_ROUTE = N_GROUPS + N_EXPERTS

N_MOD = 6
MOD_SHIFT1, MOD_SCALE1, MOD_GATE1, MOD_SHIFT2, MOD_SCALE2, MOD_GATE2 = range(N_MOD)

LOG2_E = float(np.log2(np.e))
ROPE_BASE = 10000.0
EPS = 1e-6
DEPTH = 1
DEEPNORM_ALPHA = (2.0 * DEPTH) ** 0.25

LANES = 128
SUBLANES = 8
ROW_TILE = 256
N_PAIRS = 2 * N_TOK
N_TILES = N_PAIRS // ROW_TILE
TOKEN_SUB = D_MODEL // LANES

V7X_VMEM_BYTES = 64 * 1024 * 1024
VMEM_LIMIT = V7X_VMEM_BYTES - V7X_VMEM_BYTES // 8


def _params(sem):
    return pltpu.CompilerParams(dimension_semantics=sem, vmem_limit_bytes=VMEM_LIMIT)


def _silu(x):
    return x * (1.0 / (1.0 + jnp.exp(-x)))


def _plain_norm(x):
    mu = jnp.mean(x, -1, keepdims=True)
    xc = x - mu
    var = jnp.mean(xc * xc, -1, keepdims=True)
    return xc * lax.rsqrt(var + EPS)


def _to_token_tiles(x):
    return x.astype(BF16).reshape(x.shape[0], TOKEN_SUB, LANES)


def _from_token_tiles(x):
    return x.reshape(x.shape[0], D_MODEL)


ADA_TN = 1024


def _ada_kernel(cc_ref, w_ref, b_ref, o_ref):
    s = _silu(cc_ref[...])
    o_ref[...] = jnp.dot(s, w_ref[...], preferred_element_type=F32,
                         precision=lax.Precision.HIGHEST) + b_ref[...]


def _ada(cc, w_ada, b_ada):
    n = w_ada.shape[1]
    return pl.pallas_call(
        _ada_kernel,
        grid=(n // ADA_TN,),
        in_specs=[
            pl.BlockSpec((SUBLANES, D_MODEL), lambda j: (0, 0)),
            pl.BlockSpec((D_MODEL, ADA_TN), lambda j: (0, j)),
            pl.BlockSpec((1, ADA_TN), lambda j: (0, j)),
        ],
        out_specs=pl.BlockSpec((SUBLANES, ADA_TN), lambda j: (0, j)),
        out_shape=jax.ShapeDtypeStruct((SUBLANES, n), F32),
        compiler_params=_params(("arbitrary",)),
        name="ada",
    )(cc, w_ada, b_ada)


INPROJ_TM = 1024
INPROJ_TN = 1024
assert OFF_K == INPROJ_TN and OFF_V == 2 * INPROJ_TN and OFF_CQ == 4 * INPROJ_TN
CTX_OFF_K, CTX_OFF_V = 0, INPROJ_TN
CTX_OFF_CKV = 2 * INPROJ_TN + (OFF_CKV - OFF_CQ)
CTX_OFF_KPE = 2 * INPROJ_TN + (OFF_KPE - OFF_CQ)


INPROJ_SUB = 256
INPROJ_BLOCKS = -(-IN_W // INPROJ_TN)


def _inproj_products(x_ref, shift_ref, scale_ref, w, o_ref, xn_ref):
    first = pl.program_id(1) == 0

    @pl.when(first)
    def _():
        for s in range(INPROJ_TM // INPROJ_SUB):
            rows = slice(s * INPROJ_SUB, (s + 1) * INPROJ_SUB)
            y = (_plain_norm(x_ref[rows, :]) * (1.0 + scale_ref[0]) + shift_ref[0]).astype(BF16)
            xn_ref[rows, :] = y
            o_ref[rows, :] = _dot_nt(y, w[...]).astype(BF16)

    @pl.when(jnp.logical_not(first))
    def _():
        o_ref[...] = _dot_nt(xn_ref[...], w[...]).astype(BF16)


def _inproj_latent_kernel(x_ref, shift_ref, scale_ref, wf_ref, o_ref, wbf_ref, xn_ref, wbuf_ref, in_sem, out_sem):
    i = pl.program_id(0)
    k = pl.program_id(1)
    t = i * INPROJ_BLOCKS + k
    slot = t % 2
    n_steps = pl.num_programs(0) * INPROJ_BLOCKS

    def out_copy(block, s):
        return pltpu.make_async_copy(wbuf_ref.at[s], wbf_ref.at[block], out_sem.at[s])

    def in_copy(block, s):
        return pltpu.make_async_copy(wbf_ref.at[block], wbuf_ref.at[s], in_sem.at[s])

    @pl.when(i == 0)
    def _():
        @pl.when(k >= 2)
        def _():
            out_copy(k - 2, slot).wait()

        col = lax.broadcasted_iota(I32, (INPROJ_TN, D_MODEL), 0)
        wbuf_ref[slot] = jnp.where(col < IN_W - k * INPROJ_TN, wf_ref[0], 0.0).astype(BF16)
        out_copy(k, slot).start()

    @pl.when((t + 1 >= INPROJ_BLOCKS) & (t + 1 < n_steps))
    def _():
        @pl.when(t == INPROJ_BLOCKS - 1)
        def _():
            out_copy(INPROJ_BLOCKS - 2, 1 - slot).wait()

        @pl.when(t == INPROJ_BLOCKS)
        def _():
            out_copy(INPROJ_BLOCKS - 1, 1 - slot).wait()

        in_copy((k + 1) % INPROJ_BLOCKS, 1 - slot).start()

    @pl.when(i > 0)
    def _():
        in_copy(k, slot).wait()

    _inproj_products(x_ref, shift_ref, scale_ref, wbuf_ref.at[slot], o_ref, xn_ref)


def _inproj_context_kernel(x_ref, shift_ref, scale_ref, w_ref, o_ref, xn_ref):
    _inproj_products(x_ref, shift_ref, scale_ref, w_ref.at[0], o_ref, xn_ref)


def _inproj_mod_map(j, tiles_per_sample, sample_row0):
    return lambda i, k: ((sample_row0 + i // tiles_per_sample) * N_MOD + j, 0, 0)


def _inproj_latent(x2d, mod_rows, w_in_t):
    assert INPROJ_BLOCKS % 2 == 1
    last = INPROJ_BLOCKS - 1
    return pl.pallas_call(
        _inproj_latent_kernel,
        grid=(N_TOK // INPROJ_TM, INPROJ_BLOCKS),
        in_specs=[
            pl.BlockSpec((INPROJ_TM, D_MODEL), lambda i, k: (i, 0)),
            pl.BlockSpec((1, 1, D_MODEL), _inproj_mod_map(MOD_SHIFT1, SEQ // INPROJ_TM, 0)),
            pl.BlockSpec((1, 1, D_MODEL), _inproj_mod_map(MOD_SCALE1, SEQ // INPROJ_TM, 0)),
            pl.BlockSpec((1, INPROJ_TN, D_MODEL), lambda i, k: (0, jnp.where(i == 0, k, last), 0)),
        ],
        out_specs=[pl.BlockSpec((INPROJ_TM, INPROJ_TN), lambda i, k: (i, k)), pl.BlockSpec(memory_space=pl.ANY)],
        out_shape=[jax.ShapeDtypeStruct((N_TOK, IN_W), BF16),
                   jax.ShapeDtypeStruct((INPROJ_BLOCKS, INPROJ_TN, D_MODEL), BF16)],
        scratch_shapes=[pltpu.VMEM((INPROJ_TM, D_MODEL), BF16), pltpu.VMEM((2, INPROJ_TN, D_MODEL), BF16),
                        pltpu.SemaphoreType.DMA((2,)), pltpu.SemaphoreType.DMA((2,))],
        compiler_params=_params(("arbitrary", "arbitrary")),
        name="inproj",
    )(x2d, mod_rows, mod_rows, w_in_t)


def _inproj_context(ctx2d, mod_rows, w_blocks):
    last = INPROJ_BLOCKS - 1
    out_w = 2 * INPROJ_TN + (IN_W - last * INPROJ_TN)
    return pl.pallas_call(
        _inproj_context_kernel,
        grid=(N_CTX // INPROJ_TM, 3),
        in_specs=[
            pl.BlockSpec((INPROJ_TM, D_MODEL), lambda i, k: (i, 0)),
            pl.BlockSpec((1, 1, D_MODEL), _inproj_mod_map(MOD_SHIFT1, N_CTX // INPROJ_TM, BATCH)),
            pl.BlockSpec((1, 1, D_MODEL), _inproj_mod_map(MOD_SCALE1, N_CTX // INPROJ_TM, BATCH)),
            pl.BlockSpec((1, INPROJ_TN, D_MODEL), lambda i, k: (jnp.where(k == 2, last, k + 1), 0, 0)),
        ],
        out_specs=pl.BlockSpec((INPROJ_TM, INPROJ_TN), lambda i, k: (i, k)),
        out_shape=jax.ShapeDtypeStruct((N_CTX, out_w), BF16),
        scratch_shapes=[pltpu.VMEM((INPROJ_TM, D_MODEL), BF16)],
        compiler_params=_params(("arbitrary", "arbitrary")),
        name="inproj_ctx",
    )(ctx2d, mod_rows, mod_rows, w_blocks)


def _rope_tables(width):
    half = width // 2
    quarter = half // 2
    inv_freq = ROPE_BASE ** (-np.arange(0, half, 2, dtype=np.float64) / half)
    t = np.arange(SEQ)
    cos_parts, sin_parts = [], []
    for pos in (t // GRID_W, t % GRID_W):
        ang = pos[:, None].astype(np.float64) * inv_freq[None, :]
        c, s = np.cos(ang), np.sin(ang)
        cos_parts += [c, c]
        sin_parts += [-s, s]
    assert cos_parts[0].shape[1] == quarter
    return (np.concatenate(cos_parts, 1).astype(np.float32), np.concatenate(sin_parts, 1).astype(np.float32))


def _rope(x, cos, sin, quarter):
    lane = lax.broadcasted_iota(I32, x.shape, 1)
    first = (lane % (2 * quarter)) < quarter
    swapped = jnp.where(first, pltpu.roll(x, LANES - quarter, 1), pltpu.roll(x, quarter, 1))
    return x * cos + swapped * sin


def _dot_tn(a, b):
    return lax.dot_general(a, b, (((0,), (0,)), ((), ())), preferred_element_type=F32)


def _dot_nt(a, b):
    return lax.dot_general(a, b, (((1,), (1,)), ((), ())), preferred_element_type=F32)


RET_HG = 4


def _retention_kernel(q_ref, k_ref, v_ref, g_ref, kc_ref, vc_ref, cos_ref, sin_ref, df_ref, db_ref, gn_ref,
                      o_ref, qs_ref, ks_ref, st_ref):
    for hh in range(RET_HG):
        _retention_head(hh, slice(hh * LANES, (hh + 1) * LANES), q_ref, k_ref, v_ref, g_ref, kc_ref, vc_ref,
                        cos_ref, sin_ref, df_ref, db_ref, gn_ref, o_ref, qs_ref, ks_ref, st_ref)


def _retention_head(hh, hs, q_ref, k_ref, v_ref, g_ref, kc_ref, vc_ref, cos_ref, sin_ref, df_ref, db_ref, gn_ref,
                    o_ref, qs_ref, ks_ref, st_ref):
    k_scale = RET_DK ** -0.5
    lgf = jax.nn.log_sigmoid(df_ref[hh])
    lgb = jax.nn.log_sigmoid(db_ref[hh])

    cos = cos_ref[...]
    sin = sin_ref[...]
    qs_ref[hh] = _rope(q_ref[:, hs].astype(F32), cos, sin, RET_DK // 4).astype(BF16)
    ks_ref[hh] = _rope(k_ref[:, hs].astype(F32), cos, sin, RET_DK // 4) * k_scale

    rowi = lax.broadcasted_iota(I32, (CHUNK, LANES), 0).astype(F32)
    coli = lax.broadcasted_iota(I32, (CHUNK, LANES), 1).astype(F32)
    diff = rowi - coli
    decay = jnp.exp(jnp.where(diff >= 0, lgf * diff, -lgb * diff)) * jnp.where(diff == 0, 2.0, 1.0)
    zeta_f = jnp.exp(lgf * (CHUNK - 1.0 - rowi))
    eta_b = jnp.exp(lgb * rowi)
    xi_f = jnp.exp(lgf * (rowi + 1.0))
    xi_b = jnp.exp(lgb * (CHUNK - rowi))
    cdec_f = jnp.exp(lgf * float(CHUNK))
    cdec_b = jnp.exp(lgb * float(CHUNK))

    crow = lax.broadcasted_iota(I32, (CTX_LEN, LANES), 0).astype(F32)
    kc = kc_ref[:, hs].astype(F32) * k_scale
    vc = vc_ref[:, hs]
    s_f = _dot_tn((kc * jnp.exp(lgf * (CTX_LEN - 1.0 - crow))).astype(BF16), vc)
    s_b = _dot_tn((kc * jnp.exp(lgb * crow)).astype(BF16), vc)

    upd_f, upd_b = [], []
    for i in range(N_CHUNKS):
        rows = pl.ds(i * CHUNK, CHUNK)
        kch = ks_ref[hh, rows, :]
        vch = v_ref[rows, hs]
        upd_f.append(_dot_tn((kch * zeta_f).astype(BF16), vch))
        upd_b.append(_dot_tn((kch * eta_b).astype(BF16), vch))
    state = s_f
    for i in range(N_CHUNKS):
        st_ref[hh, i, :, :RET_DV] = state.astype(BF16)
        state = cdec_f * state + upd_f[i]
    state = s_b
    for i in reversed(range(N_CHUNKS)):
        st_ref[hh, i, :, RET_DV:] = state.astype(BF16)
        state = cdec_b * state + upd_b[i]

    gn_w = gn_ref[:, hs]
    for i in range(N_CHUNKS):
        rows = pl.ds(i * CHUNK, CHUNK)
        qch = qs_ref[hh, rows, :]
        scores = _dot_nt(qch, ks_ref[hh, rows, :].astype(BF16)) * decay
        o = jnp.dot(scores.astype(BF16), v_ref[rows, hs], preferred_element_type=F32)
        cross = jnp.dot(qch, st_ref[hh, i], preferred_element_type=F32)
        o = o + xi_f * cross[:, :RET_DV] + xi_b * cross[:, RET_DV:]
        y = _plain_norm(o) * gn_w
        o_ref[rows, hs] = (_silu(g_ref[rows, hs].astype(F32)) * y).astype(BF16)


def _retention(proj, proj_c, cos, sin, decay_rows, gn_w):
    width = RET_HG * LANES
    groups = RET_HEADS // RET_HG
    blk = lambda off: pl.BlockSpec((SEQ, width), lambda b, hg: (b, off // width + hg))
    blk_c = lambda off: pl.BlockSpec((CTX_LEN, width), lambda b, hg: (b, off // width + hg))
    table = pl.BlockSpec((SEQ, LANES), lambda b, hg: (0, 0))
    return pl.pallas_call(
        _retention_kernel,
        grid=(BATCH, groups),
        in_specs=[
            blk(OFF_Q), blk(OFF_K), blk(OFF_V), blk(OFF_G), blk_c(CTX_OFF_K), blk_c(CTX_OFF_V), table, table,
            pl.BlockSpec((RET_HG, 1, LANES), lambda b, hg: (hg, 0, 0)),
            pl.BlockSpec((RET_HG, 1, LANES), lambda b, hg: (groups + hg, 0, 0)),
            pl.BlockSpec((1, width), lambda b, hg: (0, hg)),
        ],
        out_specs=pl.BlockSpec((SEQ, width), lambda b, hg: (b, hg)),
        out_shape=jax.ShapeDtypeStruct((N_TOK, RET_W), BF16),
        scratch_shapes=[
            pltpu.VMEM((RET_HG, SEQ, RET_DK), BF16),
            pltpu.VMEM((RET_HG, SEQ, RET_DK), F32),
            pltpu.VMEM((RET_HG, N_CHUNKS, RET_DK, 2 * RET_DV), BF16),
        ],
        compiler_params=_params(("arbitrary", "arbitrary")),
        name="retention",
    )(proj, proj, proj, proj, proj_c, proj_c, cos, sin, decay_rows, decay_rows, gn_w)


MLA_TM = 512


def _rms_norm(x, w):
    return x * lax.rsqrt(jnp.mean(x * x, -1, keepdims=True) + EPS) * w


def _mla_kv(ckv_ref, kpe_ref, kvn_ref, wkv_ref, cos_ref, sin_ref, k_ref, v_ref, rotate):
    ckv = _rms_norm(ckv_ref[...].astype(F32), kvn_ref[...]).astype(BF16)
    kv = jnp.dot(ckv, wkv_ref[...], preferred_element_type=F32)
    lane = lax.broadcasted_iota(I32, (ckv.shape[0], LANES), 1)
    kpe = jnp.where(lane < MLA_ROPE, kpe_ref[...].astype(F32), 0.0)
    if rotate:
        kpe = _rope(kpe, cos_ref[...], sin_ref[...], MLA_ROPE // 4)
    kpe = kpe.astype(BF16)
    for h in range(MLA_HEADS):
        k_ref[:, h * QK_PAD:h * QK_PAD + MLA_NOPE] = kv[:, 2 * h * LANES:(2 * h + 1) * LANES].astype(BF16)
        k_ref[:, h * QK_PAD + MLA_NOPE:(h + 1) * QK_PAD] = kpe
        v_ref[:, h * MLA_DV:(h + 1) * MLA_DV] = kv[:, (2 * h + 1) * LANES:(2 * h + 2) * LANES].astype(BF16)


def _mla_latent_kernel(cq_ref, ckv_ref, kpe_ref, qn_ref, kvn_ref, wq_ref, wkv_ref, cos_ref, sin_ref,
                       q_ref, k_ref, v_ref):
    cq = _rms_norm(cq_ref[...].astype(F32), qn_ref[...]).astype(BF16)
    q = jnp.dot(cq, wq_ref[...], preferred_element_type=F32)
    cos = cos_ref[...]
    sin = sin_ref[...]
    scale = MLA_DQ ** -0.5 * LOG2_E
    for h in range(MLA_HEADS):
        lo = h * QK_PAD
        q_ref[:, lo:lo + MLA_NOPE] = (q[:, lo:lo + MLA_NOPE] * scale).astype(BF16)
        qpe = _rope(q[:, lo + MLA_NOPE:lo + QK_PAD], cos, sin, MLA_ROPE // 4)
        q_ref[:, lo + MLA_NOPE:lo + QK_PAD] = (qpe * scale).astype(BF16)
    _mla_kv(ckv_ref, kpe_ref, kvn_ref, wkv_ref, cos_ref, sin_ref, k_ref, v_ref, rotate=True)


def _mla_context_kernel(ckv_ref, kpe_ref, kvn_ref, wkv_ref, k_ref, v_ref):
    _mla_kv(ckv_ref, kpe_ref, kvn_ref, wkv_ref, None, None, k_ref, v_ref, rotate=False)


def _mla_latent(proj, q_norm, kv_norm, wq_pad, wkv, cos, sin):
    row = lambda w, off: pl.BlockSpec((MLA_TM, w), lambda i: (i, off // w))
    full = lambda a: pl.BlockSpec(a.shape, lambda i: (0, 0))
    table = pl.BlockSpec((MLA_TM, LANES), lambda i: (i % (SEQ // MLA_TM), 0))
    return pl.pallas_call(
        _mla_latent_kernel,
        grid=(N_TOK // MLA_TM,),
        in_specs=[row(MLA_Q_LORA, OFF_CQ), row(MLA_KV_LORA, OFF_CKV), row(LANES, OFF_KPE),
                  full(q_norm), full(kv_norm), full(wq_pad), full(wkv), table, table],
        out_specs=[
            pl.BlockSpec((MLA_TM, MLA_HEADS * QK_PAD), lambda i: (i, 0)),
            pl.BlockSpec((MLA_TM, MLA_HEADS * QK_PAD), lambda i: (i, 0)),
            pl.BlockSpec((MLA_TM, MLA_W), lambda i: (i, 0)),
        ],
        out_shape=[
            jax.ShapeDtypeStruct((N_TOK, MLA_HEADS * QK_PAD), BF16),
            jax.ShapeDtypeStruct((N_TOK, MLA_HEADS * QK_PAD), BF16),
            jax.ShapeDtypeStruct((N_TOK, MLA_W), BF16),
        ],
        compiler_params=_params(("arbitrary",)),
        name="mla_latent",
    )(proj, proj, proj, q_norm, kv_norm, wq_pad, wkv, cos, sin)


def _mla_context(proj_c, kv_norm, wkv):
    row = lambda w, off: pl.BlockSpec((MLA_TM, w), lambda i: (i, off // w))
    full = lambda a: pl.BlockSpec(a.shape, lambda i: (0, 0))
    return pl.pallas_call(
        _mla_context_kernel,
        grid=(N_CTX // MLA_TM,),
        in_specs=[row(MLA_KV_LORA, CTX_OFF_CKV), row(LANES, CTX_OFF_KPE), full(kv_norm), full(wkv)],
        out_specs=[
            pl.BlockSpec((MLA_TM, MLA_HEADS * QK_PAD), lambda i: (i, 0)),
            pl.BlockSpec((MLA_TM, MLA_W), lambda i: (i, 0)),
        ],
        out_shape=[
            jax.ShapeDtypeStruct((N_CTX, MLA_HEADS * QK_PAD), BF16),
            jax.ShapeDtypeStruct((N_CTX, MLA_W), BF16),
        ],
        compiler_params=_params(("arbitrary",)),
        name="mla_context",
    )(proj_c, proj_c, kv_norm, wkv)


ATT_TQ = 512


def _attention_kernel(q_ref, kc_ref, kl_ref, vc_ref, vl_ref, wo_ref, o_ref, wo_bf_ref, vext_ref):
    wo_bf_ref[...] = wo_ref[...].astype(BF16)

    @pl.when(pl.program_id(1) == 0)
    def _():
        lane = lax.broadcasted_iota(I32, (CTX_LEN + SEQ, MLA_DV), 1)
        ones_col = jnp.where(lane == 0, 1.0, 0.0).astype(BF16)
        for h in range(MLA_HEADS):
            dv = slice(h * MLA_DV, (h + 1) * MLA_DV)
            vext_ref[h, :CTX_LEN, :MLA_DV] = vc_ref[:, dv]
            vext_ref[h, CTX_LEN:, :MLA_DV] = vl_ref[:, dv]
            vext_ref[h, :, MLA_DV:] = ones_col

    for h in range(MLA_HEADS):
        qk = slice(h * QK_PAD, (h + 1) * QK_PAD)
        q = q_ref[:, qk]
        s_c = _dot_nt(q, kc_ref[:, qk])
        s_l = _dot_nt(q, kl_ref[:, qk])
        m = jnp.maximum(jnp.max(s_c, -1, keepdims=True), jnp.max(s_l, -1, keepdims=True))
        p_c = jnp.exp2(s_c - m).astype(BF16)
        p_l = jnp.exp2(s_l - m).astype(BF16)
        o = (jnp.dot(p_c, vext_ref[h, :CTX_LEN, :], preferred_element_type=F32)
             + jnp.dot(p_l, vext_ref[h, CTX_LEN:, :], preferred_element_type=F32))
        o_ref[:, h * MLA_DV:(h + 1) * MLA_DV] = (o[:, :MLA_DV] / o[:, MLA_DV:MLA_DV + 1]).astype(BF16)


def _attention(q, k_ctx, k_lat, v_ctx, v_lat, w_o):
    tiles = SEQ // ATT_TQ
    band = w_o.shape[0] // (BATCH * tiles)
    step = lambda b, i: (b * tiles + i, 0)
    return pl.pallas_call(
        _attention_kernel,
        grid=(BATCH, tiles),
        in_specs=[
            pl.BlockSpec((ATT_TQ, MLA_HEADS * QK_PAD), step),
            pl.BlockSpec((CTX_LEN, MLA_HEADS * QK_PAD), lambda b, i: (b, 0)),
            pl.BlockSpec((SEQ, MLA_HEADS * QK_PAD), lambda b, i: (b, 0)),
            pl.BlockSpec((CTX_LEN, MLA_W), lambda b, i: (b, 0)),
            pl.BlockSpec((SEQ, MLA_W), lambda b, i: (b, 0)),
            pl.BlockSpec((band, D_MODEL), step),
        ],
        out_specs=[pl.BlockSpec((ATT_TQ, MLA_W), step), pl.BlockSpec((band, D_MODEL), step)],
        out_shape=[jax.ShapeDtypeStruct((N_TOK, MLA_W), BF16), jax.ShapeDtypeStruct(w_o.shape, BF16)],
        scratch_shapes=[pltpu.VMEM((MLA_HEADS, CTX_LEN + SEQ, 2 * MLA_DV), BF16)],
        compiler_params=_params(("arbitrary", "arbitrary")),
        name="attention",
    )(q, k_ctx, k_lat, v_ctx, v_lat, w_o)


OUT_TM = 512
OUT_SUB = 256
ROUTE_E1, ROUTE_E2, ROUTE_R1, ROUTE_R2, ROUTE_W1, ROUTE_W2 = range(6)


def _outproj_kernel(ret_ref, att_ref, x_ref, wo_ref, gate_ref, shift_ref, scale_ref, lnw_ref, lnb_ref,
                    wr_ref, br_ref, h_ref, t_ref, route_ref, route_t_ref, count_ref, carry_ref, wsplit_ref):
    @pl.when(pl.program_id(0) == 0)
    def _():
        carry_ref[...] = jnp.zeros_like(carry_ref)
        w = wr_ref[...]
        w_hi = w.astype(BF16)
        wsplit_ref[:, :LANES] = w_hi
        wsplit_ref[:, LANES:] = (w - w_hi.astype(F32)).astype(BF16)

    subtiles = [slice(s * OUT_SUB, (s + 1) * OUT_SUB) for s in range(OUT_TM // OUT_SUB)]
    mixes = [jnp.dot(ret_ref[rows, :], wo_ref[:RET_W, :], preferred_element_type=F32)
             + jnp.dot(att_ref[rows, :], wo_ref[RET_W:, :], preferred_element_type=F32) for rows in subtiles]
    logits = [_outproj_norms(rows, mix, x_ref, gate_ref, shift_ref, scale_ref, lnw_ref, lnb_ref, br_ref, h_ref, t_ref,
                             wsplit_ref) for rows, mix in zip(subtiles, mixes)]
    _route(jnp.concatenate(logits, 0), carry_ref, count_ref, route_ref, route_t_ref)


def _outproj_norms(rows, mix, x_ref, gate_ref, shift_ref, scale_ref, lnw_ref, lnb_ref, br_ref, h_ref, t_ref,
                   wsplit_ref):
    h = _plain_norm(DEEPNORM_ALPHA * x_ref[rows, :] + gate_ref[0] * mix) * lnw_ref[...] + lnb_ref[...]
    h_ref[rows, :] = h
    t = _plain_norm(h) * (1.0 + scale_ref[0]) + shift_ref[0]
    t_ref[rows] = _to_token_tiles(t)

    t_hi = t.astype(BF16)
    t_lo = (t - t_hi.astype(F32)).astype(BF16)
    main = jnp.dot(t_hi, wsplit_ref[...], preferred_element_type=F32)
    corr = jnp.dot(t_lo, wsplit_ref[:, :LANES], preferred_element_type=F32)
    return main[:, :LANES] + (main[:, LANES:] + corr) + br_ref[...]


def _route(logits, carry_ref, count_ref, route_ref, route_t_ref):
    lane = lax.broadcasted_iota(I32, logits.shape, 1).astype(F32)
    neg = -jnp.inf
    big = float(LANES)

    def first_lane_of(mask):
        return jnp.min(jnp.where(mask, lane, big), -1, keepdims=True)

    is_group = lane < N_GROUPS
    gl = jnp.where(is_group, logits, neg)
    g_max = jnp.max(gl, -1, keepdims=True)
    g_idx = first_lane_of(is_group & (gl == g_max))
    g_prob = 1.0 / jnp.sum(jnp.where(is_group, jnp.exp(logits - g_max), 0.0), -1, keepdims=True)

    lo = N_GROUPS + g_idx * EXPERTS_PER_GROUP
    in_group = (lane >= lo) & (lane < lo + EXPERTS_PER_GROUP)
    el = jnp.where(in_group, logits, neg)
    v1 = jnp.max(el, -1, keepdims=True)
    i1 = first_lane_of(in_group & (el == v1))
    rest = in_group & (lane != i1)
    el2 = jnp.where(rest, logits, neg)
    v2 = jnp.max(el2, -1, keepdims=True)
    i2 = first_lane_of(rest & (el2 == v2))
    d = jnp.exp(v2 - v1)
    w1 = g_prob / (1.0 + d)
    w2 = g_prob * d / (1.0 + d)

    onehot = jnp.where((lane == i1) | (lane == i2), 1.0, 0.0)
    r = lax.broadcasted_iota(I32, (OUT_TM, OUT_TM), 0)
    c = lax.broadcasted_iota(I32, (OUT_TM, OUT_TM), 1)
    tri = jnp.where(c < r, 1.0, 0.0).astype(BF16)
    carry = carry_ref[0:1, :]
    before = jnp.dot(tri, onehot.astype(BF16), preferred_element_type=F32) + carry
    r1 = jnp.sum(jnp.where(lane == i1, before, 0.0), -1, keepdims=True)
    r2 = jnp.sum(jnp.where(lane == i2, before, 0.0), -1, keepdims=True)
    carry = carry + jnp.sum(onehot, 0, keepdims=True)
    carry_ref[...] = jnp.broadcast_to(carry, carry_ref.shape)
    count_ref[...] = jnp.broadcast_to(carry, count_ref.shape)

    rec = jnp.zeros_like(logits)
    for slot, val in ((ROUTE_E1, i1 - N_GROUPS), (ROUTE_E2, i2 - N_GROUPS), (ROUTE_R1, r1), (ROUTE_R2, r2),
                      (ROUTE_W1, w1), (ROUTE_W2, w2)):
        rec = jnp.where(lane == slot, val, rec)
    route_ref[...] = rec
    route_t_ref[...] = rec.T[:SUBLANES, :]


def _outproj(ret, att, x2d, wo_bf, mod_rows, ln_w, ln_b, w_route, b_route):
    tiles_per_sample = SEQ // OUT_TM
    row = lambda w: pl.BlockSpec((OUT_TM, w), lambda i: (i, 0))
    full = lambda a: pl.BlockSpec(a.shape, lambda i: (0, 0))
    mod = lambda j: pl.BlockSpec((1, 1, D_MODEL), lambda i: ((i // tiles_per_sample) * N_MOD + j, 0, 0))
    return pl.pallas_call(
        _outproj_kernel,
        grid=(N_TOK // OUT_TM,),
        in_specs=[row(RET_W), row(MLA_W), row(D_MODEL), full(wo_bf), mod(MOD_GATE1), mod(MOD_SHIFT2), mod(MOD_SCALE2),
                  full(ln_w), full(ln_b), full(w_route), full(b_route)],
        out_specs=[row(D_MODEL), pl.BlockSpec((OUT_TM, TOKEN_SUB, LANES), lambda i: (i, 0, 0)), row(LANES),
                   pl.BlockSpec((SUBLANES, OUT_TM), lambda i: (0, i)),
                   pl.BlockSpec((SUBLANES, LANES), lambda i: (0, 0))],
        out_shape=[
            jax.ShapeDtypeStruct((N_TOK, D_MODEL), F32),
            jax.ShapeDtypeStruct((N_TOK, TOKEN_SUB, LANES), BF16),
            jax.ShapeDtypeStruct((N_TOK, LANES), F32),
            jax.ShapeDtypeStruct((SUBLANES, N_TOK), F32),
            jax.ShapeDtypeStruct((SUBLANES, LANES), F32),
        ],
        scratch_shapes=[pltpu.VMEM((SUBLANES, LANES), F32), pltpu.VMEM((D_MODEL, 2 * LANES), BF16)],
        compiler_params=_params(("arbitrary",)),
        name="outproj_route",
    )(ret, att, x2d, wo_bf, mod_rows, mod_rows, mod_rows, ln_w, ln_b, w_route, b_route)


DISPATCH_TM = 1024


def _dispatch_kernel(pos_ref, t_ref, xs_ref, sem):
    i = pl.program_id(0)

    def copy(slot, j):
        return pltpu.make_async_copy(t_ref.at[j], xs_ref.at[pos_ref[slot * N_TOK + i * DISPATCH_TM + j]], sem)

    for slot in range(2):
        lax.fori_loop(0, DISPATCH_TM, lambda j, c, slot=slot: (copy(slot, j).start(priority=slot), c)[1], 0,
                      unroll=8)
    for slot in range(2):
        pltpu.make_async_copy(t_ref, xs_ref.at[pl.ds(0, DISPATCH_TM)], sem).wait()


def _dispatch(pos, t_tiles):
    return pl.pallas_call(
        _dispatch_kernel,
        grid_spec=pltpu.PrefetchScalarGridSpec(
            num_scalar_prefetch=1,
            grid=(N_TOK // DISPATCH_TM,),
            in_specs=[pl.BlockSpec((DISPATCH_TM, TOKEN_SUB, LANES), lambda i, pos: (i, 0, 0))],
            out_specs=pl.BlockSpec(memory_space=pl.ANY),
            scratch_shapes=[pltpu.SemaphoreType.DMA(())],
        ),
        out_shape=jax.ShapeDtypeStruct((N_PAIRS, TOKEN_SUB, LANES), BF16),
        compiler_params=_params(("arbitrary",)),
        name="dispatch",
    )(pos, t_tiles)


def _experts_kernel(rend_ref, xs_ref, wg_ref, wu_ref, wd_ref, ys_ref, wgu_ref, wdn_ref, xbuf_ref, ybuf_ref,
                    xsem, ysem):
    e = pl.program_id(0)
    lo = jnp.where(e == 0, 0, rend_ref[jnp.maximum(e - 1, 0)])
    hi = rend_ref[e]

    def x_copy(g):
        return pltpu.make_async_copy(xs_ref.at[pl.ds(g * ROW_TILE, ROW_TILE)], xbuf_ref.at[g % 2], xsem.at[g % 2])

    def y_copy(g):
        return pltpu.make_async_copy(ybuf_ref.at[g % 2], ys_ref.at[pl.ds(g * ROW_TILE, ROW_TILE)], ysem.at[g % 2])

    @pl.when(e == 0)
    def _():
        x_copy(0).start()

    @pl.when(hi > lo)
    def _():
        wgu_ref[:, :D_EXPERT] = wg_ref[0].astype(BF16)
        wgu_ref[:, D_EXPERT:] = wu_ref[0].astype(BF16)
        wdn_ref[...] = wd_ref[0].astype(BF16)

        def tile(g, c):
            row0 = g * ROW_TILE
            owns_first_row = lo <= row0
            owns_last_row = hi >= row0 + ROW_TILE

            @pl.when(owns_first_row)
            def _():
                @pl.when(g + 1 < N_TILES)
                def _():
                    x_copy(g + 1).start()

                x_copy(g).wait()

            gu = jnp.dot(_from_token_tiles(xbuf_ref[g % 2]), wgu_ref[...], preferred_element_type=F32)
            hid = _silu(gu[:, :D_EXPERT]) * gu[:, D_EXPERT:]
            y = jnp.dot(hid.astype(BF16), wdn_ref[...], preferred_element_type=F32)

            @pl.when(owns_first_row)
            def _():
                @pl.when(g >= 2)
                def _():
                    y_copy(g - 2).wait()

                ybuf_ref[g % 2] = _to_token_tiles(y)

            @pl.when(jnp.logical_not(owns_first_row))
            def _():
                row = lax.broadcasted_iota(I32, y.shape, 0)
                earlier = _from_token_tiles(ybuf_ref[g % 2]).astype(F32)
                ybuf_ref[g % 2] = _to_token_tiles(jnp.where(row >= lo - row0, y, earlier))

            @pl.when(owns_last_row)
            def _():
                y_copy(g).start()

            return c

        lax.fori_loop(lo // ROW_TILE, (hi - 1) // ROW_TILE + 1, tile, 0)

    @pl.when(e == N_EXPERTS - 1)
    def _():
        y_copy(N_TILES - 2).wait()
        y_copy(N_TILES - 1).wait()


def _experts(row_end, xs, w_gate, w_up, w_down):
    w_gate = w_gate.reshape(N_EXPERTS, D_MODEL, D_EXPERT)
    w_up = w_up.reshape(N_EXPERTS, D_MODEL, D_EXPERT)
    w_down = w_down.reshape(N_EXPERTS, D_EXPERT, D_MODEL)
    expert = lambda e, rend: (e, 0, 0)
    tile_buf = pltpu.VMEM((2, ROW_TILE, TOKEN_SUB, LANES), BF16)
    return pl.pallas_call(
        _experts_kernel,
        grid_spec=pltpu.PrefetchScalarGridSpec(
            num_scalar_prefetch=1,
            grid=(N_EXPERTS,),
            in_specs=[
                pl.BlockSpec(memory_space=pl.ANY),
                pl.BlockSpec((1, D_MODEL, D_EXPERT), expert),
                pl.BlockSpec((1, D_MODEL, D_EXPERT), expert),
                pl.BlockSpec((1, D_EXPERT, D_MODEL), expert),
            ],
            out_specs=pl.BlockSpec(memory_space=pl.ANY),
            scratch_shapes=[pltpu.VMEM((D_MODEL, 2 * D_EXPERT), BF16), pltpu.VMEM((D_EXPERT, D_MODEL), BF16),
                            tile_buf, tile_buf, pltpu.SemaphoreType.DMA((2,)), pltpu.SemaphoreType.DMA((2,))],
        ),
        out_shape=jax.ShapeDtypeStruct((N_PAIRS, TOKEN_SUB, LANES), BF16),
        compiler_params=_params(("arbitrary",)),
        name="experts",
    )(row_end, xs, w_gate, w_up, w_down)


COMB_TM = 512


def _combine_kernel(pos_ref, ys_ref, h_ref, route_ref, gate_ref, lnw_ref, lnb_ref, o_ref, buf_ref, sem):
    i = pl.program_id(0)
    n = pl.num_programs(0)

    def copy(step, slot, pair, j):
        src = pos_ref[pair * N_TOK + step * COMB_TM + j]
        return pltpu.make_async_copy(ys_ref.at[src], buf_ref.at[slot, pair * COMB_TM + j], sem.at[slot])

    def start_all(step, slot):
        for pair in range(2):
            lax.fori_loop(0, COMB_TM, lambda j, c, pair=pair: (copy(step, slot, pair, j).start(priority=pair), c)[1], 0,
                          unroll=8)

    def wait_all(step, slot):
        pltpu.make_async_copy(ys_ref.at[pl.ds(0, 2 * COMB_TM)], buf_ref.at[slot], sem.at[slot]).wait()

    @pl.when(i == 0)
    def _():
        start_all(0, 0)

    slot = i % 2
    wait_all(i, slot)

    nxt = jnp.minimum(i + 1, n - 1)
    for pair in range(2):
        for j in range(COMB_TM):
            copy(nxt, 1 - slot, pair, j).start(priority=pair)

    y1 = _from_token_tiles(buf_ref[slot, :COMB_TM]).astype(F32)
    y2 = _from_token_tiles(buf_ref[slot, COMB_TM:]).astype(F32)
    y = route_ref[:, ROUTE_W1:ROUTE_W1 + 1] * y1 + route_ref[:, ROUTE_W2:ROUTE_W2 + 1] * y2
    z = DEEPNORM_ALPHA * h_ref[...] + gate_ref[0] * y
    o_ref[...] = _plain_norm(z) * lnw_ref[...] + lnb_ref[...]

    @pl.when(i == n - 1)
    def _():
        wait_all(i, 1 - slot)


def _combine(pos, ys, h, route, mod_rows, ln_w, ln_b):
    tiles_per_sample = SEQ // COMB_TM
    row = lambda w: pl.BlockSpec((COMB_TM, w), lambda i, pos: (i, 0))
    full = lambda a: pl.BlockSpec(a.shape, lambda i, pos: (0, 0))
    return pl.pallas_call(
        _combine_kernel,
        grid_spec=pltpu.PrefetchScalarGridSpec(
            num_scalar_prefetch=1,
            grid=(N_TOK // COMB_TM,),
            in_specs=[
                pl.BlockSpec(memory_space=pl.ANY), row(D_MODEL), row(LANES),
                pl.BlockSpec((1, 1, D_MODEL), lambda i, pos: ((i // tiles_per_sample) * N_MOD + MOD_GATE2, 0, 0)),
                full(ln_w), full(ln_b),
            ],
            out_specs=row(D_MODEL),
            scratch_shapes=[pltpu.VMEM((2, 2 * COMB_TM, TOKEN_SUB, LANES), BF16), pltpu.SemaphoreType.DMA((2,))],
        ),
        out_shape=jax.ShapeDtypeStruct((N_TOK, D_MODEL), F32),
        compiler_params=_params(("arbitrary",)),
        name="combine",
    )(pos, ys, h, route, mod_rows, ln_w, ln_b)


def _routing_tables(route_t, counts):
    cnt = counts[0, N_GROUPS:N_ROUTE].astype(I32)
    row_end = jnp.cumsum(cnt)
    e = route_t[ROUTE_E1:ROUTE_E2 + 1].astype(I32)
    rank = route_t[ROUTE_R1:ROUTE_R2 + 1].astype(I32)
    earlier = jnp.arange(N_EXPERTS, dtype=I32)[:, None, None] < e[None]
    base = jnp.sum(jnp.where(earlier, cnt[:, None, None], 0), 0)
    pos = (base + rank).reshape(-1)
    return pos, row_end


def kernel(x, c, ctx, c_ctx, w_ada, b_ada, w_in, ret_decay, ret_gn_w, mla_q_norm, mla_kv_norm, w_uq, w_ukv, w_o,
           ln1_w, ln1_b, router_group_w, router_group_b, router_expert_w, router_expert_b, expert_w_gate,
           expert_w_up, expert_w_down, ln2_w, ln2_b):
    x2d = x.reshape(N_TOK, D_MODEL)
    ctx2d = ctx.reshape(N_CTX, D_MODEL)

    cc = jnp.zeros((SUBLANES, D_MODEL), F32).at[:BATCH].set(c).at[BATCH].set(c_ctx)
    mod = _ada(cc, w_ada[0], b_ada)
    mod_rows = mod.reshape(SUBLANES * N_MOD, 1, D_MODEL)

    proj, w_in_blocks = _inproj_latent(x2d, mod_rows, jnp.swapaxes(w_in, 1, 2))
    proj_c = _inproj_context(ctx2d, mod_rows, w_in_blocks)

    cos_r, sin_r = _rope_tables(RET_DK)
    decay_rows = jnp.broadcast_to(ret_decay[0].reshape(2 * RET_HEADS, 1, 1), (2 * RET_HEADS, 1, LANES))
    ret = _retention(proj, proj_c, jnp.asarray(cos_r), jnp.asarray(sin_r), decay_rows, ret_gn_w)

    cos_m, sin_m = _rope_tables(MLA_ROPE)
    cos_m = np.concatenate([cos_m, np.ones_like(cos_m)], 1)
    sin_m = np.concatenate([sin_m, np.zeros_like(sin_m)], 1)
    wq = w_uq[0].reshape(MLA_Q_LORA, MLA_HEADS, MLA_DQ)
    wq_pad = jnp.pad(wq, ((0, 0), (0, 0), (0, QK_PAD - MLA_DQ))).reshape(MLA_Q_LORA, MLA_HEADS * QK_PAD).astype(BF16)
    wkv = w_ukv[0].astype(BF16)
    q, k_lat, v_lat = _mla_latent(proj, mla_q_norm, mla_kv_norm, wq_pad, wkv, jnp.asarray(cos_m), jnp.asarray(sin_m))
    k_ctx, v_ctx = _mla_context(proj_c, mla_kv_norm, wkv)
    att, w_o_bf = _attention(q, k_ctx, k_lat, v_ctx, v_lat, w_o[0])

    w_route = jnp.concatenate(
        [router_group_w[0], router_expert_w[0].transpose(1, 0, 2).reshape(D_MODEL, N_EXPERTS),
         jnp.zeros((D_MODEL, LANES - N_ROUTE), F32)], 1)
    b_route = jnp.concatenate(
        [router_group_b[0], router_expert_b[0].reshape(N_EXPERTS), jnp.zeros((LANES - N_ROUTE,), F32)])[None]
    h, t_tiles, route, route_t, counts = _outproj(ret, att, x2d, w_o_bf, mod_rows, ln1_w, ln1_b,
                                                   w_route, b_route)

    pos, row_end = _routing_tables(route_t, counts)
    xs = _dispatch(pos, t_tiles)
    ys = _experts(row_end, xs, expert_w_gate[0], expert_w_up[0], expert_w_down[0])
    out = _combine(pos, ys, h, route, mod_rows, ln2_w, ln2_b)
    return out.reshape(BATCH, SEQ, D_MODEL)
```

```python
import numpy as np
import jax
import jax.numpy as jnp
from jax import lax
from jax.experimental import pallas as pl
from jax.experimental.pallas import tpu as pltpu

F32 = jnp.float32
BF16 = jnp.bfloat16
I32 = jnp.int32

D_MODEL = 2048
BATCH = 4
SEQ = 2048
GRID_W = 64
CTX_LEN = 256
N_TOK = BATCH * SEQ
N_CTX = BATCH * CTX_LEN

RET_HEADS = 8
RET_DK = 128
RET_DV = 128
RET_W = RET_HEADS * RET_DV
CHUNK = 128
N_CHUNKS = SEQ // CHUNK

MLA_HEADS = 8
MLA_Q_LORA = 512
MLA_KV_LORA = 256
MLA_NOPE = 128
MLA_ROPE = 64
MLA_DV = 128
MLA_W = MLA_HEADS * MLA_DV
MLA_DQ = MLA_NOPE + MLA_ROPE
QK_PAD = 256

IN_SIZES = (RET_HEADS * RET_DK, RET_HEADS * RET_DK, RET_W, RET_W, MLA_Q_LORA, MLA_KV_LORA, MLA_ROPE)
IN_W = sum(IN_SIZES)
OFF_Q, OFF_K, OFF_V, OFF_G, OFF_CQ, OFF_CKV, OFF_KPE = (int(v) for v in np.cumsum((0,) + IN_SIZES[:-1]))

N_GROUPS = 4
EXPERTS_PER_GROUP = 8
N_EXPERTS = N_GROUPS * EXPERTS_PER_GROUP
D_EXPERT = 512
N_ROUTE = N_GROUPS + N_EXPERTS

N_MOD = 6
MOD_SHIFT1, MOD_SCALE1, MOD_GATE1, MOD_SHIFT2, MOD_SCALE2, MOD_GATE2 = range(N_MOD)

LOG2_E = float(np.log2(np.e))
ROPE_BASE = 10000.0
EPS = 1e-6
DEPTH = 1
DEEPNORM_ALPHA = (2.0 * DEPTH) ** 0.25

LANES = 128
SUBLANES = 8
ROW_TILE = 256
N_PAIRS = 2 * N_TOK
N_TILES = N_PAIRS // ROW_TILE
TOKEN_SUB = D_MODEL // LANES

V7X_VMEM_BYTES = 64 * 1024 * 1024
VMEM_LIMIT = V7X_VMEM_BYTES - V7X_VMEM_BYTES // 8


def _params(sem):
    return pltpu.CompilerParams(dimension_semantics=sem, vmem_limit_bytes=VMEM_LIMIT)


def _silu(x):
    return x * (1.0 / (1.0 + jnp.exp(-x)))


def _plain_norm(x):
    mu = jnp.mean(x, -1, keepdims=True)
    xc = x - mu
    var = jnp.mean(xc * xc, -1, keepdims=True)
    return xc * lax.rsqrt(var + EPS)


def _to_token_tiles(x):
    return x.astype(BF16).reshape(x.shape[0], TOKEN_SUB, LANES)


def _from_token_tiles(x):
    return x.reshape(x.shape[0], D_MODEL)


ADA_TN = 1024


def _ada_kernel(cc_ref, w_ref, b_ref, o_ref):
    s = _silu(cc_ref[...])
    o_ref[...] = jnp.dot(s, w_ref[...], preferred_element_type=F32,
                         precision=lax.Precision.HIGHEST) + b_ref[...]


def _ada(cc, w_ada, b_ada):
    n = w_ada.shape[1]
    return pl.pallas_call(
        _ada_kernel,
        grid=(n // ADA_TN,),
        in_specs=[
            pl.BlockSpec((SUBLANES, D_MODEL), lambda j: (0, 0)),
            pl.BlockSpec((D_MODEL, ADA_TN), lambda j: (0, j)),
            pl.BlockSpec((1, ADA_TN), lambda j: (0, j)),
        ],
        out_specs=pl.BlockSpec((SUBLANES, ADA_TN), lambda j: (0, j)),
        out_shape=jax.ShapeDtypeStruct((SUBLANES, n), F32),
        compiler_params=_params(("arbitrary",)),
        name="ada",
    )(cc, w_ada, b_ada)


INPROJ_TM = 1024
INPROJ_TN = 1024
assert OFF_K == INPROJ_TN and OFF_V == 2 * INPROJ_TN and OFF_CQ == 4 * INPROJ_TN
CTX_OFF_K, CTX_OFF_V = 0, INPROJ_TN
CTX_OFF_CKV = 2 * INPROJ_TN + (OFF_CKV - OFF_CQ)
CTX_OFF_KPE = 2 * INPROJ_TN + (OFF_KPE - OFF_CQ)


INPROJ_SUB = 256
INPROJ_BLOCKS = -(-IN_W // INPROJ_TN)


def _inproj_products(x_ref, shift_ref, scale_ref, w, o_ref, xn_ref):
    first = pl.program_id(1) == 0

    @pl.when(first)
    def _():
        for s in range(INPROJ_TM // INPROJ_SUB):
            rows = slice(s * INPROJ_SUB, (s + 1) * INPROJ_SUB)
            y = (_plain_norm(x_ref[rows, :]) * (1.0 + scale_ref[0]) + shift_ref[0]).astype(BF16)
            xn_ref[rows, :] = y
            o_ref[rows, :] = _dot_nt(y, w[...]).astype(BF16)

    @pl.when(jnp.logical_not(first))
    def _():
        o_ref[...] = _dot_nt(xn_ref[...], w[...]).astype(BF16)


def _inproj_latent_kernel(x_ref, shift_ref, scale_ref, wf_ref, o_ref, wbf_ref, xn_ref, wres_ref, out_sem):
    i = pl.program_id(0)
    k = pl.program_id(1)

    def out_copy(block):
        return pltpu.make_async_copy(wres_ref.at[block], wbf_ref.at[block], out_sem.at[block])

    @pl.when(i == 0)
    def _():
        col = lax.broadcasted_iota(I32, (INPROJ_TN, D_MODEL), 0)
        wres_ref[k] = jnp.where(col < IN_W - k * INPROJ_TN, wf_ref[0], 0.0).astype(BF16)
        out_copy(k).start()

    _inproj_products(x_ref, shift_ref, scale_ref, wres_ref.at[k], o_ref, xn_ref)

    @pl.when((i == pl.num_programs(0) - 1) & (k == INPROJ_BLOCKS - 1))
    def _():
        for block in range(INPROJ_BLOCKS):
            out_copy(block).wait()


def _inproj_context_kernel(x_ref, shift_ref, scale_ref, w_ref, o_ref, xn_ref):
    _inproj_products(x_ref, shift_ref, scale_ref, w_ref.at[0], o_ref, xn_ref)


def _inproj_mod_map(j, tiles_per_sample, sample_row0):
    return lambda i, k: ((sample_row0 + i // tiles_per_sample) * N_MOD + j, 0, 0)


def _inproj_latent(x2d, mod_rows, w_in_t):
    last = INPROJ_BLOCKS - 1
    return pl.pallas_call(
        _inproj_latent_kernel,
        grid=(N_TOK // INPROJ_TM, INPROJ_BLOCKS),
        in_specs=[
            pl.BlockSpec((INPROJ_TM, D_MODEL), lambda i, k: (i, 0)),
            pl.BlockSpec((1, 1, D_MODEL), _inproj_mod_map(MOD_SHIFT1, SEQ // INPROJ_TM, 0)),
            pl.BlockSpec((1, 1, D_MODEL), _inproj_mod_map(MOD_SCALE1, SEQ // INPROJ_TM, 0)),
            pl.BlockSpec((1, INPROJ_TN, D_MODEL), lambda i, k: (0, jnp.where(i == 0, k, last), 0),
                         pipeline_mode=pl.Buffered(1)),
        ],
        out_specs=[pl.BlockSpec((INPROJ_TM, INPROJ_TN), lambda i, k: (i, k)), pl.BlockSpec(memory_space=pl.ANY)],
        out_shape=[jax.ShapeDtypeStruct((N_TOK, IN_W), BF16),
                   jax.ShapeDtypeStruct((INPROJ_BLOCKS, INPROJ_TN, D_MODEL), BF16)],
        scratch_shapes=[pltpu.VMEM((INPROJ_TM, D_MODEL), BF16),
                        pltpu.VMEM((INPROJ_BLOCKS, INPROJ_TN, D_MODEL), BF16),
                        pltpu.SemaphoreType.DMA((INPROJ_BLOCKS,))],
        compiler_params=_params(("arbitrary", "arbitrary")),
        name="inproj",
    )(x2d, mod_rows, mod_rows, w_in_t)


def _inproj_context(ctx2d, mod_rows, w_blocks):
    last = INPROJ_BLOCKS - 1
    out_w = 2 * INPROJ_TN + (IN_W - last * INPROJ_TN)
    return pl.pallas_call(
        _inproj_context_kernel,
        grid=(N_CTX // INPROJ_TM, 3),
        in_specs=[
            pl.BlockSpec((INPROJ_TM, D_MODEL), lambda i, k: (i, 0)),
            pl.BlockSpec((1, 1, D_MODEL), _inproj_mod_map(MOD_SHIFT1, N_CTX // INPROJ_TM, BATCH)),
            pl.BlockSpec((1, 1, D_MODEL), _inproj_mod_map(MOD_SCALE1, N_CTX // INPROJ_TM, BATCH)),
            pl.BlockSpec((1, INPROJ_TN, D_MODEL), lambda i, k: (jnp.where(k == 2, last, k + 1), 0, 0)),
        ],
        out_specs=pl.BlockSpec((INPROJ_TM, INPROJ_TN), lambda i, k: (i, k)),
        out_shape=jax.ShapeDtypeStruct((N_CTX, out_w), BF16),
        scratch_shapes=[pltpu.VMEM((INPROJ_TM, D_MODEL), BF16)],
        compiler_params=_params(("arbitrary", "arbitrary")),
        name="inproj_ctx",
    )(ctx2d, mod_rows, mod_rows, w_blocks)


def _rope_tables(width):
    half = width // 2
    quarter = half // 2
    inv_freq = ROPE_BASE ** (-np.arange(0, half, 2, dtype=np.float64) / half)
    t = np.arange(SEQ)
    cos_parts, sin_parts = [], []
    for pos in (t // GRID_W, t % GRID_W):
        ang = pos[:, None].astype(np.float64) * inv_freq[None, :]
        c, s = np.cos(ang), np.sin(ang)
        cos_parts += [c, c]
        sin_parts += [-s, s]
    assert cos_parts[0].shape[1] == quarter
    return (np.concatenate(cos_parts, 1).astype(np.float32), np.concatenate(sin_parts, 1).astype(np.float32))


def _rope(x, cos, sin, quarter):
    lane = lax.broadcasted_iota(I32, x.shape, 1)
    first = (lane % (2 * quarter)) < quarter
    swapped = jnp.where(first, pltpu.roll(x, LANES - quarter, 1), pltpu.roll(x, quarter, 1))
    return x * cos + swapped * sin


def _dot_tn(a, b):
    return lax.dot_general(a, b, (((0,), (0,)), ((), ())), preferred_element_type=F32)


def _dot_nt(a, b):
    return lax.dot_general(a, b, (((1,), (1,)), ((), ())), preferred_element_type=F32)


RET_HG = 4


def _retention_kernel(q_ref, k_ref, v_ref, g_ref, kc_ref, vc_ref, cos_ref, sin_ref, df_ref, db_ref, gn_ref,
                      o_ref, qs_ref, ks_ref, st_ref):
    for hh in range(RET_HG):
        _retention_head(hh, slice(hh * LANES, (hh + 1) * LANES), q_ref, k_ref, v_ref, g_ref, kc_ref, vc_ref,
                        cos_ref, sin_ref, df_ref, db_ref, gn_ref, o_ref, qs_ref, ks_ref, st_ref)


def _retention_head(hh, hs, q_ref, k_ref, v_ref, g_ref, kc_ref, vc_ref, cos_ref, sin_ref, df_ref, db_ref, gn_ref,
                    o_ref, qs_ref, ks_ref, st_ref):
    k_scale = RET_DK ** -0.5
    lgf = jax.nn.log_sigmoid(df_ref[hh])
    lgb = jax.nn.log_sigmoid(db_ref[hh])

    cos = cos_ref[...]
    sin = sin_ref[...]
    qs_ref[hh] = _rope(q_ref[:, hs].astype(F32), cos, sin, RET_DK // 4).astype(BF16)
    ks_ref[hh] = _rope(k_ref[:, hs].astype(F32), cos, sin, RET_DK // 4) * k_scale

    rowi = lax.broadcasted_iota(I32, (CHUNK, LANES), 0).astype(F32)
    coli = lax.broadcasted_iota(I32, (CHUNK, LANES), 1).astype(F32)
    diff = rowi - coli
    decay = jnp.exp(jnp.where(diff >= 0, lgf * diff, -lgb * diff)) * jnp.where(diff == 0, 2.0, 1.0)
    zeta_f = jnp.exp(lgf * (CHUNK - 1.0 - rowi))
    eta_b = jnp.exp(lgb * rowi)
    xi_f = jnp.exp(lgf * (rowi + 1.0))
    xi_b = jnp.exp(lgb * (CHUNK - rowi))
    cdec_f = jnp.exp(lgf * float(CHUNK))
    cdec_b = jnp.exp(lgb * float(CHUNK))

    crow = lax.broadcasted_iota(I32, (CTX_LEN, LANES), 0).astype(F32)
    kc = kc_ref[:, hs].astype(F32) * k_scale
    vc = vc_ref[:, hs]
    s_f = _dot_tn((kc * jnp.exp(lgf * (CTX_LEN - 1.0 - crow))).astype(BF16), vc)
    s_b = _dot_tn((kc * jnp.exp(lgb * crow)).astype(BF16), vc)

    upd_f, upd_b = [], []
    for i in range(N_CHUNKS):
        rows = pl.ds(i * CHUNK, CHUNK)
        kch = ks_ref[hh, rows, :]
        vch = v_ref[rows, hs]
        upd_f.append(_dot_tn((kch * zeta_f).astype(BF16), vch))
        upd_b.append(_dot_tn((kch * eta_b).astype(BF16), vch))
    state = s_f
    for i in range(N_CHUNKS):
        st_ref[hh, i, :, :RET_DV] = state.astype(BF16)
        state = cdec_f * state + upd_f[i]
    state = s_b
    for i in reversed(range(N_CHUNKS)):
        st_ref[hh, i, :, RET_DV:] = state.astype(BF16)
        state = cdec_b * state + upd_b[i]

    gn_w = gn_ref[:, hs]
    for i in range(N_CHUNKS):
        rows = pl.ds(i * CHUNK, CHUNK)
        qch = qs_ref[hh, rows, :]
        scores = _dot_nt(qch, ks_ref[hh, rows, :].astype(BF16)) * decay
        o = jnp.dot(scores.astype(BF16), v_ref[rows, hs], preferred_element_type=F32)
        cross = jnp.dot(qch, st_ref[hh, i], preferred_element_type=F32)
        o = o + xi_f * cross[:, :RET_DV] + xi_b * cross[:, RET_DV:]
        y = _plain_norm(o) * gn_w
        o_ref[rows, hs] = (_silu(g_ref[rows, hs].astype(F32)) * y).astype(BF16)


def _retention(proj, proj_c, cos, sin, decay_rows, gn_w):
    width = RET_HG * LANES
    groups = RET_HEADS // RET_HG
    blk = lambda off: pl.BlockSpec((SEQ, width), lambda b, hg: (b, off // width + hg))
    blk_c = lambda off: pl.BlockSpec((CTX_LEN, width), lambda b, hg: (b, off // width + hg))
    table = pl.BlockSpec((SEQ, LANES), lambda b, hg: (0, 0))
    return pl.pallas_call(
        _retention_kernel,
        grid=(BATCH, groups),
        in_specs=[
            blk(OFF_Q), blk(OFF_K), blk(OFF_V), blk(OFF_G), blk_c(CTX_OFF_K), blk_c(CTX_OFF_V), table, table,
            pl.BlockSpec((RET_HG, 1, LANES), lambda b, hg: (hg, 0, 0)),
            pl.BlockSpec((RET_HG, 1, LANES), lambda b, hg: (groups + hg, 0, 0)),
            pl.BlockSpec((1, width), lambda b, hg: (0, hg)),
        ],
        out_specs=pl.BlockSpec((SEQ, width), lambda b, hg: (b, hg)),
        out_shape=jax.ShapeDtypeStruct((N_TOK, RET_W), BF16),
        scratch_shapes=[
            pltpu.VMEM((RET_HG, SEQ, RET_DK), BF16),
            pltpu.VMEM((RET_HG, SEQ, RET_DK), F32),
            pltpu.VMEM((RET_HG, N_CHUNKS, RET_DK, 2 * RET_DV), BF16),
        ],
        compiler_params=_params(("arbitrary", "arbitrary")),
        name="retention",
    )(proj, proj, proj, proj, proj_c, proj_c, cos, sin, decay_rows, decay_rows, gn_w)


MLA_TM = 512


def _rms_norm(x, w):
    return x * lax.rsqrt(jnp.mean(x * x, -1, keepdims=True) + EPS) * w


def _mla_kv(ckv_ref, kpe_ref, kvn_ref, wkv_ref, cos_ref, sin_ref, k_ref, v_ref, rotate):
    ckv = _rms_norm(ckv_ref[...].astype(F32), kvn_ref[...]).astype(BF16)
    kv = jnp.dot(ckv, wkv_ref[...], preferred_element_type=F32)
    lane = lax.broadcasted_iota(I32, (ckv.shape[0], LANES), 1)
    kpe = jnp.where(lane < MLA_ROPE, kpe_ref[...].astype(F32), 0.0)
    if rotate:
        kpe = _rope(kpe, cos_ref[...], sin_ref[...], MLA_ROPE // 4)
    kpe = kpe.astype(BF16)
    for h in range(MLA_HEADS):
        k_ref[:, h * QK_PAD:h * QK_PAD + MLA_NOPE] = kv[:, 2 * h * LANES:(2 * h + 1) * LANES].astype(BF16)
        k_ref[:, h * QK_PAD + MLA_NOPE:(h + 1) * QK_PAD] = kpe
        v_ref[:, h * MLA_DV:(h + 1) * MLA_DV] = kv[:, (2 * h + 1) * LANES:(2 * h + 2) * LANES].astype(BF16)


def _mla_latent_kernel(cq_ref, ckv_ref, kpe_ref, qn_ref, kvn_ref, wq_ref, wkv_ref, cos_ref, sin_ref,
                       q_ref, k_ref, v_ref):
    cq = _rms_norm(cq_ref[...].astype(F32), qn_ref[...]).astype(BF16)
    q = jnp.dot(cq, wq_ref[...], preferred_element_type=F32)
    cos = cos_ref[...]
    sin = sin_ref[...]
    scale = MLA_DQ ** -0.5 * LOG2_E
    for h in range(MLA_HEADS):
        lo = h * QK_PAD
        q_ref[:, lo:lo + MLA_NOPE] = (q[:, lo:lo + MLA_NOPE] * scale).astype(BF16)
        qpe = _rope(q[:, lo + MLA_NOPE:lo + QK_PAD], cos, sin, MLA_ROPE // 4)
        q_ref[:, lo + MLA_NOPE:lo + QK_PAD] = (qpe * scale).astype(BF16)
    _mla_kv(ckv_ref, kpe_ref, kvn_ref, wkv_ref, cos_ref, sin_ref, k_ref, v_ref, rotate=True)


def _mla_context_kernel(ckv_ref, kpe_ref, kvn_ref, wkv_ref, k_ref, v_ref):
    _mla_kv(ckv_ref, kpe_ref, kvn_ref, wkv_ref, None, None, k_ref, v_ref, rotate=False)


def _mla_latent(proj, q_norm, kv_norm, wq_pad, wkv, cos, sin):
    row = lambda w, off: pl.BlockSpec((MLA_TM, w), lambda i: (i, off // w))
    full = lambda a: pl.BlockSpec(a.shape, lambda i: (0, 0))
    table = pl.BlockSpec((MLA_TM, LANES), lambda i: (i % (SEQ // MLA_TM), 0))
    return pl.pallas_call(
        _mla_latent_kernel,
        grid=(N_TOK // MLA_TM,),
        in_specs=[row(MLA_Q_LORA, OFF_CQ), row(MLA_KV_LORA, OFF_CKV), row(LANES, OFF_KPE),
                  full(q_norm), full(kv_norm), full(wq_pad), full(wkv), table, table],
        out_specs=[
            pl.BlockSpec((MLA_TM, MLA_HEADS * QK_PAD), lambda i: (i, 0)),
            pl.BlockSpec((MLA_TM, MLA_HEADS * QK_PAD), lambda i: (i, 0)),
            pl.BlockSpec((MLA_TM, MLA_W), lambda i: (i, 0)),
        ],
        out_shape=[
            jax.ShapeDtypeStruct((N_TOK, MLA_HEADS * QK_PAD), BF16),
            jax.ShapeDtypeStruct((N_TOK, MLA_HEADS * QK_PAD), BF16),
            jax.ShapeDtypeStruct((N_TOK, MLA_W), BF16),
        ],
        compiler_params=_params(("arbitrary",)),
        name="mla_latent",
    )(proj, proj, proj, q_norm, kv_norm, wq_pad, wkv, cos, sin)


def _mla_context(proj_c, kv_norm, wkv):
    row = lambda w, off: pl.BlockSpec((MLA_TM, w), lambda i: (i, off // w))
    full = lambda a: pl.BlockSpec(a.shape, lambda i: (0, 0))
    return pl.pallas_call(
        _mla_context_kernel,
        grid=(N_CTX // MLA_TM,),
        in_specs=[row(MLA_KV_LORA, CTX_OFF_CKV), row(LANES, CTX_OFF_KPE), full(kv_norm), full(wkv)],
        out_specs=[
            pl.BlockSpec((MLA_TM, MLA_HEADS * QK_PAD), lambda i: (i, 0)),
            pl.BlockSpec((MLA_TM, MLA_W), lambda i: (i, 0)),
        ],
        out_shape=[
            jax.ShapeDtypeStruct((N_CTX, MLA_HEADS * QK_PAD), BF16),
            jax.ShapeDtypeStruct((N_CTX, MLA_W), BF16),
        ],
        compiler_params=_params(("arbitrary",)),
        name="mla_context",
    )(proj_c, proj_c, kv_norm, wkv)


ATT_TQ = 512


def _attention_kernel(q_ref, kc_ref, kl_ref, vc_ref, vl_ref, wo_ref, o_ref, wo_bf_ref, vext_ref):
    wo_bf_ref[...] = wo_ref[...].astype(BF16)

    @pl.when(pl.program_id(1) == 0)
    def _():
        lane = lax.broadcasted_iota(I32, (CTX_LEN + SEQ, MLA_DV), 1)
        ones_col = jnp.where(lane == 0, 1.0, 0.0).astype(BF16)
        for h in range(MLA_HEADS):
            dv = slice(h * MLA_DV, (h + 1) * MLA_DV)
            vext_ref[h, :CTX_LEN, :MLA_DV] = vc_ref[:, dv]
            vext_ref[h, CTX_LEN:, :MLA_DV] = vl_ref[:, dv]
            vext_ref[h, :, MLA_DV:] = ones_col

    for h in range(MLA_HEADS):
        qk = slice(h * QK_PAD, (h + 1) * QK_PAD)
        q = q_ref[:, qk]
        s_c = _dot_nt(q, kc_ref[:, qk])
        s_l = _dot_nt(q, kl_ref[:, qk])
        m = jnp.maximum(jnp.max(s_c, -1, keepdims=True), jnp.max(s_l, -1, keepdims=True))
        p_c = jnp.exp2(s_c - m).astype(BF16)
        p_l = jnp.exp2(s_l - m).astype(BF16)
        o = (jnp.dot(p_c, vext_ref[h, :CTX_LEN, :], preferred_element_type=F32)
             + jnp.dot(p_l, vext_ref[h, CTX_LEN:, :], preferred_element_type=F32))
        o_ref[:, h * MLA_DV:(h + 1) * MLA_DV] = (o[:, :MLA_DV] / o[:, MLA_DV:MLA_DV + 1]).astype(BF16)


def _attention(q, k_ctx, k_lat, v_ctx, v_lat, w_o):
    tiles = SEQ // ATT_TQ
    band = w_o.shape[0] // (BATCH * tiles)
    step = lambda b, i: (b * tiles + i, 0)
    return pl.pallas_call(
        _attention_kernel,
        grid=(BATCH, tiles),
        in_specs=[
            pl.BlockSpec((ATT_TQ, MLA_HEADS * QK_PAD), step),
            pl.BlockSpec((CTX_LEN, MLA_HEADS * QK_PAD), lambda b, i: (b, 0)),
            pl.BlockSpec((SEQ, MLA_HEADS * QK_PAD), lambda b, i: (b, 0)),
            pl.BlockSpec((CTX_LEN, MLA_W), lambda b, i: (b, 0)),
            pl.BlockSpec((SEQ, MLA_W), lambda b, i: (b, 0)),
            pl.BlockSpec((band, D_MODEL), step),
        ],
        out_specs=[pl.BlockSpec((ATT_TQ, MLA_W), step), pl.BlockSpec((band, D_MODEL), step)],
        out_shape=[jax.ShapeDtypeStruct((N_TOK, MLA_W), BF16), jax.ShapeDtypeStruct(w_o.shape, BF16)],
        scratch_shapes=[pltpu.VMEM((MLA_HEADS, CTX_LEN + SEQ, 2 * MLA_DV), BF16)],
        compiler_params=_params(("arbitrary", "arbitrary")),
        name="attention",
    )(q, k_ctx, k_lat, v_ctx, v_lat, w_o)


OUT_TM = 512
OUT_SUB = 256
ROUTE_E1, ROUTE_E2, ROUTE_R1, ROUTE_R2, ROUTE_W1, ROUTE_W2 = range(6)


def _outproj_kernel(ret_ref, att_ref, x_ref, wo_ref, gate_ref, shift_ref, scale_ref, lnw_ref, lnb_ref,
                    wr_ref, br_ref, h_ref, t_ref, route_ref, route_t_ref, count_ref, carry_ref, wsplit_ref):
    @pl.when(pl.program_id(0) == 0)
    def _():
        carry_ref[...] = jnp.zeros_like(carry_ref)
        w = wr_ref[...]
        w_hi = w.astype(BF16)
        wsplit_ref[:, :LANES] = w_hi
        wsplit_ref[:, LANES:] = (w - w_hi.astype(F32)).astype(BF16)

    subtiles = [slice(s * OUT_SUB, (s + 1) * OUT_SUB) for s in range(OUT_TM // OUT_SUB)]
    mixes = [jnp.dot(ret_ref[rows, :], wo_ref[:RET_W, :], preferred_element_type=F32)
             + jnp.dot(att_ref[rows, :], wo_ref[RET_W:, :], preferred_element_type=F32) for rows in subtiles]
    logits = [_outproj_norms(rows, mix, x_ref, gate_ref, shift_ref, scale_ref, lnw_ref, lnb_ref, br_ref, h_ref, t_ref,
                             wsplit_ref) for rows, mix in zip(subtiles, mixes)]
    _route(jnp.concatenate(logits, 0), carry_ref, count_ref, route_ref, route_t_ref)


def _outproj_norms(rows, mix, x_ref, gate_ref, shift_ref, scale_ref, lnw_ref, lnb_ref, br_ref, h_ref, t_ref,
                   wsplit_ref):
    h = _plain_norm(DEEPNORM_ALPHA * x_ref[rows, :] + gate_ref[0] * mix) * lnw_ref[...] + lnb_ref[...]
    h_ref[rows, :] = h
    t = _plain_norm(h) * (1.0 + scale_ref[0]) + shift_ref[0]
    t_ref[rows] = _to_token_tiles(t)

    t_hi = t.astype(BF16)
    t_lo = (t - t_hi.astype(F32)).astype(BF16)
    main = jnp.dot(t_hi, wsplit_ref[...], preferred_element_type=F32)
    corr = jnp.dot(t_lo, wsplit_ref[:, :LANES], preferred_element_type=F32)
    return main[:, :LANES] + (main[:, LANES:] + corr) + br_ref[...]


def _route(logits, carry_ref, count_ref, route_ref, route_t_ref):
    lane = lax.broadcasted_iota(I32, logits.shape, 1).astype(F32)
    neg = -jnp.inf
    big = float(LANES)

    def first_lane_of(mask):
        return jnp.min(jnp.where(mask, lane, big), -1, keepdims=True)

    is_group = lane < N_GROUPS
    gl = jnp.where(is_group, logits, neg)
    g_max = jnp.max(gl, -1, keepdims=True)
    g_idx = first_lane_of(is_group & (gl == g_max))
    g_prob = 1.0 / jnp.sum(jnp.where(is_group, jnp.exp(logits - g_max), 0.0), -1, keepdims=True)

    lo = N_GROUPS + g_idx * EXPERTS_PER_GROUP
    in_group = (lane >= lo) & (lane < lo + EXPERTS_PER_GROUP)
    el = jnp.where(in_group, logits, neg)
    v1 = jnp.max(el, -1, keepdims=True)
    i1 = first_lane_of(in_group & (el == v1))
    rest = in_group & (lane != i1)
    el2 = jnp.where(rest, logits, neg)
    v2 = jnp.max(el2, -1, keepdims=True)
    i2 = first_lane_of(rest & (el2 == v2))
    d = jnp.exp(v2 - v1)
    w1 = g_prob / (1.0 + d)
    w2 = g_prob * d / (1.0 + d)

    onehot = jnp.where((lane == i1) | (lane == i2), 1.0, 0.0)
    r = lax.broadcasted_iota(I32, (OUT_TM, OUT_TM), 0)
    c = lax.broadcasted_iota(I32, (OUT_TM, OUT_TM), 1)
    tri = jnp.where(c < r, 1.0, 0.0).astype(BF16)
    carry = carry_ref[0:1, :]
    before = jnp.dot(tri, onehot.astype(BF16), preferred_element_type=F32) + carry
    r1 = jnp.sum(jnp.where(lane == i1, before, 0.0), -1, keepdims=True)
    r2 = jnp.sum(jnp.where(lane == i2, before, 0.0), -1, keepdims=True)
    carry = carry + jnp.sum(onehot, 0, keepdims=True)
    carry_ref[...] = jnp.broadcast_to(carry, carry_ref.shape)
    count_ref[...] = jnp.broadcast_to(carry, count_ref.shape)

    rec = jnp.zeros_like(logits)
    for slot, val in ((ROUTE_E1, i1 - N_GROUPS), (ROUTE_E2, i2 - N_GROUPS), (ROUTE_R1, r1), (ROUTE_R2, r2),
                      (ROUTE_W1, w1), (ROUTE_W2, w2)):
        rec = jnp.where(lane == slot, val, rec)
    route_ref[...] = rec
    route_t_ref[...] = rec.T[:SUBLANES, :]


def _outproj(ret, att, x2d, wo_bf, mod_rows, ln_w, ln_b, w_route, b_route):
    tiles_per_sample = SEQ // OUT_TM
    row = lambda w: pl.BlockSpec((OUT_TM, w), lambda i: (i, 0))
    full = lambda a: pl.BlockSpec(a.shape, lambda i: (0, 0))
    mod = lambda j: pl.BlockSpec((1, 1, D_MODEL), lambda i: ((i // tiles_per_sample) * N_MOD + j, 0, 0))
    return pl.pallas_call(
        _outproj_kernel,
        grid=(N_TOK // OUT_TM,),
        in_specs=[row(RET_W), row(MLA_W), row(D_MODEL), full(wo_bf), mod(MOD_GATE1), mod(MOD_SHIFT2), mod(MOD_SCALE2),
                  full(ln_w), full(ln_b), full(w_route), full(b_route)],
        out_specs=[row(D_MODEL), pl.BlockSpec((OUT_TM, TOKEN_SUB, LANES), lambda i: (i, 0, 0)), row(LANES),
                   pl.BlockSpec((SUBLANES, OUT_TM), lambda i: (0, i)),
                   pl.BlockSpec((SUBLANES, LANES), lambda i: (0, 0))],
        out_shape=[
            jax.ShapeDtypeStruct((N_TOK, D_MODEL), F32),
            jax.ShapeDtypeStruct((N_TOK, TOKEN_SUB, LANES), BF16),
            jax.ShapeDtypeStruct((N_TOK, LANES), F32),
            jax.ShapeDtypeStruct((SUBLANES, N_TOK), F32),
            jax.ShapeDtypeStruct((SUBLANES, LANES), F32),
        ],
        scratch_shapes=[pltpu.VMEM((SUBLANES, LANES), F32), pltpu.VMEM((D_MODEL, 2 * LANES), BF16)],
        compiler_params=_params(("arbitrary",)),
        name="outproj_route",
    )(ret, att, x2d, wo_bf, mod_rows, mod_rows, mod_rows, ln_w, ln_b, w_route, b_route)


DISPATCH_TM = 1024


def _dispatch_kernel(pos_ref, t_ref, xs_ref, sem):
    i = pl.program_id(0)

    def copy(slot, j):
        return pltpu.make_async_copy(t_ref.at[j], xs_ref.at[pos_ref[slot * N_TOK + i * DISPATCH_TM + j]], sem)

    for slot in range(2):
        lax.fori_loop(0, DISPATCH_TM, lambda j, c, slot=slot: (copy(slot, j).start(priority=slot), c)[1], 0,
                      unroll=8)
    for slot in range(2):
        pltpu.make_async_copy(t_ref, xs_ref.at[pl.ds(0, DISPATCH_TM)], sem).wait()


def _dispatch(pos, t_tiles):
    return pl.pallas_call(
        _dispatch_kernel,
        grid_spec=pltpu.PrefetchScalarGridSpec(
            num_scalar_prefetch=1,
            grid=(N_TOK // DISPATCH_TM,),
            in_specs=[pl.BlockSpec((DISPATCH_TM, TOKEN_SUB, LANES), lambda i, pos: (i, 0, 0))],
            out_specs=pl.BlockSpec(memory_space=pl.ANY),
            scratch_shapes=[pltpu.SemaphoreType.DMA(())],
        ),
        out_shape=jax.ShapeDtypeStruct((N_PAIRS, TOKEN_SUB, LANES), BF16),
        compiler_params=_params(("arbitrary",)),
        name="dispatch",
    )(pos, t_tiles)


def _experts_kernel(rend_ref, xs_ref, wg_ref, wu_ref, wd_ref, ys_ref, wgu_ref, wdn_ref, xbuf_ref, ybuf_ref,
                    xsem, ysem):
    e = pl.program_id(0)
    lo = jnp.where(e == 0, 0, rend_ref[jnp.maximum(e - 1, 0)])
    hi = rend_ref[e]

    def x_copy(g):
        return pltpu.make_async_copy(xs_ref.at[pl.ds(g * ROW_TILE, ROW_TILE)], xbuf_ref.at[g % 2], xsem.at[g % 2])

    def y_copy(g):
        return pltpu.make_async_copy(ybuf_ref.at[g % 2], ys_ref.at[pl.ds(g * ROW_TILE, ROW_TILE)], ysem.at[g % 2])

    @pl.when(e == 0)
    def _():
        x_copy(0).start()

    @pl.when(hi > lo)
    def _():
        wgu_ref[:, :D_EXPERT] = wg_ref[0].astype(BF16)
        wgu_ref[:, D_EXPERT:] = wu_ref[0].astype(BF16)
        wdn_ref[...] = wd_ref[0].astype(BF16)

        def tile(g, c):
            row0 = g * ROW_TILE
            owns_first_row = lo <= row0
            owns_last_row = hi >= row0 + ROW_TILE

            @pl.when(owns_first_row)
            def _():
                @pl.when(g + 1 < N_TILES)
                def _():
                    x_copy(g + 1).start()

                x_copy(g).wait()

            gu = jnp.dot(_from_token_tiles(xbuf_ref[g % 2]), wgu_ref[...], preferred_element_type=F32)
            hid = _silu(gu[:, :D_EXPERT]) * gu[:, D_EXPERT:]
            y = jnp.dot(hid.astype(BF16), wdn_ref[...], preferred_element_type=F32)

            @pl.when(owns_first_row)
            def _():
                @pl.when(g >= 2)
                def _():
                    y_copy(g - 2).wait()

                ybuf_ref[g % 2] = _to_token_tiles(y)

            @pl.when(jnp.logical_not(owns_first_row))
            def _():
                row = lax.broadcasted_iota(I32, y.shape, 0)
                earlier = _from_token_tiles(ybuf_ref[g % 2]).astype(F32)
                ybuf_ref[g % 2] = _to_token_tiles(jnp.where(row >= lo - row0, y, earlier))

            @pl.when(owns_last_row)
            def _():
                y_copy(g).start()

            return c

        lax.fori_loop(lo // ROW_TILE, (hi - 1) // ROW_TILE + 1, tile, 0)

    @pl.when(e == N_EXPERTS - 1)
    def _():
        y_copy(N_TILES - 2).wait()
        y_copy(N_TILES - 1).wait()


def _experts(row_end, xs, w_gate, w_up, w_down):
    w_gate = w_gate.reshape(N_EXPERTS, D_MODEL, D_EXPERT)
    w_up = w_up.reshape(N_EXPERTS, D_MODEL, D_EXPERT)
    w_down = w_down.reshape(N_EXPERTS, D_EXPERT, D_MODEL)
    expert = lambda e, rend: (e, 0, 0)
    tile_buf = pltpu.VMEM((2, ROW_TILE, TOKEN_SUB, LANES), BF16)
    return pl.pallas_call(
        _experts_kernel,
        grid_spec=pltpu.PrefetchScalarGridSpec(
            num_scalar_prefetch=1,
            grid=(N_EXPERTS,),
            in_specs=[
                pl.BlockSpec(memory_space=pl.ANY),
                pl.BlockSpec((1, D_MODEL, D_EXPERT), expert),
                pl.BlockSpec((1, D_MODEL, D_EXPERT), expert),
                pl.BlockSpec((1, D_EXPERT, D_MODEL), expert),
            ],
            out_specs=pl.BlockSpec(memory_space=pl.ANY),
            scratch_shapes=[pltpu.VMEM((D_MODEL, 2 * D_EXPERT), BF16), pltpu.VMEM((D_EXPERT, D_MODEL), BF16),
                            tile_buf, tile_buf, pltpu.SemaphoreType.DMA((2,)), pltpu.SemaphoreType.DMA((2,))],
        ),
        out_shape=jax.ShapeDtypeStruct((N_PAIRS, TOKEN_SUB, LANES), BF16),
        compiler_params=_params(("arbitrary",)),
        name="experts",
    )(row_end, xs, w_gate, w_up, w_down)


COMB_TM = 256


def _combine_kernel(pos_ref, ys_ref, h_ref, route_ref, gate_ref, lnw_ref, lnb_ref, o_ref, buf_ref, sem):
    i = pl.program_id(0)
    n = pl.num_programs(0)

    def copy(step, slot, pair, j):
        src = pos_ref[pair * N_TOK + step * COMB_TM + j]
        return pltpu.make_async_copy(ys_ref.at[src], buf_ref.at[slot, pair * COMB_TM + j], sem.at[slot])

    def start_all(step, slot):
        for pair in range(2):
            lax.fori_loop(0, COMB_TM, lambda j, c, pair=pair: (copy(step, slot, pair, j).start(priority=pair), c)[1], 0,
                          unroll=8)

    def wait_all(step, slot):
        pltpu.make_async_copy(ys_ref.at[pl.ds(0, 2 * COMB_TM)], buf_ref.at[slot], sem.at[slot]).wait()

    @pl.when(i == 0)
    def _():
        start_all(0, 0)

    slot = i % 2
    wait_all(i, slot)

    nxt = jnp.minimum(i + 1, n - 1)
    for pair in range(2):
        for j in range(COMB_TM):
            copy(nxt, 1 - slot, pair, j).start(priority=pair)

    y1 = _from_token_tiles(buf_ref[slot, :COMB_TM]).astype(F32)
    y2 = _from_token_tiles(buf_ref[slot, COMB_TM:]).astype(F32)
    y = route_ref[:, ROUTE_W1:ROUTE_W1 + 1] * y1 + route_ref[:, ROUTE_W2:ROUTE_W2 + 1] * y2
    z = DEEPNORM_ALPHA * h_ref[...] + gate_ref[0] * y
    o_ref[...] = _plain_norm(z) * lnw_ref[...] + lnb_ref[...]

    @pl.when(i == n - 1)
    def _():
        wait_all(i, 1 - slot)


def _combine(pos, ys, h, route, mod_rows, ln_w, ln_b):
    tiles_per_sample = SEQ // COMB_TM
    row = lambda w: pl.BlockSpec((COMB_TM, w), lambda i, pos: (i, 0))
    full = lambda a: pl.BlockSpec(a.shape, lambda i, pos: (0, 0))
    return pl.pallas_call(
        _combine_kernel,
        grid_spec=pltpu.PrefetchScalarGridSpec(
            num_scalar_prefetch=1,
            grid=(N_TOK // COMB_TM,),
            in_specs=[
                pl.BlockSpec(memory_space=pl.ANY), row(D_MODEL), row(LANES),
                pl.BlockSpec((1, 1, D_MODEL), lambda i, pos: ((i // tiles_per_sample) * N_MOD + MOD_GATE2, 0, 0)),
                full(ln_w), full(ln_b),
            ],
            out_specs=row(D_MODEL),
            scratch_shapes=[pltpu.VMEM((2, 2 * COMB_TM, TOKEN_SUB, LANES), BF16), pltpu.SemaphoreType.DMA((2,))],
        ),
        out_shape=jax.ShapeDtypeStruct((N_TOK, D_MODEL), F32),
        compiler_params=_params(("arbitrary",)),
        name="combine",
    )(pos, ys, h, route, mod_rows, ln_w, ln_b)


def _routing_tables(route_t, counts):
    cnt = counts[0, N_GROUPS:N_ROUTE].astype(I32)
    row_end = jnp.cumsum(cnt)
    e = route_t[ROUTE_E1:ROUTE_E2 + 1].astype(I32)
    rank = route_t[ROUTE_R1:ROUTE_R2 + 1].astype(I32)
    earlier = jnp.arange(N_EXPERTS, dtype=I32)[:, None, None] < e[None]
    base = jnp.sum(jnp.where(earlier, cnt[:, None, None], 0), 0)
    pos = (base + rank).reshape(-1)
    return pos, row_end


def kernel(x, c, ctx, c_ctx, w_ada, b_ada, w_in, ret_decay, ret_gn_w, mla_q_norm, mla_kv_norm, w_uq, w_ukv, w_o,
           ln1_w, ln1_b, router_group_w, router_group_b, router_expert_w, router_expert_b, expert_w_gate,
           expert_w_up, expert_w_down, ln2_w, ln2_b):
    x2d = x.reshape(N_TOK, D_MODEL)
    ctx2d = ctx.reshape(N_CTX, D_MODEL)

    cc = jnp.zeros((SUBLANES, D_MODEL), F32).at[:BATCH].set(c).at[BATCH].set(c_ctx)
    mod = _ada(cc, w_ada[0], b_ada)
    mod_rows = mod.reshape(SUBLANES * N_MOD, 1, D_MODEL)

    proj, w_in_blocks = _inproj_latent(x2d, mod_rows, jnp.swapaxes(w_in, 1, 2))
    proj_c = _inproj_context(ctx2d, mod_rows, w_in_blocks)

    cos_r, sin_r = _rope_tables(RET_DK)
    decay_rows = jnp.broadcast_to(ret_decay[0].reshape(2 * RET_HEADS, 1, 1), (2 * RET_HEADS, 1, LANES))
    ret = _retention(proj, proj_c, jnp.asarray(cos_r), jnp.asarray(sin_r), decay_rows, ret_gn_w)

    cos_m, sin_m = _rope_tables(MLA_ROPE)
    cos_m = np.concatenate([cos_m, np.ones_like(cos_m)], 1)
    sin_m = np.concatenate([sin_m, np.zeros_like(sin_m)], 1)
    wq = w_uq[0].reshape(MLA_Q_LORA, MLA_HEADS, MLA_DQ)
    wq_pad = jnp.pad(wq, ((0, 0), (0, 0), (0, QK_PAD - MLA_DQ))).reshape(MLA_Q_LORA, MLA_HEADS * QK_PAD).astype(BF16)
    wkv = w_ukv[0].astype(BF16)
    q, k_lat, v_lat = _mla_latent(proj, mla_q_norm, mla_kv_norm, wq_pad, wkv, jnp.asarray(cos_m), jnp.asarray(sin_m))
    k_ctx, v_ctx = _mla_context(proj_c, mla_kv_norm, wkv)
    att, w_o_bf = _attention(q, k_ctx, k_lat, v_ctx, v_lat, w_o[0])

    w_route = jnp.concatenate(
        [router_group_w[0], router_expert_w[0].transpose(1, 0, 2).reshape(D_MODEL, N_EXPERTS),
         jnp.zeros((D_MODEL, LANES - N_ROUTE), F32)], 1)
    b_route = jnp.concatenate(
        [router_group_b[0], router_expert_b[0].reshape(N_EXPERTS), jnp.zeros((LANES - N_ROUTE,), F32)])[None]
    h, t_tiles, route, route_t, counts = _outproj(ret, att, x2d, w_o_bf, mod_rows, ln1_w, ln1_b,
                                                   w_route, b_route)

    pos, row_end = _routing_tables(route_t, counts)
    xs = _dispatch(pos, t_tiles)
    ys = _experts(row_end, xs, expert_w_gate[0], expert_w_up[0], expert_w_down[0])
    out = _combine(pos, ys, h, route, mod_rows, ln2_w, ln2_b)
    return out.reshape(BATCH, SEQ, D_MODEL)
```

```python
import numpy as np
import jax
import jax.numpy as jnp
from jax import lax
from jax.experimental import pallas as pl
from jax.experimental.pallas import tpu as pltpu

F32 = jnp.float32
BF16 = jnp.bfloat16
I32 = jnp.int32

D_MODEL = 2048
BATCH = 4
SEQ = 2048
GRID_W = 64
CTX_LEN = 256
N_TOK = BATCH * SEQ
N_CTX = BATCH * CTX_LEN

RET_HEADS = 8
RET_DK = 128
RET_DV = 128
RET_W = RET_HEADS * RET_DV
CHUNK = 128
N_CHUNKS = SEQ // CHUNK

MLA_HEADS = 8
MLA_Q_LORA = 512
MLA_KV_LORA = 256
MLA_NOPE = 128
MLA_ROPE = 64
MLA_DV = 128
MLA_W = MLA_HEADS * MLA_DV
MLA_DQ = MLA_NOPE + MLA_ROPE
QK_PAD = 256

IN_SIZES = (RET_HEADS * RET_DK, RET_HEADS * RET_DK, RET_W, RET_W, MLA_Q_LORA, MLA_KV_LORA, MLA_ROPE)
IN_W = sum(IN_SIZES)
OFF_Q, OFF_K, OFF_V, OFF_G, OFF_CQ, OFF_CKV, OFF_KPE = (int(v) for v in np.cumsum((0,) + IN_SIZES[:-1]))

N_GROUPS = 4
EXPERTS_PER_GROUP = 8
N_EXPERTS = N_GROUPS * EXPERTS_PER_GROUP
D_EXPERT = 512
N_ROUTE = N_GROUPS + N_EXPERTS

N_MOD = 6
MOD_SHIFT1, MOD_SCALE1, MOD_GATE1, MOD_SHIFT2, MOD_SCALE2, MOD_GATE2 = range(N_MOD)

LOG2_E = float(np.log2(np.e))
ROPE_BASE = 10000.0
EPS = 1e-6
DEPTH = 1
DEEPNORM_ALPHA = (2.0 * DEPTH) ** 0.25

LANES = 128
SUBLANES = 8
ROW_TILE = 256
N_PAIRS = 2 * N_TOK
N_TILES = N_PAIRS // ROW_TILE
TOKEN_SUB = D_MODEL // LANES

V7X_VMEM_BYTES = 64 * 1024 * 1024
VMEM_LIMIT = V7X_VMEM_BYTES - V7X_VMEM_BYTES // 8


def _params(sem):
    return pltpu.CompilerParams(dimension_semantics=sem, vmem_limit_bytes=VMEM_LIMIT)


def _silu(x):
    return x * (1.0 / (1.0 + jnp.exp(-x)))


def _plain_norm(x):
    mu = jnp.mean(x, -1, keepdims=True)
    xc = x - mu
    var = jnp.mean(xc * xc, -1, keepdims=True)
    return xc * lax.rsqrt(var + EPS)


def _to_token_tiles(x):
    return x.astype(BF16).reshape(x.shape[0], TOKEN_SUB, LANES)


def _from_token_tiles(x):
    return x.reshape(x.shape[0], D_MODEL)


ADA_TN = 1024


def _ada_kernel(cc_ref, w_ref, b_ref, o_ref):
    s = _silu(cc_ref[...])
    o_ref[...] = jnp.dot(s, w_ref[...], preferred_element_type=F32,
                         precision=lax.Precision.HIGHEST) + b_ref[...]


def _ada(cc, w_ada, b_ada):
    n = w_ada.shape[1]
    return pl.pallas_call(
        _ada_kernel,
        grid=(n // ADA_TN,),
        in_specs=[
            pl.BlockSpec((SUBLANES, D_MODEL), lambda j: (0, 0)),
            pl.BlockSpec((D_MODEL, ADA_TN), lambda j: (0, j)),
            pl.BlockSpec((1, ADA_TN), lambda j: (0, j)),
        ],
        out_specs=pl.BlockSpec((SUBLANES, ADA_TN), lambda j: (0, j)),
        out_shape=jax.ShapeDtypeStruct((SUBLANES, n), F32),
        compiler_params=_params(("arbitrary",)),
        name="ada",
    )(cc, w_ada, b_ada)


INPROJ_TM = 1024
INPROJ_TN = 1024


INPROJ_SUB = 256
INPROJ_BLOCKS = -(-IN_W // INPROJ_TN)


def _inproj_products(x_ref, shift_ref, scale_ref, w, o_ref, xn_ref):
    first = pl.program_id(1) == 0

    @pl.when(first)
    def _():
        for s in range(INPROJ_TM // INPROJ_SUB):
            rows = slice(s * INPROJ_SUB, (s + 1) * INPROJ_SUB)
            y = (_plain_norm(x_ref[rows, :]) * (1.0 + scale_ref[0]) + shift_ref[0]).astype(BF16)
            xn_ref[rows, :] = y
            o_ref[rows, :] = _dot_nt(y, w[...]).astype(BF16)

    @pl.when(jnp.logical_not(first))
    def _():
        o_ref[...] = _dot_nt(xn_ref[...], w[...]).astype(BF16)


def _inproj_latent_kernel(x_ref, shift_ref, scale_ref, wf_ref, o_ref, wbf_ref, xn_ref, wbuf_ref, in_sem, out_sem):
    i = pl.program_id(0)
    k = pl.program_id(1)
    t = i * INPROJ_BLOCKS + k
    slot = t % 2
    n_steps = pl.num_programs(0) * INPROJ_BLOCKS

    def out_copy(block, s):
        return pltpu.make_async_copy(wbuf_ref.at[s], wbf_ref.at[block], out_sem.at[s])

    def in_copy(block, s):
        return pltpu.make_async_copy(wbf_ref.at[block], wbuf_ref.at[s], in_sem.at[s])

    @pl.when(i == 0)
    def _():
        @pl.when(k >= 2)
        def _():
            out_copy(k - 2, slot).wait()

        col = lax.broadcasted_iota(I32, (INPROJ_TN, D_MODEL), 0)
        wbuf_ref[slot] = jnp.where(col < IN_W - k * INPROJ_TN, wf_ref[0], 0.0).astype(BF16)
        out_copy(k, slot).start()

    @pl.when((t + 1 >= INPROJ_BLOCKS) & (t + 1 < n_steps))
    def _():
        @pl.when(t == INPROJ_BLOCKS - 1)
        def _():
            out_copy(INPROJ_BLOCKS - 2, 1 - slot).wait()

        @pl.when(t == INPROJ_BLOCKS)
        def _():
            out_copy(INPROJ_BLOCKS - 1, 1 - slot).wait()

        in_copy((k + 1) % INPROJ_BLOCKS, 1 - slot).start()

    @pl.when(i > 0)
    def _():
        in_copy(k, slot).wait()

    _inproj_products(x_ref, shift_ref, scale_ref, wbuf_ref.at[slot], o_ref, xn_ref)


def _inproj_context_kernel(x_ref, shift_ref, scale_ref, w_ref, o_ref, xn_ref):
    _inproj_products(x_ref, shift_ref, scale_ref, w_ref.at[0], o_ref, xn_ref)


def _inproj_mod_map(j, tiles_per_sample, sample_row0):
    return lambda i, k: ((sample_row0 + i // tiles_per_sample) * N_MOD + j, 0, 0)


def _inproj_latent(x2d, mod_rows, w_in_t):
    assert INPROJ_BLOCKS % 2 == 1
    last = INPROJ_BLOCKS - 1
    return pl.pallas_call(
        _inproj_latent_kernel,
        grid=(N_TOK // INPROJ_TM, INPROJ_BLOCKS),
        in_specs=[
            pl.BlockSpec((INPROJ_TM, D_MODEL), lambda i, k: (i, 0)),
            pl.BlockSpec((1, 1, D_MODEL), _inproj_mod_map(MOD_SHIFT1, SEQ // INPROJ_TM, 0)),
            pl.BlockSpec((1, 1, D_MODEL), _inproj_mod_map(MOD_SCALE1, SEQ // INPROJ_TM, 0)),
            pl.BlockSpec((1, INPROJ_TN, D_MODEL), lambda i, k: (0, jnp.where(i == 0, k, last), 0)),
        ],
        out_specs=[pl.BlockSpec((INPROJ_TM, INPROJ_TN), lambda i, k: (i, k)), pl.BlockSpec(memory_space=pl.ANY)],
        out_shape=[jax.ShapeDtypeStruct((N_TOK, IN_W), BF16),
                   jax.ShapeDtypeStruct((INPROJ_BLOCKS, INPROJ_TN, D_MODEL), BF16)],
        scratch_shapes=[pltpu.VMEM((INPROJ_TM, D_MODEL), BF16), pltpu.VMEM((2, INPROJ_TN, D_MODEL), BF16),
                        pltpu.SemaphoreType.DMA((2,)), pltpu.SemaphoreType.DMA((2,))],
        compiler_params=_params(("arbitrary", "arbitrary")),
        name="inproj",
    )(x2d, mod_rows, mod_rows, w_in_t)


def _inproj_context(ctx2d, mod_rows, w_blocks):
    return pl.pallas_call(
        _inproj_context_kernel,
        grid=(N_CTX // INPROJ_TM, INPROJ_BLOCKS),
        in_specs=[
            pl.BlockSpec((INPROJ_TM, D_MODEL), lambda i, k: (i, 0)),
            pl.BlockSpec((1, 1, D_MODEL), _inproj_mod_map(MOD_SHIFT1, N_CTX // INPROJ_TM, BATCH)),
            pl.BlockSpec((1, 1, D_MODEL), _inproj_mod_map(MOD_SCALE1, N_CTX // INPROJ_TM, BATCH)),
            pl.BlockSpec((1, INPROJ_TN, D_MODEL), lambda i, k: (k, 0, 0)),
        ],
        out_specs=pl.BlockSpec((INPROJ_TM, INPROJ_TN), lambda i, k: (i, k)),
        out_shape=jax.ShapeDtypeStruct((N_CTX, IN_W), BF16),
        scratch_shapes=[pltpu.VMEM((INPROJ_TM, D_MODEL), BF16)],
        compiler_params=_params(("arbitrary", "arbitrary")),
        name="inproj_ctx",
    )(ctx2d, mod_rows, mod_rows, w_blocks)


def _rope_tables(width):
    half = width // 2
    quarter = half // 2
    inv_freq = ROPE_BASE ** (-np.arange(0, half, 2, dtype=np.float64) / half)
    t = np.arange(SEQ)
    cos_parts, sin_parts = [], []
    for pos in (t // GRID_W, t % GRID_W):
        ang = pos[:, None].astype(np.float64) * inv_freq[None, :]
        c, s = np.cos(ang), np.sin(ang)
        cos_parts += [c, c]
        sin_parts += [-s, s]
    assert cos_parts[0].shape[1] == quarter
    return (np.concatenate(cos_parts, 1).astype(np.float32), np.concatenate(sin_parts, 1).astype(np.float32))


def _rope(x, cos, sin, quarter):
    lane = lax.broadcasted_iota(I32, x.shape, 1)
    first = (lane % (2 * quarter)) < quarter
    swapped = jnp.where(first, pltpu.roll(x, LANES - quarter, 1), pltpu.roll(x, quarter, 1))
    return x * cos + swapped * sin


def _dot_tn(a, b):
    return lax.dot_general(a, b, (((0,), (0,)), ((), ())), preferred_element_type=F32)


def _dot_nt(a, b):
    return lax.dot_general(a, b, (((1,), (1,)), ((), ())), preferred_element_type=F32)


RET_HG = 4


def _retention_kernel(q_ref, k_ref, v_ref, g_ref, kc_ref, vc_ref, cos_ref, sin_ref, df_ref, db_ref, gn_ref,
                      o_ref, qs_ref, ks_ref, st_ref):
    for hh in range(RET_HG):
        _retention_head(hh, slice(hh * LANES, (hh + 1) * LANES), q_ref, k_ref, v_ref, g_ref, kc_ref, vc_ref,
                        cos_ref, sin_ref, df_ref, db_ref, gn_ref, o_ref, qs_ref, ks_ref, st_ref)


def _retention_head(hh, hs, q_ref, k_ref, v_ref, g_ref, kc_ref, vc_ref, cos_ref, sin_ref, df_ref, db_ref, gn_ref,
                    o_ref, qs_ref, ks_ref, st_ref):
    k_scale = RET_DK ** -0.5
    lgf = jax.nn.log_sigmoid(df_ref[hh])
    lgb = jax.nn.log_sigmoid(db_ref[hh])

    cos = cos_ref[...]
    sin = sin_ref[...]
    qs_ref[hh] = _rope(q_ref[:, hs].astype(F32), cos, sin, RET_DK // 4).astype(BF16)
    ks_ref[hh] = _rope(k_ref[:, hs].astype(F32), cos, sin, RET_DK // 4) * k_scale

    rowi = lax.broadcasted_iota(I32, (CHUNK, LANES), 0).astype(F32)
    coli = lax.broadcasted_iota(I32, (CHUNK, LANES), 1).astype(F32)
    diff = rowi - coli
    decay = jnp.exp(jnp.where(diff >= 0, lgf * diff, -lgb * diff)) * jnp.where(diff == 0, 2.0, 1.0)
    zeta_f = jnp.exp(lgf * (CHUNK - 1.0 - rowi))
    eta_b = jnp.exp(lgb * rowi)
    xi_f = jnp.exp(lgf * (rowi + 1.0))
    xi_b = jnp.exp(lgb * (CHUNK - rowi))
    cdec_f = jnp.exp(lgf * float(CHUNK))
    cdec_b = jnp.exp(lgb * float(CHUNK))

    crow = lax.broadcasted_iota(I32, (CTX_LEN, LANES), 0).astype(F32)
    kc = kc_ref[:, hs].astype(F32) * k_scale
    vc = vc_ref[:, hs]
    s_f = _dot_tn((kc * jnp.exp(lgf * (CTX_LEN - 1.0 - crow))).astype(BF16), vc)
    s_b = _dot_tn((kc * jnp.exp(lgb * crow)).astype(BF16), vc)

    upd_f, upd_b = [], []
    for i in range(N_CHUNKS):
        rows = pl.ds(i * CHUNK, CHUNK)
        kch = ks_ref[hh, rows, :]
        vch = v_ref[rows, hs]
        upd_f.append(_dot_tn((kch * zeta_f).astype(BF16), vch))
        upd_b.append(_dot_tn((kch * eta_b).astype(BF16), vch))
    state = s_f
    for i in range(N_CHUNKS):
        st_ref[hh, i, :, :RET_DV] = state.astype(BF16)
        state = cdec_f * state + upd_f[i]
    state = s_b
    for i in reversed(range(N_CHUNKS)):
        st_ref[hh, i, :, RET_DV:] = state.astype(BF16)
        state = cdec_b * state + upd_b[i]

    gn_w = gn_ref[:, hs]
    for i in range(N_CHUNKS):
        rows = pl.ds(i * CHUNK, CHUNK)
        qch = qs_ref[hh, rows, :]
        scores = _dot_nt(qch, ks_ref[hh, rows, :].astype(BF16)) * decay
        o = jnp.dot(scores.astype(BF16), v_ref[rows, hs], preferred_element_type=F32)
        cross = jnp.dot(qch, st_ref[hh, i], preferred_element_type=F32)
        o = o + xi_f * cross[:, :RET_DV] + xi_b * cross[:, RET_DV:]
        y = _plain_norm(o) * gn_w
        o_ref[rows, hs] = (_silu(g_ref[rows, hs].astype(F32)) * y).astype(BF16)


def _retention(proj, proj_c, cos, sin, decay_rows, gn_w):
    width = RET_HG * LANES
    groups = RET_HEADS // RET_HG
    blk = lambda off: pl.BlockSpec((SEQ, width), lambda b, hg: (b, off // width + hg))
    blk_c = lambda off: pl.BlockSpec((CTX_LEN, width), lambda b, hg: (b, off // width + hg))
    table = pl.BlockSpec((SEQ, LANES), lambda b, hg: (0, 0))
    return pl.pallas_call(
        _retention_kernel,
        grid=(BATCH, groups),
        in_specs=[
            blk(OFF_Q), blk(OFF_K), blk(OFF_V), blk(OFF_G), blk_c(OFF_K), blk_c(OFF_V), table, table,
            pl.BlockSpec((RET_HG, 1, LANES), lambda b, hg: (hg, 0, 0)),
            pl.BlockSpec((RET_HG, 1, LANES), lambda b, hg: (groups + hg, 0, 0)),
            pl.BlockSpec((1, width), lambda b, hg: (0, hg)),
        ],
        out_specs=pl.BlockSpec((SEQ, width), lambda b, hg: (b, hg)),
        out_shape=jax.ShapeDtypeStruct((N_TOK, RET_W), BF16),
        scratch_shapes=[
            pltpu.VMEM((RET_HG, SEQ, RET_DK), BF16),
            pltpu.VMEM((RET_HG, SEQ, RET_DK), F32),
            pltpu.VMEM((RET_HG, N_CHUNKS, RET_DK, 2 * RET_DV), BF16),
        ],
        compiler_params=_params(("arbitrary", "arbitrary")),
        name="retention",
    )(proj, proj, proj, proj, proj_c, proj_c, cos, sin, decay_rows, decay_rows, gn_w)


MLA_TM = 512


def _rms_norm(x, w):
    return x * lax.rsqrt(jnp.mean(x * x, -1, keepdims=True) + EPS) * w


def _mla_kv(ckv_ref, kpe_ref, kvn_ref, wkv_ref, cos_ref, sin_ref, k_ref, v_ref, rotate):
    ckv = _rms_norm(ckv_ref[...].astype(F32), kvn_ref[...]).astype(BF16)
    kv = jnp.dot(ckv, wkv_ref[...], preferred_element_type=F32)
    lane = lax.broadcasted_iota(I32, (ckv.shape[0], LANES), 1)
    kpe = jnp.where(lane < MLA_ROPE, kpe_ref[...].astype(F32), 0.0)
    if rotate:
        kpe = _rope(kpe, cos_ref[...], sin_ref[...], MLA_ROPE // 4)
    kpe = kpe.astype(BF16)
    for h in range(MLA_HEADS):
        k_ref[:, h * QK_PAD:h * QK_PAD + MLA_NOPE] = kv[:, 2 * h * LANES:(2 * h + 1) * LANES].astype(BF16)
        k_ref[:, h * QK_PAD + MLA_NOPE:(h + 1) * QK_PAD] = kpe
        v_ref[:, h * MLA_DV:(h + 1) * MLA_DV] = kv[:, (2 * h + 1) * LANES:(2 * h + 2) * LANES].astype(BF16)


def _mla_latent_kernel(cq_ref, ckv_ref, kpe_ref, qn_ref, kvn_ref, wq_ref, wkv_ref, cos_ref, sin_ref,
                       q_ref, k_ref, v_ref):
    cq = _rms_norm(cq_ref[...].astype(F32), qn_ref[...]).astype(BF16)
    q = jnp.dot(cq, wq_ref[...], preferred_element_type=F32)
    cos = cos_ref[...]
    sin = sin_ref[...]
    scale = MLA_DQ ** -0.5 * LOG2_E
    for h in range(MLA_HEADS):
        lo = h * QK_PAD
        q_ref[:, lo:lo + MLA_NOPE] = (q[:, lo:lo + MLA_NOPE] * scale).astype(BF16)
        qpe = _rope(q[:, lo + MLA_NOPE:lo + QK_PAD], cos, sin, MLA_ROPE // 4)
        q_ref[:, lo + MLA_NOPE:lo + QK_PAD] = (qpe * scale).astype(BF16)
    _mla_kv(ckv_ref, kpe_ref, kvn_ref, wkv_ref, cos_ref, sin_ref, k_ref, v_ref, rotate=True)


def _mla_context_kernel(ckv_ref, kpe_ref, kvn_ref, wkv_ref, k_ref, v_ref):
    _mla_kv(ckv_ref, kpe_ref, kvn_ref, wkv_ref, None, None, k_ref, v_ref, rotate=False)


def _mla_latent(proj, q_norm, kv_norm, wq_pad, wkv, cos, sin):
    row = lambda w, off: pl.BlockSpec((MLA_TM, w), lambda i: (i, off // w))
    full = lambda a: pl.BlockSpec(a.shape, lambda i: (0, 0))
    table = pl.BlockSpec((MLA_TM, LANES), lambda i: (i % (SEQ // MLA_TM), 0))
    return pl.pallas_call(
        _mla_latent_kernel,
        grid=(N_TOK // MLA_TM,),
        in_specs=[row(MLA_Q_LORA, OFF_CQ), row(MLA_KV_LORA, OFF_CKV), row(LANES, OFF_KPE),
                  full(q_norm), full(kv_norm), full(wq_pad), full(wkv), table, table],
        out_specs=[
            pl.BlockSpec((MLA_TM, MLA_HEADS * QK_PAD), lambda i: (i, 0)),
            pl.BlockSpec((MLA_TM, MLA_HEADS * QK_PAD), lambda i: (i, 0)),
            pl.BlockSpec((MLA_TM, MLA_W), lambda i: (i, 0)),
        ],
        out_shape=[
            jax.ShapeDtypeStruct((N_TOK, MLA_HEADS * QK_PAD), BF16),
            jax.ShapeDtypeStruct((N_TOK, MLA_HEADS * QK_PAD), BF16),
            jax.ShapeDtypeStruct((N_TOK, MLA_W), BF16),
        ],
        compiler_params=_params(("arbitrary",)),
        name="mla_latent",
    )(proj, proj, proj, q_norm, kv_norm, wq_pad, wkv, cos, sin)


def _mla_context(proj_c, kv_norm, wkv):
    row = lambda w, off: pl.BlockSpec((MLA_TM, w), lambda i: (i, off // w))
    full = lambda a: pl.BlockSpec(a.shape, lambda i: (0, 0))
    return pl.pallas_call(
        _mla_context_kernel,
        grid=(N_CTX // MLA_TM,),
        in_specs=[row(MLA_KV_LORA, OFF_CKV), row(LANES, OFF_KPE), full(kv_norm), full(wkv)],
        out_specs=[
            pl.BlockSpec((MLA_TM, MLA_HEADS * QK_PAD), lambda i: (i, 0)),
            pl.BlockSpec((MLA_TM, MLA_W), lambda i: (i, 0)),
        ],
        out_shape=[
            jax.ShapeDtypeStruct((N_CTX, MLA_HEADS * QK_PAD), BF16),
            jax.ShapeDtypeStruct((N_CTX, MLA_W), BF16),
        ],
        compiler_params=_params(("arbitrary",)),
        name="mla_context",
    )(proj_c, proj_c, kv_norm, wkv)


ATT_TQ = 512


def _attention_kernel(q_ref, kc_ref, kl_ref, vc_ref, vl_ref, wo_ref, o_ref, wo_bf_ref, vext_ref):
    wo_bf_ref[...] = wo_ref[...].astype(BF16)

    @pl.when(pl.program_id(1) == 0)
    def _():
        lane = lax.broadcasted_iota(I32, (CTX_LEN + SEQ, MLA_DV), 1)
        ones_col = jnp.where(lane == 0, 1.0, 0.0).astype(BF16)
        for h in range(MLA_HEADS):
            dv = slice(h * MLA_DV, (h + 1) * MLA_DV)
            vext_ref[h, :CTX_LEN, :MLA_DV] = vc_ref[:, dv]
            vext_ref[h, CTX_LEN:, :MLA_DV] = vl_ref[:, dv]
            vext_ref[h, :, MLA_DV:] = ones_col

    for h in range(MLA_HEADS):
        qk = slice(h * QK_PAD, (h + 1) * QK_PAD)
        q = q_ref[:, qk]
        s_c = _dot_nt(q, kc_ref[:, qk])
        s_l = _dot_nt(q, kl_ref[:, qk])
        m = jnp.maximum(jnp.max(s_c, -1, keepdims=True), jnp.max(s_l, -1, keepdims=True))
        p_c = jnp.exp2(s_c - m).astype(BF16)
        p_l = jnp.exp2(s_l - m).astype(BF16)
        o = (jnp.dot(p_c, vext_ref[h, :CTX_LEN, :], preferred_element_type=F32)
             + jnp.dot(p_l, vext_ref[h, CTX_LEN:, :], preferred_element_type=F32))
        o_ref[:, h * MLA_DV:(h + 1) * MLA_DV] = (o[:, :MLA_DV] / o[:, MLA_DV:MLA_DV + 1]).astype(BF16)


def _attention(q, k_ctx, k_lat, v_ctx, v_lat, w_o):
    tiles = SEQ // ATT_TQ
    band = w_o.shape[0] // (BATCH * tiles)
    step = lambda b, i: (b * tiles + i, 0)
    return pl.pallas_call(
        _attention_kernel,
        grid=(BATCH, tiles),
        in_specs=[
            pl.BlockSpec((ATT_TQ, MLA_HEADS * QK_PAD), step),
            pl.BlockSpec((CTX_LEN, MLA_HEADS * QK_PAD), lambda b, i: (b, 0)),
            pl.BlockSpec((SEQ, MLA_HEADS * QK_PAD), lambda b, i: (b, 0)),
            pl.BlockSpec((CTX_LEN, MLA_W), lambda b, i: (b, 0)),
            pl.BlockSpec((SEQ, MLA_W), lambda b, i: (b, 0)),
            pl.BlockSpec((band, D_MODEL), step),
        ],
        out_specs=[pl.BlockSpec((ATT_TQ, MLA_W), step), pl.BlockSpec((band, D_MODEL), step)],
        out_shape=[jax.ShapeDtypeStruct((N_TOK, MLA_W), BF16), jax.ShapeDtypeStruct(w_o.shape, BF16)],
        scratch_shapes=[pltpu.VMEM((MLA_HEADS, CTX_LEN + SEQ, 2 * MLA_DV), BF16)],
        compiler_params=_params(("arbitrary", "arbitrary")),
        name="attention",
    )(q, k_ctx, k_lat, v_ctx, v_lat, w_o)


OUT_TM = 512
OUT_SUB = 256
ROUTE_E1, ROUTE_E2, ROUTE_R1, ROUTE_R2, ROUTE_W1, ROUTE_W2 = range(6)


def _outproj_kernel(ret_ref, att_ref, x_ref, wo_ref, gate_ref, shift_ref, scale_ref, lnw_ref, lnb_ref,
                    wr_ref, br_ref, h_ref, t_ref, route_ref, route_t_ref, count_ref, carry_ref, wsplit_ref):
    @pl.when(pl.program_id(0) == 0)
    def _():
        carry_ref[...] = jnp.zeros_like(carry_ref)
        w = wr_ref[...]
        w_hi = w.astype(BF16)
        wsplit_ref[:, :LANES] = w_hi
        wsplit_ref[:, LANES:] = (w - w_hi.astype(F32)).astype(BF16)

    subtiles = [slice(s * OUT_SUB, (s + 1) * OUT_SUB) for s in range(OUT_TM // OUT_SUB)]
    mixes = [jnp.dot(ret_ref[rows, :], wo_ref[:RET_W, :], preferred_element_type=F32)
             + jnp.dot(att_ref[rows, :], wo_ref[RET_W:, :], preferred_element_type=F32) for rows in subtiles]
    logits = [_outproj_norms(rows, mix, x_ref, gate_ref, shift_ref, scale_ref, lnw_ref, lnb_ref, br_ref, h_ref, t_ref,
                             wsplit_ref) for rows, mix in zip(subtiles, mixes)]
    _route(jnp.concatenate(logits, 0), carry_ref, count_ref, route_ref, route_t_ref)


def _outproj_norms(rows, mix, x_ref, gate_ref, shift_ref, scale_ref, lnw_ref, lnb_ref, br_ref, h_ref, t_ref,
                   wsplit_ref):
    h = _plain_norm(DEEPNORM_ALPHA * x_ref[rows, :] + gate_ref[0] * mix) * lnw_ref[...] + lnb_ref[...]
    h_ref[rows, :] = h
    t = _plain_norm(h) * (1.0 + scale_ref[0]) + shift_ref[0]
    t_ref[rows] = _to_token_tiles(t)

    t_hi = t.astype(BF16)
    t_lo = (t - t_hi.astype(F32)).astype(BF16)
    main = jnp.dot(t_hi, wsplit_ref[...], preferred_element_type=F32)
    corr = jnp.dot(t_lo, wsplit_ref[:, :LANES], preferred_element_type=F32)
    return main[:, :LANES] + (main[:, LANES:] + corr) + br_ref[...]


def _route(logits, carry_ref, count_ref, route_ref, route_t_ref):
    lane = lax.broadcasted_iota(I32, logits.shape, 1).astype(F32)
    neg = -jnp.inf
    big = float(LANES)

    def first_lane_of(mask):
        return jnp.min(jnp.where(mask, lane, big), -1, keepdims=True)

    is_group = lane < N_GROUPS
    gl = jnp.where(is_group, logits, neg)
    g_max = jnp.max(gl, -1, keepdims=True)
    g_idx = first_lane_of(is_group & (gl == g_max))
    g_prob = 1.0 / jnp.sum(jnp.where(is_group, jnp.exp(logits - g_max), 0.0), -1, keepdims=True)

    lo = N_GROUPS + g_idx * EXPERTS_PER_GROUP
    in_group = (lane >= lo) & (lane < lo + EXPERTS_PER_GROUP)
    el = jnp.where(in_group, logits, neg)
    v1 = jnp.max(el, -1, keepdims=True)
    i1 = first_lane_of(in_group & (el == v1))
    rest = in_group & (lane != i1)
    el2 = jnp.where(rest, logits, neg)
    v2 = jnp.max(el2, -1, keepdims=True)
    i2 = first_lane_of(rest & (el2 == v2))
    d = jnp.exp(v2 - v1)
    w1 = g_prob / (1.0 + d)
    w2 = g_prob * d / (1.0 + d)

    onehot = jnp.where((lane == i1) | (lane == i2), 1.0, 0.0)
    r = lax.broadcasted_iota(I32, (OUT_TM, OUT_TM), 0)
    c = lax.broadcasted_iota(I32, (OUT_TM, OUT_TM), 1)
    tri = jnp.where(c < r, 1.0, 0.0).astype(BF16)
    carry = carry_ref[0:1, :]
    before = jnp.dot(tri, onehot.astype(BF16), preferred_element_type=F32) + carry
    r1 = jnp.sum(jnp.where(lane == i1, before, 0.0), -1, keepdims=True)
    r2 = jnp.sum(jnp.where(lane == i2, before, 0.0), -1, keepdims=True)
    carry = carry + jnp.sum(onehot, 0, keepdims=True)
    carry_ref[...] = jnp.broadcast_to(carry, carry_ref.shape)
    count_ref[...] = jnp.broadcast_to(carry, count_ref.shape)

    rec = jnp.zeros_like(logits)
    for slot, val in ((ROUTE_E1, i1 - N_GROUPS), (ROUTE_E2, i2 - N_GROUPS), (ROUTE_R1, r1), (ROUTE_R2, r2),
                      (ROUTE_W1, w1), (ROUTE_W2, w2)):
        rec = jnp.where(lane == slot, val, rec)
    route_ref[...] = rec
    route_t_ref[...] = rec.T[:SUBLANES, :]


def _outproj(ret, att, x2d, wo_bf, mod_rows, ln_w, ln_b, w_route, b_route):
    tiles_per_sample = SEQ // OUT_TM
    row = lambda w: pl.BlockSpec((OUT_TM, w), lambda i: (i, 0))
    full = lambda a: pl.BlockSpec(a.shape, lambda i: (0, 0))
    mod = lambda j: pl.BlockSpec((1, 1, D_MODEL), lambda i: ((i // tiles_per_sample) * N_MOD + j, 0, 0))
    return pl.pallas_call(
        _outproj_kernel,
        grid=(N_TOK // OUT_TM,),
        in_specs=[row(RET_W), row(MLA_W), row(D_MODEL), full(wo_bf), mod(MOD_GATE1), mod(MOD_SHIFT2), mod(MOD_SCALE2),
                  full(ln_w), full(ln_b), full(w_route), full(b_route)],
        out_specs=[row(D_MODEL), pl.BlockSpec((OUT_TM, TOKEN_SUB, LANES), lambda i: (i, 0, 0)), row(LANES),
                   pl.BlockSpec((SUBLANES, OUT_TM), lambda i: (0, i)),
                   pl.BlockSpec((SUBLANES, LANES), lambda i: (0, 0))],
        out_shape=[
            jax.ShapeDtypeStruct((N_TOK, D_MODEL), F32),
            jax.ShapeDtypeStruct((N_TOK, TOKEN_SUB, LANES), BF16),
            jax.ShapeDtypeStruct((N_TOK, LANES), F32),
            jax.ShapeDtypeStruct((SUBLANES, N_TOK), F32),
            jax.ShapeDtypeStruct((SUBLANES, LANES), F32),
        ],
        scratch_shapes=[pltpu.VMEM((SUBLANES, LANES), F32), pltpu.VMEM((D_MODEL, 2 * LANES), BF16)],
        compiler_params=_params(("arbitrary",)),
        name="outproj_route",
    )(ret, att, x2d, wo_bf, mod_rows, mod_rows, mod_rows, ln_w, ln_b, w_route, b_route)


DISPATCH_TM = 1024


def _dispatch_kernel(pos_ref, t_ref, xs_ref, sem):
    i = pl.program_id(0)

    def copy(slot, j):
        return pltpu.make_async_copy(t_ref.at[j], xs_ref.at[pos_ref[slot * N_TOK + i * DISPATCH_TM + j]], sem)

    for slot in range(2):
        lax.fori_loop(0, DISPATCH_TM, lambda j, c, slot=slot: (copy(slot, j).start(priority=slot), c)[1], 0,
                      unroll=8)
    for slot in range(2):
        pltpu.make_async_copy(t_ref, xs_ref.at[pl.ds(0, DISPATCH_TM)], sem).wait()


def _dispatch(pos, t_tiles):
    return pl.pallas_call(
        _dispatch_kernel,
        grid_spec=pltpu.PrefetchScalarGridSpec(
            num_scalar_prefetch=1,
            grid=(N_TOK // DISPATCH_TM,),
            in_specs=[pl.BlockSpec((DISPATCH_TM, TOKEN_SUB, LANES), lambda i, pos: (i, 0, 0))],
            out_specs=pl.BlockSpec(memory_space=pl.ANY),
            scratch_shapes=[pltpu.SemaphoreType.DMA(())],
        ),
        out_shape=jax.ShapeDtypeStruct((N_PAIRS, TOKEN_SUB, LANES), BF16),
        compiler_params=_params(("arbitrary",)),
        name="dispatch",
    )(pos, t_tiles)


def _experts_kernel(rend_ref, xs_ref, wg_ref, wu_ref, wd_ref, ys_ref, wgu_ref, wdn_ref, xbuf_ref, ybuf_ref,
                    xsem, ysem):
    e = pl.program_id(0)
    lo = jnp.where(e == 0, 0, rend_ref[jnp.maximum(e - 1, 0)])
    hi = rend_ref[e]

    def x_copy(g):
        return pltpu.make_async_copy(xs_ref.at[pl.ds(g * ROW_TILE, ROW_TILE)], xbuf_ref.at[g % 2], xsem.at[g % 2])

    def y_copy(g):
        return pltpu.make_async_copy(ybuf_ref.at[g % 2], ys_ref.at[pl.ds(g * ROW_TILE, ROW_TILE)], ysem.at[g % 2])

    @pl.when(e == 0)
    def _():
        x_copy(0).start()

    @pl.when(hi > lo)
    def _():
        wgu_ref[:, :D_EXPERT] = wg_ref[0].astype(BF16)
        wgu_ref[:, D_EXPERT:] = wu_ref[0].astype(BF16)
        wdn_ref[...] = wd_ref[0].astype(BF16)

        def tile(g, c):
            row0 = g * ROW_TILE
            owns_first_row = lo <= row0
            owns_last_row = hi >= row0 + ROW_TILE

            @pl.when(owns_first_row)
            def _():
                @pl.when(g + 1 < N_TILES)
                def _():
                    x_copy(g + 1).start()

                x_copy(g).wait()

            gu = jnp.dot(_from_token_tiles(xbuf_ref[g % 2]), wgu_ref[...], preferred_element_type=F32)
            hid = _silu(gu[:, :D_EXPERT]) * gu[:, D_EXPERT:]
            y = jnp.dot(hid.astype(BF16), wdn_ref[...], preferred_element_type=F32)

            @pl.when(owns_first_row)
            def _():
                @pl.when(g >= 2)
                def _():
                    y_copy(g - 2).wait()

                ybuf_ref[g % 2] = _to_token_tiles(y)

            @pl.when(jnp.logical_not(owns_first_row))
            def _():
                row = lax.broadcasted_iota(I32, y.shape, 0)
                earlier = _from_token_tiles(ybuf_ref[g % 2]).astype(F32)
                ybuf_ref[g % 2] = _to_token_tiles(jnp.where(row >= lo - row0, y, earlier))

            @pl.when(owns_last_row)
            def _():
                y_copy(g).start()

            return c

        lax.fori_loop(lo // ROW_TILE, (hi - 1) // ROW_TILE + 1, tile, 0)

    @pl.when(e == N_EXPERTS - 1)
    def _():
        y_copy(N_TILES - 2).wait()
        y_copy(N_TILES - 1).wait()


def _experts(row_end, xs, w_gate, w_up, w_down):
    w_gate = w_gate.reshape(N_EXPERTS, D_MODEL, D_EXPERT)
    w_up = w_up.reshape(N_EXPERTS, D_MODEL, D_EXPERT)
    w_down = w_down.reshape(N_EXPERTS, D_EXPERT, D_MODEL)
    expert = lambda e, rend: (e, 0, 0)
    tile_buf = pltpu.VMEM((2, ROW_TILE, TOKEN_SUB, LANES), BF16)
    return pl.pallas_call(
        _experts_kernel,
        grid_spec=pltpu.PrefetchScalarGridSpec(
            num_scalar_prefetch=1,
            grid=(N_EXPERTS,),
            in_specs=[
                pl.BlockSpec(memory_space=pl.ANY),
                pl.BlockSpec((1, D_MODEL, D_EXPERT), expert),
                pl.BlockSpec((1, D_MODEL, D_EXPERT), expert),
                pl.BlockSpec((1, D_EXPERT, D_MODEL), expert),
            ],
            out_specs=pl.BlockSpec(memory_space=pl.ANY),
            scratch_shapes=[pltpu.VMEM((D_MODEL, 2 * D_EXPERT), BF16), pltpu.VMEM((D_EXPERT, D_MODEL), BF16),
                            tile_buf, tile_buf, pltpu.SemaphoreType.DMA((2,)), pltpu.SemaphoreType.DMA((2,))],
        ),
        out_shape=jax.ShapeDtypeStruct((N_PAIRS, TOKEN_SUB, LANES), BF16),
        compiler_params=_params(("arbitrary",)),
        name="experts",
    )(row_end, xs, w_gate, w_up, w_down)


COMB_TM = 256


def _combine_kernel(pos_ref, ys_ref, h_ref, route_ref, gate_ref, lnw_ref, lnb_ref, o_ref, buf_ref, sem):
    i = pl.program_id(0)
    n = pl.num_programs(0)

    def copy(step, slot, pair, j):
        src = pos_ref[pair * N_TOK + step * COMB_TM + j]
        return pltpu.make_async_copy(ys_ref.at[src], buf_ref.at[slot, pair * COMB_TM + j], sem.at[slot])

    def start_all(step, slot):
        for pair in range(2):
            lax.fori_loop(0, COMB_TM, lambda j, c, pair=pair: (copy(step, slot, pair, j).start(priority=pair), c)[1], 0,
                          unroll=8)

    def wait_all(step, slot):
        pltpu.make_async_copy(ys_ref.at[pl.ds(0, 2 * COMB_TM)], buf_ref.at[slot], sem.at[slot]).wait()

    @pl.when(i == 0)
    def _():
        start_all(0, 0)

    slot = i % 2
    wait_all(i, slot)

    nxt = jnp.minimum(i + 1, n - 1)
    for pair in range(2):
        for j in range(COMB_TM):
            copy(nxt, 1 - slot, pair, j).start(priority=pair)

    y1 = _from_token_tiles(buf_ref[slot, :COMB_TM]).astype(F32)
    y2 = _from_token_tiles(buf_ref[slot, COMB_TM:]).astype(F32)
    y = route_ref[:, ROUTE_W1:ROUTE_W1 + 1] * y1 + route_ref[:, ROUTE_W2:ROUTE_W2 + 1] * y2
    z = DEEPNORM_ALPHA * h_ref[...] + gate_ref[0] * y
    o_ref[...] = _plain_norm(z) * lnw_ref[...] + lnb_ref[...]

    @pl.when(i == n - 1)
    def _():
        wait_all(i, 1 - slot)


def _combine(pos, ys, h, route, mod_rows, ln_w, ln_b):
    tiles_per_sample = SEQ // COMB_TM
    row = lambda w: pl.BlockSpec((COMB_TM, w), lambda i, pos: (i, 0))
    full = lambda a: pl.BlockSpec(a.shape, lambda i, pos: (0, 0))
    return pl.pallas_call(
        _combine_kernel,
        grid_spec=pltpu.PrefetchScalarGridSpec(
            num_scalar_prefetch=1,
            grid=(N_TOK // COMB_TM,),
            in_specs=[
                pl.BlockSpec(memory_space=pl.ANY), row(D_MODEL), row(LANES),
                pl.BlockSpec((1, 1, D_MODEL), lambda i, pos: ((i // tiles_per_sample) * N_MOD + MOD_GATE2, 0, 0)),
                full(ln_w), full(ln_b),
            ],
            out_specs=row(D_MODEL),
            scratch_shapes=[pltpu.VMEM((2, 2 * COMB_TM, TOKEN_SUB, LANES), BF16), pltpu.SemaphoreType.DMA((2,))],
        ),
        out_shape=jax.ShapeDtypeStruct((N_TOK, D_MODEL), F32),
        compiler_params=_params(("arbitrary",)),
        name="combine",
    )(pos, ys, h, route, mod_rows, ln_w, ln_b)


def _routing_tables(route_t, counts):
    cnt = counts[0, N_GROUPS:N_ROUTE].astype(I32)
    row_end = jnp.cumsum(cnt)
    e = route_t[ROUTE_E1:ROUTE_E2 + 1].astype(I32)
    rank = route_t[ROUTE_R1:ROUTE_R2 + 1].astype(I32)
    earlier = jnp.arange(N_EXPERTS, dtype=I32)[:, None, None] < e[None]
    base = jnp.sum(jnp.where(earlier, cnt[:, None, None], 0), 0)
    pos = (base + rank).reshape(-1)
    return pos, row_end


def kernel(x, c, ctx, c_ctx, w_ada, b_ada, w_in, ret_decay, ret_gn_w, mla_q_norm, mla_kv_norm, w_uq, w_ukv, w_o,
           ln1_w, ln1_b, router_group_w, router_group_b, router_expert_w, router_expert_b, expert_w_gate,
           expert_w_up, expert_w_down, ln2_w, ln2_b):
    x2d = x.reshape(N_TOK, D_MODEL)
    ctx2d = ctx.reshape(N_CTX, D_MODEL)

    cc = jnp.zeros((SUBLANES, D_MODEL), F32).at[:BATCH].set(c).at[BATCH].set(c_ctx)
    mod = _ada(cc, w_ada[0], b_ada)
    mod_rows = mod.reshape(SUBLANES * N_MOD, 1, D_MODEL)

    proj, w_in_blocks = _inproj_latent(x2d, mod_rows, jnp.swapaxes(w_in, 1, 2))
    proj_c = _inproj_context(ctx2d, mod_rows, w_in_blocks)

    cos_r, sin_r = _rope_tables(RET_DK)
    decay_rows = jnp.broadcast_to(ret_decay[0].reshape(2 * RET_HEADS, 1, 1), (2 * RET_HEADS, 1, LANES))
    ret = _retention(proj, proj_c, jnp.asarray(cos_r), jnp.asarray(sin_r), decay_rows, ret_gn_w)

    cos_m, sin_m = _rope_tables(MLA_ROPE)
    cos_m = np.concatenate([cos_m, np.ones_like(cos_m)], 1)
    sin_m = np.concatenate([sin_m, np.zeros_like(sin_m)], 1)
    wq = w_uq[0].reshape(MLA_Q_LORA, MLA_HEADS, MLA_DQ)
    wq_pad = jnp.pad(wq, ((0, 0), (0, 0), (0, QK_PAD - MLA_DQ))).reshape(MLA_Q_LORA, MLA_HEADS * QK_PAD).astype(BF16)
    wkv = w_ukv[0].astype(BF16)
    q, k_lat, v_lat = _mla_latent(proj, mla_q_norm, mla_kv_norm, wq_pad, wkv, jnp.asarray(cos_m), jnp.asarray(sin_m))
    k_ctx, v_ctx = _mla_context(proj_c, mla_kv_norm, wkv)
    att, w_o_bf = _attention(q, k_ctx, k_lat, v_ctx, v_lat, w_o[0])

    w_route = jnp.concatenate(
        [router_group_w[0], router_expert_w[0].transpose(1, 0, 2).reshape(D_MODEL, N_EXPERTS),
         jnp.zeros((D_MODEL, LANES - N_ROUTE), F32)], 1)
    b_route = jnp.concatenate(
        [router_group_b[0], router_expert_b[0].reshape(N_EXPERTS), jnp.zeros((LANES - N_ROUTE,), F32)])[None]
    h, t_tiles, route, route_t, counts = _outproj(ret, att, x2d, w_o_bf, mod_rows, ln1_w, ln1_b,
                                                   w_route, b_route)

    pos, row_end = _routing_tables(route_t, counts)
    xs = _dispatch(pos, t_tiles)
    ys = _experts(row_end, xs, expert_w_gate[0], expert_w_up[0], expert_w_down[0])
    out = _combine(pos, ys, h, route, mod_rows, ln2_w, ln2_b)
    return out.reshape(BATCH, SEQ, D_MODEL)
```

```python
import numpy as np
import jax
import jax.numpy as jnp
from jax import lax
from jax.experimental import pallas as pl
from jax.experimental.pallas import tpu as pltpu

F32 = jnp.float32
BF16 = jnp.bfloat16
I32 = jnp.int32

D_MODEL = 2048
BATCH = 4
SEQ = 2048
GRID_W = 64
CTX_LEN = 256
N_TOK = BATCH * SEQ
N_CTX = BATCH * CTX_LEN

RET_HEADS = 8
RET_DK = 128
RET_DV = 128
RET_W = RET_HEADS * RET_DV
CHUNK = 128
N_CHUNKS = SEQ // CHUNK

MLA_HEADS = 8
MLA_Q_LORA = 512
MLA_KV_LORA = 256
MLA_NOPE = 128
MLA_ROPE = 64
MLA_DV = 128
MLA_W = MLA_HEADS * MLA_DV
MLA_DQ = MLA_NOPE + MLA_ROPE
QK_PAD = 256

IN_SIZES = (RET_HEADS * RET_DK, RET_HEADS * RET_DK, RET_W, RET_W, MLA_Q_LORA, MLA_KV_LORA, MLA_ROPE)
IN_W = sum(IN_SIZES)
OFF_Q, OFF_K, OFF_V, OFF_G, OFF_CQ, OFF_CKV, OFF_KPE = (int(v) for v in np.cumsum((0,) + IN_SIZES[:-1]))

N_GROUPS = 4
EXPERTS_PER_GROUP = 8
N_EXPERTS = N_GROUPS * EXPERTS_PER_GROUP
D_EXPERT = 512
N_ROUTE = N_GROUPS + N_EXPERTS

N_MOD = 6
MOD_SHIFT1, MOD_SCALE1, MOD_GATE1, MOD_SHIFT2, MOD_SCALE2, MOD_GATE2 = range(N_MOD)

LOG2_E = float(np.log2(np.e))
ROPE_BASE = 10000.0
EPS = 1e-6
DEPTH = 1
DEEPNORM_ALPHA = (2.0 * DEPTH) ** 0.25

LANES = 128
SUBLANES = 8
ROW_TILE = 256
N_PAIRS = 2 * N_TOK
N_TILES = N_PAIRS // ROW_TILE
TOKEN_SUB = D_MODEL // LANES

V7X_VMEM_BYTES = 64 * 1024 * 1024
VMEM_LIMIT = V7X_VMEM_BYTES - V7X_VMEM_BYTES // 8


def _params(sem):
    return pltpu.CompilerParams(dimension_semantics=sem, vmem_limit_bytes=VMEM_LIMIT)


def _silu(x):
    return x * (1.0 / (1.0 + jnp.exp(-x)))


def _plain_norm(x):
    mu = jnp.mean(x, -1, keepdims=True)
    xc = x - mu
    var = jnp.mean(xc * xc, -1, keepdims=True)
    return xc * lax.rsqrt(var + EPS)


def _to_token_tiles(x):
    return x.astype(BF16).reshape(x.shape[0], TOKEN_SUB, LANES)


def _from_token_tiles(x):
    return x.reshape(x.shape[0], D_MODEL)


ADA_TN = 1024


def _ada_kernel(cc_ref, w_ref, b_ref, o_ref):
    s = _silu(cc_ref[...])
    o_ref[...] = jnp.dot(s, w_ref[...], preferred_element_type=F32,
                         precision=lax.Precision.HIGHEST) + b_ref[...]


def _ada(cc, w_ada, b_ada):
    n = w_ada.shape[1]
    return pl.pallas_call(
        _ada_kernel,
        grid=(n // ADA_TN,),
        in_specs=[
            pl.BlockSpec((SUBLANES, D_MODEL), lambda j: (0, 0)),
            pl.BlockSpec((D_MODEL, ADA_TN), lambda j: (0, j)),
            pl.BlockSpec((1, ADA_TN), lambda j: (0, j)),
        ],
        out_specs=pl.BlockSpec((SUBLANES, ADA_TN), lambda j: (0, j)),
        out_shape=jax.ShapeDtypeStruct((SUBLANES, n), F32),
        compiler_params=_params(("arbitrary",)),
        name="ada",
    )(cc, w_ada, b_ada)


INPROJ_TM = 1024
INPROJ_TN = 1024


INPROJ_SUB = 256
INPROJ_BLOCKS = -(-IN_W // INPROJ_TN)


def _inproj_products(x_ref, shift_ref, scale_ref, w, o_ref, xn_ref):
    first = pl.program_id(1) == 0

    @pl.when(first)
    def _():
        for s in range(INPROJ_TM // INPROJ_SUB):
            rows = slice(s * INPROJ_SUB, (s + 1) * INPROJ_SUB)
            y = (_plain_norm(x_ref[rows, :]) * (1.0 + scale_ref[0]) + shift_ref[0]).astype(BF16)
            xn_ref[rows, :] = y
            o_ref[rows, :] = _dot_nt(y, w[...]).astype(BF16)

    @pl.when(jnp.logical_not(first))
    def _():
        o_ref[...] = _dot_nt(xn_ref[...], w[...]).astype(BF16)


def _inproj_latent_kernel(x_ref, shift_ref, scale_ref, wf_ref, o_ref, wbf_ref, xn_ref, wbuf_ref, in_sem, out_sem):
    i = pl.program_id(0)
    k = pl.program_id(1)
    t = i * INPROJ_BLOCKS + k
    slot = t % 2
    n_steps = pl.num_programs(0) * INPROJ_BLOCKS

    def out_copy(block, s):
        return pltpu.make_async_copy(wbuf_ref.at[s], wbf_ref.at[block], out_sem.at[s])

    def in_copy(block, s):
        return pltpu.make_async_copy(wbf_ref.at[block], wbuf_ref.at[s], in_sem.at[s])

    @pl.when(i == 0)
    def _():
        @pl.when(k >= 2)
        def _():
            out_copy(k - 2, slot).wait()

        col = lax.broadcasted_iota(I32, (INPROJ_TN, D_MODEL), 0)
        wbuf_ref[slot] = jnp.where(col < IN_W - k * INPROJ_TN, wf_ref[0], 0.0).astype(BF16)
        out_copy(k, slot).start()

    @pl.when((t + 1 >= INPROJ_BLOCKS) & (t + 1 < n_steps))
    def _():
        @pl.when(t == INPROJ_BLOCKS - 1)
        def _():
            out_copy(INPROJ_BLOCKS - 2, 1 - slot).wait()

        @pl.when(t == INPROJ_BLOCKS)
        def _():
            out_copy(INPROJ_BLOCKS - 1, 1 - slot).wait()

        in_copy((k + 1) % INPROJ_BLOCKS, 1 - slot).start()

    @pl.when(i > 0)
    def _():
        in_copy(k, slot).wait()

    _inproj_products(x_ref, shift_ref, scale_ref, wbuf_ref.at[slot], o_ref, xn_ref)


def _inproj_context_kernel(x_ref, shift_ref, scale_ref, w_ref, o_ref, xn_ref):
    _inproj_products(x_ref, shift_ref, scale_ref, w_ref.at[0], o_ref, xn_ref)


def _inproj_mod_map(j, tiles_per_sample, sample_row0):
    return lambda i, k: ((sample_row0 + i // tiles_per_sample) * N_MOD + j, 0, 0)


def _inproj_latent(x2d, mod_rows, w_in_t):
    assert INPROJ_BLOCKS % 2 == 1
    last = INPROJ_BLOCKS - 1
    return pl.pallas_call(
        _inproj_latent_kernel,
        grid=(N_TOK // INPROJ_TM, INPROJ_BLOCKS),
        in_specs=[
            pl.BlockSpec((INPROJ_TM, D_MODEL), lambda i, k: (i, 0)),
            pl.BlockSpec((1, 1, D_MODEL), _inproj_mod_map(MOD_SHIFT1, SEQ // INPROJ_TM, 0)),
            pl.BlockSpec((1, 1, D_MODEL), _inproj_mod_map(MOD_SCALE1, SEQ // INPROJ_TM, 0)),
            pl.BlockSpec((1, INPROJ_TN, D_MODEL), lambda i, k: (0, jnp.where(i == 0, k, last), 0)),
        ],
        out_specs=[pl.BlockSpec((INPROJ_TM, INPROJ_TN), lambda i, k: (i, k)), pl.BlockSpec(memory_space=pl.ANY)],
        out_shape=[jax.ShapeDtypeStruct((N_TOK, IN_W), BF16),
                   jax.ShapeDtypeStruct((INPROJ_BLOCKS, INPROJ_TN, D_MODEL), BF16)],
        scratch_shapes=[pltpu.VMEM((INPROJ_TM, D_MODEL), BF16), pltpu.VMEM((2, INPROJ_TN, D_MODEL), BF16),
                        pltpu.SemaphoreType.DMA((2,)), pltpu.SemaphoreType.DMA((2,))],
        compiler_params=_params(("arbitrary", "arbitrary")),
        name="inproj",
    )(x2d, mod_rows, mod_rows, w_in_t)


def _inproj_context(ctx2d, mod_rows, w_blocks):
    return pl.pallas_call(
        _inproj_context_kernel,
        grid=(N_CTX // INPROJ_TM, INPROJ_BLOCKS),
        in_specs=[
            pl.BlockSpec((INPROJ_TM, D_MODEL), lambda i, k: (i, 0)),
            pl.BlockSpec((1, 1, D_MODEL), _inproj_mod_map(MOD_SHIFT1, N_CTX // INPROJ_TM, BATCH)),
            pl.BlockSpec((1, 1, D_MODEL), _inproj_mod_map(MOD_SCALE1, N_CTX // INPROJ_TM, BATCH)),
            pl.BlockSpec((1, INPROJ_TN, D_MODEL), lambda i, k: (k, 0, 0)),
        ],
        out_specs=pl.BlockSpec((INPROJ_TM, INPROJ_TN), lambda i, k: (i, k)),
        out_shape=jax.ShapeDtypeStruct((N_CTX, IN_W), BF16),
        scratch_shapes=[pltpu.VMEM((INPROJ_TM, D_MODEL), BF16)],
        compiler_params=_params(("arbitrary", "arbitrary")),
        name="inproj_ctx",
    )(ctx2d, mod_rows, mod_rows, w_blocks)


def _rope_tables(width):
    half = width // 2
    quarter = half // 2
    inv_freq = ROPE_BASE ** (-np.arange(0, half, 2, dtype=np.float64) / half)
    t = np.arange(SEQ)
    cos_parts, sin_parts = [], []
    for pos in (t // GRID_W, t % GRID_W):
        ang = pos[:, None].astype(np.float64) * inv_freq[None, :]
        c, s = np.cos(ang), np.sin(ang)
        cos_parts += [c, c]
        sin_parts += [-s, s]
    assert cos_parts[0].shape[1] == quarter
    return (np.concatenate(cos_parts, 1).astype(np.float32), np.concatenate(sin_parts, 1).astype(np.float32))


def _rope(x, cos, sin, quarter):
    lane = lax.broadcasted_iota(I32, x.shape, 1)
    first = (lane % (2 * quarter)) < quarter
    swapped = jnp.where(first, pltpu.roll(x, LANES - quarter, 1), pltpu.roll(x, quarter, 1))
    return x * cos + swapped * sin


def _dot_tn(a, b):
    return lax.dot_general(a, b, (((0,), (0,)), ((), ())), preferred_element_type=F32)


def _dot_nt(a, b):
    return lax.dot_general(a, b, (((1,), (1,)), ((), ())), preferred_element_type=F32)


RET_HG = 4


def _retention_kernel(q_ref, k_ref, v_ref, g_ref, kc_ref, vc_ref, cos_ref, sin_ref, df_ref, db_ref, gn_ref,
                      o_ref, qs_ref, ks_ref, st_ref):
    for hh in range(RET_HG):
        _retention_head(hh, slice(hh * LANES, (hh + 1) * LANES), q_ref, k_ref, v_ref, g_ref, kc_ref, vc_ref,
                        cos_ref, sin_ref, df_ref, db_ref, gn_ref, o_ref, qs_ref, ks_ref, st_ref)


def _retention_head(hh, hs, q_ref, k_ref, v_ref, g_ref, kc_ref, vc_ref, cos_ref, sin_ref, df_ref, db_ref, gn_ref,
                    o_ref, qs_ref, ks_ref, st_ref):
    k_scale = RET_DK ** -0.5
    lgf = jax.nn.log_sigmoid(df_ref[hh])
    lgb = jax.nn.log_sigmoid(db_ref[hh])

    cos = cos_ref[...]
    sin = sin_ref[...]
    qs_ref[hh] = _rope(q_ref[:, hs].astype(F32), cos, sin, RET_DK // 4).astype(BF16)
    ks_ref[hh] = _rope(k_ref[:, hs].astype(F32), cos, sin, RET_DK // 4) * k_scale

    rowi = lax.broadcasted_iota(I32, (CHUNK, LANES), 0).astype(F32)
    coli = lax.broadcasted_iota(I32, (CHUNK, LANES), 1).astype(F32)
    diff = rowi - coli
    decay = jnp.exp(jnp.where(diff >= 0, lgf * diff, -lgb * diff)) * jnp.where(diff == 0, 2.0, 1.0)
    zeta_f = jnp.exp(lgf * (CHUNK - 1.0 - rowi))
    eta_b = jnp.exp(lgb * rowi)
    xi_f = jnp.exp(lgf * (rowi + 1.0))
    xi_b = jnp.exp(lgb * (CHUNK - rowi))
    cdec_f = jnp.exp(lgf * float(CHUNK))
    cdec_b = jnp.exp(lgb * float(CHUNK))

    crow = lax.broadcasted_iota(I32, (CTX_LEN, LANES), 0).astype(F32)
    kc = kc_ref[:, hs].astype(F32) * k_scale
    vc = vc_ref[:, hs]
    s_f = _dot_tn((kc * jnp.exp(lgf * (CTX_LEN - 1.0 - crow))).astype(BF16), vc)
    s_b = _dot_tn((kc * jnp.exp(lgb * crow)).astype(BF16), vc)

    upd_f, upd_b = [], []
    for i in range(N_CHUNKS):
        rows = pl.ds(i * CHUNK, CHUNK)
        kch = ks_ref[hh, rows, :]
        vch = v_ref[rows, hs]
        upd_f.append(_dot_tn((kch * zeta_f).astype(BF16), vch))
        upd_b.append(_dot_tn((kch * eta_b).astype(BF16), vch))
    state = s_f
    for i in range(N_CHUNKS):
        st_ref[hh, i, :, :RET_DV] = state.astype(BF16)
        state = cdec_f * state + upd_f[i]
    state = s_b
    for i in reversed(range(N_CHUNKS)):
        st_ref[hh, i, :, RET_DV:] = state.astype(BF16)
        state = cdec_b * state + upd_b[i]

    gn_w = gn_ref[:, hs]
    for i in range(N_CHUNKS):
        rows = pl.ds(i * CHUNK, CHUNK)
        qch = qs_ref[hh, rows, :]
        scores = _dot_nt(qch, ks_ref[hh, rows, :].astype(BF16)) * decay
        o = jnp.dot(scores.astype(BF16), v_ref[rows, hs], preferred_element_type=F32)
        cross = jnp.dot(qch, st_ref[hh, i], preferred_element_type=F32)
        o = o + xi_f * cross[:, :RET_DV] + xi_b * cross[:, RET_DV:]
        y = _plain_norm(o) * gn_w
        o_ref[rows, hs] = (_silu(g_ref[rows, hs].astype(F32)) * y).astype(BF16)


def _retention(proj, proj_c, cos, sin, decay_rows, gn_w):
    width = RET_HG * LANES
    groups = RET_HEADS // RET_HG
    blk = lambda off: pl.BlockSpec((SEQ, width), lambda b, hg: (b, off // width + hg))
    blk_c = lambda off: pl.BlockSpec((CTX_LEN, width), lambda b, hg: (b, off // width + hg))
    table = pl.BlockSpec((SEQ, LANES), lambda b, hg: (0, 0))
    return pl.pallas_call(
        _retention_kernel,
        grid=(BATCH, groups),
        in_specs=[
            blk(OFF_Q), blk(OFF_K), blk(OFF_V), blk(OFF_G), blk_c(OFF_K), blk_c(OFF_V), table, table,
            pl.BlockSpec((RET_HG, 1, LANES), lambda b, hg: (hg, 0, 0)),
            pl.BlockSpec((RET_HG, 1, LANES), lambda b, hg: (groups + hg, 0, 0)),
            pl.BlockSpec((1, width), lambda b, hg: (0, hg)),
        ],
        out_specs=pl.BlockSpec((SEQ, width), lambda b, hg: (b, hg)),
        out_shape=jax.ShapeDtypeStruct((N_TOK, RET_W), BF16),
        scratch_shapes=[
            pltpu.VMEM((RET_HG, SEQ, RET_DK), BF16),
            pltpu.VMEM((RET_HG, SEQ, RET_DK), F32),
            pltpu.VMEM((RET_HG, N_CHUNKS, RET_DK, 2 * RET_DV), BF16),
        ],
        compiler_params=_params(("arbitrary", "arbitrary")),
        name="retention",
    )(proj, proj, proj, proj, proj_c, proj_c, cos, sin, decay_rows, decay_rows, gn_w)


MLA_TM = 512


def _rms_norm(x, w):
    return x * lax.rsqrt(jnp.mean(x * x, -1, keepdims=True) + EPS) * w


def _mla_kv(ckv_ref, kpe_ref, kvn_ref, wkv_ref, cos_ref, sin_ref, k_ref, v_ref, rotate):
    ckv = _rms_norm(ckv_ref[...].astype(F32), kvn_ref[...]).astype(BF16)
    kv = jnp.dot(ckv, wkv_ref[...], preferred_element_type=F32)
    lane = lax.broadcasted_iota(I32, (ckv.shape[0], LANES), 1)
    kpe = jnp.where(lane < MLA_ROPE, kpe_ref[...].astype(F32), 0.0)
    if rotate:
        kpe = _rope(kpe, cos_ref[...], sin_ref[...], MLA_ROPE // 4)
    kpe = kpe.astype(BF16)
    for h in range(MLA_HEADS):
        k_ref[:, h * QK_PAD:h * QK_PAD + MLA_NOPE] = kv[:, 2 * h * LANES:(2 * h + 1) * LANES].astype(BF16)
        k_ref[:, h * QK_PAD + MLA_NOPE:(h + 1) * QK_PAD] = kpe
        v_ref[:, h * MLA_DV:(h + 1) * MLA_DV] = kv[:, (2 * h + 1) * LANES:(2 * h + 2) * LANES].astype(BF16)


def _mla_latent_kernel(cq_ref, ckv_ref, kpe_ref, qn_ref, kvn_ref, wq_ref, wkv_ref, cos_ref, sin_ref,
                       q_ref, k_ref, v_ref):
    cq = _rms_norm(cq_ref[...].astype(F32), qn_ref[...]).astype(BF16)
    q = jnp.dot(cq, wq_ref[...], preferred_element_type=F32)
    cos = cos_ref[...]
    sin = sin_ref[...]
    scale = MLA_DQ ** -0.5 * LOG2_E
    for h in range(MLA_HEADS):
        lo = h * QK_PAD
        q_ref[:, lo:lo + MLA_NOPE] = (q[:, lo:lo + MLA_NOPE] * scale).astype(BF16)
        qpe = _rope(q[:, lo + MLA_NOPE:lo + QK_PAD], cos, sin, MLA_ROPE // 4)
        q_ref[:, lo + MLA_NOPE:lo + QK_PAD] = (qpe * scale).astype(BF16)
    _mla_kv(ckv_ref, kpe_ref, kvn_ref, wkv_ref, cos_ref, sin_ref, k_ref, v_ref, rotate=True)


def _mla_context_kernel(ckv_ref, kpe_ref, kvn_ref, wkv_ref, k_ref, v_ref):
    _mla_kv(ckv_ref, kpe_ref, kvn_ref, wkv_ref, None, None, k_ref, v_ref, rotate=False)


def _mla_latent(proj, q_norm, kv_norm, wq_pad, wkv, cos, sin):
    row = lambda w, off: pl.BlockSpec((MLA_TM, w), lambda i: (i, off // w))
    full = lambda a: pl.BlockSpec(a.shape, lambda i: (0, 0))
    table = pl.BlockSpec((MLA_TM, LANES), lambda i: (i % (SEQ // MLA_TM), 0))
    return pl.pallas_call(
        _mla_latent_kernel,
        grid=(N_TOK // MLA_TM,),
        in_specs=[row(MLA_Q_LORA, OFF_CQ), row(MLA_KV_LORA, OFF_CKV), row(LANES, OFF_KPE),
                  full(q_norm), full(kv_norm), full(wq_pad), full(wkv), table, table],
        out_specs=[
            pl.BlockSpec((MLA_TM, MLA_HEADS * QK_PAD), lambda i: (i, 0)),
            pl.BlockSpec((MLA_TM, MLA_HEADS * QK_PAD), lambda i: (i, 0)),
            pl.BlockSpec((MLA_TM, MLA_W), lambda i: (i, 0)),
        ],
        out_shape=[
            jax.ShapeDtypeStruct((N_TOK, MLA_HEADS * QK_PAD), BF16),
            jax.ShapeDtypeStruct((N_TOK, MLA_HEADS * QK_PAD), BF16),
            jax.ShapeDtypeStruct((N_TOK, MLA_W), BF16),
        ],
        compiler_params=_params(("arbitrary",)),
        name="mla_latent",
    )(proj, proj, proj, q_norm, kv_norm, wq_pad, wkv, cos, sin)


def _mla_context(proj_c, kv_norm, wkv):
    row = lambda w, off: pl.BlockSpec((MLA_TM, w), lambda i: (i, off // w))
    full = lambda a: pl.BlockSpec(a.shape, lambda i: (0, 0))
    return pl.pallas_call(
        _mla_context_kernel,
        grid=(N_CTX // MLA_TM,),
        in_specs=[row(MLA_KV_LORA, OFF_CKV), row(LANES, OFF_KPE), full(kv_norm), full(wkv)],
        out_specs=[
            pl.BlockSpec((MLA_TM, MLA_HEADS * QK_PAD), lambda i: (i, 0)),
            pl.BlockSpec((MLA_TM, MLA_W), lambda i: (i, 0)),
        ],
        out_shape=[
            jax.ShapeDtypeStruct((N_CTX, MLA_HEADS * QK_PAD), BF16),
            jax.ShapeDtypeStruct((N_CTX, MLA_W), BF16),
        ],
        compiler_params=_params(("arbitrary",)),
        name="mla_context",
    )(proj_c, proj_c, kv_norm, wkv)


ATT_TQ = 512


def _attention_kernel(q_ref, kc_ref, kl_ref, vc_ref, vl_ref, wo_ref, o_ref, wo_bf_ref, vext_ref):
    wo_bf_ref[...] = wo_ref[...].astype(BF16)

    @pl.when(pl.program_id(1) == 0)
    def _():
        lane = lax.broadcasted_iota(I32, (CTX_LEN + SEQ, MLA_DV), 1)
        ones_col = jnp.where(lane == 0, 1.0, 0.0).astype(BF16)
        for h in range(MLA_HEADS):
            dv = slice(h * MLA_DV, (h + 1) * MLA_DV)
            vext_ref[h, :CTX_LEN, :MLA_DV] = vc_ref[:, dv]
            vext_ref[h, CTX_LEN:, :MLA_DV] = vl_ref[:, dv]
            vext_ref[h, :, MLA_DV:] = ones_col

    for h in range(MLA_HEADS):
        qk = slice(h * QK_PAD, (h + 1) * QK_PAD)
        q = q_ref[:, qk]
        s_c = _dot_nt(q, kc_ref[:, qk])
        s_l = _dot_nt(q, kl_ref[:, qk])
        m = jnp.maximum(jnp.max(s_c, -1, keepdims=True), jnp.max(s_l, -1, keepdims=True))
        p_c = jnp.exp2(s_c - m).astype(BF16)
        p_l = jnp.exp2(s_l - m).astype(BF16)
        o = (jnp.dot(p_c, vext_ref[h, :CTX_LEN, :], preferred_element_type=F32)
             + jnp.dot(p_l, vext_ref[h, CTX_LEN:, :], preferred_element_type=F32))
        o_ref[:, h * MLA_DV:(h + 1) * MLA_DV] = (o[:, :MLA_DV] / o[:, MLA_DV:MLA_DV + 1]).astype(BF16)


def _attention(q, k_ctx, k_lat, v_ctx, v_lat, w_o):
    tiles = SEQ // ATT_TQ
    band = w_o.shape[0] // (BATCH * tiles)
    step = lambda b, i: (b * tiles + i, 0)
    return pl.pallas_call(
        _attention_kernel,
        grid=(BATCH, tiles),
        in_specs=[
            pl.BlockSpec((ATT_TQ, MLA_HEADS * QK_PAD), step),
            pl.BlockSpec((CTX_LEN, MLA_HEADS * QK_PAD), lambda b, i: (b, 0)),
            pl.BlockSpec((SEQ, MLA_HEADS * QK_PAD), lambda b, i: (b, 0)),
            pl.BlockSpec((CTX_LEN, MLA_W), lambda b, i: (b, 0)),
            pl.BlockSpec((SEQ, MLA_W), lambda b, i: (b, 0)),
            pl.BlockSpec((band, D_MODEL), step),
        ],
        out_specs=[pl.BlockSpec((ATT_TQ, MLA_W), step), pl.BlockSpec((band, D_MODEL), step)],
        out_shape=[jax.ShapeDtypeStruct((N_TOK, MLA_W), BF16), jax.ShapeDtypeStruct(w_o.shape, BF16)],
        scratch_shapes=[pltpu.VMEM((MLA_HEADS, CTX_LEN + SEQ, 2 * MLA_DV), BF16)],
        compiler_params=_params(("arbitrary", "arbitrary")),
        name="attention",
    )(q, k_ctx, k_lat, v_ctx, v_lat, w_o)


OUT_TM = 512
OUT_SUB = 256
ROUTE_E1, ROUTE_E2, ROUTE_R1, ROUTE_R2, ROUTE_W1, ROUTE_W2 = range(6)


def _outproj_kernel(ret_ref, att_ref, x_ref, wo_ref, gate_ref, shift_ref, scale_ref, lnw_ref, lnb_ref,
                    wr_ref, br_ref, h_ref, t_ref, route_ref, route_t_ref, count_ref, carry_ref, wsplit_ref):
    @pl.when(pl.program_id(0) == 0)
    def _():
        carry_ref[...] = jnp.zeros_like(carry_ref)
        w = wr_ref[...]
        w_hi = w.astype(BF16)
        wsplit_ref[:, :LANES] = w_hi
        wsplit_ref[:, LANES:] = (w - w_hi.astype(F32)).astype(BF16)

    subtiles = [slice(s * OUT_SUB, (s + 1) * OUT_SUB) for s in range(OUT_TM // OUT_SUB)]
    mixes = [jnp.dot(ret_ref[rows, :], wo_ref[:RET_W, :], preferred_element_type=F32)
             + jnp.dot(att_ref[rows, :], wo_ref[RET_W:, :], preferred_element_type=F32) for rows in subtiles]
    logits = [_outproj_norms(rows, mix, x_ref, gate_ref, shift_ref, scale_ref, lnw_ref, lnb_ref, br_ref, h_ref, t_ref,
                             wsplit_ref) for rows, mix in zip(subtiles, mixes)]
    _route(jnp.concatenate(logits, 0), carry_ref, count_ref, route_ref, route_t_ref)


def _outproj_norms(rows, mix, x_ref, gate_ref, shift_ref, scale_ref, lnw_ref, lnb_ref, br_ref, h_ref, t_ref,
                   wsplit_ref):
    h = _plain_norm(DEEPNORM_ALPHA * x_ref[rows, :] + gate_ref[0] * mix) * lnw_ref[...] + lnb_ref[...]
    h_ref[rows, :] = h
    t = _plain_norm(h) * (1.0 + scale_ref[0]) + shift_ref[0]
    t_ref[rows] = _to_token_tiles(t)

    t_hi = t.astype(BF16)
    t_lo = (t - t_hi.astype(F32)).astype(BF16)
    main = jnp.dot(t_hi, wsplit_ref[...], preferred_element_type=F32)
    corr = jnp.dot(t_lo, wsplit_ref[:, :LANES], preferred_element_type=F32)
    return main[:, :LANES] + (main[:, LANES:] + corr) + br_ref[...]


def _route(logits, carry_ref, count_ref, route_ref, route_t_ref):
    lane = lax.broadcasted_iota(I32, logits.shape, 1).astype(F32)
    neg = -jnp.inf
    big = float(LANES)

    def first_lane_of(mask):
        return jnp.min(jnp.where(mask, lane, big), -1, keepdims=True)

    is_group = lane < N_GROUPS
    gl = jnp.where(is_group, logits, neg)
    g_max = jnp.max(gl, -1, keepdims=True)
    g_idx = first_lane_of(is_group & (gl == g_max))
    g_prob = 1.0 / jnp.sum(jnp.where(is_group, jnp.exp(logits - g_max), 0.0), -1, keepdims=True)

    lo = N_GROUPS + g_idx * EXPERTS_PER_GROUP
    in_group = (lane >= lo) & (lane < lo + EXPERTS_PER_GROUP)
    el = jnp.where(in_group, logits, neg)
    v1 = jnp.max(el, -1, keepdims=True)
    i1 = first_lane_of(in_group & (el == v1))
    rest = in_group & (lane != i1)
    el2 = jnp.where(rest, logits, neg)
    v2 = jnp.max(el2, -1, keepdims=True)
    i2 = first_lane_of(rest & (el2 == v2))
    d = jnp.exp(v2 - v1)
    w1 = g_prob / (1.0 + d)
    w2 = g_prob * d / (1.0 + d)

    onehot = jnp.where((lane == i1) | (lane == i2), 1.0, 0.0)
    r = lax.broadcasted_iota(I32, (OUT_TM, OUT_TM), 0)
    c = lax.broadcasted_iota(I32, (OUT_TM, OUT_TM), 1)
    tri = jnp.where(c < r, 1.0, 0.0).astype(BF16)
    carry = carry_ref[0:1, :]
    before = jnp.dot(tri, onehot.astype(BF16), preferred_element_type=F32) + carry
    r1 = jnp.sum(jnp.where(lane == i1, before, 0.0), -1, keepdims=True)
    r2 = jnp.sum(jnp.where(lane == i2, before, 0.0), -1, keepdims=True)
    carry = carry + jnp.sum(onehot, 0, keepdims=True)
    carry_ref[...] = jnp.broadcast_to(carry, carry_ref.shape)
    count_ref[...] = jnp.broadcast_to(carry, count_ref.shape)

    rec = jnp.zeros_like(logits)
    for slot, val in ((ROUTE_E1, i1 - N_GROUPS), (ROUTE_E2, i2 - N_GROUPS), (ROUTE_R1, r1), (ROUTE_R2, r2),
                      (ROUTE_W1, w1), (ROUTE_W2, w2)):
        rec = jnp.where(lane == slot, val, rec)
    route_ref[...] = rec
    route_t_ref[...] = rec.T[:SUBLANES, :]


def _outproj(ret, att, x2d, wo_bf, mod_rows, ln_w, ln_b, w_route, b_route):
    tiles_per_sample = SEQ // OUT_TM
    row = lambda w: pl.BlockSpec((OUT_TM, w), lambda i: (i, 0))
    full = lambda a: pl.BlockSpec(a.shape, lambda i: (0, 0))
    mod = lambda j: pl.BlockSpec((1, 1, D_MODEL), lambda i: ((i // tiles_per_sample) * N_MOD + j, 0, 0))
    return pl.pallas_call(
        _outproj_kernel,
        grid=(N_TOK // OUT_TM,),
        in_specs=[row(RET_W), row(MLA_W), row(D_MODEL), full(wo_bf), mod(MOD_GATE1), mod(MOD_SHIFT2), mod(MOD_SCALE2),
                  full(ln_w), full(ln_b), full(w_route), full(b_route)],
        out_specs=[row(D_MODEL), pl.BlockSpec((OUT_TM, TOKEN_SUB, LANES), lambda i: (i, 0, 0)), row(LANES),
                   pl.BlockSpec((SUBLANES, OUT_TM), lambda i: (0, i)),
                   pl.BlockSpec((SUBLANES, LANES), lambda i: (0, 0))],
        out_shape=[
            jax.ShapeDtypeStruct((N_TOK, D_MODEL), F32),
            jax.ShapeDtypeStruct((N_TOK, TOKEN_SUB, LANES), BF16),
            jax.ShapeDtypeStruct((N_TOK, LANES), F32),
            jax.ShapeDtypeStruct((SUBLANES, N_TOK), F32),
            jax.ShapeDtypeStruct((SUBLANES, LANES), F32),
        ],
        scratch_shapes=[pltpu.VMEM((SUBLANES, LANES), F32), pltpu.VMEM((D_MODEL, 2 * LANES), BF16)],
        compiler_params=_params(("arbitrary",)),
        name="outproj_route",
    )(ret, att, x2d, wo_bf, mod_rows, mod_rows, mod_rows, ln_w, ln_b, w_route, b_route)


DISPATCH_TM = 1024


def _dispatch_kernel(pos_ref, t_ref, xs_ref, sem):
    i = pl.program_id(0)

    def copy(slot, j):
        return pltpu.make_async_copy(t_ref.at[j], xs_ref.at[pos_ref[slot * N_TOK + i * DISPATCH_TM + j]], sem)

    for j in range(DISPATCH_TM):
        for slot in range(2):
            copy(slot, j).start(priority=slot)
    for slot in range(2):
        pltpu.make_async_copy(t_ref, xs_ref.at[pl.ds(0, DISPATCH_TM)], sem).wait()


def _dispatch(pos, t_tiles):
    return pl.pallas_call(
        _dispatch_kernel,
        grid_spec=pltpu.PrefetchScalarGridSpec(
            num_scalar_prefetch=1,
            grid=(N_TOK // DISPATCH_TM,),
            in_specs=[pl.BlockSpec((DISPATCH_TM, TOKEN_SUB, LANES), lambda i, pos: (i, 0, 0))],
            out_specs=pl.BlockSpec(memory_space=pl.ANY),
            scratch_shapes=[pltpu.SemaphoreType.DMA(())],
        ),
        out_shape=jax.ShapeDtypeStruct((N_PAIRS, TOKEN_SUB, LANES), BF16),
        compiler_params=_params(("arbitrary",)),
        name="dispatch",
    )(pos, t_tiles)


def _experts_kernel(rend_ref, xs_ref, wg_ref, wu_ref, wd_ref, ys_ref, wgu_ref, wdn_ref, xbuf_ref, ybuf_ref,
                    xsem, ysem):
    e = pl.program_id(0)
    lo = jnp.where(e == 0, 0, rend_ref[jnp.maximum(e - 1, 0)])
    hi = rend_ref[e]

    def x_copy(g):
        return pltpu.make_async_copy(xs_ref.at[pl.ds(g * ROW_TILE, ROW_TILE)], xbuf_ref.at[g % 2], xsem.at[g % 2])

    def y_copy(g):
        return pltpu.make_async_copy(ybuf_ref.at[g % 2], ys_ref.at[pl.ds(g * ROW_TILE, ROW_TILE)], ysem.at[g % 2])

    @pl.when(e == 0)
    def _():
        x_copy(0).start()

    @pl.when(hi > lo)
    def _():
        wgu_ref[:, :D_EXPERT] = wg_ref[0].astype(BF16)
        wgu_ref[:, D_EXPERT:] = wu_ref[0].astype(BF16)
        wdn_ref[...] = wd_ref[0].astype(BF16)

        def tile(g, c):
            row0 = g * ROW_TILE
            owns_first_row = lo <= row0
            owns_last_row = hi >= row0 + ROW_TILE

            @pl.when(owns_first_row)
            def _():
                @pl.when(g + 1 < N_TILES)
                def _():
                    x_copy(g + 1).start()

                x_copy(g).wait()

            gu = jnp.dot(_from_token_tiles(xbuf_ref[g % 2]), wgu_ref[...], preferred_element_type=F32)
            hid = _silu(gu[:, :D_EXPERT]) * gu[:, D_EXPERT:]
            y = jnp.dot(hid.astype(BF16), wdn_ref[...], preferred_element_type=F32)

            @pl.when(owns_first_row)
            def _():
                @pl.when(g >= 2)
                def _():
                    y_copy(g - 2).wait()

                ybuf_ref[g % 2] = _to_token_tiles(y)

            @pl.when(jnp.logical_not(owns_first_row))
            def _():
                row = lax.broadcasted_iota(I32, y.shape, 0)
                earlier = _from_token_tiles(ybuf_ref[g % 2]).astype(F32)
                ybuf_ref[g % 2] = _to_token_tiles(jnp.where(row >= lo - row0, y, earlier))

            @pl.when(owns_last_row)
            def _():
                y_copy(g).start()

            return c

        lax.fori_loop(lo // ROW_TILE, (hi - 1) // ROW_TILE + 1, tile, 0)

    @pl.when(e == N_EXPERTS - 1)
    def _():
        y_copy(N_TILES - 2).wait()
        y_copy(N_TILES - 1).wait()


def _experts(row_end, xs, w_gate, w_up, w_down):
    w_gate = w_gate.reshape(N_EXPERTS, D_MODEL, D_EXPERT)
    w_up = w_up.reshape(N_EXPERTS, D_MODEL, D_EXPERT)
    w_down = w_down.reshape(N_EXPERTS, D_EXPERT, D_MODEL)
    expert = lambda e, rend: (e, 0, 0)
    tile_buf = pltpu.VMEM((2, ROW_TILE, TOKEN_SUB, LANES), BF16)
    return pl.pallas_call(
        _experts_kernel,
        grid_spec=pltpu.PrefetchScalarGridSpec(
            num_scalar_prefetch=1,
            grid=(N_EXPERTS,),
            in_specs=[
                pl.BlockSpec(memory_space=pl.ANY),
                pl.BlockSpec((1, D_MODEL, D_EXPERT), expert),
                pl.BlockSpec((1, D_MODEL, D_EXPERT), expert),
                pl.BlockSpec((1, D_EXPERT, D_MODEL), expert),
            ],
            out_specs=pl.BlockSpec(memory_space=pl.ANY),
            scratch_shapes=[pltpu.VMEM((D_MODEL, 2 * D_EXPERT), BF16), pltpu.VMEM((D_EXPERT, D_MODEL), BF16),
                            tile_buf, tile_buf, pltpu.SemaphoreType.DMA((2,)), pltpu.SemaphoreType.DMA((2,))],
        ),
        out_shape=jax.ShapeDtypeStruct((N_PAIRS, TOKEN_SUB, LANES), BF16),
        compiler_params=_params(("arbitrary",)),
        name="experts",
    )(row_end, xs, w_gate, w_up, w_down)


COMB_TM = 256


def _combine_kernel(pos_ref, ys_ref, h_ref, route_ref, gate_ref, lnw_ref, lnb_ref, o_ref, buf_ref, sem):
    i = pl.program_id(0)
    n = pl.num_programs(0)

    def copy(step, slot, pair, j):
        src = pos_ref[pair * N_TOK + step * COMB_TM + j]
        return pltpu.make_async_copy(ys_ref.at[src], buf_ref.at[slot, pair * COMB_TM + j], sem.at[slot])

    def start_all(step, slot):
        for pair in range(2):
            lax.fori_loop(0, COMB_TM, lambda j, c, pair=pair: (copy(step, slot, pair, j).start(priority=pair), c)[1], 0,
                          unroll=8)

    def wait_all(step, slot):
        pltpu.make_async_copy(ys_ref.at[pl.ds(0, 2 * COMB_TM)], buf_ref.at[slot], sem.at[slot]).wait()

    @pl.when(i == 0)
    def _():
        start_all(0, 0)

    slot = i % 2
    wait_all(i, slot)

    nxt = jnp.minimum(i + 1, n - 1)
    for pair in range(2):
        for j in range(COMB_TM):
            copy(nxt, 1 - slot, pair, j).start(priority=pair)

    y1 = _from_token_tiles(buf_ref[slot, :COMB_TM]).astype(F32)
    y2 = _from_token_tiles(buf_ref[slot, COMB_TM:]).astype(F32)
    y = route_ref[:, ROUTE_W1:ROUTE_W1 + 1] * y1 + route_ref[:, ROUTE_W2:ROUTE_W2 + 1] * y2
    z = DEEPNORM_ALPHA * h_ref[...] + gate_ref[0] * y
    o_ref[...] = _plain_norm(z) * lnw_ref[...] + lnb_ref[...]

    @pl.when(i == n - 1)
    def _():
        wait_all(i, 1 - slot)


def _combine(pos, ys, h, route, mod_rows, ln_w, ln_b):
    tiles_per_sample = SEQ // COMB_TM
    row = lambda w: pl.BlockSpec((COMB_TM, w), lambda i, pos: (i, 0))
    full = lambda a: pl.BlockSpec(a.shape, lambda i, pos: (0, 0))
    return pl.pallas_call(
        _combine_kernel,
        grid_spec=pltpu.PrefetchScalarGridSpec(
            num_scalar_prefetch=1,
            grid=(N_TOK // COMB_TM,),
            in_specs=[
                pl.BlockSpec(memory_space=pl.ANY), row(D_MODEL), row(LANES),
                pl.BlockSpec((1, 1, D_MODEL), lambda i, pos: ((i // tiles_per_sample) * N_MOD + MOD_GATE2, 0, 0)),
                full(ln_w), full(ln_b),
            ],
            out_specs=row(D_MODEL),
            scratch_shapes=[pltpu.VMEM((2, 2 * COMB_TM, TOKEN_SUB, LANES), BF16), pltpu.SemaphoreType.DMA((2,))],
        ),
        out_shape=jax.ShapeDtypeStruct((N_TOK, D_MODEL), F32),
        compiler_params=_params(("arbitrary",)),
        name="combine",
    )(pos, ys, h, route, mod_rows, ln_w, ln_b)


def _routing_tables(route_t, counts):
    cnt = counts[0, N_GROUPS:N_ROUTE].astype(I32)
    row_end = jnp.cumsum(cnt)
    e = route_t[ROUTE_E1:ROUTE_E2 + 1].astype(I32)
    rank = route_t[ROUTE_R1:ROUTE_R2 + 1].astype(I32)
    earlier = jnp.arange(N_EXPERTS, dtype=I32)[:, None, None] < e[None]
    base = jnp.sum(jnp.where(earlier, cnt[:, None, None], 0), 0)
    pos = (base + rank).reshape(-1)
    return pos, row_end


def kernel(x, c, ctx, c_ctx, w_ada, b_ada, w_in, ret_decay, ret_gn_w, mla_q_norm, mla_kv_norm, w_uq, w_ukv, w_o,
           ln1_w, ln1_b, router_group_w, router_group_b, router_expert_w, router_expert_b, expert_w_gate,
           expert_w_up, expert_w_down, ln2_w, ln2_b):
    x2d = x.reshape(N_TOK, D_MODEL)
    ctx2d = ctx.reshape(N_CTX, D_MODEL)

    cc = jnp.zeros((SUBLANES, D_MODEL), F32).at[:BATCH].set(c).at[BATCH].set(c_ctx)
    mod = _ada(cc, w_ada[0], b_ada)
    mod_rows = mod.reshape(SUBLANES * N_MOD, 1, D_MODEL)

    proj, w_in_blocks = _inproj_latent(x2d, mod_rows, jnp.swapaxes(w_in, 1, 2))
    proj_c = _inproj_context(ctx2d, mod_rows, w_in_blocks)

    cos_r, sin_r = _rope_tables(RET_DK)
    decay_rows = jnp.broadcast_to(ret_decay[0].reshape(2 * RET_HEADS, 1, 1), (2 * RET_HEADS, 1, LANES))
    ret = _retention(proj, proj_c, jnp.asarray(cos_r), jnp.asarray(sin_r), decay_rows, ret_gn_w)

    cos_m, sin_m = _rope_tables(MLA_ROPE)
    cos_m = np.concatenate([cos_m, np.ones_like(cos_m)], 1)
    sin_m = np.concatenate([sin_m, np.zeros_like(sin_m)], 1)
    wq = w_uq[0].reshape(MLA_Q_LORA, MLA_HEADS, MLA_DQ)
    wq_pad = jnp.pad(wq, ((0, 0), (0, 0), (0, QK_PAD - MLA_DQ))).reshape(MLA_Q_LORA, MLA_HEADS * QK_PAD).astype(BF16)
    wkv = w_ukv[0].astype(BF16)
    q, k_lat, v_lat = _mla_latent(proj, mla_q_norm, mla_kv_norm, wq_pad, wkv, jnp.asarray(cos_m), jnp.asarray(sin_m))
    k_ctx, v_ctx = _mla_context(proj_c, mla_kv_norm, wkv)
    att, w_o_bf = _attention(q, k_ctx, k_lat, v_ctx, v_lat, w_o[0])

    w_route = jnp.concatenate(
        [router_group_w[0], router_expert_w[0].transpose(1, 0, 2).reshape(D_MODEL, N_EXPERTS),
         jnp.zeros((D_MODEL, LANES - N_ROUTE), F32)], 1)
    b_route = jnp.concatenate(
        [router_group_b[0], router_expert_b[0].reshape(N_EXPERTS), jnp.zeros((LANES - N_ROUTE,), F32)])[None]
    h, t_tiles, route, route_t, counts = _outproj(ret, att, x2d, w_o_bf, mod_rows, ln1_w, ln1_b,
                                                   w_route, b_route)

    pos, row_end = _routing_tables(route_t, counts)
    xs = _dispatch(pos, t_tiles)
    ys = _experts(row_end, xs, expert_w_gate[0], expert_w_up[0], expert_w_down[0])
    out = _combine(pos, ys, h, route, mod_rows, ln2_w, ln2_b)
    return out.reshape(BATCH, SEQ, D_MODEL)
```

```python
import numpy as np
import jax
import jax.numpy as jnp
from jax import lax
from jax.experimental import pallas as pl
from jax.experimental.pallas import tpu as pltpu

F32 = jnp.float32
BF16 = jnp.bfloat16
I32 = jnp.int32

D_MODEL = 2048
BATCH = 4
SEQ = 2048
GRID_W = 64
CTX_LEN = 256
N_TOK = BATCH * SEQ
N_CTX = BATCH * CTX_LEN

RET_HEADS = 8
RET_DK = 128
RET_DV = 128
RET_W = RET_HEADS * RET_DV
CHUNK = 128
N_CHUNKS = SEQ // CHUNK

MLA_HEADS = 8
MLA_Q_LORA = 512
MLA_KV_LORA = 256
MLA_NOPE = 128
MLA_ROPE = 64
MLA_DV = 128
MLA_W = MLA_HEADS * MLA_DV
MLA_DQ = MLA_NOPE + MLA_ROPE
QK_PAD = 256

IN_SIZES = (RET_HEADS * RET_DK, RET_HEADS * RET_DK, RET_W, RET_W, MLA_Q_LORA, MLA_KV_LORA, MLA_ROPE)
IN_W = sum(IN_SIZES)
OFF_Q, OFF_K, OFF_V, OFF_G, OFF_CQ, OFF_CKV, OFF_KPE = (int(v) for v in np.cumsum((0,) + IN_SIZES[:-1]))

N_GROUPS = 4
EXPERTS_PER_GROUP = 8
N_EXPERTS = N_GROUPS * EXPERTS_PER_GROUP
D_EXPERT = 512
N_ROUTE = N_GROUPS + N_EXPERTS

N_MOD = 6
MOD_SHIFT1, MOD_SCALE1, MOD_GATE1, MOD_SHIFT2, MOD_SCALE2, MOD_GATE2 = range(N_MOD)

LOG2_E = float(np.log2(np.e))
ROPE_BASE = 10000.0
EPS = 1e-6
DEPTH = 1
DEEPNORM_ALPHA = (2.0 * DEPTH) ** 0.25

LANES = 128
SUBLANES = 8
ROW_TILE = 256
N_PAIRS = 2 * N_TOK
N_TILES = N_PAIRS // ROW_TILE
TOKEN_SUB = D_MODEL // LANES

V7X_VMEM_BYTES = 64 * 1024 * 1024
VMEM_LIMIT = V7X_VMEM_BYTES - V7X_VMEM_BYTES // 8


def _params(sem):
    return pltpu.CompilerParams(dimension_semantics=sem, vmem_limit_bytes=VMEM_LIMIT)


def _silu(x):
    return x * (1.0 / (1.0 + jnp.exp(-x)))


def _plain_norm(x):
    mu = jnp.mean(x, -1, keepdims=True)
    xc = x - mu
    var = jnp.mean(xc * xc, -1, keepdims=True)
    return xc * lax.rsqrt(var + EPS)


def _to_token_tiles(x):
    return x.astype(BF16).reshape(x.shape[0], TOKEN_SUB, LANES)


def _from_token_tiles(x):
    return x.reshape(x.shape[0], D_MODEL)


ADA_TN = 1024


def _ada_kernel(cc_ref, w_ref, b_ref, o_ref):
    s = _silu(cc_ref[...])
    o_ref[...] = jnp.dot(s, w_ref[...], preferred_element_type=F32,
                         precision=lax.Precision.HIGHEST) + b_ref[...]


def _ada(cc, w_ada, b_ada):
    n = w_ada.shape[1]
    return pl.pallas_call(
        _ada_kernel,
        grid=(n // ADA_TN,),
        in_specs=[
            pl.BlockSpec((SUBLANES, D_MODEL), lambda j: (0, 0)),
            pl.BlockSpec((D_MODEL, ADA_TN), lambda j: (0, j)),
            pl.BlockSpec((1, ADA_TN), lambda j: (0, j)),
        ],
        out_specs=pl.BlockSpec((SUBLANES, ADA_TN), lambda j: (0, j)),
        out_shape=jax.ShapeDtypeStruct((SUBLANES, n), F32),
        compiler_params=_params(("arbitrary",)),
        name="ada",
    )(cc, w_ada, b_ada)


INPROJ_TM = 1024
INPROJ_TN = 1024


INPROJ_SUB = 256
INPROJ_BLOCKS = -(-IN_W // INPROJ_TN)


def _inproj_products(x_ref, shift_ref, scale_ref, w, o_ref, xn_ref):
    first = pl.program_id(1) == 0

    @pl.when(first)
    def _():
        for s in range(INPROJ_TM // INPROJ_SUB):
            rows = slice(s * INPROJ_SUB, (s + 1) * INPROJ_SUB)
            y = (_plain_norm(x_ref[rows, :]) * (1.0 + scale_ref[0]) + shift_ref[0]).astype(BF16)
            xn_ref[rows, :] = y
            o_ref[rows, :] = _dot_nt(y, w[...]).astype(BF16)

    @pl.when(jnp.logical_not(first))
    def _():
        o_ref[...] = _dot_nt(xn_ref[...], w[...]).astype(BF16)


def _inproj_latent_kernel(x_ref, shift_ref, scale_ref, wf_ref, o_ref, wbf_ref, xn_ref, wbuf_ref, in_sem, out_sem):
    i = pl.program_id(0)
    k = pl.program_id(1)
    t = i * INPROJ_BLOCKS + k
    slot = t % 2
    n_steps = pl.num_programs(0) * INPROJ_BLOCKS

    def out_copy(block, s):
        return pltpu.make_async_copy(wbuf_ref.at[s], wbf_ref.at[block], out_sem.at[s])

    def in_copy(block, s):
        return pltpu.make_async_copy(wbf_ref.at[block], wbuf_ref.at[s], in_sem.at[s])

    @pl.when(i == 0)
    def _():
        @pl.when(k >= 2)
        def _():
            out_copy(k - 2, slot).wait()

        col = lax.broadcasted_iota(I32, (INPROJ_TN, D_MODEL), 0)
        wbuf_ref[slot] = jnp.where(col < IN_W - k * INPROJ_TN, wf_ref[0], 0.0).astype(BF16)
        out_copy(k, slot).start()

    @pl.when((t + 1 >= INPROJ_BLOCKS) & (t + 1 < n_steps))
    def _():
        @pl.when(t == INPROJ_BLOCKS - 1)
        def _():
            out_copy(INPROJ_BLOCKS - 2, 1 - slot).wait()

        @pl.when(t == INPROJ_BLOCKS)
        def _():
            out_copy(INPROJ_BLOCKS - 1, 1 - slot).wait()

        in_copy((k + 1) % INPROJ_BLOCKS, 1 - slot).start()

    @pl.when(i > 0)
    def _():
        in_copy(k, slot).wait()

    _inproj_products(x_ref, shift_ref, scale_ref, wbuf_ref.at[slot], o_ref, xn_ref)


def _inproj_context_kernel(x_ref, shift_ref, scale_ref, w_ref, o_ref, xn_ref):
    _inproj_products(x_ref, shift_ref, scale_ref, w_ref.at[0], o_ref, xn_ref)


def _inproj_mod_map(j, tiles_per_sample, sample_row0):
    return lambda i, k: ((sample_row0 + i // tiles_per_sample) * N_MOD + j, 0, 0)


def _inproj_latent(x2d, mod_rows, w_in_t):
    assert INPROJ_BLOCKS % 2 == 1
    last = INPROJ_BLOCKS - 1
    return pl.pallas_call(
        _inproj_latent_kernel,
        grid=(N_TOK // INPROJ_TM, INPROJ_BLOCKS),
        in_specs=[
            pl.BlockSpec((INPROJ_TM, D_MODEL), lambda i, k: (i, 0)),
            pl.BlockSpec((1, 1, D_MODEL), _inproj_mod_map(MOD_SHIFT1, SEQ // INPROJ_TM, 0)),
            pl.BlockSpec((1, 1, D_MODEL), _inproj_mod_map(MOD_SCALE1, SEQ // INPROJ_TM, 0)),
            pl.BlockSpec((1, INPROJ_TN, D_MODEL), lambda i, k: (0, jnp.where(i == 0, k, last), 0)),
        ],
        out_specs=[pl.BlockSpec((INPROJ_TM, INPROJ_TN), lambda i, k: (i, k)), pl.BlockSpec(memory_space=pl.ANY)],
        out_shape=[jax.ShapeDtypeStruct((N_TOK, IN_W), BF16),
                   jax.ShapeDtypeStruct((INPROJ_BLOCKS, INPROJ_TN, D_MODEL), BF16)],
        scratch_shapes=[pltpu.VMEM((INPROJ_TM, D_MODEL), BF16), pltpu.VMEM((2, INPROJ_TN, D_MODEL), BF16),
                        pltpu.SemaphoreType.DMA((2,)), pltpu.SemaphoreType.DMA((2,))],
        compiler_params=_params(("arbitrary", "arbitrary")),
        name="inproj",
    )(x2d, mod_rows, mod_rows, w_in_t)


def _inproj_context(ctx2d, mod_rows, w_blocks):
    return pl.pallas_call(
        _inproj_context_kernel,
        grid=(N_CTX // INPROJ_TM, INPROJ_BLOCKS),
        in_specs=[
            pl.BlockSpec((INPROJ_TM, D_MODEL), lambda i, k: (i, 0)),
            pl.BlockSpec((1, 1, D_MODEL), _inproj_mod_map(MOD_SHIFT1, N_CTX // INPROJ_TM, BATCH)),
            pl.BlockSpec((1, 1, D_MODEL), _inproj_mod_map(MOD_SCALE1, N_CTX // INPROJ_TM, BATCH)),
            pl.BlockSpec((1, INPROJ_TN, D_MODEL), lambda i, k: (k, 0, 0)),
        ],
        out_specs=pl.BlockSpec((INPROJ_TM, INPROJ_TN), lambda i, k: (i, k)),
        out_shape=jax.ShapeDtypeStruct((N_CTX, IN_W), BF16),
        scratch_shapes=[pltpu.VMEM((INPROJ_TM, D_MODEL), BF16)],
        compiler_params=_params(("arbitrary", "arbitrary")),
        name="inproj_ctx",
    )(ctx2d, mod_rows, mod_rows, w_blocks)


def _rope_tables(width):
    half = width // 2
    quarter = half // 2
    inv_freq = ROPE_BASE ** (-np.arange(0, half, 2, dtype=np.float64) / half)
    t = np.arange(SEQ)
    cos_parts, sin_parts = [], []
    for pos in (t // GRID_W, t % GRID_W):
        ang = pos[:, None].astype(np.float64) * inv_freq[None, :]
        c, s = np.cos(ang), np.sin(ang)
        cos_parts += [c, c]
        sin_parts += [-s, s]
    assert cos_parts[0].shape[1] == quarter
    return (np.concatenate(cos_parts, 1).astype(np.float32), np.concatenate(sin_parts, 1).astype(np.float32))


def _rope(x, cos, sin, quarter):
    lane = lax.broadcasted_iota(I32, x.shape, 1)
    first = (lane % (2 * quarter)) < quarter
    swapped = jnp.where(first, pltpu.roll(x, LANES - quarter, 1), pltpu.roll(x, quarter, 1))
    return x * cos + swapped * sin


def _dot_tn(a, b):
    return lax.dot_general(a, b, (((0,), (0,)), ((), ())), preferred_element_type=F32)


def _dot_nt(a, b):
    return lax.dot_general(a, b, (((1,), (1,)), ((), ())), preferred_element_type=F32)


RET_HG = 4


def _retention_kernel(q_ref, k_ref, v_ref, g_ref, kc_ref, vc_ref, cos_ref, sin_ref, df_ref, db_ref, gn_ref,
                      o_ref, qs_ref, ks_ref, st_ref):
    for hh in range(RET_HG):
        _retention_head(hh, slice(hh * LANES, (hh + 1) * LANES), q_ref, k_ref, v_ref, g_ref, kc_ref, vc_ref,
                        cos_ref, sin_ref, df_ref, db_ref, gn_ref, o_ref, qs_ref, ks_ref, st_ref)


def _retention_head(hh, hs, q_ref, k_ref, v_ref, g_ref, kc_ref, vc_ref, cos_ref, sin_ref, df_ref, db_ref, gn_ref,
                    o_ref, qs_ref, ks_ref, st_ref):
    k_scale = RET_DK ** -0.5
    lgf = jax.nn.log_sigmoid(df_ref[hh])
    lgb = jax.nn.log_sigmoid(db_ref[hh])

    cos = cos_ref[...]
    sin = sin_ref[...]
    qs_ref[hh] = _rope(q_ref[:, hs].astype(F32), cos, sin, RET_DK // 4).astype(BF16)
    ks_ref[hh] = _rope(k_ref[:, hs].astype(F32), cos, sin, RET_DK // 4) * k_scale

    rowi = lax.broadcasted_iota(I32, (CHUNK, LANES), 0).astype(F32)
    coli = lax.broadcasted_iota(I32, (CHUNK, LANES), 1).astype(F32)
    diff = rowi - coli
    decay = jnp.exp(jnp.where(diff >= 0, lgf * diff, -lgb * diff)) * jnp.where(diff == 0, 2.0, 1.0)
    zeta_f = jnp.exp(lgf * (CHUNK - 1.0 - rowi))
    eta_b = jnp.exp(lgb * rowi)
    xi_f = jnp.exp(lgf * (rowi + 1.0))
    xi_b = jnp.exp(lgb * (CHUNK - rowi))
    cdec_f = jnp.exp(lgf * float(CHUNK))
    cdec_b = jnp.exp(lgb * float(CHUNK))

    crow = lax.broadcasted_iota(I32, (CTX_LEN, LANES), 0).astype(F32)
    kc = kc_ref[:, hs].astype(F32) * k_scale
    vc = vc_ref[:, hs]
    s_f = _dot_tn((kc * jnp.exp(lgf * (CTX_LEN - 1.0 - crow))).astype(BF16), vc)
    s_b = _dot_tn((kc * jnp.exp(lgb * crow)).astype(BF16), vc)

    upd_f, upd_b = [], []
    for i in range(N_CHUNKS):
        rows = pl.ds(i * CHUNK, CHUNK)
        kch = ks_ref[hh, rows, :]
        vch = v_ref[rows, hs]
        upd_f.append(_dot_tn((kch * zeta_f).astype(BF16), vch))
        upd_b.append(_dot_tn((kch * eta_b).astype(BF16), vch))
    state = s_f
    for i in range(N_CHUNKS):
        st_ref[hh, i, :, :RET_DV] = state.astype(BF16)
        state = cdec_f * state + upd_f[i]
    state = s_b
    for i in reversed(range(N_CHUNKS)):
        st_ref[hh, i, :, RET_DV:] = state.astype(BF16)
        state = cdec_b * state + upd_b[i]

    gn_w = gn_ref[:, hs]
    for i in range(N_CHUNKS):
        rows = pl.ds(i * CHUNK, CHUNK)
        qch = qs_ref[hh, rows, :]
        scores = _dot_nt(qch, ks_ref[hh, rows, :].astype(BF16)) * decay
        o = jnp.dot(scores.astype(BF16), v_ref[rows, hs], preferred_element_type=F32)
        cross = jnp.dot(qch, st_ref[hh, i], preferred_element_type=F32)
        o = o + xi_f * cross[:, :RET_DV] + xi_b * cross[:, RET_DV:]
        y = _plain_norm(o) * gn_w
        o_ref[rows, hs] = (_silu(g_ref[rows, hs].astype(F32)) * y).astype(BF16)


def _retention(proj, proj_c, cos, sin, decay_rows, gn_w):
    width = RET_HG * LANES
    groups = RET_HEADS // RET_HG
    blk = lambda off: pl.BlockSpec((SEQ, width), lambda b, hg: (b, off // width + hg))
    blk_c = lambda off: pl.BlockSpec((CTX_LEN, width), lambda b, hg: (b, off // width + hg))
    table = pl.BlockSpec((SEQ, LANES), lambda b, hg: (0, 0))
    return pl.pallas_call(
        _retention_kernel,
        grid=(BATCH, groups),
        in_specs=[
            blk(OFF_Q), blk(OFF_K), blk(OFF_V), blk(OFF_G), blk_c(OFF_K), blk_c(OFF_V), table, table,
            pl.BlockSpec((RET_HG, 1, LANES), lambda b, hg: (hg, 0, 0)),
            pl.BlockSpec((RET_HG, 1, LANES), lambda b, hg: (groups + hg, 0, 0)),
            pl.BlockSpec((1, width), lambda b, hg: (0, hg)),
        ],
        out_specs=pl.BlockSpec((SEQ, width), lambda b, hg: (b, hg)),
        out_shape=jax.ShapeDtypeStruct((N_TOK, RET_W), BF16),
        scratch_shapes=[
            pltpu.VMEM((RET_HG, SEQ, RET_DK), BF16),
            pltpu.VMEM((RET_HG, SEQ, RET_DK), F32),
            pltpu.VMEM((RET_HG, N_CHUNKS, RET_DK, 2 * RET_DV), BF16),
        ],
        compiler_params=_params(("arbitrary", "arbitrary")),
        name="retention",
    )(proj, proj, proj, proj, proj_c, proj_c, cos, sin, decay_rows, decay_rows, gn_w)


MLA_TM = 512


def _rms_norm(x, w):
    return x * lax.rsqrt(jnp.mean(x * x, -1, keepdims=True) + EPS) * w


def _mla_kv(ckv_ref, kpe_ref, kvn_ref, wkv_ref, cos_ref, sin_ref, k_ref, v_ref, rotate):
    ckv = _rms_norm(ckv_ref[...].astype(F32), kvn_ref[...]).astype(BF16)
    kv = jnp.dot(ckv, wkv_ref[...], preferred_element_type=F32)
    lane = lax.broadcasted_iota(I32, (ckv.shape[0], LANES), 1)
    kpe = jnp.where(lane < MLA_ROPE, kpe_ref[...].astype(F32), 0.0)
    if rotate:
        kpe = _rope(kpe, cos_ref[...], sin_ref[...], MLA_ROPE // 4)
    kpe = kpe.astype(BF16)
    for h in range(MLA_HEADS):
        k_ref[:, h * QK_PAD:h * QK_PAD + MLA_NOPE] = kv[:, 2 * h * LANES:(2 * h + 1) * LANES].astype(BF16)
        k_ref[:, h * QK_PAD + MLA_NOPE:(h + 1) * QK_PAD] = kpe
        v_ref[:, h * MLA_DV:(h + 1) * MLA_DV] = kv[:, (2 * h + 1) * LANES:(2 * h + 2) * LANES].astype(BF16)


def _mla_latent_kernel(cq_ref, ckv_ref, kpe_ref, qn_ref, kvn_ref, wq_ref, wkv_ref, cos_ref, sin_ref,
                       q_ref, k_ref, v_ref):
    cq = _rms_norm(cq_ref[...].astype(F32), qn_ref[...]).astype(BF16)
    q = jnp.dot(cq, wq_ref[...], preferred_element_type=F32)
    cos = cos_ref[...]
    sin = sin_ref[...]
    scale = MLA_DQ ** -0.5 * LOG2_E
    for h in range(MLA_HEADS):
        lo = h * QK_PAD
        q_ref[:, lo:lo + MLA_NOPE] = (q[:, lo:lo + MLA_NOPE] * scale).astype(BF16)
        qpe = _rope(q[:, lo + MLA_NOPE:lo + QK_PAD], cos, sin, MLA_ROPE // 4)
        q_ref[:, lo + MLA_NOPE:lo + QK_PAD] = (qpe * scale).astype(BF16)
    _mla_kv(ckv_ref, kpe_ref, kvn_ref, wkv_ref, cos_ref, sin_ref, k_ref, v_ref, rotate=True)


def _mla_context_kernel(ckv_ref, kpe_ref, kvn_ref, wkv_ref, k_ref, v_ref):
    _mla_kv(ckv_ref, kpe_ref, kvn_ref, wkv_ref, None, None, k_ref, v_ref, rotate=False)


def _mla_latent(proj, q_norm, kv_norm, wq_pad, wkv, cos, sin):
    row = lambda w, off: pl.BlockSpec((MLA_TM, w), lambda i: (i, off // w))
    full = lambda a: pl.BlockSpec(a.shape, lambda i: (0, 0))
    table = pl.BlockSpec((MLA_TM, LANES), lambda i: (i % (SEQ // MLA_TM), 0))
    return pl.pallas_call(
        _mla_latent_kernel,
        grid=(N_TOK // MLA_TM,),
        in_specs=[row(MLA_Q_LORA, OFF_CQ), row(MLA_KV_LORA, OFF_CKV), row(LANES, OFF_KPE),
                  full(q_norm), full(kv_norm), full(wq_pad), full(wkv), table, table],
        out_specs=[
            pl.BlockSpec((MLA_TM, MLA_HEADS * QK_PAD), lambda i: (i, 0)),
            pl.BlockSpec((MLA_TM, MLA_HEADS * QK_PAD), lambda i: (i, 0)),
            pl.BlockSpec((MLA_TM, MLA_W), lambda i: (i, 0)),
        ],
        out_shape=[
            jax.ShapeDtypeStruct((N_TOK, MLA_HEADS * QK_PAD), BF16),
            jax.ShapeDtypeStruct((N_TOK, MLA_HEADS * QK_PAD), BF16),
            jax.ShapeDtypeStruct((N_TOK, MLA_W), BF16),
        ],
        compiler_params=_params(("arbitrary",)),
        name="mla_latent",
    )(proj, proj, proj, q_norm, kv_norm, wq_pad, wkv, cos, sin)


def _mla_context(proj_c, kv_norm, wkv):
    row = lambda w, off: pl.BlockSpec((MLA_TM, w), lambda i: (i, off // w))
    full = lambda a: pl.BlockSpec(a.shape, lambda i: (0, 0))
    return pl.pallas_call(
        _mla_context_kernel,
        grid=(N_CTX // MLA_TM,),
        in_specs=[row(MLA_KV_LORA, OFF_CKV), row(LANES, OFF_KPE), full(kv_norm), full(wkv)],
        out_specs=[
            pl.BlockSpec((MLA_TM, MLA_HEADS * QK_PAD), lambda i: (i, 0)),
            pl.BlockSpec((MLA_TM, MLA_W), lambda i: (i, 0)),
        ],
        out_shape=[
            jax.ShapeDtypeStruct((N_CTX, MLA_HEADS * QK_PAD), BF16),
            jax.ShapeDtypeStruct((N_CTX, MLA_W), BF16),
        ],
        compiler_params=_params(("arbitrary",)),
        name="mla_context",
    )(proj_c, proj_c, kv_norm, wkv)


ATT_TQ = 512


def _attention_kernel(q_ref, kc_ref, kl_ref, vc_ref, vl_ref, wo_ref, o_ref, wo_bf_ref, vext_ref):
    wo_bf_ref[...] = wo_ref[...].astype(BF16)

    @pl.when(pl.program_id(1) == 0)
    def _():
        lane = lax.broadcasted_iota(I32, (CTX_LEN + SEQ, MLA_DV), 1)
        ones_col = jnp.where(lane == 0, 1.0, 0.0).astype(BF16)
        for h in range(MLA_HEADS):
            dv = slice(h * MLA_DV, (h + 1) * MLA_DV)
            vext_ref[h, :CTX_LEN, :MLA_DV] = vc_ref[:, dv]
            vext_ref[h, CTX_LEN:, :MLA_DV] = vl_ref[:, dv]
            vext_ref[h, :, MLA_DV:] = ones_col

    for h in range(MLA_HEADS):
        qk = slice(h * QK_PAD, (h + 1) * QK_PAD)
        q = q_ref[:, qk]
        s_c = _dot_nt(q, kc_ref[:, qk])
        s_l = _dot_nt(q, kl_ref[:, qk])
        m = jnp.maximum(jnp.max(s_c, -1, keepdims=True), jnp.max(s_l, -1, keepdims=True))
        p_c = jnp.exp2(s_c - m).astype(BF16)
        p_l = jnp.exp2(s_l - m).astype(BF16)
        o = (jnp.dot(p_c, vext_ref[h, :CTX_LEN, :], preferred_element_type=F32)
             + jnp.dot(p_l, vext_ref[h, CTX_LEN:, :], preferred_element_type=F32))
        o_ref[:, h * MLA_DV:(h + 1) * MLA_DV] = (o[:, :MLA_DV] / o[:, MLA_DV:MLA_DV + 1]).astype(BF16)


def _attention(q, k_ctx, k_lat, v_ctx, v_lat, w_o):
    tiles = SEQ // ATT_TQ
    band = w_o.shape[0] // (BATCH * tiles)
    step = lambda b, i: (b * tiles + i, 0)
    return pl.pallas_call(
        _attention_kernel,
        grid=(BATCH, tiles),
        in_specs=[
            pl.BlockSpec((ATT_TQ, MLA_HEADS * QK_PAD), step),
            pl.BlockSpec((CTX_LEN, MLA_HEADS * QK_PAD), lambda b, i: (b, 0)),
            pl.BlockSpec((SEQ, MLA_HEADS * QK_PAD), lambda b, i: (b, 0)),
            pl.BlockSpec((CTX_LEN, MLA_W), lambda b, i: (b, 0)),
            pl.BlockSpec((SEQ, MLA_W), lambda b, i: (b, 0)),
            pl.BlockSpec((band, D_MODEL), step),
        ],
        out_specs=[pl.BlockSpec((ATT_TQ, MLA_W), step), pl.BlockSpec((band, D_MODEL), step)],
        out_shape=[jax.ShapeDtypeStruct((N_TOK, MLA_W), BF16), jax.ShapeDtypeStruct(w_o.shape, BF16)],
        scratch_shapes=[pltpu.VMEM((MLA_HEADS, CTX_LEN + SEQ, 2 * MLA_DV), BF16)],
        compiler_params=_params(("arbitrary", "arbitrary")),
        name="attention",
    )(q, k_ctx, k_lat, v_ctx, v_lat, w_o)


OUT_TM = 512
OUT_SUB = 256
ROUTE_E1, ROUTE_E2, ROUTE_R1, ROUTE_R2, ROUTE_W1, ROUTE_W2 = range(6)


def _outproj_kernel(ret_ref, att_ref, x_ref, wo_ref, gate_ref, shift_ref, scale_ref, lnw_ref, lnb_ref,
                    wr_ref, br_ref, h_ref, t_ref, route_ref, route_t_ref, count_ref, carry_ref, wsplit_ref):
    @pl.when(pl.program_id(0) == 0)
    def _():
        carry_ref[...] = jnp.zeros_like(carry_ref)
        w = wr_ref[...]
        w_hi = w.astype(BF16)
        wsplit_ref[:, :LANES] = w_hi
        wsplit_ref[:, LANES:] = (w - w_hi.astype(F32)).astype(BF16)

    subtiles = [slice(s * OUT_SUB, (s + 1) * OUT_SUB) for s in range(OUT_TM // OUT_SUB)]
    mixes = [jnp.dot(ret_ref[rows, :], wo_ref[:RET_W, :], preferred_element_type=F32)
             + jnp.dot(att_ref[rows, :], wo_ref[RET_W:, :], preferred_element_type=F32) for rows in subtiles]
    logits = [_outproj_norms(rows, mix, x_ref, gate_ref, shift_ref, scale_ref, lnw_ref, lnb_ref, br_ref, h_ref, t_ref,
                             wsplit_ref) for rows, mix in zip(subtiles, mixes)]
    _route(jnp.concatenate(logits, 0), carry_ref, count_ref, route_ref, route_t_ref)


def _outproj_norms(rows, mix, x_ref, gate_ref, shift_ref, scale_ref, lnw_ref, lnb_ref, br_ref, h_ref, t_ref,
                   wsplit_ref):
    h = _plain_norm(DEEPNORM_ALPHA * x_ref[rows, :] + gate_ref[0] * mix) * lnw_ref[...] + lnb_ref[...]
    h_ref[rows, :] = h
    t = _plain_norm(h) * (1.0 + scale_ref[0]) + shift_ref[0]
    t_ref[rows] = _to_token_tiles(t)

    t_hi = t.astype(BF16)
    t_lo = (t - t_hi.astype(F32)).astype(BF16)
    main = jnp.dot(t_hi, wsplit_ref[...], preferred_element_type=F32)
    corr = jnp.dot(t_lo, wsplit_ref[:, :LANES], preferred_element_type=F32)
    return main[:, :LANES] + (main[:, LANES:] + corr) + br_ref[...]


def _route(logits, carry_ref, count_ref, route_ref, route_t_ref):
    lane = lax.broadcasted_iota(I32, logits.shape, 1).astype(F32)
    neg = -jnp.inf
    big = float(LANES)

    def first_lane_of(mask):
        return jnp.min(jnp.where(mask, lane, big), -1, keepdims=True)

    is_group = lane < N_GROUPS
    gl = jnp.where(is_group, logits, neg)
    g_max = jnp.max(gl, -1, keepdims=True)
    g_idx = first_lane_of(is_group & (gl == g_max))
    g_prob = 1.0 / jnp.sum(jnp.where(is_group, jnp.exp(logits - g_max), 0.0), -1, keepdims=True)

    lo = N_GROUPS + g_idx * EXPERTS_PER_GROUP
    in_group = (lane >= lo) & (lane < lo + EXPERTS_PER_GROUP)
    el = jnp.where(in_group, logits, neg)
    v1 = jnp.max(el, -1, keepdims=True)
    i1 = first_lane_of(in_group & (el == v1))
    rest = in_group & (lane != i1)
    el2 = jnp.where(rest, logits, neg)
    v2 = jnp.max(el2, -1, keepdims=True)
    i2 = first_lane_of(rest & (el2 == v2))
    d = jnp.exp(v2 - v1)
    w1 = g_prob / (1.0 + d)
    w2 = g_prob * d / (1.0 + d)

    onehot = jnp.where((lane == i1) | (lane == i2), 1.0, 0.0)
    r = lax.broadcasted_iota(I32, (OUT_TM, OUT_TM), 0)
    c = lax.broadcasted_iota(I32, (OUT_TM, OUT_TM), 1)
    tri = jnp.where(c < r, 1.0, 0.0).astype(BF16)
    carry = carry_ref[0:1, :]
    before = jnp.dot(tri, onehot.astype(BF16), preferred_element_type=F32) + carry
    r1 = jnp.sum(jnp.where(lane == i1, before, 0.0), -1, keepdims=True)
    r2 = jnp.sum(jnp.where(lane == i2, before, 0.0), -1, keepdims=True)
    carry = carry + jnp.sum(onehot, 0, keepdims=True)
    carry_ref[...] = jnp.broadcast_to(carry, carry_ref.shape)
    count_ref[...] = jnp.broadcast_to(carry, count_ref.shape)

    rec = jnp.zeros_like(logits)
    for slot, val in ((ROUTE_E1, i1 - N_GROUPS), (ROUTE_E2, i2 - N_GROUPS), (ROUTE_R1, r1), (ROUTE_R2, r2),
                      (ROUTE_W1, w1), (ROUTE_W2, w2)):
        rec = jnp.where(lane == slot, val, rec)
    route_ref[...] = rec
    route_t_ref[...] = rec.T[:SUBLANES, :]


def _outproj(ret, att, x2d, wo_bf, mod_rows, ln_w, ln_b, w_route, b_route):
    tiles_per_sample = SEQ // OUT_TM
    row = lambda w: pl.BlockSpec((OUT_TM, w), lambda i: (i, 0))
    full = lambda a: pl.BlockSpec(a.shape, lambda i: (0, 0))
    mod = lambda j: pl.BlockSpec((1, 1, D_MODEL), lambda i: ((i // tiles_per_sample) * N_MOD + j, 0, 0))
    return pl.pallas_call(
        _outproj_kernel,
        grid=(N_TOK // OUT_TM,),
        in_specs=[row(RET_W), row(MLA_W), row(D_MODEL), full(wo_bf), mod(MOD_GATE1), mod(MOD_SHIFT2), mod(MOD_SCALE2),
                  full(ln_w), full(ln_b), full(w_route), full(b_route)],
        out_specs=[row(D_MODEL), pl.BlockSpec((OUT_TM, TOKEN_SUB, LANES), lambda i: (i, 0, 0)), row(LANES),
                   pl.BlockSpec((SUBLANES, OUT_TM), lambda i: (0, i)),
                   pl.BlockSpec((SUBLANES, LANES), lambda i: (0, 0))],
        out_shape=[
            jax.ShapeDtypeStruct((N_TOK, D_MODEL), F32),
            jax.ShapeDtypeStruct((N_TOK, TOKEN_SUB, LANES), BF16),
            jax.ShapeDtypeStruct((N_TOK, LANES), F32),
            jax.ShapeDtypeStruct((SUBLANES, N_TOK), F32),
            jax.ShapeDtypeStruct((SUBLANES, LANES), F32),
        ],
        scratch_shapes=[pltpu.VMEM((SUBLANES, LANES), F32), pltpu.VMEM((D_MODEL, 2 * LANES), BF16)],
        compiler_params=_params(("arbitrary",)),
        name="outproj_route",
    )(ret, att, x2d, wo_bf, mod_rows, mod_rows, mod_rows, ln_w, ln_b, w_route, b_route)


DISPATCH_TM = 2048


def _dispatch_kernel(pos_ref, t_ref, xs_ref, sem):
    i = pl.program_id(0)

    def copy(slot, j):
        return pltpu.make_async_copy(t_ref.at[j], xs_ref.at[pos_ref[slot * N_TOK + i * DISPATCH_TM + j]], sem)

    for j in range(DISPATCH_TM):
        for slot in range(2):
            copy(slot, j).start(priority=slot)
    for slot in range(2):
        pltpu.make_async_copy(t_ref, xs_ref.at[pl.ds(0, DISPATCH_TM)], sem).wait()


def _dispatch(pos, t_tiles):
    return pl.pallas_call(
        _dispatch_kernel,
        grid_spec=pltpu.PrefetchScalarGridSpec(
            num_scalar_prefetch=1,
            grid=(N_TOK // DISPATCH_TM,),
            in_specs=[pl.BlockSpec((DISPATCH_TM, TOKEN_SUB, LANES), lambda i, pos: (i, 0, 0))],
            out_specs=pl.BlockSpec(memory_space=pl.ANY),
            scratch_shapes=[pltpu.SemaphoreType.DMA(())],
        ),
        out_shape=jax.ShapeDtypeStruct((N_PAIRS, TOKEN_SUB, LANES), BF16),
        compiler_params=_params(("arbitrary",)),
        name="dispatch",
    )(pos, t_tiles)


def _experts_kernel(rend_ref, xs_ref, wg_ref, wu_ref, wd_ref, ys_ref, wgu_ref, wdn_ref, xbuf_ref, ybuf_ref,
                    xsem, ysem):
    e = pl.program_id(0)
    lo = jnp.where(e == 0, 0, rend_ref[jnp.maximum(e - 1, 0)])
    hi = rend_ref[e]

    def x_copy(g):
        return pltpu.make_async_copy(xs_ref.at[pl.ds(g * ROW_TILE, ROW_TILE)], xbuf_ref.at[g % 2], xsem.at[g % 2])

    def y_copy(g):
        return pltpu.make_async_copy(ybuf_ref.at[g % 2], ys_ref.at[pl.ds(g * ROW_TILE, ROW_TILE)], ysem.at[g % 2])

    @pl.when(e == 0)
    def _():
        x_copy(0).start()

    @pl.when(hi > lo)
    def _():
        wgu_ref[:, :D_EXPERT] = wg_ref[0].astype(BF16)
        wgu_ref[:, D_EXPERT:] = wu_ref[0].astype(BF16)
        wdn_ref[...] = wd_ref[0].astype(BF16)

        def tile(g, c):
            row0 = g * ROW_TILE
            owns_first_row = lo <= row0
            owns_last_row = hi >= row0 + ROW_TILE

            @pl.when(owns_first_row)
            def _():
                @pl.when(g + 1 < N_TILES)
                def _():
                    x_copy(g + 1).start()

                x_copy(g).wait()

            gu = jnp.dot(_from_token_tiles(xbuf_ref[g % 2]), wgu_ref[...], preferred_element_type=F32)
            hid = _silu(gu[:, :D_EXPERT]) * gu[:, D_EXPERT:]
            y = jnp.dot(hid.astype(BF16), wdn_ref[...], preferred_element_type=F32)

            @pl.when(owns_first_row)
            def _():
                @pl.when(g >= 2)
                def _():
                    y_copy(g - 2).wait()

                ybuf_ref[g % 2] = _to_token_tiles(y)

            @pl.when(jnp.logical_not(owns_first_row))
            def _():
                row = lax.broadcasted_iota(I32, y.shape, 0)
                earlier = _from_token_tiles(ybuf_ref[g % 2]).astype(F32)
                ybuf_ref[g % 2] = _to_token_tiles(jnp.where(row >= lo - row0, y, earlier))

            @pl.when(owns_last_row)
            def _():
                y_copy(g).start()

            return c

        lax.fori_loop(lo // ROW_TILE, (hi - 1) // ROW_TILE + 1, tile, 0)

    @pl.when(e == N_EXPERTS - 1)
    def _():
        y_copy(N_TILES - 2).wait()
        y_copy(N_TILES - 1).wait()


def _experts(row_end, xs, w_gate, w_up, w_down):
    w_gate = w_gate.reshape(N_EXPERTS, D_MODEL, D_EXPERT)
    w_up = w_up.reshape(N_EXPERTS, D_MODEL, D_EXPERT)
    w_down = w_down.reshape(N_EXPERTS, D_EXPERT, D_MODEL)
    expert = lambda e, rend: (e, 0, 0)
    tile_buf = pltpu.VMEM((2, ROW_TILE, TOKEN_SUB, LANES), BF16)
    return pl.pallas_call(
        _experts_kernel,
        grid_spec=pltpu.PrefetchScalarGridSpec(
            num_scalar_prefetch=1,
            grid=(N_EXPERTS,),
            in_specs=[
                pl.BlockSpec(memory_space=pl.ANY),
                pl.BlockSpec((1, D_MODEL, D_EXPERT), expert),
                pl.BlockSpec((1, D_MODEL, D_EXPERT), expert),
                pl.BlockSpec((1, D_EXPERT, D_MODEL), expert),
            ],
            out_specs=pl.BlockSpec(memory_space=pl.ANY),
            scratch_shapes=[pltpu.VMEM((D_MODEL, 2 * D_EXPERT), BF16), pltpu.VMEM((D_EXPERT, D_MODEL), BF16),
                            tile_buf, tile_buf, pltpu.SemaphoreType.DMA((2,)), pltpu.SemaphoreType.DMA((2,))],
        ),
        out_shape=jax.ShapeDtypeStruct((N_PAIRS, TOKEN_SUB, LANES), BF16),
        compiler_params=_params(("arbitrary",)),
        name="experts",
    )(row_end, xs, w_gate, w_up, w_down)


COMB_TM = 256


def _combine_kernel(pos_ref, ys_ref, h_ref, route_ref, gate_ref, lnw_ref, lnb_ref, o_ref, buf_ref, sem):
    i = pl.program_id(0)
    n = pl.num_programs(0)

    def copy(step, slot, pair, j):
        src = pos_ref[pair * N_TOK + step * COMB_TM + j]
        return pltpu.make_async_copy(ys_ref.at[src], buf_ref.at[slot, pair * COMB_TM + j], sem.at[slot])

    def start_all(step, slot):
        for pair in range(2):
            lax.fori_loop(0, COMB_TM, lambda j, c, pair=pair: (copy(step, slot, pair, j).start(priority=pair), c)[1], 0,
                          unroll=8)

    def wait_all(step, slot):
        pltpu.make_async_copy(ys_ref.at[pl.ds(0, 2 * COMB_TM)], buf_ref.at[slot], sem.at[slot]).wait()

    @pl.when(i == 0)
    def _():
        start_all(0, 0)

    slot = i % 2
    wait_all(i, slot)

    nxt = jnp.minimum(i + 1, n - 1)
    for pair in range(2):
        for j in range(COMB_TM):
            copy(nxt, 1 - slot, pair, j).start(priority=pair)

    y1 = _from_token_tiles(buf_ref[slot, :COMB_TM]).astype(F32)
    y2 = _from_token_tiles(buf_ref[slot, COMB_TM:]).astype(F32)
    y = route_ref[:, ROUTE_W1:ROUTE_W1 + 1] * y1 + route_ref[:, ROUTE_W2:ROUTE_W2 + 1] * y2
    z = DEEPNORM_ALPHA * h_ref[...] + gate_ref[0] * y
    o_ref[...] = _plain_norm(z) * lnw_ref[...] + lnb_ref[...]

    @pl.when(i == n - 1)
    def _():
        wait_all(i, 1 - slot)


def _combine(pos, ys, h, route, mod_rows, ln_w, ln_b):
    tiles_per_sample = SEQ // COMB_TM
    row = lambda w: pl.BlockSpec((COMB_TM, w), lambda i, pos: (i, 0))
    full = lambda a: pl.BlockSpec(a.shape, lambda i, pos: (0, 0))
    return pl.pallas_call(
        _combine_kernel,
        grid_spec=pltpu.PrefetchScalarGridSpec(
            num_scalar_prefetch=1,
            grid=(N_TOK // COMB_TM,),
            in_specs=[
                pl.BlockSpec(memory_space=pl.ANY), row(D_MODEL), row(LANES),
                pl.BlockSpec((1, 1, D_MODEL), lambda i, pos: ((i // tiles_per_sample) * N_MOD + MOD_GATE2, 0, 0)),
                full(ln_w), full(ln_b),
            ],
            out_specs=row(D_MODEL),
            scratch_shapes=[pltpu.VMEM((2, 2 * COMB_TM, TOKEN_SUB, LANES), BF16), pltpu.SemaphoreType.DMA((2,))],
        ),
        out_shape=jax.ShapeDtypeStruct((N_TOK, D_MODEL), F32),
        compiler_params=_params(("arbitrary",)),
        name="combine",
    )(pos, ys, h, route, mod_rows, ln_w, ln_b)


def _routing_tables(route_t, counts):
    cnt = counts[0, N_GROUPS:N_ROUTE].astype(I32)
    row_end = jnp.cumsum(cnt)
    e = route_t[ROUTE_E1:ROUTE_E2 + 1].astype(I32)
    rank = route_t[ROUTE_R1:ROUTE_R2 + 1].astype(I32)
    earlier = jnp.arange(N_EXPERTS, dtype=I32)[:, None, None] < e[None]
    base = jnp.sum(jnp.where(earlier, cnt[:, None, None], 0), 0)
    pos = (base + rank).reshape(-1)
    return pos, row_end


def kernel(x, c, ctx, c_ctx, w_ada, b_ada, w_in, ret_decay, ret_gn_w, mla_q_norm, mla_kv_norm, w_uq, w_ukv, w_o,
           ln1_w, ln1_b, router_group_w, router_group_b, router_expert_w, router_expert_b, expert_w_gate,
           expert_w_up, expert_w_down, ln2_w, ln2_b):
    x2d = x.reshape(N_TOK, D_MODEL)
    ctx2d = ctx.reshape(N_CTX, D_MODEL)

    cc = jnp.zeros((SUBLANES, D_MODEL), F32).at[:BATCH].set(c).at[BATCH].set(c_ctx)
    mod = _ada(cc, w_ada[0], b_ada)
    mod_rows = mod.reshape(SUBLANES * N_MOD, 1, D_MODEL)

    proj, w_in_blocks = _inproj_latent(x2d, mod_rows, jnp.swapaxes(w_in, 1, 2))
    proj_c = _inproj_context(ctx2d, mod_rows, w_in_blocks)

    cos_r, sin_r = _rope_tables(RET_DK)
    decay_rows = jnp.broadcast_to(ret_decay[0].reshape(2 * RET_HEADS, 1, 1), (2 * RET_HEADS, 1, LANES))
    ret = _retention(proj, proj_c, jnp.asarray(cos_r), jnp.asarray(sin_r), decay_rows, ret_gn_w)

    cos_m, sin_m = _rope_tables(MLA_ROPE)
    cos_m = np.concatenate([cos_m, np.ones_like(cos_m)], 1)
    sin_m = np.concatenate([sin_m, np.zeros_like(sin_m)], 1)
    wq = w_uq[0].reshape(MLA_Q_LORA, MLA_HEADS, MLA_DQ)
    wq_pad = jnp.pad(wq, ((0, 0), (0, 0), (0, QK_PAD - MLA_DQ))).reshape(MLA_Q_LORA, MLA_HEADS * QK_PAD).astype(BF16)
    wkv = w_ukv[0].astype(BF16)
    q, k_lat, v_lat = _mla_latent(proj, mla_q_norm, mla_kv_norm, wq_pad, wkv, jnp.asarray(cos_m), jnp.asarray(sin_m))
    k_ctx, v_ctx = _mla_context(proj_c, mla_kv_norm, wkv)
    att, w_o_bf = _attention(q, k_ctx, k_lat, v_ctx, v_lat, w_o[0])

    w_route = jnp.concatenate(
        [router_group_w[0], router_expert_w[0].transpose(1, 0, 2).reshape(D_MODEL, N_EXPERTS),
         jnp.zeros((D_MODEL, LANES - N_ROUTE), F32)], 1)
    b_route = jnp.concatenate(
        [router_group_b[0], router_expert_b[0].reshape(N_EXPERTS), jnp.zeros((LANES - N_ROUTE,), F32)])[None]
    h, t_tiles, route, route_t, counts = _outproj(ret, att, x2d, w_o_bf, mod_rows, ln1_w, ln1_b,
                                                   w_route, b_route)

    pos, row_end = _routing_tables(route_t, counts)
    xs = _dispatch(pos, t_tiles)
    ys = _experts(row_end, xs, expert_w_gate[0], expert_w_up[0], expert_w_down[0])
    out = _combine(pos, ys, h, route, mod_rows, ln2_w, ln2_b)
    return out.reshape(BATCH, SEQ, D_MODEL)
```

```python
import numpy as np
import jax
import jax.numpy as jnp
from jax import lax
from jax.experimental import pallas as pl
from jax.experimental.pallas import tpu as pltpu

F32 = jnp.float32
BF16 = jnp.bfloat16
I32 = jnp.int32

D_MODEL = 2048
BATCH = 4
SEQ = 2048
GRID_W = 64
CTX_LEN = 256
N_TOK = BATCH * SEQ
N_CTX = BATCH * CTX_LEN

RET_HEADS = 8
RET_DK = 128
RET_DV = 128
RET_W = RET_HEADS * RET_DV
CHUNK = 128
N_CHUNKS = SEQ // CHUNK

MLA_HEADS = 8
MLA_Q_LORA = 512
MLA_KV_LORA = 256
MLA_NOPE = 128
MLA_ROPE = 64
MLA_DV = 128
MLA_W = MLA_HEADS * MLA_DV
MLA_DQ = MLA_NOPE + MLA_ROPE
QK_PAD = 256

IN_SIZES = (RET_HEADS * RET_DK, RET_HEADS * RET_DK, RET_W, RET_W, MLA_Q_LORA, MLA_KV_LORA, MLA_ROPE)
IN_W = sum(IN_SIZES)
OFF_Q, OFF_K, OFF_V, OFF_G, OFF_CQ, OFF_CKV, OFF_KPE = (int(v) for v in np.cumsum((0,) + IN_SIZES[:-1]))

N_GROUPS = 4
EXPERTS_PER_GROUP = 8
N_EXPERTS = N_GROUPS * EXPERTS_PER_GROUP
D_EXPERT = 512
N_ROUTE = N_GROUPS + N_EXPERTS

N_MOD = 6
MOD_SHIFT1, MOD_SCALE1, MOD_GATE1, MOD_SHIFT2, MOD_SCALE2, MOD_GATE2 = range(N_MOD)

LOG2_E = float(np.log2(np.e))
ROPE_BASE = 10000.0
EPS = 1e-6
DEPTH = 1
DEEPNORM_ALPHA = (2.0 * DEPTH) ** 0.25

LANES = 128
SUBLANES = 8
ROW_TILE = 256
N_PAIRS = 2 * N_TOK
N_TILES = N_PAIRS // ROW_TILE
TOKEN_SUB = D_MODEL // LANES

V7X_VMEM_BYTES = 64 * 1024 * 1024
VMEM_LIMIT = V7X_VMEM_BYTES - V7X_VMEM_BYTES // 8


def _params(sem):
    return pltpu.CompilerParams(dimension_semantics=sem, vmem_limit_bytes=VMEM_LIMIT)


def _silu(x):
    return x * (1.0 / (1.0 + jnp.exp(-x)))


def _plain_norm(x):
    mu = jnp.mean(x, -1, keepdims=True)
    xc = x - mu
    var = jnp.mean(xc * xc, -1, keepdims=True)
    return xc * lax.rsqrt(var + EPS)


def _to_token_tiles(x):
    return x.astype(BF16).reshape(x.shape[0], TOKEN_SUB, LANES)


def _from_token_tiles(x):
    return x.reshape(x.shape[0], D_MODEL)


ADA_TN = 1024


def _ada_kernel(cc_ref, w_ref, b_ref, o_ref):
    s = _silu(cc_ref[...])
    o_ref[...] = jnp.dot(s, w_ref[...], preferred_element_type=F32,
                         precision=lax.Precision.HIGHEST) + b_ref[...]


def _ada(cc, w_ada, b_ada):
    n = w_ada.shape[1]
    return pl.pallas_call(
        _ada_kernel,
        grid=(n // ADA_TN,),
        in_specs=[
            pl.BlockSpec((SUBLANES, D_MODEL), lambda j: (0, 0)),
            pl.BlockSpec((D_MODEL, ADA_TN), lambda j: (0, j)),
            pl.BlockSpec((1, ADA_TN), lambda j: (0, j)),
        ],
        out_specs=pl.BlockSpec((SUBLANES, ADA_TN), lambda j: (0, j)),
        out_shape=jax.ShapeDtypeStruct((SUBLANES, n), F32),
        compiler_params=_params(("arbitrary",)),
        name="ada",
    )(cc, w_ada, b_ada)


INPROJ_TM = 1024
INPROJ_TN = 1024


INPROJ_SUB = 256
INPROJ_BLOCKS = -(-IN_W // INPROJ_TN)


def _inproj_products(x_ref, shift_ref, scale_ref, w, o_ref, xn_ref):
    first = pl.program_id(1) == 0

    @pl.when(first)
    def _():
        for s in range(INPROJ_TM // INPROJ_SUB):
            rows = slice(s * INPROJ_SUB, (s + 1) * INPROJ_SUB)
            y = (_plain_norm(x_ref[rows, :]) * (1.0 + scale_ref[0]) + shift_ref[0]).astype(BF16)
            xn_ref[rows, :] = y
            o_ref[rows, :] = _dot_nt(y, w[...]).astype(BF16)

    @pl.when(jnp.logical_not(first))
    def _():
        o_ref[...] = _dot_nt(xn_ref[...], w[...]).astype(BF16)


def _inproj_latent_kernel(x_ref, shift_ref, scale_ref, wf_ref, o_ref, wbf_ref, xn_ref, wbuf_ref, in_sem, out_sem):
    i = pl.program_id(0)
    k = pl.program_id(1)
    t = i * INPROJ_BLOCKS + k
    slot = t % 2
    n_steps = pl.num_programs(0) * INPROJ_BLOCKS

    def out_copy(block, s):
        return pltpu.make_async_copy(wbuf_ref.at[s], wbf_ref.at[block], out_sem.at[s])

    def in_copy(block, s):
        return pltpu.make_async_copy(wbf_ref.at[block], wbuf_ref.at[s], in_sem.at[s])

    @pl.when(i == 0)
    def _():
        @pl.when(k >= 2)
        def _():
            out_copy(k - 2, slot).wait()

        col = lax.broadcasted_iota(I32, (INPROJ_TN, D_MODEL), 0)
        wbuf_ref[slot] = jnp.where(col < IN_W - k * INPROJ_TN, wf_ref[0], 0.0).astype(BF16)
        out_copy(k, slot).start()

    @pl.when((t + 1 >= INPROJ_BLOCKS) & (t + 1 < n_steps))
    def _():
        @pl.when(t == INPROJ_BLOCKS - 1)
        def _():
            out_copy(INPROJ_BLOCKS - 2, 1 - slot).wait()

        @pl.when(t == INPROJ_BLOCKS)
        def _():
            out_copy(INPROJ_BLOCKS - 1, 1 - slot).wait()

        in_copy((k + 1) % INPROJ_BLOCKS, 1 - slot).start()

    @pl.when(i > 0)
    def _():
        in_copy(k, slot).wait()

    _inproj_products(x_ref, shift_ref, scale_ref, wbuf_ref.at[slot], o_ref, xn_ref)


def _inproj_context_kernel(x_ref, shift_ref, scale_ref, w_ref, o_ref, xn_ref):
    _inproj_products(x_ref, shift_ref, scale_ref, w_ref.at[0], o_ref, xn_ref)


def _inproj_mod_map(j, tiles_per_sample, sample_row0):
    return lambda i, k: ((sample_row0 + i // tiles_per_sample) * N_MOD + j, 0, 0)


def _inproj_latent(x2d, mod_rows, w_in_t):
    assert INPROJ_BLOCKS % 2 == 1
    last = INPROJ_BLOCKS - 1
    return pl.pallas_call(
        _inproj_latent_kernel,
        grid=(N_TOK // INPROJ_TM, INPROJ_BLOCKS),
        in_specs=[
            pl.BlockSpec((INPROJ_TM, D_MODEL), lambda i, k: (i, 0)),
            pl.BlockSpec((1, 1, D_MODEL), _inproj_mod_map(MOD_SHIFT1, SEQ // INPROJ_TM, 0)),
            pl.BlockSpec((1, 1, D_MODEL), _inproj_mod_map(MOD_SCALE1, SEQ // INPROJ_TM, 0)),
            pl.BlockSpec((1, INPROJ_TN, D_MODEL), lambda i, k: (0, jnp.where(i == 0, k, last), 0)),
        ],
        out_specs=[pl.BlockSpec((INPROJ_TM, INPROJ_TN), lambda i, k: (i, k)), pl.BlockSpec(memory_space=pl.ANY)],
        out_shape=[jax.ShapeDtypeStruct((N_TOK, IN_W), BF16),
                   jax.ShapeDtypeStruct((INPROJ_BLOCKS, INPROJ_TN, D_MODEL), BF16)],
        scratch_shapes=[pltpu.VMEM((INPROJ_TM, D_MODEL), BF16), pltpu.VMEM((2, INPROJ_TN, D_MODEL), BF16),
                        pltpu.SemaphoreType.DMA((2,)), pltpu.SemaphoreType.DMA((2,))],
        compiler_params=_params(("arbitrary", "arbitrary")),
        name="inproj",
    )(x2d, mod_rows, mod_rows, w_in_t)


def _inproj_context(ctx2d, mod_rows, w_blocks):
    return pl.pallas_call(
        _inproj_context_kernel,
        grid=(N_CTX // INPROJ_TM, INPROJ_BLOCKS),
        in_specs=[
            pl.BlockSpec((INPROJ_TM, D_MODEL), lambda i, k: (i, 0)),
            pl.BlockSpec((1, 1, D_MODEL), _inproj_mod_map(MOD_SHIFT1, N_CTX // INPROJ_TM, BATCH)),
            pl.BlockSpec((1, 1, D_MODEL), _inproj_mod_map(MOD_SCALE1, N_CTX // INPROJ_TM, BATCH)),
            pl.BlockSpec((1, INPROJ_TN, D_MODEL), lambda i, k: (k, 0, 0)),
        ],
        out_specs=pl.BlockSpec((INPROJ_TM, INPROJ_TN), lambda i, k: (i, k)),
        out_shape=jax.ShapeDtypeStruct((N_CTX, IN_W), BF16),
        scratch_shapes=[pltpu.VMEM((INPROJ_TM, D_MODEL), BF16)],
        compiler_params=_params(("arbitrary", "arbitrary")),
        name="inproj_ctx",
    )(ctx2d, mod_rows, mod_rows, w_blocks)


def _rope_tables(width):
    half = width // 2
    quarter = half // 2
    inv_freq = ROPE_BASE ** (-np.arange(0, half, 2, dtype=np.float64) / half)
    t = np.arange(SEQ)
    cos_parts, sin_parts = [], []
    for pos in (t // GRID_W, t % GRID_W):
        ang = pos[:, None].astype(np.float64) * inv_freq[None, :]
        c, s = np.cos(ang), np.sin(ang)
        cos_parts += [c, c]
        sin_parts += [-s, s]
    assert cos_parts[0].shape[1] == quarter
    return (np.concatenate(cos_parts, 1).astype(np.float32), np.concatenate(sin_parts, 1).astype(np.float32))


def _rope(x, cos, sin, quarter):
    lane = lax.broadcasted_iota(I32, x.shape, 1)
    first = (lane % (2 * quarter)) < quarter
    swapped = jnp.where(first, pltpu.roll(x, LANES - quarter, 1), pltpu.roll(x, quarter, 1))
    return x * cos + swapped * sin


def _dot_tn(a, b):
    return lax.dot_general(a, b, (((0,), (0,)), ((), ())), preferred_element_type=F32)


def _dot_nt(a, b):
    return lax.dot_general(a, b, (((1,), (1,)), ((), ())), preferred_element_type=F32)


RET_HG = 4


def _retention_kernel(q_ref, k_ref, v_ref, g_ref, kc_ref, vc_ref, cos_ref, sin_ref, df_ref, db_ref, gn_ref,
                      o_ref, qs_ref, ks_ref, st_ref):
    for hh in range(RET_HG):
        _retention_head(hh, slice(hh * LANES, (hh + 1) * LANES), q_ref, k_ref, v_ref, g_ref, kc_ref, vc_ref,
                        cos_ref, sin_ref, df_ref, db_ref, gn_ref, o_ref, qs_ref, ks_ref, st_ref)


def _retention_head(hh, hs, q_ref, k_ref, v_ref, g_ref, kc_ref, vc_ref, cos_ref, sin_ref, df_ref, db_ref, gn_ref,
                    o_ref, qs_ref, ks_ref, st_ref):
    k_scale = RET_DK ** -0.5
    lgf = jax.nn.log_sigmoid(df_ref[hh])
    lgb = jax.nn.log_sigmoid(db_ref[hh])

    cos = cos_ref[...]
    sin = sin_ref[...]
    qs_ref[hh] = _rope(q_ref[:, hs].astype(F32), cos, sin, RET_DK // 4).astype(BF16)
    ks_ref[hh] = _rope(k_ref[:, hs].astype(F32), cos, sin, RET_DK // 4) * k_scale

    rowi = lax.broadcasted_iota(I32, (CHUNK, LANES), 0).astype(F32)
    coli = lax.broadcasted_iota(I32, (CHUNK, LANES), 1).astype(F32)
    diff = rowi - coli
    decay = jnp.exp(jnp.where(diff >= 0, lgf * diff, -lgb * diff)) * jnp.where(diff == 0, 2.0, 1.0)
    zeta_f = jnp.exp(lgf * (CHUNK - 1.0 - rowi))
    eta_b = jnp.exp(lgb * rowi)
    xi_f = jnp.exp(lgf * (rowi + 1.0))
    xi_b = jnp.exp(lgb * (CHUNK - rowi))
    cdec_f = jnp.exp(lgf * float(CHUNK))
    cdec_b = jnp.exp(lgb * float(CHUNK))

    crow = lax.broadcasted_iota(I32, (CTX_LEN, LANES), 0).astype(F32)
    kc = kc_ref[:, hs].astype(F32) * k_scale
    vc = vc_ref[:, hs]
    s_f = _dot_tn((kc * jnp.exp(lgf * (CTX_LEN - 1.0 - crow))).astype(BF16), vc)
    s_b = _dot_tn((kc * jnp.exp(lgb * crow)).astype(BF16), vc)

    upd_f, upd_b = [], []
    for i in range(N_CHUNKS):
        rows = pl.ds(i * CHUNK, CHUNK)
        kch = ks_ref[hh, rows, :]
        vch = v_ref[rows, hs]
        upd_f.append(_dot_tn((kch * zeta_f).astype(BF16), vch))
        upd_b.append(_dot_tn((kch * eta_b).astype(BF16), vch))
    state = s_f
    for i in range(N_CHUNKS):
        st_ref[hh, i, :, :RET_DV] = state.astype(BF16)
        state = cdec_f * state + upd_f[i]
    state = s_b
    for i in reversed(range(N_CHUNKS)):
        st_ref[hh, i, :, RET_DV:] = state.astype(BF16)
        state = cdec_b * state + upd_b[i]

    gn_w = gn_ref[:, hs]
    for i in range(N_CHUNKS):
        rows = pl.ds(i * CHUNK, CHUNK)
        qch = qs_ref[hh, rows, :]
        scores = _dot_nt(qch, ks_ref[hh, rows, :].astype(BF16)) * decay
        o = jnp.dot(scores.astype(BF16), v_ref[rows, hs], preferred_element_type=F32)
        cross = jnp.dot(qch, st_ref[hh, i], preferred_element_type=F32)
        o = o + xi_f * cross[:, :RET_DV] + xi_b * cross[:, RET_DV:]
        y = _plain_norm(o) * gn_w
        o_ref[rows, hs] = (_silu(g_ref[rows, hs].astype(F32)) * y).astype(BF16)


def _retention(proj, proj_c, cos, sin, decay_rows, gn_w):
    width = RET_HG * LANES
    groups = RET_HEADS // RET_HG
    blk = lambda off: pl.BlockSpec((SEQ, width), lambda b, hg: (b, off // width + hg))
    blk_c = lambda off: pl.BlockSpec((CTX_LEN, width), lambda b, hg: (b, off // width + hg))
    table = pl.BlockSpec((SEQ, LANES), lambda b, hg: (0, 0))
    return pl.pallas_call(
        _retention_kernel,
        grid=(BATCH, groups),
        in_specs=[
            blk(OFF_Q), blk(OFF_K), blk(OFF_V), blk(OFF_G), blk_c(OFF_K), blk_c(OFF_V), table, table,
            pl.BlockSpec((RET_HG, 1, LANES), lambda b, hg: (hg, 0, 0)),
            pl.BlockSpec((RET_HG, 1, LANES), lambda b, hg: (groups + hg, 0, 0)),
            pl.BlockSpec((1, width), lambda b, hg: (0, hg)),
        ],
        out_specs=pl.BlockSpec((SEQ, width), lambda b, hg: (b, hg)),
        out_shape=jax.ShapeDtypeStruct((N_TOK, RET_W), BF16),
        scratch_shapes=[
            pltpu.VMEM((RET_HG, SEQ, RET_DK), BF16),
            pltpu.VMEM((RET_HG, SEQ, RET_DK), F32),
            pltpu.VMEM((RET_HG, N_CHUNKS, RET_DK, 2 * RET_DV), BF16),
        ],
        compiler_params=_params(("arbitrary", "arbitrary")),
        name="retention",
    )(proj, proj, proj, proj, proj_c, proj_c, cos, sin, decay_rows, decay_rows, gn_w)


MLA_TM = 512


def _rms_norm(x, w):
    return x * lax.rsqrt(jnp.mean(x * x, -1, keepdims=True) + EPS) * w


def _mla_kv(ckv_ref, kpe_ref, kvn_ref, wkv_ref, cos_ref, sin_ref, k_ref, v_ref, rotate):
    ckv = _rms_norm(ckv_ref[...].astype(F32), kvn_ref[...]).astype(BF16)
    kv = jnp.dot(ckv, wkv_ref[...], preferred_element_type=F32)
    lane = lax.broadcasted_iota(I32, (ckv.shape[0], LANES), 1)
    kpe = jnp.where(lane < MLA_ROPE, kpe_ref[...].astype(F32), 0.0)
    if rotate:
        kpe = _rope(kpe, cos_ref[...], sin_ref[...], MLA_ROPE // 4)
    kpe = kpe.astype(BF16)
    for h in range(MLA_HEADS):
        k_ref[:, h * QK_PAD:h * QK_PAD + MLA_NOPE] = kv[:, 2 * h * LANES:(2 * h + 1) * LANES].astype(BF16)
        k_ref[:, h * QK_PAD + MLA_NOPE:(h + 1) * QK_PAD] = kpe
        v_ref[:, h * MLA_DV:(h + 1) * MLA_DV] = kv[:, (2 * h + 1) * LANES:(2 * h + 2) * LANES].astype(BF16)


def _mla_latent_kernel(cq_ref, ckv_ref, kpe_ref, qn_ref, kvn_ref, wq_ref, wkv_ref, cos_ref, sin_ref,
                       q_ref, k_ref, v_ref):
    cq = _rms_norm(cq_ref[...].astype(F32), qn_ref[...]).astype(BF16)
    q = jnp.dot(cq, wq_ref[...], preferred_element_type=F32)
    cos = cos_ref[...]
    sin = sin_ref[...]
    scale = MLA_DQ ** -0.5 * LOG2_E
    for h in range(MLA_HEADS):
        lo = h * QK_PAD
        q_ref[:, lo:lo + MLA_NOPE] = (q[:, lo:lo + MLA_NOPE] * scale).astype(BF16)
        qpe = _rope(q[:, lo + MLA_NOPE:lo + QK_PAD], cos, sin, MLA_ROPE // 4)
        q_ref[:, lo + MLA_NOPE:lo + QK_PAD] = (qpe * scale).astype(BF16)
    _mla_kv(ckv_ref, kpe_ref, kvn_ref, wkv_ref, cos_ref, sin_ref, k_ref, v_ref, rotate=True)


def _mla_context_kernel(ckv_ref, kpe_ref, kvn_ref, wkv_ref, k_ref, v_ref):
    _mla_kv(ckv_ref, kpe_ref, kvn_ref, wkv_ref, None, None, k_ref, v_ref, rotate=False)


def _mla_latent(proj, q_norm, kv_norm, wq_pad, wkv, cos, sin):
    row = lambda w, off: pl.BlockSpec((MLA_TM, w), lambda i: (i, off // w))
    full = lambda a: pl.BlockSpec(a.shape, lambda i: (0, 0))
    table = pl.BlockSpec((MLA_TM, LANES), lambda i: (i % (SEQ // MLA_TM), 0))
    return pl.pallas_call(
        _mla_latent_kernel,
        grid=(N_TOK // MLA_TM,),
        in_specs=[row(MLA_Q_LORA, OFF_CQ), row(MLA_KV_LORA, OFF_CKV), row(LANES, OFF_KPE),
                  full(q_norm), full(kv_norm), full(wq_pad), full(wkv), table, table],
        out_specs=[
            pl.BlockSpec((MLA_TM, MLA_HEADS * QK_PAD), lambda i: (i, 0)),
            pl.BlockSpec((MLA_TM, MLA_HEADS * QK_PAD), lambda i: (i, 0)),
            pl.BlockSpec((MLA_TM, MLA_W), lambda i: (i, 0)),
        ],
        out_shape=[
            jax.ShapeDtypeStruct((N_TOK, MLA_HEADS * QK_PAD), BF16),
            jax.ShapeDtypeStruct((N_TOK, MLA_HEADS * QK_PAD), BF16),
            jax.ShapeDtypeStruct((N_TOK, MLA_W), BF16),
        ],
        compiler_params=_params(("arbitrary",)),
        name="mla_latent",
    )(proj, proj, proj, q_norm, kv_norm, wq_pad, wkv, cos, sin)


def _mla_context(proj_c, kv_norm, wkv):
    row = lambda w, off: pl.BlockSpec((MLA_TM, w), lambda i: (i, off // w))
    full = lambda a: pl.BlockSpec(a.shape, lambda i: (0, 0))
    return pl.pallas_call(
        _mla_context_kernel,
        grid=(N_CTX // MLA_TM,),
        in_specs=[row(MLA_KV_LORA, OFF_CKV), row(LANES, OFF_KPE), full(kv_norm), full(wkv)],
        out_specs=[
            pl.BlockSpec((MLA_TM, MLA_HEADS * QK_PAD), lambda i: (i, 0)),
            pl.BlockSpec((MLA_TM, MLA_W), lambda i: (i, 0)),
        ],
        out_shape=[
            jax.ShapeDtypeStruct((N_CTX, MLA_HEADS * QK_PAD), BF16),
            jax.ShapeDtypeStruct((N_CTX, MLA_W), BF16),
        ],
        compiler_params=_params(("arbitrary",)),
        name="mla_context",
    )(proj_c, proj_c, kv_norm, wkv)


ATT_TQ = 512


def _attention_kernel(q_ref, kc_ref, kl_ref, vc_ref, vl_ref, wo_ref, o_ref, wo_bf_ref, vext_ref):
    wo_bf_ref[...] = wo_ref[...].astype(BF16)

    @pl.when(pl.program_id(1) == 0)
    def _():
        lane = lax.broadcasted_iota(I32, (CTX_LEN + SEQ, MLA_DV), 1)
        ones_col = jnp.where(lane == 0, 1.0, 0.0).astype(BF16)
        for h in range(MLA_HEADS):
            dv = slice(h * MLA_DV, (h + 1) * MLA_DV)
            vext_ref[h, :CTX_LEN, :MLA_DV] = vc_ref[:, dv]
            vext_ref[h, CTX_LEN:, :MLA_DV] = vl_ref[:, dv]
            vext_ref[h, :, MLA_DV:] = ones_col

    for h in range(MLA_HEADS):
        qk = slice(h * QK_PAD, (h + 1) * QK_PAD)
        q = q_ref[:, qk]
        s_c = _dot_nt(q, kc_ref[:, qk])
        s_l = _dot_nt(q, kl_ref[:, qk])
        m = jnp.maximum(jnp.max(s_c, -1, keepdims=True), jnp.max(s_l, -1, keepdims=True))
        p_c = jnp.exp2(s_c - m).astype(BF16)
        p_l = jnp.exp2(s_l - m).astype(BF16)
        o = (jnp.dot(p_c, vext_ref[h, :CTX_LEN, :], preferred_element_type=F32)
             + jnp.dot(p_l, vext_ref[h, CTX_LEN:, :], preferred_element_type=F32))
        o_ref[:, h * MLA_DV:(h + 1) * MLA_DV] = (o[:, :MLA_DV] / o[:, MLA_DV:MLA_DV + 1]).astype(BF16)


def _attention(q, k_ctx, k_lat, v_ctx, v_lat, w_o):
    tiles = SEQ // ATT_TQ
    band = w_o.shape[0] // (BATCH * tiles)
    step = lambda b, i: (b * tiles + i, 0)
    return pl.pallas_call(
        _attention_kernel,
        grid=(BATCH, tiles),
        in_specs=[
            pl.BlockSpec((ATT_TQ, MLA_HEADS * QK_PAD), step),
            pl.BlockSpec((CTX_LEN, MLA_HEADS * QK_PAD), lambda b, i: (b, 0)),
            pl.BlockSpec((SEQ, MLA_HEADS * QK_PAD), lambda b, i: (b, 0)),
            pl.BlockSpec((CTX_LEN, MLA_W), lambda b, i: (b, 0)),
            pl.BlockSpec((SEQ, MLA_W), lambda b, i: (b, 0)),
            pl.BlockSpec((band, D_MODEL), step),
        ],
        out_specs=[pl.BlockSpec((ATT_TQ, MLA_W), step), pl.BlockSpec((band, D_MODEL), step)],
        out_shape=[jax.ShapeDtypeStruct((N_TOK, MLA_W), BF16), jax.ShapeDtypeStruct(w_o.shape, BF16)],
        scratch_shapes=[pltpu.VMEM((MLA_HEADS, CTX_LEN + SEQ, 2 * MLA_DV), BF16)],
        compiler_params=_params(("arbitrary", "arbitrary")),
        name="attention",
    )(q, k_ctx, k_lat, v_ctx, v_lat, w_o)


OUT_TM = 512
OUT_SUB = 256
ROUTE_E1, ROUTE_E2, ROUTE_R1, ROUTE_R2, ROUTE_W1, ROUTE_W2 = range(6)


def _outproj_kernel(ret_ref, att_ref, x_ref, wo_ref, gate_ref, shift_ref, scale_ref, lnw_ref, lnb_ref,
                    wr_ref, br_ref, h_ref, t_ref, route_ref, route_t_ref, count_ref, carry_ref, wsplit_ref):
    @pl.when(pl.program_id(0) == 0)
    def _():
        carry_ref[...] = jnp.zeros_like(carry_ref)
        w = wr_ref[...]
        w_hi = w.astype(BF16)
        wsplit_ref[:, :LANES] = w_hi
        wsplit_ref[:, LANES:] = (w - w_hi.astype(F32)).astype(BF16)

    subtiles = [slice(s * OUT_SUB, (s + 1) * OUT_SUB) for s in range(OUT_TM // OUT_SUB)]
    mixes = [jnp.dot(ret_ref[rows, :], wo_ref[:RET_W, :], preferred_element_type=F32)
             + jnp.dot(att_ref[rows, :], wo_ref[RET_W:, :], preferred_element_type=F32) for rows in subtiles]
    logits = [_outproj_norms(rows, mix, x_ref, gate_ref, shift_ref, scale_ref, lnw_ref, lnb_ref, br_ref, h_ref, t_ref,
                             wsplit_ref) for rows, mix in zip(subtiles, mixes)]
    _route(jnp.concatenate(logits, 0), carry_ref, count_ref, route_ref, route_t_ref)


def _outproj_norms(rows, mix, x_ref, gate_ref, shift_ref, scale_ref, lnw_ref, lnb_ref, br_ref, h_ref, t_ref,
                   wsplit_ref):
    h = _plain_norm(DEEPNORM_ALPHA * x_ref[rows, :] + gate_ref[0] * mix) * lnw_ref[...] + lnb_ref[...]
    h_ref[rows, :] = h
    t = _plain_norm(h) * (1.0 + scale_ref[0]) + shift_ref[0]
    t_ref[rows] = _to_token_tiles(t)

    t_hi = t.astype(BF16)
    t_lo = (t - t_hi.astype(F32)).astype(BF16)
    main = jnp.dot(t_hi, wsplit_ref[...], preferred_element_type=F32)
    corr = jnp.dot(t_lo, wsplit_ref[:, :LANES], preferred_element_type=F32)
    return main[:, :LANES] + (main[:, LANES:] + corr) + br_ref[...]


def _route(logits, carry_ref, count_ref, route_ref, route_t_ref):
    lane = lax.broadcasted_iota(I32, logits.shape, 1).astype(F32)
    neg = -jnp.inf
    big = float(LANES)

    def first_lane_of(mask):
        return jnp.min(jnp.where(mask, lane, big), -1, keepdims=True)

    is_group = lane < N_GROUPS
    gl = jnp.where(is_group, logits, neg)
    g_max = jnp.max(gl, -1, keepdims=True)
    g_idx = first_lane_of(is_group & (gl == g_max))
    g_prob = 1.0 / jnp.sum(jnp.where(is_group, jnp.exp(logits - g_max), 0.0), -1, keepdims=True)

    lo = N_GROUPS + g_idx * EXPERTS_PER_GROUP
    in_group = (lane >= lo) & (lane < lo + EXPERTS_PER_GROUP)
    el = jnp.where(in_group, logits, neg)
    v1 = jnp.max(el, -1, keepdims=True)
    i1 = first_lane_of(in_group & (el == v1))
    rest = in_group & (lane != i1)
    el2 = jnp.where(rest, logits, neg)
    v2 = jnp.max(el2, -1, keepdims=True)
    i2 = first_lane_of(rest & (el2 == v2))
    d = jnp.exp(v2 - v1)
    w1 = g_prob / (1.0 + d)
    w2 = g_prob * d / (1.0 + d)

    onehot = jnp.where((lane == i1) | (lane == i2), 1.0, 0.0)
    r = lax.broadcasted_iota(I32, (OUT_TM, OUT_TM), 0)
    c = lax.broadcasted_iota(I32, (OUT_TM, OUT_TM), 1)
    tri = jnp.where(c < r, 1.0, 0.0).astype(BF16)
    carry = carry_ref[0:1, :]
    before = jnp.dot(tri, onehot.astype(BF16), preferred_element_type=F32) + carry
    r1 = jnp.sum(jnp.where(lane == i1, before, 0.0), -1, keepdims=True)
    r2 = jnp.sum(jnp.where(lane == i2, before, 0.0), -1, keepdims=True)
    carry = carry + jnp.sum(onehot, 0, keepdims=True)
    carry_ref[...] = jnp.broadcast_to(carry, carry_ref.shape)
    count_ref[...] = jnp.broadcast_to(carry, count_ref.shape)

    rec = jnp.zeros_like(logits)
    for slot, val in ((ROUTE_E1, i1 - N_GROUPS), (ROUTE_E2, i2 - N_GROUPS), (ROUTE_R1, r1), (ROUTE_R2, r2),
                      (ROUTE_W1, w1), (ROUTE_W2, w2)):
        rec = jnp.where(lane == slot, val, rec)
    route_ref[...] = rec
    route_t_ref[...] = rec.T[:SUBLANES, :]


def _outproj(ret, att, x2d, wo_bf, mod_rows, ln_w, ln_b, w_route, b_route):
    tiles_per_sample = SEQ // OUT_TM
    row = lambda w: pl.BlockSpec((OUT_TM, w), lambda i: (i, 0))
    full = lambda a: pl.BlockSpec(a.shape, lambda i: (0, 0))
    mod = lambda j: pl.BlockSpec((1, 1, D_MODEL), lambda i: ((i // tiles_per_sample) * N_MOD + j, 0, 0))
    return pl.pallas_call(
        _outproj_kernel,
        grid=(N_TOK // OUT_TM,),
        in_specs=[row(RET_W), row(MLA_W), row(D_MODEL), full(wo_bf), mod(MOD_GATE1), mod(MOD_SHIFT2), mod(MOD_SCALE2),
                  full(ln_w), full(ln_b), full(w_route), full(b_route)],
        out_specs=[row(D_MODEL), pl.BlockSpec((OUT_TM, TOKEN_SUB, LANES), lambda i: (i, 0, 0)), row(LANES),
                   pl.BlockSpec((SUBLANES, OUT_TM), lambda i: (0, i)),
                   pl.BlockSpec((SUBLANES, LANES), lambda i: (0, 0))],
        out_shape=[
            jax.ShapeDtypeStruct((N_TOK, D_MODEL), F32),
            jax.ShapeDtypeStruct((N_TOK, TOKEN_SUB, LANES), BF16),
            jax.ShapeDtypeStruct((N_TOK, LANES), F32),
            jax.ShapeDtypeStruct((SUBLANES, N_TOK), F32),
            jax.ShapeDtypeStruct((SUBLANES, LANES), F32),
        ],
        scratch_shapes=[pltpu.VMEM((SUBLANES, LANES), F32), pltpu.VMEM((D_MODEL, 2 * LANES), BF16)],
        compiler_params=_params(("arbitrary",)),
        name="outproj_route",
    )(ret, att, x2d, wo_bf, mod_rows, mod_rows, mod_rows, ln_w, ln_b, w_route, b_route)


DISPATCH_TM = 1024


def _dispatch_kernel(pos_ref, t_ref, xs_ref, sem):
    i = pl.program_id(0)

    def copy(slot, j):
        return pltpu.make_async_copy(t_ref.at[j], xs_ref.at[pos_ref[slot * N_TOK + i * DISPATCH_TM + j]], sem)

    for j in range(DISPATCH_TM):
        for slot in range(2):
            copy(slot, j).start(priority=slot)
    for slot in range(2):
        pltpu.make_async_copy(t_ref, xs_ref.at[pl.ds(0, DISPATCH_TM)], sem).wait()


def _dispatch(pos, t_tiles):
    return pl.pallas_call(
        _dispatch_kernel,
        grid_spec=pltpu.PrefetchScalarGridSpec(
            num_scalar_prefetch=1,
            grid=(N_TOK // DISPATCH_TM,),
            in_specs=[pl.BlockSpec((DISPATCH_TM, TOKEN_SUB, LANES), lambda i, pos: (i, 0, 0))],
            out_specs=pl.BlockSpec(memory_space=pl.ANY),
            scratch_shapes=[pltpu.SemaphoreType.DMA(())],
        ),
        out_shape=jax.ShapeDtypeStruct((N_PAIRS, TOKEN_SUB, LANES), BF16),
        compiler_params=_params(("arbitrary",)),
        name="dispatch",
    )(pos, t_tiles)


def _experts_kernel(rend_ref, xs_ref, wg_ref, wu_ref, wd_ref, ys_ref, wgu_ref, wdn_ref, xbuf_ref, ybuf_ref,
                    xsem, ysem):
    e = pl.program_id(0)
    lo = jnp.where(e == 0, 0, rend_ref[jnp.maximum(e - 1, 0)])
    hi = rend_ref[e]

    def x_copy(g):
        return pltpu.make_async_copy(xs_ref.at[pl.ds(g * ROW_TILE, ROW_TILE)], xbuf_ref.at[g % 2], xsem.at[g % 2])

    def y_copy(g):
        return pltpu.make_async_copy(ybuf_ref.at[g % 2], ys_ref.at[pl.ds(g * ROW_TILE, ROW_TILE)], ysem.at[g % 2])

    @pl.when(e == 0)
    def _():
        x_copy(0).start()

    @pl.when(hi > lo)
    def _():
        wgu_ref[:, :D_EXPERT] = wg_ref[0].astype(BF16)
        wgu_ref[:, D_EXPERT:] = wu_ref[0].astype(BF16)
        wdn_ref[...] = wd_ref[0].astype(BF16)

        def tile(g, c):
            row0 = g * ROW_TILE
            owns_first_row = lo <= row0
            owns_last_row = hi >= row0 + ROW_TILE

            @pl.when(owns_first_row)
            def _():
                @pl.when(g + 1 < N_TILES)
                def _():
                    x_copy(g + 1).start()

                x_copy(g).wait()

            gu = jnp.dot(_from_token_tiles(xbuf_ref[g % 2]), wgu_ref[...], preferred_element_type=F32)
            hid = _silu(gu[:, :D_EXPERT]) * gu[:, D_EXPERT:]
            y = jnp.dot(hid.astype(BF16), wdn_ref[...], preferred_element_type=F32)

            @pl.when(owns_first_row)
            def _():
                @pl.when(g >= 2)
                def _():
                    y_copy(g - 2).wait()

                ybuf_ref[g % 2] = _to_token_tiles(y)

            @pl.when(jnp.logical_not(owns_first_row))
            def _():
                row = lax.broadcasted_iota(I32, y.shape, 0)
                earlier = _from_token_tiles(ybuf_ref[g % 2]).astype(F32)
                ybuf_ref[g % 2] = _to_token_tiles(jnp.where(row >= lo - row0, y, earlier))

            @pl.when(owns_last_row)
            def _():
                y_copy(g).start()

            return c

        lax.fori_loop(lo // ROW_TILE, (hi - 1) // ROW_TILE + 1, tile, 0)

    @pl.when(e == N_EXPERTS - 1)
    def _():
        y_copy(N_TILES - 2).wait()
        y_copy(N_TILES - 1).wait()


def _experts(row_end, xs, w_gate, w_up, w_down):
    w_gate = w_gate.reshape(N_EXPERTS, D_MODEL, D_EXPERT)
    w_up = w_up.reshape(N_EXPERTS, D_MODEL, D_EXPERT)
    w_down = w_down.reshape(N_EXPERTS, D_EXPERT, D_MODEL)
    expert = lambda e, rend: (e, 0, 0)
    tile_buf = pltpu.VMEM((2, ROW_TILE, TOKEN_SUB, LANES), BF16)
    return pl.pallas_call(
        _experts_kernel,
        grid_spec=pltpu.PrefetchScalarGridSpec(
            num_scalar_prefetch=1,
            grid=(N_EXPERTS,),
            in_specs=[
                pl.BlockSpec(memory_space=pl.ANY),
                pl.BlockSpec((1, D_MODEL, D_EXPERT), expert),
                pl.BlockSpec((1, D_MODEL, D_EXPERT), expert),
                pl.BlockSpec((1, D_EXPERT, D_MODEL), expert),
            ],
            out_specs=pl.BlockSpec(memory_space=pl.ANY),
            scratch_shapes=[pltpu.VMEM((D_MODEL, 2 * D_EXPERT), BF16), pltpu.VMEM((D_EXPERT, D_MODEL), BF16),
                            tile_buf, tile_buf, pltpu.SemaphoreType.DMA((2,)), pltpu.SemaphoreType.DMA((2,))],
        ),
        out_shape=jax.ShapeDtypeStruct((N_PAIRS, TOKEN_SUB, LANES), BF16),
        compiler_params=_params(("arbitrary",)),
        name="experts",
    )(row_end, xs, w_gate, w_up, w_down)


COMB_TM = 256


def _combine_kernel(pos_ref, ys_ref, h_ref, route_ref, gate_ref, lnw_ref, lnb_ref, o_ref, buf_ref, sem):
    i = pl.program_id(0)
    n = pl.num_programs(0)

    def copy(step, slot, pair, j):
        src = pos_ref[pair * N_TOK + step * COMB_TM + j]
        return pltpu.make_async_copy(ys_ref.at[src], buf_ref.at[slot, pair * COMB_TM + j], sem.at[slot])

    def start_all(step, slot):
        for pair in range(2):
            lax.fori_loop(0, COMB_TM, lambda j, c, pair=pair: (copy(step, slot, pair, j).start(priority=pair), c)[1], 0,
                          unroll=8)

    def wait_all(step, slot):
        pltpu.make_async_copy(ys_ref.at[pl.ds(0, 2 * COMB_TM)], buf_ref.at[slot], sem.at[slot]).wait()

    @pl.when(i == 0)
    def _():
        start_all(0, 0)

    slot = i % 2
    wait_all(i, slot)

    nxt = jnp.minimum(i + 1, n - 1)
    for pair in range(2):
        for j in range(COMB_TM):
            copy(nxt, 1 - slot, pair, j).start(priority=pair)

    y1 = _from_token_tiles(buf_ref[slot, :COMB_TM]).astype(F32)
    y2 = _from_token_tiles(buf_ref[slot, COMB_TM:]).astype(F32)
    y = route_ref[:, ROUTE_W1:ROUTE_W1 + 1] * y1 + route_ref[:, ROUTE_W2:ROUTE_W2 + 1] * y2
    z = DEEPNORM_ALPHA * h_ref[...] + gate_ref[0] * y
    o_ref[...] = _plain_norm(z) * lnw_ref[...] + lnb_ref[...]

    @pl.when(i == n - 1)
    def _():
        wait_all(i, 1 - slot)


def _combine(pos, ys, h, route, mod_rows, ln_w, ln_b):
    tiles_per_sample = SEQ // COMB_TM
    row = lambda w: pl.BlockSpec((COMB_TM, w), lambda i, pos: (i, 0))
    full = lambda a: pl.BlockSpec(a.shape, lambda i, pos: (0, 0))
    return pl.pallas_call(
        _combine_kernel,
        grid_spec=pltpu.PrefetchScalarGridSpec(
            num_scalar_prefetch=1,
            grid=(N_TOK // COMB_TM,),
            in_specs=[
                pl.BlockSpec(memory_space=pl.ANY), row(D_MODEL), row(LANES),
                pl.BlockSpec((1, 1, D_MODEL), lambda i, pos: ((i // tiles_per_sample) * N_MOD + MOD_GATE2, 0, 0)),
                full(ln_w), full(ln_b),
            ],
            out_specs=row(D_MODEL),
            scratch_shapes=[pltpu.VMEM((2, 2 * COMB_TM, TOKEN_SUB, LANES), BF16), pltpu.SemaphoreType.DMA((2,))],
        ),
        out_shape=jax.ShapeDtypeStruct((N_TOK, D_MODEL), F32),
        compiler_params=_params(("arbitrary",)),
        name="combine",
    )(pos, ys, h, route, mod_rows, ln_w, ln_b)


def _routing_tables(route_t, counts):
    cnt = counts[0, N_GROUPS:N_ROUTE].astype(I32)
    row_end = jnp.cumsum(cnt)
    e = route_t[ROUTE_E1:ROUTE_E2 + 1].astype(I32)
    rank = route_t[ROUTE_R1:ROUTE_R2 + 1].astype(I32)
    earlier = jnp.arange(N_EXPERTS, dtype=I32)[:, None, None] < e[None]
    base = jnp.sum(jnp.where(earlier, cnt[:, None, None], 0), 0)
    pos = (base + rank).reshape(-1)
    return pos, row_end


def kernel(x, c, ctx, c_ctx, w_ada, b_ada, w_in, ret_decay, ret_gn_w, mla_q_norm, mla_kv_norm, w_uq, w_ukv, w_o,
           ln1_w, ln1_b, router_group_w, router_group_b, router_expert_w, router_expert_b, expert_w_gate,
           expert_w_up, expert_w_down, ln2_w, ln2_b):
    x2d = x.reshape(N_TOK, D_MODEL)
    ctx2d = ctx.reshape(N_CTX, D_MODEL)

    cc = jnp.zeros((SUBLANES, D_MODEL), F32).at[:BATCH].set(c).at[BATCH].set(c_ctx)
    mod = _ada(cc, w_ada[0], b_ada)
    mod_rows = mod.reshape(SUBLANES * N_MOD, 1, D_MODEL)

    proj, w_in_blocks = _inproj_latent(x2d, mod_rows, jnp.swapaxes(w_in, 1, 2))
    proj_c = _inproj_context(ctx2d, mod_rows, w_in_blocks)

    cos_r, sin_r = _rope_tables(RET_DK)
    decay_rows = jnp.broadcast_to(ret_decay[0].reshape(2 * RET_HEADS, 1, 1), (2 * RET_HEADS, 1, LANES))
    ret = _retention(proj, proj_c, jnp.asarray(cos_r), jnp.asarray(sin_r), decay_rows, ret_gn_w)

    cos_m, sin_m = _rope_tables(MLA_ROPE)
    cos_m = np.concatenate([cos_m, np.ones_like(cos_m)], 1)
    sin_m = np.concatenate([sin_m, np.zeros_like(sin_m)], 1)
    wq = w_uq[0].reshape(MLA_Q_LORA, MLA_HEADS, MLA_DQ)
    wq_pad = jnp.pad(wq, ((0, 0), (0, 0), (0, QK_PAD - MLA_DQ))).reshape(MLA_Q_LORA, MLA_HEADS * QK_PAD).astype(BF16)
    wkv = w_ukv[0].astype(BF16)
    q, k_lat, v_lat = _mla_latent(proj, mla_q_norm, mla_kv_norm, wq_pad, wkv, jnp.asarray(cos_m), jnp.asarray(sin_m))
    k_ctx, v_ctx = _mla_context(proj_c, mla_kv_norm, wkv)
    att, w_o_bf = _attention(q, k_ctx, k_lat, v_ctx, v_lat, w_o[0])

    w_route = jnp.concatenate(
        [router_group_w[0], router_expert_w[0].transpose(1, 0, 2).reshape(D_MODEL, N_EXPERTS),
         jnp.zeros((D_MODEL, LANES - N_ROUTE), F32)], 1)
    b_route = jnp.concatenate(
        [router_group_b[0], router_expert_b[0].reshape(N_EXPERTS), jnp.zeros((LANES - N_ROUTE,), F32)])[None]
    h, t_tiles, route, route_t, counts = _outproj(ret, att, x2d, w_o_bf, mod_rows, ln1_w, ln1_b,
                                                   w_route, b_route)

    pos, row_end = _routing_tables(route_t, counts)
    xs = _dispatch(pos, t_tiles)
    ys = _experts(row_end, xs, expert_w_gate[0], expert_w_up[0], expert_w_down[0])
    out = _combine(pos, ys, h, route, mod_rows, ln2_w, ln2_b)
    return out.reshape(BATCH, SEQ, D_MODEL)
```

```python
import numpy as np
import jax
import jax.numpy as jnp
from jax import lax
from jax.experimental import pallas as pl
from jax.experimental.pallas import tpu as pltpu

F32 = jnp.float32
BF16 = jnp.bfloat16
I32 = jnp.int32

D_MODEL = 2048
BATCH = 4
SEQ = 2048
GRID_W = 64
CTX_LEN = 256
N_TOK = BATCH * SEQ
N_CTX = BATCH * CTX_LEN

RET_HEADS = 8
RET_DK = 128
RET_DV = 128
RET_W = RET_HEADS * RET_DV
CHUNK = 128
N_CHUNKS = SEQ // CHUNK

MLA_HEADS = 8
MLA_Q_LORA = 512
MLA_KV_LORA = 256
MLA_NOPE = 128
MLA_ROPE = 64
MLA_DV = 128
MLA_W = MLA_HEADS * MLA_DV
MLA_DQ = MLA_NOPE + MLA_ROPE
QK_PAD = 256

IN_SIZES = (RET_HEADS * RET_DK, RET_HEADS * RET_DK, RET_W, RET_W, MLA_Q_LORA, MLA_KV_LORA, MLA_ROPE)
IN_W = sum(IN_SIZES)
OFF_Q, OFF_K, OFF_V, OFF_G, OFF_CQ, OFF_CKV, OFF_KPE = (int(v) for v in np.cumsum((0,) + IN_SIZES[:-1]))

N_GROUPS = 4
EXPERTS_PER_GROUP = 8
N_EXPERTS = N_GROUPS * EXPERTS_PER_GROUP
D_EXPERT = 512
N_ROUTE = N_GROUPS + N_EXPERTS

N_MOD = 6
MOD_SHIFT1, MOD_SCALE1, MOD_GATE1, MOD_SHIFT2, MOD_SCALE2, MOD_GATE2 = range(N_MOD)

LOG2_E = float(np.log2(np.e))
ROPE_BASE = 10000.0
EPS = 1e-6
DEPTH = 1
DEEPNORM_ALPHA = (2.0 * DEPTH) ** 0.25

LANES = 128
SUBLANES = 8
ROW_TILE = 256
N_PAIRS = 2 * N_TOK
N_TILES = N_PAIRS // ROW_TILE
TOKEN_SUB = D_MODEL // LANES

V7X_VMEM_BYTES = 64 * 1024 * 1024
VMEM_LIMIT = V7X_VMEM_BYTES - V7X_VMEM_BYTES // 8


def _params(sem):
    return pltpu.CompilerParams(dimension_semantics=sem, vmem_limit_bytes=VMEM_LIMIT)


def _silu(x):
    return x * (1.0 / (1.0 + jnp.exp(-x)))


def _plain_norm(x):
    mu = jnp.mean(x, -1, keepdims=True)
    xc = x - mu
    var = jnp.mean(xc * xc, -1, keepdims=True)
    return xc * lax.rsqrt(var + EPS)


def _to_token_tiles(x):
    return x.astype(BF16).reshape(x.shape[0], TOKEN_SUB, LANES)


def _from_token_tiles(x):
    return x.reshape(x.shape[0], D_MODEL)


ADA_TN = 1024


def _ada_kernel(cc_ref, w_ref, b_ref, o_ref):
    s = _silu(cc_ref[...])
    o_ref[...] = jnp.dot(s, w_ref[...], preferred_element_type=F32,
                         precision=lax.Precision.HIGHEST) + b_ref[...]


def _ada(cc, w_ada, b_ada):
    n = w_ada.shape[1]
    return pl.pallas_call(
        _ada_kernel,
        grid=(n // ADA_TN,),
        in_specs=[
            pl.BlockSpec((SUBLANES, D_MODEL), lambda j: (0, 0)),
            pl.BlockSpec((D_MODEL, ADA_TN), lambda j: (0, j)),
            pl.BlockSpec((1, ADA_TN), lambda j: (0, j)),
        ],
        out_specs=pl.BlockSpec((SUBLANES, ADA_TN), lambda j: (0, j)),
        out_shape=jax.ShapeDtypeStruct((SUBLANES, n), F32),
        compiler_params=_params(("arbitrary",)),
        name="ada",
    )(cc, w_ada, b_ada)


INPROJ_TM = 1024
INPROJ_TN = 1024


INPROJ_SUB = 256
INPROJ_BLOCKS = -(-IN_W // INPROJ_TN)


def _inproj_products(x_ref, shift_ref, scale_ref, w, o_ref, xn_ref):
    first = pl.program_id(1) == 0

    @pl.when(first)
    def _():
        for s in range(INPROJ_TM // INPROJ_SUB):
            rows = slice(s * INPROJ_SUB, (s + 1) * INPROJ_SUB)
            y = (_plain_norm(x_ref[rows, :]) * (1.0 + scale_ref[0]) + shift_ref[0]).astype(BF16)
            xn_ref[rows, :] = y
            o_ref[rows, :] = _dot_nt(y, w[...]).astype(BF16)

    @pl.when(jnp.logical_not(first))
    def _():
        o_ref[...] = _dot_nt(xn_ref[...], w[...]).astype(BF16)


def _inproj_latent_kernel(x_ref, shift_ref, scale_ref, wf_ref, o_ref, wbf_ref, xn_ref, wbuf_ref, in_sem, out_sem):
    i = pl.program_id(0)
    k = pl.program_id(1)
    t = i * INPROJ_BLOCKS + k
    slot = t % 2
    n_steps = pl.num_programs(0) * INPROJ_BLOCKS

    def out_copy(block, s):
        return pltpu.make_async_copy(wbuf_ref.at[s], wbf_ref.at[block], out_sem.at[s])

    def in_copy(block, s):
        return pltpu.make_async_copy(wbf_ref.at[block], wbuf_ref.at[s], in_sem.at[s])

    @pl.when(i == 0)
    def _():
        @pl.when(k >= 2)
        def _():
            out_copy(k - 2, slot).wait()

        col = lax.broadcasted_iota(I32, (INPROJ_TN, D_MODEL), 0)
        wbuf_ref[slot] = jnp.where(col < IN_W - k * INPROJ_TN, wf_ref[0], 0.0).astype(BF16)
        out_copy(k, slot).start()

    @pl.when((t + 1 >= INPROJ_BLOCKS) & (t + 1 < n_steps))
    def _():
        @pl.when(t == INPROJ_BLOCKS - 1)
        def _():
            out_copy(INPROJ_BLOCKS - 2, 1 - slot).wait()

        @pl.when(t == INPROJ_BLOCKS)
        def _():
            out_copy(INPROJ_BLOCKS - 1, 1 - slot).wait()

        in_copy((k + 1) % INPROJ_BLOCKS, 1 - slot).start()

    @pl.when(i > 0)
    def _():
        in_copy(k, slot).wait()

    _inproj_products(x_ref, shift_ref, scale_ref, wbuf_ref.at[slot], o_ref, xn_ref)


def _inproj_context_kernel(x_ref, shift_ref, scale_ref, w_ref, o_ref, xn_ref):
    _inproj_products(x_ref, shift_ref, scale_ref, w_ref.at[0], o_ref, xn_ref)


def _inproj_mod_map(j, tiles_per_sample, sample_row0):
    return lambda i, k: ((sample_row0 + i // tiles_per_sample) * N_MOD + j, 0, 0)


def _inproj_latent(x2d, mod_rows, w_in_t):
    assert INPROJ_BLOCKS % 2 == 1
    last = INPROJ_BLOCKS - 1
    return pl.pallas_call(
        _inproj_latent_kernel,
        grid=(N_TOK // INPROJ_TM, INPROJ_BLOCKS),
        in_specs=[
            pl.BlockSpec((INPROJ_TM, D_MODEL), lambda i, k: (i, 0)),
            pl.BlockSpec((1, 1, D_MODEL), _inproj_mod_map(MOD_SHIFT1, SEQ // INPROJ_TM, 0)),
            pl.BlockSpec((1, 1, D_MODEL), _inproj_mod_map(MOD_SCALE1, SEQ // INPROJ_TM, 0)),
            pl.BlockSpec((1, INPROJ_TN, D_MODEL), lambda i, k: (0, jnp.where(i == 0, k, last), 0)),
        ],
        out_specs=[pl.BlockSpec((INPROJ_TM, INPROJ_TN), lambda i, k: (i, k)), pl.BlockSpec(memory_space=pl.ANY)],
        out_shape=[jax.ShapeDtypeStruct((N_TOK, IN_W), BF16),
                   jax.ShapeDtypeStruct((INPROJ_BLOCKS, INPROJ_TN, D_MODEL), BF16)],
        scratch_shapes=[pltpu.VMEM((INPROJ_TM, D_MODEL), BF16), pltpu.VMEM((2, INPROJ_TN, D_MODEL), BF16),
                        pltpu.SemaphoreType.DMA((2,)), pltpu.SemaphoreType.DMA((2,))],
        compiler_params=_params(("arbitrary", "arbitrary")),
        name="inproj",
    )(x2d, mod_rows, mod_rows, w_in_t)


def _inproj_context(ctx2d, mod_rows, w_blocks):
    return pl.pallas_call(
        _inproj_context_kernel,
        grid=(N_CTX // INPROJ_TM, INPROJ_BLOCKS),
        in_specs=[
            pl.BlockSpec((INPROJ_TM, D_MODEL), lambda i, k: (i, 0)),
            pl.BlockSpec((1, 1, D_MODEL), _inproj_mod_map(MOD_SHIFT1, N_CTX // INPROJ_TM, BATCH)),
            pl.BlockSpec((1, 1, D_MODEL), _inproj_mod_map(MOD_SCALE1, N_CTX // INPROJ_TM, BATCH)),
            pl.BlockSpec((1, INPROJ_TN, D_MODEL), lambda i, k: (k, 0, 0)),
        ],
        out_specs=pl.BlockSpec((INPROJ_TM, INPROJ_TN), lambda i, k: (i, k)),
        out_shape=jax.ShapeDtypeStruct((N_CTX, IN_W), BF16),
        scratch_shapes=[pltpu.VMEM((INPROJ_TM, D_MODEL), BF16)],
        compiler_params=_params(("arbitrary", "arbitrary")),
        name="inproj_ctx",
    )(ctx2d, mod_rows, mod_rows, w_blocks)


def _rope_tables(width):
    half = width // 2
    quarter = half // 2
    inv_freq = ROPE_BASE ** (-np.arange(0, half, 2, dtype=np.float64) / half)
    t = np.arange(SEQ)
    cos_parts, sin_parts = [], []
    for pos in (t // GRID_W, t % GRID_W):
        ang = pos[:, None].astype(np.float64) * inv_freq[None, :]
        c, s = np.cos(ang), np.sin(ang)
        cos_parts += [c, c]
        sin_parts += [-s, s]
    assert cos_parts[0].shape[1] == quarter
    return (np.concatenate(cos_parts, 1).astype(np.float32), np.concatenate(sin_parts, 1).astype(np.float32))


def _rope(x, cos, sin, quarter):
    lane = lax.broadcasted_iota(I32, x.shape, 1)
    first = (lane % (2 * quarter)) < quarter
    swapped = jnp.where(first, pltpu.roll(x, LANES - quarter, 1), pltpu.roll(x, quarter, 1))
    return x * cos + swapped * sin


def _dot_tn(a, b):
    return lax.dot_general(a, b, (((0,), (0,)), ((), ())), preferred_element_type=F32)


def _dot_nt(a, b):
    return lax.dot_general(a, b, (((1,), (1,)), ((), ())), preferred_element_type=F32)


RET_HG = 4


def _retention_kernel(q_ref, k_ref, v_ref, g_ref, kc_ref, vc_ref, cos_ref, sin_ref, df_ref, db_ref, gn_ref,
                      o_ref, qs_ref, ks_ref, st_ref):
    for hh in range(RET_HG):
        _retention_head(hh, slice(hh * LANES, (hh + 1) * LANES), q_ref, k_ref, v_ref, g_ref, kc_ref, vc_ref,
                        cos_ref, sin_ref, df_ref, db_ref, gn_ref, o_ref, qs_ref, ks_ref, st_ref)


def _retention_head(hh, hs, q_ref, k_ref, v_ref, g_ref, kc_ref, vc_ref, cos_ref, sin_ref, df_ref, db_ref, gn_ref,
                    o_ref, qs_ref, ks_ref, st_ref):
    k_scale = RET_DK ** -0.5
    lgf = jax.nn.log_sigmoid(df_ref[hh])
    lgb = jax.nn.log_sigmoid(db_ref[hh])

    cos = cos_ref[...]
    sin = sin_ref[...]
    qs_ref[hh] = _rope(q_ref[:, hs].astype(F32), cos, sin, RET_DK // 4).astype(BF16)
    ks_ref[hh] = _rope(k_ref[:, hs].astype(F32), cos, sin, RET_DK // 4) * k_scale

    rowi = lax.broadcasted_iota(I32, (CHUNK, LANES), 0).astype(F32)
    coli = lax.broadcasted_iota(I32, (CHUNK, LANES), 1).astype(F32)
    diff = rowi - coli
    decay = jnp.exp(jnp.where(diff >= 0, lgf * diff, -lgb * diff)) * jnp.where(diff == 0, 2.0, 1.0)
    zeta_f = jnp.exp(lgf * (CHUNK - 1.0 - rowi))
    eta_b = jnp.exp(lgb * rowi)
    xi_f = jnp.exp(lgf * (rowi + 1.0))
    xi_b = jnp.exp(lgb * (CHUNK - rowi))
    cdec_f = jnp.exp(lgf * float(CHUNK))
    cdec_b = jnp.exp(lgb * float(CHUNK))

    crow = lax.broadcasted_iota(I32, (CTX_LEN, LANES), 0).astype(F32)
    kc = kc_ref[:, hs].astype(F32) * k_scale
    vc = vc_ref[:, hs]
    s_f = _dot_tn((kc * jnp.exp(lgf * (CTX_LEN - 1.0 - crow))).astype(BF16), vc)
    s_b = _dot_tn((kc * jnp.exp(lgb * crow)).astype(BF16), vc)

    upd_f, upd_b = [], []
    for i in range(N_CHUNKS):
        rows = pl.ds(i * CHUNK, CHUNK)
        kch = ks_ref[hh, rows, :]
        vch = v_ref[rows, hs]
        upd_f.append(_dot_tn((kch * zeta_f).astype(BF16), vch))
        upd_b.append(_dot_tn((kch * eta_b).astype(BF16), vch))
    state = s_f
    for i in range(N_CHUNKS):
        st_ref[hh, i, :, :RET_DV] = state.astype(BF16)
        state = cdec_f * state + upd_f[i]
    state = s_b
    for i in reversed(range(N_CHUNKS)):
        st_ref[hh, i, :, RET_DV:] = state.astype(BF16)
        state = cdec_b * state + upd_b[i]

    gn_w = gn_ref[:, hs]
    for i in range(N_CHUNKS):
        rows = pl.ds(i * CHUNK, CHUNK)
        qch = qs_ref[hh, rows, :]
        scores = _dot_nt(qch, ks_ref[hh, rows, :].astype(BF16)) * decay
        o = jnp.dot(scores.astype(BF16), v_ref[rows, hs], preferred_element_type=F32)
        cross = jnp.dot(qch, st_ref[hh, i], preferred_element_type=F32)
        o = o + xi_f * cross[:, :RET_DV] + xi_b * cross[:, RET_DV:]
        y = _plain_norm(o) * gn_w
        o_ref[rows, hs] = (_silu(g_ref[rows, hs].astype(F32)) * y).astype(BF16)


def _retention(proj, proj_c, cos, sin, decay_rows, gn_w):
    width = RET_HG * LANES
    groups = RET_HEADS // RET_HG
    blk = lambda off: pl.BlockSpec((SEQ, width), lambda b, hg: (b, off // width + hg))
    blk_c = lambda off: pl.BlockSpec((CTX_LEN, width), lambda b, hg: (b, off // width + hg))
    table = pl.BlockSpec((SEQ, LANES), lambda b, hg: (0, 0))
    return pl.pallas_call(
        _retention_kernel,
        grid=(BATCH, groups),
        in_specs=[
            blk(OFF_Q), blk(OFF_K), blk(OFF_V), blk(OFF_G), blk_c(OFF_K), blk_c(OFF_V), table, table,
            pl.BlockSpec((RET_HG, 1, LANES), lambda b, hg: (hg, 0, 0)),
            pl.BlockSpec((RET_HG, 1, LANES), lambda b, hg: (groups + hg, 0, 0)),
            pl.BlockSpec((1, width), lambda b, hg: (0, hg)),
        ],
        out_specs=pl.BlockSpec((SEQ, width), lambda b, hg: (b, hg)),
        out_shape=jax.ShapeDtypeStruct((N_TOK, RET_W), BF16),
        scratch_shapes=[
            pltpu.VMEM((RET_HG, SEQ, RET_DK), BF16),
            pltpu.VMEM((RET_HG, SEQ, RET_DK), F32),
            pltpu.VMEM((RET_HG, N_CHUNKS, RET_DK, 2 * RET_DV), BF16),
        ],
        compiler_params=_params(("arbitrary", "arbitrary")),
        name="retention",
    )(proj, proj, proj, proj, proj_c, proj_c, cos, sin, decay_rows, decay_rows, gn_w)


MLA_TM = 512


def _rms_norm(x, w):
    return x * lax.rsqrt(jnp.mean(x * x, -1, keepdims=True) + EPS) * w


def _mla_kv(ckv_ref, kpe_ref, kvn_ref, wkv_ref, cos_ref, sin_ref, k_ref, v_ref, rotate):
    ckv = _rms_norm(ckv_ref[...].astype(F32), kvn_ref[...]).astype(BF16)
    kv = jnp.dot(ckv, wkv_ref[...], preferred_element_type=F32)
    lane = lax.broadcasted_iota(I32, (ckv.shape[0], LANES), 1)
    kpe = jnp.where(lane < MLA_ROPE, kpe_ref[...].astype(F32), 0.0)
    if rotate:
        kpe = _rope(kpe, cos_ref[...], sin_ref[...], MLA_ROPE // 4)
    kpe = kpe.astype(BF16)
    for h in range(MLA_HEADS):
        k_ref[:, h * QK_PAD:h * QK_PAD + MLA_NOPE] = kv[:, 2 * h * LANES:(2 * h + 1) * LANES].astype(BF16)
        k_ref[:, h * QK_PAD + MLA_NOPE:(h + 1) * QK_PAD] = kpe
        v_ref[:, h * MLA_DV:(h + 1) * MLA_DV] = kv[:, (2 * h + 1) * LANES:(2 * h + 2) * LANES].astype(BF16)


def _mla_latent_kernel(cq_ref, ckv_ref, kpe_ref, qn_ref, kvn_ref, wq_ref, wkv_ref, cos_ref, sin_ref,
                       q_ref, k_ref, v_ref):
    cq = _rms_norm(cq_ref[...].astype(F32), qn_ref[...]).astype(BF16)
    q = jnp.dot(cq, wq_ref[...], preferred_element_type=F32)
    cos = cos_ref[...]
    sin = sin_ref[...]
    scale = MLA_DQ ** -0.5 * LOG2_E
    for h in range(MLA_HEADS):
        lo = h * QK_PAD
        q_ref[:, lo:lo + MLA_NOPE] = (q[:, lo:lo + MLA_NOPE] * scale).astype(BF16)
        qpe = _rope(q[:, lo + MLA_NOPE:lo + QK_PAD], cos, sin, MLA_ROPE // 4)
        q_ref[:, lo + MLA_NOPE:lo + QK_PAD] = (qpe * scale).astype(BF16)
    _mla_kv(ckv_ref, kpe_ref, kvn_ref, wkv_ref, cos_ref, sin_ref, k_ref, v_ref, rotate=True)


def _mla_context_kernel(ckv_ref, kpe_ref, kvn_ref, wkv_ref, k_ref, v_ref):
    _mla_kv(ckv_ref, kpe_ref, kvn_ref, wkv_ref, None, None, k_ref, v_ref, rotate=False)


def _mla_latent(proj, q_norm, kv_norm, wq_pad, wkv, cos, sin):
    row = lambda w, off: pl.BlockSpec((MLA_TM, w), lambda i: (i, off // w))
    full = lambda a: pl.BlockSpec(a.shape, lambda i: (0, 0))
    table = pl.BlockSpec((MLA_TM, LANES), lambda i: (i % (SEQ // MLA_TM), 0))
    return pl.pallas_call(
        _mla_latent_kernel,
        grid=(N_TOK // MLA_TM,),
        in_specs=[row(MLA_Q_LORA, OFF_CQ), row(MLA_KV_LORA, OFF_CKV), row(LANES, OFF_KPE),
                  full(q_norm), full(kv_norm), full(wq_pad), full(wkv), table, table],
        out_specs=[
            pl.BlockSpec((MLA_TM, MLA_HEADS * QK_PAD), lambda i: (i, 0)),
            pl.BlockSpec((MLA_TM, MLA_HEADS * QK_PAD), lambda i: (i, 0)),
            pl.BlockSpec((MLA_TM, MLA_W), lambda i: (i, 0)),
        ],
        out_shape=[
            jax.ShapeDtypeStruct((N_TOK, MLA_HEADS * QK_PAD), BF16),
            jax.ShapeDtypeStruct((N_TOK, MLA_HEADS * QK_PAD), BF16),
            jax.ShapeDtypeStruct((N_TOK, MLA_W), BF16),
        ],
        compiler_params=_params(("arbitrary",)),
        name="mla_latent",
    )(proj, proj, proj, q_norm, kv_norm, wq_pad, wkv, cos, sin)


def _mla_context(proj_c, kv_norm, wkv):
    row = lambda w, off: pl.BlockSpec((MLA_TM, w), lambda i: (i, off // w))
    full = lambda a: pl.BlockSpec(a.shape, lambda i: (0, 0))
    return pl.pallas_call(
        _mla_context_kernel,
        grid=(N_CTX // MLA_TM,),
        in_specs=[row(MLA_KV_LORA, OFF_CKV), row(LANES, OFF_KPE), full(kv_norm), full(wkv)],
        out_specs=[
            pl.BlockSpec((MLA_TM, MLA_HEADS * QK_PAD), lambda i: (i, 0)),
            pl.BlockSpec((MLA_TM, MLA_W), lambda i: (i, 0)),
        ],
        out_shape=[
            jax.ShapeDtypeStruct((N_CTX, MLA_HEADS * QK_PAD), BF16),
            jax.ShapeDtypeStruct((N_CTX, MLA_W), BF16),
        ],
        compiler_params=_params(("arbitrary",)),
        name="mla_context",
    )(proj_c, proj_c, kv_norm, wkv)


ATT_TQ = 512


def _attention_kernel(q_ref, kc_ref, kl_ref, vc_ref, vl_ref, wo_ref, o_ref, wo_bf_ref, vext_ref):
    wo_bf_ref[...] = wo_ref[...].astype(BF16)

    @pl.when(pl.program_id(1) == 0)
    def _():
        lane = lax.broadcasted_iota(I32, (CTX_LEN + SEQ, MLA_DV), 1)
        ones_col = jnp.where(lane == 0, 1.0, 0.0).astype(BF16)
        for h in range(MLA_HEADS):
            dv = slice(h * MLA_DV, (h + 1) * MLA_DV)
            vext_ref[h, :CTX_LEN, :MLA_DV] = vc_ref[:, dv]
            vext_ref[h, CTX_LEN:, :MLA_DV] = vl_ref[:, dv]
            vext_ref[h, :, MLA_DV:] = ones_col

    for h in range(MLA_HEADS):
        qk = slice(h * QK_PAD, (h + 1) * QK_PAD)
        q = q_ref[:, qk]
        s_c = _dot_nt(q, kc_ref[:, qk])
        s_l = _dot_nt(q, kl_ref[:, qk])
        m = jnp.maximum(jnp.max(s_c, -1, keepdims=True), jnp.max(s_l, -1, keepdims=True))
        p_c = jnp.exp2(s_c - m).astype(BF16)
        p_l = jnp.exp2(s_l - m).astype(BF16)
        o = (jnp.dot(p_c, vext_ref[h, :CTX_LEN, :], preferred_element_type=F32)
             + jnp.dot(p_l, vext_ref[h, CTX_LEN:, :], preferred_element_type=F32))
        o_ref[:, h * MLA_DV:(h + 1) * MLA_DV] = (o[:, :MLA_DV] / o[:, MLA_DV:MLA_DV + 1]).astype(BF16)


def _attention(q, k_ctx, k_lat, v_ctx, v_lat, w_o):
    tiles = SEQ // ATT_TQ
    band = w_o.shape[0] // (BATCH * tiles)
    step = lambda b, i: (b * tiles + i, 0)
    return pl.pallas_call(
        _attention_kernel,
        grid=(BATCH, tiles),
        in_specs=[
            pl.BlockSpec((ATT_TQ, MLA_HEADS * QK_PAD), step),
            pl.BlockSpec((CTX_LEN, MLA_HEADS * QK_PAD), lambda b, i: (b, 0)),
            pl.BlockSpec((SEQ, MLA_HEADS * QK_PAD), lambda b, i: (b, 0)),
            pl.BlockSpec((CTX_LEN, MLA_W), lambda b, i: (b, 0)),
            pl.BlockSpec((SEQ, MLA_W), lambda b, i: (b, 0)),
            pl.BlockSpec((band, D_MODEL), step),
        ],
        out_specs=[pl.BlockSpec((ATT_TQ, MLA_W), step), pl.BlockSpec((band, D_MODEL), step)],
        out_shape=[jax.ShapeDtypeStruct((N_TOK, MLA_W), BF16), jax.ShapeDtypeStruct(w_o.shape, BF16)],
        scratch_shapes=[pltpu.VMEM((MLA_HEADS, CTX_LEN + SEQ, 2 * MLA_DV), BF16)],
        compiler_params=_params(("arbitrary", "arbitrary")),
        name="attention",
    )(q, k_ctx, k_lat, v_ctx, v_lat, w_o)


OUT_TM = 512
OUT_SUB = 256
ROUTE_E1, ROUTE_E2, ROUTE_R1, ROUTE_R2, ROUTE_W1, ROUTE_W2 = range(6)


def _outproj_kernel(ret_ref, att_ref, x_ref, wo_ref, gate_ref, shift_ref, scale_ref, lnw_ref, lnb_ref,
                    wr_ref, br_ref, h_ref, t_ref, route_ref, route_t_ref, count_ref, carry_ref, wsplit_ref):
    @pl.when(pl.program_id(0) == 0)
    def _():
        carry_ref[...] = jnp.zeros_like(carry_ref)
        w = wr_ref[...]
        w_hi = w.astype(BF16)
        wsplit_ref[:, :LANES] = w_hi
        wsplit_ref[:, LANES:] = (w - w_hi.astype(F32)).astype(BF16)

    subtiles = [slice(s * OUT_SUB, (s + 1) * OUT_SUB) for s in range(OUT_TM // OUT_SUB)]
    mixes = [jnp.dot(ret_ref[rows, :], wo_ref[:RET_W, :], preferred_element_type=F32)
             + jnp.dot(att_ref[rows, :], wo_ref[RET_W:, :], preferred_element_type=F32) for rows in subtiles]
    logits = [_outproj_norms(rows, mix, x_ref, gate_ref, shift_ref, scale_ref, lnw_ref, lnb_ref, br_ref, h_ref, t_ref,
                             wsplit_ref) for rows, mix in zip(subtiles, mixes)]
    _route(jnp.concatenate(logits, 0), carry_ref, count_ref, route_ref, route_t_ref)


def _outproj_norms(rows, mix, x_ref, gate_ref, shift_ref, scale_ref, lnw_ref, lnb_ref, br_ref, h_ref, t_ref,
                   wsplit_ref):
    h = _plain_norm(DEEPNORM_ALPHA * x_ref[rows, :] + gate_ref[0] * mix) * lnw_ref[...] + lnb_ref[...]
    h_ref[rows, :] = h
    t = _plain_norm(h) * (1.0 + scale_ref[0]) + shift_ref[0]
    t_ref[rows] = _to_token_tiles(t)

    t_hi = t.astype(BF16)
    t_lo = (t - t_hi.astype(F32)).astype(BF16)
    main = jnp.dot(t_hi, wsplit_ref[...], preferred_element_type=F32)
    corr = jnp.dot(t_lo, wsplit_ref[:, :LANES], preferred_element_type=F32)
    return main[:, :LANES] + (main[:, LANES:] + corr) + br_ref[...]


def _route(logits, carry_ref, count_ref, route_ref, route_t_ref):
    lane = lax.broadcasted_iota(I32, logits.shape, 1).astype(F32)
    neg = -jnp.inf
    big = float(LANES)

    def first_lane_of(mask):
        return jnp.min(jnp.where(mask, lane, big), -1, keepdims=True)

    is_group = lane < N_GROUPS
    gl = jnp.where(is_group, logits, neg)
    g_max = jnp.max(gl, -1, keepdims=True)
    g_idx = first_lane_of(is_group & (gl == g_max))
    g_prob = 1.0 / jnp.sum(jnp.where(is_group, jnp.exp(logits - g_max), 0.0), -1, keepdims=True)

    lo = N_GROUPS + g_idx * EXPERTS_PER_GROUP
    in_group = (lane >= lo) & (lane < lo + EXPERTS_PER_GROUP)
    el = jnp.where(in_group, logits, neg)
    v1 = jnp.max(el, -1, keepdims=True)
    i1 = first_lane_of(in_group & (el == v1))
    rest = in_group & (lane != i1)
    el2 = jnp.where(rest, logits, neg)
    v2 = jnp.max(el2, -1, keepdims=True)
    i2 = first_lane_of(rest & (el2 == v2))
    d = jnp.exp(v2 - v1)
    w1 = g_prob / (1.0 + d)
    w2 = g_prob * d / (1.0 + d)

    onehot = jnp.where((lane == i1) | (lane == i2), 1.0, 0.0)
    r = lax.broadcasted_iota(I32, (OUT_TM, OUT_TM), 0)
    c = lax.broadcasted_iota(I32, (OUT_TM, OUT_TM), 1)
    tri = jnp.where(c < r, 1.0, 0.0).astype(BF16)
    carry = carry_ref[0:1, :]
    before = jnp.dot(tri, onehot.astype(BF16), preferred_element_type=F32) + carry
    r1 = jnp.sum(jnp.where(lane == i1, before, 0.0), -1, keepdims=True)
    r2 = jnp.sum(jnp.where(lane == i2, before, 0.0), -1, keepdims=True)
    carry = carry + jnp.sum(onehot, 0, keepdims=True)
    carry_ref[...] = jnp.broadcast_to(carry, carry_ref.shape)
    count_ref[...] = jnp.broadcast_to(carry, count_ref.shape)

    rec = jnp.zeros_like(logits)
    for slot, val in ((ROUTE_E1, i1 - N_GROUPS), (ROUTE_E2, i2 - N_GROUPS), (ROUTE_R1, r1), (ROUTE_R2, r2),
                      (ROUTE_W1, w1), (ROUTE_W2, w2)):
        rec = jnp.where(lane == slot, val, rec)
    route_ref[...] = rec
    route_t_ref[...] = rec.T[:SUBLANES, :]


def _outproj(ret, att, x2d, wo_bf, mod_rows, ln_w, ln_b, w_route, b_route):
    tiles_per_sample = SEQ // OUT_TM
    row = lambda w: pl.BlockSpec((OUT_TM, w), lambda i: (i, 0))
    full = lambda a: pl.BlockSpec(a.shape, lambda i: (0, 0))
    mod = lambda j: pl.BlockSpec((1, 1, D_MODEL), lambda i: ((i // tiles_per_sample) * N_MOD + j, 0, 0))
    return pl.pallas_call(
        _outproj_kernel,
        grid=(N_TOK // OUT_TM,),
        in_specs=[row(RET_W), row(MLA_W), row(D_MODEL), full(wo_bf), mod(MOD_GATE1), mod(MOD_SHIFT2), mod(MOD_SCALE2),
                  full(ln_w), full(ln_b), full(w_route), full(b_route)],
        out_specs=[row(D_MODEL), pl.BlockSpec((OUT_TM, TOKEN_SUB, LANES), lambda i: (i, 0, 0)), row(LANES),
                   pl.BlockSpec((SUBLANES, OUT_TM), lambda i: (0, i)),
                   pl.BlockSpec((SUBLANES, LANES), lambda i: (0, 0))],
        out_shape=[
            jax.ShapeDtypeStruct((N_TOK, D_MODEL), F32),
            jax.ShapeDtypeStruct((N_TOK, TOKEN_SUB, LANES), BF16),
            jax.ShapeDtypeStruct((N_TOK, LANES), F32),
            jax.ShapeDtypeStruct((SUBLANES, N_TOK), F32),
            jax.ShapeDtypeStruct((SUBLANES, LANES), F32),
        ],
        scratch_shapes=[pltpu.VMEM((SUBLANES, LANES), F32), pltpu.VMEM((D_MODEL, 2 * LANES), BF16)],
        compiler_params=_params(("arbitrary",)),
        name="outproj_route",
    )(ret, att, x2d, wo_bf, mod_rows, mod_rows, mod_rows, ln_w, ln_b, w_route, b_route)


DISPATCH_TM = 1024


def _dispatch_kernel(pos_ref, t_ref, xs_ref, sem):
    i = pl.program_id(0)

    def copy(slot, j):
        return pltpu.make_async_copy(t_ref.at[j], xs_ref.at[pos_ref[slot * N_TOK + i * DISPATCH_TM + j]], sem)

    for j in range(DISPATCH_TM):
        for slot in range(2):
            copy(slot, j).start(priority=slot)
    for slot in range(2):
        pltpu.make_async_copy(t_ref, xs_ref.at[pl.ds(0, DISPATCH_TM)], sem).wait()


def _dispatch(pos, t_tiles):
    return pl.pallas_call(
        _dispatch_kernel,
        grid_spec=pltpu.PrefetchScalarGridSpec(
            num_scalar_prefetch=1,
            grid=(N_TOK // DISPATCH_TM,),
            in_specs=[pl.BlockSpec((DISPATCH_TM, TOKEN_SUB, LANES), lambda i, pos: (i, 0, 0))],
            out_specs=pl.BlockSpec(memory_space=pl.ANY),
            scratch_shapes=[pltpu.SemaphoreType.DMA(())],
        ),
        out_shape=jax.ShapeDtypeStruct((N_PAIRS, TOKEN_SUB, LANES), BF16),
        compiler_params=_params(("arbitrary",)),
        name="dispatch",
    )(pos, t_tiles)


def _experts_kernel(rend_ref, xs_ref, wg_ref, wu_ref, wd_ref, ys_ref, wgu_ref, wdn_ref, xbuf_ref, ybuf_ref,
                    xsem, ysem):
    e = pl.program_id(0)
    lo = jnp.where(e == 0, 0, rend_ref[jnp.maximum(e - 1, 0)])
    hi = rend_ref[e]

    def x_copy(g):
        return pltpu.make_async_copy(xs_ref.at[pl.ds(g * ROW_TILE, ROW_TILE)], xbuf_ref.at[g % 2], xsem.at[g % 2])

    def y_copy(g):
        return pltpu.make_async_copy(ybuf_ref.at[g % 2], ys_ref.at[pl.ds(g * ROW_TILE, ROW_TILE)], ysem.at[g % 2])

    @pl.when(e == 0)
    def _():
        x_copy(0).start()

    @pl.when(hi > lo)
    def _():
        wgu_ref[:, :D_EXPERT] = wg_ref[0].astype(BF16)
        wgu_ref[:, D_EXPERT:] = wu_ref[0].astype(BF16)
        wdn_ref[...] = wd_ref[0].astype(BF16)

        def tile(g, c):
            row0 = g * ROW_TILE
            owns_first_row = lo <= row0
            owns_last_row = hi >= row0 + ROW_TILE

            @pl.when(owns_first_row)
            def _():
                @pl.when(g + 1 < N_TILES)
                def _():
                    x_copy(g + 1).start()

                x_copy(g).wait()

            gu = jnp.dot(_from_token_tiles(xbuf_ref[g % 2]), wgu_ref[...], preferred_element_type=F32)
            hid = _silu(gu[:, :D_EXPERT]) * gu[:, D_EXPERT:]
            y = jnp.dot(hid.astype(BF16), wdn_ref[...], preferred_element_type=F32)

            @pl.when(owns_first_row)
            def _():
                @pl.when(g >= 2)
                def _():
                    y_copy(g - 2).wait()

                ybuf_ref[g % 2] = _to_token_tiles(y)

            @pl.when(jnp.logical_not(owns_first_row))
            def _():
                row = lax.broadcasted_iota(I32, y.shape, 0)
                earlier = _from_token_tiles(ybuf_ref[g % 2]).astype(F32)
                ybuf_ref[g % 2] = _to_token_tiles(jnp.where(row >= lo - row0, y, earlier))

            @pl.when(owns_last_row)
            def _():
                y_copy(g).start()

            return c

        lax.fori_loop(lo // ROW_TILE, (hi - 1) // ROW_TILE + 1, tile, 0)

    @pl.when(e == N_EXPERTS - 1)
    def _():
        y_copy(N_TILES - 2).wait()
        y_copy(N_TILES - 1).wait()


def _experts(row_end, xs, w_gate, w_up, w_down):
    w_gate = w_gate.reshape(N_EXPERTS, D_MODEL, D_EXPERT)
    w_up = w_up.reshape(N_EXPERTS, D_MODEL, D_EXPERT)
    w_down = w_down.reshape(N_EXPERTS, D_EXPERT, D_MODEL)
    expert = lambda e, rend: (e, 0, 0)
    tile_buf = pltpu.VMEM((2, ROW_TILE, TOKEN_SUB, LANES), BF16)
    return pl.pallas_call(
        _experts_kernel,
        grid_spec=pltpu.PrefetchScalarGridSpec(
            num_scalar_prefetch=1,
            grid=(N_EXPERTS,),
            in_specs=[
                pl.BlockSpec(memory_space=pl.ANY),
                pl.BlockSpec((1, D_MODEL, D_EXPERT), expert),
                pl.BlockSpec((1, D_MODEL, D_EXPERT), expert),
                pl.BlockSpec((1, D_EXPERT, D_MODEL), expert),
            ],
            out_specs=pl.BlockSpec(memory_space=pl.ANY),
            scratch_shapes=[pltpu.VMEM((D_MODEL, 2 * D_EXPERT), BF16), pltpu.VMEM((D_EXPERT, D_MODEL), BF16),
                            tile_buf, tile_buf, pltpu.SemaphoreType.DMA((2,)), pltpu.SemaphoreType.DMA((2,))],
        ),
        out_shape=jax.ShapeDtypeStruct((N_PAIRS, TOKEN_SUB, LANES), BF16),
        compiler_params=_params(("arbitrary",)),
        name="experts",
    )(row_end, xs, w_gate, w_up, w_down)


COMB_TM = 256


def _combine_kernel(pos_ref, ys_ref, h_ref, route_ref, gate_ref, lnw_ref, lnb_ref, o_ref, buf_ref, sem):
    i = pl.program_id(0)
    n = pl.num_programs(0)

    def copy(step, slot, pair, j):
        src = pos_ref[pair * N_TOK + step * COMB_TM + j]
        return pltpu.make_async_copy(ys_ref.at[src], buf_ref.at[slot, pair * COMB_TM + j], sem.at[slot])

    def start_all(step, slot):
        for pair in range(2):
            lax.fori_loop(0, COMB_TM, lambda j, c, pair=pair: (copy(step, slot, pair, j).start(priority=pair), c)[1], 0,
                          unroll=8)

    def wait_all(step, slot):
        pltpu.make_async_copy(ys_ref.at[pl.ds(0, 2 * COMB_TM)], buf_ref.at[slot], sem.at[slot]).wait()

    @pl.when(i == 0)
    def _():
        start_all(0, 0)

    slot = i % 2
    wait_all(i, slot)

    nxt = jnp.minimum(i + 1, n - 1)
    for j in range(COMB_TM):
        for pair in range(2):
            copy(nxt, 1 - slot, pair, j).start(priority=pair)

    y1 = _from_token_tiles(buf_ref[slot, :COMB_TM]).astype(F32)
    y2 = _from_token_tiles(buf_ref[slot, COMB_TM:]).astype(F32)
    y = route_ref[:, ROUTE_W1:ROUTE_W1 + 1] * y1 + route_ref[:, ROUTE_W2:ROUTE_W2 + 1] * y2
    z = DEEPNORM_ALPHA * h_ref[...] + gate_ref[0] * y
    o_ref[...] = _plain_norm(z) * lnw_ref[...] + lnb_ref[...]

    @pl.when(i == n - 1)
    def _():
        wait_all(i, 1 - slot)


def _combine(pos, ys, h, route, mod_rows, ln_w, ln_b):
    tiles_per_sample = SEQ // COMB_TM
    row = lambda w: pl.BlockSpec((COMB_TM, w), lambda i, pos: (i, 0))
    full = lambda a: pl.BlockSpec(a.shape, lambda i, pos: (0, 0))
    return pl.pallas_call(
        _combine_kernel,
        grid_spec=pltpu.PrefetchScalarGridSpec(
            num_scalar_prefetch=1,
            grid=(N_TOK // COMB_TM,),
            in_specs=[
                pl.BlockSpec(memory_space=pl.ANY), row(D_MODEL), row(LANES),
                pl.BlockSpec((1, 1, D_MODEL), lambda i, pos: ((i // tiles_per_sample) * N_MOD + MOD_GATE2, 0, 0)),
                full(ln_w), full(ln_b),
            ],
            out_specs=row(D_MODEL),
            scratch_shapes=[pltpu.VMEM((2, 2 * COMB_TM, TOKEN_SUB, LANES), BF16), pltpu.SemaphoreType.DMA((2,))],
        ),
        out_shape=jax.ShapeDtypeStruct((N_TOK, D_MODEL), F32),
        compiler_params=_params(("arbitrary",)),
        name="combine",
    )(pos, ys, h, route, mod_rows, ln_w, ln_b)


def _routing_tables(route_t, counts):
    cnt = counts[0, N_GROUPS:N_ROUTE].astype(I32)
    row_end = jnp.cumsum(cnt)
    e = route_t[ROUTE_E1:ROUTE_E2 + 1].astype(I32)
    rank = route_t[ROUTE_R1:ROUTE_R2 + 1].astype(I32)
    earlier = jnp.arange(N_EXPERTS, dtype=I32)[:, None, None] < e[None]
    base = jnp.sum(jnp.where(earlier, cnt[:, None, None], 0), 0)
    pos = (base + rank).reshape(-1)
    return pos, row_end


def kernel(x, c, ctx, c_ctx, w_ada, b_ada, w_in, ret_decay, ret_gn_w, mla_q_norm, mla_kv_norm, w_uq, w_ukv, w_o,
           ln1_w, ln1_b, router_group_w, router_group_b, router_expert_w, router_expert_b, expert_w_gate,
           expert_w_up, expert_w_down, ln2_w, ln2_b):
    x2d = x.reshape(N_TOK, D_MODEL)
    ctx2d = ctx.reshape(N_CTX, D_MODEL)

    cc = jnp.zeros((SUBLANES, D_MODEL), F32).at[:BATCH].set(c).at[BATCH].set(c_ctx)
    mod = _ada(cc, w_ada[0], b_ada)
    mod_rows = mod.reshape(SUBLANES * N_MOD, 1, D_MODEL)

    proj, w_in_blocks = _inproj_latent(x2d, mod_rows, jnp.swapaxes(w_in, 1, 2))
    proj_c = _inproj_context(ctx2d, mod_rows, w_in_blocks)

    cos_r, sin_r = _rope_tables(RET_DK)
    decay_rows = jnp.broadcast_to(ret_decay[0].reshape(2 * RET_HEADS, 1, 1), (2 * RET_HEADS, 1, LANES))
    ret = _retention(proj, proj_c, jnp.asarray(cos_r), jnp.asarray(sin_r), decay_rows, ret_gn_w)

    cos_m, sin_m = _rope_tables(MLA_ROPE)
    cos_m = np.concatenate([cos_m, np.ones_like(cos_m)], 1)
    sin_m = np.concatenate([sin_m, np.zeros_like(sin_m)], 1)
    wq = w_uq[0].reshape(MLA_Q_LORA, MLA_HEADS, MLA_DQ)
    wq_pad = jnp.pad(wq, ((0, 0), (0, 0), (0, QK_PAD - MLA_DQ))).reshape(MLA_Q_LORA, MLA_HEADS * QK_PAD).astype(BF16)
    wkv = w_ukv[0].astype(BF16)
    q, k_lat, v_lat = _mla_latent(proj, mla_q_norm, mla_kv_norm, wq_pad, wkv, jnp.asarray(cos_m), jnp.asarray(sin_m))
    k_ctx, v_ctx = _mla_context(proj_c, mla_kv_norm, wkv)
    att, w_o_bf = _attention(q, k_ctx, k_lat, v_ctx, v_lat, w_o[0])

    w_route = jnp.concatenate(
        [router_group_w[0], router_expert_w[0].transpose(1, 0, 2).reshape(D_MODEL, N_EXPERTS),
         jnp.zeros((D_MODEL, LANES - N_ROUTE), F32)], 1)
    b_route = jnp.concatenate(
        [router_group_b[0], router_expert_b[0].reshape(N_EXPERTS), jnp.zeros((LANES - N_ROUTE,), F32)])[None]
    h, t_tiles, route, route_t, counts = _outproj(ret, att, x2d, w_o_bf, mod_rows, ln1_w, ln1_b,
                                                   w_route, b_route)

    pos, row_end = _routing_tables(route_t, counts)
    xs = _dispatch(pos, t_tiles)
    ys = _experts(row_end, xs, expert_w_gate[0], expert_w_up[0], expert_w_down[0])
    out = _combine(pos, ys, h, route, mod_rows, ln2_w, ln2_b)
    return out.reshape(BATCH, SEQ, D_MODEL)
```
